```python
import math
import jax
import jax.numpy as jnp
from jax import lax
import numpy as np

D_MODEL = 1024
BATCH = 1
SEQ = 16384
DEPTH = 1
DEC_BATCH = 128
DEC_SEQ = 8
PAST_LEN = 16384
PAGE_SIZE = 128

HEAD_DIM = 64
MIX_WIDTH = D_MODEL
H_A = (MIX_WIDTH // 2) // HEAD_DIM
H_B = (MIX_WIDTH // 2) // HEAD_DIM
KV_B = max(1, H_B // 4)
G_B = H_B // KV_B
DILATED_PATTERNS = ((128, 1), (512, 4), (2048, 16))
WIN_A = max(w for w, _ in DILATED_PATTERNS)
WIN_B = 128
Q_BLOCK = 128
NUM_BUCKETS = 32
MAX_DISTANCE = WIN_A
N_GROUPS = 4
EXPERTS_PER_GROUP = 8
N_EXPERTS = N_GROUPS * EXPERTS_PER_GROUP
TOP_K_INNER = 2
D_EXPERT = D_MODEL // 2
MOE_BLOCK = 128
EPS = 1e-5
SPLIT_SIZES = (H_A * HEAD_DIM, H_A * HEAD_DIM, H_A * HEAD_DIM,
               H_B * HEAD_DIM, KV_B * HEAD_DIM, KV_B * HEAD_DIM)
PROJ_COLS = sum(SPLIT_SIZES)

kernel_name = 'hybrid_dilated_sinkswa_hmoe_step'


def rms_norm(x, g):
    xf = x.astype(jnp.float32)
    y = xf * lax.rsqrt(jnp.mean(xf * xf, axis=-1, keepdims=True) + EPS)
    return (y * g.astype(jnp.float32)).astype(x.dtype)


def t5_bucket(dist):
    max_exact = NUM_BUCKETS // 2
    d = jnp.maximum(dist, 1).astype(jnp.float32)
    large = max_exact + (jnp.log(d / max_exact) / math.log(MAX_DISTANCE / max_exact)
                         * (NUM_BUCKETS - max_exact)).astype(jnp.int32)
    large = jnp.minimum(large, NUM_BUCKETS - 1)
    return jnp.where(dist < max_exact, dist, large)


def split_projection(proj):
    n, t = proj.shape[:2]
    cuts = [int(c) for c in np.cumsum(SPLIT_SIZES)[:-1]]
    qa, ka, va, qb, kb, vb = jnp.split(proj, cuts, axis=-1)
    scale = HEAD_DIM ** -0.5
    qa = (qa * scale).reshape(n, t, H_A, HEAD_DIM)
    ka = ka.reshape(n, t, H_A, HEAD_DIM)
    va = va.reshape(n, t, H_A, HEAD_DIM)
    qb = (qb * scale).reshape(n, t, KV_B, G_B, HEAD_DIM)
    kb = kb.reshape(n, t, KV_B, HEAD_DIM)
    vb = vb.reshape(n, t, KV_B, HEAD_DIM)
    return qa, ka, va, qb, kb, vb


def dilated_group(q, kx, vx, qpos, off, window, dil, table_a):
    n_keys = window // dil + 1
    dist = dil * jnp.arange(n_keys, dtype=jnp.int32)
    kpos = qpos[:, None] - dist[None, :]
    idx = jnp.clip(kpos - off, 0, kx.shape[1] - 1)
    kg = kx[:, idx]
    vg = vx[:, idx]
    s = jnp.einsum('nthd,ntkhd->nhtk', q, kg).astype(jnp.float32)
    s = s + table_a[t5_bucket(dist)].astype(jnp.float32).T[None, :, None, :]
    s = jnp.where((kpos >= 0)[None, None], s, -jnp.inf)
    m = jnp.max(s, axis=-1)
    p = jnp.exp(s - m[..., None])
    l = jnp.sum(p, axis=-1)
    o = jnp.einsum('nhtk,ntkhd->nthd', (p / l[..., None]).astype(vx.dtype), vg)
    return m, l, o


def dilated_attention(q, kx, vx, qpos, off, table_a):
    parts = [dilated_group(q, kx, vx, qpos, off, w, d, table_a) for w, d in DILATED_PATTERNS]
    m_max = parts[0][0]
    for m, _, _ in parts[1:]:
        m_max = jnp.maximum(m_max, m)
    dens = [jnp.exp(m - m_max) * l for m, l, _ in parts]
    total = dens[0]
    for den in dens[1:]:
        total = total + den
    out = jnp.zeros(q.shape, jnp.float32)
    for den, (_, _, o) in zip(dens, parts):
        alpha = jnp.transpose(den / total, (0, 2, 1))[..., None]
        out = out + alpha * o.astype(jnp.float32)
    return out.astype(q.dtype)


def mixer_a_prompt(qa, ka, va, table_a):
    n, s = qa.shape[:2]
    pad = jnp.zeros((n, WIN_A, H_A, HEAD_DIM), ka.dtype)
    kx = jnp.concatenate([pad, ka], axis=1)
    vx = jnp.concatenate([pad, va], axis=1)
    nb = s // Q_BLOCK
    qb = jnp.swapaxes(qa.reshape(n, nb, Q_BLOCK, H_A, HEAD_DIM), 0, 1)

    def block(args):
        b, q_blk = args
        qpos = b * Q_BLOCK + jnp.arange(Q_BLOCK, dtype=jnp.int32)
        return dilated_attention(q_blk, kx, vx, qpos, -WIN_A, table_a)

    out = lax.map(block, (jnp.arange(nb, dtype=jnp.int32), qb))
    return jnp.swapaxes(out, 0, 1).reshape(n, s, H_A * HEAD_DIM)


def mixer_a_sample(qa, ka, va, cache_k, cache_v, table_a):
    n, t = qa.shape[:2]
    l_a = cache_k.shape[1]
    kx = jnp.concatenate([cache_k, ka], axis=1)
    vx = jnp.concatenate([cache_v, va], axis=1)
    qpos = PAST_LEN + jnp.arange(t, dtype=jnp.int32)
    out = dilated_attention(qa, kx, vx, qpos, PAST_LEN - l_a, table_a)
    return out.reshape(n, t, H_A * HEAD_DIM)


def sink_softmax_attend(q, k, v, valid, bias, sinks):
    s = jnp.einsum('...qkgd,...skd->...kgqs', q, k).astype(jnp.float32) + bias
    s = jnp.where(valid, s, -jnp.inf)
    sink = sinks.astype(jnp.float32)[..., None, None]
    m = jnp.maximum(jnp.max(s, axis=-1, keepdims=True), sink)
    p = jnp.exp(s - m)
    den = jnp.sum(p, axis=-1, keepdims=True) + jnp.exp(sink - m)
    return jnp.einsum('...kgqs,...skd->...qkgd', (p / den).astype(v.dtype), v)


def b_bias(table_b, dist):
    b = table_b[t5_bucket(jnp.maximum(dist, 0))].astype(jnp.float32)
    return b.reshape(dist.shape + (KV_B, G_B)).transpose(2, 3, 0, 1)


def mixer_b_prompt(qb, kb, vb, table_b, sinks):
    n, s = qb.shape[:2]
    w = WIN_B
    nb = s // w
    q_blk = qb.reshape(n, nb, w, KV_B, G_B, HEAD_DIM)
    k_blk = kb.reshape(n, nb, w, KV_B, HEAD_DIM)
    v_blk = vb.reshape(n, nb, w, KV_B, HEAD_DIM)
    k_band = jnp.concatenate([jnp.concatenate([jnp.zeros_like(k_blk[:, :1]), k_blk[:, :-1]], axis=1), k_blk], axis=2)
    v_band = jnp.concatenate([jnp.concatenate([jnp.zeros_like(v_blk[:, :1]), v_blk[:, :-1]], axis=1), v_blk], axis=2)
    dist = jnp.arange(w)[:, None] + w - jnp.arange(2 * w)[None, :]
    band = (dist >= 0) & (dist < w)
    key_exists = (jnp.arange(nb)[:, None] > 0) | (jnp.arange(2 * w)[None, :] >= w)
    valid = (band[None] & key_exists[:, None, :])[None, :, None, None]
    o = sink_softmax_attend(q_blk, k_band, v_band, valid, b_bias(table_b, dist), sinks)
    return o.reshape(n, s, H_B * HEAD_DIM)


def mixer_b_sample(qb, kb, vb, cache_k, cache_v, table_b, sinks):
    n, t = qb.shape[:2]
    l_b = cache_k.shape[1]
    kk = jnp.concatenate([cache_k, kb], axis=1)
    vv = jnp.concatenate([cache_v, vb], axis=1)
    kpos = PAST_LEN - l_b + jnp.arange(l_b + t)
    qpos = PAST_LEN + jnp.arange(t)
    dist = qpos[:, None] - kpos[None, :]
    valid = (dist >= 0) & (dist < WIN_B)
    o = sink_softmax_attend(qb, kk, vv, valid, b_bias(table_b, dist), sinks)
    return o.reshape(n, t, H_B * HEAD_DIM)


def grouped_experts(x2, eid, gate, w_gate, w_up, w_down):
    t, d = x2.shape
    a = t * TOP_K_INNER
    flat_e = eid.reshape(-1)
    order = jnp.argsort(flat_e)
    se = flat_e[order]
    tok = (order // TOP_K_INNER).astype(jnp.int32)
    counts = jnp.bincount(flat_e, length=N_EXPERTS)
    padded = (counts + MOE_BLOCK - 1) // MOE_BLOCK * MOE_BLOCK
    pend = jnp.cumsum(padded)
    pstart = pend - padded
    start = jnp.cumsum(counts) - counts
    dest = pstart[se] + jnp.arange(a) - start[se]
    n_blocks = -(-a // MOE_BLOCK) + N_EXPERTS
    rows = n_blocks * MOE_BLOCK
    row_tok = jnp.zeros((rows,), jnp.int32).at[dest].set(tok)
    blk_e = jnp.minimum(jnp.searchsorted(pend, jnp.arange(n_blocks) * MOE_BLOCK, side='right'), N_EXPERTS - 1)
    xb = x2[row_tok].reshape(n_blocks, MOE_BLOCK, d)

    def expert_block(args):
        x_blk, e = args
        return (jax.nn.silu(x_blk @ w_gate[e]) * (x_blk @ w_up[e])) @ w_down[e]

    yb = lax.map(expert_block, (xb, blk_e)).reshape(rows, d)
    contrib = yb[dest] * gate.reshape(-1)[order][:, None]
    return jnp.zeros_like(x2).at[tok].add(contrib)


def hier_moe(x, w_rg, b_rg, w_re, b_re, w_gate, w_up, w_down):
    n, t, d = x.shape
    x2 = x.reshape(-1, d)
    lg = (x2 @ w_rg).astype(jnp.float32) + b_rg.astype(jnp.float32)
    pg = jax.nn.softmax(lg, axis=-1)
    grp = jnp.argmax(lg, axis=-1)
    pg_top = jnp.max(pg, axis=-1)
    le = jnp.einsum('td,gde->tge', x2, w_re).astype(jnp.float32) + b_re.astype(jnp.float32)
    le_sel = le[jnp.arange(x2.shape[0]), grp]
    pe = jax.nn.softmax(le_sel, axis=-1)
    top_p, top_i = lax.top_k(pe, TOP_K_INNER)
    gate = pg_top[:, None] * top_p / jnp.sum(top_p, axis=-1, keepdims=True)
    eid = grp[:, None].astype(jnp.int32) * EXPERTS_PER_GROUP + top_i.astype(jnp.int32)
    y = grouped_experts(x2, eid, gate.astype(x.dtype), w_gate, w_up, w_down)
    return y.reshape(n, t, d)


def setup_inputs(seed: int = 0) -> dict:
    key = jax.random.key(seed)
    ks = jax.random.split(key, 24)
    f32 = jnp.float32
    l_a = min(WIN_A, PAST_LEN)
    l_b = min(WIN_B, PAST_LEN)

    def nrm(k, shape, scale):
        return scale * jax.random.normal(k, shape, f32)

    return {
        'x_prompt': nrm(ks[0], (BATCH, SEQ, D_MODEL), 1.0),
        'x_sample': nrm(ks[1], (DEC_BATCH, DEC_SEQ, D_MODEL), 1.0),
        'cache_a_k': nrm(ks[2], (DEPTH, DEC_BATCH, l_a, H_A, HEAD_DIM), 1.0),
        'cache_a_v': nrm(ks[3], (DEPTH, DEC_BATCH, l_a, H_A, HEAD_DIM), 1.0),
        'cache_b_k': nrm(ks[4], (DEPTH, DEC_BATCH, l_b, KV_B, HEAD_DIM), 1.0),
        'cache_b_v': nrm(ks[5], (DEPTH, DEC_BATCH, l_b, KV_B, HEAD_DIM), 1.0),
        'rel_bias_table': nrm(ks[6], (NUM_BUCKETS, H_A + H_B), 0.5),
        'attn_norm': 1.0 + nrm(ks[7], (DEPTH, D_MODEL), 0.02),
        'w_in': nrm(ks[8], (DEPTH, D_MODEL, PROJ_COLS), D_MODEL ** -0.5),
        'w_out': nrm(ks[9], (DEPTH, MIX_WIDTH, D_MODEL), MIX_WIDTH ** -0.5),
        'attn_sinks': nrm(ks[10], (DEPTH, H_B), 1.0),
        'ffn_norm': 1.0 + nrm(ks[11], (DEPTH, D_MODEL), 0.02),
        'w_router_group': nrm(ks[12], (DEPTH, D_MODEL, N_GROUPS), D_MODEL ** -0.5),
        'b_router_group': nrm(ks[13], (DEPTH, N_GROUPS), 0.01),
        'w_router_expert': nrm(ks[14], (DEPTH, N_GROUPS, D_MODEL, EXPERTS_PER_GROUP), D_MODEL ** -0.5),
        'b_router_expert': nrm(ks[15], (DEPTH, N_GROUPS, EXPERTS_PER_GROUP), 0.01),
        'w_gate': nrm(ks[16], (DEPTH, N_EXPERTS, D_MODEL, D_EXPERT), D_MODEL ** -0.5),
        'w_up': nrm(ks[17], (DEPTH, N_EXPERTS, D_MODEL, D_EXPERT), D_MODEL ** -0.5),
        'w_down': nrm(ks[18], (DEPTH, N_EXPERTS, D_EXPERT, D_MODEL), D_EXPERT ** -0.5),
        'final_norm': 1.0 + nrm(ks[19], (D_MODEL,), 0.02),
    }


def reference(x_prompt, x_sample, cache_a_k, cache_a_v, cache_b_k, cache_b_v, rel_bias_table,
              attn_norm, w_in, w_out, attn_sinks, ffn_norm, w_router_group, b_router_group,
              w_router_expert, b_router_expert, w_gate, w_up, w_down, final_norm):
    table_a = rel_bias_table[:, :H_A]
    table_b = rel_bias_table[:, H_A:]
    xp, xs = x_prompt, x_sample
    pak, pav, pbk, pbv = [], [], [], []
    sak, sav, sbk, sbv = [], [], [], []
    for layer in range(DEPTH):
        sinks = attn_sinks[layer].reshape(KV_B, G_B)
        qa, ka, va, qb, kb, vb = split_projection(rms_norm(xp, attn_norm[layer]) @ w_in[layer])
        oa = mixer_a_prompt(qa, ka, va, table_a)
        ob = mixer_b_prompt(qb, kb, vb, table_b, sinks)
        xp = xp + jnp.concatenate([oa, ob], axis=-1) @ w_out[layer]
        s = xp.shape[1]
        keep_a, keep_b = min(WIN_A, s), min(WIN_B, s)
        pak.append(ka[:, s - keep_a:]); pav.append(va[:, s - keep_a:])
        pbk.append(kb[:, s - keep_b:]); pbv.append(vb[:, s - keep_b:])
        qa, ka, va, qb, kb, vb = split_projection(rms_norm(xs, attn_norm[layer]) @ w_in[layer])
        oa = mixer_a_sample(qa, ka, va, cache_a_k[layer], cache_a_v[layer], table_a)
        ob = mixer_b_sample(qb, kb, vb, cache_b_k[layer], cache_b_v[layer], table_b, sinks)
        xs = xs + jnp.concatenate([oa, ob], axis=-1) @ w_out[layer]
        sak.append(ka); sav.append(va); sbk.append(kb); sbv.append(vb)
        xp = xp + hier_moe(rms_norm(xp, ffn_norm[layer]), w_router_group[layer], b_router_group[layer],
                           w_router_expert[layer], b_router_expert[layer],
                           w_gate[layer], w_up[layer], w_down[layer])
        xs = xs + hier_moe(rms_norm(xs, ffn_norm[layer]), w_router_group[layer], b_router_group[layer],
                           w_router_expert[layer], b_router_expert[layer],
                           w_gate[layer], w_up[layer], w_down[layer])
    y_prompt = rms_norm(xp, final_norm)
    y_sample = rms_norm(xs, final_norm)
    return (y_prompt, y_sample,
            jnp.stack(pak), jnp.stack(pav), jnp.stack(pbk), jnp.stack(pbv),
            jnp.stack(sak), jnp.stack(sav), jnp.stack(sbk), jnp.stack(sbv))
```

```python
import functools
import math

import jax
import jax.numpy as jnp
import numpy as np
from jax import lax
from jax.experimental import pallas as pl
from jax.experimental.pallas import tpu as pltpu

D_MODEL = 1024
HEAD_DIM = 64
H_A = 8
H_B = 8
KV_B = 2
G_B = 4
DILATIONS = (1, 4, 16)
WINDOWS = (128, 512, 2048)
WIN_A = 2048
WIN_B = 128
NUM_BUCKETS = 32
MAX_DISTANCE = 2048
N_GROUPS = 4
EXPERTS_PER_GROUP = 8
N_EXPERTS = 32
TOP_K = 2
D_EXPERT = 512
EPS = 1e-5
SCALE = HEAD_DIM ** -0.5
PAST_LEN = 16384

LANES = 128
SPAN = 2048
QB = 128
NCHUNK = 9
A_WIDTH = H_A * HEAD_DIM
MOE_ROWS = 256
NEG = -1e30
VMEM_LIMIT = 56 * 1024 * 1024

F32 = jnp.float32
BF16 = jnp.bfloat16


def _t5_bucket_np(dist):
    dist = np.asarray(dist, np.int64)
    max_exact = NUM_BUCKETS // 2
    d = np.maximum(dist, 1).astype(np.float32)
    ratio = np.log(d / np.float32(max_exact)) / np.float32(math.log(MAX_DISTANCE / max_exact))
    large = max_exact + (ratio * np.float32(NUM_BUCKETS - max_exact)).astype(np.int32)
    large = np.minimum(large, NUM_BUCKETS - 1)
    return np.where(dist < max_exact, dist, large).astype(np.int32)


def _cparams(sem, vmem=VMEM_LIMIT):
    return pltpu.CompilerParams(dimension_semantics=sem, vmem_limit_bytes=vmem)


def _proj_prompt_kernel(x_ref, g_ref, w_ref, cs_ref, aperm_ref, qb_ref, kvb_ref, akv_ref, bkv_ref,
                        h_scr, p_scr):
    n = pl.program_id(1)

    @pl.when(n == 0)
    def _():
        x = x_ref[...]
        ms = jnp.mean(x * x, axis=-1, keepdims=True)
        h_scr[...] = (x * lax.rsqrt(ms + EPS) * g_ref[...]).astype(BF16)

    p = jnp.dot(h_scr[...], w_ref[...], preferred_element_type=F32) * cs_ref[...]

    @pl.when(n < 6)
    def _():
        aperm_ref[0] = p.astype(BF16)
        p_scr[0] = p[:, :LANES]
        p_scr[1] = p[:, LANES:]
        for gi, dil in ((1, 4), (2, 16)):
            rows = SPAN // dil
            for r in range(dil):
                t = jnp.concatenate([p_scr[0, pl.ds(r, rows, stride=dil), :],
                                     p_scr[1, pl.ds(r, rows, stride=dil), :]], axis=1)
                aperm_ref[gi, r * rows:(r + 1) * rows, :] = t.astype(BF16)

    @pl.when(n < 4)
    def _():
        akv_ref[...] = p

    @pl.when(jnp.logical_or(n == 6, n == 7))
    def _():
        qb_ref[...] = p.astype(BF16)

    @pl.when(n == 8)
    def _():
        kvb_ref[...] = p.astype(BF16)
        bkv_ref[...] = p


def _proj_prompt(x, gamma, w, cscale):
    s = x.shape[0]
    nspan = s // SPAN
    return pl.pallas_call(
        _proj_prompt_kernel,
        grid=(nspan, NCHUNK),
        in_specs=[
            pl.BlockSpec((SPAN, D_MODEL), lambda b, n: (b, 0)),
            pl.BlockSpec((1, D_MODEL), lambda b, n: (0, 0)),
            pl.BlockSpec((D_MODEL, 256), lambda b, n: (0, n)),
            pl.BlockSpec((1, 256), lambda b, n: (0, n)),
        ],
        out_specs=[
            pl.BlockSpec((3, SPAN, 256), lambda b, n: (0, b, jnp.minimum(n, 5))),
            pl.BlockSpec((SPAN, 256), lambda b, n: (b, jnp.clip(n - 6, 0, 1))),
            pl.BlockSpec((SPAN, 256), lambda b, n: (b, 0)),
            pl.BlockSpec((SPAN, 256), lambda b, n: (b, jnp.minimum(n, 3))),
            pl.BlockSpec((SPAN, 256), lambda b, n: (b, 0)),
        ],
        out_shape=[
            jax.ShapeDtypeStruct((3, s, 3 * A_WIDTH), BF16),
            jax.ShapeDtypeStruct((s, 512), BF16),
            jax.ShapeDtypeStruct((s, 256), BF16),
            jax.ShapeDtypeStruct((s, 1024), F32),
            jax.ShapeDtypeStruct((s, 256), F32),
        ],
        scratch_shapes=[pltpu.VMEM((SPAN, D_MODEL), BF16), pltpu.VMEM((2, SPAN, LANES), F32)],
        compiler_params=_cparams(("arbitrary", "arbitrary")),
        name="proj_prompt",
    )(x, gamma, w, cscale)


def _proj_sample_kernel(x_ref, g_ref, w_ref, cs_ref, q_ref, kv_ref):
    x = x_ref[...]
    ms = jnp.mean(x * x, axis=-1, keepdims=True)
    h = (x * lax.rsqrt(ms + EPS) * g_ref[...]).astype(BF16)
    p = jnp.dot(h, w_ref[...], preferred_element_type=F32) * cs_ref[...]
    kv_ref[:, :1024] = p[:, :1024]
    kv_ref[:, 1024:] = p[:, 2048:]
    q_ref[...] = p[:, 1024:2048]


def _proj_sample(x, gamma, w, cscale):
    t = x.shape[0]
    tm = 512
    return pl.pallas_call(
        _proj_sample_kernel,
        grid=(t // tm,),
        in_specs=[
            pl.BlockSpec((tm, D_MODEL), lambda i: (i, 0)),
            pl.BlockSpec((1, D_MODEL), lambda i: (0, 0)),
            pl.BlockSpec((D_MODEL, 2304), lambda i: (0, 0)),
            pl.BlockSpec((1, 2304), lambda i: (0, 0)),
        ],
        out_specs=[
            pl.BlockSpec((tm, 1024), lambda i: (i, 0)),
            pl.BlockSpec((tm, 1280), lambda i: (i, 0)),
        ],
        out_shape=[
            jax.ShapeDtypeStruct((t, 1024), F32),
            jax.ShapeDtypeStruct((t, 1280), F32),
        ],
        compiler_params=_cparams(("arbitrary",)),
        name="proj_sample",
    )(x, gamma, w, cscale)


def _attn_a_kernel(q_ref, kvc_ref, kvp_ref, bias_ref, e_ref, out_ref, o_scr, st_scr):
    b = pl.program_id(0)
    g = pl.program_id(1)
    nblk = jnp.where(g == 0, 16, jnp.where(g == 1, 4, 1))
    lane = lax.broadcasted_iota(jnp.int32, (QB, LANES), 1)
    lo = lane < HEAD_DIM
    col = lax.broadcasted_iota(jnp.int32, (2 * QB, 2 * QB), 1)
    prev_cols = col < QB

    def block(cb, carry):
        first = lax.rem(cb, nblk) == 0
        row0 = pl.multiple_of(cb * QB, QB)
        prow_c = pl.multiple_of(jnp.maximum(cb - 1, 0) * QB, QB)
        prow_p = pl.multiple_of(jnp.where(first, cb + nblk - 1, 0) * QB, QB)
        negp = jnp.where(jnp.logical_and(first, b == 0), NEG, 0.0).astype(F32)
        prev_mask = jnp.where(prev_cols, negp, 0.0)
        st_tile = jnp.zeros((QB, LANES), F32)
        for hp in range(4):
            ks = slice(hp * LANES, (hp + 1) * LANES)
            vs = slice(A_WIDTH + hp * LANES, A_WIDTH + (hp + 1) * LANES)
            q2 = q_ref[pl.ds(row0, QB), ks]
            zero = jnp.zeros_like(q2)
            qq = jnp.concatenate([jnp.where(lo, q2, zero), jnp.where(lo, zero, q2)], axis=0)
            kc = kvc_ref[pl.ds(row0, QB), ks]
            vc = kvc_ref[pl.ds(row0, QB), vs]
            kp = jnp.where(first, kvp_ref[pl.ds(prow_p, QB), ks], kvc_ref[pl.ds(prow_c, QB), ks])
            vp = jnp.where(first, kvp_ref[pl.ds(prow_p, QB), vs], kvc_ref[pl.ds(prow_c, QB), vs])
            kk = jnp.concatenate([kp, kc], axis=0)
            vv = jnp.concatenate([vp, vc], axis=0)
            s = lax.dot_general(qq, kk, (((1,), (1,)), ((), ())), preferred_element_type=F32)
            s = s + bias_ref[hp] + prev_mask
            m = jnp.max(s, axis=-1, keepdims=True)
            p = jnp.exp(s - m)
            l = jnp.sum(p, axis=-1, keepdims=True)
            o = jnp.dot(p.astype(BF16), vv, preferred_element_type=F32) / l
            o_scr[g, hp, pl.ds(row0, QB), :] = jnp.where(lo, o[:QB], o[QB:])
            lse = m + jnp.log(l)
            st_tile = jnp.where(lane == 2 * hp, lse[:QB], st_tile)
            st_tile = jnp.where(lane == 2 * hp + 1, lse[QB:], st_tile)
        st_scr[g, pl.ds(row0, QB), :] = st_tile
        return carry

    lax.fori_loop(0, SPAN // QB, block, 0)

    @pl.when(g == 2)
    def _():
        def merge(c, carry):
            r2 = lax.rem(c, 4) * (SPAN // 4) + c // 4
            r3 = pl.multiple_of(c * QB, QB)
            l1 = st_scr[0, pl.ds(c, QB, stride=16), :]
            l2 = st_scr[1, pl.ds(r2, QB, stride=4), :]
            l3 = st_scr[2, pl.ds(r3, QB), :]
            mx = jnp.maximum(jnp.maximum(l1, l2), l3)
            w1 = jnp.exp(l1 - mx)
            w2 = jnp.exp(l2 - mx)
            w3 = jnp.exp(l3 - mx)
            tot = w1 + w2 + w3
            e = e_ref[...]
            a1 = jnp.dot(w1 / tot, e, precision=lax.Precision.HIGHEST, preferred_element_type=F32)
            a2 = jnp.dot(w2 / tot, e, precision=lax.Precision.HIGHEST, preferred_element_type=F32)
            a3 = jnp.dot(w3 / tot, e, precision=lax.Precision.HIGHEST, preferred_element_type=F32)
            for hp in range(4):
                sl = slice(hp * LANES, (hp + 1) * LANES)
                o1 = o_scr[0, hp, pl.ds(c, QB, stride=16), :]
                o2 = o_scr[1, hp, pl.ds(r2, QB, stride=4), :]
                o3 = o_scr[2, hp, pl.ds(r3, QB), :]
                out_ref[hp, pl.ds(c, QB, stride=16), :] = a1[:, sl] * o1 + a2[:, sl] * o2 + a3[:, sl] * o3
            return carry

        lax.fori_loop(0, 16, merge, 0)


def _attn_a_prompt(aperm, bias_a, emat):
    s = aperm.shape[1]
    nspan = s // SPAN
    return pl.pallas_call(
        _attn_a_kernel,
        grid=(nspan, 3),
        in_specs=[
            pl.BlockSpec((None, SPAN, A_WIDTH), lambda b, g: (g, b, 2)),
            pl.BlockSpec((None, SPAN, 2 * A_WIDTH), lambda b, g: (g, b, 0)),
            pl.BlockSpec((None, SPAN, 2 * A_WIDTH), lambda b, g: (g, jnp.maximum(b - 1, 0), 0)),
            pl.BlockSpec((None, 4, 2 * QB, 2 * QB), lambda b, g: (g, 0, 0, 0)),
            pl.BlockSpec((LANES, A_WIDTH), lambda b, g: (0, 0)),
        ],
        out_specs=pl.BlockSpec((4, SPAN, LANES), lambda b, g: (0, b, 0)),
        out_shape=jax.ShapeDtypeStruct((4, s, LANES), F32),
        scratch_shapes=[pltpu.VMEM((3, 4, SPAN, LANES), F32), pltpu.VMEM((3, SPAN, LANES), F32)],
        compiler_params=_cparams(("arbitrary", "arbitrary")),
        name="attn_a_prompt",
    )(aperm, aperm, aperm, bias_a, emat)


def _attn_b_kernel(q_ref, kvc_ref, kvp_ref, bias_ref, sink_ref, out_ref):
    i = pl.program_id(0)
    lane = lax.broadcasted_iota(jnp.int32, (QB, LANES), 1)
    lo = lane < HEAD_DIM
    col = lax.broadcasted_iota(jnp.int32, (8 * QB, 2 * QB), 1)
    negp = jnp.where(i == 0, NEG, 0.0).astype(F32)
    parts = []
    for g in range(G_B):
        q2 = q_ref[:, g * LANES:(g + 1) * LANES]
        zero = jnp.zeros_like(q2)
        parts += [jnp.where(lo, q2, zero), jnp.where(lo, zero, q2)]
    qq = jnp.concatenate(parts, axis=0)
    kk = jnp.concatenate([kvp_ref[:, :LANES], kvc_ref[:, :LANES]], axis=0)
    vv = jnp.concatenate([kvp_ref[:, LANES:], kvc_ref[:, LANES:]], axis=0)
    s = lax.dot_general(qq, kk, (((1,), (1,)), ((), ())), preferred_element_type=F32)
    s = s + bias_ref[...] + jnp.where(col < QB, negp, 0.0)
    sink = sink_ref[...]
    m = jnp.maximum(jnp.max(s, axis=-1, keepdims=True), sink)
    p = jnp.exp(s - m)
    den = jnp.sum(p, axis=-1, keepdims=True) + jnp.exp(sink - m)
    o = jnp.dot(p.astype(BF16), vv, preferred_element_type=F32) / den
    for g in range(G_B):
        out_ref[:, g * LANES:(g + 1) * LANES] = jnp.where(
            lo, o[(2 * g) * QB:(2 * g + 1) * QB], o[(2 * g + 1) * QB:(2 * g + 2) * QB]).astype(BF16)


def _attn_b_prompt(qb, kvb, bias_b, sink_rows):
    s = qb.shape[0]
    return pl.pallas_call(
        _attn_b_kernel,
        grid=(s // QB,),
        in_specs=[
            pl.BlockSpec((QB, 512), lambda i: (i, 0)),
            pl.BlockSpec((QB, 256), lambda i: (i, 0)),
            pl.BlockSpec((QB, 256), lambda i: (jnp.maximum(i - 1, 0), 0)),
            pl.BlockSpec((8 * QB, 2 * QB), lambda i: (0, 0)),
            pl.BlockSpec((8 * QB, 1), lambda i: (0, 0)),
        ],
        out_specs=pl.BlockSpec((QB, 512), lambda i: (i, 0)),
        out_shape=jax.ShapeDtypeStruct((s, 512), BF16),
        compiler_params=_cparams(("arbitrary",)),
        name="attn_b_prompt",
    )(qb, kvb, kvb, bias_b, sink_rows)


def _attn_sample_kernel(q_ref, kvn_ref, akt_ref, avt_ref, bkt_ref, bvt_ref, cba_ref, cbb_ref, sink_ref,
                        oa_ref, ob_ref):
    t = q_ref.shape[0]
    q = q_ref[...]
    kvn = kvn_ref[...]
    kvn_p = jnp.concatenate([kvn, jnp.zeros((LANES - t, kvn.shape[1]), F32)], axis=0).astype(BF16)
    lane_a = lax.broadcasted_iota(jnp.int32, (t, A_WIDTH), 1) // HEAD_DIM

    qa = q[:, :A_WIDTH]
    qbd = jnp.concatenate([jnp.where(lane_a == h, qa, 0.0) for h in range(H_A)], axis=0).astype(BF16)
    s_c = jnp.dot(qbd, akt_ref[...].astype(BF16), preferred_element_type=F32)
    s_n = lax.dot_general(qbd, kvn_p[:, :A_WIDTH], (((1,), (1,)), ((), ())), preferred_element_type=F32)
    s = jnp.concatenate([s_c, s_n], axis=1) + cba_ref[...]
    m = jnp.max(s, axis=-1, keepdims=True)
    p = jnp.exp(s - m)
    l = jnp.sum(p, axis=-1, keepdims=True)
    o_n = jnp.dot(p[:, WIN_A:].astype(BF16), kvn_p[:, A_WIDTH:2 * A_WIDTH], preferred_element_type=F32)
    pc = jnp.concatenate([p[:, :WIN_A], jnp.zeros((LANES - H_A * t, WIN_A), F32)], axis=0).astype(BF16)
    o_t = lax.dot_general(avt_ref[...].astype(BF16), pc, (((1,), (1,)), ((), ())),
                          preferred_element_type=F32)
    o_all = o_t.T[:H_A * t] + o_n
    o_sel = jnp.zeros((t, A_WIDTH), F32)
    l_b = jnp.ones((t, A_WIDTH), F32)
    for h in range(H_A):
        sel = lane_a == h
        o_sel = jnp.where(sel, o_all[h * t:(h + 1) * t], o_sel)
        l_b = jnp.where(sel, l[h * t:(h + 1) * t], l_b)
    oa_ref[...] = o_sel / l_b

    lane_b = lax.broadcasted_iota(jnp.int32, (G_B * t, LANES), 1)
    lo = lane_b < HEAD_DIM
    qb2 = jnp.concatenate([q[:, A_WIDTH + g * LANES:A_WIDTH + (g + 1) * LANES] for g in range(G_B)], axis=0)
    qm = jnp.concatenate([jnp.where(lo, qb2, 0.0), jnp.where(lo, 0.0, qb2)], axis=0).astype(BF16)
    kb_n = kvn_p[:, 2 * A_WIDTH:2 * A_WIDTH + LANES]
    vb_n = kvn_p[:, 2 * A_WIDTH + LANES:]
    sb_c = jnp.dot(qm, bkt_ref[...].astype(BF16), preferred_element_type=F32)
    sb_n = lax.dot_general(qm, kb_n, (((1,), (1,)), ((), ())), preferred_element_type=F32)
    sb = jnp.concatenate([sb_c, sb_n], axis=1) + cbb_ref[...]
    sink = sink_ref[...]
    mb = jnp.maximum(jnp.max(sb, axis=-1, keepdims=True), sink)
    pbb = jnp.exp(sb - mb)
    den = jnp.sum(pbb, axis=-1, keepdims=True) + jnp.exp(sink - mb)
    pbb = pbb.astype(BF16)
    ob = lax.dot_general(pbb[:, :WIN_B], bvt_ref[...].astype(BF16), (((1,), (1,)), ((), ())),
                         preferred_element_type=F32)
    ob = (ob + jnp.dot(pbb[:, WIN_B:], vb_n, preferred_element_type=F32)) / den
    half = G_B * t
    lo8 = lo[:t]
    for g in range(G_B):
        ob_ref[:, g * LANES:(g + 1) * LANES] = jnp.where(
            lo8, ob[g * t:(g + 1) * t], ob[half + g * t:half + (g + 1) * t])


def _attn_sample(q3, kvn3, akt, avt, bkt, bvt, cbias_a, cbias_b, sink_rows):
    ns, t = q3.shape[0], q3.shape[1]
    return pl.pallas_call(
        _attn_sample_kernel,
        grid=(ns,),
        in_specs=[
            pl.BlockSpec((None, t, 1024), lambda n: (n, 0, 0)),
            pl.BlockSpec((None, t, 1280), lambda n: (n, 0, 0)),
            pl.BlockSpec((None, A_WIDTH, WIN_A), lambda n: (n, 0, 0)),
            pl.BlockSpec((None, A_WIDTH, WIN_A), lambda n: (n, 0, 0)),
            pl.BlockSpec((None, LANES, WIN_B), lambda n: (n, 0, 0)),
            pl.BlockSpec((None, LANES, WIN_B), lambda n: (n, 0, 0)),
            pl.BlockSpec((H_A * t, WIN_A + LANES), lambda n: (0, 0)),
            pl.BlockSpec((H_B * t, WIN_B + LANES), lambda n: (0, 0)),
            pl.BlockSpec((H_B * t, 1), lambda n: (0, 0)),
        ],
        out_specs=[
            pl.BlockSpec((None, t, 512), lambda n: (n, 0, 0)),
            pl.BlockSpec((None, t, 512), lambda n: (n, 0, 0)),
        ],
        out_shape=[jax.ShapeDtypeStruct((ns, t, 512), F32), jax.ShapeDtypeStruct((ns, t, 512), F32)],
        compiler_params=_cparams(("arbitrary",)),
        name="attn_sample",
    )(q3, kvn3, akt, avt, bkt, bvt, cbias_a, cbias_b, sink_rows)


def _route(logits):
    lane = lax.broadcasted_iota(jnp.int32, logits.shape, 1).astype(F32)
    big = jnp.float32(1 << 20)
    ninf = jnp.float32(-jnp.inf)
    gmask = lane < N_GROUPS
    lg = jnp.where(gmask, logits, ninf)
    gmax = jnp.max(lg, axis=-1, keepdims=True)
    grp = jnp.min(jnp.where(lg == gmax, lane, big), axis=-1, keepdims=True)
    pg_top = 1.0 / jnp.sum(jnp.exp(lg - gmax), axis=-1, keepdims=True)
    e0 = N_GROUPS + grp * EXPERTS_PER_GROUP
    emask = jnp.logical_and(lane >= e0, lane < e0 + EXPERTS_PER_GROUP)
    le = jnp.where(emask, logits, ninf)
    emax = jnp.max(le, axis=-1, keepdims=True)
    esum = jnp.sum(jnp.exp(le - emax), axis=-1, keepdims=True)
    i1 = jnp.min(jnp.where(le == emax, lane, big), axis=-1, keepdims=True)
    le2 = jnp.where(lane == i1, ninf, le)
    e2max = jnp.max(le2, axis=-1, keepdims=True)
    i2 = jnp.min(jnp.where(le2 == e2max, lane, big), axis=-1, keepdims=True)
    p1 = 1.0 / esum
    p2 = jnp.exp(e2max - emax) / esum
    g1 = pg_top * p1 / (p1 + p2)
    g2 = pg_top * p2 / (p1 + p2)
    out = jnp.where(lane == 0, i1 - N_GROUPS, 0.0)
    out = jnp.where(lane == 1, i2 - N_GROUPS, out)
    out = jnp.where(lane == 2, g1, out)
    out = jnp.where(lane == 3, g2, out)
    return out


def _out_router_kernel(x_ref, a_ref, b_ref, wo_ref, g_ref, wr_ref, br_ref, x1_ref, xn_ref, route_ref):
    mix = jnp.concatenate([a_ref[0], a_ref[1], a_ref[2], a_ref[3]], axis=1).astype(BF16)
    mix = jnp.concatenate([mix, b_ref[...].astype(BF16)], axis=1)
    x1 = x_ref[...] + jnp.dot(mix, wo_ref[...], preferred_element_type=F32)
    x1_ref[...] = x1
    ms = jnp.mean(x1 * x1, axis=-1, keepdims=True)
    xn = x1 * lax.rsqrt(ms + EPS) * g_ref[...]
    xn_ref[...] = xn.astype(BF16)
    logits = jnp.dot(xn, wr_ref[...], precision=lax.Precision.HIGHEST, preferred_element_type=F32)
    route_ref[...] = _route(logits + br_ref[...])


def _out_router(x, a4, bmix, wo, gamma, wr, br):
    t = x.shape[0]
    tm = 512
    return pl.pallas_call(
        _out_router_kernel,
        grid=(t // tm,),
        in_specs=[
            pl.BlockSpec((tm, D_MODEL), lambda i: (i, 0)),
            pl.BlockSpec((4, tm, LANES), lambda i: (0, i, 0)),
            pl.BlockSpec((tm, 512), lambda i: (i, 0)),
            pl.BlockSpec((D_MODEL, D_MODEL), lambda i: (0, 0)),
            pl.BlockSpec((1, D_MODEL), lambda i: (0, 0)),
            pl.BlockSpec((D_MODEL, LANES), lambda i: (0, 0)),
            pl.BlockSpec((1, LANES), lambda i: (0, 0)),
        ],
        out_specs=[
            pl.BlockSpec((tm, D_MODEL), lambda i: (i, 0)),
            pl.BlockSpec((tm, D_MODEL), lambda i: (i, 0)),
            pl.BlockSpec((tm, LANES), lambda i: (i, 0)),
        ],
        out_shape=[
            jax.ShapeDtypeStruct((t, D_MODEL), F32),
            jax.ShapeDtypeStruct((t, D_MODEL), BF16),
            jax.ShapeDtypeStruct((t, LANES), F32),
        ],
        compiler_params=_cparams(("arbitrary",)),
        name="out_router",
    )(x, a4, bmix, wo, gamma, wr, br)


def _expert_kernel(be_ref, nu_ref, x_ref, wg_ref, wu_ref, wd_ref, o_ref, wg_s, wu_s, wd_s):
    i = pl.program_id(0)
    used = i < nu_ref[0]
    changed = jnp.logical_or(i == 0, be_ref[i] != be_ref[jnp.maximum(i - 1, 0)])

    @pl.when(jnp.logical_and(used, changed))
    def _():
        wg_s[...] = wg_ref[...].astype(BF16)
        wu_s[...] = wu_ref[...].astype(BF16)
        wd_s[...] = wd_ref[...].astype(BF16)

    @pl.when(used)
    def _():
        x = x_ref[...]
        gate = jnp.dot(x, wg_s[...], preferred_element_type=F32)
        up = jnp.dot(x, wu_s[...], preferred_element_type=F32)
        h = (gate * jax.nn.sigmoid(gate) * up).astype(BF16)
        o_ref[...] = jnp.dot(h, wd_s[...], preferred_element_type=F32).astype(BF16)


def _experts(blk_e, n_used, xb, w_gate, w_up, w_down):
    rows = xb.shape[0]
    nblocks = rows // MOE_ROWS
    grid_spec = pltpu.PrefetchScalarGridSpec(
        num_scalar_prefetch=2,
        grid=(nblocks,),
        in_specs=[
            pl.BlockSpec((MOE_ROWS, D_MODEL), lambda i, be, nu: (i, 0)),
            pl.BlockSpec((None, D_MODEL, D_EXPERT), lambda i, be, nu: (be[i], 0, 0)),
            pl.BlockSpec((None, D_MODEL, D_EXPERT), lambda i, be, nu: (be[i], 0, 0)),
            pl.BlockSpec((None, D_EXPERT, D_MODEL), lambda i, be, nu: (be[i], 0, 0)),
        ],
        out_specs=pl.BlockSpec((MOE_ROWS, D_MODEL), lambda i, be, nu: (i, 0)),
        scratch_shapes=[pltpu.VMEM((D_MODEL, D_EXPERT), BF16), pltpu.VMEM((D_MODEL, D_EXPERT), BF16),
                        pltpu.VMEM((D_EXPERT, D_MODEL), BF16)],
    )
    return pl.pallas_call(
        _expert_kernel,
        grid_spec=grid_spec,
        out_shape=jax.ShapeDtypeStruct((rows, D_MODEL), BF16),
        compiler_params=_cparams(("arbitrary",)),
        name="experts",
    )(blk_e, n_used, xb, w_gate, w_up, w_down)


def _combine_kernel(x1_ref, y1_ref, y2_ref, route_ref, g_ref, out_ref):
    r = route_ref[...]
    x = x1_ref[...] + r[:, 2:3] * y1_ref[...].astype(F32) + r[:, 3:4] * y2_ref[...].astype(F32)
    ms = jnp.mean(x * x, axis=-1, keepdims=True)
    out_ref[...] = x * lax.rsqrt(ms + EPS) * g_ref[...]


def _combine_norm(x1, y1, y2, route, gamma):
    t = x1.shape[0]
    tm = 512
    return pl.pallas_call(
        _combine_kernel,
        grid=(t // tm,),
        in_specs=[
            pl.BlockSpec((tm, D_MODEL), lambda i: (i, 0)),
            pl.BlockSpec((tm, D_MODEL), lambda i: (i, 0)),
            pl.BlockSpec((tm, D_MODEL), lambda i: (i, 0)),
            pl.BlockSpec((tm, LANES), lambda i: (i, 0)),
            pl.BlockSpec((1, D_MODEL), lambda i: (0, 0)),
        ],
        out_specs=pl.BlockSpec((tm, D_MODEL), lambda i: (i, 0)),
        out_shape=jax.ShapeDtypeStruct((t, D_MODEL), F32),
        compiler_params=_cparams(("arbitrary",)),
        name="combine_norm",
    )(x1, y1, y2, route, gamma)


def _bias_a_prompt(table_a):
    qi = np.arange(QB)[:, None]
    ki = np.arange(2 * QB)[None, :]
    j = qi + QB - ki
    valid = (j >= 0) & (j <= QB)
    idx = np.stack([_t5_bucket_np(d * np.clip(j, 0, QB)) for d in DILATIONS])
    b = jnp.transpose(table_a[idx], (0, 3, 1, 2))
    b = jnp.where(valid[None, None], b, NEG)
    return b.reshape(3, 4, 2 * QB, 2 * QB)


def _bias_b_prompt(table_b):
    qi = np.arange(QB)[:, None]
    ki = np.arange(2 * QB)[None, :]
    dist = qi + QB - ki
    valid = (dist >= 0) & (dist < WIN_B)
    b = jnp.transpose(table_b[_t5_bucket_np(np.maximum(dist, 0))], (2, 0, 1))
    b = jnp.where(valid[None], b, NEG)
    b = jnp.transpose(b.reshape(KV_B, G_B, QB, 2 * QB), (1, 0, 2, 3))
    return b.reshape(H_B * QB, 2 * QB)


def _bias_a_sample(table_a, t):
    tt = np.arange(t)[:, None]
    r = np.arange(WIN_A + LANES)[None, :]
    dist = np.where(r < WIN_A, WIN_A + tt - r, tt - (r - WIN_A))
    exists = (r < WIN_A + t) & (dist >= 0)
    count = np.zeros(dist.shape, np.int32)
    for w, d in zip(WINDOWS, DILATIONS):
        count += (exists & (dist % d == 0) & (dist <= w)).astype(np.int32)
    logc = np.log(np.maximum(count, 1)).astype(np.float32)
    b = jnp.transpose(table_a[_t5_bucket_np(np.maximum(dist, 0))], (2, 0, 1)) + logc[None]
    b = jnp.where((count > 0)[None], b, NEG)
    return b.reshape(H_A * t, WIN_A + LANES)


def _bias_b_sample(table_b, t):
    tt = np.arange(t)[:, None]
    c = np.arange(WIN_B + LANES)[None, :]
    dist = np.where(c < WIN_B, WIN_B + tt - c, tt - (c - WIN_B))
    valid = (c < WIN_B + t) & (dist >= 0) & (dist < WIN_B)
    b = jnp.transpose(table_b[_t5_bucket_np(np.maximum(dist, 0))], (2, 0, 1))
    b = jnp.where(valid[None], b, NEG)
    return b.reshape(H_B * t, WIN_B + LANES)


def _dispatch(route):
    t = route.shape[0]
    a = t * TOP_K
    eid = route[:, :TOP_K].astype(jnp.int32).reshape(a)
    onehot = (eid[:, None] == jnp.arange(N_EXPERTS, dtype=jnp.int32)[None, :]).astype(jnp.int32)
    csum = jnp.cumsum(onehot, axis=0)
    counts = csum[-1]
    rank = jnp.sum(csum * onehot, axis=1) - 1
    padded = (counts + MOE_ROWS - 1) // MOE_ROWS * MOE_ROWS
    pend = jnp.cumsum(padded)
    pstart = pend - padded
    dest = pstart[eid] + rank
    nblocks = -(-a // MOE_ROWS) + N_EXPERTS
    row_tok = jnp.zeros((nblocks * MOE_ROWS,), jnp.int32).at[dest].set(jnp.arange(a, dtype=jnp.int32) // TOP_K)
    blk_e = jnp.minimum(jnp.searchsorted(pend, jnp.arange(nblocks, dtype=jnp.int32) * MOE_ROWS, side='right'),
                        N_EXPERTS - 1).astype(jnp.int32)
    n_used = (pend[-1] // MOE_ROWS).astype(jnp.int32).reshape(1)
    return row_tok, blk_e, n_used, dest.reshape(t, TOP_K)


def kernel(x_prompt, x_sample, cache_a_k, cache_a_v, cache_b_k, cache_b_v, rel_bias_table, attn_norm, w_in,
           w_out, attn_sinks, ffn_norm, w_router_group, b_router_group, w_router_expert, b_router_expert,
           w_gate, w_up, w_down, final_norm):
    s = x_prompt.shape[1]
    ns, ts = x_sample.shape[0], x_sample.shape[1]
    table_a = rel_bias_table[:, :H_A]
    table_b = rel_bias_table[:, H_A:]

    w = w_in[0]
    wqa, wka, wva, wqb, wkb, wvb = (w[:, 0:512], w[:, 512:1024], w[:, 1024:1536], w[:, 1536:2048],
                                    w[:, 2048:2176], w[:, 2176:2304])
    wqb = jnp.transpose(wqb.reshape(D_MODEL, KV_B, G_B, HEAD_DIM), (0, 2, 1, 3)).reshape(D_MODEL, 512)
    wp = jnp.concatenate([wka, wva, wqa, wqb, wkb, wvb], axis=1).astype(BF16)
    cscale = jnp.concatenate([jnp.ones((1, 1024), F32), jnp.full((1, 1024), SCALE, F32),
                              jnp.ones((1, 256), F32)], axis=1)
    wo = w_out[0]
    wo_b = jnp.transpose(wo[512:].reshape(KV_B, G_B, HEAD_DIM, D_MODEL), (1, 0, 2, 3)).reshape(512, D_MODEL)
    wo_p = jnp.concatenate([wo[:512], wo_b], axis=0).astype(BF16)
    wr = jnp.concatenate([w_router_group[0],
                          jnp.transpose(w_router_expert[0], (1, 0, 2)).reshape(D_MODEL, N_EXPERTS),
                          jnp.zeros((D_MODEL, LANES - N_GROUPS - N_EXPERTS), F32)], axis=1)
    br = jnp.concatenate([b_router_group[0], b_router_expert[0].reshape(N_EXPERTS),
                          jnp.zeros((LANES - N_GROUPS - N_EXPERTS,), F32)]).reshape(1, LANES)
    sinks_gk = jnp.transpose(attn_sinks[0].reshape(KV_B, G_B), (1, 0)).reshape(H_B)
    sink_rows_p = jnp.repeat(sinks_gk, QB).reshape(H_B * QB, 1)
    sink_rows_s = jnp.repeat(attn_sinks[0], ts).reshape(H_B * ts, 1)
    emat = (jnp.arange(LANES)[:, None] == (jnp.arange(A_WIDTH)[None, :] // HEAD_DIM)).astype(F32)
    attn_g = attn_norm[0].reshape(1, D_MODEL)
    ffn_g = ffn_norm[0].reshape(1, D_MODEL)

    xp = x_prompt.reshape(s, D_MODEL)
    aperm, qb_p, kvb_p, akv32, bkv32 = _proj_prompt(xp, attn_g, wp, cscale)
    a4 = _attn_a_prompt(aperm, _bias_a_prompt(table_a), emat)
    ob_p = _attn_b_prompt(qb_p, kvb_p, _bias_b_prompt(table_b), sink_rows_p)
    x1_p, xn_p, route_p = _out_router(xp, a4, ob_p, wo_p, ffn_g, wr, br)

    xs = x_sample.reshape(ns * ts, D_MODEL)
    q_s, kv_s = _proj_sample(xs, attn_g, wp, cscale)
    akt = jnp.transpose(cache_a_k[0], (0, 2, 3, 1)).reshape(ns, A_WIDTH, WIN_A)
    avt = jnp.transpose(cache_a_v[0], (0, 2, 3, 1)).reshape(ns, A_WIDTH, WIN_A)
    bkt = jnp.transpose(cache_b_k[0], (0, 2, 3, 1)).reshape(ns, LANES, WIN_B)
    bvt = jnp.transpose(cache_b_v[0], (0, 2, 3, 1)).reshape(ns, LANES, WIN_B)
    oa_s, ob_s = _attn_sample(q_s.reshape(ns, ts, 1024), kv_s.reshape(ns, ts, 1280), akt, avt, bkt, bvt,
                              _bias_a_sample(table_a, ts), _bias_b_sample(table_b, ts), sink_rows_s)
    a4_s = jnp.transpose(oa_s.reshape(ns * ts, 4, LANES), (1, 0, 2))
    x1_s, xn_s, route_s = _out_router(xs, a4_s, ob_s.reshape(ns * ts, 512), wo_p, ffn_g, wr, br)

    x1 = jnp.concatenate([x1_p, x1_s], axis=0)
    xn = jnp.concatenate([xn_p, xn_s], axis=0)
    route = jnp.concatenate([route_p, route_s], axis=0)
    row_tok, blk_e, n_used, pos = _dispatch(route)
    yb = _experts(blk_e, n_used, xn[row_tok], w_gate[0], w_up[0], w_down[0])
    y = _combine_norm(x1, yb[pos[:, 0]], yb[pos[:, 1]], route, final_norm.reshape(1, D_MODEL))

    y_prompt = y[:s].reshape(1, s, D_MODEL)
    y_sample = y[s:].reshape(ns, ts, D_MODEL)
    keep_a, keep_b = min(WIN_A, s), min(WIN_B, s)
    pak = akv32[s - keep_a:, :512].reshape(1, 1, keep_a, H_A, HEAD_DIM)
    pav = akv32[s - keep_a:, 512:].reshape(1, 1, keep_a, H_A, HEAD_DIM)
    pbk = bkv32[s - keep_b:, :128].reshape(1, 1, keep_b, KV_B, HEAD_DIM)
    pbv = bkv32[s - keep_b:, 128:].reshape(1, 1, keep_b, KV_B, HEAD_DIM)
    sak = kv_s[:, 0:512].reshape(1, ns, ts, H_A, HEAD_DIM)
    sav = kv_s[:, 512:1024].reshape(1, ns, ts, H_A, HEAD_DIM)
    sbk = kv_s[:, 1024:1152].reshape(1, ns, ts, KV_B, HEAD_DIM)
    sbv = kv_s[:, 1152:1280].reshape(1, ns, ts, KV_B, HEAD_DIM)
    return (y_prompt, y_sample, pak, pav, pbk, pbv, sak, sav, sbk, sbv)
```

```python
import functools
import math

import jax
import jax.numpy as jnp
import numpy as np
from jax import lax
from jax.experimental import pallas as pl
from jax.experimental.pallas import tpu as pltpu
from jax.experimental.pallas import tpu_sc as plsc

D_MODEL = 1024
HEAD_DIM = 64
H_A = 8
H_B = 8
KV_B = 2
G_B = 4
DILATIONS = (1, 4, 16)
WINDOWS = (128, 512, 2048)
WIN_A = 2048
WIN_B = 128
NUM_BUCKETS = 32
MAX_DISTANCE = 2048
N_GROUPS = 4
EXPERTS_PER_GROUP = 8
N_EXPERTS = 32
TOP_K = 2
D_EXPERT = 512
EPS = 1e-5
SCALE = HEAD_DIM ** -0.5
PAST_LEN = 16384

LANES = 128
SPAN = 2048
QB = 128
NCHUNK = 9
A_WIDTH = H_A * HEAD_DIM
MOE_ROWS = 256
SC_CORES = 2
SC_SUBCORES = 16
SC_WORKERS = SC_CORES * SC_SUBCORES
SC_WINDOW = 64
NEG = -1e30
VMEM_LIMIT = 56 * 1024 * 1024

F32 = jnp.float32
BF16 = jnp.bfloat16


def _t5_bucket_np(dist):
    dist = np.asarray(dist, np.int64)
    max_exact = NUM_BUCKETS // 2
    d = np.maximum(dist, 1).astype(np.float32)
    ratio = np.log(d / np.float32(max_exact)) / np.float32(math.log(MAX_DISTANCE / max_exact))
    large = max_exact + (ratio * np.float32(NUM_BUCKETS - max_exact)).astype(np.int32)
    large = np.minimum(large, NUM_BUCKETS - 1)
    return np.where(dist < max_exact, dist, large).astype(np.int32)


def _cparams(sem, vmem=VMEM_LIMIT):
    return pltpu.CompilerParams(dimension_semantics=sem, vmem_limit_bytes=vmem)


def _proj_prompt_kernel(x_ref, g_ref, w_ref, cs_ref, aperm_ref, qb_ref, kvb_ref, akv_ref, bkv_ref,
                        h_scr, p_scr):
    n = pl.program_id(1)

    @pl.when(n == 0)
    def _():
        x = x_ref[...]
        ms = jnp.mean(x * x, axis=-1, keepdims=True)
        h_scr[...] = (x * lax.rsqrt(ms + EPS) * g_ref[...]).astype(BF16)

    p = jnp.dot(h_scr[...], w_ref[...], preferred_element_type=F32) * cs_ref[...]

    @pl.when(n < 6)
    def _():
        aperm_ref[0] = p.astype(BF16)
        p_scr[0] = p[:, :LANES]
        p_scr[1] = p[:, LANES:]
        for gi, dil in ((1, 4), (2, 16)):
            rows = SPAN // dil
            for r in range(dil):
                t = jnp.concatenate([p_scr[0, pl.ds(r, rows, stride=dil), :],
                                     p_scr[1, pl.ds(r, rows, stride=dil), :]], axis=1)
                aperm_ref[gi, r * rows:(r + 1) * rows, :] = t.astype(BF16)

    @pl.when(n < 4)
    def _():
        akv_ref[...] = p

    @pl.when(jnp.logical_or(n == 6, n == 7))
    def _():
        qb_ref[...] = p.astype(BF16)

    @pl.when(n == 8)
    def _():
        kvb_ref[...] = p.astype(BF16)
        bkv_ref[...] = p


def _proj_prompt(x, gamma, w, cscale):
    s = x.shape[0]
    nspan = s // SPAN
    return pl.pallas_call(
        _proj_prompt_kernel,
        grid=(nspan, NCHUNK),
        in_specs=[
            pl.BlockSpec((SPAN, D_MODEL), lambda b, n: (b, 0)),
            pl.BlockSpec((1, D_MODEL), lambda b, n: (0, 0)),
            pl.BlockSpec((D_MODEL, 256), lambda b, n: (0, n)),
            pl.BlockSpec((1, 256), lambda b, n: (0, n)),
        ],
        out_specs=[
            pl.BlockSpec((3, SPAN, 256), lambda b, n: (0, b, jnp.minimum(n, 5))),
            pl.BlockSpec((SPAN, 256), lambda b, n: (b, jnp.clip(n - 6, 0, 1))),
            pl.BlockSpec((SPAN, 256), lambda b, n: (b, 0)),
            pl.BlockSpec((SPAN, 256), lambda b, n: (b, jnp.minimum(n, 3))),
            pl.BlockSpec((SPAN, 256), lambda b, n: (b, 0)),
        ],
        out_shape=[
            jax.ShapeDtypeStruct((3, s, 3 * A_WIDTH), BF16),
            jax.ShapeDtypeStruct((s, 512), BF16),
            jax.ShapeDtypeStruct((s, 256), BF16),
            jax.ShapeDtypeStruct((s, 1024), F32),
            jax.ShapeDtypeStruct((s, 256), F32),
        ],
        scratch_shapes=[pltpu.VMEM((SPAN, D_MODEL), BF16), pltpu.VMEM((2, SPAN, LANES), F32)],
        compiler_params=_cparams(("arbitrary", "arbitrary")),
        name="proj_prompt",
    )(x, gamma, w, cscale)


def _proj_sample_kernel(x_ref, g_ref, w_ref, cs_ref, q_ref, kv_ref):
    x = x_ref[...]
    ms = jnp.mean(x * x, axis=-1, keepdims=True)
    h = (x * lax.rsqrt(ms + EPS) * g_ref[...]).astype(BF16)
    p = jnp.dot(h, w_ref[...], preferred_element_type=F32) * cs_ref[...]
    kv_ref[:, :1024] = p[:, :1024]
    kv_ref[:, 1024:] = p[:, 2048:]
    q_ref[...] = p[:, 1024:2048]


def _proj_sample(x, gamma, w, cscale):
    t = x.shape[0]
    tm = 512
    return pl.pallas_call(
        _proj_sample_kernel,
        grid=(t // tm,),
        in_specs=[
            pl.BlockSpec((tm, D_MODEL), lambda i: (i, 0)),
            pl.BlockSpec((1, D_MODEL), lambda i: (0, 0)),
            pl.BlockSpec((D_MODEL, 2304), lambda i: (0, 0)),
            pl.BlockSpec((1, 2304), lambda i: (0, 0)),
        ],
        out_specs=[
            pl.BlockSpec((tm, 1024), lambda i: (i, 0)),
            pl.BlockSpec((tm, 1280), lambda i: (i, 0)),
        ],
        out_shape=[
            jax.ShapeDtypeStruct((t, 1024), F32),
            jax.ShapeDtypeStruct((t, 1280), F32),
        ],
        compiler_params=_cparams(("arbitrary",)),
        name="proj_sample",
    )(x, gamma, w, cscale)


def _attn_a_kernel(q_ref, kvc_ref, kvp_ref, bias_ref, e_ref, out_ref, o_scr, st_scr):
    b = pl.program_id(0)
    g = pl.program_id(1)
    nblk = jnp.where(g == 0, 16, jnp.where(g == 1, 4, 1))
    lane = lax.broadcasted_iota(jnp.int32, (QB, LANES), 1)
    lo = lane < HEAD_DIM
    col = lax.broadcasted_iota(jnp.int32, (2 * QB, 2 * QB), 1)
    prev_cols = col < QB

    def block(cb, carry):
        first = lax.rem(cb, nblk) == 0
        row0 = pl.multiple_of(cb * QB, QB)
        prow_c = pl.multiple_of(jnp.maximum(cb - 1, 0) * QB, QB)
        prow_p = pl.multiple_of(jnp.where(first, cb + nblk - 1, 0) * QB, QB)
        negp = jnp.where(jnp.logical_and(first, b == 0), NEG, 0.0).astype(F32)
        prev_mask = jnp.where(prev_cols, negp, 0.0)
        st_tile = jnp.zeros((QB, LANES), F32)
        for hp in range(4):
            ks = slice(hp * LANES, (hp + 1) * LANES)
            vs = slice(A_WIDTH + hp * LANES, A_WIDTH + (hp + 1) * LANES)
            q2 = q_ref[pl.ds(row0, QB), ks]
            zero = jnp.zeros_like(q2)
            qq = jnp.concatenate([jnp.where(lo, q2, zero), jnp.where(lo, zero, q2)], axis=0)
            kc = kvc_ref[pl.ds(row0, QB), ks]
            vc = kvc_ref[pl.ds(row0, QB), vs]
            kp = jnp.where(first, kvp_ref[pl.ds(prow_p, QB), ks], kvc_ref[pl.ds(prow_c, QB), ks])
            vp = jnp.where(first, kvp_ref[pl.ds(prow_p, QB), vs], kvc_ref[pl.ds(prow_c, QB), vs])
            kk = jnp.concatenate([kp, kc], axis=0)
            vv = jnp.concatenate([vp, vc], axis=0)
            s = lax.dot_general(qq, kk, (((1,), (1,)), ((), ())), preferred_element_type=F32)
            s = s + bias_ref[hp] + prev_mask
            m = jnp.max(s, axis=-1, keepdims=True)
            p = jnp.exp(s - m)
            l = jnp.sum(p, axis=-1, keepdims=True)
            o = jnp.dot(p.astype(BF16), vv, preferred_element_type=F32) / l
            o_scr[g, hp, pl.ds(row0, QB), :] = jnp.where(lo, o[:QB], o[QB:])
            lse = m + jnp.log(l)
            st_tile = jnp.where(lane == 2 * hp, lse[:QB], st_tile)
            st_tile = jnp.where(lane == 2 * hp + 1, lse[QB:], st_tile)
        st_scr[g, pl.ds(row0, QB), :] = st_tile
        return carry

    lax.fori_loop(0, SPAN // QB, block, 0)

    @pl.when(g == 2)
    def _():
        def merge(c, carry):
            r2 = lax.rem(c, 4) * (SPAN // 4) + c // 4
            r3 = pl.multiple_of(c * QB, QB)
            l1 = st_scr[0, pl.ds(c, QB, stride=16), :]
            l2 = st_scr[1, pl.ds(r2, QB, stride=4), :]
            l3 = st_scr[2, pl.ds(r3, QB), :]
            mx = jnp.maximum(jnp.maximum(l1, l2), l3)
            w1 = jnp.exp(l1 - mx)
            w2 = jnp.exp(l2 - mx)
            w3 = jnp.exp(l3 - mx)
            tot = w1 + w2 + w3
            e = e_ref[...]
            a1 = jnp.dot(w1 / tot, e, precision=lax.Precision.HIGHEST, preferred_element_type=F32)
            a2 = jnp.dot(w2 / tot, e, precision=lax.Precision.HIGHEST, preferred_element_type=F32)
            a3 = jnp.dot(w3 / tot, e, precision=lax.Precision.HIGHEST, preferred_element_type=F32)
            for hp in range(4):
                sl = slice(hp * LANES, (hp + 1) * LANES)
                o1 = o_scr[0, hp, pl.ds(c, QB, stride=16), :]
                o2 = o_scr[1, hp, pl.ds(r2, QB, stride=4), :]
                o3 = o_scr[2, hp, pl.ds(r3, QB), :]
                out_ref[hp, pl.ds(c, QB, stride=16), :] = a1[:, sl] * o1 + a2[:, sl] * o2 + a3[:, sl] * o3
            return carry

        lax.fori_loop(0, 16, merge, 0)


def _attn_a_prompt(aperm, bias_a, emat):
    s = aperm.shape[1]
    nspan = s // SPAN
    return pl.pallas_call(
        _attn_a_kernel,
        grid=(nspan, 3),
        in_specs=[
            pl.BlockSpec((None, SPAN, A_WIDTH), lambda b, g: (g, b, 2)),
            pl.BlockSpec((None, SPAN, 2 * A_WIDTH), lambda b, g: (g, b, 0)),
            pl.BlockSpec((None, SPAN, 2 * A_WIDTH), lambda b, g: (g, jnp.maximum(b - 1, 0), 0)),
            pl.BlockSpec((None, 4, 2 * QB, 2 * QB), lambda b, g: (g, 0, 0, 0)),
            pl.BlockSpec((LANES, A_WIDTH), lambda b, g: (0, 0)),
        ],
        out_specs=pl.BlockSpec((4, SPAN, LANES), lambda b, g: (0, b, 0)),
        out_shape=jax.ShapeDtypeStruct((4, s, LANES), F32),
        scratch_shapes=[pltpu.VMEM((3, 4, SPAN, LANES), F32), pltpu.VMEM((3, SPAN, LANES), F32)],
        compiler_params=_cparams(("arbitrary", "arbitrary")),
        name="attn_a_prompt",
    )(aperm, aperm, aperm, bias_a, emat)


def _attn_b_kernel(q_ref, kvc_ref, kvp_ref, bias_ref, sink_ref, out_ref):
    i = pl.program_id(0)
    lane = lax.broadcasted_iota(jnp.int32, (QB, LANES), 1)
    lo = lane < HEAD_DIM
    col = lax.broadcasted_iota(jnp.int32, (8 * QB, 2 * QB), 1)
    negp = jnp.where(i == 0, NEG, 0.0).astype(F32)
    parts = []
    for g in range(G_B):
        q2 = q_ref[:, g * LANES:(g + 1) * LANES]
        zero = jnp.zeros_like(q2)
        parts += [jnp.where(lo, q2, zero), jnp.where(lo, zero, q2)]
    qq = jnp.concatenate(parts, axis=0)
    kk = jnp.concatenate([kvp_ref[:, :LANES], kvc_ref[:, :LANES]], axis=0)
    vv = jnp.concatenate([kvp_ref[:, LANES:], kvc_ref[:, LANES:]], axis=0)
    s = lax.dot_general(qq, kk, (((1,), (1,)), ((), ())), preferred_element_type=F32)
    s = s + bias_ref[...] + jnp.where(col < QB, negp, 0.0)
    sink = sink_ref[...]
    m = jnp.maximum(jnp.max(s, axis=-1, keepdims=True), sink)
    p = jnp.exp(s - m)
    den = jnp.sum(p, axis=-1, keepdims=True) + jnp.exp(sink - m)
    o = jnp.dot(p.astype(BF16), vv, preferred_element_type=F32) / den
    for g in range(G_B):
        out_ref[:, g * LANES:(g + 1) * LANES] = jnp.where(
            lo, o[(2 * g) * QB:(2 * g + 1) * QB], o[(2 * g + 1) * QB:(2 * g + 2) * QB]).astype(BF16)


def _attn_b_prompt(qb, kvb, bias_b, sink_rows):
    s = qb.shape[0]
    return pl.pallas_call(
        _attn_b_kernel,
        grid=(s // QB,),
        in_specs=[
            pl.BlockSpec((QB, 512), lambda i: (i, 0)),
            pl.BlockSpec((QB, 256), lambda i: (i, 0)),
            pl.BlockSpec((QB, 256), lambda i: (jnp.maximum(i - 1, 0), 0)),
            pl.BlockSpec((8 * QB, 2 * QB), lambda i: (0, 0)),
            pl.BlockSpec((8 * QB, 1), lambda i: (0, 0)),
        ],
        out_specs=pl.BlockSpec((QB, 512), lambda i: (i, 0)),
        out_shape=jax.ShapeDtypeStruct((s, 512), BF16),
        compiler_params=_cparams(("arbitrary",)),
        name="attn_b_prompt",
    )(qb, kvb, kvb, bias_b, sink_rows)


def _attn_sample_kernel(q_ref, kvn_ref, akt_ref, avt_ref, bkt_ref, bvt_ref, cba_ref, cbb_ref, sink_ref,
                        oa_ref, ob_ref):
    t = q_ref.shape[0]
    q = q_ref[...]
    kvn = kvn_ref[...]
    kvn_p = jnp.concatenate([kvn, jnp.zeros((LANES - t, kvn.shape[1]), F32)], axis=0).astype(BF16)
    lane_a = lax.broadcasted_iota(jnp.int32, (t, A_WIDTH), 1) // HEAD_DIM

    qa = q[:, :A_WIDTH]
    qbd = jnp.concatenate([jnp.where(lane_a == h, qa, 0.0) for h in range(H_A)], axis=0).astype(BF16)
    s_c = jnp.dot(qbd, akt_ref[...].astype(BF16), preferred_element_type=F32)
    s_n = lax.dot_general(qbd, kvn_p[:, :A_WIDTH], (((1,), (1,)), ((), ())), preferred_element_type=F32)
    s = jnp.concatenate([s_c, s_n], axis=1) + cba_ref[...]
    m = jnp.max(s, axis=-1, keepdims=True)
    p = jnp.exp(s - m)
    l = jnp.sum(p, axis=-1, keepdims=True)
    o_n = jnp.dot(p[:, WIN_A:].astype(BF16), kvn_p[:, A_WIDTH:2 * A_WIDTH], preferred_element_type=F32)
    pc = jnp.concatenate([p[:, :WIN_A], jnp.zeros((LANES - H_A * t, WIN_A), F32)], axis=0).astype(BF16)
    o_t = lax.dot_general(avt_ref[...].astype(BF16), pc, (((1,), (1,)), ((), ())),
                          preferred_element_type=F32)
    o_all = o_t.T[:H_A * t] + o_n
    o_sel = jnp.zeros((t, A_WIDTH), F32)
    l_b = jnp.ones((t, A_WIDTH), F32)
    for h in range(H_A):
        sel = lane_a == h
        o_sel = jnp.where(sel, o_all[h * t:(h + 1) * t], o_sel)
        l_b = jnp.where(sel, l[h * t:(h + 1) * t], l_b)
    oa_ref[...] = o_sel / l_b

    lane_b = lax.broadcasted_iota(jnp.int32, (G_B * t, LANES), 1)
    lo = lane_b < HEAD_DIM
    qb2 = jnp.concatenate([q[:, A_WIDTH + g * LANES:A_WIDTH + (g + 1) * LANES] for g in range(G_B)], axis=0)
    qm = jnp.concatenate([jnp.where(lo, qb2, 0.0), jnp.where(lo, 0.0, qb2)], axis=0).astype(BF16)
    kb_n = kvn_p[:, 2 * A_WIDTH:2 * A_WIDTH + LANES]
    vb_n = kvn_p[:, 2 * A_WIDTH + LANES:]
    sb_c = jnp.dot(qm, bkt_ref[...].astype(BF16), preferred_element_type=F32)
    sb_n = lax.dot_general(qm, kb_n, (((1,), (1,)), ((), ())), preferred_element_type=F32)
    sb = jnp.concatenate([sb_c, sb_n], axis=1) + cbb_ref[...]
    sink = sink_ref[...]
    mb = jnp.maximum(jnp.max(sb, axis=-1, keepdims=True), sink)
    pbb = jnp.exp(sb - mb)
    den = jnp.sum(pbb, axis=-1, keepdims=True) + jnp.exp(sink - mb)
    pbb = pbb.astype(BF16)
    ob = lax.dot_general(pbb[:, :WIN_B], bvt_ref[...].astype(BF16), (((1,), (1,)), ((), ())),
                         preferred_element_type=F32)
    ob = (ob + jnp.dot(pbb[:, WIN_B:], vb_n, preferred_element_type=F32)) / den
    half = G_B * t
    lo8 = lo[:t]
    for g in range(G_B):
        ob_ref[:, g * LANES:(g + 1) * LANES] = jnp.where(
            lo8, ob[g * t:(g + 1) * t], ob[half + g * t:half + (g + 1) * t])


def _attn_sample(q3, kvn3, akt, avt, bkt, bvt, cbias_a, cbias_b, sink_rows):
    ns, t = q3.shape[0], q3.shape[1]
    return pl.pallas_call(
        _attn_sample_kernel,
        grid=(ns,),
        in_specs=[
            pl.BlockSpec((None, t, 1024), lambda n: (n, 0, 0)),
            pl.BlockSpec((None, t, 1280), lambda n: (n, 0, 0)),
            pl.BlockSpec((None, A_WIDTH, WIN_A), lambda n: (n, 0, 0)),
            pl.BlockSpec((None, A_WIDTH, WIN_A), lambda n: (n, 0, 0)),
            pl.BlockSpec((None, LANES, WIN_B), lambda n: (n, 0, 0)),
            pl.BlockSpec((None, LANES, WIN_B), lambda n: (n, 0, 0)),
            pl.BlockSpec((H_A * t, WIN_A + LANES), lambda n: (0, 0)),
            pl.BlockSpec((H_B * t, WIN_B + LANES), lambda n: (0, 0)),
            pl.BlockSpec((H_B * t, 1), lambda n: (0, 0)),
        ],
        out_specs=[
            pl.BlockSpec((None, t, 512), lambda n: (n, 0, 0)),
            pl.BlockSpec((None, t, 512), lambda n: (n, 0, 0)),
        ],
        out_shape=[jax.ShapeDtypeStruct((ns, t, 512), F32), jax.ShapeDtypeStruct((ns, t, 512), F32)],
        compiler_params=_cparams(("arbitrary",)),
        name="attn_sample",
    )(q3, kvn3, akt, avt, bkt, bvt, cbias_a, cbias_b, sink_rows)


def _route(logits):
    lane = lax.broadcasted_iota(jnp.int32, logits.shape, 1).astype(F32)
    big = jnp.float32(1 << 20)
    ninf = jnp.float32(-jnp.inf)
    gmask = lane < N_GROUPS
    lg = jnp.where(gmask, logits, ninf)
    gmax = jnp.max(lg, axis=-1, keepdims=True)
    grp = jnp.min(jnp.where(lg == gmax, lane, big), axis=-1, keepdims=True)
    pg_top = 1.0 / jnp.sum(jnp.exp(lg - gmax), axis=-1, keepdims=True)
    e0 = N_GROUPS + grp * EXPERTS_PER_GROUP
    emask = jnp.logical_and(lane >= e0, lane < e0 + EXPERTS_PER_GROUP)
    le = jnp.where(emask, logits, ninf)
    emax = jnp.max(le, axis=-1, keepdims=True)
    esum = jnp.sum(jnp.exp(le - emax), axis=-1, keepdims=True)
    i1 = jnp.min(jnp.where(le == emax, lane, big), axis=-1, keepdims=True)
    le2 = jnp.where(lane == i1, ninf, le)
    e2max = jnp.max(le2, axis=-1, keepdims=True)
    i2 = jnp.min(jnp.where(le2 == e2max, lane, big), axis=-1, keepdims=True)
    p1 = 1.0 / esum
    p2 = jnp.exp(e2max - emax) / esum
    g1 = pg_top * p1 / (p1 + p2)
    g2 = pg_top * p2 / (p1 + p2)
    out = jnp.where(lane == 0, i1 - N_GROUPS, 0.0)
    out = jnp.where(lane == 1, i2 - N_GROUPS, out)
    out = jnp.where(lane == 2, g1, out)
    out = jnp.where(lane == 3, g2, out)
    return out


def _pack_bf16_pairs(x):
    half = x.shape[1] // 2

    def rne(v):
        bits = lax.bitcast_convert_type(v, jnp.int32)
        return bits + 0x7FFF + (lax.shift_right_logical(bits, 16) & 1)

    lo = lax.shift_right_logical(rne(x[:, :half]), 16)
    hi = rne(x[:, half:]) & jnp.int32(-65536)
    return lo | hi


def _unpack_bf16_pairs(w):
    lo = lax.bitcast_convert_type(lax.shift_left(w, 16), F32)
    hi = lax.bitcast_convert_type(w & jnp.int32(-65536), F32)
    return jnp.concatenate([lo, hi], axis=1)


def _out_router_kernel(xp_ref, ap_ref, bp_ref, xs_ref, as_ref, bs_ref, wo_ref, g_ref, wr_ref, br_ref,
                       x1_ref, xn_ref, route_ref, *, prompt_tiles):
    def body(x_ref, a_ref, b_ref):
        mix = jnp.concatenate([a_ref[0], a_ref[1], a_ref[2], a_ref[3]], axis=1).astype(BF16)
        mix = jnp.concatenate([mix, b_ref[...].astype(BF16)], axis=1)
        x1 = x_ref[...] + jnp.dot(mix, wo_ref[...], preferred_element_type=F32)
        x1_ref[...] = x1
        ms = jnp.mean(x1 * x1, axis=-1, keepdims=True)
        xn = x1 * lax.rsqrt(ms + EPS) * g_ref[...]
        xn_ref[...] = _pack_bf16_pairs(xn)
        logits = jnp.dot(xn, wr_ref[...], precision=lax.Precision.HIGHEST, preferred_element_type=F32)
        route_ref[...] = _route(logits + br_ref[...])

    i = pl.program_id(0)

    @pl.when(i < prompt_tiles)
    def _():
        body(xp_ref, ap_ref, bp_ref)

    @pl.when(i >= prompt_tiles)
    def _():
        body(xs_ref, as_ref, bs_ref)


def _out_router(xp, a4p, bp, xs, a4s, bs, wo, gamma, wr, br):
    tp, tsm = xp.shape[0], xs.shape[0]
    tm = 512
    npt, nst = tp // tm, tsm // tm
    t = tp + tsm
    pmap = lambda i: (jnp.minimum(i, npt - 1), 0)
    smap = lambda i: (jnp.maximum(i - npt, 0), 0)
    return pl.pallas_call(
        functools.partial(_out_router_kernel, prompt_tiles=npt),
        grid=(npt + nst,),
        in_specs=[
            pl.BlockSpec((tm, D_MODEL), pmap),
            pl.BlockSpec((4, tm, LANES), lambda i: (0, jnp.minimum(i, npt - 1), 0)),
            pl.BlockSpec((tm, 512), pmap),
            pl.BlockSpec((tm, D_MODEL), smap),
            pl.BlockSpec((4, tm, LANES), lambda i: (0, jnp.maximum(i - npt, 0), 0)),
            pl.BlockSpec((tm, 512), smap),
            pl.BlockSpec((D_MODEL, D_MODEL), lambda i: (0, 0)),
            pl.BlockSpec((1, D_MODEL), lambda i: (0, 0)),
            pl.BlockSpec((D_MODEL, LANES), lambda i: (0, 0)),
            pl.BlockSpec((1, LANES), lambda i: (0, 0)),
        ],
        out_specs=[
            pl.BlockSpec((tm, D_MODEL), lambda i: (i, 0)),
            pl.BlockSpec((tm, D_MODEL // 2), lambda i: (i, 0)),
            pl.BlockSpec((tm, LANES), lambda i: (i, 0)),
        ],
        out_shape=[
            jax.ShapeDtypeStruct((t, D_MODEL), F32),
            jax.ShapeDtypeStruct((t, D_MODEL // 2), jnp.int32),
            jax.ShapeDtypeStruct((t, LANES), F32),
        ],
        compiler_params=_cparams(("arbitrary",)),
        name="out_router",
    )(xp, a4p, bp, xs, a4s, bs, wo, gamma, wr, br)


def _sc_gather_rows(table, idx):
    b = idx.shape[0]
    d = table.shape[1]
    per_worker = b // SC_WORKERS
    nwin = per_worker // SC_WINDOW
    assert per_worker * SC_WORKERS == b and nwin * SC_WINDOW == per_worker
    mesh = plsc.VectorSubcoreMesh(core_axis_name="c", subcore_axis_name="s")

    @functools.partial(
        pl.kernel, mesh=mesh,
        out_type=jax.ShapeDtypeStruct((b, d), table.dtype),
        scratch_types=[pltpu.VMEM((SC_WINDOW,), jnp.int32), pltpu.VMEM((SC_WINDOW, d), table.dtype),
                       pltpu.SemaphoreType.DMA],
        name="sc_gather_rows",
    )
    def gather(table_hbm, idx_hbm, out_hbm, idx_v, rows_v, sem):
        wid = lax.axis_index("s") * SC_CORES + lax.axis_index("c")
        base = wid * per_worker

        @pl.loop(0, nwin)
        def _(j):
            off = pl.multiple_of(base + j * SC_WINDOW, SC_WINDOW)
            pltpu.sync_copy(idx_hbm.at[pl.ds(off, SC_WINDOW)], idx_v)
            pltpu.async_copy(table_hbm.at[idx_v], rows_v, sem).wait()
            pltpu.sync_copy(rows_v, out_hbm.at[pl.ds(off, SC_WINDOW)])

    return gather(table, idx)


def _expert_kernel(be_ref, nu_ref, x_ref, wg_ref, wu_ref, wd_ref, o_ref, wg_s, wu_s, wd_s):
    i = pl.program_id(0)
    used = i < nu_ref[0]
    changed = jnp.logical_or(i == 0, be_ref[i] != be_ref[jnp.maximum(i - 1, 0)])

    @pl.when(jnp.logical_and(used, changed))
    def _():
        wg_s[...] = wg_ref[...].astype(BF16)
        wu_s[...] = wu_ref[...].astype(BF16)
        wd_s[...] = wd_ref[...].astype(BF16)

    @pl.when(used)
    def _():
        x = _unpack_bf16_pairs(x_ref[...]).astype(BF16)
        gate = jnp.dot(x, wg_s[...], preferred_element_type=F32)
        up = jnp.dot(x, wu_s[...], preferred_element_type=F32)
        h = (gate * jax.nn.sigmoid(gate) * up).astype(BF16)
        o_ref[...] = _pack_bf16_pairs(jnp.dot(h, wd_s[...], preferred_element_type=F32))

    @pl.when(jnp.logical_not(used))
    def _():
        o_ref[...] = jnp.zeros_like(o_ref)


def _experts(blk_e, n_used, xb, w_gate, w_up, w_down):
    rows = xb.shape[0]
    nblocks = rows // MOE_ROWS
    grid_spec = pltpu.PrefetchScalarGridSpec(
        num_scalar_prefetch=2,
        grid=(nblocks,),
        in_specs=[
            pl.BlockSpec((MOE_ROWS, D_MODEL // 2), lambda i, be, nu: (i, 0)),
            pl.BlockSpec((None, D_MODEL, D_EXPERT), lambda i, be, nu: (be[i], 0, 0)),
            pl.BlockSpec((None, D_MODEL, D_EXPERT), lambda i, be, nu: (be[i], 0, 0)),
            pl.BlockSpec((None, D_EXPERT, D_MODEL), lambda i, be, nu: (be[i], 0, 0)),
        ],
        out_specs=pl.BlockSpec((MOE_ROWS, D_MODEL // 2), lambda i, be, nu: (i, 0)),
        scratch_shapes=[pltpu.VMEM((D_MODEL, D_EXPERT), BF16), pltpu.VMEM((D_MODEL, D_EXPERT), BF16),
                        pltpu.VMEM((D_EXPERT, D_MODEL), BF16)],
    )
    return pl.pallas_call(
        _expert_kernel,
        grid_spec=grid_spec,
        out_shape=jax.ShapeDtypeStruct((rows, D_MODEL // 2), jnp.int32),
        compiler_params=_cparams(("arbitrary",)),
        name="experts",
    )(blk_e, n_used, xb, w_gate, w_up, w_down)


def _combine_kernel(x1_ref, y1_ref, y2_ref, route_ref, g_ref, outp_ref, outs_ref, *, prompt_tiles):
    r = route_ref[...]
    x = (x1_ref[...] + r[:, 2:3] * _unpack_bf16_pairs(y1_ref[...])
         + r[:, 3:4] * _unpack_bf16_pairs(y2_ref[...]))
    ms = jnp.mean(x * x, axis=-1, keepdims=True)
    y = x * lax.rsqrt(ms + EPS) * g_ref[...]
    i = pl.program_id(0)

    @pl.when(i < prompt_tiles)
    def _():
        outp_ref[...] = y

    @pl.when(i >= prompt_tiles)
    def _():
        outs_ref[...] = y


def _combine_norm(x1, ygath, route, gamma, tp):
    t = x1.shape[0]
    tm = 512
    nt, npt = t // tm, tp // tm
    return pl.pallas_call(
        functools.partial(_combine_kernel, prompt_tiles=npt),
        grid=(nt,),
        in_specs=[
            pl.BlockSpec((tm, D_MODEL), lambda i: (i, 0)),
            pl.BlockSpec((tm, D_MODEL // 2), lambda i: (i, 0)),
            pl.BlockSpec((tm, D_MODEL // 2), lambda i: (i + nt, 0)),
            pl.BlockSpec((tm, LANES), lambda i: (i, 0)),
            pl.BlockSpec((1, D_MODEL), lambda i: (0, 0)),
        ],
        out_specs=[
            pl.BlockSpec((tm, D_MODEL), lambda i: (jnp.minimum(i, npt - 1), 0)),
            pl.BlockSpec((tm, D_MODEL), lambda i: (jnp.maximum(i - npt, 0), 0)),
        ],
        out_shape=[jax.ShapeDtypeStruct((tp, D_MODEL), F32), jax.ShapeDtypeStruct((t - tp, D_MODEL), F32)],
        compiler_params=_cparams(("arbitrary",)),
        name="combine_norm",
    )(x1, ygath, ygath, route, gamma)


def _toeplitz(f, n):
    p = f.shape[-1]
    u = jnp.roll(f, n - 1, axis=-1)
    flat = jnp.tile(u, (1,) * (f.ndim - 1) + (n + 1,))[..., :n * (p + 1)]
    a = flat.reshape(f.shape[:-1] + (n, p + 1))[..., :p]
    return a[..., ::-1, :]


def _bias_a_prompt(table_a):
    c = np.arange(2 * QB)
    valid = c <= QB
    idx = np.stack([_t5_bucket_np(d * np.clip(QB - c, 0, QB)) for d in DILATIONS])
    f = jnp.where(valid[None, None], jnp.transpose(table_a[idx], (0, 2, 1)), NEG)
    return _toeplitz(f, QB).reshape(3, 4, 2 * QB, 2 * QB)


def _bias_b_prompt(table_b):
    c = np.arange(2 * QB)
    valid = (c >= 1) & (c <= QB)
    f = jnp.where(valid[None], table_b[_t5_bucket_np(np.clip(QB - c, 0, QB))].T, NEG)
    f = jnp.transpose(f.reshape(KV_B, G_B, 2 * QB), (1, 0, 2))
    return _toeplitz(f, QB).reshape(H_B * QB, 2 * QB)


def _sample_bias_rows(v, span, t):
    rows = []
    for i in range(t):
        pad = jnp.full((v.shape[0], LANES - i - 1), NEG, F32)
        rows.append(jnp.concatenate([v[:, i + 1:i + 1 + span][:, ::-1], v[:, :i + 1][:, ::-1], pad], axis=1))
    return jnp.stack(rows, axis=1).reshape(v.shape[0] * t, span + LANES)


def _bias_a_sample(table_a, t):
    dist = np.arange(WIN_A + t)
    count = np.zeros(dist.shape, np.int32)
    for w, d in zip(WINDOWS, DILATIONS):
        count += ((dist % d == 0) & (dist <= w)).astype(np.int32)
    logc = np.log(np.maximum(count, 1)).astype(np.float32)
    v = jnp.where((count > 0)[None], table_a[_t5_bucket_np(dist)].T + logc[None], NEG)
    return _sample_bias_rows(v, WIN_A, t)


def _bias_b_sample(table_b, t):
    dist = np.arange(WIN_B + t)
    v = jnp.where((dist < WIN_B)[None], table_b[_t5_bucket_np(dist)].T, NEG)
    return _sample_bias_rows(v, WIN_B, t)


def _dispatch(route):
    t = route.shape[0]
    a = t * TOP_K
    eid = route[:, :TOP_K].astype(jnp.int32).reshape(a)
    onehot = (eid[:, None] == jnp.arange(N_EXPERTS, dtype=jnp.int32)[None, :]).astype(jnp.int32)
    csum = jnp.cumsum(onehot, axis=0)
    counts = csum[-1]
    rank = jnp.sum(csum * onehot, axis=1) - 1
    padded = (counts + MOE_ROWS - 1) // MOE_ROWS * MOE_ROWS
    pend = jnp.cumsum(padded)
    pstart = pend - padded
    dest = pstart[eid] + rank
    nblocks = -(-a // MOE_ROWS) + N_EXPERTS
    row_tok = jnp.zeros((nblocks * MOE_ROWS,), jnp.int32).at[dest].set(jnp.arange(a, dtype=jnp.int32) // TOP_K)
    blk_start = jnp.arange(nblocks, dtype=jnp.int32) * MOE_ROWS
    blk_e = jnp.minimum(jnp.sum((pend[None, :] <= blk_start[:, None]).astype(jnp.int32), axis=1),
                        N_EXPERTS - 1).astype(jnp.int32)
    n_used = (pend[-1] // MOE_ROWS).astype(jnp.int32).reshape(1)
    pos = jnp.transpose(dest.reshape(t, TOP_K)).reshape(a)
    return row_tok, blk_e, n_used, pos


def kernel(x_prompt, x_sample, cache_a_k, cache_a_v, cache_b_k, cache_b_v, rel_bias_table, attn_norm, w_in,
           w_out, attn_sinks, ffn_norm, w_router_group, b_router_group, w_router_expert, b_router_expert,
           w_gate, w_up, w_down, final_norm):
    s = x_prompt.shape[1]
    ns, ts = x_sample.shape[0], x_sample.shape[1]
    table_a = rel_bias_table[:, :H_A]
    table_b = rel_bias_table[:, H_A:]

    w = w_in[0]
    wqa, wka, wva, wqb, wkb, wvb = (w[:, 0:512], w[:, 512:1024], w[:, 1024:1536], w[:, 1536:2048],
                                    w[:, 2048:2176], w[:, 2176:2304])
    wqb = jnp.transpose(wqb.reshape(D_MODEL, KV_B, G_B, HEAD_DIM), (0, 2, 1, 3)).reshape(D_MODEL, 512)
    wp = jnp.concatenate([wka, wva, wqa, wqb, wkb, wvb], axis=1).astype(BF16)
    cscale = jnp.concatenate([jnp.ones((1, 1024), F32), jnp.full((1, 1024), SCALE, F32),
                              jnp.ones((1, 256), F32)], axis=1)
    wo = w_out[0]
    wo_b = jnp.transpose(wo[512:].reshape(KV_B, G_B, HEAD_DIM, D_MODEL), (1, 0, 2, 3)).reshape(512, D_MODEL)
    wo_p = jnp.concatenate([wo[:512], wo_b], axis=0).astype(BF16)
    wr = jnp.concatenate([w_router_group[0],
                          jnp.transpose(w_router_expert[0], (1, 0, 2)).reshape(D_MODEL, N_EXPERTS),
                          jnp.zeros((D_MODEL, LANES - N_GROUPS - N_EXPERTS), F32)], axis=1)
    br = jnp.concatenate([b_router_group[0], b_router_expert[0].reshape(N_EXPERTS),
                          jnp.zeros((LANES - N_GROUPS - N_EXPERTS,), F32)]).reshape(1, LANES)
    sinks_gk = jnp.transpose(attn_sinks[0].reshape(KV_B, G_B), (1, 0)).reshape(H_B)
    sink_rows_p = jnp.repeat(sinks_gk, QB).reshape(H_B * QB, 1)
    sink_rows_s = jnp.repeat(attn_sinks[0], ts).reshape(H_B * ts, 1)
    emat = (jnp.arange(LANES)[:, None] == (jnp.arange(A_WIDTH)[None, :] // HEAD_DIM)).astype(F32)
    attn_g = attn_norm[0].reshape(1, D_MODEL)
    ffn_g = ffn_norm[0].reshape(1, D_MODEL)

    xp = x_prompt.reshape(s, D_MODEL)
    aperm, qb_p, kvb_p, akv32, bkv32 = _proj_prompt(xp, attn_g, wp, cscale)
    a4 = _attn_a_prompt(aperm, _bias_a_prompt(table_a), emat)
    ob_p = _attn_b_prompt(qb_p, kvb_p, _bias_b_prompt(table_b), sink_rows_p)

    xs = x_sample.reshape(ns * ts, D_MODEL)
    q_s, kv_s = _proj_sample(xs, attn_g, wp, cscale)
    akt = jnp.transpose(cache_a_k[0], (0, 2, 3, 1)).reshape(ns, A_WIDTH, WIN_A)
    avt = jnp.transpose(cache_a_v[0], (0, 2, 3, 1)).reshape(ns, A_WIDTH, WIN_A)
    bkt = jnp.transpose(cache_b_k[0], (0, 2, 3, 1)).reshape(ns, LANES, WIN_B)
    bvt = jnp.transpose(cache_b_v[0], (0, 2, 3, 1)).reshape(ns, LANES, WIN_B)
    oa_s, ob_s = _attn_sample(q_s.reshape(ns, ts, 1024), kv_s.reshape(ns, ts, 1280), akt, avt, bkt, bvt,
                              _bias_a_sample(table_a, ts), _bias_b_sample(table_b, ts), sink_rows_s)
    a4_s = jnp.transpose(oa_s.reshape(ns * ts, 4, LANES), (1, 0, 2))

    x1, xn, route = _out_router(xp, a4, ob_p, xs, a4_s, ob_s.reshape(ns * ts, 512), wo_p, ffn_g, wr, br)
    row_tok, blk_e, n_used, pos = _dispatch(route)
    yb = _experts(blk_e, n_used, _sc_gather_rows(xn, row_tok), w_gate[0], w_up[0], w_down[0])
    y_p, y_s = _combine_norm(x1, _sc_gather_rows(yb, pos), route, final_norm.reshape(1, D_MODEL), s)

    y_prompt = y_p.reshape(1, s, D_MODEL)
    y_sample = y_s.reshape(ns, ts, D_MODEL)
    keep_a, keep_b = min(WIN_A, s), min(WIN_B, s)
    pak = akv32[s - keep_a:, :512].reshape(1, 1, keep_a, H_A, HEAD_DIM)
    pav = akv32[s - keep_a:, 512:].reshape(1, 1, keep_a, H_A, HEAD_DIM)
    pbk = bkv32[s - keep_b:, :128].reshape(1, 1, keep_b, KV_B, HEAD_DIM)
    pbv = bkv32[s - keep_b:, 128:].reshape(1, 1, keep_b, KV_B, HEAD_DIM)
    sak = kv_s[:, 0:512].reshape(1, ns, ts, H_A, HEAD_DIM)
    sav = kv_s[:, 512:1024].reshape(1, ns, ts, H_A, HEAD_DIM)
    sbk = kv_s[:, 1024:1152].reshape(1, ns, ts, KV_B, HEAD_DIM)
    sbv = kv_s[:, 1152:1280].reshape(1, ns, ts, KV_B, HEAD_DIM)
    return (y_prompt, y_sample, pak, pav, pbk, pbv, sak, sav, sbk, sbv)
```

```python
import functools
import math

import jax
import jax.numpy as jnp
import numpy as np
from jax import lax
from jax.experimental import pallas as pl
from jax.experimental.pallas import tpu as pltpu
from jax.experimental.pallas import tpu_sc as plsc

D_MODEL = 1024
HEAD_DIM = 64
H_A = 8
H_B = 8
KV_B = 2
G_B = 4
DILATIONS = (1, 4, 16)
WINDOWS = (128, 512, 2048)
WIN_A = 2048
WIN_B = 128
NUM_BUCKETS = 32
MAX_DISTANCE = 2048
N_GROUPS = 4
EXPERTS_PER_GROUP = 8
N_EXPERTS = 32
TOP_K = 2
D_EXPERT = 512
EPS = 1e-5
SCALE = HEAD_DIM ** -0.5
PAST_LEN = 16384

LANES = 128
SPAN = 2048
QB = 128
NCHUNK = 9
A_WIDTH = H_A * HEAD_DIM
MOE_ROWS = 256
SC_CORES = 2
SC_SUBCORES = 16
SC_WORKERS = SC_CORES * SC_SUBCORES
SC_WINDOW = 64
NEG = -1e30
LOG2E = math.log2(math.e)
B_STEP = 512
VMEM_LIMIT = 56 * 1024 * 1024

F32 = jnp.float32
BF16 = jnp.bfloat16


def _t5_bucket_np(dist):
    dist = np.asarray(dist, np.int64)
    max_exact = NUM_BUCKETS // 2
    d = np.maximum(dist, 1).astype(np.float32)
    ratio = np.log(d / np.float32(max_exact)) / np.float32(math.log(MAX_DISTANCE / max_exact))
    large = max_exact + (ratio * np.float32(NUM_BUCKETS - max_exact)).astype(np.int32)
    large = np.minimum(large, NUM_BUCKETS - 1)
    return np.where(dist < max_exact, dist, large).astype(np.int32)


def _cparams(sem, vmem=VMEM_LIMIT):
    return pltpu.CompilerParams(dimension_semantics=sem, vmem_limit_bytes=vmem)


def _proj_prompt_kernel(x_ref, g_ref, w_ref, cs_ref, aperm_ref, qb_ref, kvb_ref, akv_ref, bkv_ref,
                        h_scr, p_scr):
    n = pl.program_id(1)

    @pl.when(n == 0)
    def _():
        x = x_ref[...]
        ms = jnp.mean(x * x, axis=-1, keepdims=True)
        h_scr[...] = (x * lax.rsqrt(ms + EPS) * g_ref[...]).astype(BF16)

    p = jnp.dot(h_scr[...], w_ref[...], preferred_element_type=F32) * cs_ref[...]

    @pl.when(n < 6)
    def _():
        aperm_ref[0] = p.astype(BF16)
        p_scr[0] = p[:, :LANES]
        p_scr[1] = p[:, LANES:]
        for gi, dil in ((1, 4), (2, 16)):
            rows = SPAN // dil
            for r in range(dil):
                t = jnp.concatenate([p_scr[0, pl.ds(r, rows, stride=dil), :],
                                     p_scr[1, pl.ds(r, rows, stride=dil), :]], axis=1)
                aperm_ref[gi, r * rows:(r + 1) * rows, :] = t.astype(BF16)

    @pl.when(n < 4)
    def _():
        akv_ref[...] = p

    @pl.when(jnp.logical_or(n == 6, n == 7))
    def _():
        qb_ref[...] = p.astype(BF16)

    @pl.when(n == 8)
    def _():
        kvb_ref[...] = p.astype(BF16)
        bkv_ref[...] = p


def _proj_prompt(x, gamma, w, cscale):
    s = x.shape[0]
    nspan = s // SPAN
    return pl.pallas_call(
        _proj_prompt_kernel,
        grid=(nspan, NCHUNK),
        in_specs=[
            pl.BlockSpec((SPAN, D_MODEL), lambda b, n: (b, 0)),
            pl.BlockSpec((1, D_MODEL), lambda b, n: (0, 0)),
            pl.BlockSpec((D_MODEL, 256), lambda b, n: (0, n)),
            pl.BlockSpec((1, 256), lambda b, n: (0, n)),
        ],
        out_specs=[
            pl.BlockSpec((3, SPAN, 256), lambda b, n: (0, b, jnp.minimum(n, 5))),
            pl.BlockSpec((SPAN, 256), lambda b, n: (b, jnp.clip(n - 6, 0, 1))),
            pl.BlockSpec((SPAN, 256), lambda b, n: (b, 0)),
            pl.BlockSpec((SPAN, 256), lambda b, n: (b, jnp.minimum(n, 3))),
            pl.BlockSpec((SPAN, 256), lambda b, n: (b, 0)),
        ],
        out_shape=[
            jax.ShapeDtypeStruct((3, s, 3 * A_WIDTH), BF16),
            jax.ShapeDtypeStruct((s, 512), BF16),
            jax.ShapeDtypeStruct((s, 256), BF16),
            jax.ShapeDtypeStruct((s, 1024), F32),
            jax.ShapeDtypeStruct((s, 256), F32),
        ],
        scratch_shapes=[pltpu.VMEM((SPAN, D_MODEL), BF16), pltpu.VMEM((2, SPAN, LANES), F32)],
        compiler_params=_cparams(("arbitrary", "arbitrary")),
        name="proj_prompt",
    )(x, gamma, w, cscale)


def _proj_sample_kernel(x_ref, g_ref, w_ref, cs_ref, q_ref, kv_ref):
    x = x_ref[...]
    ms = jnp.mean(x * x, axis=-1, keepdims=True)
    h = (x * lax.rsqrt(ms + EPS) * g_ref[...]).astype(BF16)
    p = jnp.dot(h, w_ref[...], preferred_element_type=F32) * cs_ref[...]
    kv_ref[:, :1024] = p[:, :1024]
    kv_ref[:, 1024:] = p[:, 2048:]
    q_ref[...] = p[:, 1024:2048]


def _proj_sample(x, gamma, w, cscale):
    t = x.shape[0]
    tm = 512
    return pl.pallas_call(
        _proj_sample_kernel,
        grid=(t // tm,),
        in_specs=[
            pl.BlockSpec((tm, D_MODEL), lambda i: (i, 0)),
            pl.BlockSpec((1, D_MODEL), lambda i: (0, 0)),
            pl.BlockSpec((D_MODEL, 2304), lambda i: (0, 0)),
            pl.BlockSpec((1, 2304), lambda i: (0, 0)),
        ],
        out_specs=[
            pl.BlockSpec((tm, 1024), lambda i: (i, 0)),
            pl.BlockSpec((tm, 1280), lambda i: (i, 0)),
        ],
        out_shape=[
            jax.ShapeDtypeStruct((t, 1024), F32),
            jax.ShapeDtypeStruct((t, 1280), F32),
        ],
        compiler_params=_cparams(("arbitrary",)),
        name="proj_sample",
    )(x, gamma, w, cscale)


def _spread_heads(w, e3_ref):
    hi = w.astype(BF16)
    r1 = w - hi.astype(F32)
    mid = r1.astype(BF16)
    low = (r1 - mid.astype(F32)).astype(BF16)
    return jnp.dot(jnp.concatenate([hi, mid, low], axis=1), e3_ref[...], preferred_element_type=F32)


def _attn_a_kernel(q_ref, kvc_ref, kvp_ref, bias_ref, e_ref, out_ref, o_scr, st_scr):
    b = pl.program_id(0)
    g = pl.program_id(1)
    nblk = jnp.where(g == 0, 16, jnp.where(g == 1, 4, 1))
    lane = lax.broadcasted_iota(jnp.int32, (QB, LANES), 1)
    lo = lane < HEAD_DIM

    def block(cb, carry):
        first = lax.rem(cb, nblk) == 0
        row0 = pl.multiple_of(cb * QB, QB)
        prow_c = pl.multiple_of(jnp.maximum(cb - 1, 0) * QB, QB)
        prow_p = pl.multiple_of(jnp.where(first, cb + nblk - 1, 0) * QB, QB)
        variant = jnp.logical_and(first, b == 0).astype(jnp.int32)
        st_tile = jnp.zeros((QB, LANES), F32)
        for hp in range(4):
            ks = slice(hp * LANES, (hp + 1) * LANES)
            vs = slice(A_WIDTH + hp * LANES, A_WIDTH + (hp + 1) * LANES)
            q2 = q_ref[pl.ds(row0, QB), ks]
            zero = jnp.zeros_like(q2)
            qq = jnp.concatenate([jnp.where(lo, q2, zero), jnp.where(lo, zero, q2)], axis=0)
            kc = kvc_ref[pl.ds(row0, QB), ks]
            vc = kvc_ref[pl.ds(row0, QB), vs]
            kp = jnp.where(first, kvp_ref[pl.ds(prow_p, QB), ks], kvc_ref[pl.ds(prow_c, QB), ks])
            vp = jnp.where(first, kvp_ref[pl.ds(prow_p, QB), vs], kvc_ref[pl.ds(prow_c, QB), vs])
            kk = jnp.concatenate([kp, kc], axis=0)
            vv = jnp.concatenate([vp, vc], axis=0)
            s = lax.dot_general(qq, kk, (((1,), (1,)), ((), ())), preferred_element_type=F32)
            s = s + bias_ref[variant, hp]
            m = jnp.max(s, axis=-1, keepdims=True)
            p = jnp.exp2(s - m)
            l = jnp.sum(p, axis=-1, keepdims=True)
            o = jnp.dot(p.astype(BF16), vv, preferred_element_type=F32) / l
            o_scr[g, hp, pl.ds(row0, QB), :] = jnp.where(lo, o[:QB], o[QB:])
            lse = m + jnp.log2(l)
            st_tile = jnp.where(lane == 2 * hp, lse[:QB], st_tile)
            st_tile = jnp.where(lane == 2 * hp + 1, lse[QB:], st_tile)
        st_scr[g, pl.ds(row0, QB), :] = st_tile
        return carry

    lax.fori_loop(0, SPAN // QB, block, 0)

    @pl.when(g == 2)
    def _():
        def merge(c, carry):
            r2 = lax.rem(c, 4) * (SPAN // 4) + c // 4
            r3 = pl.multiple_of(c * QB, QB)
            l1 = st_scr[0, pl.ds(c, QB, stride=16), :]
            l2 = st_scr[1, pl.ds(r2, QB, stride=4), :]
            l3 = st_scr[2, pl.ds(r3, QB), :]
            mx = jnp.maximum(jnp.maximum(l1, l2), l3)
            w1 = jnp.exp2(l1 - mx)
            w2 = jnp.exp2(l2 - mx)
            w3 = jnp.exp2(l3 - mx)
            tot = w1 + w2 + w3
            a1 = _spread_heads(w1 / tot, e_ref)
            a2 = _spread_heads(w2 / tot, e_ref)
            a3 = _spread_heads(w3 / tot, e_ref)
            for hp in range(4):
                sl = slice(hp * LANES, (hp + 1) * LANES)
                o1 = o_scr[0, hp, pl.ds(c, QB, stride=16), :]
                o2 = o_scr[1, hp, pl.ds(r2, QB, stride=4), :]
                o3 = o_scr[2, hp, pl.ds(r3, QB), :]
                out_ref[hp, pl.ds(c, QB, stride=16), :] = a1[:, sl] * o1 + a2[:, sl] * o2 + a3[:, sl] * o3
            return carry

        lax.fori_loop(0, 16, merge, 0)


def _attn_a_prompt(aperm, bias_a, emat):
    s = aperm.shape[1]
    nspan = s // SPAN
    return pl.pallas_call(
        _attn_a_kernel,
        grid=(nspan, 3),
        in_specs=[
            pl.BlockSpec((None, SPAN, A_WIDTH), lambda b, g: (g, b, 2)),
            pl.BlockSpec((None, SPAN, 2 * A_WIDTH), lambda b, g: (g, b, 0)),
            pl.BlockSpec((None, SPAN, 2 * A_WIDTH), lambda b, g: (g, jnp.maximum(b - 1, 0), 0)),
            pl.BlockSpec((None, 2, 4, 2 * QB, 2 * QB), lambda b, g: (g, 0, 0, 0, 0)),
            pl.BlockSpec((3 * LANES, A_WIDTH), lambda b, g: (0, 0)),
        ],
        out_specs=pl.BlockSpec((4, SPAN, LANES), lambda b, g: (0, b, 0)),
        out_shape=jax.ShapeDtypeStruct((4, s, LANES), F32),
        scratch_shapes=[pltpu.VMEM((3, 4, SPAN, LANES), F32), pltpu.VMEM((3, SPAN, LANES), F32)],
        compiler_params=_cparams(("arbitrary", "arbitrary")),
        name="attn_a_prompt",
    )(aperm, aperm, aperm, bias_a, emat)


def _attn_b_kernel(q_ref, kvc_ref, kvp_ref, bias_ref, sink_ref, out_ref):
    i = pl.program_id(0)
    lane = lax.broadcasted_iota(jnp.int32, (QB, LANES), 1)
    lo = lane < HEAD_DIM
    variant = (i == 0).astype(jnp.int32)
    for j in range(B_STEP // QB):
        rows = slice(j * QB, (j + 1) * QB)
        if j == 0:
            kp, vp = kvp_ref[:, :LANES], kvp_ref[:, LANES:]
        else:
            kp, vp = kvc_ref[(j - 1) * QB:j * QB, :LANES], kvc_ref[(j - 1) * QB:j * QB, LANES:]
        kk = jnp.concatenate([kp, kvc_ref[rows, :LANES]], axis=0)
        vv = jnp.concatenate([vp, kvc_ref[rows, LANES:]], axis=0)
        for g in range(G_B):
            q2 = q_ref[rows, g * LANES:(g + 1) * LANES]
            zero = jnp.zeros_like(q2)
            qq = jnp.concatenate([jnp.where(lo, q2, zero), jnp.where(lo, zero, q2)], axis=0)
            s = lax.dot_general(qq, kk, (((1,), (1,)), ((), ())), preferred_element_type=F32)
            s = s + (bias_ref[variant, g] if j == 0 else bias_ref[0, g])
            sink = sink_ref[g]
            m = jnp.maximum(jnp.max(s, axis=-1, keepdims=True), sink)
            p = jnp.exp2(s - m)
            den = jnp.sum(p, axis=-1, keepdims=True) + jnp.exp2(sink - m)
            o = jnp.dot(p.astype(BF16), vv, preferred_element_type=F32) / den
            out_ref[rows, g * LANES:(g + 1) * LANES] = jnp.where(lo, o[:QB], o[QB:]).astype(BF16)


def _attn_b_prompt(qb, kvb, bias_b, sink_rows):
    s = qb.shape[0]
    per = B_STEP // QB
    return pl.pallas_call(
        _attn_b_kernel,
        grid=(s // B_STEP,),
        in_specs=[
            pl.BlockSpec((B_STEP, 512), lambda i: (i, 0)),
            pl.BlockSpec((B_STEP, 256), lambda i: (i, 0)),
            pl.BlockSpec((QB, 256), lambda i: (jnp.maximum(i * per - 1, 0), 0)),
            pl.BlockSpec((2, G_B, 2 * QB, 2 * QB), lambda i: (0, 0, 0, 0)),
            pl.BlockSpec((G_B, 2 * QB, 1), lambda i: (0, 0, 0)),
        ],
        out_specs=pl.BlockSpec((B_STEP, 512), lambda i: (i, 0)),
        out_shape=jax.ShapeDtypeStruct((s, 512), BF16),
        compiler_params=_cparams(("arbitrary",)),
        name="attn_b_prompt",
    )(qb, kvb, kvb, bias_b, sink_rows)


def _attn_sample_kernel(q_ref, kvn_ref, akt_ref, avt_ref, bkt_ref, bvt_ref, cba_ref, cbb_ref, sink_ref,
                        oa_ref, ob_ref):
    t = q_ref.shape[0]
    q = q_ref[...]
    kvn = kvn_ref[...]
    kvn_p = jnp.concatenate([kvn, jnp.zeros((LANES - t, kvn.shape[1]), F32)], axis=0).astype(BF16)
    lane_a = lax.broadcasted_iota(jnp.int32, (t, A_WIDTH), 1) // HEAD_DIM

    qa = q[:, :A_WIDTH]
    qbd = jnp.concatenate([jnp.where(lane_a == h, qa, 0.0) for h in range(H_A)], axis=0).astype(BF16)
    s_c = jnp.dot(qbd, akt_ref[...].astype(BF16), preferred_element_type=F32)
    s_n = lax.dot_general(qbd, kvn_p[:, :A_WIDTH], (((1,), (1,)), ((), ())), preferred_element_type=F32)
    s = jnp.concatenate([s_c, s_n], axis=1) + cba_ref[...]
    m = jnp.max(s, axis=-1, keepdims=True)
    p = jnp.exp2(s - m)
    l = jnp.sum(p, axis=-1, keepdims=True)
    o_n = jnp.dot(p[:, WIN_A:].astype(BF16), kvn_p[:, A_WIDTH:2 * A_WIDTH], preferred_element_type=F32)
    pc = jnp.concatenate([p[:, :WIN_A], jnp.zeros((LANES - H_A * t, WIN_A), F32)], axis=0).astype(BF16)
    o_t = lax.dot_general(avt_ref[...].astype(BF16), pc, (((1,), (1,)), ((), ())),
                          preferred_element_type=F32)
    o_all = o_t.T[:H_A * t] + o_n
    o_sel = jnp.zeros((t, A_WIDTH), F32)
    l_b = jnp.ones((t, A_WIDTH), F32)
    for h in range(H_A):
        sel = lane_a == h
        o_sel = jnp.where(sel, o_all[h * t:(h + 1) * t], o_sel)
        l_b = jnp.where(sel, l[h * t:(h + 1) * t], l_b)
    oa_ref[...] = o_sel / l_b

    lane_b = lax.broadcasted_iota(jnp.int32, (G_B * t, LANES), 1)
    lo = lane_b < HEAD_DIM
    qb2 = jnp.concatenate([q[:, A_WIDTH + g * LANES:A_WIDTH + (g + 1) * LANES] for g in range(G_B)], axis=0)
    qm = jnp.concatenate([jnp.where(lo, qb2, 0.0), jnp.where(lo, 0.0, qb2)], axis=0).astype(BF16)
    kb_n = kvn_p[:, 2 * A_WIDTH:2 * A_WIDTH + LANES]
    vb_n = kvn_p[:, 2 * A_WIDTH + LANES:]
    sb_c = jnp.dot(qm, bkt_ref[...].astype(BF16), preferred_element_type=F32)
    sb_n = lax.dot_general(qm, kb_n, (((1,), (1,)), ((), ())), preferred_element_type=F32)
    sb = jnp.concatenate([sb_c, sb_n], axis=1) + cbb_ref[...]
    sink = sink_ref[...]
    mb = jnp.maximum(jnp.max(sb, axis=-1, keepdims=True), sink)
    pbb = jnp.exp2(sb - mb)
    den = jnp.sum(pbb, axis=-1, keepdims=True) + jnp.exp2(sink - mb)
    pbb = pbb.astype(BF16)
    ob = lax.dot_general(pbb[:, :WIN_B], bvt_ref[...].astype(BF16), (((1,), (1,)), ((), ())),
                         preferred_element_type=F32)
    ob = (ob + jnp.dot(pbb[:, WIN_B:], vb_n, preferred_element_type=F32)) / den
    half = G_B * t
    lo8 = lo[:t]
    for g in range(G_B):
        ob_ref[:, g * LANES:(g + 1) * LANES] = jnp.where(
            lo8, ob[g * t:(g + 1) * t], ob[half + g * t:half + (g + 1) * t])


def _attn_sample(q3, kvn3, akt, avt, bkt, bvt, cbias_a, cbias_b, sink_rows):
    ns, t = q3.shape[0], q3.shape[1]
    return pl.pallas_call(
        _attn_sample_kernel,
        grid=(ns,),
        in_specs=[
            pl.BlockSpec((None, t, 1024), lambda n: (n, 0, 0)),
            pl.BlockSpec((None, t, 1280), lambda n: (n, 0, 0)),
            pl.BlockSpec((None, A_WIDTH, WIN_A), lambda n: (n, 0, 0)),
            pl.BlockSpec((None, A_WIDTH, WIN_A), lambda n: (n, 0, 0)),
            pl.BlockSpec((None, LANES, WIN_B), lambda n: (n, 0, 0)),
            pl.BlockSpec((None, LANES, WIN_B), lambda n: (n, 0, 0)),
            pl.BlockSpec((H_A * t, WIN_A + LANES), lambda n: (0, 0)),
            pl.BlockSpec((H_B * t, WIN_B + LANES), lambda n: (0, 0)),
            pl.BlockSpec((H_B * t, 1), lambda n: (0, 0)),
        ],
        out_specs=[
            pl.BlockSpec((None, t, 512), lambda n: (n, 0, 0)),
            pl.BlockSpec((None, t, 512), lambda n: (n, 0, 0)),
        ],
        out_shape=[jax.ShapeDtypeStruct((ns, t, 512), F32), jax.ShapeDtypeStruct((ns, t, 512), F32)],
        compiler_params=_cparams(("arbitrary",)),
        name="attn_sample",
    )(q3, kvn3, akt, avt, bkt, bvt, cbias_a, cbias_b, sink_rows)


def _route(logits):
    lane = lax.broadcasted_iota(jnp.int32, logits.shape, 1).astype(F32)
    big = jnp.float32(1 << 20)
    ninf = jnp.float32(-jnp.inf)
    gmask = lane < N_GROUPS
    lg = jnp.where(gmask, logits, ninf)
    gmax = jnp.max(lg, axis=-1, keepdims=True)
    grp = jnp.min(jnp.where(lg == gmax, lane, big), axis=-1, keepdims=True)
    pg_top = 1.0 / jnp.sum(jnp.exp(lg - gmax), axis=-1, keepdims=True)
    e0 = N_GROUPS + grp * EXPERTS_PER_GROUP
    emask = jnp.logical_and(lane >= e0, lane < e0 + EXPERTS_PER_GROUP)
    le = jnp.where(emask, logits, ninf)
    emax = jnp.max(le, axis=-1, keepdims=True)
    esum = jnp.sum(jnp.exp(le - emax), axis=-1, keepdims=True)
    i1 = jnp.min(jnp.where(le == emax, lane, big), axis=-1, keepdims=True)
    le2 = jnp.where(lane == i1, ninf, le)
    e2max = jnp.max(le2, axis=-1, keepdims=True)
    i2 = jnp.min(jnp.where(le2 == e2max, lane, big), axis=-1, keepdims=True)
    p1 = 1.0 / esum
    p2 = jnp.exp(e2max - emax) / esum
    g1 = pg_top * p1 / (p1 + p2)
    g2 = pg_top * p2 / (p1 + p2)
    out = jnp.where(lane == 0, i1 - N_GROUPS, 0.0)
    out = jnp.where(lane == 1, i2 - N_GROUPS, out)
    out = jnp.where(lane == 2, g1, out)
    out = jnp.where(lane == 3, g2, out)
    return out


def _pack_bf16_pairs(x):
    half = x.shape[1] // 2

    def rne(v):
        bits = lax.bitcast_convert_type(v, jnp.int32)
        return bits + 0x7FFF + (lax.shift_right_logical(bits, 16) & 1)

    lo = lax.shift_right_logical(rne(x[:, :half]), 16)
    hi = rne(x[:, half:]) & jnp.int32(-65536)
    return lo | hi


def _unpack_bf16_pairs(w):
    lo = lax.bitcast_convert_type(lax.shift_left(w, 16), F32)
    hi = lax.bitcast_convert_type(w & jnp.int32(-65536), F32)
    return jnp.concatenate([lo, hi], axis=1)


def _out_router_kernel(xp_ref, ap_ref, bp_ref, xs_ref, as_ref, bs_ref, wo_ref, g_ref, wr_ref, br_ref,
                       x1_ref, xn_ref, route_ref, *, prompt_tiles):
    def body(x_ref, a_ref, b_ref):
        mix = jnp.concatenate([a_ref[0], a_ref[1], a_ref[2], a_ref[3]], axis=1).astype(BF16)
        mix = jnp.concatenate([mix, b_ref[...].astype(BF16)], axis=1)
        x1 = x_ref[...] + jnp.dot(mix, wo_ref[...], preferred_element_type=F32)
        x1_ref[...] = x1
        ms = jnp.mean(x1 * x1, axis=-1, keepdims=True)
        xn = x1 * lax.rsqrt(ms + EPS) * g_ref[...]
        xn_ref[...] = _pack_bf16_pairs(xn)
        xh = xn.astype(BF16)
        xl = (xn - xh.astype(F32)).astype(BF16)
        logits = jnp.dot(jnp.concatenate([xh, xl, xh], axis=1), wr_ref[...], preferred_element_type=F32)
        route_ref[...] = _route(logits + br_ref[...])

    i = pl.program_id(0)

    @pl.when(i < prompt_tiles)
    def _():
        body(xp_ref, ap_ref, bp_ref)

    @pl.when(i >= prompt_tiles)
    def _():
        body(xs_ref, as_ref, bs_ref)


def _out_router(xp, a4p, bp, xs, a4s, bs, wo, gamma, wr, br):
    tp, tsm = xp.shape[0], xs.shape[0]
    tm = 512
    npt, nst = tp // tm, tsm // tm
    t = tp + tsm
    pmap = lambda i: (jnp.minimum(i, npt - 1), 0)
    smap = lambda i: (jnp.maximum(i - npt, 0), 0)
    return pl.pallas_call(
        functools.partial(_out_router_kernel, prompt_tiles=npt),
        grid=(npt + nst,),
        in_specs=[
            pl.BlockSpec((tm, D_MODEL), pmap),
            pl.BlockSpec((4, tm, LANES), lambda i: (0, jnp.minimum(i, npt - 1), 0)),
            pl.BlockSpec((tm, 512), pmap),
            pl.BlockSpec((tm, D_MODEL), smap),
            pl.BlockSpec((4, tm, LANES), lambda i: (0, jnp.maximum(i - npt, 0), 0)),
            pl.BlockSpec((tm, 512), smap),
            pl.BlockSpec((D_MODEL, D_MODEL), lambda i: (0, 0)),
            pl.BlockSpec((1, D_MODEL), lambda i: (0, 0)),
            pl.BlockSpec((3 * D_MODEL, LANES), lambda i: (0, 0)),
            pl.BlockSpec((1, LANES), lambda i: (0, 0)),
        ],
        out_specs=[
            pl.BlockSpec((tm, D_MODEL), lambda i: (i, 0)),
            pl.BlockSpec((tm, D_MODEL // 2), lambda i: (i, 0)),
            pl.BlockSpec((tm, LANES), lambda i: (i, 0)),
        ],
        out_shape=[
            jax.ShapeDtypeStruct((t, D_MODEL), F32),
            jax.ShapeDtypeStruct((t, D_MODEL // 2), jnp.int32),
            jax.ShapeDtypeStruct((t, LANES), F32),
        ],
        compiler_params=_cparams(("arbitrary",)),
        name="out_router",
    )(xp, a4p, bp, xs, a4s, bs, wo, gamma, wr, br)


def _sc_gather_rows(table, idx):
    b = idx.shape[0]
    d = table.shape[1]
    per_worker = b // SC_WORKERS
    nwin = per_worker // SC_WINDOW
    assert per_worker * SC_WORKERS == b and nwin * SC_WINDOW == per_worker
    mesh = plsc.VectorSubcoreMesh(core_axis_name="c", subcore_axis_name="s")

    @functools.partial(
        pl.kernel, mesh=mesh,
        out_type=jax.ShapeDtypeStruct((b, d), table.dtype),
        scratch_types=[pltpu.VMEM((SC_WINDOW,), jnp.int32), pltpu.VMEM((SC_WINDOW, d), table.dtype),
                       pltpu.SemaphoreType.DMA],
        name="sc_gather_rows",
    )
    def gather(table_hbm, idx_hbm, out_hbm, idx_v, rows_v, sem):
        wid = lax.axis_index("s") * SC_CORES + lax.axis_index("c")
        base = wid * per_worker

        @pl.loop(0, nwin)
        def _(j):
            off = pl.multiple_of(base + j * SC_WINDOW, SC_WINDOW)
            pltpu.sync_copy(idx_hbm.at[pl.ds(off, SC_WINDOW)], idx_v)
            pltpu.async_copy(table_hbm.at[idx_v], rows_v, sem).wait()
            pltpu.sync_copy(rows_v, out_hbm.at[pl.ds(off, SC_WINDOW)])

    return gather(table, idx)


def _expert_kernel(be_ref, nu_ref, x_ref, wg_ref, wu_ref, wd_ref, o_ref, wg_s, wu_s, wd_s):
    i = pl.program_id(0)
    used = i < nu_ref[0]
    changed = jnp.logical_or(i == 0, be_ref[i] != be_ref[jnp.maximum(i - 1, 0)])

    @pl.when(jnp.logical_and(used, changed))
    def _():
        wg_s[...] = wg_ref[...].astype(BF16)
        wu_s[...] = wu_ref[...].astype(BF16)
        wd_s[...] = wd_ref[...].astype(BF16)

    @pl.when(used)
    def _():
        x = _unpack_bf16_pairs(x_ref[...]).astype(BF16)
        gate = jnp.dot(x, wg_s[...], preferred_element_type=F32)
        up = jnp.dot(x, wu_s[...], preferred_element_type=F32)
        h = (gate * jax.nn.sigmoid(gate) * up).astype(BF16)
        o_ref[...] = _pack_bf16_pairs(jnp.dot(h, wd_s[...], preferred_element_type=F32))

    @pl.when(jnp.logical_not(used))
    def _():
        o_ref[...] = jnp.zeros_like(o_ref)


def _experts(blk_e, n_used, xb, w_gate, w_up, w_down):
    rows = xb.shape[0]
    nblocks = rows // MOE_ROWS
    grid_spec = pltpu.PrefetchScalarGridSpec(
        num_scalar_prefetch=2,
        grid=(nblocks,),
        in_specs=[
            pl.BlockSpec((MOE_ROWS, D_MODEL // 2), lambda i, be, nu: (i, 0)),
            pl.BlockSpec((None, D_MODEL, D_EXPERT), lambda i, be, nu: (be[i], 0, 0)),
            pl.BlockSpec((None, D_MODEL, D_EXPERT), lambda i, be, nu: (be[i], 0, 0)),
            pl.BlockSpec((None, D_EXPERT, D_MODEL), lambda i, be, nu: (be[i], 0, 0)),
        ],
        out_specs=pl.BlockSpec((MOE_ROWS, D_MODEL // 2), lambda i, be, nu: (i, 0)),
        scratch_shapes=[pltpu.VMEM((D_MODEL, D_EXPERT), BF16), pltpu.VMEM((D_MODEL, D_EXPERT), BF16),
                        pltpu.VMEM((D_EXPERT, D_MODEL), BF16)],
    )
    return pl.pallas_call(
        _expert_kernel,
        grid_spec=grid_spec,
        out_shape=jax.ShapeDtypeStruct((rows, D_MODEL // 2), jnp.int32),
        compiler_params=_cparams(("arbitrary",)),
        name="experts",
    )(blk_e, n_used, xb, w_gate, w_up, w_down)


def _combine_kernel(x1_ref, y1_ref, y2_ref, route_ref, g_ref, outp_ref, outs_ref, *, prompt_tiles):
    r = route_ref[...]
    x = (x1_ref[...] + r[:, 2:3] * _unpack_bf16_pairs(y1_ref[...])
         + r[:, 3:4] * _unpack_bf16_pairs(y2_ref[...]))
    ms = jnp.mean(x * x, axis=-1, keepdims=True)
    y = x * lax.rsqrt(ms + EPS) * g_ref[...]
    i = pl.program_id(0)

    @pl.when(i < prompt_tiles)
    def _():
        outp_ref[...] = y

    @pl.when(i >= prompt_tiles)
    def _():
        outs_ref[...] = y


def _combine_norm(x1, ygath, route, gamma, tp):
    t = x1.shape[0]
    tm = 512
    nt, npt = t // tm, tp // tm
    return pl.pallas_call(
        functools.partial(_combine_kernel, prompt_tiles=npt),
        grid=(nt,),
        in_specs=[
            pl.BlockSpec((tm, D_MODEL), lambda i: (i, 0)),
            pl.BlockSpec((tm, D_MODEL // 2), lambda i: (i, 0)),
            pl.BlockSpec((tm, D_MODEL // 2), lambda i: (i + nt, 0)),
            pl.BlockSpec((tm, LANES), lambda i: (i, 0)),
            pl.BlockSpec((1, D_MODEL), lambda i: (0, 0)),
        ],
        out_specs=[
            pl.BlockSpec((tm, D_MODEL), lambda i: (jnp.minimum(i, npt - 1), 0)),
            pl.BlockSpec((tm, D_MODEL), lambda i: (jnp.maximum(i - npt, 0), 0)),
        ],
        out_shape=[jax.ShapeDtypeStruct((tp, D_MODEL), F32), jax.ShapeDtypeStruct((t - tp, D_MODEL), F32)],
        compiler_params=_cparams(("arbitrary",)),
        name="combine_norm",
    )(x1, ygath, ygath, route, gamma)


def _toeplitz(f, n):
    p = f.shape[-1]
    u = jnp.roll(f, n - 1, axis=-1)
    flat = jnp.tile(u, (1,) * (f.ndim - 1) + (n + 1,))[..., :n * (p + 1)]
    a = flat.reshape(f.shape[:-1] + (n, p + 1))[..., :p]
    return a[..., ::-1, :]


def _with_prev_masked(m, axis):
    prev = np.arange(2 * QB) < QB
    return jnp.stack([m, jnp.where(prev, NEG, m)], axis=axis)


def _bias_a_prompt(table_a):
    c = np.arange(2 * QB)
    valid = c <= QB
    idx = np.stack([_t5_bucket_np(d * np.clip(QB - c, 0, QB)) for d in DILATIONS])
    f = jnp.where(valid[None, None], jnp.transpose(table_a[idx], (0, 2, 1)) * LOG2E, NEG)
    return _with_prev_masked(_toeplitz(f, QB).reshape(3, 4, 2 * QB, 2 * QB), 1)


def _bias_b_prompt(table_b):
    c = np.arange(2 * QB)
    valid = (c >= 1) & (c <= QB)
    f = jnp.where(valid[None], table_b[_t5_bucket_np(np.clip(QB - c, 0, QB))].T * LOG2E, NEG)
    f = jnp.transpose(f.reshape(KV_B, G_B, 2 * QB), (1, 0, 2))
    return _with_prev_masked(_toeplitz(f, QB).reshape(G_B, 2 * QB, 2 * QB), 0)


def _sample_bias_rows(v, span, t):
    rows = []
    for i in range(t):
        pad = jnp.full((v.shape[0], LANES - i - 1), NEG, F32)
        rows.append(jnp.concatenate([v[:, i + 1:i + 1 + span][:, ::-1], v[:, :i + 1][:, ::-1], pad], axis=1))
    return jnp.stack(rows, axis=1).reshape(v.shape[0] * t, span + LANES)


def _bias_a_sample(table_a, t):
    dist = np.arange(WIN_A + t)
    count = np.zeros(dist.shape, np.int32)
    for w, d in zip(WINDOWS, DILATIONS):
        count += ((dist % d == 0) & (dist <= w)).astype(np.int32)
    log2c = np.log2(np.maximum(count, 1)).astype(np.float32)
    v = jnp.where((count > 0)[None], table_a[_t5_bucket_np(dist)].T * LOG2E + log2c[None], NEG)
    return _sample_bias_rows(v, WIN_A, t)


def _bias_b_sample(table_b, t):
    dist = np.arange(WIN_B + t)
    v = jnp.where((dist < WIN_B)[None], table_b[_t5_bucket_np(dist)].T * LOG2E, NEG)
    return _sample_bias_rows(v, WIN_B, t)


def _dispatch(route):
    t = route.shape[0]
    a = t * TOP_K
    eid = route[:, :TOP_K].astype(jnp.int32).reshape(a)
    onehot = (eid[:, None] == jnp.arange(N_EXPERTS, dtype=jnp.int32)[None, :]).astype(jnp.int32)
    csum = jnp.cumsum(onehot, axis=0)
    counts = csum[-1]
    rank = jnp.sum(csum * onehot, axis=1) - 1
    padded = (counts + MOE_ROWS - 1) // MOE_ROWS * MOE_ROWS
    pend = jnp.cumsum(padded)
    pstart = pend - padded
    dest = pstart[eid] + rank
    nblocks = -(-a // MOE_ROWS) + N_EXPERTS
    row_tok = (jnp.arange(nblocks * MOE_ROWS, dtype=jnp.int32) % t).at[dest].set(
        jnp.arange(a, dtype=jnp.int32) // TOP_K)
    blk_start = jnp.arange(nblocks, dtype=jnp.int32) * MOE_ROWS
    blk_e = jnp.minimum(jnp.sum((pend[None, :] <= blk_start[:, None]).astype(jnp.int32), axis=1),
                        N_EXPERTS - 1).astype(jnp.int32)
    n_used = (pend[-1] // MOE_ROWS).astype(jnp.int32).reshape(1)
    pos = jnp.transpose(dest.reshape(t, TOP_K)).reshape(a)
    return row_tok, blk_e, n_used, pos


def kernel(x_prompt, x_sample, cache_a_k, cache_a_v, cache_b_k, cache_b_v, rel_bias_table, attn_norm, w_in,
           w_out, attn_sinks, ffn_norm, w_router_group, b_router_group, w_router_expert, b_router_expert,
           w_gate, w_up, w_down, final_norm):
    s = x_prompt.shape[1]
    ns, ts = x_sample.shape[0], x_sample.shape[1]
    table_a = rel_bias_table[:, :H_A]
    table_b = rel_bias_table[:, H_A:]

    w = w_in[0]
    wqa, wka, wva, wqb, wkb, wvb = (w[:, 0:512], w[:, 512:1024], w[:, 1024:1536], w[:, 1536:2048],
                                    w[:, 2048:2176], w[:, 2176:2304])
    wqb = jnp.transpose(wqb.reshape(D_MODEL, KV_B, G_B, HEAD_DIM), (0, 2, 1, 3)).reshape(D_MODEL, 512)
    wp = jnp.concatenate([wka, wva, wqa, wqb, wkb, wvb], axis=1).astype(BF16)
    cscale = jnp.concatenate([jnp.ones((1, 1024), F32), jnp.full((1, 1024), SCALE * LOG2E, F32),
                              jnp.ones((1, 256), F32)], axis=1)
    wo = w_out[0]
    wo_b = jnp.transpose(wo[512:].reshape(KV_B, G_B, HEAD_DIM, D_MODEL), (1, 0, 2, 3)).reshape(512, D_MODEL)
    wo_p = jnp.concatenate([wo[:512], wo_b], axis=0).astype(BF16)
    wr = jnp.concatenate([w_router_group[0],
                          jnp.transpose(w_router_expert[0], (1, 0, 2)).reshape(D_MODEL, N_EXPERTS),
                          jnp.zeros((D_MODEL, LANES - N_GROUPS - N_EXPERTS), F32)], axis=1)
    wr_hi = wr.astype(BF16)
    wr = jnp.concatenate([wr_hi, wr_hi, (wr - wr_hi.astype(F32)).astype(BF16)], axis=0)
    br = jnp.concatenate([b_router_group[0], b_router_expert[0].reshape(N_EXPERTS),
                          jnp.zeros((LANES - N_GROUPS - N_EXPERTS,), F32)]).reshape(1, LANES)
    sinks2 = attn_sinks[0] * LOG2E
    sinks_gk = jnp.transpose(sinks2.reshape(KV_B, G_B), (1, 0)).reshape(H_B)
    sink_rows_p = jnp.repeat(sinks_gk, QB).reshape(G_B, 2 * QB, 1)
    sink_rows_s = jnp.repeat(sinks2, ts).reshape(H_B * ts, 1)
    emat = jnp.tile(jnp.arange(LANES)[:, None] == (jnp.arange(A_WIDTH)[None, :] // HEAD_DIM),
                    (3, 1)).astype(BF16)
    attn_g = attn_norm[0].reshape(1, D_MODEL)
    ffn_g = ffn_norm[0].reshape(1, D_MODEL)

    xp = x_prompt.reshape(s, D_MODEL)
    aperm, qb_p, kvb_p, akv32, bkv32 = _proj_prompt(xp, attn_g, wp, cscale)
    a4 = _attn_a_prompt(aperm, _bias_a_prompt(table_a), emat)
    ob_p = _attn_b_prompt(qb_p, kvb_p, _bias_b_prompt(table_b), sink_rows_p)

    xs = x_sample.reshape(ns * ts, D_MODEL)
    q_s, kv_s = _proj_sample(xs, attn_g, wp, cscale)
    akt = jnp.transpose(cache_a_k[0], (0, 2, 3, 1)).reshape(ns, A_WIDTH, WIN_A)
    avt = jnp.transpose(cache_a_v[0], (0, 2, 3, 1)).reshape(ns, A_WIDTH, WIN_A)
    bkt = jnp.transpose(cache_b_k[0], (0, 2, 3, 1)).reshape(ns, LANES, WIN_B)
    bvt = jnp.transpose(cache_b_v[0], (0, 2, 3, 1)).reshape(ns, LANES, WIN_B)
    oa_s, ob_s = _attn_sample(q_s.reshape(ns, ts, 1024), kv_s.reshape(ns, ts, 1280), akt, avt, bkt, bvt,
                              _bias_a_sample(table_a, ts), _bias_b_sample(table_b, ts), sink_rows_s)
    a4_s = jnp.transpose(oa_s.reshape(ns * ts, 4, LANES), (1, 0, 2))

    x1, xn, route = _out_router(xp, a4, ob_p, xs, a4_s, ob_s.reshape(ns * ts, 512), wo_p, ffn_g, wr, br)
    row_tok, blk_e, n_used, pos = _dispatch(route)
    yb = _experts(blk_e, n_used, _sc_gather_rows(xn, row_tok), w_gate[0], w_up[0], w_down[0])
    y_p, y_s = _combine_norm(x1, _sc_gather_rows(yb, pos), route, final_norm.reshape(1, D_MODEL), s)

    y_prompt = y_p.reshape(1, s, D_MODEL)
    y_sample = y_s.reshape(ns, ts, D_MODEL)
    keep_a, keep_b = min(WIN_A, s), min(WIN_B, s)
    pak = akv32[s - keep_a:, :512].reshape(1, 1, keep_a, H_A, HEAD_DIM)
    pav = akv32[s - keep_a:, 512:].reshape(1, 1, keep_a, H_A, HEAD_DIM)
    pbk = bkv32[s - keep_b:, :128].reshape(1, 1, keep_b, KV_B, HEAD_DIM)
    pbv = bkv32[s - keep_b:, 128:].reshape(1, 1, keep_b, KV_B, HEAD_DIM)
    sak = kv_s[:, 0:512].reshape(1, ns, ts, H_A, HEAD_DIM)
    sav = kv_s[:, 512:1024].reshape(1, ns, ts, H_A, HEAD_DIM)
    sbk = kv_s[:, 1024:1152].reshape(1, ns, ts, KV_B, HEAD_DIM)
    sbv = kv_s[:, 1152:1280].reshape(1, ns, ts, KV_B, HEAD_DIM)
    return (y_prompt, y_sample, pak, pav, pbk, pbv, sak, sav, sbk, sbv)
```

```python
import functools
import math

import jax
import jax.numpy as jnp
import numpy as np
from jax import lax
from jax.experimental import pallas as pl
from jax.experimental.pallas import tpu as pltpu
from jax.experimental.pallas import tpu_sc as plsc

D_MODEL = 1024
HEAD_DIM = 64
H_A = 8
H_B = 8
KV_B = 2
G_B = 4
DILATIONS = (1, 4, 16)
WINDOWS = (128, 512, 2048)
WIN_A = 2048
WIN_B = 128
NUM_BUCKETS = 32
MAX_DISTANCE = 2048
N_GROUPS = 4
EXPERTS_PER_GROUP = 8
N_EXPERTS = 32
TOP_K = 2
D_EXPERT = 512
EPS = 1e-5
SCALE = HEAD_DIM ** -0.5
PAST_LEN = 16384

LANES = 128
SPAN = 2048
QB = 128
NCHUNK = 9
A_WIDTH = H_A * HEAD_DIM
MOE_ROWS = 256
SC_CORES = 2
SC_SUBCORES = 16
SC_WORKERS = SC_CORES * SC_SUBCORES
SC_WINDOW = 64
NEG = -1e30
LOG2E = math.log2(math.e)
B_STEP = 512
VMEM_LIMIT = 56 * 1024 * 1024

F32 = jnp.float32
BF16 = jnp.bfloat16


def _t5_bucket_np(dist):
    dist = np.asarray(dist, np.int64)
    max_exact = NUM_BUCKETS // 2
    d = np.maximum(dist, 1).astype(np.float32)
    ratio = np.log(d / np.float32(max_exact)) / np.float32(math.log(MAX_DISTANCE / max_exact))
    large = max_exact + (ratio * np.float32(NUM_BUCKETS - max_exact)).astype(np.int32)
    large = np.minimum(large, NUM_BUCKETS - 1)
    return np.where(dist < max_exact, dist, large).astype(np.int32)


def _cparams(sem, vmem=VMEM_LIMIT):
    return pltpu.CompilerParams(dimension_semantics=sem, vmem_limit_bytes=vmem)


def _proj_prompt_kernel(x_ref, g_ref, w_ref, cs_ref, aperm_ref, qb_ref, kvb_ref, akv_ref, bkv_ref,
                        h_scr, p_scr):
    n = pl.program_id(1)

    @pl.when(n == 0)
    def _():
        x = x_ref[...]
        ms = jnp.mean(x * x, axis=-1, keepdims=True)
        h_scr[...] = (x * lax.rsqrt(ms + EPS) * g_ref[...]).astype(BF16)

    p = jnp.dot(h_scr[...], w_ref[...], preferred_element_type=F32) * cs_ref[...]

    @pl.when(n < 6)
    def _():
        aperm_ref[0] = p.astype(BF16)
        p_scr[0] = p[:, :LANES]
        p_scr[1] = p[:, LANES:]
        for gi, dil in ((1, 4), (2, 16)):
            rows = SPAN // dil
            for r in range(dil):
                t = jnp.concatenate([p_scr[0, pl.ds(r, rows, stride=dil), :],
                                     p_scr[1, pl.ds(r, rows, stride=dil), :]], axis=1)
                aperm_ref[gi, r * rows:(r + 1) * rows, :] = t.astype(BF16)

    @pl.when(n < 4)
    def _():
        akv_ref[...] = p

    @pl.when(jnp.logical_or(n == 6, n == 7))
    def _():
        qb_ref[...] = p.astype(BF16)

    @pl.when(n == 8)
    def _():
        kvb_ref[...] = p.astype(BF16)
        bkv_ref[...] = p


def _proj_prompt(x, gamma, w, cscale):
    s = x.shape[0]
    nspan = s // SPAN
    return pl.pallas_call(
        _proj_prompt_kernel,
        grid=(nspan, NCHUNK),
        in_specs=[
            pl.BlockSpec((SPAN, D_MODEL), lambda b, n: (b, 0)),
            pl.BlockSpec((1, D_MODEL), lambda b, n: (0, 0)),
            pl.BlockSpec((D_MODEL, 256), lambda b, n: (0, n)),
            pl.BlockSpec((1, 256), lambda b, n: (0, n)),
        ],
        out_specs=[
            pl.BlockSpec((3, SPAN, 256), lambda b, n: (0, b, jnp.minimum(n, 5))),
            pl.BlockSpec((SPAN, 256), lambda b, n: (b, jnp.clip(n - 6, 0, 1))),
            pl.BlockSpec((SPAN, 256), lambda b, n: (b, 0)),
            pl.BlockSpec((SPAN, 256), lambda b, n: (b, jnp.minimum(n, 3))),
            pl.BlockSpec((SPAN, 256), lambda b, n: (b, 0)),
        ],
        out_shape=[
            jax.ShapeDtypeStruct((3, s, 3 * A_WIDTH), BF16),
            jax.ShapeDtypeStruct((s, 512), BF16),
            jax.ShapeDtypeStruct((s, 256), BF16),
            jax.ShapeDtypeStruct((s, 1024), F32),
            jax.ShapeDtypeStruct((s, 256), F32),
        ],
        scratch_shapes=[pltpu.VMEM((SPAN, D_MODEL), BF16), pltpu.VMEM((2, SPAN, LANES), F32)],
        compiler_params=_cparams(("arbitrary", "arbitrary")),
        name="proj_prompt",
    )(x, gamma, w, cscale)


def _proj_sample_kernel(x_ref, g_ref, w_ref, cs_ref, q_ref, kv_ref):
    x = x_ref[...]
    ms = jnp.mean(x * x, axis=-1, keepdims=True)
    h = (x * lax.rsqrt(ms + EPS) * g_ref[...]).astype(BF16)
    p = jnp.dot(h, w_ref[...], preferred_element_type=F32) * cs_ref[...]
    kv_ref[:, :1024] = p[:, :1024]
    kv_ref[:, 1024:] = p[:, 2048:]
    q_ref[...] = p[:, 1024:2048]


def _proj_sample(x, gamma, w, cscale):
    t = x.shape[0]
    tm = 512
    return pl.pallas_call(
        _proj_sample_kernel,
        grid=(t // tm,),
        in_specs=[
            pl.BlockSpec((tm, D_MODEL), lambda i: (i, 0)),
            pl.BlockSpec((1, D_MODEL), lambda i: (0, 0)),
            pl.BlockSpec((D_MODEL, 2304), lambda i: (0, 0)),
            pl.BlockSpec((1, 2304), lambda i: (0, 0)),
        ],
        out_specs=[
            pl.BlockSpec((tm, 1024), lambda i: (i, 0)),
            pl.BlockSpec((tm, 1280), lambda i: (i, 0)),
        ],
        out_shape=[
            jax.ShapeDtypeStruct((t, 1024), F32),
            jax.ShapeDtypeStruct((t, 1280), F32),
        ],
        compiler_params=_cparams(("arbitrary",)),
        name="proj_sample",
    )(x, gamma, w, cscale)


def _spread_heads(w, e3_ref):
    hi = w.astype(BF16)
    r1 = w - hi.astype(F32)
    mid = r1.astype(BF16)
    low = (r1 - mid.astype(F32)).astype(BF16)
    return jnp.dot(jnp.concatenate([hi, mid, low], axis=1), e3_ref[...], preferred_element_type=F32)


def _pair_tile(q2, kk, vv, bias_t, lo, sink=None):
    zero = jnp.zeros_like(q2)
    qq = jnp.concatenate([jnp.where(lo, q2, zero), jnp.where(lo, zero, q2)], axis=0)
    st = lax.dot_general(kk, qq, (((1,), (1,)), ((), ())), preferred_element_type=F32)
    st = st + bias_t
    m = jnp.max(st, axis=0, keepdims=True)
    if sink is not None:
        m = jnp.maximum(m, sink)
    p = jnp.exp2(st - m)
    den = jnp.sum(p, axis=0, keepdims=True)
    if sink is not None:
        den = den + jnp.exp2(sink - m)
    pn = (p * (1.0 / den)).astype(BF16)
    o = lax.dot_general(pn, vv, (((0,), (0,)), ((), ())), preferred_element_type=F32)
    return jnp.where(lo, o[:QB], o[QB:]), m + jnp.log2(den)


def _attn_a_kernel(q_ref, kvc_ref, kvp_ref, bias_ref, e_ref, out_ref, o_scr, st_scr):
    b = pl.program_id(0)
    g = pl.program_id(1)
    nblk = jnp.where(g == 0, 16, jnp.where(g == 1, 4, 1))
    lane = lax.broadcasted_iota(jnp.int32, (QB, LANES), 1)
    lo = lane < HEAD_DIM

    for cb in range(SPAN // QB):
        first = lax.rem(jnp.int32(cb), nblk) == 0
        rows = slice(cb * QB, (cb + 1) * QB)
        prow_c = max(cb - 1, 0) * QB
        prow_p = pl.multiple_of(jnp.where(first, cb + nblk - 1, 0) * QB, QB)
        variant = jnp.logical_and(first, b == 0).astype(jnp.int32)
        stats = []
        for hp in range(4):
            ks = slice(hp * LANES, (hp + 1) * LANES)
            vs = slice(A_WIDTH + hp * LANES, A_WIDTH + (hp + 1) * LANES)
            kp = jnp.where(first, kvp_ref[pl.ds(prow_p, QB), ks], kvc_ref[prow_c:prow_c + QB, ks])
            vp = jnp.where(first, kvp_ref[pl.ds(prow_p, QB), vs], kvc_ref[prow_c:prow_c + QB, vs])
            kk = jnp.concatenate([kp, kvc_ref[rows, ks]], axis=0)
            vv = jnp.concatenate([vp, kvc_ref[rows, vs]], axis=0)
            o, lse = _pair_tile(q_ref[rows, ks], kk, vv, bias_ref[variant, hp], lo)
            o_scr[g, hp, rows, :] = o
            stats += [lse[:, :QB], lse[:, QB:]]
        sm = jnp.concatenate(stats + [jnp.zeros((LANES - H_A, QB), F32)], axis=0)
        st_scr[g, rows, :] = sm.T

    @pl.when(g == 2)
    def _():
        def merge(c, carry):
            r2 = lax.rem(c, 4) * (SPAN // 4) + c // 4
            r3 = pl.multiple_of(c * QB, QB)
            l1 = st_scr[0, pl.ds(c, QB, stride=16), :]
            l2 = st_scr[1, pl.ds(r2, QB, stride=4), :]
            l3 = st_scr[2, pl.ds(r3, QB), :]
            mx = jnp.maximum(jnp.maximum(l1, l2), l3)
            w1 = jnp.exp2(l1 - mx)
            w2 = jnp.exp2(l2 - mx)
            w3 = jnp.exp2(l3 - mx)
            tot = w1 + w2 + w3
            a1 = _spread_heads(w1 / tot, e_ref)
            a2 = _spread_heads(w2 / tot, e_ref)
            a3 = _spread_heads(w3 / tot, e_ref)
            for hp in range(4):
                sl = slice(hp * LANES, (hp + 1) * LANES)
                o1 = o_scr[0, hp, pl.ds(c, QB, stride=16), :]
                o2 = o_scr[1, hp, pl.ds(r2, QB, stride=4), :]
                o3 = o_scr[2, hp, pl.ds(r3, QB), :]
                out_ref[hp, pl.ds(c, QB, stride=16), :] = a1[:, sl] * o1 + a2[:, sl] * o2 + a3[:, sl] * o3
            return carry

        lax.fori_loop(0, 16, merge, 0)


def _attn_a_prompt(aperm, bias_a, emat):
    s = aperm.shape[1]
    nspan = s // SPAN
    return pl.pallas_call(
        _attn_a_kernel,
        grid=(nspan, 3),
        in_specs=[
            pl.BlockSpec((None, SPAN, A_WIDTH), lambda b, g: (g, b, 2)),
            pl.BlockSpec((None, SPAN, 2 * A_WIDTH), lambda b, g: (g, b, 0)),
            pl.BlockSpec((None, SPAN, 2 * A_WIDTH), lambda b, g: (g, jnp.maximum(b - 1, 0), 0)),
            pl.BlockSpec((None, 2, 4, 2 * QB, 2 * QB), lambda b, g: (g, 0, 0, 0, 0)),
            pl.BlockSpec((3 * LANES, A_WIDTH), lambda b, g: (0, 0)),
        ],
        out_specs=pl.BlockSpec((4, SPAN, LANES), lambda b, g: (0, b, 0)),
        out_shape=jax.ShapeDtypeStruct((4, s, LANES), F32),
        scratch_shapes=[pltpu.VMEM((3, 4, SPAN, LANES), F32), pltpu.VMEM((3, SPAN, LANES), F32)],
        compiler_params=_cparams(("arbitrary", "arbitrary")),
        name="attn_a_prompt",
    )(aperm, aperm, aperm, bias_a, emat)


def _attn_b_kernel(q_ref, kvc_ref, kvp_ref, bias_ref, sink_ref, out_ref):
    i = pl.program_id(0)
    lane = lax.broadcasted_iota(jnp.int32, (QB, LANES), 1)
    lo = lane < HEAD_DIM
    variant = (i == 0).astype(jnp.int32)
    for j in range(B_STEP // QB):
        rows = slice(j * QB, (j + 1) * QB)
        if j == 0:
            kp, vp = kvp_ref[:, :LANES], kvp_ref[:, LANES:]
        else:
            kp, vp = kvc_ref[(j - 1) * QB:j * QB, :LANES], kvc_ref[(j - 1) * QB:j * QB, LANES:]
        kk = jnp.concatenate([kp, kvc_ref[rows, :LANES]], axis=0)
        vv = jnp.concatenate([vp, kvc_ref[rows, LANES:]], axis=0)
        for g in range(G_B):
            bias_t = bias_ref[variant, g] if j == 0 else bias_ref[0, g]
            o, _ = _pair_tile(q_ref[rows, g * LANES:(g + 1) * LANES], kk, vv, bias_t, lo, sink=sink_ref[g])
            out_ref[rows, g * LANES:(g + 1) * LANES] = o.astype(BF16)


def _attn_b_prompt(qb, kvb, bias_b, sink_rows):
    s = qb.shape[0]
    per = B_STEP // QB
    return pl.pallas_call(
        _attn_b_kernel,
        grid=(s // B_STEP,),
        in_specs=[
            pl.BlockSpec((B_STEP, 512), lambda i: (i, 0)),
            pl.BlockSpec((B_STEP, 256), lambda i: (i, 0)),
            pl.BlockSpec((QB, 256), lambda i: (jnp.maximum(i * per - 1, 0), 0)),
            pl.BlockSpec((2, G_B, 2 * QB, 2 * QB), lambda i: (0, 0, 0, 0)),
            pl.BlockSpec((G_B, 1, 2 * QB), lambda i: (0, 0, 0)),
        ],
        out_specs=pl.BlockSpec((B_STEP, 512), lambda i: (i, 0)),
        out_shape=jax.ShapeDtypeStruct((s, 512), BF16),
        compiler_params=_cparams(("arbitrary",)),
        name="attn_b_prompt",
    )(qb, kvb, kvb, bias_b, sink_rows)


def _attn_sample_kernel(q_ref, kvn_ref, akt_ref, avt_ref, bkt_ref, bvt_ref, cba_ref, cbb_ref, sink_ref,
                        oa_ref, ob_ref):
    t = q_ref.shape[0]
    q = q_ref[...]
    kvn = kvn_ref[...]
    kvn_p = jnp.concatenate([kvn, jnp.zeros((LANES - t, kvn.shape[1]), F32)], axis=0).astype(BF16)
    lane_a = lax.broadcasted_iota(jnp.int32, (t, A_WIDTH), 1) // HEAD_DIM

    qa = q[:, :A_WIDTH]
    qbd = jnp.concatenate([jnp.where(lane_a == h, qa, 0.0) for h in range(H_A)], axis=0).astype(BF16)
    s_c = jnp.dot(qbd, akt_ref[...].astype(BF16), preferred_element_type=F32)
    s_n = lax.dot_general(qbd, kvn_p[:, :A_WIDTH], (((1,), (1,)), ((), ())), preferred_element_type=F32)
    s = jnp.concatenate([s_c, s_n], axis=1) + cba_ref[...]
    m = jnp.max(s, axis=-1, keepdims=True)
    p = jnp.exp2(s - m)
    l = jnp.sum(p, axis=-1, keepdims=True)
    o_n = jnp.dot(p[:, WIN_A:].astype(BF16), kvn_p[:, A_WIDTH:2 * A_WIDTH], preferred_element_type=F32)
    pc = jnp.concatenate([p[:, :WIN_A], jnp.zeros((LANES - H_A * t, WIN_A), F32)], axis=0).astype(BF16)
    o_t = lax.dot_general(avt_ref[...].astype(BF16), pc, (((1,), (1,)), ((), ())),
                          preferred_element_type=F32)
    o_all = o_t.T[:H_A * t] + o_n
    o_sel = jnp.zeros((t, A_WIDTH), F32)
    l_b = jnp.ones((t, A_WIDTH), F32)
    for h in range(H_A):
        sel = lane_a == h
        o_sel = jnp.where(sel, o_all[h * t:(h + 1) * t], o_sel)
        l_b = jnp.where(sel, l[h * t:(h + 1) * t], l_b)
    oa_ref[...] = o_sel / l_b

    lane_b = lax.broadcasted_iota(jnp.int32, (G_B * t, LANES), 1)
    lo = lane_b < HEAD_DIM
    qb2 = jnp.concatenate([q[:, A_WIDTH + g * LANES:A_WIDTH + (g + 1) * LANES] for g in range(G_B)], axis=0)
    qm = jnp.concatenate([jnp.where(lo, qb2, 0.0), jnp.where(lo, 0.0, qb2)], axis=0).astype(BF16)
    kb_n = kvn_p[:, 2 * A_WIDTH:2 * A_WIDTH + LANES]
    vb_n = kvn_p[:, 2 * A_WIDTH + LANES:]
    sb_c = jnp.dot(qm, bkt_ref[...].astype(BF16), preferred_element_type=F32)
    sb_n = lax.dot_general(qm, kb_n, (((1,), (1,)), ((), ())), preferred_element_type=F32)
    sb = jnp.concatenate([sb_c, sb_n], axis=1) + cbb_ref[...]
    sink = sink_ref[...]
    mb = jnp.maximum(jnp.max(sb, axis=-1, keepdims=True), sink)
    pbb = jnp.exp2(sb - mb)
    den = jnp.sum(pbb, axis=-1, keepdims=True) + jnp.exp2(sink - mb)
    pbb = pbb.astype(BF16)
    ob = lax.dot_general(pbb[:, :WIN_B], bvt_ref[...].astype(BF16), (((1,), (1,)), ((), ())),
                         preferred_element_type=F32)
    ob = (ob + jnp.dot(pbb[:, WIN_B:], vb_n, preferred_element_type=F32)) / den
    half = G_B * t
    lo8 = lo[:t]
    for g in range(G_B):
        ob_ref[:, g * LANES:(g + 1) * LANES] = jnp.where(
            lo8, ob[g * t:(g + 1) * t], ob[half + g * t:half + (g + 1) * t])


def _attn_sample(q3, kvn3, akt, avt, bkt, bvt, cbias_a, cbias_b, sink_rows):
    ns, t = q3.shape[0], q3.shape[1]
    return pl.pallas_call(
        _attn_sample_kernel,
        grid=(ns,),
        in_specs=[
            pl.BlockSpec((None, t, 1024), lambda n: (n, 0, 0)),
            pl.BlockSpec((None, t, 1280), lambda n: (n, 0, 0)),
            pl.BlockSpec((None, A_WIDTH, WIN_A), lambda n: (n, 0, 0)),
            pl.BlockSpec((None, A_WIDTH, WIN_A), lambda n: (n, 0, 0)),
            pl.BlockSpec((None, LANES, WIN_B), lambda n: (n, 0, 0)),
            pl.BlockSpec((None, LANES, WIN_B), lambda n: (n, 0, 0)),
            pl.BlockSpec((H_A * t, WIN_A + LANES), lambda n: (0, 0)),
            pl.BlockSpec((H_B * t, WIN_B + LANES), lambda n: (0, 0)),
            pl.BlockSpec((H_B * t, 1), lambda n: (0, 0)),
        ],
        out_specs=[
            pl.BlockSpec((None, t, 512), lambda n: (n, 0, 0)),
            pl.BlockSpec((None, t, 512), lambda n: (n, 0, 0)),
        ],
        out_shape=[jax.ShapeDtypeStruct((ns, t, 512), F32), jax.ShapeDtypeStruct((ns, t, 512), F32)],
        compiler_params=_cparams(("arbitrary",)),
        name="attn_sample",
    )(q3, kvn3, akt, avt, bkt, bvt, cbias_a, cbias_b, sink_rows)


def _route(logits):
    lane = lax.broadcasted_iota(jnp.int32, logits.shape, 1).astype(F32)
    big = jnp.float32(1 << 20)
    ninf = jnp.float32(-jnp.inf)
    gmask = lane < N_GROUPS
    lg = jnp.where(gmask, logits, ninf)
    gmax = jnp.max(lg, axis=-1, keepdims=True)
    grp = jnp.min(jnp.where(lg == gmax, lane, big), axis=-1, keepdims=True)
    pg_top = 1.0 / jnp.sum(jnp.exp(lg - gmax), axis=-1, keepdims=True)
    e0 = N_GROUPS + grp * EXPERTS_PER_GROUP
    emask = jnp.logical_and(lane >= e0, lane < e0 + EXPERTS_PER_GROUP)
    le = jnp.where(emask, logits, ninf)
    emax = jnp.max(le, axis=-1, keepdims=True)
    esum = jnp.sum(jnp.exp(le - emax), axis=-1, keepdims=True)
    i1 = jnp.min(jnp.where(le == emax, lane, big), axis=-1, keepdims=True)
    le2 = jnp.where(lane == i1, ninf, le)
    e2max = jnp.max(le2, axis=-1, keepdims=True)
    i2 = jnp.min(jnp.where(le2 == e2max, lane, big), axis=-1, keepdims=True)
    p1 = 1.0 / esum
    p2 = jnp.exp(e2max - emax) / esum
    g1 = pg_top * p1 / (p1 + p2)
    g2 = pg_top * p2 / (p1 + p2)
    out = jnp.where(lane == 0, i1 - N_GROUPS, 0.0)
    out = jnp.where(lane == 1, i2 - N_GROUPS, out)
    out = jnp.where(lane == 2, g1, out)
    out = jnp.where(lane == 3, g2, out)
    return out


def _pack_bf16_pairs(x):
    half = x.shape[1] // 2

    def rne(v):
        bits = lax.bitcast_convert_type(v, jnp.int32)
        return bits + 0x7FFF + (lax.shift_right_logical(bits, 16) & 1)

    lo = lax.shift_right_logical(rne(x[:, :half]), 16)
    hi = rne(x[:, half:]) & jnp.int32(-65536)
    return lo | hi


def _unpack_bf16_pairs(w):
    lo = lax.bitcast_convert_type(lax.shift_left(w, 16), F32)
    hi = lax.bitcast_convert_type(w & jnp.int32(-65536), F32)
    return jnp.concatenate([lo, hi], axis=1)


def _out_router_kernel(xp_ref, ap_ref, bp_ref, xs_ref, as_ref, bs_ref, wo_ref, g_ref, wr_ref, br_ref,
                       x1_ref, xn_ref, route_ref, *, prompt_tiles):
    def body(x_ref, a_ref, b_ref):
        mix = jnp.concatenate([a_ref[0], a_ref[1], a_ref[2], a_ref[3]], axis=1).astype(BF16)
        mix = jnp.concatenate([mix, b_ref[...].astype(BF16)], axis=1)
        x1 = x_ref[...] + jnp.dot(mix, wo_ref[...], preferred_element_type=F32)
        x1_ref[...] = x1
        ms = jnp.mean(x1 * x1, axis=-1, keepdims=True)
        xn = x1 * lax.rsqrt(ms + EPS) * g_ref[...]
        xn_ref[...] = _pack_bf16_pairs(xn)
        xh = xn.astype(BF16)
        xl = (xn - xh.astype(F32)).astype(BF16)
        logits = jnp.dot(jnp.concatenate([xh, xl, xh], axis=1), wr_ref[...], preferred_element_type=F32)
        route_ref[...] = _route(logits + br_ref[...])

    i = pl.program_id(0)

    @pl.when(i < prompt_tiles)
    def _():
        body(xp_ref, ap_ref, bp_ref)

    @pl.when(i >= prompt_tiles)
    def _():
        body(xs_ref, as_ref, bs_ref)


def _out_router(xp, a4p, bp, xs, a4s, bs, wo, gamma, wr, br):
    tp, tsm = xp.shape[0], xs.shape[0]
    tm = 512
    npt, nst = tp // tm, tsm // tm
    t = tp + tsm
    pmap = lambda i: (jnp.minimum(i, npt - 1), 0)
    smap = lambda i: (jnp.maximum(i - npt, 0), 0)
    return pl.pallas_call(
        functools.partial(_out_router_kernel, prompt_tiles=npt),
        grid=(npt + nst,),
        in_specs=[
            pl.BlockSpec((tm, D_MODEL), pmap),
            pl.BlockSpec((4, tm, LANES), lambda i: (0, jnp.minimum(i, npt - 1), 0)),
            pl.BlockSpec((tm, 512), pmap),
            pl.BlockSpec((tm, D_MODEL), smap),
            pl.BlockSpec((4, tm, LANES), lambda i: (0, jnp.maximum(i - npt, 0), 0)),
            pl.BlockSpec((tm, 512), smap),
            pl.BlockSpec((D_MODEL, D_MODEL), lambda i: (0, 0)),
            pl.BlockSpec((1, D_MODEL), lambda i: (0, 0)),
            pl.BlockSpec((3 * D_MODEL, LANES), lambda i: (0, 0)),
            pl.BlockSpec((1, LANES), lambda i: (0, 0)),
        ],
        out_specs=[
            pl.BlockSpec((tm, D_MODEL), lambda i: (i, 0)),
            pl.BlockSpec((tm, D_MODEL // 2), lambda i: (i, 0)),
            pl.BlockSpec((tm, LANES), lambda i: (i, 0)),
        ],
        out_shape=[
            jax.ShapeDtypeStruct((t, D_MODEL), F32),
            jax.ShapeDtypeStruct((t, D_MODEL // 2), jnp.int32),
            jax.ShapeDtypeStruct((t, LANES), F32),
        ],
        compiler_params=_cparams(("arbitrary",)),
        name="out_router",
    )(xp, a4p, bp, xs, a4s, bs, wo, gamma, wr, br)


def _sc_gather_rows(table, idx):
    b = idx.shape[0]
    d = table.shape[1]
    per_worker = b // SC_WORKERS
    nwin = per_worker // SC_WINDOW
    assert per_worker * SC_WORKERS == b and nwin * SC_WINDOW == per_worker
    mesh = plsc.VectorSubcoreMesh(core_axis_name="c", subcore_axis_name="s")

    @functools.partial(
        pl.kernel, mesh=mesh,
        out_type=jax.ShapeDtypeStruct((b, d), table.dtype),
        scratch_types=[pltpu.VMEM((SC_WINDOW,), jnp.int32), pltpu.VMEM((SC_WINDOW, d), table.dtype),
                       pltpu.SemaphoreType.DMA],
        name="sc_gather_rows",
    )
    def gather(table_hbm, idx_hbm, out_hbm, idx_v, rows_v, sem):
        wid = lax.axis_index("s") * SC_CORES + lax.axis_index("c")
        base = wid * per_worker

        @pl.loop(0, nwin)
        def _(j):
            off = pl.multiple_of(base + j * SC_WINDOW, SC_WINDOW)
            pltpu.sync_copy(idx_hbm.at[pl.ds(off, SC_WINDOW)], idx_v)
            pltpu.async_copy(table_hbm.at[idx_v], rows_v, sem).wait()
            pltpu.sync_copy(rows_v, out_hbm.at[pl.ds(off, SC_WINDOW)])

    return gather(table, idx)


def _expert_kernel(be_ref, nu_ref, x_ref, wg_ref, wu_ref, wd_ref, o_ref, wg_s, wu_s, wd_s):
    i = pl.program_id(0)
    used = i < nu_ref[0]
    changed = jnp.logical_or(i == 0, be_ref[i] != be_ref[jnp.maximum(i - 1, 0)])

    @pl.when(jnp.logical_and(used, changed))
    def _():
        wg_s[...] = wg_ref[...].astype(BF16)
        wu_s[...] = wu_ref[...].astype(BF16)
        wd_s[...] = wd_ref[...].astype(BF16)

    @pl.when(used)
    def _():
        x = _unpack_bf16_pairs(x_ref[...]).astype(BF16)
        gate = jnp.dot(x, wg_s[...], preferred_element_type=F32)
        up = jnp.dot(x, wu_s[...], preferred_element_type=F32)
        h = (gate * jax.nn.sigmoid(gate) * up).astype(BF16)
        o_ref[...] = _pack_bf16_pairs(jnp.dot(h, wd_s[...], preferred_element_type=F32))

    @pl.when(jnp.logical_not(used))
    def _():
        o_ref[...] = jnp.zeros_like(o_ref)


def _experts(blk_e, n_used, xb, w_gate, w_up, w_down):
    rows = xb.shape[0]
    nblocks = rows // MOE_ROWS
    grid_spec = pltpu.PrefetchScalarGridSpec(
        num_scalar_prefetch=2,
        grid=(nblocks,),
        in_specs=[
            pl.BlockSpec((MOE_ROWS, D_MODEL // 2), lambda i, be, nu: (i, 0)),
            pl.BlockSpec((None, D_MODEL, D_EXPERT), lambda i, be, nu: (be[i], 0, 0)),
            pl.BlockSpec((None, D_MODEL, D_EXPERT), lambda i, be, nu: (be[i], 0, 0)),
            pl.BlockSpec((None, D_EXPERT, D_MODEL), lambda i, be, nu: (be[i], 0, 0)),
        ],
        out_specs=pl.BlockSpec((MOE_ROWS, D_MODEL // 2), lambda i, be, nu: (i, 0)),
        scratch_shapes=[pltpu.VMEM((D_MODEL, D_EXPERT), BF16), pltpu.VMEM((D_MODEL, D_EXPERT), BF16),
                        pltpu.VMEM((D_EXPERT, D_MODEL), BF16)],
    )
    return pl.pallas_call(
        _expert_kernel,
        grid_spec=grid_spec,
        out_shape=jax.ShapeDtypeStruct((rows, D_MODEL // 2), jnp.int32),
        compiler_params=_cparams(("arbitrary",)),
        name="experts",
    )(blk_e, n_used, xb, w_gate, w_up, w_down)


def _combine_kernel(x1_ref, y1_ref, y2_ref, route_ref, g_ref, outp_ref, outs_ref, *, prompt_tiles):
    r = route_ref[...]
    x = (x1_ref[...] + r[:, 2:3] * _unpack_bf16_pairs(y1_ref[...])
         + r[:, 3:4] * _unpack_bf16_pairs(y2_ref[...]))
    ms = jnp.mean(x * x, axis=-1, keepdims=True)
    y = x * lax.rsqrt(ms + EPS) * g_ref[...]
    i = pl.program_id(0)

    @pl.when(i < prompt_tiles)
    def _():
        outp_ref[...] = y

    @pl.when(i >= prompt_tiles)
    def _():
        outs_ref[...] = y


def _combine_norm(x1, ygath, route, gamma, tp):
    t = x1.shape[0]
    tm = 512
    nt, npt = t // tm, tp // tm
    return pl.pallas_call(
        functools.partial(_combine_kernel, prompt_tiles=npt),
        grid=(nt,),
        in_specs=[
            pl.BlockSpec((tm, D_MODEL), lambda i: (i, 0)),
            pl.BlockSpec((tm, D_MODEL // 2), lambda i: (i, 0)),
            pl.BlockSpec((tm, D_MODEL // 2), lambda i: (i + nt, 0)),
            pl.BlockSpec((tm, LANES), lambda i: (i, 0)),
            pl.BlockSpec((1, D_MODEL), lambda i: (0, 0)),
        ],
        out_specs=[
            pl.BlockSpec((tm, D_MODEL), lambda i: (jnp.minimum(i, npt - 1), 0)),
            pl.BlockSpec((tm, D_MODEL), lambda i: (jnp.maximum(i - npt, 0), 0)),
        ],
        out_shape=[jax.ShapeDtypeStruct((tp, D_MODEL), F32), jax.ShapeDtypeStruct((t - tp, D_MODEL), F32)],
        compiler_params=_cparams(("arbitrary",)),
        name="combine_norm",
    )(x1, ygath, ygath, route, gamma)


def _toeplitz(f, n):
    p = f.shape[-1]
    u = jnp.roll(f, n - 1, axis=-1)
    flat = jnp.tile(u, (1,) * (f.ndim - 1) + (n + 1,))[..., :n * (p + 1)]
    a = flat.reshape(f.shape[:-1] + (n, p + 1))[..., :p]
    return a[..., ::-1, :]


def _with_prev_masked(m, axis):
    prev = np.arange(2 * QB) < QB
    return jnp.stack([m, jnp.where(prev, NEG, m)], axis=axis)


def _bias_a_prompt(table_a):
    c = np.arange(2 * QB)
    valid = c <= QB
    idx = np.stack([_t5_bucket_np(d * np.clip(QB - c, 0, QB)) for d in DILATIONS])
    f = jnp.where(valid[None, None], jnp.transpose(table_a[idx], (0, 2, 1)) * LOG2E, NEG)
    m = _with_prev_masked(_toeplitz(f, QB).reshape(3, 4, 2 * QB, 2 * QB), 1)
    return jnp.swapaxes(m, -1, -2)


def _bias_b_prompt(table_b):
    c = np.arange(2 * QB)
    valid = (c >= 1) & (c <= QB)
    f = jnp.where(valid[None], table_b[_t5_bucket_np(np.clip(QB - c, 0, QB))].T * LOG2E, NEG)
    f = jnp.transpose(f.reshape(KV_B, G_B, 2 * QB), (1, 0, 2))
    m = _with_prev_masked(_toeplitz(f, QB).reshape(G_B, 2 * QB, 2 * QB), 0)
    return jnp.swapaxes(m, -1, -2)


def _sample_bias_rows(v, span, t):
    rows = []
    for i in range(t):
        pad = jnp.full((v.shape[0], LANES - i - 1), NEG, F32)
        rows.append(jnp.concatenate([v[:, i + 1:i + 1 + span][:, ::-1], v[:, :i + 1][:, ::-1], pad], axis=1))
    return jnp.stack(rows, axis=1).reshape(v.shape[0] * t, span + LANES)


def _bias_a_sample(table_a, t):
    dist = np.arange(WIN_A + t)
    count = np.zeros(dist.shape, np.int32)
    for w, d in zip(WINDOWS, DILATIONS):
        count += ((dist % d == 0) & (dist <= w)).astype(np.int32)
    log2c = np.log2(np.maximum(count, 1)).astype(np.float32)
    v = jnp.where((count > 0)[None], table_a[_t5_bucket_np(dist)].T * LOG2E + log2c[None], NEG)
    return _sample_bias_rows(v, WIN_A, t)


def _bias_b_sample(table_b, t):
    dist = np.arange(WIN_B + t)
    v = jnp.where((dist < WIN_B)[None], table_b[_t5_bucket_np(dist)].T * LOG2E, NEG)
    return _sample_bias_rows(v, WIN_B, t)


def _dispatch(route):
    t = route.shape[0]
    a = t * TOP_K
    eid = route[:, :TOP_K].astype(jnp.int32).reshape(a)
    onehot = (eid[:, None] == jnp.arange(N_EXPERTS, dtype=jnp.int32)[None, :]).astype(jnp.int32)
    csum = jnp.cumsum(onehot, axis=0)
    counts = csum[-1]
    rank = jnp.sum(csum * onehot, axis=1) - 1
    padded = (counts + MOE_ROWS - 1) // MOE_ROWS * MOE_ROWS
    pend = jnp.cumsum(padded)
    pstart = pend - padded
    dest = pstart[eid] + rank
    nblocks = -(-a // MOE_ROWS) + N_EXPERTS
    row_tok = (jnp.arange(nblocks * MOE_ROWS, dtype=jnp.int32) % t).at[dest].set(
        jnp.arange(a, dtype=jnp.int32) // TOP_K)
    blk_start = jnp.arange(nblocks, dtype=jnp.int32) * MOE_ROWS
    blk_e = jnp.minimum(jnp.sum((pend[None, :] <= blk_start[:, None]).astype(jnp.int32), axis=1),
                        N_EXPERTS - 1).astype(jnp.int32)
    n_used = (pend[-1] // MOE_ROWS).astype(jnp.int32).reshape(1)
    pos = jnp.transpose(dest.reshape(t, TOP_K)).reshape(a)
    return row_tok, blk_e, n_used, pos


def kernel(x_prompt, x_sample, cache_a_k, cache_a_v, cache_b_k, cache_b_v, rel_bias_table, attn_norm, w_in,
           w_out, attn_sinks, ffn_norm, w_router_group, b_router_group, w_router_expert, b_router_expert,
           w_gate, w_up, w_down, final_norm):
    s = x_prompt.shape[1]
    ns, ts = x_sample.shape[0], x_sample.shape[1]
    table_a = rel_bias_table[:, :H_A]
    table_b = rel_bias_table[:, H_A:]

    w = w_in[0]
    wqa, wka, wva, wqb, wkb, wvb = (w[:, 0:512], w[:, 512:1024], w[:, 1024:1536], w[:, 1536:2048],
                                    w[:, 2048:2176], w[:, 2176:2304])
    wqb = jnp.transpose(wqb.reshape(D_MODEL, KV_B, G_B, HEAD_DIM), (0, 2, 1, 3)).reshape(D_MODEL, 512)
    wp = jnp.concatenate([wka, wva, wqa, wqb, wkb, wvb], axis=1).astype(BF16)
    cscale = jnp.concatenate([jnp.ones((1, 1024), F32), jnp.full((1, 1024), SCALE * LOG2E, F32),
                              jnp.ones((1, 256), F32)], axis=1)
    wo = w_out[0]
    wo_b = jnp.transpose(wo[512:].reshape(KV_B, G_B, HEAD_DIM, D_MODEL), (1, 0, 2, 3)).reshape(512, D_MODEL)
    wo_p = jnp.concatenate([wo[:512], wo_b], axis=0).astype(BF16)
    wr = jnp.concatenate([w_router_group[0],
                          jnp.transpose(w_router_expert[0], (1, 0, 2)).reshape(D_MODEL, N_EXPERTS),
                          jnp.zeros((D_MODEL, LANES - N_GROUPS - N_EXPERTS), F32)], axis=1)
    wr_hi = wr.astype(BF16)
    wr = jnp.concatenate([wr_hi, wr_hi, (wr - wr_hi.astype(F32)).astype(BF16)], axis=0)
    br = jnp.concatenate([b_router_group[0], b_router_expert[0].reshape(N_EXPERTS),
                          jnp.zeros((LANES - N_GROUPS - N_EXPERTS,), F32)]).reshape(1, LANES)
    sinks2 = attn_sinks[0] * LOG2E
    sinks_gk = jnp.transpose(sinks2.reshape(KV_B, G_B), (1, 0)).reshape(H_B)
    sink_rows_p = jnp.repeat(sinks_gk, QB).reshape(G_B, 1, 2 * QB)
    sink_rows_s = jnp.repeat(sinks2, ts).reshape(H_B * ts, 1)
    emat = jnp.tile(jnp.arange(LANES)[:, None] == (jnp.arange(A_WIDTH)[None, :] // HEAD_DIM),
                    (3, 1)).astype(BF16)
    attn_g = attn_norm[0].reshape(1, D_MODEL)
    ffn_g = ffn_norm[0].reshape(1, D_MODEL)

    xp = x_prompt.reshape(s, D_MODEL)
    aperm, qb_p, kvb_p, akv32, bkv32 = _proj_prompt(xp, attn_g, wp, cscale)
    a4 = _attn_a_prompt(aperm, _bias_a_prompt(table_a), emat)
    ob_p = _attn_b_prompt(qb_p, kvb_p, _bias_b_prompt(table_b), sink_rows_p)

    xs = x_sample.reshape(ns * ts, D_MODEL)
    q_s, kv_s = _proj_sample(xs, attn_g, wp, cscale)
    akt = jnp.transpose(cache_a_k[0], (0, 2, 3, 1)).reshape(ns, A_WIDTH, WIN_A)
    avt = jnp.transpose(cache_a_v[0], (0, 2, 3, 1)).reshape(ns, A_WIDTH, WIN_A)
    bkt = jnp.transpose(cache_b_k[0], (0, 2, 3, 1)).reshape(ns, LANES, WIN_B)
    bvt = jnp.transpose(cache_b_v[0], (0, 2, 3, 1)).reshape(ns, LANES, WIN_B)
    oa_s, ob_s = _attn_sample(q_s.reshape(ns, ts, 1024), kv_s.reshape(ns, ts, 1280), akt, avt, bkt, bvt,
                              _bias_a_sample(table_a, ts), _bias_b_sample(table_b, ts), sink_rows_s)
    a4_s = jnp.transpose(oa_s.reshape(ns * ts, 4, LANES), (1, 0, 2))

    x1, xn, route = _out_router(xp, a4, ob_p, xs, a4_s, ob_s.reshape(ns * ts, 512), wo_p, ffn_g, wr, br)
    row_tok, blk_e, n_used, pos = _dispatch(route)
    yb = _experts(blk_e, n_used, _sc_gather_rows(xn, row_tok), w_gate[0], w_up[0], w_down[0])
    y_p, y_s = _combine_norm(x1, _sc_gather_rows(yb, pos), route, final_norm.reshape(1, D_MODEL), s)

    y_prompt = y_p.reshape(1, s, D_MODEL)
    y_sample = y_s.reshape(ns, ts, D_MODEL)
    keep_a, keep_b = min(WIN_A, s), min(WIN_B, s)
    pak = akv32[s - keep_a:, :512].reshape(1, 1, keep_a, H_A, HEAD_DIM)
    pav = akv32[s - keep_a:, 512:].reshape(1, 1, keep_a, H_A, HEAD_DIM)
    pbk = bkv32[s - keep_b:, :128].reshape(1, 1, keep_b, KV_B, HEAD_DIM)
    pbv = bkv32[s - keep_b:, 128:].reshape(1, 1, keep_b, KV_B, HEAD_DIM)
    sak = kv_s[:, 0:512].reshape(1, ns, ts, H_A, HEAD_DIM)
    sav = kv_s[:, 512:1024].reshape(1, ns, ts, H_A, HEAD_DIM)
    sbk = kv_s[:, 1024:1152].reshape(1, ns, ts, KV_B, HEAD_DIM)
    sbv = kv_s[:, 1152:1280].reshape(1, ns, ts, KV_B, HEAD_DIM)
    return (y_prompt, y_sample, pak, pav, pbk, pbv, sak, sav, sbk, sbv)
```

```python
import functools
import math

import jax
import jax.numpy as jnp
import numpy as np
from jax import lax
from jax.experimental import pallas as pl
from jax.experimental.pallas import tpu as pltpu
from jax.experimental.pallas import tpu_sc as plsc

D_MODEL = 1024
HEAD_DIM = 64
H_A = 8
H_B = 8
KV_B = 2
G_B = 4
DILATIONS = (1, 4, 16)
WINDOWS = (128, 512, 2048)
WIN_A = 2048
WIN_B = 128
NUM_BUCKETS = 32
MAX_DISTANCE = 2048
N_GROUPS = 4
EXPERTS_PER_GROUP = 8
N_EXPERTS = 32
TOP_K = 2
D_EXPERT = 512
EPS = 1e-5
SCALE = HEAD_DIM ** -0.5
PAST_LEN = 16384

LANES = 128
SPAN = 2048
QB = 128
NCHUNK = 9
A_WIDTH = H_A * HEAD_DIM
MOE_ROWS = 256
SC_CORES = 2
SC_SUBCORES = 16
SC_WORKERS = SC_CORES * SC_SUBCORES
SC_WINDOW = 64
SC_SCATTER_WINDOW = 32
NEG = -1e30
LOG2E = math.log2(math.e)
B_STEP = 512
VMEM_LIMIT = 56 * 1024 * 1024

F32 = jnp.float32
BF16 = jnp.bfloat16


def _t5_bucket_np(dist):
    dist = np.asarray(dist, np.int64)
    max_exact = NUM_BUCKETS // 2
    d = np.maximum(dist, 1).astype(np.float32)
    ratio = np.log(d / np.float32(max_exact)) / np.float32(math.log(MAX_DISTANCE / max_exact))
    large = max_exact + (ratio * np.float32(NUM_BUCKETS - max_exact)).astype(np.int32)
    large = np.minimum(large, NUM_BUCKETS - 1)
    return np.where(dist < max_exact, dist, large).astype(np.int32)


def _cparams(sem, vmem=VMEM_LIMIT):
    return pltpu.CompilerParams(dimension_semantics=sem, vmem_limit_bytes=vmem)


def _proj_prompt_kernel(x_ref, g_ref, w_ref, cs_ref, aperm_ref, qb_ref, kvb_ref, akv_ref, bkv_ref,
                        h_scr, p_scr):
    n = pl.program_id(1)

    @pl.when(n == 0)
    def _():
        x = x_ref[...]
        ms = jnp.mean(x * x, axis=-1, keepdims=True)
        h_scr[...] = (x * lax.rsqrt(ms + EPS) * g_ref[...]).astype(BF16)

    p = jnp.dot(h_scr[...], w_ref[...], preferred_element_type=F32) * cs_ref[...]

    @pl.when(n < 6)
    def _():
        aperm_ref[0] = p.astype(BF16)
        p_scr[0] = p[:, :LANES]
        p_scr[1] = p[:, LANES:]
        for gi, dil in ((1, 4), (2, 16)):
            rows = SPAN // dil
            for r in range(dil):
                t = jnp.concatenate([p_scr[0, pl.ds(r, rows, stride=dil), :],
                                     p_scr[1, pl.ds(r, rows, stride=dil), :]], axis=1)
                aperm_ref[gi, r * rows:(r + 1) * rows, :] = t.astype(BF16)

    @pl.when(n < 4)
    def _():
        akv_ref[...] = p

    @pl.when(jnp.logical_or(n == 6, n == 7))
    def _():
        qb_ref[...] = p.astype(BF16)

    @pl.when(n == 8)
    def _():
        kvb_ref[...] = p.astype(BF16)
        bkv_ref[...] = p


def _proj_prompt(x, gamma, w, cscale):
    s = x.shape[0]
    nspan = s // SPAN
    return pl.pallas_call(
        _proj_prompt_kernel,
        grid=(nspan, NCHUNK),
        in_specs=[
            pl.BlockSpec((SPAN, D_MODEL), lambda b, n: (b, 0)),
            pl.BlockSpec((1, D_MODEL), lambda b, n: (0, 0)),
            pl.BlockSpec((D_MODEL, 256), lambda b, n: (0, n)),
            pl.BlockSpec((1, 256), lambda b, n: (0, n)),
        ],
        out_specs=[
            pl.BlockSpec((3, SPAN, 256), lambda b, n: (0, b, jnp.minimum(n, 5))),
            pl.BlockSpec((SPAN, 256), lambda b, n: (b, jnp.clip(n - 6, 0, 1))),
            pl.BlockSpec((SPAN, 256), lambda b, n: (b, 0)),
            pl.BlockSpec((SPAN, 256), lambda b, n: (b, jnp.minimum(n, 3))),
            pl.BlockSpec((SPAN, 256), lambda b, n: (b, 0)),
        ],
        out_shape=[
            jax.ShapeDtypeStruct((3, s, 3 * A_WIDTH), BF16),
            jax.ShapeDtypeStruct((s, 512), BF16),
            jax.ShapeDtypeStruct((s, 256), BF16),
            jax.ShapeDtypeStruct((s, 1024), F32),
            jax.ShapeDtypeStruct((s, 256), F32),
        ],
        scratch_shapes=[pltpu.VMEM((SPAN, D_MODEL), BF16), pltpu.VMEM((2, SPAN, LANES), F32)],
        compiler_params=_cparams(("arbitrary", "arbitrary")),
        name="proj_prompt",
    )(x, gamma, w, cscale)


def _proj_sample_kernel(x_ref, g_ref, w_ref, cs_ref, q_ref, kv_ref):
    x = x_ref[...]
    ms = jnp.mean(x * x, axis=-1, keepdims=True)
    h = (x * lax.rsqrt(ms + EPS) * g_ref[...]).astype(BF16)
    p = jnp.dot(h, w_ref[...], preferred_element_type=F32) * cs_ref[...]
    kv_ref[:, :1024] = p[:, :1024]
    kv_ref[:, 1024:] = p[:, 2048:]
    q_ref[...] = p[:, 1024:2048]


def _proj_sample(x, gamma, w, cscale):
    t = x.shape[0]
    tm = 512
    return pl.pallas_call(
        _proj_sample_kernel,
        grid=(t // tm,),
        in_specs=[
            pl.BlockSpec((tm, D_MODEL), lambda i: (i, 0)),
            pl.BlockSpec((1, D_MODEL), lambda i: (0, 0)),
            pl.BlockSpec((D_MODEL, 2304), lambda i: (0, 0)),
            pl.BlockSpec((1, 2304), lambda i: (0, 0)),
        ],
        out_specs=[
            pl.BlockSpec((tm, 1024), lambda i: (i, 0)),
            pl.BlockSpec((tm, 1280), lambda i: (i, 0)),
        ],
        out_shape=[
            jax.ShapeDtypeStruct((t, 1024), F32),
            jax.ShapeDtypeStruct((t, 1280), F32),
        ],
        compiler_params=_cparams(("arbitrary",)),
        name="proj_sample",
    )(x, gamma, w, cscale)


def _spread_heads(w, e3_ref):
    hi = w.astype(BF16)
    r1 = w - hi.astype(F32)
    mid = r1.astype(BF16)
    low = (r1 - mid.astype(F32)).astype(BF16)
    return jnp.dot(jnp.concatenate([hi, mid, low], axis=1), e3_ref[...], preferred_element_type=F32)


def _pair_tile(q2, kk, vv, bias_t, lo, sink=None):
    zero = jnp.zeros_like(q2)
    qq = jnp.concatenate([jnp.where(lo, q2, zero), jnp.where(lo, zero, q2)], axis=0)
    st = lax.dot_general(kk, qq, (((1,), (1,)), ((), ())), preferred_element_type=F32)
    st = st + bias_t
    m = jnp.max(st, axis=0, keepdims=True)
    if sink is not None:
        m = jnp.maximum(m, sink)
    p = jnp.exp2(st - m)
    den = jnp.sum(p, axis=0, keepdims=True)
    if sink is not None:
        den = den + jnp.exp2(sink - m)
    pn = (p * (1.0 / den)).astype(BF16)
    o = lax.dot_general(pn, vv, (((0,), (0,)), ((), ())), preferred_element_type=F32)
    return jnp.where(lo, o[:QB], o[QB:]), m + jnp.log2(den)


def _attn_a_kernel(q_ref, kvc_ref, kvp_ref, bias_ref, e_ref, out_ref, o_scr, st_scr):
    b = pl.program_id(0)
    g = pl.program_id(1)
    nblk = jnp.where(g == 0, 16, jnp.where(g == 1, 4, 1))
    lane = lax.broadcasted_iota(jnp.int32, (QB, LANES), 1)
    lo = lane < HEAD_DIM

    for cb in range(SPAN // QB):
        first = lax.rem(jnp.int32(cb), nblk) == 0
        rows = slice(cb * QB, (cb + 1) * QB)
        prow_c = max(cb - 1, 0) * QB
        prow_p = pl.multiple_of(jnp.where(first, cb + nblk - 1, 0) * QB, QB)
        variant = jnp.logical_and(first, b == 0).astype(jnp.int32)
        stats = []
        for hp in range(4):
            ks = slice(hp * LANES, (hp + 1) * LANES)
            vs = slice(A_WIDTH + hp * LANES, A_WIDTH + (hp + 1) * LANES)
            kp = jnp.where(first, kvp_ref[pl.ds(prow_p, QB), ks], kvc_ref[prow_c:prow_c + QB, ks])
            vp = jnp.where(first, kvp_ref[pl.ds(prow_p, QB), vs], kvc_ref[prow_c:prow_c + QB, vs])
            kk = jnp.concatenate([kp, kvc_ref[rows, ks]], axis=0)
            vv = jnp.concatenate([vp, kvc_ref[rows, vs]], axis=0)
            o, lse = _pair_tile(q_ref[rows, ks], kk, vv, bias_ref[variant, hp], lo)
            o_scr[g, hp, rows, :] = o
            stats += [lse[:, :QB], lse[:, QB:]]
        sm = jnp.concatenate(stats + [jnp.zeros((LANES - H_A, QB), F32)], axis=0)
        st_scr[g, rows, :] = sm.T

    @pl.when(g == 2)
    def _():
        def merge(c, carry):
            r2 = lax.rem(c, 4) * (SPAN // 4) + c // 4
            r3 = pl.multiple_of(c * QB, QB)
            l1 = st_scr[0, pl.ds(c, QB, stride=16), :]
            l2 = st_scr[1, pl.ds(r2, QB, stride=4), :]
            l3 = st_scr[2, pl.ds(r3, QB), :]
            mx = jnp.maximum(jnp.maximum(l1, l2), l3)
            w1 = jnp.exp2(l1 - mx)
            w2 = jnp.exp2(l2 - mx)
            w3 = jnp.exp2(l3 - mx)
            tot = w1 + w2 + w3
            a1 = _spread_heads(w1 / tot, e_ref)
            a2 = _spread_heads(w2 / tot, e_ref)
            a3 = _spread_heads(w3 / tot, e_ref)
            for hp in range(4):
                sl = slice(hp * LANES, (hp + 1) * LANES)
                o1 = o_scr[0, hp, pl.ds(c, QB, stride=16), :]
                o2 = o_scr[1, hp, pl.ds(r2, QB, stride=4), :]
                o3 = o_scr[2, hp, pl.ds(r3, QB), :]
                out_ref[hp, pl.ds(c, QB, stride=16), :] = a1[:, sl] * o1 + a2[:, sl] * o2 + a3[:, sl] * o3
            return carry

        lax.fori_loop(0, 16, merge, 0)


def _attn_a_prompt(aperm, bias_a, emat):
    s = aperm.shape[1]
    nspan = s // SPAN
    return pl.pallas_call(
        _attn_a_kernel,
        grid=(nspan, 3),
        in_specs=[
            pl.BlockSpec((None, SPAN, A_WIDTH), lambda b, g: (g, b, 2)),
            pl.BlockSpec((None, SPAN, 2 * A_WIDTH), lambda b, g: (g, b, 0)),
            pl.BlockSpec((None, SPAN, 2 * A_WIDTH), lambda b, g: (g, jnp.maximum(b - 1, 0), 0)),
            pl.BlockSpec((None, 2, 4, 2 * QB, 2 * QB), lambda b, g: (g, 0, 0, 0, 0)),
            pl.BlockSpec((3 * LANES, A_WIDTH), lambda b, g: (0, 0)),
        ],
        out_specs=pl.BlockSpec((4, SPAN, LANES), lambda b, g: (0, b, 0)),
        out_shape=jax.ShapeDtypeStruct((4, s, LANES), F32),
        scratch_shapes=[pltpu.VMEM((3, 4, SPAN, LANES), F32), pltpu.VMEM((3, SPAN, LANES), F32)],
        compiler_params=_cparams(("arbitrary", "arbitrary")),
        name="attn_a_prompt",
    )(aperm, aperm, aperm, bias_a, emat)


def _attn_b_kernel(q_ref, kvc_ref, kvp_ref, bias_ref, sink_ref, out_ref):
    i = pl.program_id(0)
    lane = lax.broadcasted_iota(jnp.int32, (QB, LANES), 1)
    lo = lane < HEAD_DIM
    variant = (i == 0).astype(jnp.int32)
    for j in range(B_STEP // QB):
        rows = slice(j * QB, (j + 1) * QB)
        if j == 0:
            kp, vp = kvp_ref[:, :LANES], kvp_ref[:, LANES:]
        else:
            kp, vp = kvc_ref[(j - 1) * QB:j * QB, :LANES], kvc_ref[(j - 1) * QB:j * QB, LANES:]
        kk = jnp.concatenate([kp, kvc_ref[rows, :LANES]], axis=0)
        vv = jnp.concatenate([vp, kvc_ref[rows, LANES:]], axis=0)
        for g in range(G_B):
            bias_t = bias_ref[variant, g] if j == 0 else bias_ref[0, g]
            o, _ = _pair_tile(q_ref[rows, g * LANES:(g + 1) * LANES], kk, vv, bias_t, lo, sink=sink_ref[g])
            out_ref[rows, g * LANES:(g + 1) * LANES] = o.astype(BF16)


def _attn_b_prompt(qb, kvb, bias_b, sink_rows):
    s = qb.shape[0]
    per = B_STEP // QB
    return pl.pallas_call(
        _attn_b_kernel,
        grid=(s // B_STEP,),
        in_specs=[
            pl.BlockSpec((B_STEP, 512), lambda i: (i, 0)),
            pl.BlockSpec((B_STEP, 256), lambda i: (i, 0)),
            pl.BlockSpec((QB, 256), lambda i: (jnp.maximum(i * per - 1, 0), 0)),
            pl.BlockSpec((2, G_B, 2 * QB, 2 * QB), lambda i: (0, 0, 0, 0)),
            pl.BlockSpec((G_B, 1, 2 * QB), lambda i: (0, 0, 0)),
        ],
        out_specs=pl.BlockSpec((B_STEP, 512), lambda i: (i, 0)),
        out_shape=jax.ShapeDtypeStruct((s, 512), BF16),
        compiler_params=_cparams(("arbitrary",)),
        name="attn_b_prompt",
    )(qb, kvb, kvb, bias_b, sink_rows)


def _attn_sample_kernel(q_ref, kvn_ref, akt_ref, avt_ref, bkt_ref, bvt_ref, cba_ref, cbb_ref, sink_ref,
                        oa_ref, ob_ref):
    t = q_ref.shape[0]
    q = q_ref[...]
    kvn = kvn_ref[...]
    kvn_p = jnp.concatenate([kvn, jnp.zeros((LANES - t, kvn.shape[1]), F32)], axis=0).astype(BF16)
    lane_a = lax.broadcasted_iota(jnp.int32, (t, A_WIDTH), 1) // HEAD_DIM

    qa = q[:, :A_WIDTH]
    qbd = jnp.concatenate([jnp.where(lane_a == h, qa, 0.0) for h in range(H_A)], axis=0).astype(BF16)
    s_c = jnp.dot(qbd, akt_ref[...].astype(BF16), preferred_element_type=F32)
    s_n = lax.dot_general(qbd, kvn_p[:, :A_WIDTH], (((1,), (1,)), ((), ())), preferred_element_type=F32)
    s = jnp.concatenate([s_c, s_n], axis=1) + cba_ref[...]
    m = jnp.max(s, axis=-1, keepdims=True)
    p = jnp.exp2(s - m)
    l = jnp.sum(p, axis=-1, keepdims=True)
    o_n = jnp.dot(p[:, WIN_A:].astype(BF16), kvn_p[:, A_WIDTH:2 * A_WIDTH], preferred_element_type=F32)
    pc = jnp.concatenate([p[:, :WIN_A], jnp.zeros((LANES - H_A * t, WIN_A), F32)], axis=0).astype(BF16)
    o_t = lax.dot_general(avt_ref[...].astype(BF16), pc, (((1,), (1,)), ((), ())),
                          preferred_element_type=F32)
    o_all = o_t.T[:H_A * t] + o_n
    o_sel = jnp.zeros((t, A_WIDTH), F32)
    l_b = jnp.ones((t, A_WIDTH), F32)
    for h in range(H_A):
        sel = lane_a == h
        o_sel = jnp.where(sel, o_all[h * t:(h + 1) * t], o_sel)
        l_b = jnp.where(sel, l[h * t:(h + 1) * t], l_b)
    oa_ref[...] = o_sel / l_b

    lane_b = lax.broadcasted_iota(jnp.int32, (G_B * t, LANES), 1)
    lo = lane_b < HEAD_DIM
    qb2 = jnp.concatenate([q[:, A_WIDTH + g * LANES:A_WIDTH + (g + 1) * LANES] for g in range(G_B)], axis=0)
    qm = jnp.concatenate([jnp.where(lo, qb2, 0.0), jnp.where(lo, 0.0, qb2)], axis=0).astype(BF16)
    kb_n = kvn_p[:, 2 * A_WIDTH:2 * A_WIDTH + LANES]
    vb_n = kvn_p[:, 2 * A_WIDTH + LANES:]
    sb_c = jnp.dot(qm, bkt_ref[...].astype(BF16), preferred_element_type=F32)
    sb_n = lax.dot_general(qm, kb_n, (((1,), (1,)), ((), ())), preferred_element_type=F32)
    sb = jnp.concatenate([sb_c, sb_n], axis=1) + cbb_ref[...]
    sink = sink_ref[...]
    mb = jnp.maximum(jnp.max(sb, axis=-1, keepdims=True), sink)
    pbb = jnp.exp2(sb - mb)
    den = jnp.sum(pbb, axis=-1, keepdims=True) + jnp.exp2(sink - mb)
    pbb = pbb.astype(BF16)
    ob = lax.dot_general(pbb[:, :WIN_B], bvt_ref[...].astype(BF16), (((1,), (1,)), ((), ())),
                         preferred_element_type=F32)
    ob = (ob + jnp.dot(pbb[:, WIN_B:], vb_n, preferred_element_type=F32)) / den
    half = G_B * t
    lo8 = lo[:t]
    for g in range(G_B):
        ob_ref[:, g * LANES:(g + 1) * LANES] = jnp.where(
            lo8, ob[g * t:(g + 1) * t], ob[half + g * t:half + (g + 1) * t])


def _attn_sample(q3, kvn3, akt, avt, bkt, bvt, cbias_a, cbias_b, sink_rows):
    ns, t = q3.shape[0], q3.shape[1]
    return pl.pallas_call(
        _attn_sample_kernel,
        grid=(ns,),
        in_specs=[
            pl.BlockSpec((None, t, 1024), lambda n: (n, 0, 0)),
            pl.BlockSpec((None, t, 1280), lambda n: (n, 0, 0)),
            pl.BlockSpec((None, A_WIDTH, WIN_A), lambda n: (n, 0, 0)),
            pl.BlockSpec((None, A_WIDTH, WIN_A), lambda n: (n, 0, 0)),
            pl.BlockSpec((None, LANES, WIN_B), lambda n: (n, 0, 0)),
            pl.BlockSpec((None, LANES, WIN_B), lambda n: (n, 0, 0)),
            pl.BlockSpec((H_A * t, WIN_A + LANES), lambda n: (0, 0)),
            pl.BlockSpec((H_B * t, WIN_B + LANES), lambda n: (0, 0)),
            pl.BlockSpec((H_B * t, 1), lambda n: (0, 0)),
        ],
        out_specs=[
            pl.BlockSpec((None, t, 512), lambda n: (n, 0, 0)),
            pl.BlockSpec((None, t, 512), lambda n: (n, 0, 0)),
        ],
        out_shape=[jax.ShapeDtypeStruct((ns, t, 512), F32), jax.ShapeDtypeStruct((ns, t, 512), F32)],
        compiler_params=_cparams(("arbitrary",)),
        name="attn_sample",
    )(q3, kvn3, akt, avt, bkt, bvt, cbias_a, cbias_b, sink_rows)


def _route(logits):
    lane = lax.broadcasted_iota(jnp.int32, logits.shape, 1).astype(F32)
    big = jnp.float32(1 << 20)
    ninf = jnp.float32(-jnp.inf)
    gmask = lane < N_GROUPS
    lg = jnp.where(gmask, logits, ninf)
    gmax = jnp.max(lg, axis=-1, keepdims=True)
    grp = jnp.min(jnp.where(lg == gmax, lane, big), axis=-1, keepdims=True)
    pg_top = 1.0 / jnp.sum(jnp.exp(lg - gmax), axis=-1, keepdims=True)
    e0 = N_GROUPS + grp * EXPERTS_PER_GROUP
    emask = jnp.logical_and(lane >= e0, lane < e0 + EXPERTS_PER_GROUP)
    le = jnp.where(emask, logits, ninf)
    emax = jnp.max(le, axis=-1, keepdims=True)
    esum = jnp.sum(jnp.exp(le - emax), axis=-1, keepdims=True)
    i1 = jnp.min(jnp.where(le == emax, lane, big), axis=-1, keepdims=True)
    le2 = jnp.where(lane == i1, ninf, le)
    e2max = jnp.max(le2, axis=-1, keepdims=True)
    i2 = jnp.min(jnp.where(le2 == e2max, lane, big), axis=-1, keepdims=True)
    p1 = 1.0 / esum
    p2 = jnp.exp(e2max - emax) / esum
    g1 = pg_top * p1 / (p1 + p2)
    g2 = pg_top * p2 / (p1 + p2)
    out = jnp.where(lane == 0, i1 - N_GROUPS, 0.0)
    out = jnp.where(lane == 1, i2 - N_GROUPS, out)
    out = jnp.where(lane == 2, g1, out)
    out = jnp.where(lane == 3, g2, out)
    return out


def _pack_bf16_pairs(x):
    half = x.shape[1] // 2

    def rne(v):
        bits = lax.bitcast_convert_type(v, jnp.int32)
        return bits + 0x7FFF + (lax.shift_right_logical(bits, 16) & 1)

    lo = lax.shift_right_logical(rne(x[:, :half]), 16)
    hi = rne(x[:, half:]) & jnp.int32(-65536)
    return lo | hi


def _unpack_bf16_pairs(w):
    lo = lax.bitcast_convert_type(lax.shift_left(w, 16), F32)
    hi = lax.bitcast_convert_type(w & jnp.int32(-65536), F32)
    return jnp.concatenate([lo, hi], axis=1)


def _out_router_kernel(xp_ref, ap_ref, bp_ref, xs_ref, as_ref, bs_ref, wo_ref, g_ref, wr_ref, br_ref,
                       x1_ref, xn_ref, route_ref, *, prompt_tiles):
    def body(x_ref, a_ref, b_ref):
        mix = jnp.concatenate([a_ref[0], a_ref[1], a_ref[2], a_ref[3]], axis=1).astype(BF16)
        mix = jnp.concatenate([mix, b_ref[...].astype(BF16)], axis=1)
        x1 = x_ref[...] + jnp.dot(mix, wo_ref[...], preferred_element_type=F32)
        x1_ref[...] = x1
        ms = jnp.mean(x1 * x1, axis=-1, keepdims=True)
        xn = x1 * lax.rsqrt(ms + EPS) * g_ref[...]
        xn_ref[...] = _pack_bf16_pairs(xn)
        xh = xn.astype(BF16)
        xl = (xn - xh.astype(F32)).astype(BF16)
        logits = jnp.dot(jnp.concatenate([xh, xl, xh], axis=1), wr_ref[...], preferred_element_type=F32)
        route_ref[...] = _route(logits + br_ref[...])

    i = pl.program_id(0)

    @pl.when(i < prompt_tiles)
    def _():
        body(xp_ref, ap_ref, bp_ref)

    @pl.when(i >= prompt_tiles)
    def _():
        body(xs_ref, as_ref, bs_ref)


def _out_router(xp, a4p, bp, xs, a4s, bs, wo, gamma, wr, br):
    tp, tsm = xp.shape[0], xs.shape[0]
    tm = 512
    npt, nst = tp // tm, tsm // tm
    t = tp + tsm
    pmap = lambda i: (jnp.minimum(i, npt - 1), 0)
    smap = lambda i: (jnp.maximum(i - npt, 0), 0)
    return pl.pallas_call(
        functools.partial(_out_router_kernel, prompt_tiles=npt),
        grid=(npt + nst,),
        in_specs=[
            pl.BlockSpec((tm, D_MODEL), pmap),
            pl.BlockSpec((4, tm, LANES), lambda i: (0, jnp.minimum(i, npt - 1), 0)),
            pl.BlockSpec((tm, 512), pmap),
            pl.BlockSpec((tm, D_MODEL), smap),
            pl.BlockSpec((4, tm, LANES), lambda i: (0, jnp.maximum(i - npt, 0), 0)),
            pl.BlockSpec((tm, 512), smap),
            pl.BlockSpec((D_MODEL, D_MODEL), lambda i: (0, 0)),
            pl.BlockSpec((1, D_MODEL), lambda i: (0, 0)),
            pl.BlockSpec((3 * D_MODEL, LANES), lambda i: (0, 0)),
            pl.BlockSpec((1, LANES), lambda i: (0, 0)),
        ],
        out_specs=[
            pl.BlockSpec((tm, D_MODEL), lambda i: (i, 0)),
            pl.BlockSpec((tm, D_MODEL // 2), lambda i: (i, 0)),
            pl.BlockSpec((tm, LANES), lambda i: (i, 0)),
        ],
        out_shape=[
            jax.ShapeDtypeStruct((t, D_MODEL), F32),
            jax.ShapeDtypeStruct((t, D_MODEL // 2), jnp.int32),
            jax.ShapeDtypeStruct((t, LANES), F32),
        ],
        compiler_params=_cparams(("arbitrary",)),
        name="out_router",
    )(xp, a4p, bp, xs, a4s, bs, wo, gamma, wr, br)


def _sc_gather_rows(table, idx):
    b = idx.shape[0]
    d = table.shape[1]
    per_worker = b // SC_WORKERS
    nwin = per_worker // SC_WINDOW
    assert per_worker * SC_WORKERS == b and nwin * SC_WINDOW == per_worker
    mesh = plsc.VectorSubcoreMesh(core_axis_name="c", subcore_axis_name="s")

    @functools.partial(
        pl.kernel, mesh=mesh,
        out_type=jax.ShapeDtypeStruct((b, d), table.dtype),
        scratch_types=[pltpu.VMEM((SC_WINDOW,), jnp.int32), pltpu.VMEM((SC_WINDOW, d), table.dtype),
                       pltpu.SemaphoreType.DMA],
        name="sc_gather_rows",
    )
    def gather(table_hbm, idx_hbm, out_hbm, idx_v, rows_v, sem):
        wid = lax.axis_index("s") * SC_CORES + lax.axis_index("c")
        base = wid * per_worker

        @pl.loop(0, nwin)
        def _(j):
            off = pl.multiple_of(base + j * SC_WINDOW, SC_WINDOW)
            pltpu.sync_copy(idx_hbm.at[pl.ds(off, SC_WINDOW)], idx_v)
            pltpu.async_copy(table_hbm.at[idx_v], rows_v, sem).wait()
            pltpu.sync_copy(rows_v, out_hbm.at[pl.ds(off, SC_WINDOW)])

    return gather(table, idx)


def _sc_scatter_rows(x, dest2, nrows):
    t, d = x.shape
    per_worker = t // SC_WORKERS
    nwin = per_worker // SC_SCATTER_WINDOW
    assert per_worker * SC_WORKERS == t and nwin * SC_SCATTER_WINDOW == per_worker
    mesh = plsc.VectorSubcoreMesh(core_axis_name="c", subcore_axis_name="s")

    @functools.partial(
        pl.kernel, mesh=mesh,
        out_type=jax.ShapeDtypeStruct((nrows, d), x.dtype),
        scratch_types=[pltpu.VMEM((TOP_K, SC_SCATTER_WINDOW), jnp.int32),
                       pltpu.VMEM((SC_SCATTER_WINDOW, d), x.dtype), pltpu.SemaphoreType.DMA],
        name="sc_scatter_rows",
    )
    def scatter(x_hbm, dest_hbm, out_hbm, idx_v, rows_v, sem):
        wid = lax.axis_index("s") * SC_CORES + lax.axis_index("c")
        base = wid * per_worker

        @pl.loop(0, nwin)
        def _(j):
            off = pl.multiple_of(base + j * SC_SCATTER_WINDOW, SC_SCATTER_WINDOW)
            pltpu.sync_copy(x_hbm.at[pl.ds(off, SC_SCATTER_WINDOW)], rows_v)
            for k in range(TOP_K):
                pltpu.sync_copy(dest_hbm.at[k, pl.ds(off, SC_SCATTER_WINDOW)], idx_v.at[k])
            for k in range(TOP_K):
                pltpu.async_copy(rows_v, out_hbm.at[idx_v.at[k]], sem).wait()

    return scatter(x, dest2)


def _expert_kernel(be_ref, nu_ref, nv_ref, x_ref, wg_ref, wu_ref, wd_ref, o_ref, wg_s, wu_s, wd_s):
    i = pl.program_id(0)
    used = i < nu_ref[0]
    changed = jnp.logical_or(i == 0, be_ref[i] != be_ref[jnp.maximum(i - 1, 0)])

    @pl.when(jnp.logical_and(used, changed))
    def _():
        wg_s[...] = wg_ref[...].astype(BF16)
        wu_s[...] = wu_ref[...].astype(BF16)
        wd_s[...] = wd_ref[...].astype(BF16)

    @pl.when(used)
    def _():
        row = lax.broadcasted_iota(jnp.int32, x_ref.shape, 0)
        x = _unpack_bf16_pairs(jnp.where(row < nv_ref[i], x_ref[...], 0)).astype(BF16)
        gate = jnp.dot(x, wg_s[...], preferred_element_type=F32)
        up = jnp.dot(x, wu_s[...], preferred_element_type=F32)
        h = (gate * jax.nn.sigmoid(gate) * up).astype(BF16)
        o_ref[...] = _pack_bf16_pairs(jnp.dot(h, wd_s[...], preferred_element_type=F32))

    @pl.when(jnp.logical_not(used))
    def _():
        o_ref[...] = jnp.zeros_like(o_ref)


def _experts(blk_e, n_used, nvalid, xb, w_gate, w_up, w_down):
    rows = xb.shape[0]
    nblocks = rows // MOE_ROWS
    grid_spec = pltpu.PrefetchScalarGridSpec(
        num_scalar_prefetch=3,
        grid=(nblocks,),
        in_specs=[
            pl.BlockSpec((MOE_ROWS, D_MODEL // 2), lambda i, be, nu, nv: (i, 0)),
            pl.BlockSpec((None, D_MODEL, D_EXPERT), lambda i, be, nu, nv: (be[i], 0, 0)),
            pl.BlockSpec((None, D_MODEL, D_EXPERT), lambda i, be, nu, nv: (be[i], 0, 0)),
            pl.BlockSpec((None, D_EXPERT, D_MODEL), lambda i, be, nu, nv: (be[i], 0, 0)),
        ],
        out_specs=pl.BlockSpec((MOE_ROWS, D_MODEL // 2), lambda i, be, nu, nv: (i, 0)),
        scratch_shapes=[pltpu.VMEM((D_MODEL, D_EXPERT), BF16), pltpu.VMEM((D_MODEL, D_EXPERT), BF16),
                        pltpu.VMEM((D_EXPERT, D_MODEL), BF16)],
    )
    return pl.pallas_call(
        _expert_kernel,
        grid_spec=grid_spec,
        out_shape=jax.ShapeDtypeStruct((rows, D_MODEL // 2), jnp.int32),
        compiler_params=_cparams(("arbitrary",)),
        name="experts",
    )(blk_e, n_used, nvalid, xb, w_gate, w_up, w_down)


def _combine_kernel(x1_ref, y1_ref, y2_ref, route_ref, g_ref, outp_ref, outs_ref, *, prompt_tiles):
    r = route_ref[...]
    x = (x1_ref[...] + r[:, 2:3] * _unpack_bf16_pairs(y1_ref[...])
         + r[:, 3:4] * _unpack_bf16_pairs(y2_ref[...]))
    ms = jnp.mean(x * x, axis=-1, keepdims=True)
    y = x * lax.rsqrt(ms + EPS) * g_ref[...]
    i = pl.program_id(0)

    @pl.when(i < prompt_tiles)
    def _():
        outp_ref[...] = y

    @pl.when(i >= prompt_tiles)
    def _():
        outs_ref[...] = y


def _combine_norm(x1, ygath, route, gamma, tp):
    t = x1.shape[0]
    tm = 512
    nt, npt = t // tm, tp // tm
    return pl.pallas_call(
        functools.partial(_combine_kernel, prompt_tiles=npt),
        grid=(nt,),
        in_specs=[
            pl.BlockSpec((tm, D_MODEL), lambda i: (i, 0)),
            pl.BlockSpec((tm, D_MODEL // 2), lambda i: (i, 0)),
            pl.BlockSpec((tm, D_MODEL // 2), lambda i: (i + nt, 0)),
            pl.BlockSpec((tm, LANES), lambda i: (i, 0)),
            pl.BlockSpec((1, D_MODEL), lambda i: (0, 0)),
        ],
        out_specs=[
            pl.BlockSpec((tm, D_MODEL), lambda i: (jnp.minimum(i, npt - 1), 0)),
            pl.BlockSpec((tm, D_MODEL), lambda i: (jnp.maximum(i - npt, 0), 0)),
        ],
        out_shape=[jax.ShapeDtypeStruct((tp, D_MODEL), F32), jax.ShapeDtypeStruct((t - tp, D_MODEL), F32)],
        compiler_params=_cparams(("arbitrary",)),
        name="combine_norm",
    )(x1, ygath, ygath, route, gamma)


BAND_PERIOD = 2 * QB + 1


def _band_tiles(h):
    nk = 2 * QB
    lead, heads = h.shape[:-2], h.shape[-2]
    flat = jnp.tile(h, (1,) * (h.ndim - 1) + (nk,))[..., :nk * nk]
    a = flat.reshape(lead + (heads // 2, 2, nk, nk))[..., :QB]
    a = jnp.swapaxes(a, -3, -2).reshape(lead + (heads // 2, nk, nk))
    prev = (np.arange(nk) < QB)[:, None]
    return jnp.stack([a, jnp.where(prev, NEG, a)], axis=len(lead))


def _band_index():
    c = (BAND_PERIOD - np.arange(BAND_PERIOD)) % BAND_PERIOD
    return c, c <= QB


def _bias_a_prompt(table_a):
    c, valid = _band_index()
    idx = np.stack([_t5_bucket_np(d * np.clip(QB - c, 0, QB)) for d in DILATIONS])
    h = jnp.where(valid, jnp.transpose(table_a[idx], (0, 2, 1)) * LOG2E, NEG)
    return _band_tiles(h)


def _bias_b_prompt(table_b):
    c, valid = _band_index()
    valid = valid & (c >= 1)
    h = jnp.where(valid, table_b[_t5_bucket_np(np.clip(QB - c, 0, QB))].T * LOG2E, NEG)
    h = jnp.transpose(h.reshape(KV_B, G_B, BAND_PERIOD), (1, 0, 2)).reshape(H_B, BAND_PERIOD)
    return _band_tiles(h)


def _sample_bias_rows(w, span, t):
    total = span + t
    rows = []
    for i in range(t):
        pad = jnp.full((w.shape[0], LANES - i - 1), NEG, F32)
        rows.append(jnp.concatenate([w[:, t - 1 - i:t - 1 - i + span], w[:, total - 1 - i:], pad], axis=1))
    return jnp.stack(rows, axis=1).reshape(w.shape[0] * t, span + LANES)


def _bias_a_sample(table_a, t):
    dist = np.arange(WIN_A + t)[::-1]
    count = np.zeros(dist.shape, np.int32)
    for w, d in zip(WINDOWS, DILATIONS):
        count += ((dist % d == 0) & (dist <= w)).astype(np.int32)
    log2c = np.log2(np.maximum(count, 1)).astype(np.float32)
    w = jnp.where(count > 0, table_a[_t5_bucket_np(dist)].T * LOG2E + log2c, NEG)
    return _sample_bias_rows(w, WIN_A, t)


def _bias_b_sample(table_b, t):
    dist = np.arange(WIN_B + t)[::-1]
    w = jnp.where(dist < WIN_B, table_b[_t5_bucket_np(dist)].T * LOG2E, NEG)
    return _sample_bias_rows(w, WIN_B, t)


def _dest_kernel(route_ref, tri_ref, dest_ref, cnt_ref, cnt_scr, run_scr, pst_scr):
    ph = pl.program_id(0)
    i = pl.program_id(1)
    tm = route_ref.shape[0]
    r = route_ref[...]
    lane = lax.broadcasted_iota(jnp.int32, (tm, LANES), 1)
    lanef = lane.astype(F32)
    oh0 = lanef == r[:, 0:1]
    oh1 = lanef == r[:, 1:2]
    ohf = jnp.concatenate([oh0, oh1], axis=0).astype(F32)
    colsum = jnp.sum(ohf, axis=0, keepdims=True)

    @pl.when(ph == 0)
    def _():
        @pl.when(i == 0)
        def _():
            cnt_scr[...] = jnp.zeros_like(cnt_scr)

        cnt_scr[...] += colsum

    @pl.when(ph == 1)
    def _():
        @pl.when(i == 0)
        def _():
            cnt = jnp.broadcast_to(cnt_scr[...], (8, LANES))
            padded = jnp.floor((cnt + (MOE_ROWS - 1)) * (1.0 / MOE_ROWS)) * MOE_ROWS
            lane8 = lax.broadcasted_iota(jnp.int32, (8, LANES), 1)
            x = padded
            for sh in (1, 2, 4, 8, 16, 32, 64):
                x = x + jnp.where(lane8 >= sh, pltpu.roll(x, sh, 1), 0.0)
            pst_scr[...] = (x - padded)[0:1]
            run_scr[...] = jnp.zeros_like(run_scr)
            cnt_ref[...] = cnt

        csum = jnp.dot(tri_ref[...], ohf.astype(BF16), preferred_element_type=F32)
        val = csum + (run_scr[...] + pst_scr[...] - 1.0)
        d0 = jnp.sum(jnp.where(oh0, val[:tm], 0.0), axis=-1, keepdims=True)
        d1 = jnp.sum(jnp.where(oh1, val[tm:], 0.0), axis=-1, keepdims=True)
        dest_ref[...] = jnp.where(lane == 0, d0, jnp.where(lane == 1, d1, 0.0)).astype(jnp.int32)
        run_scr[...] += colsum


def _dispatch(route):
    t = route.shape[0]
    tm = 512
    tri = (jnp.arange(2 * tm)[:, None] >= jnp.arange(2 * tm)[None, :]).astype(BF16)
    dest, cnt = pl.pallas_call(
        _dest_kernel,
        grid=(2, t // tm),
        in_specs=[pl.BlockSpec((tm, LANES), lambda ph, i: (i, 0)),
                  pl.BlockSpec((2 * tm, 2 * tm), lambda ph, i: (0, 0))],
        out_specs=[pl.BlockSpec((tm, LANES), lambda ph, i: (i * ph, 0)),
                   pl.BlockSpec((8, LANES), lambda ph, i: (0, 0))],
        out_shape=[jax.ShapeDtypeStruct((t, LANES), jnp.int32), jax.ShapeDtypeStruct((8, LANES), F32)],
        scratch_shapes=[pltpu.VMEM((1, LANES), F32), pltpu.VMEM((1, LANES), F32), pltpu.VMEM((1, LANES), F32)],
        compiler_params=_cparams(("arbitrary", "arbitrary")),
        name="moe_dest",
    )(route, tri)
    counts = cnt[0, :N_EXPERTS].astype(jnp.int32)
    nblk_e = (counts + MOE_ROWS - 1) // MOE_ROWS
    bend = jnp.cumsum(nblk_e)
    nblocks = -(-t * TOP_K // MOE_ROWS) + N_EXPERTS
    blk = jnp.arange(nblocks, dtype=jnp.int32)
    blk_e = jnp.minimum(jnp.sum((bend[None, :] <= blk[:, None]).astype(jnp.int32), axis=1), N_EXPERTS - 1)
    within = blk - (bend - nblk_e)[blk_e]
    nvalid = jnp.clip(counts[blk_e] - within * MOE_ROWS, 0, MOE_ROWS).astype(jnp.int32)
    n_used = bend[-1].astype(jnp.int32).reshape(1)
    dest2 = jnp.transpose(dest[:, :TOP_K])
    return dest2, blk_e.astype(jnp.int32), n_used, nvalid


def kernel(x_prompt, x_sample, cache_a_k, cache_a_v, cache_b_k, cache_b_v, rel_bias_table, attn_norm, w_in,
           w_out, attn_sinks, ffn_norm, w_router_group, b_router_group, w_router_expert, b_router_expert,
           w_gate, w_up, w_down, final_norm):
    s = x_prompt.shape[1]
    ns, ts = x_sample.shape[0], x_sample.shape[1]
    table_a = rel_bias_table[:, :H_A]
    table_b = rel_bias_table[:, H_A:]

    w = w_in[0]
    wqa, wka, wva, wqb, wkb, wvb = (w[:, 0:512], w[:, 512:1024], w[:, 1024:1536], w[:, 1536:2048],
                                    w[:, 2048:2176], w[:, 2176:2304])
    wqb = jnp.transpose(wqb.reshape(D_MODEL, KV_B, G_B, HEAD_DIM), (0, 2, 1, 3)).reshape(D_MODEL, 512)
    wp = jnp.concatenate([wka, wva, wqa, wqb, wkb, wvb], axis=1).astype(BF16)
    cscale = jnp.concatenate([jnp.ones((1, 1024), F32), jnp.full((1, 1024), SCALE * LOG2E, F32),
                              jnp.ones((1, 256), F32)], axis=1)
    wo = w_out[0]
    wo_b = jnp.transpose(wo[512:].reshape(KV_B, G_B, HEAD_DIM, D_MODEL), (1, 0, 2, 3)).reshape(512, D_MODEL)
    wo_p = jnp.concatenate([wo[:512], wo_b], axis=0).astype(BF16)
    wr = jnp.concatenate([w_router_group[0],
                          jnp.transpose(w_router_expert[0], (1, 0, 2)).reshape(D_MODEL, N_EXPERTS),
                          jnp.zeros((D_MODEL, LANES - N_GROUPS - N_EXPERTS), F32)], axis=1)
    wr_hi = wr.astype(BF16)
    wr = jnp.concatenate([wr_hi, wr_hi, (wr - wr_hi.astype(F32)).astype(BF16)], axis=0)
    br = jnp.concatenate([b_router_group[0], b_router_expert[0].reshape(N_EXPERTS),
                          jnp.zeros((LANES - N_GROUPS - N_EXPERTS,), F32)]).reshape(1, LANES)
    sinks2 = attn_sinks[0] * LOG2E
    sinks_gk = jnp.transpose(sinks2.reshape(KV_B, G_B), (1, 0)).reshape(H_B)
    sink_rows_p = jnp.repeat(sinks_gk, QB).reshape(G_B, 1, 2 * QB)
    sink_rows_s = jnp.repeat(sinks2, ts).reshape(H_B * ts, 1)
    emat = jnp.tile(jnp.arange(LANES)[:, None] == (jnp.arange(A_WIDTH)[None, :] // HEAD_DIM),
                    (3, 1)).astype(BF16)
    attn_g = attn_norm[0].reshape(1, D_MODEL)
    ffn_g = ffn_norm[0].reshape(1, D_MODEL)

    xp = x_prompt.reshape(s, D_MODEL)
    aperm, qb_p, kvb_p, akv32, bkv32 = _proj_prompt(xp, attn_g, wp, cscale)
    a4 = _attn_a_prompt(aperm, _bias_a_prompt(table_a), emat)
    ob_p = _attn_b_prompt(qb_p, kvb_p, _bias_b_prompt(table_b), sink_rows_p)

    xs = x_sample.reshape(ns * ts, D_MODEL)
    q_s, kv_s = _proj_sample(xs, attn_g, wp, cscale)
    akt = jnp.transpose(cache_a_k[0], (0, 2, 3, 1)).reshape(ns, A_WIDTH, WIN_A)
    avt = jnp.transpose(cache_a_v[0], (0, 2, 3, 1)).reshape(ns, A_WIDTH, WIN_A)
    bkt = jnp.transpose(cache_b_k[0], (0, 2, 3, 1)).reshape(ns, LANES, WIN_B)
    bvt = jnp.transpose(cache_b_v[0], (0, 2, 3, 1)).reshape(ns, LANES, WIN_B)
    oa_s, ob_s = _attn_sample(q_s.reshape(ns, ts, 1024), kv_s.reshape(ns, ts, 1280), akt, avt, bkt, bvt,
                              _bias_a_sample(table_a, ts), _bias_b_sample(table_b, ts), sink_rows_s)
    a4_s = jnp.transpose(oa_s.reshape(ns * ts, 4, LANES), (1, 0, 2))

    x1, xn, route = _out_router(xp, a4, ob_p, xs, a4_s, ob_s.reshape(ns * ts, 512), wo_p, ffn_g, wr, br)
    dest2, blk_e, n_used, nvalid = _dispatch(route)
    xb = _sc_scatter_rows(xn, dest2, blk_e.shape[0] * MOE_ROWS)
    yb = _experts(blk_e, n_used, nvalid, xb, w_gate[0], w_up[0], w_down[0])
    y_p, y_s = _combine_norm(x1, _sc_gather_rows(yb, dest2.reshape(-1)), route, final_norm.reshape(1, D_MODEL), s)

    y_prompt = y_p.reshape(1, s, D_MODEL)
    y_sample = y_s.reshape(ns, ts, D_MODEL)
    keep_a, keep_b = min(WIN_A, s), min(WIN_B, s)
    pak = akv32[s - keep_a:, :512].reshape(1, 1, keep_a, H_A, HEAD_DIM)
    pav = akv32[s - keep_a:, 512:].reshape(1, 1, keep_a, H_A, HEAD_DIM)
    pbk = bkv32[s - keep_b:, :128].reshape(1, 1, keep_b, KV_B, HEAD_DIM)
    pbv = bkv32[s - keep_b:, 128:].reshape(1, 1, keep_b, KV_B, HEAD_DIM)
    sak = kv_s[:, 0:512].reshape(1, ns, ts, H_A, HEAD_DIM)
    sav = kv_s[:, 512:1024].reshape(1, ns, ts, H_A, HEAD_DIM)
    sbk = kv_s[:, 1024:1152].reshape(1, ns, ts, KV_B, HEAD_DIM)
    sbv = kv_s[:, 1152:1280].reshape(1, ns, ts, KV_B, HEAD_DIM)
    return (y_prompt, y_sample, pak, pav, pbk, pbv, sak, sav, sbk, sbv)
```

```python
import functools
import math

import jax
import jax.numpy as jnp
import numpy as np
from jax import lax
from jax.experimental import pallas as pl
from jax.experimental.pallas import tpu as pltpu
from jax.experimental.pallas import tpu_sc as plsc

D_MODEL = 1024
HEAD_DIM = 64
H_A = 8
H_B = 8
KV_B = 2
G_B = 4
DILATIONS = (1, 4, 16)
WINDOWS = (128, 512, 2048)
WIN_A = 2048
WIN_B = 128
NUM_BUCKETS = 32
MAX_DISTANCE = 2048
N_GROUPS = 4
EXPERTS_PER_GROUP = 8
N_EXPERTS = 32
TOP_K = 2
D_EXPERT = 512
EPS = 1e-5
SCALE = HEAD_DIM ** -0.5
PAST_LEN = 16384

LANES = 128
SPAN = 2048
QB = 128
NCHUNK = 9
A_WIDTH = H_A * HEAD_DIM
MOE_ROWS = 256
SC_CORES = 2
SC_SUBCORES = 16
SC_WORKERS = SC_CORES * SC_SUBCORES
SC_WINDOW = 64
SC_SCATTER_WINDOW = 32
NEG = -1e30
LOG2E = math.log2(math.e)
B_STEP = 512
VMEM_LIMIT = 56 * 1024 * 1024

F32 = jnp.float32
BF16 = jnp.bfloat16


def _t5_bucket_np(dist):
    dist = np.asarray(dist, np.int64)
    max_exact = NUM_BUCKETS // 2
    d = np.maximum(dist, 1).astype(np.float32)
    ratio = np.log(d / np.float32(max_exact)) / np.float32(math.log(MAX_DISTANCE / max_exact))
    large = max_exact + (ratio * np.float32(NUM_BUCKETS - max_exact)).astype(np.int32)
    large = np.minimum(large, NUM_BUCKETS - 1)
    return np.where(dist < max_exact, dist, large).astype(np.int32)


def _cparams(sem, vmem=VMEM_LIMIT):
    return pltpu.CompilerParams(dimension_semantics=sem, vmem_limit_bytes=vmem)


def _proj_prompt_kernel(x_ref, g_ref, w_ref, cs_ref, aperm_ref, qb_ref, kvb_ref, akv_ref, bkv_ref,
                        h_scr, p_scr):
    n = pl.program_id(1)

    @pl.when(n == 0)
    def _():
        x = x_ref[...]
        ms = jnp.mean(x * x, axis=-1, keepdims=True)
        h_scr[...] = (x * lax.rsqrt(ms + EPS) * g_ref[...]).astype(BF16)

    p = jnp.dot(h_scr[...], w_ref[...], preferred_element_type=F32) * cs_ref[...]

    @pl.when(n < 6)
    def _():
        aperm_ref[0] = p.astype(BF16)
        p_scr[0] = p[:, :LANES]
        p_scr[1] = p[:, LANES:]
        for gi, dil in ((1, 4), (2, 16)):
            rows = SPAN // dil
            for r in range(dil):
                t = jnp.concatenate([p_scr[0, pl.ds(r, rows, stride=dil), :],
                                     p_scr[1, pl.ds(r, rows, stride=dil), :]], axis=1)
                aperm_ref[gi, r * rows:(r + 1) * rows, :] = t.astype(BF16)

    @pl.when(n < 4)
    def _():
        akv_ref[...] = p

    @pl.when(jnp.logical_or(n == 6, n == 7))
    def _():
        qb_ref[...] = p.astype(BF16)

    @pl.when(n == 8)
    def _():
        kvb_ref[...] = p.astype(BF16)
        bkv_ref[...] = p


def _proj_prompt(x, gamma, w, cscale):
    s = x.shape[0]
    nspan = s // SPAN
    return pl.pallas_call(
        _proj_prompt_kernel,
        grid=(nspan, NCHUNK),
        in_specs=[
            pl.BlockSpec((SPAN, D_MODEL), lambda b, n: (b, 0)),
            pl.BlockSpec((1, D_MODEL), lambda b, n: (0, 0)),
            pl.BlockSpec((D_MODEL, 256), lambda b, n: (0, n)),
            pl.BlockSpec((1, 256), lambda b, n: (0, n)),
        ],
        out_specs=[
            pl.BlockSpec((3, SPAN, 256), lambda b, n: (0, b, jnp.minimum(n, 5))),
            pl.BlockSpec((SPAN, 256), lambda b, n: (b, jnp.clip(n - 6, 0, 1))),
            pl.BlockSpec((SPAN, 256), lambda b, n: (b, 0)),
            pl.BlockSpec((SPAN, 256), lambda b, n: (b, jnp.minimum(n, 3))),
            pl.BlockSpec((SPAN, 256), lambda b, n: (b, 0)),
        ],
        out_shape=[
            jax.ShapeDtypeStruct((3, s, 3 * A_WIDTH), BF16),
            jax.ShapeDtypeStruct((s, 512), BF16),
            jax.ShapeDtypeStruct((s, 256), BF16),
            jax.ShapeDtypeStruct((s, 1024), F32),
            jax.ShapeDtypeStruct((s, 256), F32),
        ],
        scratch_shapes=[pltpu.VMEM((SPAN, D_MODEL), BF16), pltpu.VMEM((2, SPAN, LANES), F32)],
        compiler_params=_cparams(("arbitrary", "arbitrary")),
        name="proj_prompt",
    )(x, gamma, w, cscale)


def _proj_sample_kernel(x_ref, g_ref, w_ref, cs_ref, q_ref, kv_ref):
    x = x_ref[...]
    ms = jnp.mean(x * x, axis=-1, keepdims=True)
    h = (x * lax.rsqrt(ms + EPS) * g_ref[...]).astype(BF16)
    p = jnp.dot(h, w_ref[...], preferred_element_type=F32) * cs_ref[...]
    kv_ref[:, :1024] = p[:, :1024]
    kv_ref[:, 1024:] = p[:, 2048:]
    q_ref[...] = p[:, 1024:2048]


def _proj_sample(x, gamma, w, cscale):
    t = x.shape[0]
    tm = 512
    return pl.pallas_call(
        _proj_sample_kernel,
        grid=(t // tm,),
        in_specs=[
            pl.BlockSpec((tm, D_MODEL), lambda i: (i, 0)),
            pl.BlockSpec((1, D_MODEL), lambda i: (0, 0)),
            pl.BlockSpec((D_MODEL, 2304), lambda i: (0, 0)),
            pl.BlockSpec((1, 2304), lambda i: (0, 0)),
        ],
        out_specs=[
            pl.BlockSpec((tm, 1024), lambda i: (i, 0)),
            pl.BlockSpec((tm, 1280), lambda i: (i, 0)),
        ],
        out_shape=[
            jax.ShapeDtypeStruct((t, 1024), F32),
            jax.ShapeDtypeStruct((t, 1280), F32),
        ],
        compiler_params=_cparams(("arbitrary",)),
        name="proj_sample",
    )(x, gamma, w, cscale)


def _spread_heads(w, e3_ref):
    hi = w.astype(BF16)
    r1 = w - hi.astype(F32)
    mid = r1.astype(BF16)
    low = (r1 - mid.astype(F32)).astype(BF16)
    return jnp.dot(jnp.concatenate([hi, mid, low], axis=1), e3_ref[...], preferred_element_type=F32)


def _pair_tile(q2, kk, vv, bias_t, lo, sink=None):
    zero = jnp.zeros_like(q2)
    qq = jnp.concatenate([jnp.where(lo, q2, zero), jnp.where(lo, zero, q2)], axis=0)
    st = lax.dot_general(kk, qq, (((1,), (1,)), ((), ())), preferred_element_type=F32)
    st = st + bias_t
    m = jnp.max(st, axis=0, keepdims=True)
    if sink is not None:
        m = jnp.maximum(m, sink)
    p = jnp.exp2(st - m)
    den = jnp.sum(p, axis=0, keepdims=True)
    if sink is not None:
        den = den + jnp.exp2(sink - m)
    pn = (p * (1.0 / den)).astype(BF16)
    o = lax.dot_general(pn, vv, (((0,), (0,)), ((), ())), preferred_element_type=F32)
    return jnp.where(lo, o[:QB], o[QB:]), m + jnp.log2(den)


def _attn_a_kernel(q_ref, kvc_ref, kvp_ref, bias_ref, e_ref, out_ref, o_scr, st_scr):
    b = pl.program_id(0)
    g = pl.program_id(1)
    nblk = jnp.where(g == 0, 16, jnp.where(g == 1, 4, 1))
    lane = lax.broadcasted_iota(jnp.int32, (QB, LANES), 1)
    lo = lane < HEAD_DIM

    for cb in range(SPAN // QB):
        first = lax.rem(jnp.int32(cb), nblk) == 0
        rows = slice(cb * QB, (cb + 1) * QB)
        prow_c = max(cb - 1, 0) * QB
        prow_p = pl.multiple_of(jnp.where(first, cb + nblk - 1, 0) * QB, QB)
        variant = jnp.logical_and(first, b == 0).astype(jnp.int32)
        stats = []
        for hp in range(4):
            ks = slice(hp * LANES, (hp + 1) * LANES)
            vs = slice(A_WIDTH + hp * LANES, A_WIDTH + (hp + 1) * LANES)
            kp = jnp.where(first, kvp_ref[pl.ds(prow_p, QB), ks], kvc_ref[prow_c:prow_c + QB, ks])
            vp = jnp.where(first, kvp_ref[pl.ds(prow_p, QB), vs], kvc_ref[prow_c:prow_c + QB, vs])
            kk = jnp.concatenate([kp, kvc_ref[rows, ks]], axis=0)
            vv = jnp.concatenate([vp, kvc_ref[rows, vs]], axis=0)
            o, lse = _pair_tile(q_ref[rows, ks], kk, vv, bias_ref[variant, hp], lo)
            o_scr[g, hp, rows, :] = o
            stats += [lse[:, :QB], lse[:, QB:]]
        sm = jnp.concatenate(stats + [jnp.zeros((LANES - H_A, QB), F32)], axis=0)
        st_scr[g, rows, :] = sm.T

    @pl.when(g == 2)
    def _():
        def merge(c, carry):
            r2 = lax.rem(c, 4) * (SPAN // 4) + c // 4
            r3 = pl.multiple_of(c * QB, QB)
            l1 = st_scr[0, pl.ds(c, QB, stride=16), :]
            l2 = st_scr[1, pl.ds(r2, QB, stride=4), :]
            l3 = st_scr[2, pl.ds(r3, QB), :]
            mx = jnp.maximum(jnp.maximum(l1, l2), l3)
            w1 = jnp.exp2(l1 - mx)
            w2 = jnp.exp2(l2 - mx)
            w3 = jnp.exp2(l3 - mx)
            tot = w1 + w2 + w3
            a1 = _spread_heads(w1 / tot, e_ref)
            a2 = _spread_heads(w2 / tot, e_ref)
            a3 = _spread_heads(w3 / tot, e_ref)
            for hp in range(4):
                sl = slice(hp * LANES, (hp + 1) * LANES)
                o1 = o_scr[0, hp, pl.ds(c, QB, stride=16), :]
                o2 = o_scr[1, hp, pl.ds(r2, QB, stride=4), :]
                o3 = o_scr[2, hp, pl.ds(r3, QB), :]
                out_ref[hp, pl.ds(c, QB, stride=16), :] = a1[:, sl] * o1 + a2[:, sl] * o2 + a3[:, sl] * o3
            return carry

        lax.fori_loop(0, 16, merge, 0)


def _attn_a_prompt(aperm, bias_a, emat):
    s = aperm.shape[1]
    nspan = s // SPAN
    return pl.pallas_call(
        _attn_a_kernel,
        grid=(nspan, 3),
        in_specs=[
            pl.BlockSpec((None, SPAN, A_WIDTH), lambda b, g: (g, b, 2)),
            pl.BlockSpec((None, SPAN, 2 * A_WIDTH), lambda b, g: (g, b, 0)),
            pl.BlockSpec((None, SPAN, 2 * A_WIDTH), lambda b, g: (g, jnp.maximum(b - 1, 0), 0)),
            pl.BlockSpec((None, 2, 4, 2 * QB, 2 * QB), lambda b, g: (g, 0, 0, 0, 0)),
            pl.BlockSpec((3 * LANES, A_WIDTH), lambda b, g: (0, 0)),
        ],
        out_specs=pl.BlockSpec((4, SPAN, LANES), lambda b, g: (0, b, 0)),
        out_shape=jax.ShapeDtypeStruct((4, s, LANES), F32),
        scratch_shapes=[pltpu.VMEM((3, 4, SPAN, LANES), F32), pltpu.VMEM((3, SPAN, LANES), F32)],
        compiler_params=_cparams(("arbitrary", "arbitrary")),
        name="attn_a_prompt",
    )(aperm, aperm, aperm, bias_a, emat)


def _attn_b_kernel(q_ref, kvc_ref, kvp_ref, bias_ref, sink_ref, out_ref):
    i = pl.program_id(0)
    lane = lax.broadcasted_iota(jnp.int32, (QB, LANES), 1)
    lo = lane < HEAD_DIM
    variant = (i == 0).astype(jnp.int32)
    for j in range(B_STEP // QB):
        rows = slice(j * QB, (j + 1) * QB)
        if j == 0:
            kp, vp = kvp_ref[:, :LANES], kvp_ref[:, LANES:]
        else:
            kp, vp = kvc_ref[(j - 1) * QB:j * QB, :LANES], kvc_ref[(j - 1) * QB:j * QB, LANES:]
        kk = jnp.concatenate([kp, kvc_ref[rows, :LANES]], axis=0)
        vv = jnp.concatenate([vp, kvc_ref[rows, LANES:]], axis=0)
        for g in range(G_B):
            bias_t = bias_ref[variant, g] if j == 0 else bias_ref[0, g]
            o, _ = _pair_tile(q_ref[rows, g * LANES:(g + 1) * LANES], kk, vv, bias_t, lo, sink=sink_ref[g])
            out_ref[rows, g * LANES:(g + 1) * LANES] = o.astype(BF16)


def _attn_b_prompt(qb, kvb, bias_b, sink_rows):
    s = qb.shape[0]
    per = B_STEP // QB
    return pl.pallas_call(
        _attn_b_kernel,
        grid=(s // B_STEP,),
        in_specs=[
            pl.BlockSpec((B_STEP, 512), lambda i: (i, 0)),
            pl.BlockSpec((B_STEP, 256), lambda i: (i, 0)),
            pl.BlockSpec((QB, 256), lambda i: (jnp.maximum(i * per - 1, 0), 0)),
            pl.BlockSpec((2, G_B, 2 * QB, 2 * QB), lambda i: (0, 0, 0, 0)),
            pl.BlockSpec((G_B, 1, 2 * QB), lambda i: (0, 0, 0)),
        ],
        out_specs=pl.BlockSpec((B_STEP, 512), lambda i: (i, 0)),
        out_shape=jax.ShapeDtypeStruct((s, 512), BF16),
        compiler_params=_cparams(("arbitrary",)),
        name="attn_b_prompt",
    )(qb, kvb, kvb, bias_b, sink_rows)


def _attn_sample_kernel(q_ref, kvn_ref, akt_ref, avt_ref, bkt_ref, bvt_ref, cba_ref, cbb_ref, sink_ref,
                        oa_ref, ob_ref):
    t = q_ref.shape[0]
    q = q_ref[...]
    kvn = kvn_ref[...]
    kvn_p = jnp.concatenate([kvn, jnp.zeros((LANES - t, kvn.shape[1]), F32)], axis=0).astype(BF16)
    lane_a = lax.broadcasted_iota(jnp.int32, (t, A_WIDTH), 1) // HEAD_DIM

    qa = q[:, :A_WIDTH]
    qbd = jnp.concatenate([jnp.where(lane_a == h, qa, 0.0) for h in range(H_A)], axis=0).astype(BF16)
    s_c = jnp.dot(qbd, akt_ref[...].astype(BF16), preferred_element_type=F32)
    s_n = lax.dot_general(qbd, kvn_p[:, :A_WIDTH], (((1,), (1,)), ((), ())), preferred_element_type=F32)
    s = jnp.concatenate([s_c, s_n], axis=1) + cba_ref[...]
    m = jnp.max(s, axis=-1, keepdims=True)
    p = jnp.exp2(s - m)
    l = jnp.sum(p, axis=-1, keepdims=True)
    o_n = jnp.dot(p[:, WIN_A:].astype(BF16), kvn_p[:, A_WIDTH:2 * A_WIDTH], preferred_element_type=F32)
    pc = jnp.concatenate([p[:, :WIN_A], jnp.zeros((LANES - H_A * t, WIN_A), F32)], axis=0).astype(BF16)
    o_t = lax.dot_general(avt_ref[...].astype(BF16), pc, (((1,), (1,)), ((), ())),
                          preferred_element_type=F32)
    o_all = o_t.T[:H_A * t] + o_n
    o_sel = jnp.zeros((t, A_WIDTH), F32)
    l_b = jnp.ones((t, A_WIDTH), F32)
    for h in range(H_A):
        sel = lane_a == h
        o_sel = jnp.where(sel, o_all[h * t:(h + 1) * t], o_sel)
        l_b = jnp.where(sel, l[h * t:(h + 1) * t], l_b)
    oa_ref[...] = o_sel / l_b

    lane_b = lax.broadcasted_iota(jnp.int32, (G_B * t, LANES), 1)
    lo = lane_b < HEAD_DIM
    qb2 = jnp.concatenate([q[:, A_WIDTH + g * LANES:A_WIDTH + (g + 1) * LANES] for g in range(G_B)], axis=0)
    qm = jnp.concatenate([jnp.where(lo, qb2, 0.0), jnp.where(lo, 0.0, qb2)], axis=0).astype(BF16)
    kb_n = kvn_p[:, 2 * A_WIDTH:2 * A_WIDTH + LANES]
    vb_n = kvn_p[:, 2 * A_WIDTH + LANES:]
    sb_c = jnp.dot(qm, bkt_ref[...].astype(BF16), preferred_element_type=F32)
    sb_n = lax.dot_general(qm, kb_n, (((1,), (1,)), ((), ())), preferred_element_type=F32)
    sb = jnp.concatenate([sb_c, sb_n], axis=1) + cbb_ref[...]
    sink = sink_ref[...]
    mb = jnp.maximum(jnp.max(sb, axis=-1, keepdims=True), sink)
    pbb = jnp.exp2(sb - mb)
    den = jnp.sum(pbb, axis=-1, keepdims=True) + jnp.exp2(sink - mb)
    pbb = pbb.astype(BF16)
    ob = lax.dot_general(pbb[:, :WIN_B], bvt_ref[...].astype(BF16), (((1,), (1,)), ((), ())),
                         preferred_element_type=F32)
    ob = (ob + jnp.dot(pbb[:, WIN_B:], vb_n, preferred_element_type=F32)) / den
    half = G_B * t
    lo8 = lo[:t]
    for g in range(G_B):
        ob_ref[:, g * LANES:(g + 1) * LANES] = jnp.where(
            lo8, ob[g * t:(g + 1) * t], ob[half + g * t:half + (g + 1) * t])


def _attn_sample(q3, kvn3, akt, avt, bkt, bvt, cbias_a, cbias_b, sink_rows):
    ns, t = q3.shape[0], q3.shape[1]
    return pl.pallas_call(
        _attn_sample_kernel,
        grid=(ns,),
        in_specs=[
            pl.BlockSpec((None, t, 1024), lambda n: (n, 0, 0)),
            pl.BlockSpec((None, t, 1280), lambda n: (n, 0, 0)),
            pl.BlockSpec((None, A_WIDTH, WIN_A), lambda n: (n, 0, 0)),
            pl.BlockSpec((None, A_WIDTH, WIN_A), lambda n: (n, 0, 0)),
            pl.BlockSpec((None, LANES, WIN_B), lambda n: (n, 0, 0)),
            pl.BlockSpec((None, LANES, WIN_B), lambda n: (n, 0, 0)),
            pl.BlockSpec((H_A * t, WIN_A + LANES), lambda n: (0, 0)),
            pl.BlockSpec((H_B * t, WIN_B + LANES), lambda n: (0, 0)),
            pl.BlockSpec((H_B * t, 1), lambda n: (0, 0)),
        ],
        out_specs=[
            pl.BlockSpec((None, t, 512), lambda n: (n, 0, 0)),
            pl.BlockSpec((None, t, 512), lambda n: (n, 0, 0)),
        ],
        out_shape=[jax.ShapeDtypeStruct((ns, t, 512), F32), jax.ShapeDtypeStruct((ns, t, 512), F32)],
        compiler_params=_cparams(("arbitrary",)),
        name="attn_sample",
    )(q3, kvn3, akt, avt, bkt, bvt, cbias_a, cbias_b, sink_rows)


def _route(logits):
    lane = lax.broadcasted_iota(jnp.int32, logits.shape, 1).astype(F32)
    big = jnp.float32(1 << 20)
    ninf = jnp.float32(-jnp.inf)
    gmask = lane < N_GROUPS
    lg = jnp.where(gmask, logits, ninf)
    gmax = jnp.max(lg, axis=-1, keepdims=True)
    grp = jnp.min(jnp.where(lg == gmax, lane, big), axis=-1, keepdims=True)
    pg_top = 1.0 / jnp.sum(jnp.exp(lg - gmax), axis=-1, keepdims=True)
    e0 = N_GROUPS + grp * EXPERTS_PER_GROUP
    emask = jnp.logical_and(lane >= e0, lane < e0 + EXPERTS_PER_GROUP)
    le = jnp.where(emask, logits, ninf)
    emax = jnp.max(le, axis=-1, keepdims=True)
    esum = jnp.sum(jnp.exp(le - emax), axis=-1, keepdims=True)
    i1 = jnp.min(jnp.where(le == emax, lane, big), axis=-1, keepdims=True)
    le2 = jnp.where(lane == i1, ninf, le)
    e2max = jnp.max(le2, axis=-1, keepdims=True)
    i2 = jnp.min(jnp.where(le2 == e2max, lane, big), axis=-1, keepdims=True)
    p1 = 1.0 / esum
    p2 = jnp.exp(e2max - emax) / esum
    g1 = pg_top * p1 / (p1 + p2)
    g2 = pg_top * p2 / (p1 + p2)
    out = jnp.where(lane == 0, i1 - N_GROUPS, 0.0)
    out = jnp.where(lane == 1, i2 - N_GROUPS, out)
    out = jnp.where(lane == 2, g1, out)
    out = jnp.where(lane == 3, g2, out)
    return out


def _pack_bf16_pairs(x):
    half = x.shape[1] // 2

    def rne(v):
        bits = lax.bitcast_convert_type(v, jnp.int32)
        return bits + 0x7FFF + (lax.shift_right_logical(bits, 16) & 1)

    lo = lax.shift_right_logical(rne(x[:, :half]), 16)
    hi = rne(x[:, half:]) & jnp.int32(-65536)
    return lo | hi


def _unpack_bf16_pairs(w):
    lo = lax.bitcast_convert_type(lax.shift_left(w, 16), F32)
    hi = lax.bitcast_convert_type(w & jnp.int32(-65536), F32)
    return jnp.concatenate([lo, hi], axis=1)


def _out_router_kernel(xp_ref, ap_ref, bp_ref, xs_ref, as_ref, bs_ref, wo_ref, g_ref, wr_ref, br_ref,
                       x1_ref, xn_ref, route_ref, cnt_ref, *, prompt_tiles):
    i = pl.program_id(0)

    @pl.when(i == 0)
    def _():
        cnt_ref[...] = jnp.zeros_like(cnt_ref)

    def body(x_ref, a_ref, b_ref):
        mix = jnp.concatenate([a_ref[0], a_ref[1], a_ref[2], a_ref[3]], axis=1).astype(BF16)
        mix = jnp.concatenate([mix, b_ref[...].astype(BF16)], axis=1)
        x1 = x_ref[...] + jnp.dot(mix, wo_ref[...], preferred_element_type=F32)
        x1_ref[...] = x1
        ms = jnp.mean(x1 * x1, axis=-1, keepdims=True)
        xn = x1 * lax.rsqrt(ms + EPS) * g_ref[...]
        xn_ref[...] = _pack_bf16_pairs(xn)
        xh = xn.astype(BF16)
        xl = (xn - xh.astype(F32)).astype(BF16)
        logits = jnp.dot(jnp.concatenate([xh, xl, xh], axis=1), wr_ref[...], preferred_element_type=F32)
        route = _route(logits + br_ref[...])
        route_ref[...] = route
        lanef = lax.broadcasted_iota(jnp.int32, route.shape, 1).astype(F32)
        hits = (lanef == route[:, 0:1]).astype(F32) + (lanef == route[:, 1:2]).astype(F32)
        cnt_ref[...] += jnp.sum(hits, axis=0, keepdims=True)

    @pl.when(i < prompt_tiles)
    def _():
        body(xp_ref, ap_ref, bp_ref)

    @pl.when(i >= prompt_tiles)
    def _():
        body(xs_ref, as_ref, bs_ref)


def _out_router(xp, a4p, bp, xs, a4s, bs, wo, gamma, wr, br):
    tp, tsm = xp.shape[0], xs.shape[0]
    tm = 512
    npt, nst = tp // tm, tsm // tm
    t = tp + tsm
    pmap = lambda i: (jnp.minimum(i, npt - 1), 0)
    smap = lambda i: (jnp.maximum(i - npt, 0), 0)
    return pl.pallas_call(
        functools.partial(_out_router_kernel, prompt_tiles=npt),
        grid=(npt + nst,),
        in_specs=[
            pl.BlockSpec((tm, D_MODEL), pmap),
            pl.BlockSpec((4, tm, LANES), lambda i: (0, jnp.minimum(i, npt - 1), 0)),
            pl.BlockSpec((tm, 512), pmap),
            pl.BlockSpec((tm, D_MODEL), smap),
            pl.BlockSpec((4, tm, LANES), lambda i: (0, jnp.maximum(i - npt, 0), 0)),
            pl.BlockSpec((tm, 512), smap),
            pl.BlockSpec((D_MODEL, D_MODEL), lambda i: (0, 0)),
            pl.BlockSpec((1, D_MODEL), lambda i: (0, 0)),
            pl.BlockSpec((3 * D_MODEL, LANES), lambda i: (0, 0)),
            pl.BlockSpec((1, LANES), lambda i: (0, 0)),
        ],
        out_specs=[
            pl.BlockSpec((tm, D_MODEL), lambda i: (i, 0)),
            pl.BlockSpec((tm, D_MODEL // 2), lambda i: (i, 0)),
            pl.BlockSpec((tm, LANES), lambda i: (i, 0)),
            pl.BlockSpec((1, LANES), lambda i: (0, 0)),
        ],
        out_shape=[
            jax.ShapeDtypeStruct((t, D_MODEL), F32),
            jax.ShapeDtypeStruct((t, D_MODEL // 2), jnp.int32),
            jax.ShapeDtypeStruct((t, LANES), F32),
            jax.ShapeDtypeStruct((1, LANES), F32),
        ],
        compiler_params=_cparams(("arbitrary",)),
        name="out_router",
    )(xp, a4p, bp, xs, a4s, bs, wo, gamma, wr, br)


def _sc_gather_rows(table, idx):
    b = idx.shape[0]
    d = table.shape[1]
    per_worker = b // SC_WORKERS
    nwin = per_worker // SC_WINDOW
    assert per_worker * SC_WORKERS == b and nwin * SC_WINDOW == per_worker
    mesh = plsc.VectorSubcoreMesh(core_axis_name="c", subcore_axis_name="s")

    @functools.partial(
        pl.kernel, mesh=mesh,
        out_type=jax.ShapeDtypeStruct((b, d), table.dtype),
        scratch_types=[pltpu.VMEM((SC_WINDOW,), jnp.int32), pltpu.VMEM((SC_WINDOW, d), table.dtype),
                       pltpu.SemaphoreType.DMA],
        name="sc_gather_rows",
    )
    def gather(table_hbm, idx_hbm, out_hbm, idx_v, rows_v, sem):
        wid = lax.axis_index("s") * SC_CORES + lax.axis_index("c")
        base = wid * per_worker

        @pl.loop(0, nwin)
        def _(j):
            off = pl.multiple_of(base + j * SC_WINDOW, SC_WINDOW)
            pltpu.sync_copy(idx_hbm.at[pl.ds(off, SC_WINDOW)], idx_v)
            pltpu.async_copy(table_hbm.at[idx_v], rows_v, sem).wait()
            pltpu.sync_copy(rows_v, out_hbm.at[pl.ds(off, SC_WINDOW)])

    return gather(table, idx)


def _sc_scatter_rows(x, dest2, nrows):
    t, d = x.shape
    per_worker = t // SC_WORKERS
    nwin = per_worker // SC_SCATTER_WINDOW
    assert per_worker * SC_WORKERS == t and nwin * SC_SCATTER_WINDOW == per_worker
    mesh = plsc.VectorSubcoreMesh(core_axis_name="c", subcore_axis_name="s")

    @functools.partial(
        pl.kernel, mesh=mesh,
        out_type=jax.ShapeDtypeStruct((nrows, d), x.dtype),
        scratch_types=[pltpu.VMEM((TOP_K, SC_SCATTER_WINDOW), jnp.int32),
                       pltpu.VMEM((SC_SCATTER_WINDOW, d), x.dtype), pltpu.SemaphoreType.DMA],
        name="sc_scatter_rows",
    )
    def scatter(x_hbm, dest_hbm, out_hbm, idx_v, rows_v, sem):
        wid = lax.axis_index("s") * SC_CORES + lax.axis_index("c")
        base = wid * per_worker

        @pl.loop(0, nwin)
        def _(j):
            off = pl.multiple_of(base + j * SC_SCATTER_WINDOW, SC_SCATTER_WINDOW)
            pltpu.sync_copy(x_hbm.at[pl.ds(off, SC_SCATTER_WINDOW)], rows_v)
            for k in range(TOP_K):
                pltpu.sync_copy(dest_hbm.at[k, pl.ds(off, SC_SCATTER_WINDOW)], idx_v.at[k])
            for k in range(TOP_K):
                pltpu.async_copy(rows_v, out_hbm.at[idx_v.at[k]], sem).wait()

    return scatter(x, dest2)


def _expert_kernel(be_ref, nu_ref, nv_ref, x_ref, wg_ref, wu_ref, wd_ref, o_ref, wg_s, wu_s, wd_s):
    i = pl.program_id(0)
    used = i < nu_ref[0]
    changed = jnp.logical_or(i == 0, be_ref[i] != be_ref[jnp.maximum(i - 1, 0)])

    @pl.when(jnp.logical_and(used, changed))
    def _():
        wg_s[...] = wg_ref[...].astype(BF16)
        wu_s[...] = wu_ref[...].astype(BF16)
        wd_s[...] = wd_ref[...].astype(BF16)

    @pl.when(used)
    def _():
        row = lax.broadcasted_iota(jnp.int32, x_ref.shape, 0)
        x = _unpack_bf16_pairs(jnp.where(row < nv_ref[i], x_ref[...], 0)).astype(BF16)
        gate = jnp.dot(x, wg_s[...], preferred_element_type=F32)
        up = jnp.dot(x, wu_s[...], preferred_element_type=F32)
        h = (gate * jax.nn.sigmoid(gate) * up).astype(BF16)
        o_ref[...] = _pack_bf16_pairs(jnp.dot(h, wd_s[...], preferred_element_type=F32))

    @pl.when(jnp.logical_not(used))
    def _():
        o_ref[...] = jnp.zeros_like(o_ref)


def _experts(blk_e, n_used, nvalid, xb, w_gate, w_up, w_down):
    rows = xb.shape[0]
    nblocks = rows // MOE_ROWS
    grid_spec = pltpu.PrefetchScalarGridSpec(
        num_scalar_prefetch=3,
        grid=(nblocks,),
        in_specs=[
            pl.BlockSpec((MOE_ROWS, D_MODEL // 2), lambda i, be, nu, nv: (i, 0)),
            pl.BlockSpec((None, D_MODEL, D_EXPERT), lambda i, be, nu, nv: (be[i], 0, 0)),
            pl.BlockSpec((None, D_MODEL, D_EXPERT), lambda i, be, nu, nv: (be[i], 0, 0)),
            pl.BlockSpec((None, D_EXPERT, D_MODEL), lambda i, be, nu, nv: (be[i], 0, 0)),
        ],
        out_specs=pl.BlockSpec((MOE_ROWS, D_MODEL // 2), lambda i, be, nu, nv: (i, 0)),
        scratch_shapes=[pltpu.VMEM((D_MODEL, D_EXPERT), BF16), pltpu.VMEM((D_MODEL, D_EXPERT), BF16),
                        pltpu.VMEM((D_EXPERT, D_MODEL), BF16)],
    )
    return pl.pallas_call(
        _expert_kernel,
        grid_spec=grid_spec,
        out_shape=jax.ShapeDtypeStruct((rows, D_MODEL // 2), jnp.int32),
        compiler_params=_cparams(("arbitrary",)),
        name="experts",
    )(blk_e, n_used, nvalid, xb, w_gate, w_up, w_down)


def _combine_kernel(x1_ref, y1_ref, y2_ref, route_ref, g_ref, outp_ref, outs_ref, *, prompt_tiles):
    r = route_ref[...]
    x = (x1_ref[...] + r[:, 2:3] * _unpack_bf16_pairs(y1_ref[...])
         + r[:, 3:4] * _unpack_bf16_pairs(y2_ref[...]))
    ms = jnp.mean(x * x, axis=-1, keepdims=True)
    y = x * lax.rsqrt(ms + EPS) * g_ref[...]
    i = pl.program_id(0)

    @pl.when(i < prompt_tiles)
    def _():
        outp_ref[...] = y

    @pl.when(i >= prompt_tiles)
    def _():
        outs_ref[...] = y


def _combine_norm(x1, ygath, route, gamma, tp):
    t = x1.shape[0]
    tm = 512
    nt, npt = t // tm, tp // tm
    return pl.pallas_call(
        functools.partial(_combine_kernel, prompt_tiles=npt),
        grid=(nt,),
        in_specs=[
            pl.BlockSpec((tm, D_MODEL), lambda i: (i, 0)),
            pl.BlockSpec((tm, D_MODEL // 2), lambda i: (i, 0)),
            pl.BlockSpec((tm, D_MODEL // 2), lambda i: (i + nt, 0)),
            pl.BlockSpec((tm, LANES), lambda i: (i, 0)),
            pl.BlockSpec((1, D_MODEL), lambda i: (0, 0)),
        ],
        out_specs=[
            pl.BlockSpec((tm, D_MODEL), lambda i: (jnp.minimum(i, npt - 1), 0)),
            pl.BlockSpec((tm, D_MODEL), lambda i: (jnp.maximum(i - npt, 0), 0)),
        ],
        out_shape=[jax.ShapeDtypeStruct((tp, D_MODEL), F32), jax.ShapeDtypeStruct((t - tp, D_MODEL), F32)],
        compiler_params=_cparams(("arbitrary",)),
        name="combine_norm",
    )(x1, ygath, ygath, route, gamma)


BAND_PERIOD = 2 * QB + 1


def _band_tiles(h):
    nk = 2 * QB
    lead, heads = h.shape[:-2], h.shape[-2]
    flat = jnp.tile(h, (1,) * (h.ndim - 1) + (nk,))[..., :nk * nk]
    a = flat.reshape(lead + (heads // 2, 2, nk, nk))[..., :QB]
    a = jnp.swapaxes(a, -3, -2).reshape(lead + (heads // 2, nk, nk))
    prev = (np.arange(nk) < QB)[:, None]
    return jnp.stack([a, jnp.where(prev, NEG, a)], axis=len(lead))


def _band_index():
    c = (BAND_PERIOD - np.arange(BAND_PERIOD)) % BAND_PERIOD
    return c, c <= QB


def _bias_a_prompt(table_a):
    c, valid = _band_index()
    idx = np.stack([_t5_bucket_np(d * np.clip(QB - c, 0, QB)) for d in DILATIONS])
    h = jnp.where(valid, jnp.transpose(table_a[idx], (0, 2, 1)) * LOG2E, NEG)
    return _band_tiles(h)


def _bias_b_prompt(table_b):
    c, valid = _band_index()
    valid = valid & (c >= 1)
    h = jnp.where(valid, table_b[_t5_bucket_np(np.clip(QB - c, 0, QB))].T * LOG2E, NEG)
    h = jnp.transpose(h.reshape(KV_B, G_B, BAND_PERIOD), (1, 0, 2)).reshape(H_B, BAND_PERIOD)
    return _band_tiles(h)


def _sample_bias(table, span, t, log2_weight):
    cols = span + LANES
    period = cols + LANES
    x = np.arange(period)
    dist = np.where(x >= period - t, span - x + period, span - x)
    extra = log2_weight(dist)
    valid = np.isfinite(extra)
    u = jnp.where(valid, table[_t5_bucket_np(np.maximum(dist, 0))].T * LOG2E
                  + np.where(valid, extra, 0.0).astype(np.float32), NEG)
    rows = jnp.tile(u, (1, t))[:, :t * (period - 1)].reshape(u.shape[0], t, period - 1)[:, :, :cols]
    return rows.reshape(u.shape[0] * t, cols)


def _bias_a_sample(table_a, t):
    def log2_count(dist):
        count = np.zeros(dist.shape, np.int64)
        for w, d in zip(WINDOWS, DILATIONS):
            count += (dist >= 0) & (dist % d == 0) & (dist <= w)
        return np.where(count > 0, np.log2(np.maximum(count, 1)), -np.inf)

    return _sample_bias(table_a, WIN_A, t, log2_count)


def _bias_b_sample(table_b, t):
    return _sample_bias(table_b, WIN_B, t,
                        lambda dist: np.where((dist >= 0) & (dist < WIN_B), 0.0, -np.inf))


def _dest_kernel(route_ref, cnt_ref, tri_ref, dest_ref, run_scr, pst_scr):
    i = pl.program_id(0)
    tm = route_ref.shape[0]
    r = route_ref[...]
    lane = lax.broadcasted_iota(jnp.int32, (tm, LANES), 1)
    lanef = lane.astype(F32)
    oh0 = lanef == r[:, 0:1]
    oh1 = lanef == r[:, 1:2]
    ohf = jnp.concatenate([oh0, oh1], axis=0).astype(F32)

    @pl.when(i == 0)
    def _():
        cnt = jnp.broadcast_to(cnt_ref[...], (8, LANES))
        padded = jnp.floor((cnt + (MOE_ROWS - 1)) * (1.0 / MOE_ROWS)) * MOE_ROWS
        lane8 = lax.broadcasted_iota(jnp.int32, (8, LANES), 1)
        x = padded
        for sh in (1, 2, 4, 8, 16, 32, 64):
            x = x + jnp.where(lane8 >= sh, pltpu.roll(x, sh, 1), 0.0)
        pst_scr[...] = (x - padded)[0:1]
        run_scr[...] = jnp.zeros_like(run_scr)

    csum = jnp.dot(tri_ref[...], ohf.astype(BF16), preferred_element_type=F32)
    val = csum + (run_scr[...] + pst_scr[...] - 1.0)
    d0 = jnp.sum(jnp.where(oh0, val[:tm], 0.0), axis=-1, keepdims=True)
    d1 = jnp.sum(jnp.where(oh1, val[tm:], 0.0), axis=-1, keepdims=True)
    tile = jnp.where(lane == 0, d0, jnp.where(lane == 1, d1, 0.0))
    dest_ref[...] = tile.T[:8].astype(jnp.int32)
    run_scr[...] += jnp.sum(ohf, axis=0, keepdims=True)


def _dispatch(route, cnt):
    t = route.shape[0]
    tm = 512
    tri = (jnp.arange(2 * tm)[:, None] >= jnp.arange(2 * tm)[None, :]).astype(BF16)
    dest = pl.pallas_call(
        _dest_kernel,
        grid=(t // tm,),
        in_specs=[pl.BlockSpec((tm, LANES), lambda i: (i, 0)),
                  pl.BlockSpec((1, LANES), lambda i: (0, 0)),
                  pl.BlockSpec((2 * tm, 2 * tm), lambda i: (0, 0))],
        out_specs=pl.BlockSpec((8, tm), lambda i: (0, i)),
        out_shape=jax.ShapeDtypeStruct((8, t), jnp.int32),
        scratch_shapes=[pltpu.VMEM((1, LANES), F32), pltpu.VMEM((1, LANES), F32)],
        compiler_params=_cparams(("arbitrary",)),
        name="moe_dest",
    )(route, cnt, tri)
    counts = cnt[0, :N_EXPERTS].astype(jnp.int32)
    nblk_e = (counts + MOE_ROWS - 1) // MOE_ROWS
    bend = jnp.cumsum(nblk_e)
    nblocks = -(-t * TOP_K // MOE_ROWS) + N_EXPERTS
    blk = jnp.arange(nblocks, dtype=jnp.int32)
    blk_e = jnp.minimum(jnp.sum((bend[None, :] <= blk[:, None]).astype(jnp.int32), axis=1), N_EXPERTS - 1)
    within = blk - (bend - nblk_e)[blk_e]
    nvalid = jnp.clip(counts[blk_e] - within * MOE_ROWS, 0, MOE_ROWS).astype(jnp.int32)
    n_used = bend[-1].astype(jnp.int32).reshape(1)
    dest2 = dest[:TOP_K]
    return dest2, blk_e.astype(jnp.int32), n_used, nvalid


def kernel(x_prompt, x_sample, cache_a_k, cache_a_v, cache_b_k, cache_b_v, rel_bias_table, attn_norm, w_in,
           w_out, attn_sinks, ffn_norm, w_router_group, b_router_group, w_router_expert, b_router_expert,
           w_gate, w_up, w_down, final_norm):
    s = x_prompt.shape[1]
    ns, ts = x_sample.shape[0], x_sample.shape[1]
    table_a = rel_bias_table[:, :H_A]
    table_b = rel_bias_table[:, H_A:]

    w = w_in[0]
    wqa, wka, wva, wqb, wkb, wvb = (w[:, 0:512], w[:, 512:1024], w[:, 1024:1536], w[:, 1536:2048],
                                    w[:, 2048:2176], w[:, 2176:2304])
    wqb = jnp.transpose(wqb.reshape(D_MODEL, KV_B, G_B, HEAD_DIM), (0, 2, 1, 3)).reshape(D_MODEL, 512)
    wp = jnp.concatenate([wka, wva, wqa, wqb, wkb, wvb], axis=1).astype(BF16)
    cscale = jnp.concatenate([jnp.ones((1, 1024), F32), jnp.full((1, 1024), SCALE * LOG2E, F32),
                              jnp.ones((1, 256), F32)], axis=1)
    wo = w_out[0]
    wo_b = jnp.transpose(wo[512:].reshape(KV_B, G_B, HEAD_DIM, D_MODEL), (1, 0, 2, 3)).reshape(512, D_MODEL)
    wo_p = jnp.concatenate([wo[:512], wo_b], axis=0).astype(BF16)
    wr = jnp.concatenate([w_router_group[0],
                          jnp.transpose(w_router_expert[0], (1, 0, 2)).reshape(D_MODEL, N_EXPERTS),
                          jnp.zeros((D_MODEL, LANES - N_GROUPS - N_EXPERTS), F32)], axis=1)
    wr_hi = wr.astype(BF16)
    wr = jnp.concatenate([wr_hi, wr_hi, (wr - wr_hi.astype(F32)).astype(BF16)], axis=0)
    br = jnp.concatenate([b_router_group[0], b_router_expert[0].reshape(N_EXPERTS),
                          jnp.zeros((LANES - N_GROUPS - N_EXPERTS,), F32)]).reshape(1, LANES)
    sinks2 = attn_sinks[0] * LOG2E
    sinks_gk = jnp.transpose(sinks2.reshape(KV_B, G_B), (1, 0)).reshape(H_B)
    sink_rows_p = jnp.repeat(sinks_gk, QB).reshape(G_B, 1, 2 * QB)
    sink_rows_s = jnp.repeat(sinks2, ts).reshape(H_B * ts, 1)
    emat = jnp.tile(jnp.arange(LANES)[:, None] == (jnp.arange(A_WIDTH)[None, :] // HEAD_DIM),
                    (3, 1)).astype(BF16)
    attn_g = attn_norm[0].reshape(1, D_MODEL)
    ffn_g = ffn_norm[0].reshape(1, D_MODEL)

    xp = x_prompt.reshape(s, D_MODEL)
    aperm, qb_p, kvb_p, akv32, bkv32 = _proj_prompt(xp, attn_g, wp, cscale)
    a4 = _attn_a_prompt(aperm, _bias_a_prompt(table_a), emat)
    ob_p = _attn_b_prompt(qb_p, kvb_p, _bias_b_prompt(table_b), sink_rows_p)

    xs = x_sample.reshape(ns * ts, D_MODEL)
    q_s, kv_s = _proj_sample(xs, attn_g, wp, cscale)
    akt = jnp.transpose(cache_a_k[0], (0, 2, 3, 1)).reshape(ns, A_WIDTH, WIN_A)
    avt = jnp.transpose(cache_a_v[0], (0, 2, 3, 1)).reshape(ns, A_WIDTH, WIN_A)
    bkt = jnp.transpose(cache_b_k[0], (0, 2, 3, 1)).reshape(ns, LANES, WIN_B)
    bvt = jnp.transpose(cache_b_v[0], (0, 2, 3, 1)).reshape(ns, LANES, WIN_B)
    oa_s, ob_s = _attn_sample(q_s.reshape(ns, ts, 1024), kv_s.reshape(ns, ts, 1280), akt, avt, bkt, bvt,
                              _bias_a_sample(table_a, ts), _bias_b_sample(table_b, ts), sink_rows_s)
    a4_s = jnp.transpose(oa_s.reshape(ns * ts, 4, LANES), (1, 0, 2))

    x1, xn, route, cnt = _out_router(xp, a4, ob_p, xs, a4_s, ob_s.reshape(ns * ts, 512), wo_p, ffn_g, wr, br)
    dest2, blk_e, n_used, nvalid = _dispatch(route, cnt)
    xb = _sc_scatter_rows(xn, dest2, blk_e.shape[0] * MOE_ROWS)
    yb = _experts(blk_e, n_used, nvalid, xb, w_gate[0], w_up[0], w_down[0])
    y_p, y_s = _combine_norm(x1, _sc_gather_rows(yb, dest2.reshape(-1)), route, final_norm.reshape(1, D_MODEL), s)

    y_prompt = y_p.reshape(1, s, D_MODEL)
    y_sample = y_s.reshape(ns, ts, D_MODEL)
    keep_a, keep_b = min(WIN_A, s), min(WIN_B, s)
    pak = akv32[s - keep_a:, :512].reshape(1, 1, keep_a, H_A, HEAD_DIM)
    pav = akv32[s - keep_a:, 512:].reshape(1, 1, keep_a, H_A, HEAD_DIM)
    pbk = bkv32[s - keep_b:, :128].reshape(1, 1, keep_b, KV_B, HEAD_DIM)
    pbv = bkv32[s - keep_b:, 128:].reshape(1, 1, keep_b, KV_B, HEAD_DIM)
    sak = kv_s[:, 0:512].reshape(1, ns, ts, H_A, HEAD_DIM)
    sav = kv_s[:, 512:1024].reshape(1, ns, ts, H_A, HEAD_DIM)
    sbk = kv_s[:, 1024:1152].reshape(1, ns, ts, KV_B, HEAD_DIM)
    sbv = kv_s[:, 1152:1280].reshape(1, ns, ts, KV_B, HEAD_DIM)
    return (y_prompt, y_sample, pak, pav, pbk, pbv, sak, sav, sbk, sbv)
```

```python
import functools
import math

import jax
import jax.numpy as jnp
import numpy as np
from jax import lax
from jax.experimental import pallas as pl
from jax.experimental.pallas import tpu as pltpu
from jax.experimental.pallas import tpu_sc as plsc

D_MODEL = 1024
HEAD_DIM = 64
H_A = 8
H_B = 8
KV_B = 2
G_B = 4
DILATIONS = (1, 4, 16)
WINDOWS = (128, 512, 2048)
WIN_A = 2048
WIN_B = 128
NUM_BUCKETS = 32
MAX_DISTANCE = 2048
N_GROUPS = 4
EXPERTS_PER_GROUP = 8
N_EXPERTS = 32
TOP_K = 2
D_EXPERT = 512
EPS = 1e-5
SCALE = HEAD_DIM ** -0.5
PAST_LEN = 16384

LANES = 128
SPAN = 2048
QB = 128
NCHUNK = 9
A_WIDTH = H_A * HEAD_DIM
MOE_ROWS = 512
SC_CORES = 2
SC_SUBCORES = 16
SC_WORKERS = SC_CORES * SC_SUBCORES
SC_WINDOW = 64
SC_SCATTER_WINDOW = 32
NEG = -1e30
LOG2E = math.log2(math.e)
B_STEP = 512
VMEM_LIMIT = 56 * 1024 * 1024

F32 = jnp.float32
BF16 = jnp.bfloat16


def _t5_bucket_np(dist):
    dist = np.asarray(dist, np.int64)
    max_exact = NUM_BUCKETS // 2
    d = np.maximum(dist, 1).astype(np.float32)
    ratio = np.log(d / np.float32(max_exact)) / np.float32(math.log(MAX_DISTANCE / max_exact))
    large = max_exact + (ratio * np.float32(NUM_BUCKETS - max_exact)).astype(np.int32)
    large = np.minimum(large, NUM_BUCKETS - 1)
    return np.where(dist < max_exact, dist, large).astype(np.int32)


def _cparams(sem, vmem=VMEM_LIMIT):
    return pltpu.CompilerParams(dimension_semantics=sem, vmem_limit_bytes=vmem)


def _proj_prompt_kernel(x_ref, g_ref, w_ref, cs_ref, aperm_ref, qb_ref, kvb_ref, akv_ref, bkv_ref,
                        h_scr, p_scr):
    n = pl.program_id(1)

    @pl.when(n == 0)
    def _():
        x = x_ref[...]
        ms = jnp.mean(x * x, axis=-1, keepdims=True)
        h_scr[...] = (x * lax.rsqrt(ms + EPS) * g_ref[...]).astype(BF16)

    p = jnp.dot(h_scr[...], w_ref[...], preferred_element_type=F32) * cs_ref[...]

    @pl.when(n < 6)
    def _():
        aperm_ref[0] = p.astype(BF16)
        p_scr[0, 0] = p[:, :LANES]
        p_scr[0, 1] = p[:, LANES:]
        quarter = SPAN // 4
        for r in range(4):
            lo = p_scr[0, 0, pl.ds(r, quarter, stride=4), :]
            hi = p_scr[0, 1, pl.ds(r, quarter, stride=4), :]
            p_scr[1, 0, r * quarter:(r + 1) * quarter, :] = lo
            p_scr[1, 1, r * quarter:(r + 1) * quarter, :] = hi
            aperm_ref[1, r * quarter:(r + 1) * quarter, :] = jnp.concatenate([lo, hi], axis=1).astype(BF16)
        for r16 in range(16):
            start = (r16 % 4) * quarter + r16 // 4
            t = jnp.concatenate([p_scr[1, 0, pl.ds(start, QB, stride=4), :],
                                 p_scr[1, 1, pl.ds(start, QB, stride=4), :]], axis=1)
            aperm_ref[2, r16 * QB:(r16 + 1) * QB, :] = t.astype(BF16)

    @pl.when(n < 4)
    def _():
        akv_ref[...] = p

    @pl.when(jnp.logical_or(n == 6, n == 7))
    def _():
        qb_ref[...] = p.astype(BF16)

    @pl.when(n == 8)
    def _():
        kvb_ref[...] = p.astype(BF16)
        bkv_ref[...] = p


def _proj_prompt(x, gamma, w, cscale):
    s = x.shape[0]
    nspan = s // SPAN
    return pl.pallas_call(
        _proj_prompt_kernel,
        grid=(nspan, NCHUNK),
        in_specs=[
            pl.BlockSpec((SPAN, D_MODEL), lambda b, n: (b, 0)),
            pl.BlockSpec((1, D_MODEL), lambda b, n: (0, 0)),
            pl.BlockSpec((D_MODEL, 256), lambda b, n: (0, n)),
            pl.BlockSpec((1, 256), lambda b, n: (0, n)),
        ],
        out_specs=[
            pl.BlockSpec((3, SPAN, 256), lambda b, n: (0, b, jnp.minimum(n, 5))),
            pl.BlockSpec((SPAN, 256), lambda b, n: (b, jnp.clip(n - 6, 0, 1))),
            pl.BlockSpec((SPAN, 256), lambda b, n: (b, 0)),
            pl.BlockSpec((SPAN, 256), lambda b, n: (b, jnp.minimum(n, 3))),
            pl.BlockSpec((SPAN, 256), lambda b, n: (b, 0)),
        ],
        out_shape=[
            jax.ShapeDtypeStruct((3, s, 3 * A_WIDTH), BF16),
            jax.ShapeDtypeStruct((s, 512), BF16),
            jax.ShapeDtypeStruct((s, 256), BF16),
            jax.ShapeDtypeStruct((s, 1024), F32),
            jax.ShapeDtypeStruct((s, 256), F32),
        ],
        scratch_shapes=[pltpu.VMEM((SPAN, D_MODEL), BF16), pltpu.VMEM((2, 2, SPAN, LANES), F32)],
        compiler_params=_cparams(("arbitrary", "arbitrary")),
        name="proj_prompt",
    )(x, gamma, w, cscale)


def _proj_sample_kernel(x_ref, g_ref, w_ref, cs_ref, q_ref, kv_ref):
    x = x_ref[...]
    ms = jnp.mean(x * x, axis=-1, keepdims=True)
    h = (x * lax.rsqrt(ms + EPS) * g_ref[...]).astype(BF16)
    p = jnp.dot(h, w_ref[...], preferred_element_type=F32) * cs_ref[...]
    kv_ref[:, :1024] = p[:, :1024]
    kv_ref[:, 1024:] = p[:, 2048:]
    q_ref[...] = p[:, 1024:2048]


def _proj_sample(x, gamma, w, cscale):
    t = x.shape[0]
    tm = 512
    return pl.pallas_call(
        _proj_sample_kernel,
        grid=(t // tm,),
        in_specs=[
            pl.BlockSpec((tm, D_MODEL), lambda i: (i, 0)),
            pl.BlockSpec((1, D_MODEL), lambda i: (0, 0)),
            pl.BlockSpec((D_MODEL, 2304), lambda i: (0, 0)),
            pl.BlockSpec((1, 2304), lambda i: (0, 0)),
        ],
        out_specs=[
            pl.BlockSpec((tm, 1024), lambda i: (i, 0)),
            pl.BlockSpec((tm, 1280), lambda i: (i, 0)),
        ],
        out_shape=[
            jax.ShapeDtypeStruct((t, 1024), F32),
            jax.ShapeDtypeStruct((t, 1280), F32),
        ],
        compiler_params=_cparams(("arbitrary",)),
        name="proj_sample",
    )(x, gamma, w, cscale)


def _spread_heads(w, e3_ref):
    hi = w.astype(BF16)
    r1 = w - hi.astype(F32)
    mid = r1.astype(BF16)
    low = (r1 - mid.astype(F32)).astype(BF16)
    return jnp.dot(jnp.concatenate([hi, mid, low], axis=1), e3_ref[...], preferred_element_type=F32)


def _pair_tile(q2, kk, vv, bias_t, lo, sink=None):
    zero = jnp.zeros_like(q2)
    qq = jnp.concatenate([jnp.where(lo, q2, zero), jnp.where(lo, zero, q2)], axis=0)
    st = lax.dot_general(kk, qq, (((1,), (1,)), ((), ())), preferred_element_type=F32)
    st = st + bias_t
    m = jnp.max(st, axis=0, keepdims=True)
    if sink is not None:
        m = jnp.maximum(m, sink)
    p = jnp.exp2(st - m)
    den = jnp.sum(p, axis=0, keepdims=True)
    if sink is not None:
        den = den + jnp.exp2(sink - m)
    pn = (p * (1.0 / den)).astype(BF16)
    o = lax.dot_general(pn, vv, (((0,), (0,)), ((), ())), preferred_element_type=F32)
    return jnp.where(lo, o[:QB], o[QB:]), m + jnp.log2(den)


def _attn_a_kernel(q_ref, kvc_ref, kvp_ref, bias_ref, e_ref, out_ref, o_scr, st_scr):
    b = pl.program_id(0)
    g = pl.program_id(1)
    nblk = jnp.where(g == 0, 16, jnp.where(g == 1, 4, 1))
    lane = lax.broadcasted_iota(jnp.int32, (QB, LANES), 1)
    lo = lane < HEAD_DIM

    for cb in range(SPAN // QB):
        first = lax.rem(jnp.int32(cb), nblk) == 0
        rows = slice(cb * QB, (cb + 1) * QB)
        prow_c = max(cb - 1, 0) * QB
        prow_p = pl.multiple_of(jnp.where(first, cb + nblk - 1, 0) * QB, QB)
        variant = jnp.logical_and(first, b == 0).astype(jnp.int32)
        stats = []
        for hp in range(4):
            ks = slice(hp * LANES, (hp + 1) * LANES)
            vs = slice(A_WIDTH + hp * LANES, A_WIDTH + (hp + 1) * LANES)
            kp = jnp.where(first, kvp_ref[pl.ds(prow_p, QB), ks], kvc_ref[prow_c:prow_c + QB, ks])
            vp = jnp.where(first, kvp_ref[pl.ds(prow_p, QB), vs], kvc_ref[prow_c:prow_c + QB, vs])
            kk = jnp.concatenate([kp, kvc_ref[rows, ks]], axis=0)
            vv = jnp.concatenate([vp, kvc_ref[rows, vs]], axis=0)
            o, lse = _pair_tile(q_ref[rows, ks], kk, vv, bias_ref[variant, hp], lo)
            o_scr[g, hp, rows, :] = o
            stats += [lse[:, :QB], lse[:, QB:]]
        sm = jnp.concatenate(stats + [jnp.zeros((LANES - H_A, QB), F32)], axis=0)
        st_scr[g, rows, :] = sm.T

    @pl.when(g == 2)
    def _():
        def merge(c, carry):
            r2 = lax.rem(c, 4) * (SPAN // 4) + c // 4
            r3 = pl.multiple_of(c * QB, QB)
            l1 = st_scr[0, pl.ds(c, QB, stride=16), :]
            l2 = st_scr[1, pl.ds(r2, QB, stride=4), :]
            l3 = st_scr[2, pl.ds(r3, QB), :]
            mx = jnp.maximum(jnp.maximum(l1, l2), l3)
            w1 = jnp.exp2(l1 - mx)
            w2 = jnp.exp2(l2 - mx)
            w3 = jnp.exp2(l3 - mx)
            tot = w1 + w2 + w3
            a1 = _spread_heads(w1 / tot, e_ref)
            a2 = _spread_heads(w2 / tot, e_ref)
            a3 = _spread_heads(w3 / tot, e_ref)
            for hp in range(4):
                sl = slice(hp * LANES, (hp + 1) * LANES)
                o1 = o_scr[0, hp, pl.ds(c, QB, stride=16), :]
                o2 = o_scr[1, hp, pl.ds(r2, QB, stride=4), :]
                o3 = o_scr[2, hp, pl.ds(r3, QB), :]
                out_ref[hp, pl.ds(c, QB, stride=16), :] = a1[:, sl] * o1 + a2[:, sl] * o2 + a3[:, sl] * o3
            return carry

        lax.fori_loop(0, 16, merge, 0, unroll=4)


def _attn_a_prompt(aperm, bias_a, emat):
    s = aperm.shape[1]
    nspan = s // SPAN
    return pl.pallas_call(
        _attn_a_kernel,
        grid=(nspan, 3),
        in_specs=[
            pl.BlockSpec((None, SPAN, A_WIDTH), lambda b, g: (g, b, 2)),
            pl.BlockSpec((None, SPAN, 2 * A_WIDTH), lambda b, g: (g, b, 0)),
            pl.BlockSpec((None, SPAN, 2 * A_WIDTH), lambda b, g: (g, jnp.maximum(b - 1, 0), 0)),
            pl.BlockSpec((None, 2, 4, 2 * QB, 2 * QB), lambda b, g: (g, 0, 0, 0, 0)),
            pl.BlockSpec((3 * LANES, A_WIDTH), lambda b, g: (0, 0)),
        ],
        out_specs=pl.BlockSpec((4, SPAN, LANES), lambda b, g: (0, b, 0)),
        out_shape=jax.ShapeDtypeStruct((4, s, LANES), F32),
        scratch_shapes=[pltpu.VMEM((3, 4, SPAN, LANES), F32), pltpu.VMEM((3, SPAN, LANES), F32)],
        compiler_params=_cparams(("arbitrary", "arbitrary")),
        name="attn_a_prompt",
    )(aperm, aperm, aperm, bias_a, emat)


def _attn_b_kernel(q_ref, kvc_ref, kvp_ref, bias_ref, sink_ref, out_ref):
    i = pl.program_id(0)
    lane = lax.broadcasted_iota(jnp.int32, (QB, LANES), 1)
    lo = lane < HEAD_DIM
    variant = (i == 0).astype(jnp.int32)
    for j in range(B_STEP // QB):
        rows = slice(j * QB, (j + 1) * QB)
        if j == 0:
            kp, vp = kvp_ref[:, :LANES], kvp_ref[:, LANES:]
        else:
            kp, vp = kvc_ref[(j - 1) * QB:j * QB, :LANES], kvc_ref[(j - 1) * QB:j * QB, LANES:]
        kk = jnp.concatenate([kp, kvc_ref[rows, :LANES]], axis=0)
        vv = jnp.concatenate([vp, kvc_ref[rows, LANES:]], axis=0)
        for g in range(G_B):
            bias_t = bias_ref[variant, g] if j == 0 else bias_ref[0, g]
            o, _ = _pair_tile(q_ref[rows, g * LANES:(g + 1) * LANES], kk, vv, bias_t, lo, sink=sink_ref[g])
            out_ref[rows, g * LANES:(g + 1) * LANES] = o.astype(BF16)


def _attn_b_prompt(qb, kvb, bias_b, sink_rows):
    s = qb.shape[0]
    per = B_STEP // QB
    return pl.pallas_call(
        _attn_b_kernel,
        grid=(s // B_STEP,),
        in_specs=[
            pl.BlockSpec((B_STEP, 512), lambda i: (i, 0)),
            pl.BlockSpec((B_STEP, 256), lambda i: (i, 0)),
            pl.BlockSpec((QB, 256), lambda i: (jnp.maximum(i * per - 1, 0), 0)),
            pl.BlockSpec((2, G_B, 2 * QB, 2 * QB), lambda i: (0, 0, 0, 0)),
            pl.BlockSpec((G_B, 1, 2 * QB), lambda i: (0, 0, 0)),
        ],
        out_specs=pl.BlockSpec((B_STEP, 512), lambda i: (i, 0)),
        out_shape=jax.ShapeDtypeStruct((s, 512), BF16),
        compiler_params=_cparams(("arbitrary",)),
        name="attn_b_prompt",
    )(qb, kvb, kvb, bias_b, sink_rows)


def _attn_sample_kernel(q_ref, kvn_ref, akt_ref, avt_ref, bkt_ref, bvt_ref, cba_ref, cbb_ref, sink_ref,
                        oa_ref, ob_ref):
    t = q_ref.shape[0]
    q = q_ref[...]
    kvn = kvn_ref[...]
    kvn_p = jnp.concatenate([kvn, jnp.zeros((LANES - t, kvn.shape[1]), F32)], axis=0).astype(BF16)
    lane_a = lax.broadcasted_iota(jnp.int32, (t, A_WIDTH), 1) // HEAD_DIM

    qa = q[:, :A_WIDTH]
    qbd = jnp.concatenate([jnp.where(lane_a == h, qa, 0.0) for h in range(H_A)], axis=0).astype(BF16)
    s_c = jnp.dot(qbd, akt_ref[...].astype(BF16), preferred_element_type=F32)
    s_n = lax.dot_general(qbd, kvn_p[:, :A_WIDTH], (((1,), (1,)), ((), ())), preferred_element_type=F32)
    s = jnp.concatenate([s_c, s_n], axis=1) + cba_ref[...]
    m = jnp.max(s, axis=-1, keepdims=True)
    p = jnp.exp2(s - m)
    l = jnp.sum(p, axis=-1, keepdims=True)
    o_n = jnp.dot(p[:, WIN_A:].astype(BF16), kvn_p[:, A_WIDTH:2 * A_WIDTH], preferred_element_type=F32)
    pc = jnp.concatenate([p[:, :WIN_A], jnp.zeros((LANES - H_A * t, WIN_A), F32)], axis=0).astype(BF16)
    o_t = lax.dot_general(avt_ref[...].astype(BF16), pc, (((1,), (1,)), ((), ())),
                          preferred_element_type=F32)
    o_all = o_t.T[:H_A * t] + o_n
    o_sel = jnp.zeros((t, A_WIDTH), F32)
    l_b = jnp.ones((t, A_WIDTH), F32)
    for h in range(H_A):
        sel = lane_a == h
        o_sel = jnp.where(sel, o_all[h * t:(h + 1) * t], o_sel)
        l_b = jnp.where(sel, l[h * t:(h + 1) * t], l_b)
    oa_ref[...] = o_sel / l_b

    lane_b = lax.broadcasted_iota(jnp.int32, (G_B * t, LANES), 1)
    lo = lane_b < HEAD_DIM
    qb2 = jnp.concatenate([q[:, A_WIDTH + g * LANES:A_WIDTH + (g + 1) * LANES] for g in range(G_B)], axis=0)
    qm = jnp.concatenate([jnp.where(lo, qb2, 0.0), jnp.where(lo, 0.0, qb2)], axis=0).astype(BF16)
    kb_n = kvn_p[:, 2 * A_WIDTH:2 * A_WIDTH + LANES]
    vb_n = kvn_p[:, 2 * A_WIDTH + LANES:]
    sb_c = jnp.dot(qm, bkt_ref[...].astype(BF16), preferred_element_type=F32)
    sb_n = lax.dot_general(qm, kb_n, (((1,), (1,)), ((), ())), preferred_element_type=F32)
    sb = jnp.concatenate([sb_c, sb_n], axis=1) + cbb_ref[...]
    sink = sink_ref[...]
    mb = jnp.maximum(jnp.max(sb, axis=-1, keepdims=True), sink)
    pbb = jnp.exp2(sb - mb)
    den = jnp.sum(pbb, axis=-1, keepdims=True) + jnp.exp2(sink - mb)
    pbb = pbb.astype(BF16)
    ob = lax.dot_general(pbb[:, :WIN_B], bvt_ref[...].astype(BF16), (((1,), (1,)), ((), ())),
                         preferred_element_type=F32)
    ob = (ob + jnp.dot(pbb[:, WIN_B:], vb_n, preferred_element_type=F32)) / den
    half = G_B * t
    lo8 = lo[:t]
    for g in range(G_B):
        ob_ref[:, g * LANES:(g + 1) * LANES] = jnp.where(
            lo8, ob[g * t:(g + 1) * t], ob[half + g * t:half + (g + 1) * t])


def _attn_sample(q3, kvn3, akt, avt, bkt, bvt, cbias_a, cbias_b, sink_rows):
    ns, t = q3.shape[0], q3.shape[1]
    return pl.pallas_call(
        _attn_sample_kernel,
        grid=(ns,),
        in_specs=[
            pl.BlockSpec((None, t, 1024), lambda n: (n, 0, 0)),
            pl.BlockSpec((None, t, 1280), lambda n: (n, 0, 0)),
            pl.BlockSpec((None, A_WIDTH, WIN_A), lambda n: (n, 0, 0)),
            pl.BlockSpec((None, A_WIDTH, WIN_A), lambda n: (n, 0, 0)),
            pl.BlockSpec((None, LANES, WIN_B), lambda n: (n, 0, 0)),
            pl.BlockSpec((None, LANES, WIN_B), lambda n: (n, 0, 0)),
            pl.BlockSpec((H_A * t, WIN_A + LANES), lambda n: (0, 0)),
            pl.BlockSpec((H_B * t, WIN_B + LANES), lambda n: (0, 0)),
            pl.BlockSpec((H_B * t, 1), lambda n: (0, 0)),
        ],
        out_specs=[
            pl.BlockSpec((None, t, 512), lambda n: (n, 0, 0)),
            pl.BlockSpec((None, t, 512), lambda n: (n, 0, 0)),
        ],
        out_shape=[jax.ShapeDtypeStruct((ns, t, 512), F32), jax.ShapeDtypeStruct((ns, t, 512), F32)],
        compiler_params=_cparams(("arbitrary",)),
        name="attn_sample",
    )(q3, kvn3, akt, avt, bkt, bvt, cbias_a, cbias_b, sink_rows)


def _route(logits):
    lane = lax.broadcasted_iota(jnp.int32, logits.shape, 1).astype(F32)
    big = jnp.float32(1 << 20)
    ninf = jnp.float32(-jnp.inf)
    gmask = lane < N_GROUPS
    lg = jnp.where(gmask, logits, ninf)
    gmax = jnp.max(lg, axis=-1, keepdims=True)
    grp = jnp.min(jnp.where(lg == gmax, lane, big), axis=-1, keepdims=True)
    pg_top = 1.0 / jnp.sum(jnp.exp(lg - gmax), axis=-1, keepdims=True)
    e0 = N_GROUPS + grp * EXPERTS_PER_GROUP
    emask = jnp.logical_and(lane >= e0, lane < e0 + EXPERTS_PER_GROUP)
    le = jnp.where(emask, logits, ninf)
    emax = jnp.max(le, axis=-1, keepdims=True)
    esum = jnp.sum(jnp.exp(le - emax), axis=-1, keepdims=True)
    i1 = jnp.min(jnp.where(le == emax, lane, big), axis=-1, keepdims=True)
    le2 = jnp.where(lane == i1, ninf, le)
    e2max = jnp.max(le2, axis=-1, keepdims=True)
    i2 = jnp.min(jnp.where(le2 == e2max, lane, big), axis=-1, keepdims=True)
    p1 = 1.0 / esum
    p2 = jnp.exp(e2max - emax) / esum
    g1 = pg_top * p1 / (p1 + p2)
    g2 = pg_top * p2 / (p1 + p2)
    out = jnp.where(lane == 0, i1 - N_GROUPS, 0.0)
    out = jnp.where(lane == 1, i2 - N_GROUPS, out)
    out = jnp.where(lane == 2, g1, out)
    out = jnp.where(lane == 3, g2, out)
    return out


def _pack_bf16_pairs(x):
    half = x.shape[1] // 2

    def rne(v):
        bits = lax.bitcast_convert_type(v, jnp.int32)
        return bits + 0x7FFF + (lax.shift_right_logical(bits, 16) & 1)

    lo = lax.shift_right_logical(rne(x[:, :half]), 16)
    hi = rne(x[:, half:]) & jnp.int32(-65536)
    return lo | hi


def _unpack_bf16_pairs(w):
    lo = lax.bitcast_convert_type(lax.shift_left(w, 16), F32)
    hi = lax.bitcast_convert_type(w & jnp.int32(-65536), F32)
    return jnp.concatenate([lo, hi], axis=1)


def _out_router_kernel(xp_ref, ap_ref, bp_ref, xs_ref, as_ref, bs_ref, wo_ref, g_ref, wr_ref, br_ref,
                       x1_ref, xn_ref, route_ref, cnt_ref, *, prompt_tiles):
    i = pl.program_id(0)

    @pl.when(i == 0)
    def _():
        cnt_ref[...] = jnp.zeros_like(cnt_ref)

    def body(x_ref, a_ref, b_ref):
        mix = jnp.concatenate([a_ref[0], a_ref[1], a_ref[2], a_ref[3]], axis=1).astype(BF16)
        mix = jnp.concatenate([mix, b_ref[...].astype(BF16)], axis=1)
        x1 = x_ref[...] + jnp.dot(mix, wo_ref[...], preferred_element_type=F32)
        x1_ref[...] = x1
        ms = jnp.mean(x1 * x1, axis=-1, keepdims=True)
        xn = x1 * lax.rsqrt(ms + EPS) * g_ref[...]
        xn_ref[...] = _pack_bf16_pairs(xn)
        xh = xn.astype(BF16)
        xl = (xn - xh.astype(F32)).astype(BF16)
        logits = jnp.dot(jnp.concatenate([xh, xl, xh], axis=1), wr_ref[...], preferred_element_type=F32)
        route = _route(logits + br_ref[...])
        route_ref[...] = route
        lanef = lax.broadcasted_iota(jnp.int32, route.shape, 1).astype(F32)
        hits = (lanef == route[:, 0:1]).astype(F32) + (lanef == route[:, 1:2]).astype(F32)
        cnt_ref[...] += jnp.sum(hits, axis=0, keepdims=True)

    @pl.when(i < prompt_tiles)
    def _():
        body(xp_ref, ap_ref, bp_ref)

    @pl.when(i >= prompt_tiles)
    def _():
        body(xs_ref, as_ref, bs_ref)


def _out_router(xp, a4p, bp, xs, a4s, bs, wo, gamma, wr, br):
    tp, tsm = xp.shape[0], xs.shape[0]
    tm = 512
    npt, nst = tp // tm, tsm // tm
    t = tp + tsm
    pmap = lambda i: (jnp.minimum(i, npt - 1), 0)
    smap = lambda i: (jnp.maximum(i - npt, 0), 0)
    return pl.pallas_call(
        functools.partial(_out_router_kernel, prompt_tiles=npt),
        grid=(npt + nst,),
        in_specs=[
            pl.BlockSpec((tm, D_MODEL), pmap),
            pl.BlockSpec((4, tm, LANES), lambda i: (0, jnp.minimum(i, npt - 1), 0)),
            pl.BlockSpec((tm, 512), pmap),
            pl.BlockSpec((tm, D_MODEL), smap),
            pl.BlockSpec((4, tm, LANES), lambda i: (0, jnp.maximum(i - npt, 0), 0)),
            pl.BlockSpec((tm, 512), smap),
            pl.BlockSpec((D_MODEL, D_MODEL), lambda i: (0, 0)),
            pl.BlockSpec((1, D_MODEL), lambda i: (0, 0)),
            pl.BlockSpec((3 * D_MODEL, LANES), lambda i: (0, 0)),
            pl.BlockSpec((1, LANES), lambda i: (0, 0)),
        ],
        out_specs=[
            pl.BlockSpec((tm, D_MODEL), lambda i: (i, 0)),
            pl.BlockSpec((tm, D_MODEL // 2), lambda i: (i, 0)),
            pl.BlockSpec((tm, LANES), lambda i: (i, 0)),
            pl.BlockSpec((1, LANES), lambda i: (0, 0)),
        ],
        out_shape=[
            jax.ShapeDtypeStruct((t, D_MODEL), F32),
            jax.ShapeDtypeStruct((t, D_MODEL // 2), jnp.int32),
            jax.ShapeDtypeStruct((t, LANES), F32),
            jax.ShapeDtypeStruct((1, LANES), F32),
        ],
        compiler_params=_cparams(("arbitrary",)),
        name="out_router",
    )(xp, a4p, bp, xs, a4s, bs, wo, gamma, wr, br)


def _sc_gather_rows(table, idx):
    b = idx.shape[0]
    d = table.shape[1]
    per_worker = b // SC_WORKERS
    nwin = per_worker // SC_WINDOW
    assert per_worker * SC_WORKERS == b and nwin * SC_WINDOW == per_worker
    mesh = plsc.VectorSubcoreMesh(core_axis_name="c", subcore_axis_name="s")

    @functools.partial(
        pl.kernel, mesh=mesh,
        out_type=jax.ShapeDtypeStruct((b, d), table.dtype),
        scratch_types=[pltpu.VMEM((SC_WINDOW,), jnp.int32), pltpu.VMEM((SC_WINDOW, d), table.dtype),
                       pltpu.SemaphoreType.DMA],
        name="sc_gather_rows",
    )
    def gather(table_hbm, idx_hbm, out_hbm, idx_v, rows_v, sem):
        wid = lax.axis_index("s") * SC_CORES + lax.axis_index("c")
        base = wid * per_worker

        @pl.loop(0, nwin)
        def _(j):
            off = pl.multiple_of(base + j * SC_WINDOW, SC_WINDOW)
            pltpu.sync_copy(idx_hbm.at[pl.ds(off, SC_WINDOW)], idx_v)
            pltpu.async_copy(table_hbm.at[idx_v], rows_v, sem).wait()
            pltpu.sync_copy(rows_v, out_hbm.at[pl.ds(off, SC_WINDOW)])

    return gather(table, idx)


def _sc_scatter_rows(x, dest2, nrows):
    t, d = x.shape
    per_worker = t // SC_WORKERS
    nwin = per_worker // SC_SCATTER_WINDOW
    assert per_worker * SC_WORKERS == t and nwin * SC_SCATTER_WINDOW == per_worker
    mesh = plsc.VectorSubcoreMesh(core_axis_name="c", subcore_axis_name="s")

    @functools.partial(
        pl.kernel, mesh=mesh,
        out_type=jax.ShapeDtypeStruct((nrows, d), x.dtype),
        scratch_types=[pltpu.VMEM((TOP_K, SC_SCATTER_WINDOW), jnp.int32),
                       pltpu.VMEM((SC_SCATTER_WINDOW, d), x.dtype), pltpu.SemaphoreType.DMA],
        name="sc_scatter_rows",
    )
    def scatter(x_hbm, dest_hbm, out_hbm, idx_v, rows_v, sem):
        wid = lax.axis_index("s") * SC_CORES + lax.axis_index("c")
        base = wid * per_worker

        @pl.loop(0, nwin)
        def _(j):
            off = pl.multiple_of(base + j * SC_SCATTER_WINDOW, SC_SCATTER_WINDOW)
            pltpu.sync_copy(x_hbm.at[pl.ds(off, SC_SCATTER_WINDOW)], rows_v)
            for k in range(TOP_K):
                pltpu.sync_copy(dest_hbm.at[k, pl.ds(off, SC_SCATTER_WINDOW)], idx_v.at[k])
            for k in range(TOP_K):
                pltpu.async_copy(rows_v, out_hbm.at[idx_v.at[k]], sem).wait()

    return scatter(x, dest2)


def _expert_kernel(be_ref, nu_ref, nv_ref, x_ref, wg_ref, wu_ref, wd_ref, o_ref, wg_s, wu_s, wd_s):
    i = pl.program_id(0)
    used = i < nu_ref[0]
    changed = jnp.logical_or(i == 0, be_ref[i] != be_ref[jnp.maximum(i - 1, 0)])

    @pl.when(jnp.logical_and(used, changed))
    def _():
        wg_s[...] = wg_ref[...].astype(BF16)
        wu_s[...] = wu_ref[...].astype(BF16)
        wd_s[...] = wd_ref[...].astype(BF16)

    @pl.when(used)
    def _():
        row = lax.broadcasted_iota(jnp.int32, x_ref.shape, 0)
        x = _unpack_bf16_pairs(jnp.where(row < nv_ref[i], x_ref[...], 0)).astype(BF16)
        gate = jnp.dot(x, wg_s[...], preferred_element_type=F32)
        up = jnp.dot(x, wu_s[...], preferred_element_type=F32)
        h = (gate * jax.nn.sigmoid(gate) * up).astype(BF16)
        o_ref[...] = _pack_bf16_pairs(jnp.dot(h, wd_s[...], preferred_element_type=F32))

    @pl.when(jnp.logical_not(used))
    def _():
        o_ref[...] = jnp.zeros_like(o_ref)


def _experts(blk_e, n_used, nvalid, xb, w_gate, w_up, w_down):
    rows = xb.shape[0]
    nblocks = rows // MOE_ROWS
    grid_spec = pltpu.PrefetchScalarGridSpec(
        num_scalar_prefetch=3,
        grid=(nblocks,),
        in_specs=[
            pl.BlockSpec((MOE_ROWS, D_MODEL // 2), lambda i, be, nu, nv: (i, 0)),
            pl.BlockSpec((None, D_MODEL, D_EXPERT), lambda i, be, nu, nv: (be[i], 0, 0)),
            pl.BlockSpec((None, D_MODEL, D_EXPERT), lambda i, be, nu, nv: (be[i], 0, 0)),
            pl.BlockSpec((None, D_EXPERT, D_MODEL), lambda i, be, nu, nv: (be[i], 0, 0)),
        ],
        out_specs=pl.BlockSpec((MOE_ROWS, D_MODEL // 2), lambda i, be, nu, nv: (i, 0)),
        scratch_shapes=[pltpu.VMEM((D_MODEL, D_EXPERT), BF16), pltpu.VMEM((D_MODEL, D_EXPERT), BF16),
                        pltpu.VMEM((D_EXPERT, D_MODEL), BF16)],
    )
    return pl.pallas_call(
        _expert_kernel,
        grid_spec=grid_spec,
        out_shape=jax.ShapeDtypeStruct((rows, D_MODEL // 2), jnp.int32),
        compiler_params=_cparams(("arbitrary",)),
        name="experts",
    )(blk_e, n_used, nvalid, xb, w_gate, w_up, w_down)


def _combine_kernel(x1_ref, y1_ref, y2_ref, route_ref, g_ref, outp_ref, outs_ref, *, prompt_tiles):
    r = route_ref[...]
    x = (x1_ref[...] + r[:, 2:3] * _unpack_bf16_pairs(y1_ref[...])
         + r[:, 3:4] * _unpack_bf16_pairs(y2_ref[...]))
    ms = jnp.mean(x * x, axis=-1, keepdims=True)
    y = x * lax.rsqrt(ms + EPS) * g_ref[...]
    i = pl.program_id(0)

    @pl.when(i < prompt_tiles)
    def _():
        outp_ref[...] = y

    @pl.when(i >= prompt_tiles)
    def _():
        outs_ref[...] = y


def _combine_norm(x1, ygath, route, gamma, tp):
    t = x1.shape[0]
    tm = 512
    nt, npt = t // tm, tp // tm
    return pl.pallas_call(
        functools.partial(_combine_kernel, prompt_tiles=npt),
        grid=(nt,),
        in_specs=[
            pl.BlockSpec((tm, D_MODEL), lambda i: (i, 0)),
            pl.BlockSpec((tm, D_MODEL // 2), lambda i: (i, 0)),
            pl.BlockSpec((tm, D_MODEL // 2), lambda i: (i + nt, 0)),
            pl.BlockSpec((tm, LANES), lambda i: (i, 0)),
            pl.BlockSpec((1, D_MODEL), lambda i: (0, 0)),
        ],
        out_specs=[
            pl.BlockSpec((tm, D_MODEL), lambda i: (jnp.minimum(i, npt - 1), 0)),
            pl.BlockSpec((tm, D_MODEL), lambda i: (jnp.maximum(i - npt, 0), 0)),
        ],
        out_shape=[jax.ShapeDtypeStruct((tp, D_MODEL), F32), jax.ShapeDtypeStruct((t - tp, D_MODEL), F32)],
        compiler_params=_cparams(("arbitrary",)),
        name="combine_norm",
    )(x1, ygath, ygath, route, gamma)


BAND_PERIOD = 2 * QB + 1


def _band_tiles(h):
    nk = 2 * QB
    lead, heads = h.shape[:-2], h.shape[-2]
    flat = jnp.tile(h, (1,) * (h.ndim - 1) + (nk,))[..., :nk * nk]
    a = flat.reshape(lead + (heads // 2, 2, nk, nk))[..., :QB]
    a = jnp.swapaxes(a, -3, -2).reshape(lead + (heads // 2, nk, nk))
    prev = (np.arange(nk) < QB)[:, None]
    return jnp.stack([a, jnp.where(prev, NEG, a)], axis=len(lead))


def _band_index():
    c = (BAND_PERIOD - np.arange(BAND_PERIOD)) % BAND_PERIOD
    return c, c <= QB


def _bias_a_prompt(table_a):
    c, valid = _band_index()
    idx = np.stack([_t5_bucket_np(d * np.clip(QB - c, 0, QB)) for d in DILATIONS])
    h = jnp.where(valid, jnp.transpose(table_a[idx], (0, 2, 1)) * LOG2E, NEG)
    return _band_tiles(h)


def _bias_b_prompt(table_b):
    c, valid = _band_index()
    valid = valid & (c >= 1)
    h = jnp.where(valid, table_b[_t5_bucket_np(np.clip(QB - c, 0, QB))].T * LOG2E, NEG)
    h = jnp.transpose(h.reshape(KV_B, G_B, BAND_PERIOD), (1, 0, 2)).reshape(H_B, BAND_PERIOD)
    return _band_tiles(h)


def _sample_bias(table, span, t, log2_weight):
    cols = span + LANES
    period = cols + LANES
    x = np.arange(period)
    dist = np.where(x >= period - t, span - x + period, span - x)
    extra = log2_weight(dist)
    valid = np.isfinite(extra)
    u = jnp.where(valid, table[_t5_bucket_np(np.maximum(dist, 0))].T * LOG2E
                  + np.where(valid, extra, 0.0).astype(np.float32), NEG)
    rows = jnp.tile(u, (1, t))[:, :t * (period - 1)].reshape(u.shape[0], t, period - 1)[:, :, :cols]
    return rows.reshape(u.shape[0] * t, cols)


def _bias_a_sample(table_a, t):
    def log2_count(dist):
        count = np.zeros(dist.shape, np.int64)
        for w, d in zip(WINDOWS, DILATIONS):
            count += (dist >= 0) & (dist % d == 0) & (dist <= w)
        return np.where(count > 0, np.log2(np.maximum(count, 1)), -np.inf)

    return _sample_bias(table_a, WIN_A, t, log2_count)


def _bias_b_sample(table_b, t):
    return _sample_bias(table_b, WIN_B, t,
                        lambda dist: np.where((dist >= 0) & (dist < WIN_B), 0.0, -np.inf))


def _dest_kernel(route_ref, cnt_ref, tri_ref, dest_ref, meta_ref, run_scr, pst_scr):
    i = pl.program_id(0)
    tm = route_ref.shape[0]
    r = route_ref[...]
    lane = lax.broadcasted_iota(jnp.int32, (tm, LANES), 1)
    lanef = lane.astype(F32)
    oh0 = lanef == r[:, 0:1]
    oh1 = lanef == r[:, 1:2]
    ohf = jnp.concatenate([oh0, oh1], axis=0).astype(F32)

    @pl.when(i == 0)
    def _():
        cnt = jnp.broadcast_to(cnt_ref[...], (LANES, LANES))
        padded = jnp.floor((cnt + (MOE_ROWS - 1)) * (1.0 / MOE_ROWS)) * MOE_ROWS
        lane_e = lax.broadcasted_iota(jnp.int32, (LANES, LANES), 1)
        x = padded
        for sh in (1, 2, 4, 8, 16, 32, 64):
            x = x + jnp.where(lane_e >= sh, pltpu.roll(x, sh, 1), 0.0)
        pst_scr[...] = (x - padded)[0:1]
        run_scr[...] = jnp.zeros_like(run_scr)
        wide = lambda v: jnp.concatenate([v.T, v.T], axis=1)
        cnt_t, bend_t = wide(cnt), wide(x * (1.0 / MOE_ROWS))
        bstart_t = wide((x - padded) * (1.0 / MOE_ROWS))
        blk = lax.broadcasted_iota(jnp.int32, (LANES, 2 * LANES), 1).astype(F32)
        exp = lax.broadcasted_iota(jnp.int32, (LANES, 2 * LANES), 0)
        real = exp < N_EXPERTS
        blk_e = jnp.minimum(jnp.sum(jnp.where(real & (bend_t <= blk), 1.0, 0.0), axis=0, keepdims=True),
                            N_EXPERTS - 1.0)
        mine = exp.astype(F32) == blk_e
        within = blk[0:1] - jnp.sum(jnp.where(mine, bstart_t, 0.0), axis=0, keepdims=True)
        nvalid = jnp.clip(jnp.sum(jnp.where(mine, cnt_t, 0.0), axis=0, keepdims=True) - within * MOE_ROWS,
                          0.0, float(MOE_ROWS))
        n_used = jnp.max(jnp.where(real, bend_t, 0.0), axis=0, keepdims=True)
        meta_ref[...] = jnp.concatenate([blk_e, nvalid, n_used, jnp.zeros((5, 2 * LANES), F32)],
                                        axis=0).astype(jnp.int32)

    csum = jnp.dot(tri_ref[...], ohf.astype(BF16), preferred_element_type=F32)
    val = csum + (run_scr[...] + pst_scr[...] - 1.0)
    d0 = jnp.sum(jnp.where(oh0, val[:tm], 0.0), axis=-1, keepdims=True)
    d1 = jnp.sum(jnp.where(oh1, val[tm:], 0.0), axis=-1, keepdims=True)
    tile = jnp.where(lane == 0, d0, jnp.where(lane == 1, d1, 0.0))
    dest_ref[...] = tile.T[:8].astype(jnp.int32)
    run_scr[...] += jnp.sum(ohf, axis=0, keepdims=True)


def _dispatch(route, cnt):
    t = route.shape[0]
    tm = 512
    tri = (jnp.arange(2 * tm)[:, None] >= jnp.arange(2 * tm)[None, :]).astype(BF16)
    nblocks = -(-t * TOP_K // MOE_ROWS) + N_EXPERTS
    assert nblocks <= 2 * LANES
    dest, meta = pl.pallas_call(
        _dest_kernel,
        grid=(t // tm,),
        in_specs=[pl.BlockSpec((tm, LANES), lambda i: (i, 0)),
                  pl.BlockSpec((1, LANES), lambda i: (0, 0)),
                  pl.BlockSpec((2 * tm, 2 * tm), lambda i: (0, 0))],
        out_specs=[pl.BlockSpec((8, tm), lambda i: (0, i)),
                   pl.BlockSpec((8, 2 * LANES), lambda i: (0, 0))],
        out_shape=[jax.ShapeDtypeStruct((8, t), jnp.int32), jax.ShapeDtypeStruct((8, 2 * LANES), jnp.int32)],
        scratch_shapes=[pltpu.VMEM((1, LANES), F32), pltpu.VMEM((1, LANES), F32)],
        compiler_params=_cparams(("arbitrary",)),
        name="moe_dest",
    )(route, cnt, tri)
    return dest[:TOP_K], meta[0, :nblocks], meta[2, :1], meta[1, :nblocks]


def kernel(x_prompt, x_sample, cache_a_k, cache_a_v, cache_b_k, cache_b_v, rel_bias_table, attn_norm, w_in,
           w_out, attn_sinks, ffn_norm, w_router_group, b_router_group, w_router_expert, b_router_expert,
           w_gate, w_up, w_down, final_norm):
    s = x_prompt.shape[1]
    ns, ts = x_sample.shape[0], x_sample.shape[1]
    table_a = rel_bias_table[:, :H_A]
    table_b = rel_bias_table[:, H_A:]

    w = w_in[0]
    wqa, wka, wva, wqb, wkb, wvb = (w[:, 0:512], w[:, 512:1024], w[:, 1024:1536], w[:, 1536:2048],
                                    w[:, 2048:2176], w[:, 2176:2304])
    wqb = jnp.transpose(wqb.reshape(D_MODEL, KV_B, G_B, HEAD_DIM), (0, 2, 1, 3)).reshape(D_MODEL, 512)
    wp = jnp.concatenate([wka, wva, wqa, wqb, wkb, wvb], axis=1).astype(BF16)
    cscale = jnp.concatenate([jnp.ones((1, 1024), F32), jnp.full((1, 1024), SCALE * LOG2E, F32),
                              jnp.ones((1, 256), F32)], axis=1)
    wo = w_out[0]
    wo_b = jnp.transpose(wo[512:].reshape(KV_B, G_B, HEAD_DIM, D_MODEL), (1, 0, 2, 3)).reshape(512, D_MODEL)
    wo_p = jnp.concatenate([wo[:512], wo_b], axis=0).astype(BF16)
    wr = jnp.concatenate([w_router_group[0],
                          jnp.transpose(w_router_expert[0], (1, 0, 2)).reshape(D_MODEL, N_EXPERTS),
                          jnp.zeros((D_MODEL, LANES - N_GROUPS - N_EXPERTS), F32)], axis=1)
    wr_hi = wr.astype(BF16)
    wr = jnp.concatenate([wr_hi, wr_hi, (wr - wr_hi.astype(F32)).astype(BF16)], axis=0)
    br = jnp.concatenate([b_router_group[0], b_router_expert[0].reshape(N_EXPERTS),
                          jnp.zeros((LANES - N_GROUPS - N_EXPERTS,), F32)]).reshape(1, LANES)
    sinks2 = attn_sinks[0] * LOG2E
    sinks_gk = jnp.transpose(sinks2.reshape(KV_B, G_B), (1, 0)).reshape(H_B)
    sink_rows_p = jnp.repeat(sinks_gk, QB).reshape(G_B, 1, 2 * QB)
    sink_rows_s = jnp.repeat(sinks2, ts).reshape(H_B * ts, 1)
    emat = jnp.tile(jnp.arange(LANES)[:, None] == (jnp.arange(A_WIDTH)[None, :] // HEAD_DIM),
                    (3, 1)).astype(BF16)
    attn_g = attn_norm[0].reshape(1, D_MODEL)
    ffn_g = ffn_norm[0].reshape(1, D_MODEL)

    xp = x_prompt.reshape(s, D_MODEL)
    aperm, qb_p, kvb_p, akv32, bkv32 = _proj_prompt(xp, attn_g, wp, cscale)
    a4 = _attn_a_prompt(aperm, _bias_a_prompt(table_a), emat)
    ob_p = _attn_b_prompt(qb_p, kvb_p, _bias_b_prompt(table_b), sink_rows_p)

    xs = x_sample.reshape(ns * ts, D_MODEL)
    q_s, kv_s = _proj_sample(xs, attn_g, wp, cscale)
    akt = jnp.transpose(cache_a_k[0], (0, 2, 3, 1)).reshape(ns, A_WIDTH, WIN_A)
    avt = jnp.transpose(cache_a_v[0], (0, 2, 3, 1)).reshape(ns, A_WIDTH, WIN_A)
    bkt = jnp.transpose(cache_b_k[0], (0, 2, 3, 1)).reshape(ns, LANES, WIN_B)
    bvt = jnp.transpose(cache_b_v[0], (0, 2, 3, 1)).reshape(ns, LANES, WIN_B)
    oa_s, ob_s = _attn_sample(q_s.reshape(ns, ts, 1024), kv_s.reshape(ns, ts, 1280), akt, avt, bkt, bvt,
                              _bias_a_sample(table_a, ts), _bias_b_sample(table_b, ts), sink_rows_s)
    a4_s = jnp.transpose(oa_s.reshape(ns * ts, 4, LANES), (1, 0, 2))

    x1, xn, route, cnt = _out_router(xp, a4, ob_p, xs, a4_s, ob_s.reshape(ns * ts, 512), wo_p, ffn_g, wr, br)
    dest2, blk_e, n_used, nvalid = _dispatch(route, cnt)
    xb = _sc_scatter_rows(xn, dest2, blk_e.shape[0] * MOE_ROWS)
    yb = _experts(blk_e, n_used, nvalid, xb, w_gate[0], w_up[0], w_down[0])
    y_p, y_s = _combine_norm(x1, _sc_gather_rows(yb, dest2.reshape(-1)), route, final_norm.reshape(1, D_MODEL), s)

    y_prompt = y_p.reshape(1, s, D_MODEL)
    y_sample = y_s.reshape(ns, ts, D_MODEL)
    keep_a, keep_b = min(WIN_A, s), min(WIN_B, s)
    pak = akv32[s - keep_a:, :512].reshape(1, 1, keep_a, H_A, HEAD_DIM)
    pav = akv32[s - keep_a:, 512:].reshape(1, 1, keep_a, H_A, HEAD_DIM)
    pbk = bkv32[s - keep_b:, :128].reshape(1, 1, keep_b, KV_B, HEAD_DIM)
    pbv = bkv32[s - keep_b:, 128:].reshape(1, 1, keep_b, KV_B, HEAD_DIM)
    sak = kv_s[:, 0:512].reshape(1, ns, ts, H_A, HEAD_DIM)
    sav = kv_s[:, 512:1024].reshape(1, ns, ts, H_A, HEAD_DIM)
    sbk = kv_s[:, 1024:1152].reshape(1, ns, ts, KV_B, HEAD_DIM)
    sbv = kv_s[:, 1152:1280].reshape(1, ns, ts, KV_B, HEAD_DIM)
    return (y_prompt, y_sample, pak, pav, pbk, pbv, sak, sav, sbk, sbv)
```

```python
import functools
import math

import jax
import jax.numpy as jnp
import numpy as np
from jax import lax
from jax.experimental import pallas as pl
from jax.experimental.pallas import tpu as pltpu
from jax.experimental.pallas import tpu_sc as plsc

D_MODEL = 1024
HEAD_DIM = 64
H_A = 8
H_B = 8
KV_B = 2
G_B = 4
DILATIONS = (1, 4, 16)
WINDOWS = (128, 512, 2048)
WIN_A = 2048
WIN_B = 128
NUM_BUCKETS = 32
MAX_DISTANCE = 2048
N_GROUPS = 4
EXPERTS_PER_GROUP = 8
N_EXPERTS = 32
TOP_K = 2
D_EXPERT = 512
EPS = 1e-5
SCALE = HEAD_DIM ** -0.5
PAST_LEN = 16384

LANES = 128
SPAN = 2048
QB = 128
NCHUNK = 9
A_WIDTH = H_A * HEAD_DIM
MOE_ROWS = 512
SC_CORES = 2
SC_SUBCORES = 16
SC_WORKERS = SC_CORES * SC_SUBCORES
SC_WINDOW = 64
SC_SCATTER_WINDOW = 32
NEG = -1e30
LOG2E = math.log2(math.e)
B_STEP = 512
VMEM_LIMIT = 56 * 1024 * 1024

F32 = jnp.float32
BF16 = jnp.bfloat16


def _t5_bucket_np(dist):
    dist = np.asarray(dist, np.int64)
    max_exact = NUM_BUCKETS // 2
    d = np.maximum(dist, 1).astype(np.float32)
    ratio = np.log(d / np.float32(max_exact)) / np.float32(math.log(MAX_DISTANCE / max_exact))
    large = max_exact + (ratio * np.float32(NUM_BUCKETS - max_exact)).astype(np.int32)
    large = np.minimum(large, NUM_BUCKETS - 1)
    return np.where(dist < max_exact, dist, large).astype(np.int32)


def _cparams(sem, vmem=VMEM_LIMIT):
    return pltpu.CompilerParams(dimension_semantics=sem, vmem_limit_bytes=vmem)


def _proj_prompt_kernel(x_ref, g_ref, w_ref, cs_ref, aperm_ref, qb_ref, kvb_ref, akv_ref, bkv_ref,
                        h_scr, p_scr):
    n = pl.program_id(1)

    @pl.when(n == 0)
    def _():
        x = x_ref[...]
        ms = jnp.mean(x * x, axis=-1, keepdims=True)
        h_scr[...] = (x * lax.rsqrt(ms + EPS) * g_ref[...]).astype(BF16)

    p = jnp.dot(h_scr[...], w_ref[...], preferred_element_type=F32) * cs_ref[...]

    @pl.when(n < 6)
    def _():
        aperm_ref[0] = p.astype(BF16)
        p_scr[0, 0] = p[:, :LANES]
        p_scr[0, 1] = p[:, LANES:]
        quarter = SPAN // 4
        for r in range(4):
            lo = p_scr[0, 0, pl.ds(r, quarter, stride=4), :]
            hi = p_scr[0, 1, pl.ds(r, quarter, stride=4), :]
            p_scr[1, 0, r * quarter:(r + 1) * quarter, :] = lo
            p_scr[1, 1, r * quarter:(r + 1) * quarter, :] = hi
            aperm_ref[1, r * quarter:(r + 1) * quarter, :] = jnp.concatenate([lo, hi], axis=1).astype(BF16)
        for r16 in range(16):
            start = (r16 % 4) * quarter + r16 // 4
            t = jnp.concatenate([p_scr[1, 0, pl.ds(start, QB, stride=4), :],
                                 p_scr[1, 1, pl.ds(start, QB, stride=4), :]], axis=1)
            aperm_ref[2, r16 * QB:(r16 + 1) * QB, :] = t.astype(BF16)

    @pl.when(n < 4)
    def _():
        akv_ref[...] = p

    @pl.when(jnp.logical_or(n == 6, n == 7))
    def _():
        qb_ref[...] = p.astype(BF16)

    @pl.when(n == 8)
    def _():
        kvb_ref[...] = p.astype(BF16)
        bkv_ref[...] = p


def _proj_prompt(x, gamma, w, cscale):
    s = x.shape[0]
    nspan = s // SPAN
    return pl.pallas_call(
        _proj_prompt_kernel,
        grid=(nspan, NCHUNK),
        in_specs=[
            pl.BlockSpec((SPAN, D_MODEL), lambda b, n: (b, 0)),
            pl.BlockSpec((1, D_MODEL), lambda b, n: (0, 0)),
            pl.BlockSpec((D_MODEL, 256), lambda b, n: (0, n)),
            pl.BlockSpec((1, 256), lambda b, n: (0, n)),
        ],
        out_specs=[
            pl.BlockSpec((3, SPAN, 256), lambda b, n: (0, b, jnp.minimum(n, 5))),
            pl.BlockSpec((SPAN, 256), lambda b, n: (b, jnp.clip(n - 6, 0, 1))),
            pl.BlockSpec((SPAN, 256), lambda b, n: (b, 0)),
            pl.BlockSpec((SPAN, 256), lambda b, n: (b, jnp.minimum(n, 3))),
            pl.BlockSpec((SPAN, 256), lambda b, n: (b, 0)),
        ],
        out_shape=[
            jax.ShapeDtypeStruct((3, s, 3 * A_WIDTH), BF16),
            jax.ShapeDtypeStruct((s, 512), BF16),
            jax.ShapeDtypeStruct((s, 256), BF16),
            jax.ShapeDtypeStruct((s, 1024), F32),
            jax.ShapeDtypeStruct((s, 256), F32),
        ],
        scratch_shapes=[pltpu.VMEM((SPAN, D_MODEL), BF16), pltpu.VMEM((2, 2, SPAN, LANES), F32)],
        compiler_params=_cparams(("arbitrary", "arbitrary")),
        name="proj_prompt",
    )(x, gamma, w, cscale)


def _proj_sample_kernel(x_ref, g_ref, w_ref, cs_ref, q_ref, kv_ref):
    x = x_ref[...]
    ms = jnp.mean(x * x, axis=-1, keepdims=True)
    h = (x * lax.rsqrt(ms + EPS) * g_ref[...]).astype(BF16)
    p = jnp.dot(h, w_ref[...], preferred_element_type=F32) * cs_ref[...]
    kv_ref[:, :1024] = p[:, :1024]
    kv_ref[:, 1024:] = p[:, 2048:]
    q_ref[...] = p[:, 1024:2048]


def _proj_sample(x, gamma, w, cscale):
    t = x.shape[0]
    tm = 512
    return pl.pallas_call(
        _proj_sample_kernel,
        grid=(t // tm,),
        in_specs=[
            pl.BlockSpec((tm, D_MODEL), lambda i: (i, 0)),
            pl.BlockSpec((1, D_MODEL), lambda i: (0, 0)),
            pl.BlockSpec((D_MODEL, 2304), lambda i: (0, 0)),
            pl.BlockSpec((1, 2304), lambda i: (0, 0)),
        ],
        out_specs=[
            pl.BlockSpec((tm, 1024), lambda i: (i, 0)),
            pl.BlockSpec((tm, 1280), lambda i: (i, 0)),
        ],
        out_shape=[
            jax.ShapeDtypeStruct((t, 1024), F32),
            jax.ShapeDtypeStruct((t, 1280), F32),
        ],
        compiler_params=_cparams(("arbitrary",)),
        name="proj_sample",
    )(x, gamma, w, cscale)


def _spread_heads(w, e3_ref):
    hi = w.astype(BF16)
    r1 = w - hi.astype(F32)
    mid = r1.astype(BF16)
    low = (r1 - mid.astype(F32)).astype(BF16)
    return jnp.dot(jnp.concatenate([hi, mid, low], axis=1), e3_ref[...], preferred_element_type=F32)


def _pair_tile(q2, kk, vv, bias_t, lo, sink=None):
    zero = jnp.zeros_like(q2)
    qq = jnp.concatenate([jnp.where(lo, q2, zero), jnp.where(lo, zero, q2)], axis=0)
    st = lax.dot_general(kk, qq, (((1,), (1,)), ((), ())), preferred_element_type=F32)
    st = st + bias_t
    m = jnp.max(st, axis=0, keepdims=True)
    if sink is not None:
        m = jnp.maximum(m, sink)
    p = jnp.exp2(st - m)
    den = jnp.sum(p, axis=0, keepdims=True)
    if sink is not None:
        den = den + jnp.exp2(sink - m)
    pn = (p * (1.0 / den)).astype(BF16)
    o = lax.dot_general(pn, vv, (((0,), (0,)), ((), ())), preferred_element_type=F32)
    return jnp.where(lo, o[:QB], o[QB:]), m + jnp.log2(den)


def _attn_a_kernel(q_ref, kvc_ref, kvp_ref, bias_ref, e_ref, out_ref, o_scr, st_scr):
    b = pl.program_id(0)
    g = pl.program_id(1)
    nblk = jnp.where(g == 0, 16, jnp.where(g == 1, 4, 1))
    lane = lax.broadcasted_iota(jnp.int32, (QB, LANES), 1)
    lo = lane < HEAD_DIM

    for cb in range(SPAN // QB):
        first = lax.rem(jnp.int32(cb), nblk) == 0
        rows = slice(cb * QB, (cb + 1) * QB)
        prow_c = max(cb - 1, 0) * QB
        prow_p = pl.multiple_of(jnp.where(first, cb + nblk - 1, 0) * QB, QB)
        variant = jnp.logical_and(first, b == 0).astype(jnp.int32)
        stats = []
        for hp in range(4):
            ks = slice(hp * LANES, (hp + 1) * LANES)
            vs = slice(A_WIDTH + hp * LANES, A_WIDTH + (hp + 1) * LANES)
            kp = jnp.where(first, kvp_ref[pl.ds(prow_p, QB), ks], kvc_ref[prow_c:prow_c + QB, ks])
            vp = jnp.where(first, kvp_ref[pl.ds(prow_p, QB), vs], kvc_ref[prow_c:prow_c + QB, vs])
            kk = jnp.concatenate([kp, kvc_ref[rows, ks]], axis=0)
            vv = jnp.concatenate([vp, kvc_ref[rows, vs]], axis=0)
            o, lse = _pair_tile(q_ref[rows, ks], kk, vv, bias_ref[variant, hp], lo)
            o_scr[g, hp, rows, :] = o
            stats += [lse[:, :QB], lse[:, QB:]]
        sm = jnp.concatenate(stats + [jnp.zeros((LANES - H_A, QB), F32)], axis=0)
        st_scr[g, rows, :] = sm.T

    @pl.when(g == 2)
    def _():
        def merge(c, carry):
            r2 = lax.rem(c, 4) * (SPAN // 4) + c // 4
            r3 = pl.multiple_of(c * QB, QB)
            l1 = st_scr[0, pl.ds(c, QB, stride=16), :]
            l2 = st_scr[1, pl.ds(r2, QB, stride=4), :]
            l3 = st_scr[2, pl.ds(r3, QB), :]
            mx = jnp.maximum(jnp.maximum(l1, l2), l3)
            w1 = jnp.exp2(l1 - mx)
            w2 = jnp.exp2(l2 - mx)
            w3 = jnp.exp2(l3 - mx)
            tot = w1 + w2 + w3
            a1 = _spread_heads(w1 / tot, e_ref)
            a2 = _spread_heads(w2 / tot, e_ref)
            a3 = _spread_heads(w3 / tot, e_ref)
            for hp in range(4):
                sl = slice(hp * LANES, (hp + 1) * LANES)
                o1 = o_scr[0, hp, pl.ds(c, QB, stride=16), :]
                o2 = o_scr[1, hp, pl.ds(r2, QB, stride=4), :]
                o3 = o_scr[2, hp, pl.ds(r3, QB), :]
                out_ref[hp, pl.ds(c, QB, stride=16), :] = a1[:, sl] * o1 + a2[:, sl] * o2 + a3[:, sl] * o3
            return carry

        lax.fori_loop(0, 16, merge, 0, unroll=4)


def _attn_a_prompt(aperm, bias_a, emat):
    s = aperm.shape[1]
    nspan = s // SPAN
    return pl.pallas_call(
        _attn_a_kernel,
        grid=(nspan, 3),
        in_specs=[
            pl.BlockSpec((None, SPAN, A_WIDTH), lambda b, g: (g, b, 2)),
            pl.BlockSpec((None, SPAN, 2 * A_WIDTH), lambda b, g: (g, b, 0)),
            pl.BlockSpec((None, SPAN, 2 * A_WIDTH), lambda b, g: (g, jnp.maximum(b - 1, 0), 0)),
            pl.BlockSpec((None, 2, 4, 2 * QB, 2 * QB), lambda b, g: (g, 0, 0, 0, 0)),
            pl.BlockSpec((3 * LANES, A_WIDTH), lambda b, g: (0, 0)),
        ],
        out_specs=pl.BlockSpec((4, SPAN, LANES), lambda b, g: (0, b, 0)),
        out_shape=jax.ShapeDtypeStruct((4, s, LANES), F32),
        scratch_shapes=[pltpu.VMEM((3, 4, SPAN, LANES), F32), pltpu.VMEM((3, SPAN, LANES), F32)],
        compiler_params=_cparams(("arbitrary", "arbitrary")),
        name="attn_a_prompt",
    )(aperm, aperm, aperm, bias_a, emat)


def _attn_b_kernel(q_ref, kvc_ref, kvp_ref, bias_ref, sink_ref, out_ref):
    i = pl.program_id(0)
    lane = lax.broadcasted_iota(jnp.int32, (QB, LANES), 1)
    lo = lane < HEAD_DIM
    variant = (i == 0).astype(jnp.int32)
    for j in range(B_STEP // QB):
        rows = slice(j * QB, (j + 1) * QB)
        if j == 0:
            kp, vp = kvp_ref[:, :LANES], kvp_ref[:, LANES:]
        else:
            kp, vp = kvc_ref[(j - 1) * QB:j * QB, :LANES], kvc_ref[(j - 1) * QB:j * QB, LANES:]
        kk = jnp.concatenate([kp, kvc_ref[rows, :LANES]], axis=0)
        vv = jnp.concatenate([vp, kvc_ref[rows, LANES:]], axis=0)
        for g in range(G_B):
            bias_t = bias_ref[variant, g] if j == 0 else bias_ref[0, g]
            o, _ = _pair_tile(q_ref[rows, g * LANES:(g + 1) * LANES], kk, vv, bias_t, lo, sink=sink_ref[g])
            out_ref[rows, g * LANES:(g + 1) * LANES] = o.astype(BF16)


def _attn_b_prompt(qb, kvb, bias_b, sink_rows):
    s = qb.shape[0]
    per = B_STEP // QB
    return pl.pallas_call(
        _attn_b_kernel,
        grid=(s // B_STEP,),
        in_specs=[
            pl.BlockSpec((B_STEP, 512), lambda i: (i, 0)),
            pl.BlockSpec((B_STEP, 256), lambda i: (i, 0)),
            pl.BlockSpec((QB, 256), lambda i: (jnp.maximum(i * per - 1, 0), 0)),
            pl.BlockSpec((2, G_B, 2 * QB, 2 * QB), lambda i: (0, 0, 0, 0)),
            pl.BlockSpec((G_B, 1, 2 * QB), lambda i: (0, 0, 0)),
        ],
        out_specs=pl.BlockSpec((B_STEP, 512), lambda i: (i, 0)),
        out_shape=jax.ShapeDtypeStruct((s, 512), BF16),
        compiler_params=_cparams(("arbitrary",)),
        name="attn_b_prompt",
    )(qb, kvb, kvb, bias_b, sink_rows)


def _attn_sample_kernel(q_ref, kvn_ref, akt_ref, avt_ref, bkt_ref, bvt_ref, cba_ref, cbb_ref, sink_ref,
                        oa_ref, ob_ref):
    t = q_ref.shape[0]
    q = q_ref[...]
    kvn = kvn_ref[...]
    kvn_p = jnp.concatenate([kvn, jnp.zeros((LANES - t, kvn.shape[1]), F32)], axis=0).astype(BF16)
    lane_a = lax.broadcasted_iota(jnp.int32, (t, A_WIDTH), 1) // HEAD_DIM

    qa = q[:, :A_WIDTH]
    qbd = jnp.concatenate([jnp.where(lane_a == h, qa, 0.0) for h in range(H_A)], axis=0).astype(BF16)
    s_c = jnp.dot(qbd, akt_ref[...].astype(BF16), preferred_element_type=F32)
    s_n = lax.dot_general(qbd, kvn_p[:, :A_WIDTH], (((1,), (1,)), ((), ())), preferred_element_type=F32)
    s = jnp.concatenate([s_c, s_n], axis=1) + cba_ref[...]
    m = jnp.max(s, axis=-1, keepdims=True)
    p = jnp.exp2(s - m)
    l = jnp.sum(p, axis=-1, keepdims=True)
    o_n = jnp.dot(p[:, WIN_A:].astype(BF16), kvn_p[:, A_WIDTH:2 * A_WIDTH], preferred_element_type=F32)
    pc = jnp.concatenate([p[:, :WIN_A], jnp.zeros((LANES - H_A * t, WIN_A), F32)], axis=0).astype(BF16)
    o_t = lax.dot_general(avt_ref[...].astype(BF16), pc, (((1,), (1,)), ((), ())),
                          preferred_element_type=F32)
    o_all = o_t.T[:H_A * t] + o_n
    o_sel = jnp.zeros((t, A_WIDTH), F32)
    l_b = jnp.ones((t, A_WIDTH), F32)
    for h in range(H_A):
        sel = lane_a == h
        o_sel = jnp.where(sel, o_all[h * t:(h + 1) * t], o_sel)
        l_b = jnp.where(sel, l[h * t:(h + 1) * t], l_b)
    oa_ref[...] = o_sel / l_b

    lane_b = lax.broadcasted_iota(jnp.int32, (G_B * t, LANES), 1)
    lo = lane_b < HEAD_DIM
    qb2 = jnp.concatenate([q[:, A_WIDTH + g * LANES:A_WIDTH + (g + 1) * LANES] for g in range(G_B)], axis=0)
    qm = jnp.concatenate([jnp.where(lo, qb2, 0.0), jnp.where(lo, 0.0, qb2)], axis=0).astype(BF16)
    kb_n = kvn_p[:, 2 * A_WIDTH:2 * A_WIDTH + LANES]
    vb_n = kvn_p[:, 2 * A_WIDTH + LANES:]
    sb_c = jnp.dot(qm, bkt_ref[...].astype(BF16), preferred_element_type=F32)
    sb_n = lax.dot_general(qm, kb_n, (((1,), (1,)), ((), ())), preferred_element_type=F32)
    sb = jnp.concatenate([sb_c, sb_n], axis=1) + cbb_ref[...]
    sink = sink_ref[...]
    mb = jnp.maximum(jnp.max(sb, axis=-1, keepdims=True), sink)
    pbb = jnp.exp2(sb - mb)
    den = jnp.sum(pbb, axis=-1, keepdims=True) + jnp.exp2(sink - mb)
    pbb = pbb.astype(BF16)
    ob = lax.dot_general(pbb[:, :WIN_B], bvt_ref[...].astype(BF16), (((1,), (1,)), ((), ())),
                         preferred_element_type=F32)
    ob = (ob + jnp.dot(pbb[:, WIN_B:], vb_n, preferred_element_type=F32)) / den
    half = G_B * t
    lo8 = lo[:t]
    for g in range(G_B):
        ob_ref[:, g * LANES:(g + 1) * LANES] = jnp.where(
            lo8, ob[g * t:(g + 1) * t], ob[half + g * t:half + (g + 1) * t])


def _attn_sample(q3, kvn3, akt, avt, bkt, bvt, cbias_a, cbias_b, sink_rows):
    ns, t = q3.shape[0], q3.shape[1]
    return pl.pallas_call(
        _attn_sample_kernel,
        grid=(ns,),
        in_specs=[
            pl.BlockSpec((None, t, 1024), lambda n: (n, 0, 0)),
            pl.BlockSpec((None, t, 1280), lambda n: (n, 0, 0)),
            pl.BlockSpec((None, A_WIDTH, WIN_A), lambda n: (n, 0, 0)),
            pl.BlockSpec((None, A_WIDTH, WIN_A), lambda n: (n, 0, 0)),
            pl.BlockSpec((None, LANES, WIN_B), lambda n: (n, 0, 0)),
            pl.BlockSpec((None, LANES, WIN_B), lambda n: (n, 0, 0)),
            pl.BlockSpec((H_A * t, WIN_A + LANES), lambda n: (0, 0)),
            pl.BlockSpec((H_B * t, WIN_B + LANES), lambda n: (0, 0)),
            pl.BlockSpec((H_B * t, 1), lambda n: (0, 0)),
        ],
        out_specs=[
            pl.BlockSpec((None, t, 512), lambda n: (n, 0, 0)),
            pl.BlockSpec((None, t, 512), lambda n: (n, 0, 0)),
        ],
        out_shape=[jax.ShapeDtypeStruct((ns, t, 512), F32), jax.ShapeDtypeStruct((ns, t, 512), F32)],
        compiler_params=_cparams(("arbitrary",)),
        name="attn_sample",
    )(q3, kvn3, akt, avt, bkt, bvt, cbias_a, cbias_b, sink_rows)


def _route(logits):
    lane = lax.broadcasted_iota(jnp.int32, logits.shape, 1).astype(F32)
    big = jnp.float32(1 << 20)
    ninf = jnp.float32(-jnp.inf)
    gmask = lane < N_GROUPS
    lg = jnp.where(gmask, logits, ninf)
    gmax = jnp.max(lg, axis=-1, keepdims=True)
    grp = jnp.min(jnp.where(lg == gmax, lane, big), axis=-1, keepdims=True)
    pg_top = 1.0 / jnp.sum(jnp.exp(lg - gmax), axis=-1, keepdims=True)
    e0 = N_GROUPS + grp * EXPERTS_PER_GROUP
    emask = jnp.logical_and(lane >= e0, lane < e0 + EXPERTS_PER_GROUP)
    le = jnp.where(emask, logits, ninf)
    emax = jnp.max(le, axis=-1, keepdims=True)
    esum = jnp.sum(jnp.exp(le - emax), axis=-1, keepdims=True)
    i1 = jnp.min(jnp.where(le == emax, lane, big), axis=-1, keepdims=True)
    le2 = jnp.where(lane == i1, ninf, le)
    e2max = jnp.max(le2, axis=-1, keepdims=True)
    i2 = jnp.min(jnp.where(le2 == e2max, lane, big), axis=-1, keepdims=True)
    p1 = 1.0 / esum
    p2 = jnp.exp(e2max - emax) / esum
    g1 = pg_top * p1 / (p1 + p2)
    g2 = pg_top * p2 / (p1 + p2)
    out = jnp.where(lane == 0, i1 - N_GROUPS, 0.0)
    out = jnp.where(lane == 1, i2 - N_GROUPS, out)
    out = jnp.where(lane == 2, g1, out)
    out = jnp.where(lane == 3, g2, out)
    return out


def _pack_bf16_pairs(x):
    half = x.shape[1] // 2

    def rne(v):
        bits = lax.bitcast_convert_type(v, jnp.int32)
        return bits + 0x7FFF + (lax.shift_right_logical(bits, 16) & 1)

    lo = lax.shift_right_logical(rne(x[:, :half]), 16)
    hi = rne(x[:, half:]) & jnp.int32(-65536)
    return lo | hi


def _unpack_bf16_pairs(w):
    lo = lax.bitcast_convert_type(lax.shift_left(w, 16), F32)
    hi = lax.bitcast_convert_type(w & jnp.int32(-65536), F32)
    return jnp.concatenate([lo, hi], axis=1)


def _out_router_kernel(xp_ref, ap_ref, bp_ref, xs_ref, as_ref, bs_ref, wo_ref, g_ref, wr_ref, br_ref,
                       x1_ref, xn_ref, route_ref, cnt_ref, xcat_scr, *, prompt_tiles, tiles):
    i = pl.program_id(0)

    @pl.when(i == 0)
    def _():
        cnt_ref[...] = jnp.zeros_like(cnt_ref)
        xcat_scr[...] = jnp.zeros_like(xcat_scr)

    slot = lax.rem(i, 2)

    def body(x_ref, a_ref, b_ref):
        logits = jnp.dot(xcat_scr[1 - slot], wr_ref[...], preferred_element_type=F32)
        route = _route(logits + br_ref[...])
        route_ref[...] = route
        lanef = lax.broadcasted_iota(jnp.int32, route.shape, 1).astype(F32)
        hits = (lanef == route[:, 0:1]).astype(F32) + (lanef == route[:, 1:2]).astype(F32)
        cnt_ref[...] += jnp.sum(hits, axis=0, keepdims=True) * (i > 0).astype(F32)

        mix = jnp.concatenate([a_ref[0], a_ref[1], a_ref[2], a_ref[3]], axis=1).astype(BF16)
        mix = jnp.concatenate([mix, b_ref[...].astype(BF16)], axis=1)
        x1 = x_ref[...] + jnp.dot(mix, wo_ref[...], preferred_element_type=F32)
        x1_ref[...] = x1
        ms = jnp.mean(x1 * x1, axis=-1, keepdims=True)
        xn = x1 * lax.rsqrt(ms + EPS) * g_ref[...]
        xn_ref[...] = _pack_bf16_pairs(xn)
        xh = xn.astype(BF16)
        xl = (xn - xh.astype(F32)).astype(BF16)
        xcat_scr[slot] = jnp.concatenate([xh, xl, xh], axis=1)

    @pl.when(i < prompt_tiles)
    def _():
        body(xp_ref, ap_ref, bp_ref)

    @pl.when(i >= prompt_tiles)
    def _():
        body(xs_ref, as_ref, bs_ref)


def _out_router(xp, a4p, bp, xs, a4s, bs, wo, gamma, wr, br):
    tp, tsm = xp.shape[0], xs.shape[0]
    tm = 512
    npt, nst = tp // tm, tsm // tm
    nt = npt + nst
    t = tp + tsm
    pmap = lambda i: (jnp.minimum(i, npt - 1), 0)
    smap = lambda i: (jnp.clip(i - npt, 0, nst - 1), 0)
    cur = lambda i: (jnp.minimum(i, nt - 1), 0)
    return pl.pallas_call(
        functools.partial(_out_router_kernel, prompt_tiles=npt, tiles=nt),
        grid=(nt + 1,),
        in_specs=[
            pl.BlockSpec((tm, D_MODEL), pmap),
            pl.BlockSpec((4, tm, LANES), lambda i: (0, jnp.minimum(i, npt - 1), 0)),
            pl.BlockSpec((tm, 512), pmap),
            pl.BlockSpec((tm, D_MODEL), smap),
            pl.BlockSpec((4, tm, LANES), lambda i: (0, jnp.clip(i - npt, 0, nst - 1), 0)),
            pl.BlockSpec((tm, 512), smap),
            pl.BlockSpec((D_MODEL, D_MODEL), lambda i: (0, 0)),
            pl.BlockSpec((1, D_MODEL), lambda i: (0, 0)),
            pl.BlockSpec((3 * D_MODEL, LANES), lambda i: (0, 0)),
            pl.BlockSpec((1, LANES), lambda i: (0, 0)),
        ],
        out_specs=[
            pl.BlockSpec((tm, D_MODEL), cur),
            pl.BlockSpec((tm, D_MODEL // 2), cur),
            pl.BlockSpec((tm, LANES), lambda i: (jnp.maximum(i - 1, 0), 0)),
            pl.BlockSpec((1, LANES), lambda i: (0, 0)),
        ],
        scratch_shapes=[pltpu.VMEM((2, tm, 3 * D_MODEL), BF16)],
        out_shape=[
            jax.ShapeDtypeStruct((t, D_MODEL), F32),
            jax.ShapeDtypeStruct((t, D_MODEL // 2), jnp.int32),
            jax.ShapeDtypeStruct((t, LANES), F32),
            jax.ShapeDtypeStruct((1, LANES), F32),
        ],
        compiler_params=_cparams(("arbitrary",)),
        name="out_router",
    )(xp, a4p, bp, xs, a4s, bs, wo, gamma, wr, br)


def _sc_gather_rows(table, idx):
    b = idx.shape[0]
    d = table.shape[1]
    per_worker = b // SC_WORKERS
    nwin = per_worker // SC_WINDOW
    assert per_worker * SC_WORKERS == b and nwin * SC_WINDOW == per_worker
    mesh = plsc.VectorSubcoreMesh(core_axis_name="c", subcore_axis_name="s")

    @functools.partial(
        pl.kernel, mesh=mesh,
        out_type=jax.ShapeDtypeStruct((b, d), table.dtype),
        scratch_types=[pltpu.VMEM((SC_WINDOW,), jnp.int32), pltpu.VMEM((SC_WINDOW, d), table.dtype),
                       pltpu.SemaphoreType.DMA],
        name="sc_gather_rows",
    )
    def gather(table_hbm, idx_hbm, out_hbm, idx_v, rows_v, sem):
        wid = lax.axis_index("s") * SC_CORES + lax.axis_index("c")
        base = wid * per_worker

        @pl.loop(0, nwin)
        def _(j):
            off = pl.multiple_of(base + j * SC_WINDOW, SC_WINDOW)
            pltpu.sync_copy(idx_hbm.at[pl.ds(off, SC_WINDOW)], idx_v)
            pltpu.async_copy(table_hbm.at[idx_v], rows_v, sem).wait()
            pltpu.sync_copy(rows_v, out_hbm.at[pl.ds(off, SC_WINDOW)])

    return gather(table, idx)


def _sc_scatter_rows(x, dest2, nrows):
    t, d = x.shape
    per_worker = t // SC_WORKERS
    nwin = per_worker // SC_SCATTER_WINDOW
    assert per_worker * SC_WORKERS == t and nwin * SC_SCATTER_WINDOW == per_worker
    mesh = plsc.VectorSubcoreMesh(core_axis_name="c", subcore_axis_name="s")

    @functools.partial(
        pl.kernel, mesh=mesh,
        out_type=jax.ShapeDtypeStruct((nrows, d), x.dtype),
        scratch_types=[pltpu.VMEM((TOP_K, SC_SCATTER_WINDOW), jnp.int32),
                       pltpu.VMEM((SC_SCATTER_WINDOW, d), x.dtype), pltpu.SemaphoreType.DMA],
        name="sc_scatter_rows",
    )
    def scatter(x_hbm, dest_hbm, out_hbm, idx_v, rows_v, sem):
        wid = lax.axis_index("s") * SC_CORES + lax.axis_index("c")
        base = wid * per_worker

        @pl.loop(0, nwin)
        def _(j):
            off = pl.multiple_of(base + j * SC_SCATTER_WINDOW, SC_SCATTER_WINDOW)
            pltpu.sync_copy(x_hbm.at[pl.ds(off, SC_SCATTER_WINDOW)], rows_v)
            for k in range(TOP_K):
                pltpu.sync_copy(dest_hbm.at[k, pl.ds(off, SC_SCATTER_WINDOW)], idx_v.at[k])
            for k in range(TOP_K):
                pltpu.async_copy(rows_v, out_hbm.at[idx_v.at[k]], sem).wait()

    return scatter(x, dest2)


def _expert_kernel(be_ref, nu_ref, nv_ref, x_ref, wg_ref, wu_ref, wd_ref, o_ref, wg_s, wu_s, wd_s):
    i = pl.program_id(0)
    used = i < nu_ref[0]
    changed = jnp.logical_or(i == 0, be_ref[i] != be_ref[jnp.maximum(i - 1, 0)])

    @pl.when(jnp.logical_and(used, changed))
    def _():
        wg_s[...] = wg_ref[...].astype(BF16)
        wu_s[...] = wu_ref[...].astype(BF16)
        wd_s[...] = wd_ref[...].astype(BF16)

    @pl.when(used)
    def _():
        row = lax.broadcasted_iota(jnp.int32, x_ref.shape, 0)
        x = _unpack_bf16_pairs(jnp.where(row < nv_ref[i], x_ref[...], 0)).astype(BF16)
        gate = jnp.dot(x, wg_s[...], preferred_element_type=F32)
        up = jnp.dot(x, wu_s[...], preferred_element_type=F32)
        h = (gate * jax.nn.sigmoid(gate) * up).astype(BF16)
        o_ref[...] = _pack_bf16_pairs(jnp.dot(h, wd_s[...], preferred_element_type=F32))

    @pl.when(jnp.logical_not(used))
    def _():
        o_ref[...] = jnp.zeros_like(o_ref)


def _experts(blk_e, n_used, nvalid, xb, w_gate, w_up, w_down):
    rows = xb.shape[0]
    nblocks = rows // MOE_ROWS
    grid_spec = pltpu.PrefetchScalarGridSpec(
        num_scalar_prefetch=3,
        grid=(nblocks,),
        in_specs=[
            pl.BlockSpec((MOE_ROWS, D_MODEL // 2), lambda i, be, nu, nv: (i, 0)),
            pl.BlockSpec((None, D_MODEL, D_EXPERT), lambda i, be, nu, nv: (be[i], 0, 0)),
            pl.BlockSpec((None, D_MODEL, D_EXPERT), lambda i, be, nu, nv: (be[i], 0, 0)),
            pl.BlockSpec((None, D_EXPERT, D_MODEL), lambda i, be, nu, nv: (be[i], 0, 0)),
        ],
        out_specs=pl.BlockSpec((MOE_ROWS, D_MODEL // 2), lambda i, be, nu, nv: (i, 0)),
        scratch_shapes=[pltpu.VMEM((D_MODEL, D_EXPERT), BF16), pltpu.VMEM((D_MODEL, D_EXPERT), BF16),
                        pltpu.VMEM((D_EXPERT, D_MODEL), BF16)],
    )
    return pl.pallas_call(
        _expert_kernel,
        grid_spec=grid_spec,
        out_shape=jax.ShapeDtypeStruct((rows, D_MODEL // 2), jnp.int32),
        compiler_params=_cparams(("arbitrary",)),
        name="experts",
    )(blk_e, n_used, nvalid, xb, w_gate, w_up, w_down)


def _combine_kernel(x1_ref, y1_ref, y2_ref, route_ref, g_ref, outp_ref, outs_ref, *, prompt_tiles):
    r = route_ref[...]
    x = (x1_ref[...] + r[:, 2:3] * _unpack_bf16_pairs(y1_ref[...])
         + r[:, 3:4] * _unpack_bf16_pairs(y2_ref[...]))
    ms = jnp.mean(x * x, axis=-1, keepdims=True)
    y = x * lax.rsqrt(ms + EPS) * g_ref[...]
    i = pl.program_id(0)

    @pl.when(i < prompt_tiles)
    def _():
        outp_ref[...] = y

    @pl.when(i >= prompt_tiles)
    def _():
        outs_ref[...] = y


def _combine_norm(x1, ygath, route, gamma, tp):
    t = x1.shape[0]
    tm = 512
    nt, npt = t // tm, tp // tm
    return pl.pallas_call(
        functools.partial(_combine_kernel, prompt_tiles=npt),
        grid=(nt,),
        in_specs=[
            pl.BlockSpec((tm, D_MODEL), lambda i: (i, 0)),
            pl.BlockSpec((tm, D_MODEL // 2), lambda i: (i, 0)),
            pl.BlockSpec((tm, D_MODEL // 2), lambda i: (i + nt, 0)),
            pl.BlockSpec((tm, LANES), lambda i: (i, 0)),
            pl.BlockSpec((1, D_MODEL), lambda i: (0, 0)),
        ],
        out_specs=[
            pl.BlockSpec((tm, D_MODEL), lambda i: (jnp.minimum(i, npt - 1), 0)),
            pl.BlockSpec((tm, D_MODEL), lambda i: (jnp.maximum(i - npt, 0), 0)),
        ],
        out_shape=[jax.ShapeDtypeStruct((tp, D_MODEL), F32), jax.ShapeDtypeStruct((t - tp, D_MODEL), F32)],
        compiler_params=_cparams(("arbitrary",)),
        name="combine_norm",
    )(x1, ygath, ygath, route, gamma)


BAND_PERIOD = 2 * QB + 1


def _band_tiles(h):
    nk = 2 * QB
    lead, heads = h.shape[:-2], h.shape[-2]
    flat = jnp.tile(h, (1,) * (h.ndim - 1) + (nk,))[..., :nk * nk]
    a = flat.reshape(lead + (heads // 2, 2, nk, nk))[..., :QB]
    a = jnp.swapaxes(a, -3, -2).reshape(lead + (heads // 2, nk, nk))
    prev = (np.arange(nk) < QB)[:, None]
    return jnp.stack([a, jnp.where(prev, NEG, a)], axis=len(lead))


def _band_index():
    c = (BAND_PERIOD - np.arange(BAND_PERIOD)) % BAND_PERIOD
    return c, c <= QB


def _bias_a_prompt(table_a):
    c, valid = _band_index()
    idx = np.stack([_t5_bucket_np(d * np.clip(QB - c, 0, QB)) for d in DILATIONS])
    h = jnp.where(valid, jnp.transpose(table_a[idx], (0, 2, 1)) * LOG2E, NEG)
    return _band_tiles(h)


def _bias_b_prompt(table_b):
    c, valid = _band_index()
    valid = valid & (c >= 1)
    h = jnp.where(valid, table_b[_t5_bucket_np(np.clip(QB - c, 0, QB))].T * LOG2E, NEG)
    h = jnp.transpose(h.reshape(KV_B, G_B, BAND_PERIOD), (1, 0, 2)).reshape(H_B, BAND_PERIOD)
    return _band_tiles(h)


def _sample_bias(table, span, t, log2_weight):
    cols = span + LANES
    period = cols + LANES
    x = np.arange(period)
    dist = np.where(x >= period - t, span - x + period, span - x)
    extra = log2_weight(dist)
    valid = np.isfinite(extra)
    u = jnp.where(valid, table[_t5_bucket_np(np.maximum(dist, 0))].T * LOG2E
                  + np.where(valid, extra, 0.0).astype(np.float32), NEG)
    rows = jnp.tile(u, (1, t))[:, :t * (period - 1)].reshape(u.shape[0], t, period - 1)[:, :, :cols]
    return rows.reshape(u.shape[0] * t, cols)


def _bias_a_sample(table_a, t):
    def log2_count(dist):
        count = np.zeros(dist.shape, np.int64)
        for w, d in zip(WINDOWS, DILATIONS):
            count += (dist >= 0) & (dist % d == 0) & (dist <= w)
        return np.where(count > 0, np.log2(np.maximum(count, 1)), -np.inf)

    return _sample_bias(table_a, WIN_A, t, log2_count)


def _bias_b_sample(table_b, t):
    return _sample_bias(table_b, WIN_B, t,
                        lambda dist: np.where((dist >= 0) & (dist < WIN_B), 0.0, -np.inf))


def _dest_kernel(route_ref, cnt_ref, tri_ref, dest_ref, meta_ref, run_scr, pst_scr):
    i = pl.program_id(0)
    tm = route_ref.shape[0]
    r = route_ref[...]
    lane = lax.broadcasted_iota(jnp.int32, (tm, LANES), 1)
    lanef = lane.astype(F32)
    oh0 = lanef == r[:, 0:1]
    oh1 = lanef == r[:, 1:2]
    ohf = jnp.concatenate([oh0, oh1], axis=0).astype(F32)

    @pl.when(i == 0)
    def _():
        cnt = jnp.broadcast_to(cnt_ref[...], (LANES, LANES))
        padded = jnp.floor((cnt + (MOE_ROWS - 1)) * (1.0 / MOE_ROWS)) * MOE_ROWS
        lane_e = lax.broadcasted_iota(jnp.int32, (LANES, LANES), 1)
        x = padded
        for sh in (1, 2, 4, 8, 16, 32, 64):
            x = x + jnp.where(lane_e >= sh, pltpu.roll(x, sh, 1), 0.0)
        pst_scr[...] = (x - padded)[0:1]
        run_scr[...] = jnp.zeros_like(run_scr)
        wide = lambda v: jnp.concatenate([v.T, v.T], axis=1)
        cnt_t, bend_t = wide(cnt), wide(x * (1.0 / MOE_ROWS))
        bstart_t = wide((x - padded) * (1.0 / MOE_ROWS))
        blk = lax.broadcasted_iota(jnp.int32, (LANES, 2 * LANES), 1).astype(F32)
        exp = lax.broadcasted_iota(jnp.int32, (LANES, 2 * LANES), 0)
        real = exp < N_EXPERTS
        blk_e = jnp.minimum(jnp.sum(jnp.where(real & (bend_t <= blk), 1.0, 0.0), axis=0, keepdims=True),
                            N_EXPERTS - 1.0)
        mine = exp.astype(F32) == blk_e
        within = blk[0:1] - jnp.sum(jnp.where(mine, bstart_t, 0.0), axis=0, keepdims=True)
        nvalid = jnp.clip(jnp.sum(jnp.where(mine, cnt_t, 0.0), axis=0, keepdims=True) - within * MOE_ROWS,
                          0.0, float(MOE_ROWS))
        n_used = jnp.max(jnp.where(real, bend_t, 0.0), axis=0, keepdims=True)
        meta_ref[...] = jnp.concatenate([blk_e, nvalid, n_used, jnp.zeros((5, 2 * LANES), F32)],
                                        axis=0).astype(jnp.int32)

    csum = jnp.dot(tri_ref[...], ohf.astype(BF16), preferred_element_type=F32)
    val = csum + (run_scr[...] + pst_scr[...] - 1.0)
    d0 = jnp.sum(jnp.where(oh0, val[:tm], 0.0), axis=-1, keepdims=True)
    d1 = jnp.sum(jnp.where(oh1, val[tm:], 0.0), axis=-1, keepdims=True)
    tile = jnp.where(lane == 0, d0, jnp.where(lane == 1, d1, 0.0))
    dest_ref[...] = tile.T[:8].astype(jnp.int32)
    run_scr[...] += jnp.sum(ohf, axis=0, keepdims=True)


def _dispatch(route, cnt):
    t = route.shape[0]
    tm = 512
    tri = (jnp.arange(2 * tm)[:, None] >= jnp.arange(2 * tm)[None, :]).astype(BF16)
    nblocks = -(-t * TOP_K // MOE_ROWS) + N_EXPERTS
    assert nblocks <= 2 * LANES
    dest, meta = pl.pallas_call(
        _dest_kernel,
        grid=(t // tm,),
        in_specs=[pl.BlockSpec((tm, LANES), lambda i: (i, 0)),
                  pl.BlockSpec((1, LANES), lambda i: (0, 0)),
                  pl.BlockSpec((2 * tm, 2 * tm), lambda i: (0, 0))],
        out_specs=[pl.BlockSpec((8, tm), lambda i: (0, i)),
                   pl.BlockSpec((8, 2 * LANES), lambda i: (0, 0))],
        out_shape=[jax.ShapeDtypeStruct((8, t), jnp.int32), jax.ShapeDtypeStruct((8, 2 * LANES), jnp.int32)],
        scratch_shapes=[pltpu.VMEM((1, LANES), F32), pltpu.VMEM((1, LANES), F32)],
        compiler_params=_cparams(("arbitrary",)),
        name="moe_dest",
    )(route, cnt, tri)
    return dest[:TOP_K], meta[0, :nblocks], meta[2, :1], meta[1, :nblocks]


def kernel(x_prompt, x_sample, cache_a_k, cache_a_v, cache_b_k, cache_b_v, rel_bias_table, attn_norm, w_in,
           w_out, attn_sinks, ffn_norm, w_router_group, b_router_group, w_router_expert, b_router_expert,
           w_gate, w_up, w_down, final_norm):
    s = x_prompt.shape[1]
    ns, ts = x_sample.shape[0], x_sample.shape[1]
    table_a = rel_bias_table[:, :H_A]
    table_b = rel_bias_table[:, H_A:]

    w = w_in[0]
    wqa, wka, wva, wqb, wkb, wvb = (w[:, 0:512], w[:, 512:1024], w[:, 1024:1536], w[:, 1536:2048],
                                    w[:, 2048:2176], w[:, 2176:2304])
    wqb = jnp.transpose(wqb.reshape(D_MODEL, KV_B, G_B, HEAD_DIM), (0, 2, 1, 3)).reshape(D_MODEL, 512)
    wp = jnp.concatenate([wka, wva, wqa, wqb, wkb, wvb], axis=1).astype(BF16)
    cscale = jnp.concatenate([jnp.ones((1, 1024), F32), jnp.full((1, 1024), SCALE * LOG2E, F32),
                              jnp.ones((1, 256), F32)], axis=1)
    wo = w_out[0]
    wo_b = jnp.transpose(wo[512:].reshape(KV_B, G_B, HEAD_DIM, D_MODEL), (1, 0, 2, 3)).reshape(512, D_MODEL)
    wo_p = jnp.concatenate([wo[:512], wo_b], axis=0).astype(BF16)
    wr = jnp.concatenate([w_router_group[0],
                          jnp.transpose(w_router_expert[0], (1, 0, 2)).reshape(D_MODEL, N_EXPERTS),
                          jnp.zeros((D_MODEL, LANES - N_GROUPS - N_EXPERTS), F32)], axis=1)
    wr_hi = wr.astype(BF16)
    wr = jnp.concatenate([wr_hi, wr_hi, (wr - wr_hi.astype(F32)).astype(BF16)], axis=0)
    br = jnp.concatenate([b_router_group[0], b_router_expert[0].reshape(N_EXPERTS),
                          jnp.zeros((LANES - N_GROUPS - N_EXPERTS,), F32)]).reshape(1, LANES)
    sinks2 = attn_sinks[0] * LOG2E
    sinks_gk = jnp.transpose(sinks2.reshape(KV_B, G_B), (1, 0)).reshape(H_B)
    sink_rows_p = jnp.repeat(sinks_gk, QB).reshape(G_B, 1, 2 * QB)
    sink_rows_s = jnp.repeat(sinks2, ts).reshape(H_B * ts, 1)
    emat = jnp.tile(jnp.arange(LANES)[:, None] == (jnp.arange(A_WIDTH)[None, :] // HEAD_DIM),
                    (3, 1)).astype(BF16)
    attn_g = attn_norm[0].reshape(1, D_MODEL)
    ffn_g = ffn_norm[0].reshape(1, D_MODEL)

    xp = x_prompt.reshape(s, D_MODEL)
    aperm, qb_p, kvb_p, akv32, bkv32 = _proj_prompt(xp, attn_g, wp, cscale)
    a4 = _attn_a_prompt(aperm, _bias_a_prompt(table_a), emat)
    ob_p = _attn_b_prompt(qb_p, kvb_p, _bias_b_prompt(table_b), sink_rows_p)

    xs = x_sample.reshape(ns * ts, D_MODEL)
    q_s, kv_s = _proj_sample(xs, attn_g, wp, cscale)
    akt = jnp.transpose(cache_a_k[0], (0, 2, 3, 1)).reshape(ns, A_WIDTH, WIN_A)
    avt = jnp.transpose(cache_a_v[0], (0, 2, 3, 1)).reshape(ns, A_WIDTH, WIN_A)
    bkt = jnp.transpose(cache_b_k[0], (0, 2, 3, 1)).reshape(ns, LANES, WIN_B)
    bvt = jnp.transpose(cache_b_v[0], (0, 2, 3, 1)).reshape(ns, LANES, WIN_B)
    oa_s, ob_s = _attn_sample(q_s.reshape(ns, ts, 1024), kv_s.reshape(ns, ts, 1280), akt, avt, bkt, bvt,
                              _bias_a_sample(table_a, ts), _bias_b_sample(table_b, ts), sink_rows_s)
    a4_s = jnp.transpose(oa_s.reshape(ns * ts, 4, LANES), (1, 0, 2))

    x1, xn, route, cnt = _out_router(xp, a4, ob_p, xs, a4_s, ob_s.reshape(ns * ts, 512), wo_p, ffn_g, wr, br)
    dest2, blk_e, n_used, nvalid = _dispatch(route, cnt)
    xb = _sc_scatter_rows(xn, dest2, blk_e.shape[0] * MOE_ROWS)
    yb = _experts(blk_e, n_used, nvalid, xb, w_gate[0], w_up[0], w_down[0])
    y_p, y_s = _combine_norm(x1, _sc_gather_rows(yb, dest2.reshape(-1)), route, final_norm.reshape(1, D_MODEL), s)

    y_prompt = y_p.reshape(1, s, D_MODEL)
    y_sample = y_s.reshape(ns, ts, D_MODEL)
    keep_a, keep_b = min(WIN_A, s), min(WIN_B, s)
    pak = akv32[s - keep_a:, :512].reshape(1, 1, keep_a, H_A, HEAD_DIM)
    pav = akv32[s - keep_a:, 512:].reshape(1, 1, keep_a, H_A, HEAD_DIM)
    pbk = bkv32[s - keep_b:, :128].reshape(1, 1, keep_b, KV_B, HEAD_DIM)
    pbv = bkv32[s - keep_b:, 128:].reshape(1, 1, keep_b, KV_B, HEAD_DIM)
    sak = kv_s[:, 0:512].reshape(1, ns, ts, H_A, HEAD_DIM)
    sav = kv_s[:, 512:1024].reshape(1, ns, ts, H_A, HEAD_DIM)
    sbk = kv_s[:, 1024:1152].reshape(1, ns, ts, KV_B, HEAD_DIM)
    sbv = kv_s[:, 1152:1280].reshape(1, ns, ts, KV_B, HEAD_DIM)
    return (y_prompt, y_sample, pak, pav, pbk, pbv, sak, sav, sbk, sbv)
```

```python
import functools
import math

import jax
import jax.numpy as jnp
import numpy as np
from jax import lax
from jax.experimental import pallas as pl
from jax.experimental.pallas import tpu as pltpu
from jax.experimental.pallas import tpu_sc as plsc

D_MODEL = 1024
HEAD_DIM = 64
H_A = 8
H_B = 8
KV_B = 2
G_B = 4
DILATIONS = (1, 4, 16)
WINDOWS = (128, 512, 2048)
WIN_A = 2048
WIN_B = 128
NUM_BUCKETS = 32
MAX_DISTANCE = 2048
N_GROUPS = 4
EXPERTS_PER_GROUP = 8
N_EXPERTS = 32
TOP_K = 2
D_EXPERT = 512
EPS = 1e-5
SCALE = HEAD_DIM ** -0.5
PAST_LEN = 16384

LANES = 128
SPAN = 2048
QB = 128
NCHUNK = 9
A_WIDTH = H_A * HEAD_DIM
MOE_ROWS = 512
SC_CORES = 2
SC_SUBCORES = 16
SC_WORKERS = SC_CORES * SC_SUBCORES
SC_WINDOW = 64
SC_SCATTER_WINDOW = 32
NEG = -1e30
LOG2E = math.log2(math.e)
B_STEP = 512
VMEM_LIMIT = 56 * 1024 * 1024

F32 = jnp.float32
BF16 = jnp.bfloat16


def _t5_bucket_np(dist):
    dist = np.asarray(dist, np.int64)
    max_exact = NUM_BUCKETS // 2
    d = np.maximum(dist, 1).astype(np.float32)
    ratio = np.log(d / np.float32(max_exact)) / np.float32(math.log(MAX_DISTANCE / max_exact))
    large = max_exact + (ratio * np.float32(NUM_BUCKETS - max_exact)).astype(np.int32)
    large = np.minimum(large, NUM_BUCKETS - 1)
    return np.where(dist < max_exact, dist, large).astype(np.int32)


def _cparams(sem, vmem=VMEM_LIMIT):
    return pltpu.CompilerParams(dimension_semantics=sem, vmem_limit_bytes=vmem)


def _proj_prompt_kernel(x_ref, g_ref, w_ref, cs_ref, aperm_ref, qb_ref, kvb_ref, akv_ref, bkv_ref,
                        h_scr, p_scr):
    n = pl.program_id(1)

    @pl.when(n == 0)
    def _():
        x = x_ref[...]
        ms = jnp.mean(x * x, axis=-1, keepdims=True)
        h_scr[...] = (x * lax.rsqrt(ms + EPS) * g_ref[...]).astype(BF16)

    p = jnp.dot(h_scr[...], w_ref[...], preferred_element_type=F32) * cs_ref[...]

    @pl.when(n < 6)
    def _():
        aperm_ref[0] = p.astype(BF16)
        p_scr[0, 0] = p[:, :LANES]
        p_scr[0, 1] = p[:, LANES:]
        quarter = SPAN // 4
        for r in range(4):
            lo = p_scr[0, 0, pl.ds(r, quarter, stride=4), :]
            hi = p_scr[0, 1, pl.ds(r, quarter, stride=4), :]
            p_scr[1, 0, r * quarter:(r + 1) * quarter, :] = lo
            p_scr[1, 1, r * quarter:(r + 1) * quarter, :] = hi
            aperm_ref[1, r * quarter:(r + 1) * quarter, :] = jnp.concatenate([lo, hi], axis=1).astype(BF16)
        for r16 in range(16):
            start = (r16 % 4) * quarter + r16 // 4
            t = jnp.concatenate([p_scr[1, 0, pl.ds(start, QB, stride=4), :],
                                 p_scr[1, 1, pl.ds(start, QB, stride=4), :]], axis=1)
            aperm_ref[2, r16 * QB:(r16 + 1) * QB, :] = t.astype(BF16)

    @pl.when(n < 4)
    def _():
        akv_ref[...] = p

    @pl.when(jnp.logical_or(n == 6, n == 7))
    def _():
        qb_ref[...] = p.astype(BF16)

    @pl.when(n == 8)
    def _():
        kvb_ref[...] = p.astype(BF16)
        bkv_ref[...] = p


def _proj_prompt(x, gamma, w, cscale):
    s = x.shape[0]
    nspan = s // SPAN
    return pl.pallas_call(
        _proj_prompt_kernel,
        grid=(nspan, NCHUNK),
        in_specs=[
            pl.BlockSpec((SPAN, D_MODEL), lambda b, n: (b, 0)),
            pl.BlockSpec((1, D_MODEL), lambda b, n: (0, 0)),
            pl.BlockSpec((D_MODEL, 256), lambda b, n: (0, n)),
            pl.BlockSpec((1, 256), lambda b, n: (0, n)),
        ],
        out_specs=[
            pl.BlockSpec((3, SPAN, 256), lambda b, n: (0, b, jnp.minimum(n, 5))),
            pl.BlockSpec((SPAN, 256), lambda b, n: (b, jnp.clip(n - 6, 0, 1))),
            pl.BlockSpec((SPAN, 256), lambda b, n: (b, 0)),
            pl.BlockSpec((SPAN, 256), lambda b, n: (b, jnp.minimum(n, 3))),
            pl.BlockSpec((SPAN, 256), lambda b, n: (b, 0)),
        ],
        out_shape=[
            jax.ShapeDtypeStruct((3, s, 3 * A_WIDTH), BF16),
            jax.ShapeDtypeStruct((s, 512), BF16),
            jax.ShapeDtypeStruct((s, 256), BF16),
            jax.ShapeDtypeStruct((s, 1024), F32),
            jax.ShapeDtypeStruct((s, 256), F32),
        ],
        scratch_shapes=[pltpu.VMEM((SPAN, D_MODEL), BF16), pltpu.VMEM((2, 2, SPAN, LANES), F32)],
        compiler_params=_cparams(("arbitrary", "arbitrary")),
        name="proj_prompt",
    )(x, gamma, w, cscale)


def _proj_sample_kernel(x_ref, g_ref, w_ref, cs_ref, q_ref, kv_ref):
    x = x_ref[...]
    ms = jnp.mean(x * x, axis=-1, keepdims=True)
    h = (x * lax.rsqrt(ms + EPS) * g_ref[...]).astype(BF16)
    p = jnp.dot(h, w_ref[...], preferred_element_type=F32) * cs_ref[...]
    kv_ref[:, :1024] = p[:, :1024]
    kv_ref[:, 1024:] = p[:, 2048:]
    q_ref[...] = p[:, 1024:2048]


def _proj_sample(x, gamma, w, cscale):
    t = x.shape[0]
    tm = 512
    return pl.pallas_call(
        _proj_sample_kernel,
        grid=(t // tm,),
        in_specs=[
            pl.BlockSpec((tm, D_MODEL), lambda i: (i, 0)),
            pl.BlockSpec((1, D_MODEL), lambda i: (0, 0)),
            pl.BlockSpec((D_MODEL, 2304), lambda i: (0, 0)),
            pl.BlockSpec((1, 2304), lambda i: (0, 0)),
        ],
        out_specs=[
            pl.BlockSpec((tm, 1024), lambda i: (i, 0)),
            pl.BlockSpec((tm, 1280), lambda i: (i, 0)),
        ],
        out_shape=[
            jax.ShapeDtypeStruct((t, 1024), F32),
            jax.ShapeDtypeStruct((t, 1280), F32),
        ],
        compiler_params=_cparams(("arbitrary",)),
        name="proj_sample",
    )(x, gamma, w, cscale)


def _spread_heads(w, e3_ref):
    hi = w.astype(BF16)
    r1 = w - hi.astype(F32)
    mid = r1.astype(BF16)
    low = (r1 - mid.astype(F32)).astype(BF16)
    return jnp.dot(jnp.concatenate([hi, mid, low], axis=1), e3_ref[...], preferred_element_type=F32)


def _pair_tile(q2, kk, vv, bias_t, lo, sink=None):
    zero = jnp.zeros_like(q2)
    qq = jnp.concatenate([jnp.where(lo, q2, zero), jnp.where(lo, zero, q2)], axis=0)
    st = lax.dot_general(kk, qq, (((1,), (1,)), ((), ())), preferred_element_type=F32)
    st = st + bias_t
    m = jnp.max(st, axis=0, keepdims=True)
    if sink is not None:
        m = jnp.maximum(m, sink)
    p = jnp.exp2(st - m)
    den = jnp.sum(p, axis=0, keepdims=True)
    if sink is not None:
        den = den + jnp.exp2(sink - m)
    pn = (p * (1.0 / den)).astype(BF16)
    o = lax.dot_general(pn, vv, (((0,), (0,)), ((), ())), preferred_element_type=F32)
    return jnp.where(lo, o[:QB], o[QB:]), m + jnp.log2(den)


def _fill_band_tiles(h_ref, bias_scr):
    nk = 2 * QB
    prev = lax.broadcasted_iota(jnp.int32, (nk, nk), 0) < QB
    for pair in range(h_ref.shape[0] // 2):
        halves = []
        for hh in range(2):
            row = h_ref[2 * pair + hh:2 * pair + hh + 1, :]
            band = pltpu.roll(jnp.broadcast_to(row, (nk, nk)), 0, 1, stride=1, stride_axis=0)
            halves.append(band[:, :QB])
        tile = jnp.concatenate(halves, axis=1)
        bias_scr[0, pair] = tile
        bias_scr[1, pair] = jnp.where(prev, NEG, tile)


def _attn_a_kernel(q_ref, kvc_ref, kvp_ref, h_ref, e_ref, out_ref, o_scr, st_scr, bias_scr):
    b = pl.program_id(0)
    g = pl.program_id(1)
    nblk = jnp.where(g == 0, 16, jnp.where(g == 1, 4, 1))
    lane = lax.broadcasted_iota(jnp.int32, (QB, LANES), 1)
    lo = lane < HEAD_DIM

    @pl.when(b == 0)
    def _():
        _fill_band_tiles(h_ref, bias_scr.at[g])

    bias_ref = bias_scr.at[g]

    for cb in range(SPAN // QB):
        first = lax.rem(jnp.int32(cb), nblk) == 0
        rows = slice(cb * QB, (cb + 1) * QB)
        prow_c = max(cb - 1, 0) * QB
        prow_p = pl.multiple_of(jnp.where(first, cb + nblk - 1, 0) * QB, QB)
        variant = jnp.logical_and(first, b == 0).astype(jnp.int32)
        stats = []
        for hp in range(4):
            ks = slice(hp * LANES, (hp + 1) * LANES)
            vs = slice(A_WIDTH + hp * LANES, A_WIDTH + (hp + 1) * LANES)
            kp = jnp.where(first, kvp_ref[pl.ds(prow_p, QB), ks], kvc_ref[prow_c:prow_c + QB, ks])
            vp = jnp.where(first, kvp_ref[pl.ds(prow_p, QB), vs], kvc_ref[prow_c:prow_c + QB, vs])
            kk = jnp.concatenate([kp, kvc_ref[rows, ks]], axis=0)
            vv = jnp.concatenate([vp, kvc_ref[rows, vs]], axis=0)
            o, lse = _pair_tile(q_ref[rows, ks], kk, vv, bias_ref[variant, hp], lo)
            o_scr[g, hp, rows, :] = o
            stats += [lse[:, :QB], lse[:, QB:]]
        sm = jnp.concatenate(stats + [jnp.zeros((LANES - H_A, QB), F32)], axis=0)
        st_scr[g, rows, :] = sm.T

    @pl.when(g == 2)
    def _():
        def merge(c, carry):
            r2 = lax.rem(c, 4) * (SPAN // 4) + c // 4
            r3 = pl.multiple_of(c * QB, QB)
            l1 = st_scr[0, pl.ds(c, QB, stride=16), :]
            l2 = st_scr[1, pl.ds(r2, QB, stride=4), :]
            l3 = st_scr[2, pl.ds(r3, QB), :]
            mx = jnp.maximum(jnp.maximum(l1, l2), l3)
            w1 = jnp.exp2(l1 - mx)
            w2 = jnp.exp2(l2 - mx)
            w3 = jnp.exp2(l3 - mx)
            tot = w1 + w2 + w3
            a1 = _spread_heads(w1 / tot, e_ref)
            a2 = _spread_heads(w2 / tot, e_ref)
            a3 = _spread_heads(w3 / tot, e_ref)
            for hp in range(4):
                sl = slice(hp * LANES, (hp + 1) * LANES)
                o1 = o_scr[0, hp, pl.ds(c, QB, stride=16), :]
                o2 = o_scr[1, hp, pl.ds(r2, QB, stride=4), :]
                o3 = o_scr[2, hp, pl.ds(r3, QB), :]
                out_ref[hp, pl.ds(c, QB, stride=16), :] = a1[:, sl] * o1 + a2[:, sl] * o2 + a3[:, sl] * o3
            return carry

        lax.fori_loop(0, 16, merge, 0, unroll=4)


def _attn_a_prompt(aperm, bias_a, emat):
    s = aperm.shape[1]
    nspan = s // SPAN
    return pl.pallas_call(
        _attn_a_kernel,
        grid=(nspan, 3),
        in_specs=[
            pl.BlockSpec((None, SPAN, A_WIDTH), lambda b, g: (g, b, 2)),
            pl.BlockSpec((None, SPAN, 2 * A_WIDTH), lambda b, g: (g, b, 0)),
            pl.BlockSpec((None, SPAN, 2 * A_WIDTH), lambda b, g: (g, jnp.maximum(b - 1, 0), 0)),
            pl.BlockSpec((None, H_A, 2 * QB), lambda b, g: (g, 0, 0)),
            pl.BlockSpec((3 * LANES, A_WIDTH), lambda b, g: (0, 0)),
        ],
        out_specs=pl.BlockSpec((4, SPAN, LANES), lambda b, g: (0, b, 0)),
        out_shape=jax.ShapeDtypeStruct((4, s, LANES), F32),
        scratch_shapes=[pltpu.VMEM((3, 4, SPAN, LANES), F32), pltpu.VMEM((3, SPAN, LANES), F32),
                        pltpu.VMEM((3, 2, 4, 2 * QB, 2 * QB), F32)],
        compiler_params=_cparams(("arbitrary", "arbitrary")),
        name="attn_a_prompt",
    )(aperm, aperm, aperm, bias_a, emat)


def _attn_b_kernel(q_ref, kvc_ref, kvp_ref, h_ref, sink_ref, out_ref, bias_ref):
    i = pl.program_id(0)
    lane = lax.broadcasted_iota(jnp.int32, (QB, LANES), 1)
    lo = lane < HEAD_DIM

    @pl.when(i == 0)
    def _():
        _fill_band_tiles(h_ref, bias_ref)

    variant = (i == 0).astype(jnp.int32)
    for j in range(B_STEP // QB):
        rows = slice(j * QB, (j + 1) * QB)
        if j == 0:
            kp, vp = kvp_ref[:, :LANES], kvp_ref[:, LANES:]
        else:
            kp, vp = kvc_ref[(j - 1) * QB:j * QB, :LANES], kvc_ref[(j - 1) * QB:j * QB, LANES:]
        kk = jnp.concatenate([kp, kvc_ref[rows, :LANES]], axis=0)
        vv = jnp.concatenate([vp, kvc_ref[rows, LANES:]], axis=0)
        for g in range(G_B):
            bias_t = bias_ref[variant, g] if j == 0 else bias_ref[0, g]
            o, _ = _pair_tile(q_ref[rows, g * LANES:(g + 1) * LANES], kk, vv, bias_t, lo, sink=sink_ref[g])
            out_ref[rows, g * LANES:(g + 1) * LANES] = o.astype(BF16)


def _attn_b_prompt(qb, kvb, bias_b, sink_rows):
    s = qb.shape[0]
    per = B_STEP // QB
    return pl.pallas_call(
        _attn_b_kernel,
        grid=(s // B_STEP,),
        in_specs=[
            pl.BlockSpec((B_STEP, 512), lambda i: (i, 0)),
            pl.BlockSpec((B_STEP, 256), lambda i: (i, 0)),
            pl.BlockSpec((QB, 256), lambda i: (jnp.maximum(i * per - 1, 0), 0)),
            pl.BlockSpec((H_B, 2 * QB), lambda i: (0, 0)),
            pl.BlockSpec((G_B, 1, 2 * QB), lambda i: (0, 0, 0)),
        ],
        out_specs=pl.BlockSpec((B_STEP, 512), lambda i: (i, 0)),
        out_shape=jax.ShapeDtypeStruct((s, 512), BF16),
        scratch_shapes=[pltpu.VMEM((2, G_B, 2 * QB, 2 * QB), F32)],
        compiler_params=_cparams(("arbitrary",)),
        name="attn_b_prompt",
    )(qb, kvb, kvb, bias_b, sink_rows)


def _attn_sample_kernel(q_ref, kvn_ref, akt_ref, avt_ref, bkt_ref, bvt_ref, cba_ref, cbb_ref, sink_ref,
                        oa_ref, ob_ref):
    t = q_ref.shape[0]
    q = q_ref[...]
    kvn = kvn_ref[...]
    kvn_p = jnp.concatenate([kvn, jnp.zeros((LANES - t, kvn.shape[1]), F32)], axis=0).astype(BF16)
    lane_a = lax.broadcasted_iota(jnp.int32, (t, A_WIDTH), 1) // HEAD_DIM

    qa = q[:, :A_WIDTH]
    qbd = jnp.concatenate([jnp.where(lane_a == h, qa, 0.0) for h in range(H_A)], axis=0).astype(BF16)
    s_c = jnp.dot(qbd, akt_ref[...].astype(BF16), preferred_element_type=F32)
    s_n = lax.dot_general(qbd, kvn_p[:, :A_WIDTH], (((1,), (1,)), ((), ())), preferred_element_type=F32)
    s = jnp.concatenate([s_c, s_n], axis=1) + cba_ref[...]
    m = jnp.max(s, axis=-1, keepdims=True)
    p = jnp.exp2(s - m)
    l = jnp.sum(p, axis=-1, keepdims=True)
    o_n = jnp.dot(p[:, WIN_A:].astype(BF16), kvn_p[:, A_WIDTH:2 * A_WIDTH], preferred_element_type=F32)
    pc = jnp.concatenate([p[:, :WIN_A], jnp.zeros((LANES - H_A * t, WIN_A), F32)], axis=0).astype(BF16)
    o_t = lax.dot_general(avt_ref[...].astype(BF16), pc, (((1,), (1,)), ((), ())),
                          preferred_element_type=F32)
    o_all = o_t.T[:H_A * t] + o_n
    o_sel = jnp.zeros((t, A_WIDTH), F32)
    l_b = jnp.ones((t, A_WIDTH), F32)
    for h in range(H_A):
        sel = lane_a == h
        o_sel = jnp.where(sel, o_all[h * t:(h + 1) * t], o_sel)
        l_b = jnp.where(sel, l[h * t:(h + 1) * t], l_b)
    oa_ref[...] = o_sel / l_b

    lane_b = lax.broadcasted_iota(jnp.int32, (G_B * t, LANES), 1)
    lo = lane_b < HEAD_DIM
    qb2 = jnp.concatenate([q[:, A_WIDTH + g * LANES:A_WIDTH + (g + 1) * LANES] for g in range(G_B)], axis=0)
    qm = jnp.concatenate([jnp.where(lo, qb2, 0.0), jnp.where(lo, 0.0, qb2)], axis=0).astype(BF16)
    kb_n = kvn_p[:, 2 * A_WIDTH:2 * A_WIDTH + LANES]
    vb_n = kvn_p[:, 2 * A_WIDTH + LANES:]
    sb_c = jnp.dot(qm, bkt_ref[...].astype(BF16), preferred_element_type=F32)
    sb_n = lax.dot_general(qm, kb_n, (((1,), (1,)), ((), ())), preferred_element_type=F32)
    sb = jnp.concatenate([sb_c, sb_n], axis=1) + cbb_ref[...]
    sink = sink_ref[...]
    mb = jnp.maximum(jnp.max(sb, axis=-1, keepdims=True), sink)
    pbb = jnp.exp2(sb - mb)
    den = jnp.sum(pbb, axis=-1, keepdims=True) + jnp.exp2(sink - mb)
    pbb = pbb.astype(BF16)
    ob = lax.dot_general(pbb[:, :WIN_B], bvt_ref[...].astype(BF16), (((1,), (1,)), ((), ())),
                         preferred_element_type=F32)
    ob = (ob + jnp.dot(pbb[:, WIN_B:], vb_n, preferred_element_type=F32)) / den
    half = G_B * t
    lo8 = lo[:t]
    for g in range(G_B):
        ob_ref[:, g * LANES:(g + 1) * LANES] = jnp.where(
            lo8, ob[g * t:(g + 1) * t], ob[half + g * t:half + (g + 1) * t])


def _attn_sample(q3, kvn3, akt, avt, bkt, bvt, cbias_a, cbias_b, sink_rows):
    ns, t = q3.shape[0], q3.shape[1]
    return pl.pallas_call(
        _attn_sample_kernel,
        grid=(ns,),
        in_specs=[
            pl.BlockSpec((None, t, 1024), lambda n: (n, 0, 0)),
            pl.BlockSpec((None, t, 1280), lambda n: (n, 0, 0)),
            pl.BlockSpec((None, A_WIDTH, WIN_A), lambda n: (n, 0, 0)),
            pl.BlockSpec((None, A_WIDTH, WIN_A), lambda n: (n, 0, 0)),
            pl.BlockSpec((None, LANES, WIN_B), lambda n: (n, 0, 0)),
            pl.BlockSpec((None, LANES, WIN_B), lambda n: (n, 0, 0)),
            pl.BlockSpec((H_A * t, WIN_A + LANES), lambda n: (0, 0)),
            pl.BlockSpec((H_B * t, WIN_B + LANES), lambda n: (0, 0)),
            pl.BlockSpec((H_B * t, 1), lambda n: (0, 0)),
        ],
        out_specs=[
            pl.BlockSpec((None, t, 512), lambda n: (n, 0, 0)),
            pl.BlockSpec((None, t, 512), lambda n: (n, 0, 0)),
        ],
        out_shape=[jax.ShapeDtypeStruct((ns, t, 512), F32), jax.ShapeDtypeStruct((ns, t, 512), F32)],
        compiler_params=_cparams(("arbitrary",)),
        name="attn_sample",
    )(q3, kvn3, akt, avt, bkt, bvt, cbias_a, cbias_b, sink_rows)


def _route(logits):
    lane = lax.broadcasted_iota(jnp.int32, logits.shape, 1).astype(F32)
    big = jnp.float32(1 << 20)
    ninf = jnp.float32(-jnp.inf)
    gmask = lane < N_GROUPS
    lg = jnp.where(gmask, logits, ninf)
    gmax = jnp.max(lg, axis=-1, keepdims=True)
    grp = jnp.min(jnp.where(lg == gmax, lane, big), axis=-1, keepdims=True)
    pg_top = 1.0 / jnp.sum(jnp.exp(lg - gmax), axis=-1, keepdims=True)
    e0 = N_GROUPS + grp * EXPERTS_PER_GROUP
    emask = jnp.logical_and(lane >= e0, lane < e0 + EXPERTS_PER_GROUP)
    le = jnp.where(emask, logits, ninf)
    emax = jnp.max(le, axis=-1, keepdims=True)
    esum = jnp.sum(jnp.exp(le - emax), axis=-1, keepdims=True)
    i1 = jnp.min(jnp.where(le == emax, lane, big), axis=-1, keepdims=True)
    le2 = jnp.where(lane == i1, ninf, le)
    e2max = jnp.max(le2, axis=-1, keepdims=True)
    i2 = jnp.min(jnp.where(le2 == e2max, lane, big), axis=-1, keepdims=True)
    p1 = 1.0 / esum
    p2 = jnp.exp(e2max - emax) / esum
    g1 = pg_top * p1 / (p1 + p2)
    g2 = pg_top * p2 / (p1 + p2)
    out = jnp.where(lane == 0, i1 - N_GROUPS, 0.0)
    out = jnp.where(lane == 1, i2 - N_GROUPS, out)
    out = jnp.where(lane == 2, g1, out)
    out = jnp.where(lane == 3, g2, out)
    return out


def _pack_bf16_pairs(x):
    half = x.shape[1] // 2

    def rne(v):
        bits = lax.bitcast_convert_type(v, jnp.int32)
        return bits + 0x7FFF + (lax.shift_right_logical(bits, 16) & 1)

    lo = lax.shift_right_logical(rne(x[:, :half]), 16)
    hi = rne(x[:, half:]) & jnp.int32(-65536)
    return lo | hi


def _unpack_bf16_pairs(w):
    lo = lax.bitcast_convert_type(lax.shift_left(w, 16), F32)
    hi = lax.bitcast_convert_type(w & jnp.int32(-65536), F32)
    return jnp.concatenate([lo, hi], axis=1)


def _out_router_kernel(xp_ref, ap_ref, bp_ref, xs_ref, as_ref, bs_ref, wo_ref, g_ref, wr_ref, br_ref,
                       x1_ref, xn_ref, route_ref, cnt_ref, xcat_scr, *, prompt_tiles, tiles):
    i = pl.program_id(0)

    @pl.when(i == 0)
    def _():
        cnt_ref[...] = jnp.zeros_like(cnt_ref)
        xcat_scr[...] = jnp.zeros_like(xcat_scr)

    slot = lax.rem(i, 2)

    def body(x_ref, a_ref, b_ref):
        logits = jnp.dot(xcat_scr[1 - slot], wr_ref[...], preferred_element_type=F32)
        route = _route(logits + br_ref[...])
        route_ref[...] = route
        lanef = lax.broadcasted_iota(jnp.int32, route.shape, 1).astype(F32)
        hits = (lanef == route[:, 0:1]).astype(F32) + (lanef == route[:, 1:2]).astype(F32)
        cnt_ref[...] += jnp.sum(hits, axis=0, keepdims=True) * (i > 0).astype(F32)

        mix = jnp.concatenate([a_ref[0], a_ref[1], a_ref[2], a_ref[3]], axis=1).astype(BF16)
        mix = jnp.concatenate([mix, b_ref[...].astype(BF16)], axis=1)
        x1 = x_ref[...] + jnp.dot(mix, wo_ref[...], preferred_element_type=F32)
        x1_ref[...] = x1
        ms = jnp.mean(x1 * x1, axis=-1, keepdims=True)
        xn = x1 * lax.rsqrt(ms + EPS) * g_ref[...]
        xn_ref[...] = _pack_bf16_pairs(xn)
        xh = xn.astype(BF16)
        xl = (xn - xh.astype(F32)).astype(BF16)
        xcat_scr[slot] = jnp.concatenate([xh, xl, xh], axis=1)

    @pl.when(i < prompt_tiles)
    def _():
        body(xp_ref, ap_ref, bp_ref)

    @pl.when(i >= prompt_tiles)
    def _():
        body(xs_ref, as_ref, bs_ref)


def _out_router(xp, a4p, bp, xs, a4s, bs, wo, gamma, wr, br):
    tp, tsm = xp.shape[0], xs.shape[0]
    tm = 512
    npt, nst = tp // tm, tsm // tm
    nt = npt + nst
    t = tp + tsm
    pmap = lambda i: (jnp.minimum(i, npt - 1), 0)
    smap = lambda i: (jnp.clip(i - npt, 0, nst - 1), 0)
    cur = lambda i: (jnp.minimum(i, nt - 1), 0)
    return pl.pallas_call(
        functools.partial(_out_router_kernel, prompt_tiles=npt, tiles=nt),
        grid=(nt + 1,),
        in_specs=[
            pl.BlockSpec((tm, D_MODEL), pmap),
            pl.BlockSpec((4, tm, LANES), lambda i: (0, jnp.minimum(i, npt - 1), 0)),
            pl.BlockSpec((tm, 512), pmap),
            pl.BlockSpec((tm, D_MODEL), smap),
            pl.BlockSpec((4, tm, LANES), lambda i: (0, jnp.clip(i - npt, 0, nst - 1), 0)),
            pl.BlockSpec((tm, 512), smap),
            pl.BlockSpec((D_MODEL, D_MODEL), lambda i: (0, 0)),
            pl.BlockSpec((1, D_MODEL), lambda i: (0, 0)),
            pl.BlockSpec((3 * D_MODEL, LANES), lambda i: (0, 0)),
            pl.BlockSpec((1, LANES), lambda i: (0, 0)),
        ],
        out_specs=[
            pl.BlockSpec((tm, D_MODEL), cur),
            pl.BlockSpec((tm, D_MODEL // 2), cur),
            pl.BlockSpec((tm, LANES), lambda i: (jnp.maximum(i - 1, 0), 0)),
            pl.BlockSpec((1, LANES), lambda i: (0, 0)),
        ],
        scratch_shapes=[pltpu.VMEM((2, tm, 3 * D_MODEL), BF16)],
        out_shape=[
            jax.ShapeDtypeStruct((t, D_MODEL), F32),
            jax.ShapeDtypeStruct((t, D_MODEL // 2), jnp.int32),
            jax.ShapeDtypeStruct((t, LANES), F32),
            jax.ShapeDtypeStruct((1, LANES), F32),
        ],
        compiler_params=_cparams(("arbitrary",)),
        name="out_router",
    )(xp, a4p, bp, xs, a4s, bs, wo, gamma, wr, br)


def _sc_gather_rows(table, idx):
    b = idx.shape[0]
    d = table.shape[1]
    per_worker = b // SC_WORKERS
    nwin = per_worker // SC_WINDOW
    assert per_worker * SC_WORKERS == b and nwin * SC_WINDOW == per_worker
    mesh = plsc.VectorSubcoreMesh(core_axis_name="c", subcore_axis_name="s")

    @functools.partial(
        pl.kernel, mesh=mesh,
        out_type=jax.ShapeDtypeStruct((b, d), table.dtype),
        scratch_types=[pltpu.VMEM((SC_WINDOW,), jnp.int32), pltpu.VMEM((SC_WINDOW, d), table.dtype),
                       pltpu.SemaphoreType.DMA],
        name="sc_gather_rows",
    )
    def gather(table_hbm, idx_hbm, out_hbm, idx_v, rows_v, sem):
        wid = lax.axis_index("s") * SC_CORES + lax.axis_index("c")
        base = wid * per_worker

        @pl.loop(0, nwin)
        def _(j):
            off = pl.multiple_of(base + j * SC_WINDOW, SC_WINDOW)
            pltpu.sync_copy(idx_hbm.at[pl.ds(off, SC_WINDOW)], idx_v)
            pltpu.async_copy(table_hbm.at[idx_v], rows_v, sem).wait()
            pltpu.sync_copy(rows_v, out_hbm.at[pl.ds(off, SC_WINDOW)])

    return gather(table, idx)


def _sc_scatter_rows(x, dest2, nrows):
    t, d = x.shape
    per_worker = t // SC_WORKERS
    nwin = per_worker // SC_SCATTER_WINDOW
    assert per_worker * SC_WORKERS == t and nwin * SC_SCATTER_WINDOW == per_worker
    mesh = plsc.VectorSubcoreMesh(core_axis_name="c", subcore_axis_name="s")

    @functools.partial(
        pl.kernel, mesh=mesh,
        out_type=jax.ShapeDtypeStruct((nrows, d), x.dtype),
        scratch_types=[pltpu.VMEM((TOP_K, SC_SCATTER_WINDOW), jnp.int32),
                       pltpu.VMEM((SC_SCATTER_WINDOW, d), x.dtype), pltpu.SemaphoreType.DMA],
        name="sc_scatter_rows",
    )
    def scatter(x_hbm, dest_hbm, out_hbm, idx_v, rows_v, sem):
        wid = lax.axis_index("s") * SC_CORES + lax.axis_index("c")
        base = wid * per_worker

        @pl.loop(0, nwin)
        def _(j):
            off = pl.multiple_of(base + j * SC_SCATTER_WINDOW, SC_SCATTER_WINDOW)
            pltpu.sync_copy(x_hbm.at[pl.ds(off, SC_SCATTER_WINDOW)], rows_v)
            for k in range(TOP_K):
                pltpu.sync_copy(dest_hbm.at[k, pl.ds(off, SC_SCATTER_WINDOW)], idx_v.at[k])
            for k in range(TOP_K):
                pltpu.async_copy(rows_v, out_hbm.at[idx_v.at[k]], sem).wait()

    return scatter(x, dest2)


def _expert_kernel(be_ref, nu_ref, nv_ref, x_ref, wg_ref, wu_ref, wd_ref, o_ref, wg_s, wu_s, wd_s):
    i = pl.program_id(0)
    used = i < nu_ref[0]
    changed = jnp.logical_or(i == 0, be_ref[i] != be_ref[jnp.maximum(i - 1, 0)])

    @pl.when(jnp.logical_and(used, changed))
    def _():
        wg_s[...] = wg_ref[...].astype(BF16)
        wu_s[...] = wu_ref[...].astype(BF16)
        wd_s[...] = wd_ref[...].astype(BF16)

    @pl.when(used)
    def _():
        row = lax.broadcasted_iota(jnp.int32, x_ref.shape, 0)
        x = _unpack_bf16_pairs(jnp.where(row < nv_ref[i], x_ref[...], 0)).astype(BF16)
        gate = jnp.dot(x, wg_s[...], preferred_element_type=F32)
        up = jnp.dot(x, wu_s[...], preferred_element_type=F32)
        h = (gate * jax.nn.sigmoid(gate) * up).astype(BF16)
        o_ref[...] = _pack_bf16_pairs(jnp.dot(h, wd_s[...], preferred_element_type=F32))

    @pl.when(jnp.logical_not(used))
    def _():
        o_ref[...] = jnp.zeros_like(o_ref)


def _experts(blk_e, n_used, nvalid, xb, w_gate, w_up, w_down):
    rows = xb.shape[0]
    nblocks = rows // MOE_ROWS
    grid_spec = pltpu.PrefetchScalarGridSpec(
        num_scalar_prefetch=3,
        grid=(nblocks,),
        in_specs=[
            pl.BlockSpec((MOE_ROWS, D_MODEL // 2), lambda i, be, nu, nv: (i, 0)),
            pl.BlockSpec((None, D_MODEL, D_EXPERT), lambda i, be, nu, nv: (be[i], 0, 0)),
            pl.BlockSpec((None, D_MODEL, D_EXPERT), lambda i, be, nu, nv: (be[i], 0, 0)),
            pl.BlockSpec((None, D_EXPERT, D_MODEL), lambda i, be, nu, nv: (be[i], 0, 0)),
        ],
        out_specs=pl.BlockSpec((MOE_ROWS, D_MODEL // 2), lambda i, be, nu, nv: (i, 0)),
        scratch_shapes=[pltpu.VMEM((D_MODEL, D_EXPERT), BF16), pltpu.VMEM((D_MODEL, D_EXPERT), BF16),
                        pltpu.VMEM((D_EXPERT, D_MODEL), BF16)],
    )
    return pl.pallas_call(
        _expert_kernel,
        grid_spec=grid_spec,
        out_shape=jax.ShapeDtypeStruct((rows, D_MODEL // 2), jnp.int32),
        compiler_params=_cparams(("arbitrary",)),
        name="experts",
    )(blk_e, n_used, nvalid, xb, w_gate, w_up, w_down)


def _combine_kernel(x1_ref, y1_ref, y2_ref, route_ref, g_ref, outp_ref, outs_ref, *, prompt_tiles):
    r = route_ref[...]
    x = (x1_ref[...] + r[:, 2:3] * _unpack_bf16_pairs(y1_ref[...])
         + r[:, 3:4] * _unpack_bf16_pairs(y2_ref[...]))
    ms = jnp.mean(x * x, axis=-1, keepdims=True)
    y = x * lax.rsqrt(ms + EPS) * g_ref[...]
    i = pl.program_id(0)

    @pl.when(i < prompt_tiles)
    def _():
        outp_ref[...] = y

    @pl.when(i >= prompt_tiles)
    def _():
        outs_ref[...] = y


def _combine_norm(x1, ygath, route, gamma, tp):
    t = x1.shape[0]
    tm = 512
    nt, npt = t // tm, tp // tm
    return pl.pallas_call(
        functools.partial(_combine_kernel, prompt_tiles=npt),
        grid=(nt,),
        in_specs=[
            pl.BlockSpec((tm, D_MODEL), lambda i: (i, 0)),
            pl.BlockSpec((tm, D_MODEL // 2), lambda i: (i, 0)),
            pl.BlockSpec((tm, D_MODEL // 2), lambda i: (i + nt, 0)),
            pl.BlockSpec((tm, LANES), lambda i: (i, 0)),
            pl.BlockSpec((1, D_MODEL), lambda i: (0, 0)),
        ],
        out_specs=[
            pl.BlockSpec((tm, D_MODEL), lambda i: (jnp.minimum(i, npt - 1), 0)),
            pl.BlockSpec((tm, D_MODEL), lambda i: (jnp.maximum(i - npt, 0), 0)),
        ],
        out_shape=[jax.ShapeDtypeStruct((tp, D_MODEL), F32), jax.ShapeDtypeStruct((t - tp, D_MODEL), F32)],
        compiler_params=_cparams(("arbitrary",)),
        name="combine_norm",
    )(x1, ygath, ygath, route, gamma)


def _band_index():
    c = (2 * QB - np.arange(2 * QB)) % (2 * QB)
    return c, c <= QB


def _bias_a_prompt(table_a):
    c, valid = _band_index()
    idx = np.stack([_t5_bucket_np(d * np.clip(QB - c, 0, QB)) for d in DILATIONS])
    return jnp.where(valid, jnp.transpose(table_a[idx], (0, 2, 1)) * LOG2E, NEG)


def _bias_b_prompt(table_b):
    c, valid = _band_index()
    valid = valid & (c >= 1)
    h = jnp.where(valid, table_b[_t5_bucket_np(np.clip(QB - c, 0, QB))].T * LOG2E, NEG)
    return jnp.transpose(h.reshape(KV_B, G_B, 2 * QB), (1, 0, 2)).reshape(H_B, 2 * QB)


def _sample_bias(table, span, t, log2_weight):
    cols = span + LANES
    period = cols + LANES
    x = np.arange(period)
    dist = np.where(x >= period - t, span - x + period, span - x)
    extra = log2_weight(dist)
    valid = np.isfinite(extra)
    u = jnp.where(valid, table[_t5_bucket_np(np.maximum(dist, 0))].T * LOG2E
                  + np.where(valid, extra, 0.0).astype(np.float32), NEG)
    rows = jnp.tile(u, (1, t))[:, :t * (period - 1)].reshape(u.shape[0], t, period - 1)[:, :, :cols]
    return rows.reshape(u.shape[0] * t, cols)


def _bias_a_sample(table_a, t):
    def log2_count(dist):
        count = np.zeros(dist.shape, np.int64)
        for w, d in zip(WINDOWS, DILATIONS):
            count += (dist >= 0) & (dist % d == 0) & (dist <= w)
        return np.where(count > 0, np.log2(np.maximum(count, 1)), -np.inf)

    return _sample_bias(table_a, WIN_A, t, log2_count)


def _bias_b_sample(table_b, t):
    return _sample_bias(table_b, WIN_B, t,
                        lambda dist: np.where((dist >= 0) & (dist < WIN_B), 0.0, -np.inf))


def _dest_kernel(route_ref, cnt_ref, tri_ref, dest_ref, meta_ref, run_scr, pst_scr):
    i = pl.program_id(0)
    tm = route_ref.shape[0]
    r = route_ref[...]
    lane = lax.broadcasted_iota(jnp.int32, (tm, LANES), 1)
    lanef = lane.astype(F32)
    oh0 = lanef == r[:, 0:1]
    oh1 = lanef == r[:, 1:2]
    ohf = jnp.concatenate([oh0, oh1], axis=0).astype(F32)

    @pl.when(i == 0)
    def _():
        cnt = jnp.broadcast_to(cnt_ref[...], (LANES, LANES))
        padded = jnp.floor((cnt + (MOE_ROWS - 1)) * (1.0 / MOE_ROWS)) * MOE_ROWS
        lane_e = lax.broadcasted_iota(jnp.int32, (LANES, LANES), 1)
        x = padded
        for sh in (1, 2, 4, 8, 16, 32, 64):
            x = x + jnp.where(lane_e >= sh, pltpu.roll(x, sh, 1), 0.0)
        pst_scr[...] = (x - padded)[0:1]
        run_scr[...] = jnp.zeros_like(run_scr)
        wide = lambda v: jnp.concatenate([v.T, v.T], axis=1)
        cnt_t, bend_t = wide(cnt), wide(x * (1.0 / MOE_ROWS))
        bstart_t = wide((x - padded) * (1.0 / MOE_ROWS))
        blk = lax.broadcasted_iota(jnp.int32, (LANES, 2 * LANES), 1).astype(F32)
        exp = lax.broadcasted_iota(jnp.int32, (LANES, 2 * LANES), 0)
        real = exp < N_EXPERTS
        blk_e = jnp.minimum(jnp.sum(jnp.where(real & (bend_t <= blk), 1.0, 0.0), axis=0, keepdims=True),
                            N_EXPERTS - 1.0)
        mine = exp.astype(F32) == blk_e
        within = blk[0:1] - jnp.sum(jnp.where(mine, bstart_t, 0.0), axis=0, keepdims=True)
        nvalid = jnp.clip(jnp.sum(jnp.where(mine, cnt_t, 0.0), axis=0, keepdims=True) - within * MOE_ROWS,
                          0.0, float(MOE_ROWS))
        n_used = jnp.max(jnp.where(real, bend_t, 0.0), axis=0, keepdims=True)
        meta_ref[...] = jnp.concatenate([blk_e, nvalid, n_used, jnp.zeros((5, 2 * LANES), F32)],
                                        axis=0).astype(jnp.int32)

    csum = jnp.dot(tri_ref[...], ohf.astype(BF16), preferred_element_type=F32)
    val = csum + (run_scr[...] + pst_scr[...] - 1.0)
    d0 = jnp.sum(jnp.where(oh0, val[:tm], 0.0), axis=-1, keepdims=True)
    d1 = jnp.sum(jnp.where(oh1, val[tm:], 0.0), axis=-1, keepdims=True)
    tile = jnp.where(lane == 0, d0, jnp.where(lane == 1, d1, 0.0))
    dest_ref[...] = tile.T[:8].astype(jnp.int32)
    run_scr[...] += jnp.sum(ohf, axis=0, keepdims=True)


def _dispatch(route, cnt):
    t = route.shape[0]
    tm = 512
    tri = (jnp.arange(2 * tm)[:, None] >= jnp.arange(2 * tm)[None, :]).astype(BF16)
    nblocks = -(-t * TOP_K // MOE_ROWS) + N_EXPERTS
    assert nblocks <= 2 * LANES
    dest, meta = pl.pallas_call(
        _dest_kernel,
        grid=(t // tm,),
        in_specs=[pl.BlockSpec((tm, LANES), lambda i: (i, 0)),
                  pl.BlockSpec((1, LANES), lambda i: (0, 0)),
                  pl.BlockSpec((2 * tm, 2 * tm), lambda i: (0, 0))],
        out_specs=[pl.BlockSpec((8, tm), lambda i: (0, i)),
                   pl.BlockSpec((8, 2 * LANES), lambda i: (0, 0))],
        out_shape=[jax.ShapeDtypeStruct((8, t), jnp.int32), jax.ShapeDtypeStruct((8, 2 * LANES), jnp.int32)],
        scratch_shapes=[pltpu.VMEM((1, LANES), F32), pltpu.VMEM((1, LANES), F32)],
        compiler_params=_cparams(("arbitrary",)),
        name="moe_dest",
    )(route, cnt, tri)
    return dest[:TOP_K], meta[0, :nblocks], meta[2, :1], meta[1, :nblocks]


def kernel(x_prompt, x_sample, cache_a_k, cache_a_v, cache_b_k, cache_b_v, rel_bias_table, attn_norm, w_in,
           w_out, attn_sinks, ffn_norm, w_router_group, b_router_group, w_router_expert, b_router_expert,
           w_gate, w_up, w_down, final_norm):
    s = x_prompt.shape[1]
    ns, ts = x_sample.shape[0], x_sample.shape[1]
    table_a = rel_bias_table[:, :H_A]
    table_b = rel_bias_table[:, H_A:]

    w = w_in[0]
    wqa, wka, wva, wqb, wkb, wvb = (w[:, 0:512], w[:, 512:1024], w[:, 1024:1536], w[:, 1536:2048],
                                    w[:, 2048:2176], w[:, 2176:2304])
    wqb = jnp.transpose(wqb.reshape(D_MODEL, KV_B, G_B, HEAD_DIM), (0, 2, 1, 3)).reshape(D_MODEL, 512)
    wp = jnp.concatenate([wka, wva, wqa, wqb, wkb, wvb], axis=1).astype(BF16)
    cscale = jnp.concatenate([jnp.ones((1, 1024), F32), jnp.full((1, 1024), SCALE * LOG2E, F32),
                              jnp.ones((1, 256), F32)], axis=1)
    wo = w_out[0]
    wo_b = jnp.transpose(wo[512:].reshape(KV_B, G_B, HEAD_DIM, D_MODEL), (1, 0, 2, 3)).reshape(512, D_MODEL)
    wo_p = jnp.concatenate([wo[:512], wo_b], axis=0).astype(BF16)
    wr = jnp.concatenate([w_router_group[0],
                          jnp.transpose(w_router_expert[0], (1, 0, 2)).reshape(D_MODEL, N_EXPERTS),
                          jnp.zeros((D_MODEL, LANES - N_GROUPS - N_EXPERTS), F32)], axis=1)
    wr_hi = wr.astype(BF16)
    wr = jnp.concatenate([wr_hi, wr_hi, (wr - wr_hi.astype(F32)).astype(BF16)], axis=0)
    br = jnp.concatenate([b_router_group[0], b_router_expert[0].reshape(N_EXPERTS),
                          jnp.zeros((LANES - N_GROUPS - N_EXPERTS,), F32)]).reshape(1, LANES)
    sinks2 = attn_sinks[0] * LOG2E
    sinks_gk = jnp.transpose(sinks2.reshape(KV_B, G_B), (1, 0)).reshape(H_B)
    sink_rows_p = jnp.repeat(sinks_gk, QB).reshape(G_B, 1, 2 * QB)
    sink_rows_s = jnp.repeat(sinks2, ts).reshape(H_B * ts, 1)
    emat = jnp.tile(jnp.arange(LANES)[:, None] == (jnp.arange(A_WIDTH)[None, :] // HEAD_DIM),
                    (3, 1)).astype(BF16)
    attn_g = attn_norm[0].reshape(1, D_MODEL)
    ffn_g = ffn_norm[0].reshape(1, D_MODEL)

    xp = x_prompt.reshape(s, D_MODEL)
    aperm, qb_p, kvb_p, akv32, bkv32 = _proj_prompt(xp, attn_g, wp, cscale)
    a4 = _attn_a_prompt(aperm, _bias_a_prompt(table_a), emat)
    ob_p = _attn_b_prompt(qb_p, kvb_p, _bias_b_prompt(table_b), sink_rows_p)

    xs = x_sample.reshape(ns * ts, D_MODEL)
    q_s, kv_s = _proj_sample(xs, attn_g, wp, cscale)
    akt = jnp.transpose(cache_a_k[0], (0, 2, 3, 1)).reshape(ns, A_WIDTH, WIN_A)
    avt = jnp.transpose(cache_a_v[0], (0, 2, 3, 1)).reshape(ns, A_WIDTH, WIN_A)
    bkt = jnp.transpose(cache_b_k[0], (0, 2, 3, 1)).reshape(ns, LANES, WIN_B)
    bvt = jnp.transpose(cache_b_v[0], (0, 2, 3, 1)).reshape(ns, LANES, WIN_B)
    oa_s, ob_s = _attn_sample(q_s.reshape(ns, ts, 1024), kv_s.reshape(ns, ts, 1280), akt, avt, bkt, bvt,
                              _bias_a_sample(table_a, ts), _bias_b_sample(table_b, ts), sink_rows_s)
    a4_s = jnp.transpose(oa_s.reshape(ns * ts, 4, LANES), (1, 0, 2))

    x1, xn, route, cnt = _out_router(xp, a4, ob_p, xs, a4_s, ob_s.reshape(ns * ts, 512), wo_p, ffn_g, wr, br)
    dest2, blk_e, n_used, nvalid = _dispatch(route, cnt)
    xb = _sc_scatter_rows(xn, dest2, blk_e.shape[0] * MOE_ROWS)
    yb = _experts(blk_e, n_used, nvalid, xb, w_gate[0], w_up[0], w_down[0])
    y_p, y_s = _combine_norm(x1, _sc_gather_rows(yb, dest2.reshape(-1)), route, final_norm.reshape(1, D_MODEL), s)

    y_prompt = y_p.reshape(1, s, D_MODEL)
    y_sample = y_s.reshape(ns, ts, D_MODEL)
    keep_a, keep_b = min(WIN_A, s), min(WIN_B, s)
    pak = akv32[s - keep_a:, :512].reshape(1, 1, keep_a, H_A, HEAD_DIM)
    pav = akv32[s - keep_a:, 512:].reshape(1, 1, keep_a, H_A, HEAD_DIM)
    pbk = bkv32[s - keep_b:, :128].reshape(1, 1, keep_b, KV_B, HEAD_DIM)
    pbv = bkv32[s - keep_b:, 128:].reshape(1, 1, keep_b, KV_B, HEAD_DIM)
    sak = kv_s[:, 0:512].reshape(1, ns, ts, H_A, HEAD_DIM)
    sav = kv_s[:, 512:1024].reshape(1, ns, ts, H_A, HEAD_DIM)
    sbk = kv_s[:, 1024:1152].reshape(1, ns, ts, KV_B, HEAD_DIM)
    sbv = kv_s[:, 1152:1280].reshape(1, ns, ts, KV_B, HEAD_DIM)
    return (y_prompt, y_sample, pak, pav, pbk, pbv, sak, sav, sbk, sbv)
```

```python
import functools
import math

import jax
import jax.numpy as jnp
import numpy as np
from jax import lax
from jax.experimental import pallas as pl
from jax.experimental.pallas import tpu as pltpu
from jax.experimental.pallas import tpu_sc as plsc

D_MODEL = 1024
HEAD_DIM = 64
H_A = 8
H_B = 8
KV_B = 2
G_B = 4
DILATIONS = (1, 4, 16)
WINDOWS = (128, 512, 2048)
WIN_A = 2048
WIN_B = 128
NUM_BUCKETS = 32
MAX_DISTANCE = 2048
N_GROUPS = 4
EXPERTS_PER_GROUP = 8
N_EXPERTS = 32
TOP_K = 2
D_EXPERT = 512
EPS = 1e-5
SCALE = HEAD_DIM ** -0.5
PAST_LEN = 16384

LANES = 128
SPAN = 2048
QB = 128
NCHUNK = 9
A_WIDTH = H_A * HEAD_DIM
MOE_ROWS = 512
SC_CORES = 2
SC_SUBCORES = 16
SC_WORKERS = SC_CORES * SC_SUBCORES
SC_WINDOW = 64
SC_SCATTER_WINDOW = 32
NEG = -1e30
LOG2E = math.log2(math.e)
B_STEP = 512
VMEM_LIMIT = 56 * 1024 * 1024

F32 = jnp.float32
BF16 = jnp.bfloat16


def _t5_bucket_np(dist):
    dist = np.asarray(dist, np.int64)
    max_exact = NUM_BUCKETS // 2
    d = np.maximum(dist, 1).astype(np.float32)
    ratio = np.log(d / np.float32(max_exact)) / np.float32(math.log(MAX_DISTANCE / max_exact))
    large = max_exact + (ratio * np.float32(NUM_BUCKETS - max_exact)).astype(np.int32)
    large = np.minimum(large, NUM_BUCKETS - 1)
    return np.where(dist < max_exact, dist, large).astype(np.int32)


def _cparams(sem, vmem=VMEM_LIMIT):
    return pltpu.CompilerParams(dimension_semantics=sem, vmem_limit_bytes=vmem)


def _proj_prompt_kernel(x_ref, g_ref, w_ref, cs_ref, aperm_ref, qb_ref, kvb_ref, akv_ref, bkv_ref,
                        h_scr, p_scr):
    n = pl.program_id(1)

    @pl.when(n == 0)
    def _():
        x = x_ref[...]
        ms = jnp.mean(x * x, axis=-1, keepdims=True)
        h_scr[...] = (x * lax.rsqrt(ms + EPS) * g_ref[...]).astype(BF16)

    p = jnp.dot(h_scr[...], w_ref[...], preferred_element_type=F32) * cs_ref[...]

    @pl.when(n < 6)
    def _():
        aperm_ref[0] = p.astype(BF16)
        p_scr[0, 0] = p[:, :LANES]
        p_scr[0, 1] = p[:, LANES:]
        quarter = SPAN // 4
        for r in range(4):
            lo = p_scr[0, 0, pl.ds(r, quarter, stride=4), :]
            hi = p_scr[0, 1, pl.ds(r, quarter, stride=4), :]
            p_scr[1, 0, r * quarter:(r + 1) * quarter, :] = lo
            p_scr[1, 1, r * quarter:(r + 1) * quarter, :] = hi
            aperm_ref[1, r * quarter:(r + 1) * quarter, :] = jnp.concatenate([lo, hi], axis=1).astype(BF16)
        for r16 in range(16):
            start = (r16 % 4) * quarter + r16 // 4
            t = jnp.concatenate([p_scr[1, 0, pl.ds(start, QB, stride=4), :],
                                 p_scr[1, 1, pl.ds(start, QB, stride=4), :]], axis=1)
            aperm_ref[2, r16 * QB:(r16 + 1) * QB, :] = t.astype(BF16)

    @pl.when(n < 4)
    def _():
        akv_ref[...] = p

    @pl.when(jnp.logical_or(n == 6, n == 7))
    def _():
        qb_ref[...] = p.astype(BF16)

    @pl.when(n == 8)
    def _():
        kvb_ref[...] = p.astype(BF16)
        bkv_ref[...] = p


def _proj_prompt(x, gamma, w, cscale):
    s = x.shape[0]
    nspan = s // SPAN
    return pl.pallas_call(
        _proj_prompt_kernel,
        grid=(nspan, NCHUNK),
        in_specs=[
            pl.BlockSpec((SPAN, D_MODEL), lambda b, n: (b, 0)),
            pl.BlockSpec((1, D_MODEL), lambda b, n: (0, 0)),
            pl.BlockSpec((D_MODEL, 256), lambda b, n: (0, n)),
            pl.BlockSpec((1, 256), lambda b, n: (0, n)),
        ],
        out_specs=[
            pl.BlockSpec((3, SPAN, 256), lambda b, n: (0, b, jnp.minimum(n, 5))),
            pl.BlockSpec((SPAN, 256), lambda b, n: (b, jnp.clip(n - 6, 0, 1))),
            pl.BlockSpec((SPAN, 256), lambda b, n: (b, 0)),
            pl.BlockSpec((SPAN, 256), lambda b, n: (b, jnp.minimum(n, 3))),
            pl.BlockSpec((SPAN, 256), lambda b, n: (b, 0)),
        ],
        out_shape=[
            jax.ShapeDtypeStruct((3, s, 3 * A_WIDTH), BF16),
            jax.ShapeDtypeStruct((s, 512), BF16),
            jax.ShapeDtypeStruct((s, 256), BF16),
            jax.ShapeDtypeStruct((s, 1024), F32),
            jax.ShapeDtypeStruct((s, 256), F32),
        ],
        scratch_shapes=[pltpu.VMEM((SPAN, D_MODEL), BF16), pltpu.VMEM((2, 2, SPAN, LANES), F32)],
        compiler_params=_cparams(("arbitrary", "arbitrary")),
        name="proj_prompt",
    )(x, gamma, w, cscale)


def _proj_sample_kernel(x_ref, g_ref, w_ref, cs_ref, q_ref, kv_ref):
    x = x_ref[...]
    ms = jnp.mean(x * x, axis=-1, keepdims=True)
    h = (x * lax.rsqrt(ms + EPS) * g_ref[...]).astype(BF16)
    p = jnp.dot(h, w_ref[...], preferred_element_type=F32) * cs_ref[...]
    kv_ref[:, :1024] = p[:, :1024]
    kv_ref[:, 1024:] = p[:, 2048:]
    q_ref[...] = p[:, 1024:2048]


def _proj_sample(x, gamma, w, cscale):
    t = x.shape[0]
    tm = 512
    return pl.pallas_call(
        _proj_sample_kernel,
        grid=(t // tm,),
        in_specs=[
            pl.BlockSpec((tm, D_MODEL), lambda i: (i, 0)),
            pl.BlockSpec((1, D_MODEL), lambda i: (0, 0)),
            pl.BlockSpec((D_MODEL, 2304), lambda i: (0, 0)),
            pl.BlockSpec((1, 2304), lambda i: (0, 0)),
        ],
        out_specs=[
            pl.BlockSpec((tm, 1024), lambda i: (i, 0)),
            pl.BlockSpec((tm, 1280), lambda i: (i, 0)),
        ],
        out_shape=[
            jax.ShapeDtypeStruct((t, 1024), F32),
            jax.ShapeDtypeStruct((t, 1280), F32),
        ],
        compiler_params=_cparams(("arbitrary",)),
        name="proj_sample",
    )(x, gamma, w, cscale)


def _spread_heads(w, e3_ref):
    hi = w.astype(BF16)
    r1 = w - hi.astype(F32)
    mid = r1.astype(BF16)
    low = (r1 - mid.astype(F32)).astype(BF16)
    return jnp.dot(jnp.concatenate([hi, mid, low], axis=1), e3_ref[...], preferred_element_type=F32)


def _pair_tile(q2, kk, vv, bias_t, lo, sink=None):
    zero = jnp.zeros_like(q2)
    qq = jnp.concatenate([jnp.where(lo, q2, zero), jnp.where(lo, zero, q2)], axis=0)
    st = lax.dot_general(kk, qq, (((1,), (1,)), ((), ())), preferred_element_type=F32)
    st = st + bias_t
    m = jnp.max(st, axis=0, keepdims=True)
    if sink is not None:
        m = jnp.maximum(m, sink)
    p = jnp.exp2(st - m)
    den = jnp.sum(p, axis=0, keepdims=True)
    if sink is not None:
        den = den + jnp.exp2(sink - m)
    pn = (p * (1.0 / den)).astype(BF16)
    o = lax.dot_general(pn, vv, (((0,), (0,)), ((), ())), preferred_element_type=F32)
    return jnp.where(lo, o[:QB], o[QB:]), m + jnp.log2(den)


def _fill_band_tiles(h_ref, bias_scr):
    nk = 2 * QB
    prev = lax.broadcasted_iota(jnp.int32, (nk, nk), 0) < QB
    for pair in range(h_ref.shape[0] // 2):
        halves = []
        for hh in range(2):
            row = h_ref[2 * pair + hh:2 * pair + hh + 1, :]
            band = pltpu.roll(jnp.broadcast_to(row, (nk, nk)), 0, 1, stride=1, stride_axis=0)
            halves.append(band[:, :QB])
        tile = jnp.concatenate(halves, axis=1)
        bias_scr[0, pair] = tile
        bias_scr[1, pair] = jnp.where(prev, NEG, tile)


def _attn_a_kernel(q_ref, kvc_ref, kvp_ref, h_ref, e_ref, out_ref, o_scr, st_scr, bias_scr):
    b = pl.program_id(0)
    g = pl.program_id(1)
    nblk = jnp.where(g == 0, 16, jnp.where(g == 1, 4, 1))
    lane = lax.broadcasted_iota(jnp.int32, (QB, LANES), 1)
    lo = lane < HEAD_DIM

    @pl.when(b == 0)
    def _():
        _fill_band_tiles(h_ref, bias_scr.at[g])

    bias_ref = bias_scr.at[g]

    for cb in range(SPAN // QB):
        first = lax.rem(jnp.int32(cb), nblk) == 0
        rows = slice(cb * QB, (cb + 1) * QB)
        prow_c = max(cb - 1, 0) * QB
        prow_p = pl.multiple_of(jnp.where(first, cb + nblk - 1, 0) * QB, QB)
        variant = jnp.logical_and(first, b == 0).astype(jnp.int32)
        stats = []
        for hp in range(4):
            ks = slice(hp * LANES, (hp + 1) * LANES)
            vs = slice(A_WIDTH + hp * LANES, A_WIDTH + (hp + 1) * LANES)
            kp = jnp.where(first, kvp_ref[pl.ds(prow_p, QB), ks], kvc_ref[prow_c:prow_c + QB, ks])
            vp = jnp.where(first, kvp_ref[pl.ds(prow_p, QB), vs], kvc_ref[prow_c:prow_c + QB, vs])
            kk = jnp.concatenate([kp, kvc_ref[rows, ks]], axis=0)
            vv = jnp.concatenate([vp, kvc_ref[rows, vs]], axis=0)
            o, lse = _pair_tile(q_ref[rows, ks], kk, vv, bias_ref[variant, hp], lo)
            o_scr[g, hp, rows, :] = o
            stats += [lse[:, :QB], lse[:, QB:]]
        sm = jnp.concatenate(stats + [jnp.zeros((LANES - H_A, QB), F32)], axis=0)
        st_scr[g, rows, :] = sm.T

    @pl.when(g == 2)
    def _():
        def merge(c, carry):
            r2 = lax.rem(c, 4) * (SPAN // 4) + c // 4
            r3 = pl.multiple_of(c * QB, QB)
            l1 = st_scr[0, pl.ds(c, QB, stride=16), :]
            l2 = st_scr[1, pl.ds(r2, QB, stride=4), :]
            l3 = st_scr[2, pl.ds(r3, QB), :]
            mx = jnp.maximum(jnp.maximum(l1, l2), l3)
            w1 = jnp.exp2(l1 - mx)
            w2 = jnp.exp2(l2 - mx)
            w3 = jnp.exp2(l3 - mx)
            tot = w1 + w2 + w3
            a1 = _spread_heads(w1 / tot, e_ref)
            a2 = _spread_heads(w2 / tot, e_ref)
            a3 = _spread_heads(w3 / tot, e_ref)
            for hp in range(4):
                sl = slice(hp * LANES, (hp + 1) * LANES)
                o1 = o_scr[0, hp, pl.ds(c, QB, stride=16), :]
                o2 = o_scr[1, hp, pl.ds(r2, QB, stride=4), :]
                o3 = o_scr[2, hp, pl.ds(r3, QB), :]
                out_ref[hp, pl.ds(c, QB, stride=16), :] = a1[:, sl] * o1 + a2[:, sl] * o2 + a3[:, sl] * o3
            return carry

        lax.fori_loop(0, 16, merge, 0, unroll=4)


def _attn_a_prompt(aperm, bias_a, emat):
    s = aperm.shape[1]
    nspan = s // SPAN
    return pl.pallas_call(
        _attn_a_kernel,
        grid=(nspan, 3),
        in_specs=[
            pl.BlockSpec((None, SPAN, A_WIDTH), lambda b, g: (g, b, 2)),
            pl.BlockSpec((None, SPAN, 2 * A_WIDTH), lambda b, g: (g, b, 0)),
            pl.BlockSpec((None, SPAN, 2 * A_WIDTH), lambda b, g: (g, jnp.maximum(b - 1, 0), 0)),
            pl.BlockSpec((None, H_A, 2 * QB), lambda b, g: (g, 0, 0)),
            pl.BlockSpec((3 * LANES, A_WIDTH), lambda b, g: (0, 0)),
        ],
        out_specs=pl.BlockSpec((4, SPAN, LANES), lambda b, g: (0, b, 0)),
        out_shape=jax.ShapeDtypeStruct((4, s, LANES), F32),
        scratch_shapes=[pltpu.VMEM((3, 4, SPAN, LANES), F32), pltpu.VMEM((3, SPAN, LANES), F32),
                        pltpu.VMEM((3, 2, 4, 2 * QB, 2 * QB), F32)],
        compiler_params=_cparams(("arbitrary", "arbitrary")),
        name="attn_a_prompt",
    )(aperm, aperm, aperm, bias_a, emat)


def _attn_b_kernel(q_ref, kvc_ref, kvp_ref, h_ref, sink_ref, out_ref, bias_ref):
    i = pl.program_id(0)
    lane = lax.broadcasted_iota(jnp.int32, (QB, LANES), 1)
    lo = lane < HEAD_DIM

    @pl.when(i == 0)
    def _():
        _fill_band_tiles(h_ref, bias_ref)

    variant = (i == 0).astype(jnp.int32)
    for j in range(B_STEP // QB):
        rows = slice(j * QB, (j + 1) * QB)
        if j == 0:
            kp, vp = kvp_ref[:, :LANES], kvp_ref[:, LANES:]
        else:
            kp, vp = kvc_ref[(j - 1) * QB:j * QB, :LANES], kvc_ref[(j - 1) * QB:j * QB, LANES:]
        kk = jnp.concatenate([kp, kvc_ref[rows, :LANES]], axis=0)
        vv = jnp.concatenate([vp, kvc_ref[rows, LANES:]], axis=0)
        for g in range(G_B):
            bias_t = bias_ref[variant, g] if j == 0 else bias_ref[0, g]
            o, _ = _pair_tile(q_ref[rows, g * LANES:(g + 1) * LANES], kk, vv, bias_t, lo, sink=sink_ref[g])
            out_ref[rows, g * LANES:(g + 1) * LANES] = o.astype(BF16)


def _attn_b_prompt(qb, kvb, bias_b, sink_rows):
    s = qb.shape[0]
    per = B_STEP // QB
    return pl.pallas_call(
        _attn_b_kernel,
        grid=(s // B_STEP,),
        in_specs=[
            pl.BlockSpec((B_STEP, 512), lambda i: (i, 0)),
            pl.BlockSpec((B_STEP, 256), lambda i: (i, 0)),
            pl.BlockSpec((QB, 256), lambda i: (jnp.maximum(i * per - 1, 0), 0)),
            pl.BlockSpec((H_B, 2 * QB), lambda i: (0, 0)),
            pl.BlockSpec((G_B, 1, 2 * QB), lambda i: (0, 0, 0)),
        ],
        out_specs=pl.BlockSpec((B_STEP, 512), lambda i: (i, 0)),
        out_shape=jax.ShapeDtypeStruct((s, 512), BF16),
        scratch_shapes=[pltpu.VMEM((2, G_B, 2 * QB, 2 * QB), F32)],
        compiler_params=_cparams(("arbitrary",)),
        name="attn_b_prompt",
    )(qb, kvb, kvb, bias_b, sink_rows)


def _attn_sample_kernel(q_ref, kvn_ref, akt_ref, avt_ref, bkt_ref, bvt_ref, cba_ref, cbb_ref, sink_ref,
                        oa_ref, ob_ref):
    t = q_ref.shape[0]
    q = q_ref[...]
    kvn = kvn_ref[...]
    kvn_p = jnp.concatenate([kvn, jnp.zeros((LANES - t, kvn.shape[1]), F32)], axis=0).astype(BF16)
    lane_a = lax.broadcasted_iota(jnp.int32, (t, A_WIDTH), 1) // HEAD_DIM

    qa = q[:, :A_WIDTH]
    qbd = jnp.concatenate([jnp.where(lane_a == h, qa, 0.0) for h in range(H_A)], axis=0).astype(BF16)
    s_c = jnp.dot(qbd, akt_ref[...].astype(BF16), preferred_element_type=F32)
    s_n = lax.dot_general(qbd, kvn_p[:, :A_WIDTH], (((1,), (1,)), ((), ())), preferred_element_type=F32)
    s = jnp.concatenate([s_c, s_n], axis=1) + cba_ref[...]
    m = jnp.max(s, axis=-1, keepdims=True)
    p = jnp.exp2(s - m)
    l = jnp.sum(p, axis=-1, keepdims=True)
    o_n = jnp.dot(p[:, WIN_A:].astype(BF16), kvn_p[:, A_WIDTH:2 * A_WIDTH], preferred_element_type=F32)
    pc = jnp.concatenate([p[:, :WIN_A], jnp.zeros((LANES - H_A * t, WIN_A), F32)], axis=0).astype(BF16)
    o_t = lax.dot_general(avt_ref[...].astype(BF16), pc, (((1,), (1,)), ((), ())),
                          preferred_element_type=F32)
    o_all = o_t.T[:H_A * t] + o_n
    o_sel = jnp.zeros((t, A_WIDTH), F32)
    l_b = jnp.ones((t, A_WIDTH), F32)
    for h in range(H_A):
        sel = lane_a == h
        o_sel = jnp.where(sel, o_all[h * t:(h + 1) * t], o_sel)
        l_b = jnp.where(sel, l[h * t:(h + 1) * t], l_b)
    oa_ref[...] = o_sel / l_b

    lane_b = lax.broadcasted_iota(jnp.int32, (G_B * t, LANES), 1)
    lo = lane_b < HEAD_DIM
    qb2 = jnp.concatenate([q[:, A_WIDTH + g * LANES:A_WIDTH + (g + 1) * LANES] for g in range(G_B)], axis=0)
    qm = jnp.concatenate([jnp.where(lo, qb2, 0.0), jnp.where(lo, 0.0, qb2)], axis=0).astype(BF16)
    kb_n = kvn_p[:, 2 * A_WIDTH:2 * A_WIDTH + LANES]
    vb_n = kvn_p[:, 2 * A_WIDTH + LANES:]
    sb_c = jnp.dot(qm, bkt_ref[...].astype(BF16), preferred_element_type=F32)
    sb_n = lax.dot_general(qm, kb_n, (((1,), (1,)), ((), ())), preferred_element_type=F32)
    sb = jnp.concatenate([sb_c, sb_n], axis=1) + cbb_ref[...]
    sink = sink_ref[...]
    mb = jnp.maximum(jnp.max(sb, axis=-1, keepdims=True), sink)
    pbb = jnp.exp2(sb - mb)
    den = jnp.sum(pbb, axis=-1, keepdims=True) + jnp.exp2(sink - mb)
    pbb = pbb.astype(BF16)
    ob = lax.dot_general(pbb[:, :WIN_B], bvt_ref[...].astype(BF16), (((1,), (1,)), ((), ())),
                         preferred_element_type=F32)
    ob = (ob + jnp.dot(pbb[:, WIN_B:], vb_n, preferred_element_type=F32)) / den
    half = G_B * t
    lo8 = lo[:t]
    for g in range(G_B):
        ob_ref[:, g * LANES:(g + 1) * LANES] = jnp.where(
            lo8, ob[g * t:(g + 1) * t], ob[half + g * t:half + (g + 1) * t])


def _attn_sample(q3, kvn3, akt, avt, bkt, bvt, cbias_a, cbias_b, sink_rows):
    ns, t = q3.shape[0], q3.shape[1]
    return pl.pallas_call(
        _attn_sample_kernel,
        grid=(ns,),
        in_specs=[
            pl.BlockSpec((None, t, 1024), lambda n: (n, 0, 0)),
            pl.BlockSpec((None, t, 1280), lambda n: (n, 0, 0)),
            pl.BlockSpec((None, A_WIDTH, WIN_A), lambda n: (n, 0, 0)),
            pl.BlockSpec((None, A_WIDTH, WIN_A), lambda n: (n, 0, 0)),
            pl.BlockSpec((None, LANES, WIN_B), lambda n: (n, 0, 0)),
            pl.BlockSpec((None, LANES, WIN_B), lambda n: (n, 0, 0)),
            pl.BlockSpec((H_A * t, WIN_A + LANES), lambda n: (0, 0)),
            pl.BlockSpec((H_B * t, WIN_B + LANES), lambda n: (0, 0)),
            pl.BlockSpec((H_B * t, 1), lambda n: (0, 0)),
        ],
        out_specs=[
            pl.BlockSpec((None, t, 512), lambda n: (n, 0, 0)),
            pl.BlockSpec((None, t, 512), lambda n: (n, 0, 0)),
        ],
        out_shape=[jax.ShapeDtypeStruct((ns, t, 512), F32), jax.ShapeDtypeStruct((ns, t, 512), F32)],
        compiler_params=_cparams(("arbitrary",)),
        name="attn_sample",
    )(q3, kvn3, akt, avt, bkt, bvt, cbias_a, cbias_b, sink_rows)


def _route(logits):
    lane = lax.broadcasted_iota(jnp.int32, logits.shape, 1).astype(F32)
    big = jnp.float32(1 << 20)
    ninf = jnp.float32(-jnp.inf)
    gmask = lane < N_GROUPS
    lg = jnp.where(gmask, logits, ninf)
    gmax = jnp.max(lg, axis=-1, keepdims=True)
    grp = jnp.min(jnp.where(lg == gmax, lane, big), axis=-1, keepdims=True)
    pg_top = 1.0 / jnp.sum(jnp.exp(lg - gmax), axis=-1, keepdims=True)
    e0 = N_GROUPS + grp * EXPERTS_PER_GROUP
    emask = jnp.logical_and(lane >= e0, lane < e0 + EXPERTS_PER_GROUP)
    le = jnp.where(emask, logits, ninf)
    emax = jnp.max(le, axis=-1, keepdims=True)
    esum = jnp.sum(jnp.exp(le - emax), axis=-1, keepdims=True)
    i1 = jnp.min(jnp.where(le == emax, lane, big), axis=-1, keepdims=True)
    le2 = jnp.where(lane == i1, ninf, le)
    e2max = jnp.max(le2, axis=-1, keepdims=True)
    i2 = jnp.min(jnp.where(le2 == e2max, lane, big), axis=-1, keepdims=True)
    p1 = 1.0 / esum
    p2 = jnp.exp(e2max - emax) / esum
    g1 = pg_top * p1 / (p1 + p2)
    g2 = pg_top * p2 / (p1 + p2)
    out = jnp.where(lane == 0, i1 - N_GROUPS, 0.0)
    out = jnp.where(lane == 1, i2 - N_GROUPS, out)
    out = jnp.where(lane == 2, g1, out)
    out = jnp.where(lane == 3, g2, out)
    return out


def _pack_bf16_pairs(x):
    half = x.shape[1] // 2

    def rne(v):
        bits = lax.bitcast_convert_type(v, jnp.int32)
        return bits + 0x7FFF + (lax.shift_right_logical(bits, 16) & 1)

    lo = lax.shift_right_logical(rne(x[:, :half]), 16)
    hi = rne(x[:, half:]) & jnp.int32(-65536)
    return lo | hi


def _unpack_bf16_pairs(w):
    lo = lax.bitcast_convert_type(lax.shift_left(w, 16), F32)
    hi = lax.bitcast_convert_type(w & jnp.int32(-65536), F32)
    return jnp.concatenate([lo, hi], axis=1)


def _out_router_kernel(xp_ref, ap_ref, bp_ref, xs_ref, as_ref, bs_ref, wo_ref, g_ref, wr_ref, br_ref,
                       x1_ref, xn_ref, route_ref, cnt_ref, xcat_scr, *, prompt_tiles, tiles):
    i = pl.program_id(0)

    @pl.when(i == 0)
    def _():
        cnt_ref[...] = jnp.zeros_like(cnt_ref)
        xcat_scr[...] = jnp.zeros_like(xcat_scr)

    slot = lax.rem(i, 2)

    def body(x_ref, a_ref, b_ref):
        logits = jnp.dot(xcat_scr[1 - slot], wr_ref[...], preferred_element_type=F32)
        route = _route(logits + br_ref[...])
        route_ref[...] = route
        lanef = lax.broadcasted_iota(jnp.int32, route.shape, 1).astype(F32)
        hits = (lanef == route[:, 0:1]).astype(F32) + (lanef == route[:, 1:2]).astype(F32)
        cnt_ref[...] += jnp.sum(hits, axis=0, keepdims=True) * (i > 0).astype(F32)

        mix = jnp.concatenate([a_ref[0], a_ref[1], a_ref[2], a_ref[3]], axis=1).astype(BF16)
        mix = jnp.concatenate([mix, b_ref[...].astype(BF16)], axis=1)
        x1 = x_ref[...] + jnp.dot(mix, wo_ref[...], preferred_element_type=F32)
        x1_ref[...] = x1
        ms = jnp.mean(x1 * x1, axis=-1, keepdims=True)
        xn = x1 * lax.rsqrt(ms + EPS) * g_ref[...]
        xn_ref[...] = _pack_bf16_pairs(xn)
        xh = xn.astype(BF16)
        xl = (xn - xh.astype(F32)).astype(BF16)
        xcat_scr[slot] = jnp.concatenate([xh, xl, xh], axis=1)

    @pl.when(i < prompt_tiles)
    def _():
        body(xp_ref, ap_ref, bp_ref)

    @pl.when(i >= prompt_tiles)
    def _():
        body(xs_ref, as_ref, bs_ref)


def _out_router(xp, a4p, bp, xs, a4s, bs, wo, gamma, wr, br):
    tp, tsm = xp.shape[0], xs.shape[0]
    tm = 512
    npt, nst = tp // tm, tsm // tm
    nt = npt + nst
    t = tp + tsm
    pmap = lambda i: (jnp.minimum(i, npt - 1), 0)
    smap = lambda i: (jnp.clip(i - npt, 0, nst - 1), 0)
    cur = lambda i: (jnp.minimum(i, nt - 1), 0)
    return pl.pallas_call(
        functools.partial(_out_router_kernel, prompt_tiles=npt, tiles=nt),
        grid=(nt + 1,),
        in_specs=[
            pl.BlockSpec((tm, D_MODEL), pmap),
            pl.BlockSpec((4, tm, LANES), lambda i: (0, jnp.minimum(i, npt - 1), 0)),
            pl.BlockSpec((tm, 512), pmap),
            pl.BlockSpec((tm, D_MODEL), smap),
            pl.BlockSpec((4, tm, LANES), lambda i: (0, jnp.clip(i - npt, 0, nst - 1), 0)),
            pl.BlockSpec((tm, 512), smap),
            pl.BlockSpec((D_MODEL, D_MODEL), lambda i: (0, 0)),
            pl.BlockSpec((1, D_MODEL), lambda i: (0, 0)),
            pl.BlockSpec((3 * D_MODEL, LANES), lambda i: (0, 0)),
            pl.BlockSpec((1, LANES), lambda i: (0, 0)),
        ],
        out_specs=[
            pl.BlockSpec((tm, D_MODEL), cur),
            pl.BlockSpec((tm, D_MODEL // 2), cur),
            pl.BlockSpec((tm, LANES), lambda i: (jnp.maximum(i - 1, 0), 0)),
            pl.BlockSpec((1, LANES), lambda i: (0, 0)),
        ],
        scratch_shapes=[pltpu.VMEM((2, tm, 3 * D_MODEL), BF16)],
        out_shape=[
            jax.ShapeDtypeStruct((t, D_MODEL), F32),
            jax.ShapeDtypeStruct((t, D_MODEL // 2), jnp.int32),
            jax.ShapeDtypeStruct((t, LANES), F32),
            jax.ShapeDtypeStruct((1, LANES), F32),
        ],
        compiler_params=_cparams(("arbitrary",)),
        name="out_router",
    )(xp, a4p, bp, xs, a4s, bs, wo, gamma, wr, br)


def _sc_gather_rows(table, idx):
    b = idx.shape[0]
    d = table.shape[1]
    w = SC_WINDOW
    per_worker = b // SC_WORKERS
    nwin = per_worker // w
    assert per_worker * SC_WORKERS == b and nwin * w == per_worker
    mesh = plsc.VectorSubcoreMesh(core_axis_name="c", subcore_axis_name="s")

    @functools.partial(
        pl.kernel, mesh=mesh,
        out_type=jax.ShapeDtypeStruct((b, d), table.dtype),
        scratch_types=[pltpu.VMEM((nwin, w), jnp.int32), pltpu.VMEM((2, w, d), table.dtype),
                       pltpu.SemaphoreType.DMA((2,)), pltpu.SemaphoreType.DMA((2,))],
        name="sc_gather_rows",
    )
    def gather(table_hbm, idx_hbm, out_hbm, idx_v, rows_v, sem_in, sem_out):
        wid = lax.axis_index("s") * SC_CORES + lax.axis_index("c")
        base = wid * per_worker
        pltpu.sync_copy(idx_hbm.at[wid], idx_v)

        def fetch(j):
            return pltpu.make_async_copy(table_hbm.at[idx_v.at[j]], rows_v.at[j % 2], sem_in.at[j % 2])

        def flush(j):
            return pltpu.make_async_copy(rows_v.at[j % 2], out_hbm.at[pl.ds(base + j * w, w)],
                                         sem_out.at[j % 2])

        fetch(0).start()
        for j in range(nwin):
            fetch(j).wait()
            if j + 1 < nwin:
                if j >= 1:
                    flush(j - 1).wait()
                fetch(j + 1).start()
            flush(j).start()
        for j in range(max(nwin - 2, 0), nwin):
            flush(j).wait()

    return gather(table, idx.reshape(SC_WORKERS, nwin, w))


def _sc_scatter_rows(x, dest2, nrows):
    t, d = x.shape
    w = SC_SCATTER_WINDOW
    per_worker = t // SC_WORKERS
    nwin = per_worker // w
    assert per_worker * SC_WORKERS == t and nwin * w == per_worker
    mesh = plsc.VectorSubcoreMesh(core_axis_name="c", subcore_axis_name="s")

    @functools.partial(
        pl.kernel, mesh=mesh,
        out_type=jax.ShapeDtypeStruct((nrows, d), x.dtype),
        scratch_types=[pltpu.VMEM((TOP_K, nwin, w), jnp.int32), pltpu.VMEM((2, w, d), x.dtype),
                       pltpu.SemaphoreType.DMA((2,)), pltpu.SemaphoreType.DMA((2,))],
        name="sc_scatter_rows",
    )
    def scatter(x_hbm, dest_hbm, out_hbm, idx_v, rows_v, sem_in, sem_out):
        wid = lax.axis_index("s") * SC_CORES + lax.axis_index("c")
        base = wid * per_worker
        for k in range(TOP_K):
            pltpu.sync_copy(dest_hbm.at[k, wid], idx_v.at[k])

        def fetch(j):
            return pltpu.make_async_copy(x_hbm.at[pl.ds(base + j * w, w)], rows_v.at[j % 2], sem_in.at[j % 2])

        def spread(j, k):
            return pltpu.make_async_copy(rows_v.at[j % 2], out_hbm.at[idx_v.at[k, j]], sem_out.at[j % 2])

        fetch(0).start()
        for j in range(nwin):
            fetch(j).wait()
            if j + 1 < nwin:
                if j >= 1:
                    for k in range(TOP_K):
                        spread(j - 1, k).wait()
                fetch(j + 1).start()
            for k in range(TOP_K):
                spread(j, k).start()
        for j in range(max(nwin - 2, 0), nwin):
            for k in range(TOP_K):
                spread(j, k).wait()

    return scatter(x, dest2.reshape(TOP_K, SC_WORKERS, nwin, w))


def _expert_kernel(be_ref, nu_ref, nv_ref, x_ref, wg_ref, wu_ref, wd_ref, o_ref, wg_s, wu_s, wd_s):
    i = pl.program_id(0)
    used = i < nu_ref[0]
    changed = jnp.logical_or(i == 0, be_ref[i] != be_ref[jnp.maximum(i - 1, 0)])

    @pl.when(jnp.logical_and(used, changed))
    def _():
        wg_s[...] = wg_ref[...].astype(BF16)
        wu_s[...] = wu_ref[...].astype(BF16)
        wd_s[...] = wd_ref[...].astype(BF16)

    @pl.when(used)
    def _():
        row = lax.broadcasted_iota(jnp.int32, x_ref.shape, 0)
        x = _unpack_bf16_pairs(jnp.where(row < nv_ref[i], x_ref[...], 0)).astype(BF16)
        gate = jnp.dot(x, wg_s[...], preferred_element_type=F32)
        up = jnp.dot(x, wu_s[...], preferred_element_type=F32)
        h = (gate * jax.nn.sigmoid(gate) * up).astype(BF16)
        o_ref[...] = _pack_bf16_pairs(jnp.dot(h, wd_s[...], preferred_element_type=F32))

    @pl.when(jnp.logical_not(used))
    def _():
        o_ref[...] = jnp.zeros_like(o_ref)


def _experts(blk_e, n_used, nvalid, xb, w_gate, w_up, w_down):
    rows = xb.shape[0]
    nblocks = rows // MOE_ROWS
    grid_spec = pltpu.PrefetchScalarGridSpec(
        num_scalar_prefetch=3,
        grid=(nblocks,),
        in_specs=[
            pl.BlockSpec((MOE_ROWS, D_MODEL // 2), lambda i, be, nu, nv: (i, 0)),
            pl.BlockSpec((None, D_MODEL, D_EXPERT), lambda i, be, nu, nv: (be[i], 0, 0)),
            pl.BlockSpec((None, D_MODEL, D_EXPERT), lambda i, be, nu, nv: (be[i], 0, 0)),
            pl.BlockSpec((None, D_EXPERT, D_MODEL), lambda i, be, nu, nv: (be[i], 0, 0)),
        ],
        out_specs=pl.BlockSpec((MOE_ROWS, D_MODEL // 2), lambda i, be, nu, nv: (i, 0)),
        scratch_shapes=[pltpu.VMEM((D_MODEL, D_EXPERT), BF16), pltpu.VMEM((D_MODEL, D_EXPERT), BF16),
                        pltpu.VMEM((D_EXPERT, D_MODEL), BF16)],
    )
    return pl.pallas_call(
        _expert_kernel,
        grid_spec=grid_spec,
        out_shape=jax.ShapeDtypeStruct((rows, D_MODEL // 2), jnp.int32),
        compiler_params=_cparams(("arbitrary",)),
        name="experts",
    )(blk_e, n_used, nvalid, xb, w_gate, w_up, w_down)


def _combine_kernel(x1_ref, y1_ref, y2_ref, route_ref, g_ref, outp_ref, outs_ref, *, prompt_tiles):
    r = route_ref[...]
    x = (x1_ref[...] + r[:, 2:3] * _unpack_bf16_pairs(y1_ref[...])
         + r[:, 3:4] * _unpack_bf16_pairs(y2_ref[...]))
    ms = jnp.mean(x * x, axis=-1, keepdims=True)
    y = x * lax.rsqrt(ms + EPS) * g_ref[...]
    i = pl.program_id(0)

    @pl.when(i < prompt_tiles)
    def _():
        outp_ref[...] = y

    @pl.when(i >= prompt_tiles)
    def _():
        outs_ref[...] = y


def _combine_norm(x1, ygath, route, gamma, tp):
    t = x1.shape[0]
    tm = 512
    nt, npt = t // tm, tp // tm
    return pl.pallas_call(
        functools.partial(_combine_kernel, prompt_tiles=npt),
        grid=(nt,),
        in_specs=[
            pl.BlockSpec((tm, D_MODEL), lambda i: (i, 0)),
            pl.BlockSpec((tm, D_MODEL // 2), lambda i: (i, 0)),
            pl.BlockSpec((tm, D_MODEL // 2), lambda i: (i + nt, 0)),
            pl.BlockSpec((tm, LANES), lambda i: (i, 0)),
            pl.BlockSpec((1, D_MODEL), lambda i: (0, 0)),
        ],
        out_specs=[
            pl.BlockSpec((tm, D_MODEL), lambda i: (jnp.minimum(i, npt - 1), 0)),
            pl.BlockSpec((tm, D_MODEL), lambda i: (jnp.maximum(i - npt, 0), 0)),
        ],
        out_shape=[jax.ShapeDtypeStruct((tp, D_MODEL), F32), jax.ShapeDtypeStruct((t - tp, D_MODEL), F32)],
        compiler_params=_cparams(("arbitrary",)),
        name="combine_norm",
    )(x1, ygath, ygath, route, gamma)


def _band_index():
    c = (2 * QB - np.arange(2 * QB)) % (2 * QB)
    return c, c <= QB


def _bias_a_prompt(table_a):
    c, valid = _band_index()
    idx = np.stack([_t5_bucket_np(d * np.clip(QB - c, 0, QB)) for d in DILATIONS])
    return jnp.where(valid, jnp.transpose(table_a[idx], (0, 2, 1)) * LOG2E, NEG)


def _bias_b_prompt(table_b):
    c, valid = _band_index()
    valid = valid & (c >= 1)
    h = jnp.where(valid, table_b[_t5_bucket_np(np.clip(QB - c, 0, QB))].T * LOG2E, NEG)
    return jnp.transpose(h.reshape(KV_B, G_B, 2 * QB), (1, 0, 2)).reshape(H_B, 2 * QB)


def _sample_bias(table, span, t, log2_weight):
    cols = span + LANES
    period = cols + LANES
    x = np.arange(period)
    dist = np.where(x >= period - t, span - x + period, span - x)
    extra = log2_weight(dist)
    valid = np.isfinite(extra)
    u = jnp.where(valid, table[_t5_bucket_np(np.maximum(dist, 0))].T * LOG2E
                  + np.where(valid, extra, 0.0).astype(np.float32), NEG)
    rows = jnp.tile(u, (1, t))[:, :t * (period - 1)].reshape(u.shape[0], t, period - 1)[:, :, :cols]
    return rows.reshape(u.shape[0] * t, cols)


def _bias_a_sample(table_a, t):
    def log2_count(dist):
        count = np.zeros(dist.shape, np.int64)
        for w, d in zip(WINDOWS, DILATIONS):
            count += (dist >= 0) & (dist % d == 0) & (dist <= w)
        return np.where(count > 0, np.log2(np.maximum(count, 1)), -np.inf)

    return _sample_bias(table_a, WIN_A, t, log2_count)


def _bias_b_sample(table_b, t):
    return _sample_bias(table_b, WIN_B, t,
                        lambda dist: np.where((dist >= 0) & (dist < WIN_B), 0.0, -np.inf))


def _dest_kernel(route_ref, cnt_ref, tri_ref, dest_ref, meta_ref, run_scr, pst_scr):
    i = pl.program_id(0)
    tm = route_ref.shape[0]
    r = route_ref[...]
    lane = lax.broadcasted_iota(jnp.int32, (tm, LANES), 1)
    lanef = lane.astype(F32)
    oh0 = lanef == r[:, 0:1]
    oh1 = lanef == r[:, 1:2]
    ohf = jnp.concatenate([oh0, oh1], axis=0).astype(F32)

    @pl.when(i == 0)
    def _():
        cnt = jnp.broadcast_to(cnt_ref[...], (LANES, LANES))
        padded = jnp.floor((cnt + (MOE_ROWS - 1)) * (1.0 / MOE_ROWS)) * MOE_ROWS
        lane_e = lax.broadcasted_iota(jnp.int32, (LANES, LANES), 1)
        x = padded
        for sh in (1, 2, 4, 8, 16, 32, 64):
            x = x + jnp.where(lane_e >= sh, pltpu.roll(x, sh, 1), 0.0)
        pst_scr[...] = (x - padded)[0:1]
        run_scr[...] = jnp.zeros_like(run_scr)
        wide = lambda v: jnp.concatenate([v.T, v.T], axis=1)
        cnt_t, bend_t = wide(cnt), wide(x * (1.0 / MOE_ROWS))
        bstart_t = wide((x - padded) * (1.0 / MOE_ROWS))
        blk = lax.broadcasted_iota(jnp.int32, (LANES, 2 * LANES), 1).astype(F32)
        exp = lax.broadcasted_iota(jnp.int32, (LANES, 2 * LANES), 0)
        real = exp < N_EXPERTS
        blk_e = jnp.minimum(jnp.sum(jnp.where(real & (bend_t <= blk), 1.0, 0.0), axis=0, keepdims=True),
                            N_EXPERTS - 1.0)
        mine = exp.astype(F32) == blk_e
        within = blk[0:1] - jnp.sum(jnp.where(mine, bstart_t, 0.0), axis=0, keepdims=True)
        nvalid = jnp.clip(jnp.sum(jnp.where(mine, cnt_t, 0.0), axis=0, keepdims=True) - within * MOE_ROWS,
                          0.0, float(MOE_ROWS))
        n_used = jnp.max(jnp.where(real, bend_t, 0.0), axis=0, keepdims=True)
        meta_ref[...] = jnp.concatenate([blk_e, nvalid, n_used, jnp.zeros((5, 2 * LANES), F32)],
                                        axis=0).astype(jnp.int32)

    csum = jnp.dot(tri_ref[...], ohf.astype(BF16), preferred_element_type=F32)
    val = csum + (run_scr[...] + pst_scr[...] - 1.0)
    d0 = jnp.sum(jnp.where(oh0, val[:tm], 0.0), axis=-1, keepdims=True)
    d1 = jnp.sum(jnp.where(oh1, val[tm:], 0.0), axis=-1, keepdims=True)
    tile = jnp.where(lane == 0, d0, jnp.where(lane == 1, d1, 0.0))
    dest_ref[...] = tile.T[:8].astype(jnp.int32)
    run_scr[...] += jnp.sum(ohf, axis=0, keepdims=True)


def _dispatch(route, cnt):
    t = route.shape[0]
    tm = 512
    tri = (jnp.arange(2 * tm)[:, None] >= jnp.arange(2 * tm)[None, :]).astype(BF16)
    nblocks = -(-t * TOP_K // MOE_ROWS) + N_EXPERTS
    assert nblocks <= 2 * LANES
    dest, meta = pl.pallas_call(
        _dest_kernel,
        grid=(t // tm,),
        in_specs=[pl.BlockSpec((tm, LANES), lambda i: (i, 0)),
                  pl.BlockSpec((1, LANES), lambda i: (0, 0)),
                  pl.BlockSpec((2 * tm, 2 * tm), lambda i: (0, 0))],
        out_specs=[pl.BlockSpec((8, tm), lambda i: (0, i)),
                   pl.BlockSpec((8, 2 * LANES), lambda i: (0, 0))],
        out_shape=[jax.ShapeDtypeStruct((8, t), jnp.int32), jax.ShapeDtypeStruct((8, 2 * LANES), jnp.int32)],
        scratch_shapes=[pltpu.VMEM((1, LANES), F32), pltpu.VMEM((1, LANES), F32)],
        compiler_params=_cparams(("arbitrary",)),
        name="moe_dest",
    )(route, cnt, tri)
    return dest[:TOP_K], meta[0, :nblocks], meta[2, :1], meta[1, :nblocks]


def kernel(x_prompt, x_sample, cache_a_k, cache_a_v, cache_b_k, cache_b_v, rel_bias_table, attn_norm, w_in,
           w_out, attn_sinks, ffn_norm, w_router_group, b_router_group, w_router_expert, b_router_expert,
           w_gate, w_up, w_down, final_norm):
    s = x_prompt.shape[1]
    ns, ts = x_sample.shape[0], x_sample.shape[1]
    table_a = rel_bias_table[:, :H_A]
    table_b = rel_bias_table[:, H_A:]

    w = w_in[0]
    wqa, wka, wva, wqb, wkb, wvb = (w[:, 0:512], w[:, 512:1024], w[:, 1024:1536], w[:, 1536:2048],
                                    w[:, 2048:2176], w[:, 2176:2304])
    wqb = jnp.transpose(wqb.reshape(D_MODEL, KV_B, G_B, HEAD_DIM), (0, 2, 1, 3)).reshape(D_MODEL, 512)
    wp = jnp.concatenate([wka, wva, wqa, wqb, wkb, wvb], axis=1).astype(BF16)
    cscale = jnp.concatenate([jnp.ones((1, 1024), F32), jnp.full((1, 1024), SCALE * LOG2E, F32),
                              jnp.ones((1, 256), F32)], axis=1)
    wo = w_out[0]
    wo_b = jnp.transpose(wo[512:].reshape(KV_B, G_B, HEAD_DIM, D_MODEL), (1, 0, 2, 3)).reshape(512, D_MODEL)
    wo_p = jnp.concatenate([wo[:512], wo_b], axis=0).astype(BF16)
    wr = jnp.concatenate([w_router_group[0],
                          jnp.transpose(w_router_expert[0], (1, 0, 2)).reshape(D_MODEL, N_EXPERTS),
                          jnp.zeros((D_MODEL, LANES - N_GROUPS - N_EXPERTS), F32)], axis=1)
    wr_hi = wr.astype(BF16)
    wr = jnp.concatenate([wr_hi, wr_hi, (wr - wr_hi.astype(F32)).astype(BF16)], axis=0)
    br = jnp.concatenate([b_router_group[0], b_router_expert[0].reshape(N_EXPERTS),
                          jnp.zeros((LANES - N_GROUPS - N_EXPERTS,), F32)]).reshape(1, LANES)
    sinks2 = attn_sinks[0] * LOG2E
    sinks_gk = jnp.transpose(sinks2.reshape(KV_B, G_B), (1, 0)).reshape(H_B)
    sink_rows_p = jnp.repeat(sinks_gk, QB).reshape(G_B, 1, 2 * QB)
    sink_rows_s = jnp.repeat(sinks2, ts).reshape(H_B * ts, 1)
    emat = jnp.tile(jnp.arange(LANES)[:, None] == (jnp.arange(A_WIDTH)[None, :] // HEAD_DIM),
                    (3, 1)).astype(BF16)
    attn_g = attn_norm[0].reshape(1, D_MODEL)
    ffn_g = ffn_norm[0].reshape(1, D_MODEL)

    xp = x_prompt.reshape(s, D_MODEL)
    aperm, qb_p, kvb_p, akv32, bkv32 = _proj_prompt(xp, attn_g, wp, cscale)
    a4 = _attn_a_prompt(aperm, _bias_a_prompt(table_a), emat)
    ob_p = _attn_b_prompt(qb_p, kvb_p, _bias_b_prompt(table_b), sink_rows_p)

    xs = x_sample.reshape(ns * ts, D_MODEL)
    q_s, kv_s = _proj_sample(xs, attn_g, wp, cscale)
    akt = jnp.transpose(cache_a_k[0], (0, 2, 3, 1)).reshape(ns, A_WIDTH, WIN_A)
    avt = jnp.transpose(cache_a_v[0], (0, 2, 3, 1)).reshape(ns, A_WIDTH, WIN_A)
    bkt = jnp.transpose(cache_b_k[0], (0, 2, 3, 1)).reshape(ns, LANES, WIN_B)
    bvt = jnp.transpose(cache_b_v[0], (0, 2, 3, 1)).reshape(ns, LANES, WIN_B)
    oa_s, ob_s = _attn_sample(q_s.reshape(ns, ts, 1024), kv_s.reshape(ns, ts, 1280), akt, avt, bkt, bvt,
                              _bias_a_sample(table_a, ts), _bias_b_sample(table_b, ts), sink_rows_s)
    a4_s = jnp.transpose(oa_s.reshape(ns * ts, 4, LANES), (1, 0, 2))

    x1, xn, route, cnt = _out_router(xp, a4, ob_p, xs, a4_s, ob_s.reshape(ns * ts, 512), wo_p, ffn_g, wr, br)
    dest2, blk_e, n_used, nvalid = _dispatch(route, cnt)
    xb = _sc_scatter_rows(xn, dest2, blk_e.shape[0] * MOE_ROWS)
    yb = _experts(blk_e, n_used, nvalid, xb, w_gate[0], w_up[0], w_down[0])
    y_p, y_s = _combine_norm(x1, _sc_gather_rows(yb, dest2.reshape(-1)), route, final_norm.reshape(1, D_MODEL), s)

    y_prompt = y_p.reshape(1, s, D_MODEL)
    y_sample = y_s.reshape(ns, ts, D_MODEL)
    keep_a, keep_b = min(WIN_A, s), min(WIN_B, s)
    pak = akv32[s - keep_a:, :512].reshape(1, 1, keep_a, H_A, HEAD_DIM)
    pav = akv32[s - keep_a:, 512:].reshape(1, 1, keep_a, H_A, HEAD_DIM)
    pbk = bkv32[s - keep_b:, :128].reshape(1, 1, keep_b, KV_B, HEAD_DIM)
    pbv = bkv32[s - keep_b:, 128:].reshape(1, 1, keep_b, KV_B, HEAD_DIM)
    sak = kv_s[:, 0:512].reshape(1, ns, ts, H_A, HEAD_DIM)
    sav = kv_s[:, 512:1024].reshape(1, ns, ts, H_A, HEAD_DIM)
    sbk = kv_s[:, 1024:1152].reshape(1, ns, ts, KV_B, HEAD_DIM)
    sbv = kv_s[:, 1152:1280].reshape(1, ns, ts, KV_B, HEAD_DIM)
    return (y_prompt, y_sample, pak, pav, pbk, pbv, sak, sav, sbk, sbv)
```

```python
import functools
import math

import jax
import jax.numpy as jnp
import numpy as np
from jax import lax
from jax.experimental import pallas as pl
from jax.experimental.pallas import tpu as pltpu
from jax.experimental.pallas import tpu_sc as plsc

D_MODEL = 1024
HEAD_DIM = 64
H_A = 8
H_B = 8
KV_B = 2
G_B = 4
DILATIONS = (1, 4, 16)
WINDOWS = (128, 512, 2048)
WIN_A = 2048
WIN_B = 128
NUM_BUCKETS = 32
MAX_DISTANCE = 2048
N_GROUPS = 4
EXPERTS_PER_GROUP = 8
N_EXPERTS = 32
TOP_K = 2
D_EXPERT = 512
EPS = 1e-5
SCALE = HEAD_DIM ** -0.5
PAST_LEN = 16384

LANES = 128
SPAN = 2048
QB = 128
NCHUNK = 9
A_WIDTH = H_A * HEAD_DIM
MOE_ROWS = 512
SC_CORES = 2
SC_SUBCORES = 16
SC_WORKERS = SC_CORES * SC_SUBCORES
SC_WINDOW = 64
SC_SCATTER_WINDOW = 32
NEG = -1e30
LOG2E = math.log2(math.e)
B_STEP = 512
VMEM_LIMIT = 56 * 1024 * 1024
OUT_ROUTER_VMEM = 60 * 1024 * 1024

F32 = jnp.float32
BF16 = jnp.bfloat16


def _t5_bucket_np(dist):
    dist = np.asarray(dist, np.int64)
    max_exact = NUM_BUCKETS // 2
    d = np.maximum(dist, 1).astype(np.float32)
    ratio = np.log(d / np.float32(max_exact)) / np.float32(math.log(MAX_DISTANCE / max_exact))
    large = max_exact + (ratio * np.float32(NUM_BUCKETS - max_exact)).astype(np.int32)
    large = np.minimum(large, NUM_BUCKETS - 1)
    return np.where(dist < max_exact, dist, large).astype(np.int32)


def _cparams(sem, vmem=VMEM_LIMIT):
    return pltpu.CompilerParams(dimension_semantics=sem, vmem_limit_bytes=vmem)


def _proj_prompt_kernel(x_ref, g_ref, w_ref, cs_ref, aperm_ref, qb_ref, kvb_ref, akv_ref, bkv_ref,
                        h_scr, p_scr):
    n = pl.program_id(1)

    @pl.when(n == 0)
    def _():
        x = x_ref[...]
        ms = jnp.mean(x * x, axis=-1, keepdims=True)
        h_scr[...] = (x * lax.rsqrt(ms + EPS) * g_ref[...]).astype(BF16)

    p = jnp.dot(h_scr[...], w_ref[...], preferred_element_type=F32) * cs_ref[...]

    @pl.when(n < 6)
    def _():
        aperm_ref[0] = p.astype(BF16)
        p_scr[0, 0] = p[:, :LANES]
        p_scr[0, 1] = p[:, LANES:]
        quarter = SPAN // 4
        for r in range(4):
            lo = p_scr[0, 0, pl.ds(r, quarter, stride=4), :]
            hi = p_scr[0, 1, pl.ds(r, quarter, stride=4), :]
            p_scr[1, 0, r * quarter:(r + 1) * quarter, :] = lo
            p_scr[1, 1, r * quarter:(r + 1) * quarter, :] = hi
            aperm_ref[1, r * quarter:(r + 1) * quarter, :] = jnp.concatenate([lo, hi], axis=1).astype(BF16)
        for r16 in range(16):
            start = (r16 % 4) * quarter + r16 // 4
            t = jnp.concatenate([p_scr[1, 0, pl.ds(start, QB, stride=4), :],
                                 p_scr[1, 1, pl.ds(start, QB, stride=4), :]], axis=1)
            aperm_ref[2, r16 * QB:(r16 + 1) * QB, :] = t.astype(BF16)

    @pl.when(n < 4)
    def _():
        akv_ref[...] = p

    @pl.when(jnp.logical_or(n == 6, n == 7))
    def _():
        qb_ref[...] = p.astype(BF16)

    @pl.when(n == 8)
    def _():
        kvb_ref[...] = p.astype(BF16)
        bkv_ref[...] = p


def _proj_prompt(x, gamma, w, cscale):
    s = x.shape[0]
    nspan = s // SPAN
    return pl.pallas_call(
        _proj_prompt_kernel,
        grid=(nspan, NCHUNK),
        in_specs=[
            pl.BlockSpec((SPAN, D_MODEL), lambda b, n: (b, 0)),
            pl.BlockSpec((1, D_MODEL), lambda b, n: (0, 0)),
            pl.BlockSpec((D_MODEL, 256), lambda b, n: (0, n)),
            pl.BlockSpec((1, 256), lambda b, n: (0, n)),
        ],
        out_specs=[
            pl.BlockSpec((3, SPAN, 256), lambda b, n: (0, b, jnp.minimum(n, 5))),
            pl.BlockSpec((SPAN, 256), lambda b, n: (b, jnp.clip(n - 6, 0, 1))),
            pl.BlockSpec((SPAN, 256), lambda b, n: (b, 0)),
            pl.BlockSpec((SPAN, 256), lambda b, n: (b, jnp.minimum(n, 3))),
            pl.BlockSpec((SPAN, 256), lambda b, n: (b, 0)),
        ],
        out_shape=[
            jax.ShapeDtypeStruct((3, s, 3 * A_WIDTH), BF16),
            jax.ShapeDtypeStruct((s, 512), BF16),
            jax.ShapeDtypeStruct((s, 256), BF16),
            jax.ShapeDtypeStruct((s, 1024), F32),
            jax.ShapeDtypeStruct((s, 256), F32),
        ],
        scratch_shapes=[pltpu.VMEM((SPAN, D_MODEL), BF16), pltpu.VMEM((2, 2, SPAN, LANES), F32)],
        compiler_params=_cparams(("arbitrary", "arbitrary")),
        name="proj_prompt",
    )(x, gamma, w, cscale)


def _proj_sample_kernel(x_ref, g_ref, w_ref, cs_ref, q_ref, kv_ref):
    x = x_ref[...]
    ms = jnp.mean(x * x, axis=-1, keepdims=True)
    h = (x * lax.rsqrt(ms + EPS) * g_ref[...]).astype(BF16)
    p = jnp.dot(h, w_ref[...], preferred_element_type=F32) * cs_ref[...]
    kv_ref[:, :1024] = p[:, :1024]
    kv_ref[:, 1024:] = p[:, 2048:]
    q_ref[...] = p[:, 1024:2048]


def _proj_sample(x, gamma, w, cscale):
    t = x.shape[0]
    tm = 512
    return pl.pallas_call(
        _proj_sample_kernel,
        grid=(t // tm,),
        in_specs=[
            pl.BlockSpec((tm, D_MODEL), lambda i: (i, 0)),
            pl.BlockSpec((1, D_MODEL), lambda i: (0, 0)),
            pl.BlockSpec((D_MODEL, 2304), lambda i: (0, 0)),
            pl.BlockSpec((1, 2304), lambda i: (0, 0)),
        ],
        out_specs=[
            pl.BlockSpec((tm, 1024), lambda i: (i, 0)),
            pl.BlockSpec((tm, 1280), lambda i: (i, 0)),
        ],
        out_shape=[
            jax.ShapeDtypeStruct((t, 1024), F32),
            jax.ShapeDtypeStruct((t, 1280), F32),
        ],
        compiler_params=_cparams(("arbitrary",)),
        name="proj_sample",
    )(x, gamma, w, cscale)


def _spread_heads(w, e3_ref):
    hi = w.astype(BF16)
    r1 = w - hi.astype(F32)
    mid = r1.astype(BF16)
    low = (r1 - mid.astype(F32)).astype(BF16)
    return jnp.dot(jnp.concatenate([hi, mid, low], axis=1), e3_ref[...], preferred_element_type=F32)


def _pair_tile(q2, kk, vv, bias_t, lo, sink=None):
    zero = jnp.zeros_like(q2)
    qq = jnp.concatenate([jnp.where(lo, q2, zero), jnp.where(lo, zero, q2)], axis=0)
    st = lax.dot_general(kk, qq, (((1,), (1,)), ((), ())), preferred_element_type=F32)
    st = st + bias_t
    m = jnp.max(st, axis=0, keepdims=True)
    if sink is not None:
        m = jnp.maximum(m, sink)
    p = jnp.exp2(st - m)
    den = jnp.sum(p, axis=0, keepdims=True)
    if sink is not None:
        den = den + jnp.exp2(sink - m)
    pn = (p * (1.0 / den)).astype(BF16)
    o = lax.dot_general(pn, vv, (((0,), (0,)), ((), ())), preferred_element_type=F32)
    return jnp.where(lo, o[:QB], o[QB:]), m + jnp.log2(den)


def _fill_band_tiles(h_ref, bias_scr):
    nk = 2 * QB
    prev = lax.broadcasted_iota(jnp.int32, (nk, nk), 0) < QB
    for pair in range(h_ref.shape[0] // 2):
        halves = []
        for hh in range(2):
            row = h_ref[2 * pair + hh:2 * pair + hh + 1, :]
            band = pltpu.roll(jnp.broadcast_to(row, (nk, nk)), 0, 1, stride=1, stride_axis=0)
            halves.append(band[:, :QB])
        tile = jnp.concatenate(halves, axis=1)
        bias_scr[0, pair] = tile
        bias_scr[1, pair] = jnp.where(prev, NEG, tile)


def _attn_a_kernel(q_ref, kvc_ref, kvp_ref, h_ref, e_ref, out_ref, o_scr, st_scr, bias_scr):
    b = pl.program_id(0)
    g = pl.program_id(1)
    nblk = jnp.where(g == 0, 16, jnp.where(g == 1, 4, 1))
    lane = lax.broadcasted_iota(jnp.int32, (QB, LANES), 1)
    lo = lane < HEAD_DIM

    @pl.when(b == 0)
    def _():
        _fill_band_tiles(h_ref, bias_scr.at[g])

    bias_ref = bias_scr.at[g]

    for cb in range(SPAN // QB):
        first = lax.rem(jnp.int32(cb), nblk) == 0
        rows = slice(cb * QB, (cb + 1) * QB)
        prow_c = max(cb - 1, 0) * QB
        prow_p = pl.multiple_of(jnp.where(first, cb + nblk - 1, 0) * QB, QB)
        variant = jnp.logical_and(first, b == 0).astype(jnp.int32)
        stats = []
        for hp in range(4):
            ks = slice(hp * LANES, (hp + 1) * LANES)
            vs = slice(A_WIDTH + hp * LANES, A_WIDTH + (hp + 1) * LANES)
            kp = jnp.where(first, kvp_ref[pl.ds(prow_p, QB), ks], kvc_ref[prow_c:prow_c + QB, ks])
            vp = jnp.where(first, kvp_ref[pl.ds(prow_p, QB), vs], kvc_ref[prow_c:prow_c + QB, vs])
            kk = jnp.concatenate([kp, kvc_ref[rows, ks]], axis=0)
            vv = jnp.concatenate([vp, kvc_ref[rows, vs]], axis=0)
            o, lse = _pair_tile(q_ref[rows, ks], kk, vv, bias_ref[variant, hp], lo)
            o_scr[g, hp, rows, :] = o
            stats += [lse[:, :QB], lse[:, QB:]]
        sm = jnp.concatenate(stats + [jnp.zeros((LANES - H_A, QB), F32)], axis=0)
        st_scr[g, rows, :] = sm.T

    @pl.when(g == 2)
    def _():
        def merge(c, carry):
            r2 = lax.rem(c, 4) * (SPAN // 4) + c // 4
            r3 = pl.multiple_of(c * QB, QB)
            l1 = st_scr[0, pl.ds(c, QB, stride=16), :]
            l2 = st_scr[1, pl.ds(r2, QB, stride=4), :]
            l3 = st_scr[2, pl.ds(r3, QB), :]
            mx = jnp.maximum(jnp.maximum(l1, l2), l3)
            w1 = jnp.exp2(l1 - mx)
            w2 = jnp.exp2(l2 - mx)
            w3 = jnp.exp2(l3 - mx)
            tot = w1 + w2 + w3
            a1 = _spread_heads(w1 / tot, e_ref)
            a2 = _spread_heads(w2 / tot, e_ref)
            a3 = _spread_heads(w3 / tot, e_ref)
            for hp in range(4):
                sl = slice(hp * LANES, (hp + 1) * LANES)
                o1 = o_scr[0, hp, pl.ds(c, QB, stride=16), :]
                o2 = o_scr[1, hp, pl.ds(r2, QB, stride=4), :]
                o3 = o_scr[2, hp, pl.ds(r3, QB), :]
                out_ref[hp, pl.ds(c, QB, stride=16), :] = a1[:, sl] * o1 + a2[:, sl] * o2 + a3[:, sl] * o3
            return carry

        lax.fori_loop(0, 16, merge, 0, unroll=4)


def _attn_a_prompt(aperm, bias_a, emat):
    s = aperm.shape[1]
    nspan = s // SPAN
    return pl.pallas_call(
        _attn_a_kernel,
        grid=(nspan, 3),
        in_specs=[
            pl.BlockSpec((None, SPAN, A_WIDTH), lambda b, g: (g, b, 2)),
            pl.BlockSpec((None, SPAN, 2 * A_WIDTH), lambda b, g: (g, b, 0)),
            pl.BlockSpec((None, SPAN, 2 * A_WIDTH), lambda b, g: (g, jnp.maximum(b - 1, 0), 0)),
            pl.BlockSpec((None, H_A, 2 * QB), lambda b, g: (g, 0, 0)),
            pl.BlockSpec((3 * LANES, A_WIDTH), lambda b, g: (0, 0)),
        ],
        out_specs=pl.BlockSpec((4, SPAN, LANES), lambda b, g: (0, b, 0)),
        out_shape=jax.ShapeDtypeStruct((4, s, LANES), F32),
        scratch_shapes=[pltpu.VMEM((3, 4, SPAN, LANES), F32), pltpu.VMEM((3, SPAN, LANES), F32),
                        pltpu.VMEM((3, 2, 4, 2 * QB, 2 * QB), F32)],
        compiler_params=_cparams(("arbitrary", "arbitrary")),
        name="attn_a_prompt",
    )(aperm, aperm, aperm, bias_a, emat)


def _attn_b_kernel(q_ref, kvc_ref, kvp_ref, h_ref, sink_ref, out_ref, bias_ref):
    i = pl.program_id(0)
    lane = lax.broadcasted_iota(jnp.int32, (QB, LANES), 1)
    lo = lane < HEAD_DIM

    @pl.when(i == 0)
    def _():
        _fill_band_tiles(h_ref, bias_ref)

    variant = (i == 0).astype(jnp.int32)
    for j in range(B_STEP // QB):
        rows = slice(j * QB, (j + 1) * QB)
        if j == 0:
            kp, vp = kvp_ref[:, :LANES], kvp_ref[:, LANES:]
        else:
            kp, vp = kvc_ref[(j - 1) * QB:j * QB, :LANES], kvc_ref[(j - 1) * QB:j * QB, LANES:]
        kk = jnp.concatenate([kp, kvc_ref[rows, :LANES]], axis=0)
        vv = jnp.concatenate([vp, kvc_ref[rows, LANES:]], axis=0)
        for g in range(G_B):
            bias_t = bias_ref[variant, g] if j == 0 else bias_ref[0, g]
            o, _ = _pair_tile(q_ref[rows, g * LANES:(g + 1) * LANES], kk, vv, bias_t, lo, sink=sink_ref[g])
            out_ref[rows, g * LANES:(g + 1) * LANES] = o.astype(BF16)


def _attn_b_prompt(qb, kvb, bias_b, sink_rows):
    s = qb.shape[0]
    per = B_STEP // QB
    return pl.pallas_call(
        _attn_b_kernel,
        grid=(s // B_STEP,),
        in_specs=[
            pl.BlockSpec((B_STEP, 512), lambda i: (i, 0)),
            pl.BlockSpec((B_STEP, 256), lambda i: (i, 0)),
            pl.BlockSpec((QB, 256), lambda i: (jnp.maximum(i * per - 1, 0), 0)),
            pl.BlockSpec((H_B, 2 * QB), lambda i: (0, 0)),
            pl.BlockSpec((G_B, 1, 2 * QB), lambda i: (0, 0, 0)),
        ],
        out_specs=pl.BlockSpec((B_STEP, 512), lambda i: (i, 0)),
        out_shape=jax.ShapeDtypeStruct((s, 512), BF16),
        scratch_shapes=[pltpu.VMEM((2, G_B, 2 * QB, 2 * QB), F32)],
        compiler_params=_cparams(("arbitrary",)),
        name="attn_b_prompt",
    )(qb, kvb, kvb, bias_b, sink_rows)


def _sample_attention(q, kvn, akt, avt, bkt, bvt, cba, cbb, sink):
    t = q.shape[0]
    kvn_p = jnp.concatenate([kvn, jnp.zeros((LANES - t, kvn.shape[1]), F32)], axis=0).astype(BF16)
    lane_a = lax.broadcasted_iota(jnp.int32, (t, A_WIDTH), 1) // HEAD_DIM

    qa = q[:, :A_WIDTH]
    qbd = jnp.concatenate([jnp.where(lane_a == h, qa, 0.0) for h in range(H_A)], axis=0).astype(BF16)
    s_c = jnp.dot(qbd, akt.astype(BF16), preferred_element_type=F32)
    s_n = lax.dot_general(qbd, kvn_p[:, :A_WIDTH], (((1,), (1,)), ((), ())), preferred_element_type=F32)
    s = jnp.concatenate([s_c, s_n], axis=1) + cba
    m = jnp.max(s, axis=-1, keepdims=True)
    p = jnp.exp2(s - m)
    l = jnp.sum(p, axis=-1, keepdims=True)
    o_n = jnp.dot(p[:, WIN_A:].astype(BF16), kvn_p[:, A_WIDTH:2 * A_WIDTH], preferred_element_type=F32)
    pc = jnp.concatenate([p[:, :WIN_A], jnp.zeros((LANES - H_A * t, WIN_A), F32)], axis=0).astype(BF16)
    o_t = lax.dot_general(avt.astype(BF16), pc, (((1,), (1,)), ((), ())),
                          preferred_element_type=F32)
    o_all = o_t.T[:H_A * t] + o_n
    o_sel = jnp.zeros((t, A_WIDTH), F32)
    l_b = jnp.ones((t, A_WIDTH), F32)
    for h in range(H_A):
        sel = lane_a == h
        o_sel = jnp.where(sel, o_all[h * t:(h + 1) * t], o_sel)
        l_b = jnp.where(sel, l[h * t:(h + 1) * t], l_b)
    oa = o_sel / l_b

    lane_b = lax.broadcasted_iota(jnp.int32, (G_B * t, LANES), 1)
    lo = lane_b < HEAD_DIM
    qb2 = jnp.concatenate([q[:, A_WIDTH + g * LANES:A_WIDTH + (g + 1) * LANES] for g in range(G_B)], axis=0)
    qm = jnp.concatenate([jnp.where(lo, qb2, 0.0), jnp.where(lo, 0.0, qb2)], axis=0).astype(BF16)
    kb_n = kvn_p[:, 2 * A_WIDTH:2 * A_WIDTH + LANES]
    vb_n = kvn_p[:, 2 * A_WIDTH + LANES:]
    sb_c = jnp.dot(qm, bkt.astype(BF16), preferred_element_type=F32)
    sb_n = lax.dot_general(qm, kb_n, (((1,), (1,)), ((), ())), preferred_element_type=F32)
    sb = jnp.concatenate([sb_c, sb_n], axis=1) + cbb
    mb = jnp.maximum(jnp.max(sb, axis=-1, keepdims=True), sink)
    pbb = jnp.exp2(sb - mb)
    den = jnp.sum(pbb, axis=-1, keepdims=True) + jnp.exp2(sink - mb)
    pbb = pbb.astype(BF16)
    ob = lax.dot_general(pbb[:, :WIN_B], bvt.astype(BF16), (((1,), (1,)), ((), ())),
                         preferred_element_type=F32)
    ob = (ob + jnp.dot(pbb[:, WIN_B:], vb_n, preferred_element_type=F32)) / den
    half = G_B * t
    lo8 = lo[:t]
    ob = jnp.concatenate([jnp.where(lo8, ob[g * t:(g + 1) * t], ob[half + g * t:half + (g + 1) * t])
                          for g in range(G_B)], axis=1)
    return oa, ob


def _route(logits):
    lane = lax.broadcasted_iota(jnp.int32, logits.shape, 1).astype(F32)
    big = jnp.float32(1 << 20)
    ninf = jnp.float32(-jnp.inf)
    gmask = lane < N_GROUPS
    lg = jnp.where(gmask, logits, ninf)
    gmax = jnp.max(lg, axis=-1, keepdims=True)
    grp = jnp.min(jnp.where(lg == gmax, lane, big), axis=-1, keepdims=True)
    pg_top = 1.0 / jnp.sum(jnp.exp(lg - gmax), axis=-1, keepdims=True)
    e0 = N_GROUPS + grp * EXPERTS_PER_GROUP
    emask = jnp.logical_and(lane >= e0, lane < e0 + EXPERTS_PER_GROUP)
    le = jnp.where(emask, logits, ninf)
    emax = jnp.max(le, axis=-1, keepdims=True)
    esum = jnp.sum(jnp.exp(le - emax), axis=-1, keepdims=True)
    i1 = jnp.min(jnp.where(le == emax, lane, big), axis=-1, keepdims=True)
    le2 = jnp.where(lane == i1, ninf, le)
    e2max = jnp.max(le2, axis=-1, keepdims=True)
    i2 = jnp.min(jnp.where(le2 == e2max, lane, big), axis=-1, keepdims=True)
    p1 = 1.0 / esum
    p2 = jnp.exp(e2max - emax) / esum
    g1 = pg_top * p1 / (p1 + p2)
    g2 = pg_top * p2 / (p1 + p2)
    out = jnp.where(lane == 0, i1 - N_GROUPS, 0.0)
    out = jnp.where(lane == 1, i2 - N_GROUPS, out)
    out = jnp.where(lane == 2, g1, out)
    out = jnp.where(lane == 3, g2, out)
    return out


def _pack_bf16_pairs(x):
    half = x.shape[1] // 2

    def rne(v):
        bits = lax.bitcast_convert_type(v, jnp.int32)
        return bits + 0x7FFF + (lax.shift_right_logical(bits, 16) & 1)

    lo = lax.shift_right_logical(rne(x[:, :half]), 16)
    hi = rne(x[:, half:]) & jnp.int32(-65536)
    return lo | hi


def _unpack_bf16_pairs(w):
    lo = lax.bitcast_convert_type(lax.shift_left(w, 16), F32)
    hi = lax.bitcast_convert_type(w & jnp.int32(-65536), F32)
    return jnp.concatenate([lo, hi], axis=1)


def _out_router_kernel(xp_ref, ap_ref, bp_ref, xs_ref, q_ref, kvn_ref, akt_hbm, avt_hbm, bkt_ref, bvt_ref,
                       cba_ref, cbb_ref, sink_ref, wo_ref, g_ref, wr_ref, br_ref,
                       x1_ref, xn_ref, route_ref, cnt_ref,
                       xcat_scr, mix_scr, kbuf, vbuf, sem, *, prompt_tiles, decode_tiles, seqs_per_step):
    i = pl.program_id(0)
    seqs = prompt_tiles * seqs_per_step

    def cache_copies(n, slot):
        return (pltpu.make_async_copy(akt_hbm.at[n], kbuf.at[slot], sem.at[0, slot]),
                pltpu.make_async_copy(avt_hbm.at[n], vbuf.at[slot], sem.at[1, slot]))

    @pl.when(i == 0)
    def _():
        cnt_ref[...] = jnp.zeros_like(cnt_ref)
        xcat_scr[...] = jnp.zeros_like(xcat_scr)
        for c in cache_copies(0, 0):
            c.start()

    @pl.when(i == prompt_tiles)
    def _():
        for c in cache_copies(seqs - 1, 0):
            c.wait()

    pslot = lax.rem(i, 2)

    def route_previous():
        logits = jnp.dot(xcat_scr[1 - pslot], wr_ref[...], preferred_element_type=F32)
        route = _route(logits + br_ref[...])
        route_ref[...] = route
        lanef = lax.broadcasted_iota(jnp.int32, route.shape, 1).astype(F32)
        hits = (lanef == route[:, 0:1]).astype(F32) + (lanef == route[:, 1:2]).astype(F32)
        cnt_ref[...] += jnp.sum(hits, axis=0, keepdims=True) * (i > 0).astype(F32)

    def project(x_ref, mix):
        x1 = x_ref[...] + jnp.dot(mix, wo_ref[...], preferred_element_type=F32)
        x1_ref[...] = x1
        ms = jnp.mean(x1 * x1, axis=-1, keepdims=True)
        xn = x1 * lax.rsqrt(ms + EPS) * g_ref[...]
        xn_ref[...] = _pack_bf16_pairs(xn)
        xh = xn.astype(BF16)
        xl = (xn - xh.astype(F32)).astype(BF16)
        xcat_scr[pslot] = jnp.concatenate([xh, xl, xh], axis=1)

    @pl.when(i < prompt_tiles)
    def _():
        route_previous()
        mix = jnp.concatenate([ap_ref[0], ap_ref[1], ap_ref[2], ap_ref[3]], axis=1).astype(BF16)
        project(xp_ref, jnp.concatenate([mix, bp_ref[...]], axis=1))
        t = q_ref.shape[1]
        for s in range(seqs_per_step):
            n = i * seqs_per_step + s
            slot = s % 2
            for c in cache_copies(n, slot):
                c.wait()
            for c in cache_copies(jnp.minimum(n + 1, seqs - 1), 1 - slot):
                c.start()
            oa, ob = _sample_attention(q_ref[s], kvn_ref[s], kbuf[slot], vbuf[slot], bkt_ref[s], bvt_ref[s],
                                       cba_ref[...], cbb_ref[...], sink_ref[...])
            row = pl.multiple_of(n * t, t)
            mix_scr[pl.ds(row, t), :A_WIDTH] = oa
            mix_scr[pl.ds(row, t), A_WIDTH:] = ob

    @pl.when(i >= prompt_tiles)
    def _():
        route_previous()
        tm = xs_ref.shape[0]
        row = pl.multiple_of(jnp.clip(i - prompt_tiles, 0, decode_tiles - 1) * tm, tm)
        project(xs_ref, mix_scr[pl.ds(row, tm), :].astype(BF16))


def _out_router(xp, a4p, bp, xs, q3, kvn3, akt, avt, bkt, bvt, cbias_a, cbias_b, sink_rows, wo, gamma, wr, br):
    tp, tsm = xp.shape[0], xs.shape[0]
    ns, ts = q3.shape[0], q3.shape[1]
    tm = 512
    npt, nst = tp // tm, tsm // tm
    nt = npt + nst
    t = tp + tsm
    sps = ns // npt
    assert sps * npt == ns and sps % 2 == 0 and ns * ts == tsm
    pmap = lambda i: (jnp.minimum(i, npt - 1), 0)
    pmap3 = lambda i: (jnp.minimum(i, npt - 1), 0, 0)
    smap = lambda i: (jnp.clip(i - npt, 0, nst - 1), 0)
    cur = lambda i: (jnp.minimum(i, nt - 1), 0)
    const = lambda i: (0, 0)
    return pl.pallas_call(
        functools.partial(_out_router_kernel, prompt_tiles=npt, decode_tiles=nst, seqs_per_step=sps),
        grid=(nt + 1,),
        in_specs=[
            pl.BlockSpec((tm, D_MODEL), pmap),
            pl.BlockSpec((4, tm, LANES), lambda i: (0, jnp.minimum(i, npt - 1), 0)),
            pl.BlockSpec((tm, 512), pmap),
            pl.BlockSpec((tm, D_MODEL), smap),
            pl.BlockSpec((sps, ts, 1024), pmap3),
            pl.BlockSpec((sps, ts, 1280), pmap3),
            pl.BlockSpec(memory_space=pl.ANY),
            pl.BlockSpec(memory_space=pl.ANY),
            pl.BlockSpec((sps, LANES, WIN_B), pmap3),
            pl.BlockSpec((sps, LANES, WIN_B), pmap3),
            pl.BlockSpec((H_A * ts, WIN_A + LANES), const),
            pl.BlockSpec((H_B * ts, WIN_B + LANES), const),
            pl.BlockSpec((H_B * ts, 1), const),
            pl.BlockSpec((D_MODEL, D_MODEL), const),
            pl.BlockSpec((1, D_MODEL), const),
            pl.BlockSpec((3 * D_MODEL, LANES), const),
            pl.BlockSpec((1, LANES), const),
        ],
        out_specs=[
            pl.BlockSpec((tm, D_MODEL), cur),
            pl.BlockSpec((tm, D_MODEL // 2), cur),
            pl.BlockSpec((tm, LANES), lambda i: (jnp.maximum(i - 1, 0), 0)),
            pl.BlockSpec((1, LANES), const),
        ],
        scratch_shapes=[pltpu.VMEM((2, tm, 3 * D_MODEL), BF16), pltpu.VMEM((tsm, D_MODEL), F32),
                        pltpu.VMEM((2, A_WIDTH, WIN_A), F32), pltpu.VMEM((2, A_WIDTH, WIN_A), F32),
                        pltpu.SemaphoreType.DMA((2, 2))],
        out_shape=[
            jax.ShapeDtypeStruct((t, D_MODEL), F32),
            jax.ShapeDtypeStruct((t, D_MODEL // 2), jnp.int32),
            jax.ShapeDtypeStruct((t, LANES), F32),
            jax.ShapeDtypeStruct((1, LANES), F32),
        ],
        compiler_params=_cparams(("arbitrary",), vmem=OUT_ROUTER_VMEM),
        name="out_router",
    )(xp, a4p, bp, xs, q3, kvn3, akt, avt, bkt, bvt, cbias_a, cbias_b, sink_rows, wo, gamma, wr, br)


def _sc_gather_rows(table, idx):
    b = idx.shape[0]
    d = table.shape[1]
    w = SC_WINDOW
    per_worker = b // SC_WORKERS
    nwin = per_worker // w
    assert per_worker * SC_WORKERS == b and nwin * w == per_worker
    mesh = plsc.VectorSubcoreMesh(core_axis_name="c", subcore_axis_name="s")

    @functools.partial(
        pl.kernel, mesh=mesh,
        out_type=jax.ShapeDtypeStruct((b, d), table.dtype),
        scratch_types=[pltpu.VMEM((nwin, w), jnp.int32), pltpu.VMEM((2, w, d), table.dtype),
                       pltpu.SemaphoreType.DMA((2,)), pltpu.SemaphoreType.DMA((2,))],
        name="sc_gather_rows",
    )
    def gather(table_hbm, idx_hbm, out_hbm, idx_v, rows_v, sem_in, sem_out):
        wid = lax.axis_index("s") * SC_CORES + lax.axis_index("c")
        base = wid * per_worker
        pltpu.sync_copy(idx_hbm.at[wid], idx_v)

        def fetch(j):
            return pltpu.make_async_copy(table_hbm.at[idx_v.at[j]], rows_v.at[j % 2], sem_in.at[j % 2])

        def flush(j):
            return pltpu.make_async_copy(rows_v.at[j % 2], out_hbm.at[pl.ds(base + j * w, w)],
                                         sem_out.at[j % 2])

        fetch(0).start()
        for j in range(nwin):
            fetch(j).wait()
            if j + 1 < nwin:
                if j >= 1:
                    flush(j - 1).wait()
                fetch(j + 1).start()
            flush(j).start()
        for j in range(max(nwin - 2, 0), nwin):
            flush(j).wait()

    return gather(table, idx.reshape(SC_WORKERS, nwin, w))


def _sc_scatter_rows(x, dest2, nrows):
    t, d = x.shape
    w = SC_SCATTER_WINDOW
    per_worker = t // SC_WORKERS
    nwin = per_worker // w
    assert per_worker * SC_WORKERS == t and nwin * w == per_worker
    mesh = plsc.VectorSubcoreMesh(core_axis_name="c", subcore_axis_name="s")

    @functools.partial(
        pl.kernel, mesh=mesh,
        out_type=jax.ShapeDtypeStruct((nrows, d), x.dtype),
        scratch_types=[pltpu.VMEM((TOP_K, nwin, w), jnp.int32), pltpu.VMEM((2, w, d), x.dtype),
                       pltpu.SemaphoreType.DMA((2,)), pltpu.SemaphoreType.DMA((2,))],
        name="sc_scatter_rows",
    )
    def scatter(x_hbm, dest_hbm, out_hbm, idx_v, rows_v, sem_in, sem_out):
        wid = lax.axis_index("s") * SC_CORES + lax.axis_index("c")
        base = wid * per_worker
        for k in range(TOP_K):
            pltpu.sync_copy(dest_hbm.at[k, wid], idx_v.at[k])

        def fetch(j):
            return pltpu.make_async_copy(x_hbm.at[pl.ds(base + j * w, w)], rows_v.at[j % 2], sem_in.at[j % 2])

        def spread(j, k):
            return pltpu.make_async_copy(rows_v.at[j % 2], out_hbm.at[idx_v.at[k, j]], sem_out.at[j % 2])

        fetch(0).start()
        for j in range(nwin):
            fetch(j).wait()
            if j + 1 < nwin:
                if j >= 1:
                    for k in range(TOP_K):
                        spread(j - 1, k).wait()
                fetch(j + 1).start()
            for k in range(TOP_K):
                spread(j, k).start()
        for j in range(max(nwin - 2, 0), nwin):
            for k in range(TOP_K):
                spread(j, k).wait()

    return scatter(x, dest2.reshape(TOP_K, SC_WORKERS, nwin, w))


def _expert_kernel(be_ref, nu_ref, nv_ref, x_ref, wg_ref, wu_ref, wd_ref, o_ref, wg_s, wu_s, wd_s):
    i = pl.program_id(0)
    used = i < nu_ref[0]
    changed = jnp.logical_or(i == 0, be_ref[i] != be_ref[jnp.maximum(i - 1, 0)])

    @pl.when(jnp.logical_and(used, changed))
    def _():
        wg_s[...] = wg_ref[...].astype(BF16)
        wu_s[...] = wu_ref[...].astype(BF16)
        wd_s[...] = wd_ref[...].astype(BF16)

    @pl.when(used)
    def _():
        row = lax.broadcasted_iota(jnp.int32, x_ref.shape, 0)
        x = _unpack_bf16_pairs(jnp.where(row < nv_ref[i], x_ref[...], 0)).astype(BF16)
        gate = jnp.dot(x, wg_s[...], preferred_element_type=F32)
        up = jnp.dot(x, wu_s[...], preferred_element_type=F32)
        h = (gate * jax.nn.sigmoid(gate) * up).astype(BF16)
        o_ref[...] = _pack_bf16_pairs(jnp.dot(h, wd_s[...], preferred_element_type=F32))

    @pl.when(jnp.logical_not(used))
    def _():
        o_ref[...] = jnp.zeros_like(o_ref)


def _experts(blk_e, n_used, nvalid, xb, w_gate, w_up, w_down):
    rows = xb.shape[0]
    nblocks = rows // MOE_ROWS
    grid_spec = pltpu.PrefetchScalarGridSpec(
        num_scalar_prefetch=3,
        grid=(nblocks,),
        in_specs=[
            pl.BlockSpec((MOE_ROWS, D_MODEL // 2), lambda i, be, nu, nv: (i, 0)),
            pl.BlockSpec((None, D_MODEL, D_EXPERT), lambda i, be, nu, nv: (be[i], 0, 0)),
            pl.BlockSpec((None, D_MODEL, D_EXPERT), lambda i, be, nu, nv: (be[i], 0, 0)),
            pl.BlockSpec((None, D_EXPERT, D_MODEL), lambda i, be, nu, nv: (be[i], 0, 0)),
        ],
        out_specs=pl.BlockSpec((MOE_ROWS, D_MODEL // 2), lambda i, be, nu, nv: (i, 0)),
        scratch_shapes=[pltpu.VMEM((D_MODEL, D_EXPERT), BF16), pltpu.VMEM((D_MODEL, D_EXPERT), BF16),
                        pltpu.VMEM((D_EXPERT, D_MODEL), BF16)],
    )
    return pl.pallas_call(
        _expert_kernel,
        grid_spec=grid_spec,
        out_shape=jax.ShapeDtypeStruct((rows, D_MODEL // 2), jnp.int32),
        compiler_params=_cparams(("arbitrary",)),
        name="experts",
    )(blk_e, n_used, nvalid, xb, w_gate, w_up, w_down)


def _combine_kernel(x1_ref, y1_ref, y2_ref, route_ref, g_ref, outp_ref, outs_ref, *, prompt_tiles):
    r = route_ref[...]
    x = (x1_ref[...] + r[:, 2:3] * _unpack_bf16_pairs(y1_ref[...])
         + r[:, 3:4] * _unpack_bf16_pairs(y2_ref[...]))
    ms = jnp.mean(x * x, axis=-1, keepdims=True)
    y = x * lax.rsqrt(ms + EPS) * g_ref[...]
    i = pl.program_id(0)

    @pl.when(i < prompt_tiles)
    def _():
        outp_ref[...] = y

    @pl.when(i >= prompt_tiles)
    def _():
        outs_ref[...] = y


def _combine_norm(x1, ygath, route, gamma, tp):
    t = x1.shape[0]
    tm = 512
    nt, npt = t // tm, tp // tm
    return pl.pallas_call(
        functools.partial(_combine_kernel, prompt_tiles=npt),
        grid=(nt,),
        in_specs=[
            pl.BlockSpec((tm, D_MODEL), lambda i: (i, 0)),
            pl.BlockSpec((tm, D_MODEL // 2), lambda i: (i, 0)),
            pl.BlockSpec((tm, D_MODEL // 2), lambda i: (i + nt, 0)),
            pl.BlockSpec((tm, LANES), lambda i: (i, 0)),
            pl.BlockSpec((1, D_MODEL), lambda i: (0, 0)),
        ],
        out_specs=[
            pl.BlockSpec((tm, D_MODEL), lambda i: (jnp.minimum(i, npt - 1), 0)),
            pl.BlockSpec((tm, D_MODEL), lambda i: (jnp.maximum(i - npt, 0), 0)),
        ],
        out_shape=[jax.ShapeDtypeStruct((tp, D_MODEL), F32), jax.ShapeDtypeStruct((t - tp, D_MODEL), F32)],
        compiler_params=_cparams(("arbitrary",)),
        name="combine_norm",
    )(x1, ygath, ygath, route, gamma)


def _band_index():
    c = (2 * QB - np.arange(2 * QB)) % (2 * QB)
    return c, c <= QB


def _bias_a_prompt(table_a):
    c, valid = _band_index()
    idx = np.stack([_t5_bucket_np(d * np.clip(QB - c, 0, QB)) for d in DILATIONS])
    return jnp.where(valid, jnp.transpose(table_a[idx], (0, 2, 1)) * LOG2E, NEG)


def _bias_b_prompt(table_b):
    c, valid = _band_index()
    valid = valid & (c >= 1)
    h = jnp.where(valid, table_b[_t5_bucket_np(np.clip(QB - c, 0, QB))].T * LOG2E, NEG)
    return jnp.transpose(h.reshape(KV_B, G_B, 2 * QB), (1, 0, 2)).reshape(H_B, 2 * QB)


def _sample_bias(table, span, t, log2_weight):
    cols = span + LANES
    period = cols + LANES
    x = np.arange(period)
    dist = np.where(x >= period - t, span - x + period, span - x)
    extra = log2_weight(dist)
    valid = np.isfinite(extra)
    u = jnp.where(valid, table[_t5_bucket_np(np.maximum(dist, 0))].T * LOG2E
                  + np.where(valid, extra, 0.0).astype(np.float32), NEG)
    rows = jnp.tile(u, (1, t))[:, :t * (period - 1)].reshape(u.shape[0], t, period - 1)[:, :, :cols]
    return rows.reshape(u.shape[0] * t, cols)


def _bias_a_sample(table_a, t):
    def log2_count(dist):
        count = np.zeros(dist.shape, np.int64)
        for w, d in zip(WINDOWS, DILATIONS):
            count += (dist >= 0) & (dist % d == 0) & (dist <= w)
        return np.where(count > 0, np.log2(np.maximum(count, 1)), -np.inf)

    return _sample_bias(table_a, WIN_A, t, log2_count)


def _bias_b_sample(table_b, t):
    return _sample_bias(table_b, WIN_B, t,
                        lambda dist: np.where((dist >= 0) & (dist < WIN_B), 0.0, -np.inf))


def _dest_kernel(route_ref, cnt_ref, tri_ref, dest_ref, meta_ref, run_scr, pst_scr):
    i = pl.program_id(0)
    tm = route_ref.shape[0]
    r = route_ref[...]
    lane = lax.broadcasted_iota(jnp.int32, (tm, LANES), 1)
    lanef = lane.astype(F32)
    oh0 = lanef == r[:, 0:1]
    oh1 = lanef == r[:, 1:2]
    ohf = jnp.concatenate([oh0, oh1], axis=0).astype(F32)

    @pl.when(i == 0)
    def _():
        cnt = jnp.broadcast_to(cnt_ref[...], (LANES, LANES))
        padded = jnp.floor((cnt + (MOE_ROWS - 1)) * (1.0 / MOE_ROWS)) * MOE_ROWS
        lane_e = lax.broadcasted_iota(jnp.int32, (LANES, LANES), 1)
        x = padded
        for sh in (1, 2, 4, 8, 16, 32, 64):
            x = x + jnp.where(lane_e >= sh, pltpu.roll(x, sh, 1), 0.0)
        pst_scr[...] = (x - padded)[0:1]
        run_scr[...] = jnp.zeros_like(run_scr)
        wide = lambda v: jnp.concatenate([v.T, v.T], axis=1)
        cnt_t, bend_t = wide(cnt), wide(x * (1.0 / MOE_ROWS))
        bstart_t = wide((x - padded) * (1.0 / MOE_ROWS))
        blk = lax.broadcasted_iota(jnp.int32, (LANES, 2 * LANES), 1).astype(F32)
        exp = lax.broadcasted_iota(jnp.int32, (LANES, 2 * LANES), 0)
        real = exp < N_EXPERTS
        blk_e = jnp.minimum(jnp.sum(jnp.where(real & (bend_t <= blk), 1.0, 0.0), axis=0, keepdims=True),
                            N_EXPERTS - 1.0)
        mine = exp.astype(F32) == blk_e
        within = blk[0:1] - jnp.sum(jnp.where(mine, bstart_t, 0.0), axis=0, keepdims=True)
        nvalid = jnp.clip(jnp.sum(jnp.where(mine, cnt_t, 0.0), axis=0, keepdims=True) - within * MOE_ROWS,
                          0.0, float(MOE_ROWS))
        n_used = jnp.max(jnp.where(real, bend_t, 0.0), axis=0, keepdims=True)
        meta_ref[...] = jnp.concatenate([blk_e, nvalid, n_used, jnp.zeros((5, 2 * LANES), F32)],
                                        axis=0).astype(jnp.int32)

    csum = jnp.dot(tri_ref[...], ohf.astype(BF16), preferred_element_type=F32)
    val = csum + (run_scr[...] + pst_scr[...] - 1.0)
    d0 = jnp.sum(jnp.where(oh0, val[:tm], 0.0), axis=-1, keepdims=True)
    d1 = jnp.sum(jnp.where(oh1, val[tm:], 0.0), axis=-1, keepdims=True)
    tile = jnp.where(lane == 0, d0, jnp.where(lane == 1, d1, 0.0))
    dest_ref[...] = tile.T[:8].astype(jnp.int32)
    run_scr[...] += jnp.sum(ohf, axis=0, keepdims=True)


def _dispatch(route, cnt):
    t = route.shape[0]
    tm = 512
    tri = (jnp.arange(2 * tm)[:, None] >= jnp.arange(2 * tm)[None, :]).astype(BF16)
    nblocks = -(-t * TOP_K // MOE_ROWS) + N_EXPERTS
    assert nblocks <= 2 * LANES
    dest, meta = pl.pallas_call(
        _dest_kernel,
        grid=(t // tm,),
        in_specs=[pl.BlockSpec((tm, LANES), lambda i: (i, 0)),
                  pl.BlockSpec((1, LANES), lambda i: (0, 0)),
                  pl.BlockSpec((2 * tm, 2 * tm), lambda i: (0, 0))],
        out_specs=[pl.BlockSpec((8, tm), lambda i: (0, i)),
                   pl.BlockSpec((8, 2 * LANES), lambda i: (0, 0))],
        out_shape=[jax.ShapeDtypeStruct((8, t), jnp.int32), jax.ShapeDtypeStruct((8, 2 * LANES), jnp.int32)],
        scratch_shapes=[pltpu.VMEM((1, LANES), F32), pltpu.VMEM((1, LANES), F32)],
        compiler_params=_cparams(("arbitrary",)),
        name="moe_dest",
    )(route, cnt, tri)
    return dest[:TOP_K], meta[0, :nblocks], meta[2, :1], meta[1, :nblocks]


def kernel(x_prompt, x_sample, cache_a_k, cache_a_v, cache_b_k, cache_b_v, rel_bias_table, attn_norm, w_in,
           w_out, attn_sinks, ffn_norm, w_router_group, b_router_group, w_router_expert, b_router_expert,
           w_gate, w_up, w_down, final_norm):
    s = x_prompt.shape[1]
    ns, ts = x_sample.shape[0], x_sample.shape[1]
    table_a = rel_bias_table[:, :H_A]
    table_b = rel_bias_table[:, H_A:]

    w = w_in[0]
    wqa, wka, wva, wqb, wkb, wvb = (w[:, 0:512], w[:, 512:1024], w[:, 1024:1536], w[:, 1536:2048],
                                    w[:, 2048:2176], w[:, 2176:2304])
    wqb = jnp.transpose(wqb.reshape(D_MODEL, KV_B, G_B, HEAD_DIM), (0, 2, 1, 3)).reshape(D_MODEL, 512)
    wp = jnp.concatenate([wka, wva, wqa, wqb, wkb, wvb], axis=1).astype(BF16)
    cscale = jnp.concatenate([jnp.ones((1, 1024), F32), jnp.full((1, 1024), SCALE * LOG2E, F32),
                              jnp.ones((1, 256), F32)], axis=1)
    wo = w_out[0]
    wo_b = jnp.transpose(wo[512:].reshape(KV_B, G_B, HEAD_DIM, D_MODEL), (1, 0, 2, 3)).reshape(512, D_MODEL)
    wo_p = jnp.concatenate([wo[:512], wo_b], axis=0).astype(BF16)
    wr = jnp.concatenate([w_router_group[0],
                          jnp.transpose(w_router_expert[0], (1, 0, 2)).reshape(D_MODEL, N_EXPERTS),
                          jnp.zeros((D_MODEL, LANES - N_GROUPS - N_EXPERTS), F32)], axis=1)
    wr_hi = wr.astype(BF16)
    wr = jnp.concatenate([wr_hi, wr_hi, (wr - wr_hi.astype(F32)).astype(BF16)], axis=0)
    br = jnp.concatenate([b_router_group[0], b_router_expert[0].reshape(N_EXPERTS),
                          jnp.zeros((LANES - N_GROUPS - N_EXPERTS,), F32)]).reshape(1, LANES)
    sinks2 = attn_sinks[0] * LOG2E
    sinks_gk = jnp.transpose(sinks2.reshape(KV_B, G_B), (1, 0)).reshape(H_B)
    sink_rows_p = jnp.repeat(sinks_gk, QB).reshape(G_B, 1, 2 * QB)
    sink_rows_s = jnp.repeat(sinks2, ts).reshape(H_B * ts, 1)
    emat = jnp.tile(jnp.arange(LANES)[:, None] == (jnp.arange(A_WIDTH)[None, :] // HEAD_DIM),
                    (3, 1)).astype(BF16)
    attn_g = attn_norm[0].reshape(1, D_MODEL)
    ffn_g = ffn_norm[0].reshape(1, D_MODEL)

    xp = x_prompt.reshape(s, D_MODEL)
    aperm, qb_p, kvb_p, akv32, bkv32 = _proj_prompt(xp, attn_g, wp, cscale)
    a4 = _attn_a_prompt(aperm, _bias_a_prompt(table_a), emat)
    ob_p = _attn_b_prompt(qb_p, kvb_p, _bias_b_prompt(table_b), sink_rows_p)

    xs = x_sample.reshape(ns * ts, D_MODEL)
    q_s, kv_s = _proj_sample(xs, attn_g, wp, cscale)
    akt = jnp.transpose(cache_a_k[0], (0, 2, 3, 1)).reshape(ns, A_WIDTH, WIN_A)
    avt = jnp.transpose(cache_a_v[0], (0, 2, 3, 1)).reshape(ns, A_WIDTH, WIN_A)
    bkt = jnp.transpose(cache_b_k[0], (0, 2, 3, 1)).reshape(ns, LANES, WIN_B)
    bvt = jnp.transpose(cache_b_v[0], (0, 2, 3, 1)).reshape(ns, LANES, WIN_B)

    x1, xn, route, cnt = _out_router(xp, a4, ob_p, xs, q_s.reshape(ns, ts, 1024), kv_s.reshape(ns, ts, 1280),
                                     akt, avt, bkt, bvt, _bias_a_sample(table_a, ts),
                                     _bias_b_sample(table_b, ts), sink_rows_s, wo_p, ffn_g, wr, br)
    dest2, blk_e, n_used, nvalid = _dispatch(route, cnt)
    xb = _sc_scatter_rows(xn, dest2, blk_e.shape[0] * MOE_ROWS)
    yb = _experts(blk_e, n_used, nvalid, xb, w_gate[0], w_up[0], w_down[0])
    y_p, y_s = _combine_norm(x1, _sc_gather_rows(yb, dest2.reshape(-1)), route, final_norm.reshape(1, D_MODEL), s)

    y_prompt = y_p.reshape(1, s, D_MODEL)
    y_sample = y_s.reshape(ns, ts, D_MODEL)
    keep_a, keep_b = min(WIN_A, s), min(WIN_B, s)
    pak = akv32[s - keep_a:, :512].reshape(1, 1, keep_a, H_A, HEAD_DIM)
    pav = akv32[s - keep_a:, 512:].reshape(1, 1, keep_a, H_A, HEAD_DIM)
    pbk = bkv32[s - keep_b:, :128].reshape(1, 1, keep_b, KV_B, HEAD_DIM)
    pbv = bkv32[s - keep_b:, 128:].reshape(1, 1, keep_b, KV_B, HEAD_DIM)
    sak = kv_s[:, 0:512].reshape(1, ns, ts, H_A, HEAD_DIM)
    sav = kv_s[:, 512:1024].reshape(1, ns, ts, H_A, HEAD_DIM)
    sbk = kv_s[:, 1024:1152].reshape(1, ns, ts, KV_B, HEAD_DIM)
    sbv = kv_s[:, 1152:1280].reshape(1, ns, ts, KV_B, HEAD_DIM)
    return (y_prompt, y_sample, pak, pav, pbk, pbv, sak, sav, sbk, sbv)
```

```python
import functools
import math

import jax
import jax.numpy as jnp
import numpy as np
from jax import lax
from jax.experimental import pallas as pl
from jax.experimental.pallas import tpu as pltpu
from jax.experimental.pallas import tpu_sc as plsc

D_MODEL = 1024
HEAD_DIM = 64
H_A = 8
H_B = 8
KV_B = 2
G_B = 4
DILATIONS = (1, 4, 16)
WINDOWS = (128, 512, 2048)
WIN_A = 2048
WIN_B = 128
NUM_BUCKETS = 32
MAX_DISTANCE = 2048
N_GROUPS = 4
EXPERTS_PER_GROUP = 8
N_EXPERTS = 32
TOP_K = 2
D_EXPERT = 512
EPS = 1e-5
SCALE = HEAD_DIM ** -0.5
PAST_LEN = 16384

LANES = 128
SPAN = 2048
QB = 128
NCHUNK = 9
A_WIDTH = H_A * HEAD_DIM
MOE_ROWS = 512
SC_CORES = 2
SC_SUBCORES = 16
SC_WORKERS = SC_CORES * SC_SUBCORES
SC_WINDOW = 64
SC_SCATTER_WINDOW = 32
NEG = -1e30
LOG2E = math.log2(math.e)
B_STEP = 512
VMEM_LIMIT = 56 * 1024 * 1024
CACHE_BUFFERS = 3
OUT_ROUTER_VMEM = 60 * 1024 * 1024

F32 = jnp.float32
BF16 = jnp.bfloat16


def _t5_bucket_np(dist):
    dist = np.asarray(dist, np.int64)
    max_exact = NUM_BUCKETS // 2
    d = np.maximum(dist, 1).astype(np.float32)
    ratio = np.log(d / np.float32(max_exact)) / np.float32(math.log(MAX_DISTANCE / max_exact))
    large = max_exact + (ratio * np.float32(NUM_BUCKETS - max_exact)).astype(np.int32)
    large = np.minimum(large, NUM_BUCKETS - 1)
    return np.where(dist < max_exact, dist, large).astype(np.int32)


def _cparams(sem, vmem=VMEM_LIMIT):
    return pltpu.CompilerParams(dimension_semantics=sem, vmem_limit_bytes=vmem)


def _proj_prompt_kernel(x_ref, g_ref, w_ref, cs_ref, aperm_ref, qb_ref, kvb_ref, akv_ref, bkv_ref,
                        h_scr, p_scr):
    n = pl.program_id(1)

    @pl.when(n == 0)
    def _():
        x = x_ref[...]
        ms = jnp.mean(x * x, axis=-1, keepdims=True)
        h_scr[...] = (x * lax.rsqrt(ms + EPS) * g_ref[...]).astype(BF16)

    p = jnp.dot(h_scr[...], w_ref[...], preferred_element_type=F32) * cs_ref[...]

    @pl.when(n < 6)
    def _():
        aperm_ref[0] = p.astype(BF16)
        p_scr[0, 0] = p[:, :LANES]
        p_scr[0, 1] = p[:, LANES:]
        quarter = SPAN // 4
        for r in range(4):
            lo = p_scr[0, 0, pl.ds(r, quarter, stride=4), :]
            hi = p_scr[0, 1, pl.ds(r, quarter, stride=4), :]
            p_scr[1, 0, r * quarter:(r + 1) * quarter, :] = lo
            p_scr[1, 1, r * quarter:(r + 1) * quarter, :] = hi
            aperm_ref[1, r * quarter:(r + 1) * quarter, :] = jnp.concatenate([lo, hi], axis=1).astype(BF16)
        for r16 in range(16):
            start = (r16 % 4) * quarter + r16 // 4
            t = jnp.concatenate([p_scr[1, 0, pl.ds(start, QB, stride=4), :],
                                 p_scr[1, 1, pl.ds(start, QB, stride=4), :]], axis=1)
            aperm_ref[2, r16 * QB:(r16 + 1) * QB, :] = t.astype(BF16)

    @pl.when(n < 4)
    def _():
        akv_ref[...] = p

    @pl.when(jnp.logical_or(n == 6, n == 7))
    def _():
        qb_ref[...] = p.astype(BF16)

    @pl.when(n == 8)
    def _():
        kvb_ref[...] = p.astype(BF16)
        bkv_ref[...] = p


def _proj_prompt(x, gamma, w, cscale):
    s = x.shape[0]
    nspan = s // SPAN
    return pl.pallas_call(
        _proj_prompt_kernel,
        grid=(nspan, NCHUNK),
        in_specs=[
            pl.BlockSpec((SPAN, D_MODEL), lambda b, n: (b, 0)),
            pl.BlockSpec((1, D_MODEL), lambda b, n: (0, 0)),
            pl.BlockSpec((D_MODEL, 256), lambda b, n: (0, n)),
            pl.BlockSpec((1, 256), lambda b, n: (0, n)),
        ],
        out_specs=[
            pl.BlockSpec((3, SPAN, 256), lambda b, n: (0, b, jnp.minimum(n, 5))),
            pl.BlockSpec((SPAN, 256), lambda b, n: (b, jnp.clip(n - 6, 0, 1))),
            pl.BlockSpec((SPAN, 256), lambda b, n: (b, 0)),
            pl.BlockSpec((SPAN, 256), lambda b, n: (b, jnp.minimum(n, 3))),
            pl.BlockSpec((SPAN, 256), lambda b, n: (b, 0)),
        ],
        out_shape=[
            jax.ShapeDtypeStruct((3, s, 3 * A_WIDTH), BF16),
            jax.ShapeDtypeStruct((s, 512), BF16),
            jax.ShapeDtypeStruct((s, 256), BF16),
            jax.ShapeDtypeStruct((s, 1024), F32),
            jax.ShapeDtypeStruct((s, 256), F32),
        ],
        scratch_shapes=[pltpu.VMEM((SPAN, D_MODEL), BF16), pltpu.VMEM((2, 2, SPAN, LANES), F32)],
        compiler_params=_cparams(("arbitrary", "arbitrary")),
        name="proj_prompt",
    )(x, gamma, w, cscale)


def _proj_sample_kernel(x_ref, g_ref, w_ref, cs_ref, q_ref, kv_ref):
    x = x_ref[...]
    ms = jnp.mean(x * x, axis=-1, keepdims=True)
    h = (x * lax.rsqrt(ms + EPS) * g_ref[...]).astype(BF16)
    p = jnp.dot(h, w_ref[...], preferred_element_type=F32) * cs_ref[...]
    kv_ref[:, :1024] = p[:, :1024]
    kv_ref[:, 1024:] = p[:, 2048:]
    q_ref[...] = p[:, 1024:2048]


def _proj_sample(x, gamma, w, cscale):
    t = x.shape[0]
    tm = 512
    return pl.pallas_call(
        _proj_sample_kernel,
        grid=(t // tm,),
        in_specs=[
            pl.BlockSpec((tm, D_MODEL), lambda i: (i, 0)),
            pl.BlockSpec((1, D_MODEL), lambda i: (0, 0)),
            pl.BlockSpec((D_MODEL, 2304), lambda i: (0, 0)),
            pl.BlockSpec((1, 2304), lambda i: (0, 0)),
        ],
        out_specs=[
            pl.BlockSpec((tm, 1024), lambda i: (i, 0)),
            pl.BlockSpec((tm, 1280), lambda i: (i, 0)),
        ],
        out_shape=[
            jax.ShapeDtypeStruct((t, 1024), F32),
            jax.ShapeDtypeStruct((t, 1280), F32),
        ],
        compiler_params=_cparams(("arbitrary",)),
        name="proj_sample",
    )(x, gamma, w, cscale)


def _spread_heads(w, e3_ref):
    hi = w.astype(BF16)
    r1 = w - hi.astype(F32)
    mid = r1.astype(BF16)
    low = (r1 - mid.astype(F32)).astype(BF16)
    return jnp.dot(jnp.concatenate([hi, mid, low], axis=1), e3_ref[...], preferred_element_type=F32)


def _pair_tile(q2, kk, vv, bias_t, lo, sink=None):
    zero = jnp.zeros_like(q2)
    qq = jnp.concatenate([jnp.where(lo, q2, zero), jnp.where(lo, zero, q2)], axis=0)
    st = lax.dot_general(kk, qq, (((1,), (1,)), ((), ())), preferred_element_type=F32)
    st = st + bias_t
    m = jnp.max(st, axis=0, keepdims=True)
    if sink is not None:
        m = jnp.maximum(m, sink)
    p = jnp.exp2(st - m)
    den = jnp.sum(p, axis=0, keepdims=True)
    if sink is not None:
        den = den + jnp.exp2(sink - m)
    pn = (p * (1.0 / den)).astype(BF16)
    o = lax.dot_general(pn, vv, (((0,), (0,)), ((), ())), preferred_element_type=F32)
    return jnp.where(lo, o[:QB], o[QB:]), m + jnp.log2(den)


def _fill_band_tiles(h_ref, bias_scr):
    nk = 2 * QB
    prev = lax.broadcasted_iota(jnp.int32, (nk, nk), 0) < QB
    for pair in range(h_ref.shape[0] // 2):
        halves = []
        for hh in range(2):
            row = h_ref[2 * pair + hh:2 * pair + hh + 1, :]
            band = pltpu.roll(jnp.broadcast_to(row, (nk, nk)), 0, 1, stride=1, stride_axis=0)
            halves.append(band[:, :QB])
        tile = jnp.concatenate(halves, axis=1)
        bias_scr[0, pair] = tile
        bias_scr[1, pair] = jnp.where(prev, NEG, tile)


def _attn_a_kernel(q_ref, kvc_ref, kvp_ref, h_ref, e_ref, out_ref, o_scr, st_scr, bias_scr):
    b = pl.program_id(0)
    g = pl.program_id(1)
    nblk = jnp.where(g == 0, 16, jnp.where(g == 1, 4, 1))
    lane = lax.broadcasted_iota(jnp.int32, (QB, LANES), 1)
    lo = lane < HEAD_DIM

    @pl.when(b == 0)
    def _():
        _fill_band_tiles(h_ref, bias_scr.at[g])

    bias_ref = bias_scr.at[g]

    for cb in range(SPAN // QB):
        first = lax.rem(jnp.int32(cb), nblk) == 0
        rows = slice(cb * QB, (cb + 1) * QB)
        prow_c = max(cb - 1, 0) * QB
        prow_p = pl.multiple_of(jnp.where(first, cb + nblk - 1, 0) * QB, QB)
        variant = jnp.logical_and(first, b == 0).astype(jnp.int32)
        stats = []
        for hp in range(4):
            ks = slice(hp * LANES, (hp + 1) * LANES)
            vs = slice(A_WIDTH + hp * LANES, A_WIDTH + (hp + 1) * LANES)
            kp = jnp.where(first, kvp_ref[pl.ds(prow_p, QB), ks], kvc_ref[prow_c:prow_c + QB, ks])
            vp = jnp.where(first, kvp_ref[pl.ds(prow_p, QB), vs], kvc_ref[prow_c:prow_c + QB, vs])
            kk = jnp.concatenate([kp, kvc_ref[rows, ks]], axis=0)
            vv = jnp.concatenate([vp, kvc_ref[rows, vs]], axis=0)
            o, lse = _pair_tile(q_ref[rows, ks], kk, vv, bias_ref[variant, hp], lo)
            o_scr[g, hp, rows, :] = o
            stats += [lse[:, :QB], lse[:, QB:]]
        sm = jnp.concatenate(stats + [jnp.zeros((LANES - H_A, QB), F32)], axis=0)
        st_scr[g, rows, :] = sm.T

    @pl.when(g == 2)
    def _():
        def merge(c, carry):
            r2 = lax.rem(c, 4) * (SPAN // 4) + c // 4
            r3 = pl.multiple_of(c * QB, QB)
            l1 = st_scr[0, pl.ds(c, QB, stride=16), :]
            l2 = st_scr[1, pl.ds(r2, QB, stride=4), :]
            l3 = st_scr[2, pl.ds(r3, QB), :]
            mx = jnp.maximum(jnp.maximum(l1, l2), l3)
            w1 = jnp.exp2(l1 - mx)
            w2 = jnp.exp2(l2 - mx)
            w3 = jnp.exp2(l3 - mx)
            tot = w1 + w2 + w3
            a1 = _spread_heads(w1 / tot, e_ref)
            a2 = _spread_heads(w2 / tot, e_ref)
            a3 = _spread_heads(w3 / tot, e_ref)
            for hp in range(4):
                sl = slice(hp * LANES, (hp + 1) * LANES)
                o1 = o_scr[0, hp, pl.ds(c, QB, stride=16), :]
                o2 = o_scr[1, hp, pl.ds(r2, QB, stride=4), :]
                o3 = o_scr[2, hp, pl.ds(r3, QB), :]
                out_ref[hp, pl.ds(c, QB, stride=16), :] = a1[:, sl] * o1 + a2[:, sl] * o2 + a3[:, sl] * o3
            return carry

        lax.fori_loop(0, 16, merge, 0, unroll=4)


def _attn_a_prompt(aperm, bias_a, emat):
    s = aperm.shape[1]
    nspan = s // SPAN
    return pl.pallas_call(
        _attn_a_kernel,
        grid=(nspan, 3),
        in_specs=[
            pl.BlockSpec((None, SPAN, A_WIDTH), lambda b, g: (g, b, 2)),
            pl.BlockSpec((None, SPAN, 2 * A_WIDTH), lambda b, g: (g, b, 0)),
            pl.BlockSpec((None, SPAN, 2 * A_WIDTH), lambda b, g: (g, jnp.maximum(b - 1, 0), 0)),
            pl.BlockSpec((None, H_A, 2 * QB), lambda b, g: (g, 0, 0)),
            pl.BlockSpec((3 * LANES, A_WIDTH), lambda b, g: (0, 0)),
        ],
        out_specs=pl.BlockSpec((4, SPAN, LANES), lambda b, g: (0, b, 0)),
        out_shape=jax.ShapeDtypeStruct((4, s, LANES), F32),
        scratch_shapes=[pltpu.VMEM((3, 4, SPAN, LANES), F32), pltpu.VMEM((3, SPAN, LANES), F32),
                        pltpu.VMEM((3, 2, 4, 2 * QB, 2 * QB), F32)],
        compiler_params=_cparams(("arbitrary", "arbitrary")),
        name="attn_a_prompt",
    )(aperm, aperm, aperm, bias_a, emat)


def _attn_b_kernel(q_ref, kvc_ref, kvp_ref, h_ref, sink_ref, out_ref, bias_ref):
    i = pl.program_id(0)
    lane = lax.broadcasted_iota(jnp.int32, (QB, LANES), 1)
    lo = lane < HEAD_DIM

    @pl.when(i == 0)
    def _():
        _fill_band_tiles(h_ref, bias_ref)

    variant = (i == 0).astype(jnp.int32)
    for j in range(B_STEP // QB):
        rows = slice(j * QB, (j + 1) * QB)
        if j == 0:
            kp, vp = kvp_ref[:, :LANES], kvp_ref[:, LANES:]
        else:
            kp, vp = kvc_ref[(j - 1) * QB:j * QB, :LANES], kvc_ref[(j - 1) * QB:j * QB, LANES:]
        kk = jnp.concatenate([kp, kvc_ref[rows, :LANES]], axis=0)
        vv = jnp.concatenate([vp, kvc_ref[rows, LANES:]], axis=0)
        for g in range(G_B):
            bias_t = bias_ref[variant, g] if j == 0 else bias_ref[0, g]
            o, _ = _pair_tile(q_ref[rows, g * LANES:(g + 1) * LANES], kk, vv, bias_t, lo, sink=sink_ref[g])
            out_ref[rows, g * LANES:(g + 1) * LANES] = o.astype(BF16)


def _attn_b_prompt(qb, kvb, bias_b, sink_rows):
    s = qb.shape[0]
    per = B_STEP // QB
    return pl.pallas_call(
        _attn_b_kernel,
        grid=(s // B_STEP,),
        in_specs=[
            pl.BlockSpec((B_STEP, 512), lambda i: (i, 0)),
            pl.BlockSpec((B_STEP, 256), lambda i: (i, 0)),
            pl.BlockSpec((QB, 256), lambda i: (jnp.maximum(i * per - 1, 0), 0)),
            pl.BlockSpec((H_B, 2 * QB), lambda i: (0, 0)),
            pl.BlockSpec((G_B, 1, 2 * QB), lambda i: (0, 0, 0)),
        ],
        out_specs=pl.BlockSpec((B_STEP, 512), lambda i: (i, 0)),
        out_shape=jax.ShapeDtypeStruct((s, 512), BF16),
        scratch_shapes=[pltpu.VMEM((2, G_B, 2 * QB, 2 * QB), F32)],
        compiler_params=_cparams(("arbitrary",)),
        name="attn_b_prompt",
    )(qb, kvb, kvb, bias_b, sink_rows)


def _sample_attention(q, kvn, akt, avt, bkt, bvt, cba, cbb, sink):
    t = q.shape[0]
    kvn_p = jnp.concatenate([kvn, jnp.zeros((LANES - t, kvn.shape[1]), F32)], axis=0).astype(BF16)
    lane_a = lax.broadcasted_iota(jnp.int32, (t, A_WIDTH), 1) // HEAD_DIM

    qa = q[:, :A_WIDTH]
    qbd = jnp.concatenate([jnp.where(lane_a == h, qa, 0.0) for h in range(H_A)], axis=0).astype(BF16)
    s_c = jnp.dot(qbd, akt.astype(BF16), preferred_element_type=F32)
    s_n = lax.dot_general(qbd, kvn_p[:, :A_WIDTH], (((1,), (1,)), ((), ())), preferred_element_type=F32)
    s = jnp.concatenate([s_c, s_n], axis=1) + cba
    m = jnp.max(s, axis=-1, keepdims=True)
    p = jnp.exp2(s - m)
    l = jnp.sum(p, axis=-1, keepdims=True)
    o_n = jnp.dot(p[:, WIN_A:].astype(BF16), kvn_p[:, A_WIDTH:2 * A_WIDTH], preferred_element_type=F32)
    pc = jnp.concatenate([p[:, :WIN_A], jnp.zeros((LANES - H_A * t, WIN_A), F32)], axis=0).astype(BF16)
    o_t = lax.dot_general(avt.astype(BF16), pc, (((1,), (1,)), ((), ())),
                          preferred_element_type=F32)
    o_all = o_t.T[:H_A * t] + o_n
    o_sel = jnp.zeros((t, A_WIDTH), F32)
    l_b = jnp.ones((t, A_WIDTH), F32)
    for h in range(H_A):
        sel = lane_a == h
        o_sel = jnp.where(sel, o_all[h * t:(h + 1) * t], o_sel)
        l_b = jnp.where(sel, l[h * t:(h + 1) * t], l_b)
    oa = o_sel / l_b

    lane_b = lax.broadcasted_iota(jnp.int32, (G_B * t, LANES), 1)
    lo = lane_b < HEAD_DIM
    qb2 = jnp.concatenate([q[:, A_WIDTH + g * LANES:A_WIDTH + (g + 1) * LANES] for g in range(G_B)], axis=0)
    qm = jnp.concatenate([jnp.where(lo, qb2, 0.0), jnp.where(lo, 0.0, qb2)], axis=0).astype(BF16)
    kb_n = kvn_p[:, 2 * A_WIDTH:2 * A_WIDTH + LANES]
    vb_n = kvn_p[:, 2 * A_WIDTH + LANES:]
    sb_c = jnp.dot(qm, bkt.astype(BF16), preferred_element_type=F32)
    sb_n = lax.dot_general(qm, kb_n, (((1,), (1,)), ((), ())), preferred_element_type=F32)
    sb = jnp.concatenate([sb_c, sb_n], axis=1) + cbb
    mb = jnp.maximum(jnp.max(sb, axis=-1, keepdims=True), sink)
    pbb = jnp.exp2(sb - mb)
    den = jnp.sum(pbb, axis=-1, keepdims=True) + jnp.exp2(sink - mb)
    pbb = pbb.astype(BF16)
    ob = lax.dot_general(pbb[:, :WIN_B], bvt.astype(BF16), (((1,), (1,)), ((), ())),
                         preferred_element_type=F32)
    ob = (ob + jnp.dot(pbb[:, WIN_B:], vb_n, preferred_element_type=F32)) / den
    half = G_B * t
    lo8 = lo[:t]
    ob = jnp.concatenate([jnp.where(lo8, ob[g * t:(g + 1) * t], ob[half + g * t:half + (g + 1) * t])
                          for g in range(G_B)], axis=1)
    return oa, ob


def _route(logits):
    lane = lax.broadcasted_iota(jnp.int32, logits.shape, 1).astype(F32)
    big = jnp.float32(1 << 20)
    ninf = jnp.float32(-jnp.inf)
    gmask = lane < N_GROUPS
    lg = jnp.where(gmask, logits, ninf)
    gmax = jnp.max(lg, axis=-1, keepdims=True)
    grp = jnp.min(jnp.where(lg == gmax, lane, big), axis=-1, keepdims=True)
    pg_top = 1.0 / jnp.sum(jnp.exp(lg - gmax), axis=-1, keepdims=True)
    e0 = N_GROUPS + grp * EXPERTS_PER_GROUP
    emask = jnp.logical_and(lane >= e0, lane < e0 + EXPERTS_PER_GROUP)
    le = jnp.where(emask, logits, ninf)
    emax = jnp.max(le, axis=-1, keepdims=True)
    esum = jnp.sum(jnp.exp(le - emax), axis=-1, keepdims=True)
    i1 = jnp.min(jnp.where(le == emax, lane, big), axis=-1, keepdims=True)
    le2 = jnp.where(lane == i1, ninf, le)
    e2max = jnp.max(le2, axis=-1, keepdims=True)
    i2 = jnp.min(jnp.where(le2 == e2max, lane, big), axis=-1, keepdims=True)
    p1 = 1.0 / esum
    p2 = jnp.exp(e2max - emax) / esum
    g1 = pg_top * p1 / (p1 + p2)
    g2 = pg_top * p2 / (p1 + p2)
    out = jnp.where(lane == 0, i1 - N_GROUPS, 0.0)
    out = jnp.where(lane == 1, i2 - N_GROUPS, out)
    out = jnp.where(lane == 2, g1, out)
    out = jnp.where(lane == 3, g2, out)
    return out


def _pack_bf16_pairs(x):
    half = x.shape[1] // 2

    def rne(v):
        bits = lax.bitcast_convert_type(v, jnp.int32)
        return bits + 0x7FFF + (lax.shift_right_logical(bits, 16) & 1)

    lo = lax.shift_right_logical(rne(x[:, :half]), 16)
    hi = rne(x[:, half:]) & jnp.int32(-65536)
    return lo | hi


def _unpack_bf16_pairs(w):
    lo = lax.bitcast_convert_type(lax.shift_left(w, 16), F32)
    hi = lax.bitcast_convert_type(w & jnp.int32(-65536), F32)
    return jnp.concatenate([lo, hi], axis=1)


def _out_router_kernel(xp_ref, ap_ref, bp_ref, xs_ref, q_ref, kvn_ref, akt_hbm, avt_hbm, bkt_ref, bvt_ref,
                       cba_ref, cbb_ref, sink_ref, wo_ref, g_ref, wr_ref, br_ref,
                       x1_ref, xn_ref, route_ref, cnt_ref,
                       xcat_scr, mix_scr, kbuf, vbuf, sem, *, prompt_tiles, decode_tiles, seqs_per_step):
    i = pl.program_id(0)
    seqs = prompt_tiles * seqs_per_step

    def cache_copies(n, slot):
        return (pltpu.make_async_copy(akt_hbm.at[n], kbuf.at[slot], sem.at[0, slot]),
                pltpu.make_async_copy(avt_hbm.at[n], vbuf.at[slot], sem.at[1, slot]))

    @pl.when(i == 0)
    def _():
        cnt_ref[...] = jnp.zeros_like(cnt_ref)
        xcat_scr[...] = jnp.zeros_like(xcat_scr)
        for n0 in range(2):
            for c in cache_copies(n0, n0):
                c.start()

    @pl.when(i == prompt_tiles)
    def _():
        for n1 in (seqs, seqs + 1):
            for c in cache_copies(seqs - 1, n1 % CACHE_BUFFERS):
                c.wait()

    pslot = lax.rem(i, 2)

    def route_previous():
        logits = jnp.dot(xcat_scr[1 - pslot], wr_ref[...], preferred_element_type=F32)
        route = _route(logits + br_ref[...])
        route_ref[...] = route
        lanef = lax.broadcasted_iota(jnp.int32, route.shape, 1).astype(F32)
        hits = (lanef == route[:, 0:1]).astype(F32) + (lanef == route[:, 1:2]).astype(F32)
        cnt_ref[...] += jnp.sum(hits, axis=0, keepdims=True) * (i > 0).astype(F32)

    def project(x_ref, mix):
        x1 = x_ref[...] + jnp.dot(mix, wo_ref[...], preferred_element_type=F32)
        x1_ref[...] = x1
        ms = jnp.mean(x1 * x1, axis=-1, keepdims=True)
        xn = x1 * lax.rsqrt(ms + EPS) * g_ref[...]
        xn_ref[...] = _pack_bf16_pairs(xn)
        xh = xn.astype(BF16)
        xl = (xn - xh.astype(F32)).astype(BF16)
        xcat_scr[pslot] = jnp.concatenate([xh, xl, xh], axis=1)

    @pl.when(i < prompt_tiles)
    def _():
        route_previous()
        mix = jnp.concatenate([ap_ref[0], ap_ref[1], ap_ref[2], ap_ref[3]], axis=1).astype(BF16)
        project(xp_ref, jnp.concatenate([mix, bp_ref[...]], axis=1))
        t = q_ref.shape[1]
        for s in range(seqs_per_step):
            n = i * seqs_per_step + s
            slot = lax.rem(n, CACHE_BUFFERS)
            for c in cache_copies(n, slot):
                c.wait()
            for c in cache_copies(jnp.minimum(n + 2, seqs - 1), lax.rem(n + 2, CACHE_BUFFERS)):
                c.start()
            oa, ob = _sample_attention(q_ref[s], kvn_ref[s], kbuf[slot], vbuf[slot], bkt_ref[s], bvt_ref[s],
                                       cba_ref[...], cbb_ref[...], sink_ref[...])
            row = pl.multiple_of(n * t, t)
            mix_scr[pl.ds(row, t), :A_WIDTH] = oa
            mix_scr[pl.ds(row, t), A_WIDTH:] = ob

    @pl.when(i >= prompt_tiles)
    def _():
        route_previous()
        tm = xs_ref.shape[0]
        row = pl.multiple_of(jnp.clip(i - prompt_tiles, 0, decode_tiles - 1) * tm, tm)
        project(xs_ref, mix_scr[pl.ds(row, tm), :].astype(BF16))


def _out_router(xp, a4p, bp, xs, q3, kvn3, akt, avt, bkt, bvt, cbias_a, cbias_b, sink_rows, wo, gamma, wr, br):
    tp, tsm = xp.shape[0], xs.shape[0]
    ns, ts = q3.shape[0], q3.shape[1]
    tm = 512
    npt, nst = tp // tm, tsm // tm
    nt = npt + nst
    t = tp + tsm
    sps = ns // npt
    assert sps * npt == ns and ns * ts == tsm and ns >= CACHE_BUFFERS
    pmap = lambda i: (jnp.minimum(i, npt - 1), 0)
    pmap3 = lambda i: (jnp.minimum(i, npt - 1), 0, 0)
    smap = lambda i: (jnp.clip(i - npt, 0, nst - 1), 0)
    cur = lambda i: (jnp.minimum(i, nt - 1), 0)
    const = lambda i: (0, 0)
    return pl.pallas_call(
        functools.partial(_out_router_kernel, prompt_tiles=npt, decode_tiles=nst, seqs_per_step=sps),
        grid=(nt + 1,),
        in_specs=[
            pl.BlockSpec((tm, D_MODEL), pmap),
            pl.BlockSpec((4, tm, LANES), lambda i: (0, jnp.minimum(i, npt - 1), 0)),
            pl.BlockSpec((tm, 512), pmap),
            pl.BlockSpec((tm, D_MODEL), smap),
            pl.BlockSpec((sps, ts, 1024), pmap3),
            pl.BlockSpec((sps, ts, 1280), pmap3),
            pl.BlockSpec(memory_space=pl.ANY),
            pl.BlockSpec(memory_space=pl.ANY),
            pl.BlockSpec((sps, LANES, WIN_B), pmap3),
            pl.BlockSpec((sps, LANES, WIN_B), pmap3),
            pl.BlockSpec((H_A * ts, WIN_A + LANES), const),
            pl.BlockSpec((H_B * ts, WIN_B + LANES), const),
            pl.BlockSpec((H_B * ts, 1), const),
            pl.BlockSpec((D_MODEL, D_MODEL), const),
            pl.BlockSpec((1, D_MODEL), const),
            pl.BlockSpec((3 * D_MODEL, LANES), const),
            pl.BlockSpec((1, LANES), const),
        ],
        out_specs=[
            pl.BlockSpec((tm, D_MODEL), cur),
            pl.BlockSpec((tm, D_MODEL // 2), cur),
            pl.BlockSpec((tm, LANES), lambda i: (jnp.maximum(i - 1, 0), 0)),
            pl.BlockSpec((1, LANES), const),
        ],
        scratch_shapes=[pltpu.VMEM((2, tm, 3 * D_MODEL), BF16), pltpu.VMEM((tsm, D_MODEL), F32),
                        pltpu.VMEM((CACHE_BUFFERS, A_WIDTH, WIN_A), F32),
                        pltpu.VMEM((CACHE_BUFFERS, A_WIDTH, WIN_A), F32),
                        pltpu.SemaphoreType.DMA((2, CACHE_BUFFERS))],
        out_shape=[
            jax.ShapeDtypeStruct((t, D_MODEL), F32),
            jax.ShapeDtypeStruct((t, D_MODEL // 2), jnp.int32),
            jax.ShapeDtypeStruct((t, LANES), F32),
            jax.ShapeDtypeStruct((1, LANES), F32),
        ],
        compiler_params=_cparams(("arbitrary",), vmem=OUT_ROUTER_VMEM),
        name="out_router",
    )(xp, a4p, bp, xs, q3, kvn3, akt, avt, bkt, bvt, cbias_a, cbias_b, sink_rows, wo, gamma, wr, br)


def _sc_gather_rows(table, idx):
    b = idx.shape[0]
    d = table.shape[1]
    w = SC_WINDOW
    per_worker = b // SC_WORKERS
    nwin = per_worker // w
    assert per_worker * SC_WORKERS == b and nwin * w == per_worker
    mesh = plsc.VectorSubcoreMesh(core_axis_name="c", subcore_axis_name="s")

    @functools.partial(
        pl.kernel, mesh=mesh,
        out_type=jax.ShapeDtypeStruct((b, d), table.dtype),
        scratch_types=[pltpu.VMEM((nwin, w), jnp.int32), pltpu.VMEM((2, w, d), table.dtype),
                       pltpu.SemaphoreType.DMA((2,)), pltpu.SemaphoreType.DMA((2,))],
        name="sc_gather_rows",
    )
    def gather(table_hbm, idx_hbm, out_hbm, idx_v, rows_v, sem_in, sem_out):
        wid = lax.axis_index("s") * SC_CORES + lax.axis_index("c")
        base = wid * per_worker
        pltpu.sync_copy(idx_hbm.at[wid], idx_v)

        def fetch(j):
            return pltpu.make_async_copy(table_hbm.at[idx_v.at[j]], rows_v.at[j % 2], sem_in.at[j % 2])

        def flush(j):
            return pltpu.make_async_copy(rows_v.at[j % 2], out_hbm.at[pl.ds(base + j * w, w)],
                                         sem_out.at[j % 2])

        fetch(0).start()
        for j in range(nwin):
            fetch(j).wait()
            if j + 1 < nwin:
                if j >= 1:
                    flush(j - 1).wait()
                fetch(j + 1).start()
            flush(j).start()
        for j in range(max(nwin - 2, 0), nwin):
            flush(j).wait()

    return gather(table, idx.reshape(SC_WORKERS, nwin, w))


def _sc_scatter_rows(x, dest2, nrows):
    t, d = x.shape
    w = SC_SCATTER_WINDOW
    per_worker = t // SC_WORKERS
    nwin = per_worker // w
    assert per_worker * SC_WORKERS == t and nwin * w == per_worker
    mesh = plsc.VectorSubcoreMesh(core_axis_name="c", subcore_axis_name="s")

    @functools.partial(
        pl.kernel, mesh=mesh,
        out_type=jax.ShapeDtypeStruct((nrows, d), x.dtype),
        scratch_types=[pltpu.VMEM((TOP_K, nwin, w), jnp.int32), pltpu.VMEM((2, w, d), x.dtype),
                       pltpu.SemaphoreType.DMA((2,)), pltpu.SemaphoreType.DMA((2,))],
        name="sc_scatter_rows",
    )
    def scatter(x_hbm, dest_hbm, out_hbm, idx_v, rows_v, sem_in, sem_out):
        wid = lax.axis_index("s") * SC_CORES + lax.axis_index("c")
        base = wid * per_worker
        for k in range(TOP_K):
            pltpu.sync_copy(dest_hbm.at[k, wid], idx_v.at[k])

        def fetch(j):
            return pltpu.make_async_copy(x_hbm.at[pl.ds(base + j * w, w)], rows_v.at[j % 2], sem_in.at[j % 2])

        def spread(j, k):
            return pltpu.make_async_copy(rows_v.at[j % 2], out_hbm.at[idx_v.at[k, j]], sem_out.at[j % 2])

        fetch(0).start()
        for j in range(nwin):
            fetch(j).wait()
            if j + 1 < nwin:
                if j >= 1:
                    for k in range(TOP_K):
                        spread(j - 1, k).wait()
                fetch(j + 1).start()
            for k in range(TOP_K):
                spread(j, k).start()
        for j in range(max(nwin - 2, 0), nwin):
            for k in range(TOP_K):
                spread(j, k).wait()

    return scatter(x, dest2.reshape(TOP_K, SC_WORKERS, nwin, w))


def _expert_kernel(be_ref, nu_ref, nv_ref, x_ref, wg_ref, wu_ref, wd_ref, o_ref, wg_s, wu_s, wd_s):
    i = pl.program_id(0)
    used = i < nu_ref[0]
    changed = jnp.logical_or(i == 0, be_ref[i] != be_ref[jnp.maximum(i - 1, 0)])

    @pl.when(jnp.logical_and(used, changed))
    def _():
        wg_s[...] = wg_ref[...].astype(BF16)
        wu_s[...] = wu_ref[...].astype(BF16)
        wd_s[...] = wd_ref[...].astype(BF16)

    @pl.when(used)
    def _():
        row = lax.broadcasted_iota(jnp.int32, x_ref.shape, 0)
        x = _unpack_bf16_pairs(jnp.where(row < nv_ref[i], x_ref[...], 0)).astype(BF16)
        gate = jnp.dot(x, wg_s[...], preferred_element_type=F32)
        up = jnp.dot(x, wu_s[...], preferred_element_type=F32)
        h = (gate * jax.nn.sigmoid(gate) * up).astype(BF16)
        o_ref[...] = _pack_bf16_pairs(jnp.dot(h, wd_s[...], preferred_element_type=F32))

    @pl.when(jnp.logical_not(used))
    def _():
        o_ref[...] = jnp.zeros_like(o_ref)


def _experts(blk_e, n_used, nvalid, xb, w_gate, w_up, w_down):
    rows = xb.shape[0]
    nblocks = rows // MOE_ROWS
    grid_spec = pltpu.PrefetchScalarGridSpec(
        num_scalar_prefetch=3,
        grid=(nblocks,),
        in_specs=[
            pl.BlockSpec((MOE_ROWS, D_MODEL // 2), lambda i, be, nu, nv: (i, 0)),
            pl.BlockSpec((None, D_MODEL, D_EXPERT), lambda i, be, nu, nv: (be[i], 0, 0)),
            pl.BlockSpec((None, D_MODEL, D_EXPERT), lambda i, be, nu, nv: (be[i], 0, 0)),
            pl.BlockSpec((None, D_EXPERT, D_MODEL), lambda i, be, nu, nv: (be[i], 0, 0)),
        ],
        out_specs=pl.BlockSpec((MOE_ROWS, D_MODEL // 2), lambda i, be, nu, nv: (i, 0)),
        scratch_shapes=[pltpu.VMEM((D_MODEL, D_EXPERT), BF16), pltpu.VMEM((D_MODEL, D_EXPERT), BF16),
                        pltpu.VMEM((D_EXPERT, D_MODEL), BF16)],
    )
    return pl.pallas_call(
        _expert_kernel,
        grid_spec=grid_spec,
        out_shape=jax.ShapeDtypeStruct((rows, D_MODEL // 2), jnp.int32),
        compiler_params=_cparams(("arbitrary",)),
        name="experts",
    )(blk_e, n_used, nvalid, xb, w_gate, w_up, w_down)


def _combine_kernel(x1_ref, y1_ref, y2_ref, route_ref, g_ref, outp_ref, outs_ref, *, prompt_tiles):
    r = route_ref[...]
    x = (x1_ref[...] + r[:, 2:3] * _unpack_bf16_pairs(y1_ref[...])
         + r[:, 3:4] * _unpack_bf16_pairs(y2_ref[...]))
    ms = jnp.mean(x * x, axis=-1, keepdims=True)
    y = x * lax.rsqrt(ms + EPS) * g_ref[...]
    i = pl.program_id(0)

    @pl.when(i < prompt_tiles)
    def _():
        outp_ref[...] = y

    @pl.when(i >= prompt_tiles)
    def _():
        outs_ref[...] = y


def _combine_norm(x1, ygath, route, gamma, tp):
    t = x1.shape[0]
    tm = 512
    nt, npt = t // tm, tp // tm
    return pl.pallas_call(
        functools.partial(_combine_kernel, prompt_tiles=npt),
        grid=(nt,),
        in_specs=[
            pl.BlockSpec((tm, D_MODEL), lambda i: (i, 0)),
            pl.BlockSpec((tm, D_MODEL // 2), lambda i: (i, 0)),
            pl.BlockSpec((tm, D_MODEL // 2), lambda i: (i + nt, 0)),
            pl.BlockSpec((tm, LANES), lambda i: (i, 0)),
            pl.BlockSpec((1, D_MODEL), lambda i: (0, 0)),
        ],
        out_specs=[
            pl.BlockSpec((tm, D_MODEL), lambda i: (jnp.minimum(i, npt - 1), 0)),
            pl.BlockSpec((tm, D_MODEL), lambda i: (jnp.maximum(i - npt, 0), 0)),
        ],
        out_shape=[jax.ShapeDtypeStruct((tp, D_MODEL), F32), jax.ShapeDtypeStruct((t - tp, D_MODEL), F32)],
        compiler_params=_cparams(("arbitrary",)),
        name="combine_norm",
    )(x1, ygath, ygath, route, gamma)


def _band_index():
    c = (2 * QB - np.arange(2 * QB)) % (2 * QB)
    return c, c <= QB


def _bias_a_prompt(table_a):
    c, valid = _band_index()
    idx = np.stack([_t5_bucket_np(d * np.clip(QB - c, 0, QB)) for d in DILATIONS])
    return jnp.where(valid, jnp.transpose(table_a[idx], (0, 2, 1)) * LOG2E, NEG)


def _bias_b_prompt(table_b):
    c, valid = _band_index()
    valid = valid & (c >= 1)
    h = jnp.where(valid, table_b[_t5_bucket_np(np.clip(QB - c, 0, QB))].T * LOG2E, NEG)
    return jnp.transpose(h.reshape(KV_B, G_B, 2 * QB), (1, 0, 2)).reshape(H_B, 2 * QB)


def _sample_bias(table, span, t, log2_weight):
    cols = span + LANES
    period = cols + LANES
    x = np.arange(period)
    dist = np.where(x >= period - t, span - x + period, span - x)
    extra = log2_weight(dist)
    valid = np.isfinite(extra)
    u = jnp.where(valid, table[_t5_bucket_np(np.maximum(dist, 0))].T * LOG2E
                  + np.where(valid, extra, 0.0).astype(np.float32), NEG)
    rows = jnp.tile(u, (1, t))[:, :t * (period - 1)].reshape(u.shape[0], t, period - 1)[:, :, :cols]
    return rows.reshape(u.shape[0] * t, cols)


def _bias_a_sample(table_a, t):
    def log2_count(dist):
        count = np.zeros(dist.shape, np.int64)
        for w, d in zip(WINDOWS, DILATIONS):
            count += (dist >= 0) & (dist % d == 0) & (dist <= w)
        return np.where(count > 0, np.log2(np.maximum(count, 1)), -np.inf)

    return _sample_bias(table_a, WIN_A, t, log2_count)


def _bias_b_sample(table_b, t):
    return _sample_bias(table_b, WIN_B, t,
                        lambda dist: np.where((dist >= 0) & (dist < WIN_B), 0.0, -np.inf))


def _dest_kernel(route_ref, cnt_ref, tri_ref, dest_ref, meta_ref, run_scr, pst_scr):
    i = pl.program_id(0)
    tm = route_ref.shape[0]
    r = route_ref[...]
    lane = lax.broadcasted_iota(jnp.int32, (tm, LANES), 1)
    lanef = lane.astype(F32)
    oh0 = lanef == r[:, 0:1]
    oh1 = lanef == r[:, 1:2]
    ohf = jnp.concatenate([oh0, oh1], axis=0).astype(F32)

    @pl.when(i == 0)
    def _():
        cnt = jnp.broadcast_to(cnt_ref[...], (LANES, LANES))
        padded = jnp.floor((cnt + (MOE_ROWS - 1)) * (1.0 / MOE_ROWS)) * MOE_ROWS
        lane_e = lax.broadcasted_iota(jnp.int32, (LANES, LANES), 1)
        x = padded
        for sh in (1, 2, 4, 8, 16, 32, 64):
            x = x + jnp.where(lane_e >= sh, pltpu.roll(x, sh, 1), 0.0)
        pst_scr[...] = (x - padded)[0:1]
        run_scr[...] = jnp.zeros_like(run_scr)
        wide = lambda v: jnp.concatenate([v.T, v.T], axis=1)
        cnt_t, bend_t = wide(cnt), wide(x * (1.0 / MOE_ROWS))
        bstart_t = wide((x - padded) * (1.0 / MOE_ROWS))
        blk = lax.broadcasted_iota(jnp.int32, (LANES, 2 * LANES), 1).astype(F32)
        exp = lax.broadcasted_iota(jnp.int32, (LANES, 2 * LANES), 0)
        real = exp < N_EXPERTS
        blk_e = jnp.minimum(jnp.sum(jnp.where(real & (bend_t <= blk), 1.0, 0.0), axis=0, keepdims=True),
                            N_EXPERTS - 1.0)
        mine = exp.astype(F32) == blk_e
        within = blk[0:1] - jnp.sum(jnp.where(mine, bstart_t, 0.0), axis=0, keepdims=True)
        nvalid = jnp.clip(jnp.sum(jnp.where(mine, cnt_t, 0.0), axis=0, keepdims=True) - within * MOE_ROWS,
                          0.0, float(MOE_ROWS))
        n_used = jnp.max(jnp.where(real, bend_t, 0.0), axis=0, keepdims=True)
        meta_ref[...] = jnp.concatenate([blk_e, nvalid, n_used, jnp.zeros((5, 2 * LANES), F32)],
                                        axis=0).astype(jnp.int32)

    csum = jnp.dot(tri_ref[...], ohf.astype(BF16), preferred_element_type=F32)
    val = csum + (run_scr[...] + pst_scr[...] - 1.0)
    d0 = jnp.sum(jnp.where(oh0, val[:tm], 0.0), axis=-1, keepdims=True)
    d1 = jnp.sum(jnp.where(oh1, val[tm:], 0.0), axis=-1, keepdims=True)
    tile = jnp.where(lane == 0, d0, jnp.where(lane == 1, d1, 0.0))
    dest_ref[...] = tile.T[:8].astype(jnp.int32)
    run_scr[...] += jnp.sum(ohf, axis=0, keepdims=True)


def _dispatch(route, cnt):
    t = route.shape[0]
    tm = 512
    tri = (jnp.arange(2 * tm)[:, None] >= jnp.arange(2 * tm)[None, :]).astype(BF16)
    nblocks = -(-t * TOP_K // MOE_ROWS) + N_EXPERTS
    assert nblocks <= 2 * LANES
    dest, meta = pl.pallas_call(
        _dest_kernel,
        grid=(t // tm,),
        in_specs=[pl.BlockSpec((tm, LANES), lambda i: (i, 0)),
                  pl.BlockSpec((1, LANES), lambda i: (0, 0)),
                  pl.BlockSpec((2 * tm, 2 * tm), lambda i: (0, 0))],
        out_specs=[pl.BlockSpec((8, tm), lambda i: (0, i)),
                   pl.BlockSpec((8, 2 * LANES), lambda i: (0, 0))],
        out_shape=[jax.ShapeDtypeStruct((8, t), jnp.int32), jax.ShapeDtypeStruct((8, 2 * LANES), jnp.int32)],
        scratch_shapes=[pltpu.VMEM((1, LANES), F32), pltpu.VMEM((1, LANES), F32)],
        compiler_params=_cparams(("arbitrary",)),
        name="moe_dest",
    )(route, cnt, tri)
    return dest[:TOP_K], meta[0, :nblocks], meta[2, :1], meta[1, :nblocks]


def kernel(x_prompt, x_sample, cache_a_k, cache_a_v, cache_b_k, cache_b_v, rel_bias_table, attn_norm, w_in,
           w_out, attn_sinks, ffn_norm, w_router_group, b_router_group, w_router_expert, b_router_expert,
           w_gate, w_up, w_down, final_norm):
    s = x_prompt.shape[1]
    ns, ts = x_sample.shape[0], x_sample.shape[1]
    table_a = rel_bias_table[:, :H_A]
    table_b = rel_bias_table[:, H_A:]

    w = w_in[0]
    wqa, wka, wva, wqb, wkb, wvb = (w[:, 0:512], w[:, 512:1024], w[:, 1024:1536], w[:, 1536:2048],
                                    w[:, 2048:2176], w[:, 2176:2304])
    wqb = jnp.transpose(wqb.reshape(D_MODEL, KV_B, G_B, HEAD_DIM), (0, 2, 1, 3)).reshape(D_MODEL, 512)
    wp = jnp.concatenate([wka, wva, wqa, wqb, wkb, wvb], axis=1).astype(BF16)
    cscale = jnp.concatenate([jnp.ones((1, 1024), F32), jnp.full((1, 1024), SCALE * LOG2E, F32),
                              jnp.ones((1, 256), F32)], axis=1)
    wo = w_out[0]
    wo_b = jnp.transpose(wo[512:].reshape(KV_B, G_B, HEAD_DIM, D_MODEL), (1, 0, 2, 3)).reshape(512, D_MODEL)
    wo_p = jnp.concatenate([wo[:512], wo_b], axis=0).astype(BF16)
    wr = jnp.concatenate([w_router_group[0],
                          jnp.transpose(w_router_expert[0], (1, 0, 2)).reshape(D_MODEL, N_EXPERTS),
                          jnp.zeros((D_MODEL, LANES - N_GROUPS - N_EXPERTS), F32)], axis=1)
    wr_hi = wr.astype(BF16)
    wr = jnp.concatenate([wr_hi, wr_hi, (wr - wr_hi.astype(F32)).astype(BF16)], axis=0)
    br = jnp.concatenate([b_router_group[0], b_router_expert[0].reshape(N_EXPERTS),
                          jnp.zeros((LANES - N_GROUPS - N_EXPERTS,), F32)]).reshape(1, LANES)
    sinks2 = attn_sinks[0] * LOG2E
    sinks_gk = jnp.transpose(sinks2.reshape(KV_B, G_B), (1, 0)).reshape(H_B)
    sink_rows_p = jnp.repeat(sinks_gk, QB).reshape(G_B, 1, 2 * QB)
    sink_rows_s = jnp.repeat(sinks2, ts).reshape(H_B * ts, 1)
    emat = jnp.tile(jnp.arange(LANES)[:, None] == (jnp.arange(A_WIDTH)[None, :] // HEAD_DIM),
                    (3, 1)).astype(BF16)
    attn_g = attn_norm[0].reshape(1, D_MODEL)
    ffn_g = ffn_norm[0].reshape(1, D_MODEL)

    xp = x_prompt.reshape(s, D_MODEL)
    aperm, qb_p, kvb_p, akv32, bkv32 = _proj_prompt(xp, attn_g, wp, cscale)
    a4 = _attn_a_prompt(aperm, _bias_a_prompt(table_a), emat)
    ob_p = _attn_b_prompt(qb_p, kvb_p, _bias_b_prompt(table_b), sink_rows_p)

    xs = x_sample.reshape(ns * ts, D_MODEL)
    q_s, kv_s = _proj_sample(xs, attn_g, wp, cscale)
    akt = jnp.transpose(cache_a_k[0], (0, 2, 3, 1)).reshape(ns, A_WIDTH, WIN_A)
    avt = jnp.transpose(cache_a_v[0], (0, 2, 3, 1)).reshape(ns, A_WIDTH, WIN_A)
    bkt = jnp.transpose(cache_b_k[0], (0, 2, 3, 1)).reshape(ns, LANES, WIN_B)
    bvt = jnp.transpose(cache_b_v[0], (0, 2, 3, 1)).reshape(ns, LANES, WIN_B)

    x1, xn, route, cnt = _out_router(xp, a4, ob_p, xs, q_s.reshape(ns, ts, 1024), kv_s.reshape(ns, ts, 1280),
                                     akt, avt, bkt, bvt, _bias_a_sample(table_a, ts),
                                     _bias_b_sample(table_b, ts), sink_rows_s, wo_p, ffn_g, wr, br)
    dest2, blk_e, n_used, nvalid = _dispatch(route, cnt)
    xb = _sc_scatter_rows(xn, dest2, blk_e.shape[0] * MOE_ROWS)
    yb = _experts(blk_e, n_used, nvalid, xb, w_gate[0], w_up[0], w_down[0])
    y_p, y_s = _combine_norm(x1, _sc_gather_rows(yb, dest2.reshape(-1)), route, final_norm.reshape(1, D_MODEL), s)

    y_prompt = y_p.reshape(1, s, D_MODEL)
    y_sample = y_s.reshape(ns, ts, D_MODEL)
    keep_a, keep_b = min(WIN_A, s), min(WIN_B, s)
    pak = akv32[s - keep_a:, :512].reshape(1, 1, keep_a, H_A, HEAD_DIM)
    pav = akv32[s - keep_a:, 512:].reshape(1, 1, keep_a, H_A, HEAD_DIM)
    pbk = bkv32[s - keep_b:, :128].reshape(1, 1, keep_b, KV_B, HEAD_DIM)
    pbv = bkv32[s - keep_b:, 128:].reshape(1, 1, keep_b, KV_B, HEAD_DIM)
    sak = kv_s[:, 0:512].reshape(1, ns, ts, H_A, HEAD_DIM)
    sav = kv_s[:, 512:1024].reshape(1, ns, ts, H_A, HEAD_DIM)
    sbk = kv_s[:, 1024:1152].reshape(1, ns, ts, KV_B, HEAD_DIM)
    sbv = kv_s[:, 1152:1280].reshape(1, ns, ts, KV_B, HEAD_DIM)
    return (y_prompt, y_sample, pak, pav, pbk, pbv, sak, sav, sbk, sbv)
```

```python
import functools
import math

import jax
import jax.numpy as jnp
import numpy as np
from jax import lax
from jax.experimental import pallas as pl
from jax.experimental.pallas import tpu as pltpu
from jax.experimental.pallas import tpu_sc as plsc

D_MODEL = 1024
HEAD_DIM = 64
H_A = 8
H_B = 8
KV_B = 2
G_B = 4
DILATIONS = (1, 4, 16)
WINDOWS = (128, 512, 2048)
WIN_A = 2048
WIN_B = 128
NUM_BUCKETS = 32
MAX_DISTANCE = 2048
N_GROUPS = 4
EXPERTS_PER_GROUP = 8
N_EXPERTS = 32
TOP_K = 2
D_EXPERT = 512
EPS = 1e-5
SCALE = HEAD_DIM ** -0.5
PAST_LEN = 16384

LANES = 128
SPAN = 2048
QB = 128
NCHUNK = 9
A_WIDTH = H_A * HEAD_DIM
MOE_ROWS = 512
SC_CORES = 2
SC_SUBCORES = 16
SC_WORKERS = SC_CORES * SC_SUBCORES
SC_WINDOW = 64
SC_SCATTER_WINDOW = 32
NEG = -1e30
LOG2E = math.log2(math.e)
B_STEP = 512
VMEM_LIMIT = 56 * 1024 * 1024
CACHE_BUFFERS = 3
OUT_ROUTER_VMEM = 60 * 1024 * 1024

F32 = jnp.float32
BF16 = jnp.bfloat16


def _t5_bucket_np(dist):
    dist = np.asarray(dist, np.int64)
    max_exact = NUM_BUCKETS // 2
    d = np.maximum(dist, 1).astype(np.float32)
    ratio = np.log(d / np.float32(max_exact)) / np.float32(math.log(MAX_DISTANCE / max_exact))
    large = max_exact + (ratio * np.float32(NUM_BUCKETS - max_exact)).astype(np.int32)
    large = np.minimum(large, NUM_BUCKETS - 1)
    return np.where(dist < max_exact, dist, large).astype(np.int32)


def _cparams(sem, vmem=VMEM_LIMIT):
    return pltpu.CompilerParams(dimension_semantics=sem, vmem_limit_bytes=vmem)


def _proj_prompt_kernel(x_ref, g_ref, w_ref, cs_ref, aperm_ref, qb_ref, kvb_ref, akv_ref, bkv_ref,
                        h_scr, p_scr):
    n = pl.program_id(1)

    @pl.when(n == 0)
    def _():
        x = x_ref[...]
        ms = jnp.mean(x * x, axis=-1, keepdims=True)
        h_scr[...] = (x * lax.rsqrt(ms + EPS) * g_ref[...]).astype(BF16)

    p = jnp.dot(h_scr[...], w_ref[...], preferred_element_type=F32) * cs_ref[...]

    @pl.when(n < 6)
    def _():
        aperm_ref[0] = p.astype(BF16)
        p_scr[0, 0] = p[:, :LANES]
        p_scr[0, 1] = p[:, LANES:]
        quarter = SPAN // 4
        for r in range(4):
            lo = p_scr[0, 0, pl.ds(r, quarter, stride=4), :]
            hi = p_scr[0, 1, pl.ds(r, quarter, stride=4), :]
            p_scr[1, 0, r * quarter:(r + 1) * quarter, :] = lo
            p_scr[1, 1, r * quarter:(r + 1) * quarter, :] = hi
            aperm_ref[1, r * quarter:(r + 1) * quarter, :] = jnp.concatenate([lo, hi], axis=1).astype(BF16)
        for r16 in range(16):
            start = (r16 % 4) * quarter + r16 // 4
            t = jnp.concatenate([p_scr[1, 0, pl.ds(start, QB, stride=4), :],
                                 p_scr[1, 1, pl.ds(start, QB, stride=4), :]], axis=1)
            aperm_ref[2, r16 * QB:(r16 + 1) * QB, :] = t.astype(BF16)

    @pl.when(n < 4)
    def _():
        akv_ref[...] = p

    @pl.when(jnp.logical_or(n == 6, n == 7))
    def _():
        qb_ref[...] = p.astype(BF16)

    @pl.when(n == 8)
    def _():
        kvb_ref[...] = p.astype(BF16)
        bkv_ref[...] = p


def _proj_prompt(x, gamma, w, cscale):
    s = x.shape[0]
    nspan = s // SPAN
    return pl.pallas_call(
        _proj_prompt_kernel,
        grid=(nspan, NCHUNK),
        in_specs=[
            pl.BlockSpec((SPAN, D_MODEL), lambda b, n: (b, 0)),
            pl.BlockSpec((1, D_MODEL), lambda b, n: (0, 0)),
            pl.BlockSpec((D_MODEL, 256), lambda b, n: (0, n)),
            pl.BlockSpec((1, 256), lambda b, n: (0, n)),
        ],
        out_specs=[
            pl.BlockSpec((3, SPAN, 256), lambda b, n: (0, b, jnp.minimum(n, 5))),
            pl.BlockSpec((SPAN, 256), lambda b, n: (b, jnp.clip(n - 6, 0, 1))),
            pl.BlockSpec((SPAN, 256), lambda b, n: (b, 0)),
            pl.BlockSpec((SPAN, 256), lambda b, n: (b, jnp.minimum(n, 3))),
            pl.BlockSpec((SPAN, 256), lambda b, n: (b, 0)),
        ],
        out_shape=[
            jax.ShapeDtypeStruct((3, s, 3 * A_WIDTH), BF16),
            jax.ShapeDtypeStruct((s, 512), BF16),
            jax.ShapeDtypeStruct((s, 256), BF16),
            jax.ShapeDtypeStruct((s, 1024), F32),
            jax.ShapeDtypeStruct((s, 256), F32),
        ],
        scratch_shapes=[pltpu.VMEM((SPAN, D_MODEL), BF16), pltpu.VMEM((2, 2, SPAN, LANES), F32)],
        compiler_params=_cparams(("arbitrary", "arbitrary")),
        name="proj_prompt",
    )(x, gamma, w, cscale)


def _proj_sample_kernel(x_ref, g_ref, w_ref, cs_ref, q_ref, kv_ref):
    x = x_ref[...]
    ms = jnp.mean(x * x, axis=-1, keepdims=True)
    h = (x * lax.rsqrt(ms + EPS) * g_ref[...]).astype(BF16)
    p = jnp.dot(h, w_ref[...], preferred_element_type=F32) * cs_ref[...]
    kv_ref[:, :1024] = p[:, :1024]
    kv_ref[:, 1024:] = p[:, 2048:]
    q_ref[...] = p[:, 1024:2048]


def _proj_sample(x, gamma, w, cscale):
    t = x.shape[0]
    tm = 512
    return pl.pallas_call(
        _proj_sample_kernel,
        grid=(t // tm,),
        in_specs=[
            pl.BlockSpec((tm, D_MODEL), lambda i: (i, 0)),
            pl.BlockSpec((1, D_MODEL), lambda i: (0, 0)),
            pl.BlockSpec((D_MODEL, 2304), lambda i: (0, 0)),
            pl.BlockSpec((1, 2304), lambda i: (0, 0)),
        ],
        out_specs=[
            pl.BlockSpec((tm, 1024), lambda i: (i, 0)),
            pl.BlockSpec((tm, 1280), lambda i: (i, 0)),
        ],
        out_shape=[
            jax.ShapeDtypeStruct((t, 1024), F32),
            jax.ShapeDtypeStruct((t, 1280), F32),
        ],
        compiler_params=_cparams(("arbitrary",)),
        name="proj_sample",
    )(x, gamma, w, cscale)


def _spread_heads(w, e3_ref):
    hi = w.astype(BF16)
    r1 = w - hi.astype(F32)
    mid = r1.astype(BF16)
    low = (r1 - mid.astype(F32)).astype(BF16)
    return jnp.dot(jnp.concatenate([hi, mid, low], axis=1), e3_ref[...], preferred_element_type=F32)


def _pair_tile(q2, kk, vv, bias_t, lo, sink=None):
    zero = jnp.zeros_like(q2)
    qq = jnp.concatenate([jnp.where(lo, q2, zero), jnp.where(lo, zero, q2)], axis=0)
    st = lax.dot_general(kk, qq, (((1,), (1,)), ((), ())), preferred_element_type=F32)
    st = st + bias_t
    m = jnp.max(st, axis=0, keepdims=True)
    if sink is not None:
        m = jnp.maximum(m, sink)
    p = jnp.exp2(st - m)
    den = jnp.sum(p, axis=0, keepdims=True)
    if sink is not None:
        den = den + jnp.exp2(sink - m)
    pn = (p * (1.0 / den)).astype(BF16)
    o = lax.dot_general(pn, vv, (((0,), (0,)), ((), ())), preferred_element_type=F32)
    return jnp.where(lo, o[:QB], o[QB:]), m + jnp.log2(den)


def _fill_band_tiles(h_ref, bias_scr):
    nk = 2 * QB
    prev = lax.broadcasted_iota(jnp.int32, (nk, nk), 0) < QB
    for pair in range(h_ref.shape[0] // 2):
        halves = []
        for hh in range(2):
            row = h_ref[2 * pair + hh:2 * pair + hh + 1, :]
            band = pltpu.roll(jnp.broadcast_to(row, (nk, nk)), 0, 1, stride=1, stride_axis=0)
            halves.append(band[:, :QB])
        tile = jnp.concatenate(halves, axis=1)
        bias_scr[0, pair] = tile
        bias_scr[1, pair] = jnp.where(prev, NEG, tile)


def _attn_a_kernel(q_ref, kvc_ref, kvp_ref, h_ref, e_ref, out_ref, o_scr, st_scr, bias_scr):
    b = pl.program_id(0)
    g = pl.program_id(1)
    nblk = jnp.where(g == 0, 16, jnp.where(g == 1, 4, 1))
    lane = lax.broadcasted_iota(jnp.int32, (QB, LANES), 1)
    lo = lane < HEAD_DIM

    @pl.when(b == 0)
    def _():
        _fill_band_tiles(h_ref, bias_scr.at[g])

    bias_ref = bias_scr.at[g]

    for cb in range(SPAN // QB):
        first = lax.rem(jnp.int32(cb), nblk) == 0
        rows = slice(cb * QB, (cb + 1) * QB)
        prow_c = max(cb - 1, 0) * QB
        prow_p = pl.multiple_of(jnp.where(first, cb + nblk - 1, 0) * QB, QB)
        variant = jnp.logical_and(first, b == 0).astype(jnp.int32)
        stats = []
        for hp in range(4):
            ks = slice(hp * LANES, (hp + 1) * LANES)
            vs = slice(A_WIDTH + hp * LANES, A_WIDTH + (hp + 1) * LANES)
            kp = jnp.where(first, kvp_ref[pl.ds(prow_p, QB), ks], kvc_ref[prow_c:prow_c + QB, ks])
            vp = jnp.where(first, kvp_ref[pl.ds(prow_p, QB), vs], kvc_ref[prow_c:prow_c + QB, vs])
            kk = jnp.concatenate([kp, kvc_ref[rows, ks]], axis=0)
            vv = jnp.concatenate([vp, kvc_ref[rows, vs]], axis=0)
            o, lse = _pair_tile(q_ref[rows, ks], kk, vv, bias_ref[variant, hp], lo)
            o_scr[g, hp, rows, :] = o
            stats += [lse[:, :QB], lse[:, QB:]]
        sm = jnp.concatenate(stats + [jnp.zeros((LANES - H_A, QB), F32)], axis=0)
        st_scr[g, rows, :] = sm.T

    @pl.when(g == 2)
    def _():
        def merge(c, carry):
            r2 = lax.rem(c, 4) * (SPAN // 4) + c // 4
            r3 = pl.multiple_of(c * QB, QB)
            l1 = st_scr[0, pl.ds(c, QB, stride=16), :]
            l2 = st_scr[1, pl.ds(r2, QB, stride=4), :]
            l3 = st_scr[2, pl.ds(r3, QB), :]
            mx = jnp.maximum(jnp.maximum(l1, l2), l3)
            w1 = jnp.exp2(l1 - mx)
            w2 = jnp.exp2(l2 - mx)
            w3 = jnp.exp2(l3 - mx)
            tot = w1 + w2 + w3
            a1 = _spread_heads(w1 / tot, e_ref)
            a2 = _spread_heads(w2 / tot, e_ref)
            a3 = _spread_heads(w3 / tot, e_ref)
            for hp in range(4):
                sl = slice(hp * LANES, (hp + 1) * LANES)
                o1 = o_scr[0, hp, pl.ds(c, QB, stride=16), :]
                o2 = o_scr[1, hp, pl.ds(r2, QB, stride=4), :]
                o3 = o_scr[2, hp, pl.ds(r3, QB), :]
                out_ref[hp, pl.ds(c, QB, stride=16), :] = a1[:, sl] * o1 + a2[:, sl] * o2 + a3[:, sl] * o3
            return carry

        lax.fori_loop(0, 16, merge, 0, unroll=4)


def _attn_a_prompt(aperm, bias_a, emat):
    s = aperm.shape[1]
    nspan = s // SPAN
    return pl.pallas_call(
        _attn_a_kernel,
        grid=(nspan, 3),
        in_specs=[
            pl.BlockSpec((None, SPAN, A_WIDTH), lambda b, g: (g, b, 2)),
            pl.BlockSpec((None, SPAN, 2 * A_WIDTH), lambda b, g: (g, b, 0)),
            pl.BlockSpec((None, SPAN, 2 * A_WIDTH), lambda b, g: (g, jnp.maximum(b - 1, 0), 0)),
            pl.BlockSpec((None, H_A, 2 * QB), lambda b, g: (g, 0, 0)),
            pl.BlockSpec((3 * LANES, A_WIDTH), lambda b, g: (0, 0)),
        ],
        out_specs=pl.BlockSpec((4, SPAN, LANES), lambda b, g: (0, b, 0)),
        out_shape=jax.ShapeDtypeStruct((4, s, LANES), F32),
        scratch_shapes=[pltpu.VMEM((3, 4, SPAN, LANES), F32), pltpu.VMEM((3, SPAN, LANES), F32),
                        pltpu.VMEM((3, 2, 4, 2 * QB, 2 * QB), F32)],
        compiler_params=_cparams(("arbitrary", "arbitrary")),
        name="attn_a_prompt",
    )(aperm, aperm, aperm, bias_a, emat)


def _attn_b_kernel(q_ref, kvc_ref, kvp_ref, h_ref, sink_ref, out_ref, bias_ref):
    i = pl.program_id(0)
    lane = lax.broadcasted_iota(jnp.int32, (QB, LANES), 1)
    lo = lane < HEAD_DIM

    @pl.when(i == 0)
    def _():
        _fill_band_tiles(h_ref, bias_ref)

    variant = (i == 0).astype(jnp.int32)
    for j in range(B_STEP // QB):
        rows = slice(j * QB, (j + 1) * QB)
        if j == 0:
            kp, vp = kvp_ref[:, :LANES], kvp_ref[:, LANES:]
        else:
            kp, vp = kvc_ref[(j - 1) * QB:j * QB, :LANES], kvc_ref[(j - 1) * QB:j * QB, LANES:]
        kk = jnp.concatenate([kp, kvc_ref[rows, :LANES]], axis=0)
        vv = jnp.concatenate([vp, kvc_ref[rows, LANES:]], axis=0)
        for g in range(G_B):
            bias_t = bias_ref[variant, g] if j == 0 else bias_ref[0, g]
            o, _ = _pair_tile(q_ref[rows, g * LANES:(g + 1) * LANES], kk, vv, bias_t, lo, sink=sink_ref[g])
            out_ref[rows, g * LANES:(g + 1) * LANES] = o.astype(BF16)


def _attn_b_prompt(qb, kvb, bias_b, sink_rows):
    s = qb.shape[0]
    per = B_STEP // QB
    return pl.pallas_call(
        _attn_b_kernel,
        grid=(s // B_STEP,),
        in_specs=[
            pl.BlockSpec((B_STEP, 512), lambda i: (i, 0)),
            pl.BlockSpec((B_STEP, 256), lambda i: (i, 0)),
            pl.BlockSpec((QB, 256), lambda i: (jnp.maximum(i * per - 1, 0), 0)),
            pl.BlockSpec((H_B, 2 * QB), lambda i: (0, 0)),
            pl.BlockSpec((G_B, 1, 2 * QB), lambda i: (0, 0, 0)),
        ],
        out_specs=pl.BlockSpec((B_STEP, 512), lambda i: (i, 0)),
        out_shape=jax.ShapeDtypeStruct((s, 512), BF16),
        scratch_shapes=[pltpu.VMEM((2, G_B, 2 * QB, 2 * QB), F32)],
        compiler_params=_cparams(("arbitrary",)),
        name="attn_b_prompt",
    )(qb, kvb, kvb, bias_b, sink_rows)


def _sample_attention(q, kvn, akt, avt, bkt, bvt, cba, cbb, sink):
    t = q.shape[0]
    kvn_p = jnp.concatenate([kvn, jnp.zeros((LANES - t, kvn.shape[1]), F32)], axis=0).astype(BF16)
    lane_a = lax.broadcasted_iota(jnp.int32, (t, A_WIDTH), 1) // HEAD_DIM

    qa = q[:, :A_WIDTH]
    qbd = jnp.concatenate([jnp.where(lane_a == h, qa, 0.0) for h in range(H_A)], axis=0).astype(BF16)
    s_c = jnp.dot(qbd, akt.astype(BF16), preferred_element_type=F32)
    s_n = lax.dot_general(qbd, kvn_p[:, :A_WIDTH], (((1,), (1,)), ((), ())), preferred_element_type=F32)
    s = jnp.concatenate([s_c, s_n], axis=1) + cba
    m = jnp.max(s, axis=-1, keepdims=True)
    p = jnp.exp2(s - m)
    l = jnp.sum(p, axis=-1, keepdims=True)
    pb = p.astype(BF16)
    o_n = jnp.dot(pb[:, WIN_A:], kvn_p[:, A_WIDTH:2 * A_WIDTH], preferred_element_type=F32)
    o_all = lax.dot_general(pb[:, :WIN_A], avt.astype(BF16), (((1,), (1,)), ((), ())),
                            preferred_element_type=F32) + o_n
    o_sel = jnp.zeros((t, A_WIDTH), F32)
    l_b = jnp.ones((t, A_WIDTH), F32)
    for h in range(H_A):
        sel = lane_a == h
        o_sel = jnp.where(sel, o_all[h * t:(h + 1) * t], o_sel)
        l_b = jnp.where(sel, l[h * t:(h + 1) * t], l_b)
    oa = o_sel / l_b

    lane_b = lax.broadcasted_iota(jnp.int32, (G_B * t, LANES), 1)
    lo = lane_b < HEAD_DIM
    qb2 = jnp.concatenate([q[:, A_WIDTH + g * LANES:A_WIDTH + (g + 1) * LANES] for g in range(G_B)], axis=0)
    qm = jnp.concatenate([jnp.where(lo, qb2, 0.0), jnp.where(lo, 0.0, qb2)], axis=0).astype(BF16)
    kb_n = kvn_p[:, 2 * A_WIDTH:2 * A_WIDTH + LANES]
    vb_n = kvn_p[:, 2 * A_WIDTH + LANES:]
    sb_c = jnp.dot(qm, bkt.astype(BF16), preferred_element_type=F32)
    sb_n = lax.dot_general(qm, kb_n, (((1,), (1,)), ((), ())), preferred_element_type=F32)
    sb = jnp.concatenate([sb_c, sb_n], axis=1) + cbb
    mb = jnp.maximum(jnp.max(sb, axis=-1, keepdims=True), sink)
    pbb = jnp.exp2(sb - mb)
    den = jnp.sum(pbb, axis=-1, keepdims=True) + jnp.exp2(sink - mb)
    pbb = pbb.astype(BF16)
    ob = lax.dot_general(pbb[:, :WIN_B], bvt.astype(BF16), (((1,), (1,)), ((), ())),
                         preferred_element_type=F32)
    ob = (ob + jnp.dot(pbb[:, WIN_B:], vb_n, preferred_element_type=F32)) / den
    half = G_B * t
    lo8 = lo[:t]
    ob = jnp.concatenate([jnp.where(lo8, ob[g * t:(g + 1) * t], ob[half + g * t:half + (g + 1) * t])
                          for g in range(G_B)], axis=1)
    return oa, ob


def _route(logits):
    lane = lax.broadcasted_iota(jnp.int32, logits.shape, 1).astype(F32)
    big = jnp.float32(1 << 20)
    ninf = jnp.float32(-jnp.inf)
    gmask = lane < N_GROUPS
    lg = jnp.where(gmask, logits, ninf)
    gmax = jnp.max(lg, axis=-1, keepdims=True)
    grp = jnp.min(jnp.where(lg == gmax, lane, big), axis=-1, keepdims=True)
    pg_top = 1.0 / jnp.sum(jnp.exp(lg - gmax), axis=-1, keepdims=True)
    e0 = N_GROUPS + grp * EXPERTS_PER_GROUP
    emask = jnp.logical_and(lane >= e0, lane < e0 + EXPERTS_PER_GROUP)
    le = jnp.where(emask, logits, ninf)
    emax = jnp.max(le, axis=-1, keepdims=True)
    esum = jnp.sum(jnp.exp(le - emax), axis=-1, keepdims=True)
    i1 = jnp.min(jnp.where(le == emax, lane, big), axis=-1, keepdims=True)
    le2 = jnp.where(lane == i1, ninf, le)
    e2max = jnp.max(le2, axis=-1, keepdims=True)
    i2 = jnp.min(jnp.where(le2 == e2max, lane, big), axis=-1, keepdims=True)
    p1 = 1.0 / esum
    p2 = jnp.exp(e2max - emax) / esum
    g1 = pg_top * p1 / (p1 + p2)
    g2 = pg_top * p2 / (p1 + p2)
    out = jnp.where(lane == 0, i1 - N_GROUPS, 0.0)
    out = jnp.where(lane == 1, i2 - N_GROUPS, out)
    out = jnp.where(lane == 2, g1, out)
    out = jnp.where(lane == 3, g2, out)
    return out


def _pack_bf16_pairs(x):
    half = x.shape[1] // 2

    def rne(v):
        bits = lax.bitcast_convert_type(v, jnp.int32)
        return bits + 0x7FFF + (lax.shift_right_logical(bits, 16) & 1)

    lo = lax.shift_right_logical(rne(x[:, :half]), 16)
    hi = rne(x[:, half:]) & jnp.int32(-65536)
    return lo | hi


def _unpack_bf16_pairs(w):
    lo = lax.bitcast_convert_type(lax.shift_left(w, 16), F32)
    hi = lax.bitcast_convert_type(w & jnp.int32(-65536), F32)
    return jnp.concatenate([lo, hi], axis=1)


def _out_router_kernel(xp_ref, ap_ref, bp_ref, xs_ref, q_ref, kvn_ref, akt_hbm, avt_hbm, bkt_ref, bvt_ref,
                       cba_ref, cbb_ref, sink_ref, wo_ref, g_ref, wr_ref, br_ref,
                       x1_ref, xn_ref, route_ref, cnt_ref,
                       xcat_scr, mix_scr, kbuf, vbuf, sem, *, prompt_tiles, decode_tiles, seqs_per_step):
    i = pl.program_id(0)
    seqs = prompt_tiles * seqs_per_step

    def cache_copies(n, slot):
        return (pltpu.make_async_copy(akt_hbm.at[n], kbuf.at[slot], sem.at[0, slot]),
                pltpu.make_async_copy(avt_hbm.at[n], vbuf.at[slot], sem.at[1, slot]))

    @pl.when(i == 0)
    def _():
        cnt_ref[...] = jnp.zeros_like(cnt_ref)
        xcat_scr[...] = jnp.zeros_like(xcat_scr)
        for n0 in range(2):
            for c in cache_copies(n0, n0):
                c.start()

    @pl.when(i == prompt_tiles)
    def _():
        for n1 in (seqs, seqs + 1):
            for c in cache_copies(seqs - 1, n1 % CACHE_BUFFERS):
                c.wait()

    pslot = lax.rem(i, 2)

    def route_previous():
        logits = jnp.dot(xcat_scr[1 - pslot], wr_ref[...], preferred_element_type=F32)
        route = _route(logits + br_ref[...])
        route_ref[...] = route
        lanef = lax.broadcasted_iota(jnp.int32, route.shape, 1).astype(F32)
        hits = (lanef == route[:, 0:1]).astype(F32) + (lanef == route[:, 1:2]).astype(F32)
        cnt_ref[...] += jnp.sum(hits, axis=0, keepdims=True) * (i > 0).astype(F32)

    def project(x_ref, mix):
        x1 = x_ref[...] + jnp.dot(mix, wo_ref[...], preferred_element_type=F32)
        x1_ref[...] = x1
        ms = jnp.mean(x1 * x1, axis=-1, keepdims=True)
        xn = x1 * lax.rsqrt(ms + EPS) * g_ref[...]
        xn_ref[...] = _pack_bf16_pairs(xn)
        xh = xn.astype(BF16)
        xl = (xn - xh.astype(F32)).astype(BF16)
        xcat_scr[pslot] = jnp.concatenate([xh, xl, xh], axis=1)

    @pl.when(i < prompt_tiles)
    def _():
        route_previous()
        mix = jnp.concatenate([ap_ref[0], ap_ref[1], ap_ref[2], ap_ref[3]], axis=1).astype(BF16)
        project(xp_ref, jnp.concatenate([mix, bp_ref[...]], axis=1))
        t = q_ref.shape[1]
        for s in range(seqs_per_step):
            n = i * seqs_per_step + s
            slot = lax.rem(n, CACHE_BUFFERS)
            for c in cache_copies(n, slot):
                c.wait()
            for c in cache_copies(jnp.minimum(n + 2, seqs - 1), lax.rem(n + 2, CACHE_BUFFERS)):
                c.start()
            oa, ob = _sample_attention(q_ref[s], kvn_ref[s], kbuf[slot], vbuf[slot], bkt_ref[s], bvt_ref[s],
                                       cba_ref[...], cbb_ref[...], sink_ref[...])
            row = pl.multiple_of(n * t, t)
            mix_scr[pl.ds(row, t), :A_WIDTH] = oa
            mix_scr[pl.ds(row, t), A_WIDTH:] = ob

    @pl.when(i >= prompt_tiles)
    def _():
        route_previous()
        tm = xs_ref.shape[0]
        row = pl.multiple_of(jnp.clip(i - prompt_tiles, 0, decode_tiles - 1) * tm, tm)
        project(xs_ref, mix_scr[pl.ds(row, tm), :].astype(BF16))


def _out_router(xp, a4p, bp, xs, q3, kvn3, akt, avt, bkt, bvt, cbias_a, cbias_b, sink_rows, wo, gamma, wr, br):
    tp, tsm = xp.shape[0], xs.shape[0]
    ns, ts = q3.shape[0], q3.shape[1]
    tm = 512
    npt, nst = tp // tm, tsm // tm
    nt = npt + nst
    t = tp + tsm
    sps = ns // npt
    assert sps * npt == ns and ns * ts == tsm and ns >= CACHE_BUFFERS
    pmap = lambda i: (jnp.minimum(i, npt - 1), 0)
    pmap3 = lambda i: (jnp.minimum(i, npt - 1), 0, 0)
    smap = lambda i: (jnp.clip(i - npt, 0, nst - 1), 0)
    cur = lambda i: (jnp.minimum(i, nt - 1), 0)
    const = lambda i: (0, 0)
    return pl.pallas_call(
        functools.partial(_out_router_kernel, prompt_tiles=npt, decode_tiles=nst, seqs_per_step=sps),
        grid=(nt + 1,),
        in_specs=[
            pl.BlockSpec((tm, D_MODEL), pmap),
            pl.BlockSpec((4, tm, LANES), lambda i: (0, jnp.minimum(i, npt - 1), 0)),
            pl.BlockSpec((tm, 512), pmap),
            pl.BlockSpec((tm, D_MODEL), smap),
            pl.BlockSpec((sps, ts, 1024), pmap3),
            pl.BlockSpec((sps, ts, 1280), pmap3),
            pl.BlockSpec(memory_space=pl.ANY),
            pl.BlockSpec(memory_space=pl.ANY),
            pl.BlockSpec((sps, LANES, WIN_B), pmap3),
            pl.BlockSpec((sps, LANES, WIN_B), pmap3),
            pl.BlockSpec((H_A * ts, WIN_A + LANES), const),
            pl.BlockSpec((H_B * ts, WIN_B + LANES), const),
            pl.BlockSpec((H_B * ts, 1), const),
            pl.BlockSpec((D_MODEL, D_MODEL), const),
            pl.BlockSpec((1, D_MODEL), const),
            pl.BlockSpec((3 * D_MODEL, LANES), const),
            pl.BlockSpec((1, LANES), const),
        ],
        out_specs=[
            pl.BlockSpec((tm, D_MODEL), cur),
            pl.BlockSpec((tm, D_MODEL // 2), cur),
            pl.BlockSpec((tm, LANES), lambda i: (jnp.maximum(i - 1, 0), 0)),
            pl.BlockSpec((1, LANES), const),
        ],
        scratch_shapes=[pltpu.VMEM((2, tm, 3 * D_MODEL), BF16), pltpu.VMEM((tsm, D_MODEL), F32),
                        pltpu.VMEM((CACHE_BUFFERS, A_WIDTH, WIN_A), F32),
                        pltpu.VMEM((CACHE_BUFFERS, A_WIDTH, WIN_A), F32),
                        pltpu.SemaphoreType.DMA((2, CACHE_BUFFERS))],
        out_shape=[
            jax.ShapeDtypeStruct((t, D_MODEL), F32),
            jax.ShapeDtypeStruct((t, D_MODEL // 2), jnp.int32),
            jax.ShapeDtypeStruct((t, LANES), F32),
            jax.ShapeDtypeStruct((1, LANES), F32),
        ],
        compiler_params=_cparams(("arbitrary",), vmem=OUT_ROUTER_VMEM),
        name="out_router",
    )(xp, a4p, bp, xs, q3, kvn3, akt, avt, bkt, bvt, cbias_a, cbias_b, sink_rows, wo, gamma, wr, br)


def _sc_gather_rows(table, idx):
    b = idx.shape[0]
    d = table.shape[1]
    w = SC_WINDOW
    per_worker = b // SC_WORKERS
    nwin = per_worker // w
    assert per_worker * SC_WORKERS == b and nwin * w == per_worker
    mesh = plsc.VectorSubcoreMesh(core_axis_name="c", subcore_axis_name="s")

    @functools.partial(
        pl.kernel, mesh=mesh,
        out_type=jax.ShapeDtypeStruct((b, d), table.dtype),
        scratch_types=[pltpu.VMEM((nwin, w), jnp.int32), pltpu.VMEM((2, w, d), table.dtype),
                       pltpu.SemaphoreType.DMA((2,)), pltpu.SemaphoreType.DMA((2,))],
        name="sc_gather_rows",
    )
    def gather(table_hbm, idx_hbm, out_hbm, idx_v, rows_v, sem_in, sem_out):
        wid = lax.axis_index("s") * SC_CORES + lax.axis_index("c")
        base = wid * per_worker
        pltpu.sync_copy(idx_hbm.at[wid], idx_v)

        def fetch(j):
            return pltpu.make_async_copy(table_hbm.at[idx_v.at[j]], rows_v.at[j % 2], sem_in.at[j % 2])

        def flush(j):
            return pltpu.make_async_copy(rows_v.at[j % 2], out_hbm.at[pl.ds(base + j * w, w)],
                                         sem_out.at[j % 2])

        fetch(0).start()
        for j in range(nwin):
            fetch(j).wait()
            if j + 1 < nwin:
                if j >= 1:
                    flush(j - 1).wait()
                fetch(j + 1).start()
            flush(j).start()
        for j in range(max(nwin - 2, 0), nwin):
            flush(j).wait()

    return gather(table, idx.reshape(SC_WORKERS, nwin, w))


def _sc_scatter_rows(x, dest2, nrows):
    t, d = x.shape
    w = SC_SCATTER_WINDOW
    per_worker = t // SC_WORKERS
    nwin = per_worker // w
    assert per_worker * SC_WORKERS == t and nwin * w == per_worker
    mesh = plsc.VectorSubcoreMesh(core_axis_name="c", subcore_axis_name="s")

    @functools.partial(
        pl.kernel, mesh=mesh,
        out_type=jax.ShapeDtypeStruct((nrows, d), x.dtype),
        scratch_types=[pltpu.VMEM((TOP_K, nwin, w), jnp.int32), pltpu.VMEM((2, w, d), x.dtype),
                       pltpu.SemaphoreType.DMA((2,)), pltpu.SemaphoreType.DMA((2,))],
        name="sc_scatter_rows",
    )
    def scatter(x_hbm, dest_hbm, out_hbm, idx_v, rows_v, sem_in, sem_out):
        wid = lax.axis_index("s") * SC_CORES + lax.axis_index("c")
        base = wid * per_worker
        for k in range(TOP_K):
            pltpu.sync_copy(dest_hbm.at[k, wid], idx_v.at[k])

        def fetch(j):
            return pltpu.make_async_copy(x_hbm.at[pl.ds(base + j * w, w)], rows_v.at[j % 2], sem_in.at[j % 2])

        def spread(j, k):
            return pltpu.make_async_copy(rows_v.at[j % 2], out_hbm.at[idx_v.at[k, j]], sem_out.at[j % 2])

        fetch(0).start()
        for j in range(nwin):
            fetch(j).wait()
            if j + 1 < nwin:
                if j >= 1:
                    for k in range(TOP_K):
                        spread(j - 1, k).wait()
                fetch(j + 1).start()
            for k in range(TOP_K):
                spread(j, k).start()
        for j in range(max(nwin - 2, 0), nwin):
            for k in range(TOP_K):
                spread(j, k).wait()

    return scatter(x, dest2.reshape(TOP_K, SC_WORKERS, nwin, w))


def _expert_kernel(be_ref, nu_ref, nv_ref, x_ref, wg_ref, wu_ref, wd_ref, o_ref, wg_s, wu_s, wd_s):
    i = pl.program_id(0)
    used = i < nu_ref[0]
    changed = jnp.logical_or(i == 0, be_ref[i] != be_ref[jnp.maximum(i - 1, 0)])

    @pl.when(jnp.logical_and(used, changed))
    def _():
        wg_s[...] = wg_ref[...].astype(BF16)
        wu_s[...] = wu_ref[...].astype(BF16)
        wd_s[...] = wd_ref[...].astype(BF16)

    @pl.when(used)
    def _():
        row = lax.broadcasted_iota(jnp.int32, x_ref.shape, 0)
        x = _unpack_bf16_pairs(jnp.where(row < nv_ref[i], x_ref[...], 0)).astype(BF16)
        gate = jnp.dot(x, wg_s[...], preferred_element_type=F32)
        up = jnp.dot(x, wu_s[...], preferred_element_type=F32)
        h = (gate * jax.nn.sigmoid(gate) * up).astype(BF16)
        o_ref[...] = _pack_bf16_pairs(jnp.dot(h, wd_s[...], preferred_element_type=F32))

    @pl.when(jnp.logical_not(used))
    def _():
        o_ref[...] = jnp.zeros_like(o_ref)


def _experts(blk_e, n_used, nvalid, xb, w_gate, w_up, w_down):
    rows = xb.shape[0]
    nblocks = rows // MOE_ROWS
    grid_spec = pltpu.PrefetchScalarGridSpec(
        num_scalar_prefetch=3,
        grid=(nblocks,),
        in_specs=[
            pl.BlockSpec((MOE_ROWS, D_MODEL // 2), lambda i, be, nu, nv: (i, 0)),
            pl.BlockSpec((None, D_MODEL, D_EXPERT), lambda i, be, nu, nv: (be[i], 0, 0)),
            pl.BlockSpec((None, D_MODEL, D_EXPERT), lambda i, be, nu, nv: (be[i], 0, 0)),
            pl.BlockSpec((None, D_EXPERT, D_MODEL), lambda i, be, nu, nv: (be[i], 0, 0)),
        ],
        out_specs=pl.BlockSpec((MOE_ROWS, D_MODEL // 2), lambda i, be, nu, nv: (i, 0)),
        scratch_shapes=[pltpu.VMEM((D_MODEL, D_EXPERT), BF16), pltpu.VMEM((D_MODEL, D_EXPERT), BF16),
                        pltpu.VMEM((D_EXPERT, D_MODEL), BF16)],
    )
    return pl.pallas_call(
        _expert_kernel,
        grid_spec=grid_spec,
        out_shape=jax.ShapeDtypeStruct((rows, D_MODEL // 2), jnp.int32),
        compiler_params=_cparams(("arbitrary",)),
        name="experts",
    )(blk_e, n_used, nvalid, xb, w_gate, w_up, w_down)


def _combine_kernel(x1_ref, y1_ref, y2_ref, route_ref, g_ref, outp_ref, outs_ref, *, prompt_tiles):
    r = route_ref[...]
    x = (x1_ref[...] + r[:, 2:3] * _unpack_bf16_pairs(y1_ref[...])
         + r[:, 3:4] * _unpack_bf16_pairs(y2_ref[...]))
    ms = jnp.mean(x * x, axis=-1, keepdims=True)
    y = x * lax.rsqrt(ms + EPS) * g_ref[...]
    i = pl.program_id(0)

    @pl.when(i < prompt_tiles)
    def _():
        outp_ref[...] = y

    @pl.when(i >= prompt_tiles)
    def _():
        outs_ref[...] = y


def _combine_norm(x1, ygath, route, gamma, tp):
    t = x1.shape[0]
    tm = 512
    nt, npt = t // tm, tp // tm
    return pl.pallas_call(
        functools.partial(_combine_kernel, prompt_tiles=npt),
        grid=(nt,),
        in_specs=[
            pl.BlockSpec((tm, D_MODEL), lambda i: (i, 0)),
            pl.BlockSpec((tm, D_MODEL // 2), lambda i: (i, 0)),
            pl.BlockSpec((tm, D_MODEL // 2), lambda i: (i + nt, 0)),
            pl.BlockSpec((tm, LANES), lambda i: (i, 0)),
            pl.BlockSpec((1, D_MODEL), lambda i: (0, 0)),
        ],
        out_specs=[
            pl.BlockSpec((tm, D_MODEL), lambda i: (jnp.minimum(i, npt - 1), 0)),
            pl.BlockSpec((tm, D_MODEL), lambda i: (jnp.maximum(i - npt, 0), 0)),
        ],
        out_shape=[jax.ShapeDtypeStruct((tp, D_MODEL), F32), jax.ShapeDtypeStruct((t - tp, D_MODEL), F32)],
        compiler_params=_cparams(("arbitrary",)),
        name="combine_norm",
    )(x1, ygath, ygath, route, gamma)


def _band_index():
    c = (2 * QB - np.arange(2 * QB)) % (2 * QB)
    return c, c <= QB


def _bias_a_prompt(table_a):
    c, valid = _band_index()
    idx = np.stack([_t5_bucket_np(d * np.clip(QB - c, 0, QB)) for d in DILATIONS])
    return jnp.where(valid, jnp.transpose(table_a[idx], (0, 2, 1)) * LOG2E, NEG)


def _bias_b_prompt(table_b):
    c, valid = _band_index()
    valid = valid & (c >= 1)
    h = jnp.where(valid, table_b[_t5_bucket_np(np.clip(QB - c, 0, QB))].T * LOG2E, NEG)
    return jnp.transpose(h.reshape(KV_B, G_B, 2 * QB), (1, 0, 2)).reshape(H_B, 2 * QB)


def _sample_bias(table, span, t, log2_weight):
    cols = span + LANES
    period = cols + LANES
    x = np.arange(period)
    dist = np.where(x >= period - t, span - x + period, span - x)
    extra = log2_weight(dist)
    valid = np.isfinite(extra)
    u = jnp.where(valid, table[_t5_bucket_np(np.maximum(dist, 0))].T * LOG2E
                  + np.where(valid, extra, 0.0).astype(np.float32), NEG)
    rows = jnp.tile(u, (1, t))[:, :t * (period - 1)].reshape(u.shape[0], t, period - 1)[:, :, :cols]
    return rows.reshape(u.shape[0] * t, cols)


def _bias_a_sample(table_a, t):
    def log2_count(dist):
        count = np.zeros(dist.shape, np.int64)
        for w, d in zip(WINDOWS, DILATIONS):
            count += (dist >= 0) & (dist % d == 0) & (dist <= w)
        return np.where(count > 0, np.log2(np.maximum(count, 1)), -np.inf)

    return _sample_bias(table_a, WIN_A, t, log2_count)


def _bias_b_sample(table_b, t):
    return _sample_bias(table_b, WIN_B, t,
                        lambda dist: np.where((dist >= 0) & (dist < WIN_B), 0.0, -np.inf))


def _dest_kernel(route_ref, cnt_ref, tri_ref, dest_ref, meta_ref, run_scr, pst_scr):
    i = pl.program_id(0)
    tm = route_ref.shape[0]
    r = route_ref[...]
    lane = lax.broadcasted_iota(jnp.int32, (tm, LANES), 1)
    lanef = lane.astype(F32)
    oh0 = lanef == r[:, 0:1]
    oh1 = lanef == r[:, 1:2]
    ohf = jnp.concatenate([oh0, oh1], axis=0).astype(F32)

    @pl.when(i == 0)
    def _():
        cnt = jnp.broadcast_to(cnt_ref[...], (LANES, LANES))
        padded = jnp.floor((cnt + (MOE_ROWS - 1)) * (1.0 / MOE_ROWS)) * MOE_ROWS
        lane_e = lax.broadcasted_iota(jnp.int32, (LANES, LANES), 1)
        x = padded
        for sh in (1, 2, 4, 8, 16, 32, 64):
            x = x + jnp.where(lane_e >= sh, pltpu.roll(x, sh, 1), 0.0)
        pst_scr[...] = (x - padded)[0:1]
        run_scr[...] = jnp.zeros_like(run_scr)
        wide = lambda v: jnp.concatenate([v.T, v.T], axis=1)
        cnt_t, bend_t = wide(cnt), wide(x * (1.0 / MOE_ROWS))
        bstart_t = wide((x - padded) * (1.0 / MOE_ROWS))
        blk = lax.broadcasted_iota(jnp.int32, (LANES, 2 * LANES), 1).astype(F32)
        exp = lax.broadcasted_iota(jnp.int32, (LANES, 2 * LANES), 0)
        real = exp < N_EXPERTS
        blk_e = jnp.minimum(jnp.sum(jnp.where(real & (bend_t <= blk), 1.0, 0.0), axis=0, keepdims=True),
                            N_EXPERTS - 1.0)
        mine = exp.astype(F32) == blk_e
        within = blk[0:1] - jnp.sum(jnp.where(mine, bstart_t, 0.0), axis=0, keepdims=True)
        nvalid = jnp.clip(jnp.sum(jnp.where(mine, cnt_t, 0.0), axis=0, keepdims=True) - within * MOE_ROWS,
                          0.0, float(MOE_ROWS))
        n_used = jnp.max(jnp.where(real, bend_t, 0.0), axis=0, keepdims=True)
        meta_ref[...] = jnp.concatenate([blk_e, nvalid, n_used, jnp.zeros((5, 2 * LANES), F32)],
                                        axis=0).astype(jnp.int32)

    csum = jnp.dot(tri_ref[...], ohf.astype(BF16), preferred_element_type=F32)
    val = csum + (run_scr[...] + pst_scr[...] - 1.0)
    d0 = jnp.sum(jnp.where(oh0, val[:tm], 0.0), axis=-1, keepdims=True)
    d1 = jnp.sum(jnp.where(oh1, val[tm:], 0.0), axis=-1, keepdims=True)
    tile = jnp.where(lane == 0, d0, jnp.where(lane == 1, d1, 0.0))
    dest_ref[...] = tile.T[:8].astype(jnp.int32)
    run_scr[...] += jnp.sum(ohf, axis=0, keepdims=True)


def _dispatch(route, cnt):
    t = route.shape[0]
    tm = 512
    tri = (jnp.arange(2 * tm)[:, None] >= jnp.arange(2 * tm)[None, :]).astype(BF16)
    nblocks = -(-t * TOP_K // MOE_ROWS) + N_EXPERTS
    assert nblocks <= 2 * LANES
    dest, meta = pl.pallas_call(
        _dest_kernel,
        grid=(t // tm,),
        in_specs=[pl.BlockSpec((tm, LANES), lambda i: (i, 0)),
                  pl.BlockSpec((1, LANES), lambda i: (0, 0)),
                  pl.BlockSpec((2 * tm, 2 * tm), lambda i: (0, 0))],
        out_specs=[pl.BlockSpec((8, tm), lambda i: (0, i)),
                   pl.BlockSpec((8, 2 * LANES), lambda i: (0, 0))],
        out_shape=[jax.ShapeDtypeStruct((8, t), jnp.int32), jax.ShapeDtypeStruct((8, 2 * LANES), jnp.int32)],
        scratch_shapes=[pltpu.VMEM((1, LANES), F32), pltpu.VMEM((1, LANES), F32)],
        compiler_params=_cparams(("arbitrary",)),
        name="moe_dest",
    )(route, cnt, tri)
    return dest[:TOP_K], meta[0, :nblocks], meta[2, :1], meta[1, :nblocks]


def kernel(x_prompt, x_sample, cache_a_k, cache_a_v, cache_b_k, cache_b_v, rel_bias_table, attn_norm, w_in,
           w_out, attn_sinks, ffn_norm, w_router_group, b_router_group, w_router_expert, b_router_expert,
           w_gate, w_up, w_down, final_norm):
    s = x_prompt.shape[1]
    ns, ts = x_sample.shape[0], x_sample.shape[1]
    table_a = rel_bias_table[:, :H_A]
    table_b = rel_bias_table[:, H_A:]

    w = w_in[0]
    wqa, wka, wva, wqb, wkb, wvb = (w[:, 0:512], w[:, 512:1024], w[:, 1024:1536], w[:, 1536:2048],
                                    w[:, 2048:2176], w[:, 2176:2304])
    wqb = jnp.transpose(wqb.reshape(D_MODEL, KV_B, G_B, HEAD_DIM), (0, 2, 1, 3)).reshape(D_MODEL, 512)
    wp = jnp.concatenate([wka, wva, wqa, wqb, wkb, wvb], axis=1).astype(BF16)
    cscale = jnp.concatenate([jnp.ones((1, 1024), F32), jnp.full((1, 1024), SCALE * LOG2E, F32),
                              jnp.ones((1, 256), F32)], axis=1)
    wo = w_out[0]
    wo_b = jnp.transpose(wo[512:].reshape(KV_B, G_B, HEAD_DIM, D_MODEL), (1, 0, 2, 3)).reshape(512, D_MODEL)
    wo_p = jnp.concatenate([wo[:512], wo_b], axis=0).astype(BF16)
    wr = jnp.concatenate([w_router_group[0],
                          jnp.transpose(w_router_expert[0], (1, 0, 2)).reshape(D_MODEL, N_EXPERTS),
                          jnp.zeros((D_MODEL, LANES - N_GROUPS - N_EXPERTS), F32)], axis=1)
    wr_hi = wr.astype(BF16)
    wr = jnp.concatenate([wr_hi, wr_hi, (wr - wr_hi.astype(F32)).astype(BF16)], axis=0)
    br = jnp.concatenate([b_router_group[0], b_router_expert[0].reshape(N_EXPERTS),
                          jnp.zeros((LANES - N_GROUPS - N_EXPERTS,), F32)]).reshape(1, LANES)
    sinks2 = attn_sinks[0] * LOG2E
    sinks_gk = jnp.transpose(sinks2.reshape(KV_B, G_B), (1, 0)).reshape(H_B)
    sink_rows_p = jnp.repeat(sinks_gk, QB).reshape(G_B, 1, 2 * QB)
    sink_rows_s = jnp.repeat(sinks2, ts).reshape(H_B * ts, 1)
    emat = jnp.tile(jnp.arange(LANES)[:, None] == (jnp.arange(A_WIDTH)[None, :] // HEAD_DIM),
                    (3, 1)).astype(BF16)
    attn_g = attn_norm[0].reshape(1, D_MODEL)
    ffn_g = ffn_norm[0].reshape(1, D_MODEL)

    xp = x_prompt.reshape(s, D_MODEL)
    aperm, qb_p, kvb_p, akv32, bkv32 = _proj_prompt(xp, attn_g, wp, cscale)
    a4 = _attn_a_prompt(aperm, _bias_a_prompt(table_a), emat)
    ob_p = _attn_b_prompt(qb_p, kvb_p, _bias_b_prompt(table_b), sink_rows_p)

    xs = x_sample.reshape(ns * ts, D_MODEL)
    q_s, kv_s = _proj_sample(xs, attn_g, wp, cscale)
    akt = jnp.transpose(cache_a_k[0], (0, 2, 3, 1)).reshape(ns, A_WIDTH, WIN_A)
    avt = jnp.transpose(cache_a_v[0], (0, 2, 3, 1)).reshape(ns, A_WIDTH, WIN_A)
    bkt = jnp.transpose(cache_b_k[0], (0, 2, 3, 1)).reshape(ns, LANES, WIN_B)
    bvt = jnp.transpose(cache_b_v[0], (0, 2, 3, 1)).reshape(ns, LANES, WIN_B)

    x1, xn, route, cnt = _out_router(xp, a4, ob_p, xs, q_s.reshape(ns, ts, 1024), kv_s.reshape(ns, ts, 1280),
                                     akt, avt, bkt, bvt, _bias_a_sample(table_a, ts),
                                     _bias_b_sample(table_b, ts), sink_rows_s, wo_p, ffn_g, wr, br)
    dest2, blk_e, n_used, nvalid = _dispatch(route, cnt)
    xb = _sc_scatter_rows(xn, dest2, blk_e.shape[0] * MOE_ROWS)
    yb = _experts(blk_e, n_used, nvalid, xb, w_gate[0], w_up[0], w_down[0])
    y_p, y_s = _combine_norm(x1, _sc_gather_rows(yb, dest2.reshape(-1)), route, final_norm.reshape(1, D_MODEL), s)

    y_prompt = y_p.reshape(1, s, D_MODEL)
    y_sample = y_s.reshape(ns, ts, D_MODEL)
    keep_a, keep_b = min(WIN_A, s), min(WIN_B, s)
    pak = akv32[s - keep_a:, :512].reshape(1, 1, keep_a, H_A, HEAD_DIM)
    pav = akv32[s - keep_a:, 512:].reshape(1, 1, keep_a, H_A, HEAD_DIM)
    pbk = bkv32[s - keep_b:, :128].reshape(1, 1, keep_b, KV_B, HEAD_DIM)
    pbv = bkv32[s - keep_b:, 128:].reshape(1, 1, keep_b, KV_B, HEAD_DIM)
    sak = kv_s[:, 0:512].reshape(1, ns, ts, H_A, HEAD_DIM)
    sav = kv_s[:, 512:1024].reshape(1, ns, ts, H_A, HEAD_DIM)
    sbk = kv_s[:, 1024:1152].reshape(1, ns, ts, KV_B, HEAD_DIM)
    sbv = kv_s[:, 1152:1280].reshape(1, ns, ts, KV_B, HEAD_DIM)
    return (y_prompt, y_sample, pak, pav, pbk, pbv, sak, sav, sbk, sbv)
```

```python
import functools
import math

import jax
import jax.numpy as jnp
import numpy as np
from jax import lax
from jax.experimental import pallas as pl
from jax.experimental.pallas import tpu as pltpu
from jax.experimental.pallas import tpu_sc as plsc

D_MODEL = 1024
HEAD_DIM = 64
H_A = 8
H_B = 8
KV_B = 2
G_B = 4
DILATIONS = (1, 4, 16)
WINDOWS = (128, 512, 2048)
WIN_A = 2048
WIN_B = 128
NUM_BUCKETS = 32
MAX_DISTANCE = 2048
N_GROUPS = 4
EXPERTS_PER_GROUP = 8
N_EXPERTS = 32
TOP_K = 2
D_EXPERT = 512
EPS = 1e-5
SCALE = HEAD_DIM ** -0.5
PAST_LEN = 16384

LANES = 128
SPAN = 2048
QB = 128
NCHUNK = 9
A_WIDTH = H_A * HEAD_DIM
MOE_ROWS = 512
SC_CORES = 2
SC_SUBCORES = 16
SC_WORKERS = SC_CORES * SC_SUBCORES
SC_WINDOW = 64
SC_SCATTER_WINDOW = 32
NEG = -1e30
LOG2E = math.log2(math.e)
B_STEP = 512
VMEM_LIMIT = 56 * 1024 * 1024
CACHE_BUFFERS = 3
OUT_ROUTER_VMEM = 60 * 1024 * 1024

F32 = jnp.float32
BF16 = jnp.bfloat16


def _t5_bucket_np(dist):
    dist = np.asarray(dist, np.int64)
    max_exact = NUM_BUCKETS // 2
    d = np.maximum(dist, 1).astype(np.float32)
    ratio = np.log(d / np.float32(max_exact)) / np.float32(math.log(MAX_DISTANCE / max_exact))
    large = max_exact + (ratio * np.float32(NUM_BUCKETS - max_exact)).astype(np.int32)
    large = np.minimum(large, NUM_BUCKETS - 1)
    return np.where(dist < max_exact, dist, large).astype(np.int32)


def _cparams(sem, vmem=VMEM_LIMIT):
    return pltpu.CompilerParams(dimension_semantics=sem, vmem_limit_bytes=vmem)


def _proj_prompt_kernel(x_ref, g_ref, w_ref, cs_ref, aperm_ref, qb_ref, kvb_ref, akv_ref, bkv_ref,
                        h_scr, pe_scr, po_scr, p4_scr):
    n = pl.program_id(1)

    @pl.when(n == 0)
    def _():
        x = x_ref[...]
        ms = jnp.mean(x * x, axis=-1, keepdims=True)
        h_scr[...] = (x * lax.rsqrt(ms + EPS) * g_ref[...]).astype(BF16)

    def project_chunk(dst):
        p = jnp.dot(h_scr[...], w_ref[...], preferred_element_type=F32) * cs_ref[...]
        dst[0] = p[:, :LANES]
        dst[1] = p[:, LANES:]

    def reorder(src):
        aperm_ref[0] = jnp.concatenate([src[0], src[1]], axis=1).astype(BF16)
        quarter = SPAN // 4
        for r in range(4):
            lo = src[0, pl.ds(r, quarter, stride=4), :]
            hi = src[1, pl.ds(r, quarter, stride=4), :]
            p4_scr[0, r * quarter:(r + 1) * quarter, :] = lo
            p4_scr[1, r * quarter:(r + 1) * quarter, :] = hi
            aperm_ref[1, r * quarter:(r + 1) * quarter, :] = jnp.concatenate([lo, hi], axis=1).astype(BF16)
        for r16 in range(16):
            start = (r16 % 4) * quarter + r16 // 4
            t = jnp.concatenate([p4_scr[0, pl.ds(start, QB, stride=4), :],
                                 p4_scr[1, pl.ds(start, QB, stride=4), :]], axis=1)
            aperm_ref[2, r16 * QB:(r16 + 1) * QB, :] = t.astype(BF16)

    with_reorder = jnp.logical_and(n >= 1, n <= 6)
    for parity, (cur, prev) in enumerate(((pe_scr, po_scr), (po_scr, pe_scr))):
        mine = lax.rem(n, 2) == parity

        @pl.when(jnp.logical_and(mine, with_reorder))
        def _(cur=cur, prev=prev):
            reorder(prev)
            project_chunk(cur)

        @pl.when(jnp.logical_and(mine, jnp.logical_not(with_reorder)))
        def _(cur=cur):
            project_chunk(cur)

        def chunk(cur=cur):
            return jnp.concatenate([cur[0], cur[1]], axis=1)

        @pl.when(jnp.logical_and(mine, n < 4))
        def _(chunk=chunk):
            akv_ref[...] = chunk()

        @pl.when(jnp.logical_and(mine, jnp.logical_or(n == 6, n == 7)))
        def _(chunk=chunk):
            qb_ref[...] = chunk().astype(BF16)

        @pl.when(jnp.logical_and(mine, n == 8))
        def _(chunk=chunk):
            p = chunk()
            kvb_ref[...] = p.astype(BF16)
            bkv_ref[...] = p


def _proj_prompt(x, gamma, w, cscale):
    s = x.shape[0]
    nspan = s // SPAN
    return pl.pallas_call(
        _proj_prompt_kernel,
        grid=(nspan, NCHUNK),
        in_specs=[
            pl.BlockSpec((SPAN, D_MODEL), lambda b, n: (b, 0)),
            pl.BlockSpec((1, D_MODEL), lambda b, n: (0, 0)),
            pl.BlockSpec((D_MODEL, 256), lambda b, n: (0, n)),
            pl.BlockSpec((1, 256), lambda b, n: (0, n)),
        ],
        out_specs=[
            pl.BlockSpec((3, SPAN, 256), lambda b, n: (0, b, jnp.clip(n - 1, 0, 5))),
            pl.BlockSpec((SPAN, 256), lambda b, n: (b, jnp.clip(n - 6, 0, 1))),
            pl.BlockSpec((SPAN, 256), lambda b, n: (b, 0)),
            pl.BlockSpec((SPAN, 256), lambda b, n: (b, jnp.minimum(n, 3))),
            pl.BlockSpec((SPAN, 256), lambda b, n: (b, 0)),
        ],
        out_shape=[
            jax.ShapeDtypeStruct((3, s, 3 * A_WIDTH), BF16),
            jax.ShapeDtypeStruct((s, 512), BF16),
            jax.ShapeDtypeStruct((s, 256), BF16),
            jax.ShapeDtypeStruct((s, 1024), F32),
            jax.ShapeDtypeStruct((s, 256), F32),
        ],
        scratch_shapes=[pltpu.VMEM((SPAN, D_MODEL), BF16), pltpu.VMEM((2, SPAN, LANES), F32),
                        pltpu.VMEM((2, SPAN, LANES), F32), pltpu.VMEM((2, SPAN, LANES), F32)],
        compiler_params=_cparams(("arbitrary", "arbitrary")),
        name="proj_prompt",
    )(x, gamma, w, cscale)


def _proj_sample_kernel(x_ref, g_ref, w_ref, cs_ref, q_ref, kv_ref):
    x = x_ref[...]
    ms = jnp.mean(x * x, axis=-1, keepdims=True)
    h = (x * lax.rsqrt(ms + EPS) * g_ref[...]).astype(BF16)
    p = jnp.dot(h, w_ref[...], preferred_element_type=F32) * cs_ref[...]
    kv_ref[:, :1024] = p[:, :1024]
    kv_ref[:, 1024:] = p[:, 2048:]
    q_ref[...] = p[:, 1024:2048]


def _proj_sample(x, gamma, w, cscale):
    t = x.shape[0]
    tm = 512
    return pl.pallas_call(
        _proj_sample_kernel,
        grid=(t // tm,),
        in_specs=[
            pl.BlockSpec((tm, D_MODEL), lambda i: (i, 0)),
            pl.BlockSpec((1, D_MODEL), lambda i: (0, 0)),
            pl.BlockSpec((D_MODEL, 2304), lambda i: (0, 0)),
            pl.BlockSpec((1, 2304), lambda i: (0, 0)),
        ],
        out_specs=[
            pl.BlockSpec((tm, 1024), lambda i: (i, 0)),
            pl.BlockSpec((tm, 1280), lambda i: (i, 0)),
        ],
        out_shape=[
            jax.ShapeDtypeStruct((t, 1024), F32),
            jax.ShapeDtypeStruct((t, 1280), F32),
        ],
        compiler_params=_cparams(("arbitrary",)),
        name="proj_sample",
    )(x, gamma, w, cscale)


def _spread_heads(w, e3_ref):
    hi = w.astype(BF16)
    r1 = w - hi.astype(F32)
    mid = r1.astype(BF16)
    low = (r1 - mid.astype(F32)).astype(BF16)
    return jnp.dot(jnp.concatenate([hi, mid, low], axis=1), e3_ref[...], preferred_element_type=F32)


def _pair_tile(q2, kk, vv, bias_t, lo, sink=None):
    zero = jnp.zeros_like(q2)
    qq = jnp.concatenate([jnp.where(lo, q2, zero), jnp.where(lo, zero, q2)], axis=0)
    st = lax.dot_general(kk, qq, (((1,), (1,)), ((), ())), preferred_element_type=F32)
    st = st + bias_t
    m = jnp.max(st, axis=0, keepdims=True)
    if sink is not None:
        m = jnp.maximum(m, sink)
    p = jnp.exp2(st - m)
    den = jnp.sum(p, axis=0, keepdims=True)
    if sink is not None:
        den = den + jnp.exp2(sink - m)
    pn = (p * (1.0 / den)).astype(BF16)
    o = lax.dot_general(pn, vv, (((0,), (0,)), ((), ())), preferred_element_type=F32)
    return jnp.where(lo, o[:QB], o[QB:]), m + jnp.log2(den)


def _fill_band_tiles(h_ref, bias_scr):
    nk = 2 * QB
    prev = lax.broadcasted_iota(jnp.int32, (nk, nk), 0) < QB
    for pair in range(h_ref.shape[0] // 2):
        halves = []
        for hh in range(2):
            row = h_ref[2 * pair + hh:2 * pair + hh + 1, :]
            band = pltpu.roll(jnp.broadcast_to(row, (nk, nk)), 0, 1, stride=1, stride_axis=0)
            halves.append(band[:, :QB])
        tile = jnp.concatenate(halves, axis=1)
        bias_scr[0, pair] = tile
        bias_scr[1, pair] = jnp.where(prev, NEG, tile)


def _attn_a_kernel(q_ref, kvc_ref, kvp_ref, h_ref, e_ref, out_ref, o_scr, st_scr, bias_scr):
    b = pl.program_id(0)
    g = pl.program_id(1)
    nblk = jnp.where(g == 0, 16, jnp.where(g == 1, 4, 1))
    lane = lax.broadcasted_iota(jnp.int32, (QB, LANES), 1)
    lo = lane < HEAD_DIM

    @pl.when(b == 0)
    def _():
        _fill_band_tiles(h_ref, bias_scr.at[g])

    bias_ref = bias_scr.at[g]

    for cb in range(SPAN // QB):
        first = lax.rem(jnp.int32(cb), nblk) == 0
        rows = slice(cb * QB, (cb + 1) * QB)
        prow_c = max(cb - 1, 0) * QB
        prow_p = pl.multiple_of(jnp.where(first, cb + nblk - 1, 0) * QB, QB)
        variant = jnp.logical_and(first, b == 0).astype(jnp.int32)
        stats = []
        for hp in range(4):
            ks = slice(hp * LANES, (hp + 1) * LANES)
            vs = slice(A_WIDTH + hp * LANES, A_WIDTH + (hp + 1) * LANES)
            kp = jnp.where(first, kvp_ref[pl.ds(prow_p, QB), ks], kvc_ref[prow_c:prow_c + QB, ks])
            vp = jnp.where(first, kvp_ref[pl.ds(prow_p, QB), vs], kvc_ref[prow_c:prow_c + QB, vs])
            kk = jnp.concatenate([kp, kvc_ref[rows, ks]], axis=0)
            vv = jnp.concatenate([vp, kvc_ref[rows, vs]], axis=0)
            o, lse = _pair_tile(q_ref[rows, ks], kk, vv, bias_ref[variant, hp], lo)
            o_scr[g, hp, rows, :] = o
            stats += [lse[:, :QB], lse[:, QB:]]
        sm = jnp.concatenate(stats + [jnp.zeros((LANES - H_A, QB), F32)], axis=0)
        st_scr[g, rows, :] = sm.T

    @pl.when(g == 2)
    def _():
        def merge(c, carry):
            r2 = lax.rem(c, 4) * (SPAN // 4) + c // 4
            r3 = pl.multiple_of(c * QB, QB)
            l1 = st_scr[0, pl.ds(c, QB, stride=16), :]
            l2 = st_scr[1, pl.ds(r2, QB, stride=4), :]
            l3 = st_scr[2, pl.ds(r3, QB), :]
            mx = jnp.maximum(jnp.maximum(l1, l2), l3)
            w1 = jnp.exp2(l1 - mx)
            w2 = jnp.exp2(l2 - mx)
            w3 = jnp.exp2(l3 - mx)
            tot = w1 + w2 + w3
            a1 = _spread_heads(w1 / tot, e_ref)
            a2 = _spread_heads(w2 / tot, e_ref)
            a3 = _spread_heads(w3 / tot, e_ref)
            for hp in range(4):
                sl = slice(hp * LANES, (hp + 1) * LANES)
                o1 = o_scr[0, hp, pl.ds(c, QB, stride=16), :]
                o2 = o_scr[1, hp, pl.ds(r2, QB, stride=4), :]
                o3 = o_scr[2, hp, pl.ds(r3, QB), :]
                out_ref[hp, pl.ds(c, QB, stride=16), :] = a1[:, sl] * o1 + a2[:, sl] * o2 + a3[:, sl] * o3
            return carry

        lax.fori_loop(0, 16, merge, 0, unroll=4)


def _attn_a_prompt(aperm, bias_a, emat):
    s = aperm.shape[1]
    nspan = s // SPAN
    return pl.pallas_call(
        _attn_a_kernel,
        grid=(nspan, 3),
        in_specs=[
            pl.BlockSpec((None, SPAN, A_WIDTH), lambda b, g: (g, b, 2)),
            pl.BlockSpec((None, SPAN, 2 * A_WIDTH), lambda b, g: (g, b, 0)),
            pl.BlockSpec((None, SPAN, 2 * A_WIDTH), lambda b, g: (g, jnp.maximum(b - 1, 0), 0)),
            pl.BlockSpec((None, H_A, 2 * QB), lambda b, g: (g, 0, 0)),
            pl.BlockSpec((3 * LANES, A_WIDTH), lambda b, g: (0, 0)),
        ],
        out_specs=pl.BlockSpec((4, SPAN, LANES), lambda b, g: (0, b, 0)),
        out_shape=jax.ShapeDtypeStruct((4, s, LANES), F32),
        scratch_shapes=[pltpu.VMEM((3, 4, SPAN, LANES), F32), pltpu.VMEM((3, SPAN, LANES), F32),
                        pltpu.VMEM((3, 2, 4, 2 * QB, 2 * QB), F32)],
        compiler_params=_cparams(("arbitrary", "arbitrary")),
        name="attn_a_prompt",
    )(aperm, aperm, aperm, bias_a, emat)


def _attn_b_kernel(q_ref, kvc_ref, kvp_ref, h_ref, sink_ref, out_ref, bias_ref):
    i = pl.program_id(0)
    lane = lax.broadcasted_iota(jnp.int32, (QB, LANES), 1)
    lo = lane < HEAD_DIM

    @pl.when(i == 0)
    def _():
        _fill_band_tiles(h_ref, bias_ref)

    variant = (i == 0).astype(jnp.int32)
    for j in range(B_STEP // QB):
        rows = slice(j * QB, (j + 1) * QB)
        if j == 0:
            kp, vp = kvp_ref[:, :LANES], kvp_ref[:, LANES:]
        else:
            kp, vp = kvc_ref[(j - 1) * QB:j * QB, :LANES], kvc_ref[(j - 1) * QB:j * QB, LANES:]
        kk = jnp.concatenate([kp, kvc_ref[rows, :LANES]], axis=0)
        vv = jnp.concatenate([vp, kvc_ref[rows, LANES:]], axis=0)
        for g in range(G_B):
            bias_t = bias_ref[variant, g] if j == 0 else bias_ref[0, g]
            o, _ = _pair_tile(q_ref[rows, g * LANES:(g + 1) * LANES], kk, vv, bias_t, lo, sink=sink_ref[g])
            out_ref[rows, g * LANES:(g + 1) * LANES] = o.astype(BF16)


def _attn_b_prompt(qb, kvb, bias_b, sink_rows):
    s = qb.shape[0]
    per = B_STEP // QB
    return pl.pallas_call(
        _attn_b_kernel,
        grid=(s // B_STEP,),
        in_specs=[
            pl.BlockSpec((B_STEP, 512), lambda i: (i, 0)),
            pl.BlockSpec((B_STEP, 256), lambda i: (i, 0)),
            pl.BlockSpec((QB, 256), lambda i: (jnp.maximum(i * per - 1, 0), 0)),
            pl.BlockSpec((H_B, 2 * QB), lambda i: (0, 0)),
            pl.BlockSpec((G_B, 1, 2 * QB), lambda i: (0, 0, 0)),
        ],
        out_specs=pl.BlockSpec((B_STEP, 512), lambda i: (i, 0)),
        out_shape=jax.ShapeDtypeStruct((s, 512), BF16),
        scratch_shapes=[pltpu.VMEM((2, G_B, 2 * QB, 2 * QB), F32)],
        compiler_params=_cparams(("arbitrary",)),
        name="attn_b_prompt",
    )(qb, kvb, kvb, bias_b, sink_rows)


def _sample_attention(q, kvn, akt, avt, bkt, bvt, cba, cbb, sink):
    t = q.shape[0]
    kvn_p = jnp.concatenate([kvn, jnp.zeros((LANES - t, kvn.shape[1]), F32)], axis=0).astype(BF16)
    lane_a = lax.broadcasted_iota(jnp.int32, (t, A_WIDTH), 1) // HEAD_DIM

    qa = q[:, :A_WIDTH]
    qbd = jnp.concatenate([jnp.where(lane_a == h, qa, 0.0) for h in range(H_A)], axis=0).astype(BF16)
    s_c = jnp.dot(qbd, akt.astype(BF16), preferred_element_type=F32)
    s_n = lax.dot_general(qbd, kvn_p[:, :A_WIDTH], (((1,), (1,)), ((), ())), preferred_element_type=F32)
    s = jnp.concatenate([s_c, s_n], axis=1) + cba
    m = jnp.max(s, axis=-1, keepdims=True)
    p = jnp.exp2(s - m)
    l = jnp.sum(p, axis=-1, keepdims=True)
    pb = p.astype(BF16)
    o_n = jnp.dot(pb[:, WIN_A:], kvn_p[:, A_WIDTH:2 * A_WIDTH], preferred_element_type=F32)
    o_all = lax.dot_general(pb[:, :WIN_A], avt.astype(BF16), (((1,), (1,)), ((), ())),
                            preferred_element_type=F32) + o_n
    o_sel = jnp.zeros((t, A_WIDTH), F32)
    l_b = jnp.ones((t, A_WIDTH), F32)
    for h in range(H_A):
        sel = lane_a == h
        o_sel = jnp.where(sel, o_all[h * t:(h + 1) * t], o_sel)
        l_b = jnp.where(sel, l[h * t:(h + 1) * t], l_b)
    oa = o_sel / l_b

    lane_b = lax.broadcasted_iota(jnp.int32, (G_B * t, LANES), 1)
    lo = lane_b < HEAD_DIM
    qb2 = jnp.concatenate([q[:, A_WIDTH + g * LANES:A_WIDTH + (g + 1) * LANES] for g in range(G_B)], axis=0)
    qm = jnp.concatenate([jnp.where(lo, qb2, 0.0), jnp.where(lo, 0.0, qb2)], axis=0).astype(BF16)
    kb_n = kvn_p[:, 2 * A_WIDTH:2 * A_WIDTH + LANES]
    vb_n = kvn_p[:, 2 * A_WIDTH + LANES:]
    sb_c = jnp.dot(qm, bkt.astype(BF16), preferred_element_type=F32)
    sb_n = lax.dot_general(qm, kb_n, (((1,), (1,)), ((), ())), preferred_element_type=F32)
    sb = jnp.concatenate([sb_c, sb_n], axis=1) + cbb
    mb = jnp.maximum(jnp.max(sb, axis=-1, keepdims=True), sink)
    pbb = jnp.exp2(sb - mb)
    den = jnp.sum(pbb, axis=-1, keepdims=True) + jnp.exp2(sink - mb)
    pbb = pbb.astype(BF16)
    ob = lax.dot_general(pbb[:, :WIN_B], bvt.astype(BF16), (((1,), (1,)), ((), ())),
                         preferred_element_type=F32)
    ob = (ob + jnp.dot(pbb[:, WIN_B:], vb_n, preferred_element_type=F32)) / den
    half = G_B * t
    lo8 = lo[:t]
    ob = jnp.concatenate([jnp.where(lo8, ob[g * t:(g + 1) * t], ob[half + g * t:half + (g + 1) * t])
                          for g in range(G_B)], axis=1)
    return oa, ob


def _route(logits):
    lane = lax.broadcasted_iota(jnp.int32, logits.shape, 1).astype(F32)
    big = jnp.float32(1 << 20)
    ninf = jnp.float32(-jnp.inf)
    gmask = lane < N_GROUPS
    lg = jnp.where(gmask, logits, ninf)
    gmax = jnp.max(lg, axis=-1, keepdims=True)
    grp = jnp.min(jnp.where(lg == gmax, lane, big), axis=-1, keepdims=True)
    pg_top = 1.0 / jnp.sum(jnp.exp(lg - gmax), axis=-1, keepdims=True)
    e0 = N_GROUPS + grp * EXPERTS_PER_GROUP
    emask = jnp.logical_and(lane >= e0, lane < e0 + EXPERTS_PER_GROUP)
    le = jnp.where(emask, logits, ninf)
    emax = jnp.max(le, axis=-1, keepdims=True)
    esum = jnp.sum(jnp.exp(le - emax), axis=-1, keepdims=True)
    i1 = jnp.min(jnp.where(le == emax, lane, big), axis=-1, keepdims=True)
    le2 = jnp.where(lane == i1, ninf, le)
    e2max = jnp.max(le2, axis=-1, keepdims=True)
    i2 = jnp.min(jnp.where(le2 == e2max, lane, big), axis=-1, keepdims=True)
    p1 = 1.0 / esum
    p2 = jnp.exp(e2max - emax) / esum
    g1 = pg_top * p1 / (p1 + p2)
    g2 = pg_top * p2 / (p1 + p2)
    out = jnp.where(lane == 0, i1 - N_GROUPS, 0.0)
    out = jnp.where(lane == 1, i2 - N_GROUPS, out)
    out = jnp.where(lane == 2, g1, out)
    out = jnp.where(lane == 3, g2, out)
    return out


def _pack_bf16_pairs(x):
    half = x.shape[1] // 2

    def rne(v):
        bits = lax.bitcast_convert_type(v, jnp.int32)
        return bits + 0x7FFF + (lax.shift_right_logical(bits, 16) & 1)

    lo = lax.shift_right_logical(rne(x[:, :half]), 16)
    hi = rne(x[:, half:]) & jnp.int32(-65536)
    return lo | hi


def _unpack_bf16_pairs(w):
    lo = lax.bitcast_convert_type(lax.shift_left(w, 16), F32)
    hi = lax.bitcast_convert_type(w & jnp.int32(-65536), F32)
    return jnp.concatenate([lo, hi], axis=1)


def _out_router_kernel(xp_ref, ap_ref, bp_ref, xs_ref, q_ref, kvn_ref, akt_hbm, avt_hbm, bkt_ref, bvt_ref,
                       cba_ref, cbb_ref, sink_ref, wo_ref, g_ref, wr_ref, br_ref,
                       x1_ref, xn_ref, route_ref, cnt_ref,
                       xcat_scr, mix_scr, kbuf, vbuf, sem, *, prompt_tiles, decode_tiles, seqs_per_step):
    i = pl.program_id(0)
    seqs = prompt_tiles * seqs_per_step

    def cache_copies(n, slot):
        return (pltpu.make_async_copy(akt_hbm.at[n], kbuf.at[slot], sem.at[0, slot]),
                pltpu.make_async_copy(avt_hbm.at[n], vbuf.at[slot], sem.at[1, slot]))

    @pl.when(i == 0)
    def _():
        cnt_ref[...] = jnp.zeros_like(cnt_ref)
        xcat_scr[...] = jnp.zeros_like(xcat_scr)
        for n0 in range(2):
            for c in cache_copies(n0, n0):
                c.start()

    @pl.when(i == prompt_tiles)
    def _():
        for n1 in (seqs, seqs + 1):
            for c in cache_copies(seqs - 1, n1 % CACHE_BUFFERS):
                c.wait()

    pslot = lax.rem(i, 2)

    def route_previous():
        logits = jnp.dot(xcat_scr[1 - pslot], wr_ref[...], preferred_element_type=F32)
        route = _route(logits + br_ref[...])
        route_ref[...] = route
        lanef = lax.broadcasted_iota(jnp.int32, route.shape, 1).astype(F32)
        hits = (lanef == route[:, 0:1]).astype(F32) + (lanef == route[:, 1:2]).astype(F32)
        cnt_ref[...] += jnp.sum(hits, axis=0, keepdims=True) * (i > 0).astype(F32)

    def project(x_ref, mix):
        x1 = x_ref[...] + jnp.dot(mix, wo_ref[...], preferred_element_type=F32)
        x1_ref[...] = x1
        ms = jnp.mean(x1 * x1, axis=-1, keepdims=True)
        xn = x1 * lax.rsqrt(ms + EPS) * g_ref[...]
        xn_ref[...] = _pack_bf16_pairs(xn)
        xh = xn.astype(BF16)
        xl = (xn - xh.astype(F32)).astype(BF16)
        xcat_scr[pslot] = jnp.concatenate([xh, xl, xh], axis=1)

    @pl.when(i < prompt_tiles)
    def _():
        route_previous()
        mix = jnp.concatenate([ap_ref[0], ap_ref[1], ap_ref[2], ap_ref[3]], axis=1).astype(BF16)
        project(xp_ref, jnp.concatenate([mix, bp_ref[...]], axis=1))
        t = q_ref.shape[1]
        for s in range(seqs_per_step):
            n = i * seqs_per_step + s
            slot = lax.rem(n, CACHE_BUFFERS)
            for c in cache_copies(n, slot):
                c.wait()
            for c in cache_copies(jnp.minimum(n + 2, seqs - 1), lax.rem(n + 2, CACHE_BUFFERS)):
                c.start()
            oa, ob = _sample_attention(q_ref[s], kvn_ref[s], kbuf[slot], vbuf[slot], bkt_ref[s], bvt_ref[s],
                                       cba_ref[...], cbb_ref[...], sink_ref[...])
            row = pl.multiple_of(n * t, t)
            mix_scr[pl.ds(row, t), :A_WIDTH] = oa
            mix_scr[pl.ds(row, t), A_WIDTH:] = ob

    @pl.when(i >= prompt_tiles)
    def _():
        route_previous()
        tm = xs_ref.shape[0]
        row = pl.multiple_of(jnp.clip(i - prompt_tiles, 0, decode_tiles - 1) * tm, tm)
        project(xs_ref, mix_scr[pl.ds(row, tm), :].astype(BF16))


def _out_router(xp, a4p, bp, xs, q3, kvn3, akt, avt, bkt, bvt, cbias_a, cbias_b, sink_rows, wo, gamma, wr, br):
    tp, tsm = xp.shape[0], xs.shape[0]
    ns, ts = q3.shape[0], q3.shape[1]
    tm = 512
    npt, nst = tp // tm, tsm // tm
    nt = npt + nst
    t = tp + tsm
    sps = ns // npt
    assert sps * npt == ns and ns * ts == tsm and ns >= CACHE_BUFFERS
    pmap = lambda i: (jnp.minimum(i, npt - 1), 0)
    pmap3 = lambda i: (jnp.minimum(i, npt - 1), 0, 0)
    smap = lambda i: (jnp.clip(i - npt, 0, nst - 1), 0)
    cur = lambda i: (jnp.minimum(i, nt - 1), 0)
    const = lambda i: (0, 0)
    return pl.pallas_call(
        functools.partial(_out_router_kernel, prompt_tiles=npt, decode_tiles=nst, seqs_per_step=sps),
        grid=(nt + 1,),
        in_specs=[
            pl.BlockSpec((tm, D_MODEL), pmap),
            pl.BlockSpec((4, tm, LANES), lambda i: (0, jnp.minimum(i, npt - 1), 0)),
            pl.BlockSpec((tm, 512), pmap),
            pl.BlockSpec((tm, D_MODEL), smap),
            pl.BlockSpec((sps, ts, 1024), pmap3),
            pl.BlockSpec((sps, ts, 1280), pmap3),
            pl.BlockSpec(memory_space=pl.ANY),
            pl.BlockSpec(memory_space=pl.ANY),
            pl.BlockSpec((sps, LANES, WIN_B), pmap3),
            pl.BlockSpec((sps, LANES, WIN_B), pmap3),
            pl.BlockSpec((H_A * ts, WIN_A + LANES), const),
            pl.BlockSpec((H_B * ts, WIN_B + LANES), const),
            pl.BlockSpec((H_B * ts, 1), const),
            pl.BlockSpec((D_MODEL, D_MODEL), const),
            pl.BlockSpec((1, D_MODEL), const),
            pl.BlockSpec((3 * D_MODEL, LANES), const),
            pl.BlockSpec((1, LANES), const),
        ],
        out_specs=[
            pl.BlockSpec((tm, D_MODEL), cur),
            pl.BlockSpec((tm, D_MODEL // 2), cur),
            pl.BlockSpec((tm, LANES), lambda i: (jnp.maximum(i - 1, 0), 0)),
            pl.BlockSpec((1, LANES), const),
        ],
        scratch_shapes=[pltpu.VMEM((2, tm, 3 * D_MODEL), BF16), pltpu.VMEM((tsm, D_MODEL), F32),
                        pltpu.VMEM((CACHE_BUFFERS, A_WIDTH, WIN_A), F32),
                        pltpu.VMEM((CACHE_BUFFERS, A_WIDTH, WIN_A), F32),
                        pltpu.SemaphoreType.DMA((2, CACHE_BUFFERS))],
        out_shape=[
            jax.ShapeDtypeStruct((t, D_MODEL), F32),
            jax.ShapeDtypeStruct((t, D_MODEL // 2), jnp.int32),
            jax.ShapeDtypeStruct((t, LANES), F32),
            jax.ShapeDtypeStruct((1, LANES), F32),
        ],
        compiler_params=_cparams(("arbitrary",), vmem=OUT_ROUTER_VMEM),
        name="out_router",
    )(xp, a4p, bp, xs, q3, kvn3, akt, avt, bkt, bvt, cbias_a, cbias_b, sink_rows, wo, gamma, wr, br)


def _sc_gather_rows(table, idx):
    b = idx.shape[0]
    d = table.shape[1]
    w = SC_WINDOW
    per_worker = b // SC_WORKERS
    nwin = per_worker // w
    assert per_worker * SC_WORKERS == b and nwin * w == per_worker
    mesh = plsc.VectorSubcoreMesh(core_axis_name="c", subcore_axis_name="s")

    @functools.partial(
        pl.kernel, mesh=mesh,
        out_type=jax.ShapeDtypeStruct((b, d), table.dtype),
        scratch_types=[pltpu.VMEM((nwin, w), jnp.int32), pltpu.VMEM((2, w, d), table.dtype),
                       pltpu.SemaphoreType.DMA((2,)), pltpu.SemaphoreType.DMA((2,))],
        name="sc_gather_rows",
    )
    def gather(table_hbm, idx_hbm, out_hbm, idx_v, rows_v, sem_in, sem_out):
        wid = lax.axis_index("s") * SC_CORES + lax.axis_index("c")
        base = wid * per_worker
        pltpu.sync_copy(idx_hbm.at[wid], idx_v)

        def fetch(j):
            return pltpu.make_async_copy(table_hbm.at[idx_v.at[j]], rows_v.at[j % 2], sem_in.at[j % 2])

        def flush(j):
            return pltpu.make_async_copy(rows_v.at[j % 2], out_hbm.at[pl.ds(base + j * w, w)],
                                         sem_out.at[j % 2])

        fetch(0).start()
        for j in range(nwin):
            fetch(j).wait()
            if j + 1 < nwin:
                if j >= 1:
                    flush(j - 1).wait()
                fetch(j + 1).start()
            flush(j).start()
        for j in range(max(nwin - 2, 0), nwin):
            flush(j).wait()

    return gather(table, idx.reshape(SC_WORKERS, nwin, w))


def _sc_scatter_rows(x, dest2, nrows):
    t, d = x.shape
    w = SC_SCATTER_WINDOW
    per_worker = t // SC_WORKERS
    nwin = per_worker // w
    assert per_worker * SC_WORKERS == t and nwin * w == per_worker
    mesh = plsc.VectorSubcoreMesh(core_axis_name="c", subcore_axis_name="s")

    @functools.partial(
        pl.kernel, mesh=mesh,
        out_type=jax.ShapeDtypeStruct((nrows, d), x.dtype),
        scratch_types=[pltpu.VMEM((TOP_K, nwin, w), jnp.int32), pltpu.VMEM((2, w, d), x.dtype),
                       pltpu.SemaphoreType.DMA((2,)), pltpu.SemaphoreType.DMA((2,))],
        name="sc_scatter_rows",
    )
    def scatter(x_hbm, dest_hbm, out_hbm, idx_v, rows_v, sem_in, sem_out):
        wid = lax.axis_index("s") * SC_CORES + lax.axis_index("c")
        base = wid * per_worker
        for k in range(TOP_K):
            pltpu.sync_copy(dest_hbm.at[k, wid], idx_v.at[k])

        def fetch(j):
            return pltpu.make_async_copy(x_hbm.at[pl.ds(base + j * w, w)], rows_v.at[j % 2], sem_in.at[j % 2])

        def spread(j, k):
            return pltpu.make_async_copy(rows_v.at[j % 2], out_hbm.at[idx_v.at[k, j]], sem_out.at[j % 2])

        fetch(0).start()
        for j in range(nwin):
            fetch(j).wait()
            if j + 1 < nwin:
                if j >= 1:
                    for k in range(TOP_K):
                        spread(j - 1, k).wait()
                fetch(j + 1).start()
            for k in range(TOP_K):
                spread(j, k).start()
        for j in range(max(nwin - 2, 0), nwin):
            for k in range(TOP_K):
                spread(j, k).wait()

    return scatter(x, dest2.reshape(TOP_K, SC_WORKERS, nwin, w))


def _expert_kernel(be_ref, nu_ref, nv_ref, x_ref, wg_ref, wu_ref, wd_ref, o_ref, wg_s, wu_s, wd_s):
    i = pl.program_id(0)
    used = i < nu_ref[0]
    changed = jnp.logical_or(i == 0, be_ref[i] != be_ref[jnp.maximum(i - 1, 0)])

    @pl.when(jnp.logical_and(used, changed))
    def _():
        wg_s[...] = wg_ref[...].astype(BF16)
        wu_s[...] = wu_ref[...].astype(BF16)
        wd_s[...] = wd_ref[...].astype(BF16)

    nv = nv_ref[i]
    half = MOE_ROWS // 2

    def swiglu(rows):
        row = lax.broadcasted_iota(jnp.int32, (rows, x_ref.shape[1]), 0)
        x = _unpack_bf16_pairs(jnp.where(row < nv, x_ref[:rows, :], 0)).astype(BF16)
        gate = jnp.dot(x, wg_s[...], preferred_element_type=F32)
        up = jnp.dot(x, wu_s[...], preferred_element_type=F32)
        h = (gate * jax.nn.sigmoid(gate) * up).astype(BF16)
        o_ref[:rows, :] = _pack_bf16_pairs(jnp.dot(h, wd_s[...], preferred_element_type=F32))

    @pl.when(jnp.logical_and(used, nv > half))
    def _():
        swiglu(MOE_ROWS)

    @pl.when(jnp.logical_and(used, nv <= half))
    def _():
        swiglu(half)
        o_ref[half:, :] = jnp.zeros((MOE_ROWS - half, o_ref.shape[1]), o_ref.dtype)

    @pl.when(jnp.logical_not(used))
    def _():
        o_ref[...] = jnp.zeros_like(o_ref)


def _experts(blk_e, n_used, nvalid, xb, w_gate, w_up, w_down):
    rows = xb.shape[0]
    nblocks = rows // MOE_ROWS
    grid_spec = pltpu.PrefetchScalarGridSpec(
        num_scalar_prefetch=3,
        grid=(nblocks,),
        in_specs=[
            pl.BlockSpec((MOE_ROWS, D_MODEL // 2), lambda i, be, nu, nv: (i, 0)),
            pl.BlockSpec((None, D_MODEL, D_EXPERT), lambda i, be, nu, nv: (be[i], 0, 0)),
            pl.BlockSpec((None, D_MODEL, D_EXPERT), lambda i, be, nu, nv: (be[i], 0, 0)),
            pl.BlockSpec((None, D_EXPERT, D_MODEL), lambda i, be, nu, nv: (be[i], 0, 0)),
        ],
        out_specs=pl.BlockSpec((MOE_ROWS, D_MODEL // 2), lambda i, be, nu, nv: (i, 0)),
        scratch_shapes=[pltpu.VMEM((D_MODEL, D_EXPERT), BF16), pltpu.VMEM((D_MODEL, D_EXPERT), BF16),
                        pltpu.VMEM((D_EXPERT, D_MODEL), BF16)],
    )
    return pl.pallas_call(
        _expert_kernel,
        grid_spec=grid_spec,
        out_shape=jax.ShapeDtypeStruct((rows, D_MODEL // 2), jnp.int32),
        compiler_params=_cparams(("arbitrary",)),
        name="experts",
    )(blk_e, n_used, nvalid, xb, w_gate, w_up, w_down)


def _combine_kernel(x1_ref, y1_ref, y2_ref, route_ref, g_ref, outp_ref, outs_ref, *, prompt_tiles):
    r = route_ref[...]
    x = (x1_ref[...] + r[:, 2:3] * _unpack_bf16_pairs(y1_ref[...])
         + r[:, 3:4] * _unpack_bf16_pairs(y2_ref[...]))
    ms = jnp.mean(x * x, axis=-1, keepdims=True)
    y = x * lax.rsqrt(ms + EPS) * g_ref[...]
    i = pl.program_id(0)

    @pl.when(i < prompt_tiles)
    def _():
        outp_ref[...] = y

    @pl.when(i >= prompt_tiles)
    def _():
        outs_ref[...] = y


def _combine_norm(x1, ygath, route, gamma, tp):
    t = x1.shape[0]
    tm = 512
    nt, npt = t // tm, tp // tm
    return pl.pallas_call(
        functools.partial(_combine_kernel, prompt_tiles=npt),
        grid=(nt,),
        in_specs=[
            pl.BlockSpec((tm, D_MODEL), lambda i: (i, 0)),
            pl.BlockSpec((tm, D_MODEL // 2), lambda i: (i, 0)),
            pl.BlockSpec((tm, D_MODEL // 2), lambda i: (i + nt, 0)),
            pl.BlockSpec((tm, LANES), lambda i: (i, 0)),
            pl.BlockSpec((1, D_MODEL), lambda i: (0, 0)),
        ],
        out_specs=[
            pl.BlockSpec((tm, D_MODEL), lambda i: (jnp.minimum(i, npt - 1), 0)),
            pl.BlockSpec((tm, D_MODEL), lambda i: (jnp.maximum(i - npt, 0), 0)),
        ],
        out_shape=[jax.ShapeDtypeStruct((tp, D_MODEL), F32), jax.ShapeDtypeStruct((t - tp, D_MODEL), F32)],
        compiler_params=_cparams(("arbitrary",)),
        name="combine_norm",
    )(x1, ygath, ygath, route, gamma)


def _band_index():
    c = (2 * QB - np.arange(2 * QB)) % (2 * QB)
    return c, c <= QB


def _bias_a_prompt(table_a):
    c, valid = _band_index()
    idx = np.stack([_t5_bucket_np(d * np.clip(QB - c, 0, QB)) for d in DILATIONS])
    return jnp.where(valid, jnp.transpose(table_a[idx], (0, 2, 1)) * LOG2E, NEG)


def _bias_b_prompt(table_b):
    c, valid = _band_index()
    valid = valid & (c >= 1)
    h = jnp.where(valid, table_b[_t5_bucket_np(np.clip(QB - c, 0, QB))].T * LOG2E, NEG)
    return jnp.transpose(h.reshape(KV_B, G_B, 2 * QB), (1, 0, 2)).reshape(H_B, 2 * QB)


def _sample_bias(table, span, t, log2_weight):
    cols = span + LANES
    period = cols + LANES
    x = np.arange(period)
    dist = np.where(x >= period - t, span - x + period, span - x)
    extra = log2_weight(dist)
    valid = np.isfinite(extra)
    u = jnp.where(valid, table[_t5_bucket_np(np.maximum(dist, 0))].T * LOG2E
                  + np.where(valid, extra, 0.0).astype(np.float32), NEG)
    rows = jnp.tile(u, (1, t))[:, :t * (period - 1)].reshape(u.shape[0], t, period - 1)[:, :, :cols]
    return rows.reshape(u.shape[0] * t, cols)


def _bias_a_sample(table_a, t):
    def log2_count(dist):
        count = np.zeros(dist.shape, np.int64)
        for w, d in zip(WINDOWS, DILATIONS):
            count += (dist >= 0) & (dist % d == 0) & (dist <= w)
        return np.where(count > 0, np.log2(np.maximum(count, 1)), -np.inf)

    return _sample_bias(table_a, WIN_A, t, log2_count)


def _bias_b_sample(table_b, t):
    return _sample_bias(table_b, WIN_B, t,
                        lambda dist: np.where((dist >= 0) & (dist < WIN_B), 0.0, -np.inf))


def _dest_kernel(route_ref, cnt_ref, tri_ref, dest_ref, meta_ref, run_scr, pst_scr):
    i = pl.program_id(0)
    tm = route_ref.shape[0]
    r = route_ref[...]
    lane = lax.broadcasted_iota(jnp.int32, (tm, LANES), 1)
    lanef = lane.astype(F32)
    oh0 = lanef == r[:, 0:1]
    oh1 = lanef == r[:, 1:2]
    ohf = jnp.concatenate([oh0, oh1], axis=0).astype(F32)

    @pl.when(i == 0)
    def _():
        cnt = jnp.broadcast_to(cnt_ref[...], (LANES, LANES))
        padded = jnp.floor((cnt + (MOE_ROWS - 1)) * (1.0 / MOE_ROWS)) * MOE_ROWS
        lane_e = lax.broadcasted_iota(jnp.int32, (LANES, LANES), 1)
        x = padded
        for sh in (1, 2, 4, 8, 16, 32, 64):
            x = x + jnp.where(lane_e >= sh, pltpu.roll(x, sh, 1), 0.0)
        pst_scr[...] = (x - padded)[0:1]
        run_scr[...] = jnp.zeros_like(run_scr)
        wide = lambda v: jnp.concatenate([v.T, v.T], axis=1)
        cnt_t, bend_t = wide(cnt), wide(x * (1.0 / MOE_ROWS))
        bstart_t = wide((x - padded) * (1.0 / MOE_ROWS))
        blk = lax.broadcasted_iota(jnp.int32, (LANES, 2 * LANES), 1).astype(F32)
        exp = lax.broadcasted_iota(jnp.int32, (LANES, 2 * LANES), 0)
        real = exp < N_EXPERTS
        blk_e = jnp.minimum(jnp.sum(jnp.where(real & (bend_t <= blk), 1.0, 0.0), axis=0, keepdims=True),
                            N_EXPERTS - 1.0)
        mine = exp.astype(F32) == blk_e
        within = blk[0:1] - jnp.sum(jnp.where(mine, bstart_t, 0.0), axis=0, keepdims=True)
        nvalid = jnp.clip(jnp.sum(jnp.where(mine, cnt_t, 0.0), axis=0, keepdims=True) - within * MOE_ROWS,
                          0.0, float(MOE_ROWS))
        n_used = jnp.max(jnp.where(real, bend_t, 0.0), axis=0, keepdims=True)
        meta_ref[...] = jnp.concatenate([blk_e, nvalid, n_used, jnp.zeros((5, 2 * LANES), F32)],
                                        axis=0).astype(jnp.int32)

    base = run_scr[...] + pst_scr[...] - 1.0
    vals = []
    for c in range(2 * tm // LANES):
        ohc = ohf[c * LANES:(c + 1) * LANES]
        vals.append(jnp.dot(tri_ref[...], ohc.astype(BF16), preferred_element_type=F32) + base)
        base = base + jnp.sum(ohc, axis=0, keepdims=True)
    val = jnp.concatenate(vals, axis=0)
    d0 = jnp.sum(jnp.where(oh0, val[:tm], 0.0), axis=-1, keepdims=True)
    d1 = jnp.sum(jnp.where(oh1, val[tm:], 0.0), axis=-1, keepdims=True)
    tile = jnp.where(lane == 0, d0, jnp.where(lane == 1, d1, 0.0))
    dest_ref[...] = tile.T[:8].astype(jnp.int32)
    run_scr[...] += jnp.sum(ohf, axis=0, keepdims=True)


def _dispatch(route, cnt):
    t = route.shape[0]
    tm = 512
    tri = (jnp.arange(LANES)[:, None] >= jnp.arange(LANES)[None, :]).astype(BF16)
    nblocks = -(-t * TOP_K // MOE_ROWS) + N_EXPERTS
    assert nblocks <= 2 * LANES
    dest, meta = pl.pallas_call(
        _dest_kernel,
        grid=(t // tm,),
        in_specs=[pl.BlockSpec((tm, LANES), lambda i: (i, 0)),
                  pl.BlockSpec((1, LANES), lambda i: (0, 0)),
                  pl.BlockSpec((LANES, LANES), lambda i: (0, 0))],
        out_specs=[pl.BlockSpec((8, tm), lambda i: (0, i)),
                   pl.BlockSpec((8, 2 * LANES), lambda i: (0, 0))],
        out_shape=[jax.ShapeDtypeStruct((8, t), jnp.int32), jax.ShapeDtypeStruct((8, 2 * LANES), jnp.int32)],
        scratch_shapes=[pltpu.VMEM((1, LANES), F32), pltpu.VMEM((1, LANES), F32)],
        compiler_params=_cparams(("arbitrary",)),
        name="moe_dest",
    )(route, cnt, tri)
    return dest[:TOP_K], meta[0, :nblocks], meta[2, :1], meta[1, :nblocks]


def kernel(x_prompt, x_sample, cache_a_k, cache_a_v, cache_b_k, cache_b_v, rel_bias_table, attn_norm, w_in,
           w_out, attn_sinks, ffn_norm, w_router_group, b_router_group, w_router_expert, b_router_expert,
           w_gate, w_up, w_down, final_norm):
    s = x_prompt.shape[1]
    ns, ts = x_sample.shape[0], x_sample.shape[1]
    table_a = rel_bias_table[:, :H_A]
    table_b = rel_bias_table[:, H_A:]

    w = w_in[0]
    wqa, wka, wva, wqb, wkb, wvb = (w[:, 0:512], w[:, 512:1024], w[:, 1024:1536], w[:, 1536:2048],
                                    w[:, 2048:2176], w[:, 2176:2304])
    wqb = jnp.transpose(wqb.reshape(D_MODEL, KV_B, G_B, HEAD_DIM), (0, 2, 1, 3)).reshape(D_MODEL, 512)
    wp = jnp.concatenate([wka, wva, wqa, wqb, wkb, wvb], axis=1).astype(BF16)
    cscale = jnp.concatenate([jnp.ones((1, 1024), F32), jnp.full((1, 1024), SCALE * LOG2E, F32),
                              jnp.ones((1, 256), F32)], axis=1)
    wo = w_out[0]
    wo_b = jnp.transpose(wo[512:].reshape(KV_B, G_B, HEAD_DIM, D_MODEL), (1, 0, 2, 3)).reshape(512, D_MODEL)
    wo_p = jnp.concatenate([wo[:512], wo_b], axis=0).astype(BF16)
    wr = jnp.concatenate([w_router_group[0],
                          jnp.transpose(w_router_expert[0], (1, 0, 2)).reshape(D_MODEL, N_EXPERTS),
                          jnp.zeros((D_MODEL, LANES - N_GROUPS - N_EXPERTS), F32)], axis=1)
    wr_hi = wr.astype(BF16)
    wr = jnp.concatenate([wr_hi, wr_hi, (wr - wr_hi.astype(F32)).astype(BF16)], axis=0)
    br = jnp.concatenate([b_router_group[0], b_router_expert[0].reshape(N_EXPERTS),
                          jnp.zeros((LANES - N_GROUPS - N_EXPERTS,), F32)]).reshape(1, LANES)
    sinks2 = attn_sinks[0] * LOG2E
    sinks_gk = jnp.transpose(sinks2.reshape(KV_B, G_B), (1, 0)).reshape(H_B)
    sink_rows_p = jnp.repeat(sinks_gk, QB).reshape(G_B, 1, 2 * QB)
    sink_rows_s = jnp.repeat(sinks2, ts).reshape(H_B * ts, 1)
    emat = jnp.tile(jnp.arange(LANES)[:, None] == (jnp.arange(A_WIDTH)[None, :] // HEAD_DIM),
                    (3, 1)).astype(BF16)
    attn_g = attn_norm[0].reshape(1, D_MODEL)
    ffn_g = ffn_norm[0].reshape(1, D_MODEL)

    xp = x_prompt.reshape(s, D_MODEL)
    aperm, qb_p, kvb_p, akv32, bkv32 = _proj_prompt(xp, attn_g, wp, cscale)
    a4 = _attn_a_prompt(aperm, _bias_a_prompt(table_a), emat)
    ob_p = _attn_b_prompt(qb_p, kvb_p, _bias_b_prompt(table_b), sink_rows_p)

    xs = x_sample.reshape(ns * ts, D_MODEL)
    q_s, kv_s = _proj_sample(xs, attn_g, wp, cscale)
    akt = jnp.transpose(cache_a_k[0], (0, 2, 3, 1)).reshape(ns, A_WIDTH, WIN_A)
    avt = jnp.transpose(cache_a_v[0], (0, 2, 3, 1)).reshape(ns, A_WIDTH, WIN_A)
    bkt = jnp.transpose(cache_b_k[0], (0, 2, 3, 1)).reshape(ns, LANES, WIN_B)
    bvt = jnp.transpose(cache_b_v[0], (0, 2, 3, 1)).reshape(ns, LANES, WIN_B)

    x1, xn, route, cnt = _out_router(xp, a4, ob_p, xs, q_s.reshape(ns, ts, 1024), kv_s.reshape(ns, ts, 1280),
                                     akt, avt, bkt, bvt, _bias_a_sample(table_a, ts),
                                     _bias_b_sample(table_b, ts), sink_rows_s, wo_p, ffn_g, wr, br)
    dest2, blk_e, n_used, nvalid = _dispatch(route, cnt)
    xb = _sc_scatter_rows(xn, dest2, blk_e.shape[0] * MOE_ROWS)
    yb = _experts(blk_e, n_used, nvalid, xb, w_gate[0], w_up[0], w_down[0])
    y_p, y_s = _combine_norm(x1, _sc_gather_rows(yb, dest2.reshape(-1)), route, final_norm.reshape(1, D_MODEL), s)

    y_prompt = y_p.reshape(1, s, D_MODEL)
    y_sample = y_s.reshape(ns, ts, D_MODEL)
    keep_a, keep_b = min(WIN_A, s), min(WIN_B, s)
    pak = akv32[s - keep_a:, :512].reshape(1, 1, keep_a, H_A, HEAD_DIM)
    pav = akv32[s - keep_a:, 512:].reshape(1, 1, keep_a, H_A, HEAD_DIM)
    pbk = bkv32[s - keep_b:, :128].reshape(1, 1, keep_b, KV_B, HEAD_DIM)
    pbv = bkv32[s - keep_b:, 128:].reshape(1, 1, keep_b, KV_B, HEAD_DIM)
    sak = kv_s[:, 0:512].reshape(1, ns, ts, H_A, HEAD_DIM)
    sav = kv_s[:, 512:1024].reshape(1, ns, ts, H_A, HEAD_DIM)
    sbk = kv_s[:, 1024:1152].reshape(1, ns, ts, KV_B, HEAD_DIM)
    sbv = kv_s[:, 1152:1280].reshape(1, ns, ts, KV_B, HEAD_DIM)
    return (y_prompt, y_sample, pak, pav, pbk, pbv, sak, sav, sbk, sbv)
```

```python
import functools
import math

import jax
import jax.numpy as jnp
import numpy as np
from jax import lax
from jax.experimental import pallas as pl
from jax.experimental.pallas import tpu as pltpu
from jax.experimental.pallas import tpu_sc as plsc

D_MODEL = 1024
HEAD_DIM = 64
H_A = 8
H_B = 8
KV_B = 2
G_B = 4
DILATIONS = (1, 4, 16)
WINDOWS = (128, 512, 2048)
WIN_A = 2048
WIN_B = 128
NUM_BUCKETS = 32
MAX_DISTANCE = 2048
N_GROUPS = 4
EXPERTS_PER_GROUP = 8
N_EXPERTS = 32
TOP_K = 2
D_EXPERT = 512
EPS = 1e-5
SCALE = HEAD_DIM ** -0.5
PAST_LEN = 16384

LANES = 128
SPAN = 2048
QB = 128
NCHUNK = 9
A_WIDTH = H_A * HEAD_DIM
MOE_ROWS = 512
SC_CORES = 2
SC_SUBCORES = 16
SC_WORKERS = SC_CORES * SC_SUBCORES
SC_WINDOW = 64
SC_SCATTER_WINDOW = 32
NEG = -1e30
LOG2E = math.log2(math.e)
B_STEP = 512
VMEM_LIMIT = 56 * 1024 * 1024
CACHE_BUFFERS = 3
OUT_ROUTER_VMEM = 60 * 1024 * 1024

F32 = jnp.float32
BF16 = jnp.bfloat16


def _t5_bucket_np(dist):
    dist = np.asarray(dist, np.int64)
    max_exact = NUM_BUCKETS // 2
    d = np.maximum(dist, 1).astype(np.float32)
    ratio = np.log(d / np.float32(max_exact)) / np.float32(math.log(MAX_DISTANCE / max_exact))
    large = max_exact + (ratio * np.float32(NUM_BUCKETS - max_exact)).astype(np.int32)
    large = np.minimum(large, NUM_BUCKETS - 1)
    return np.where(dist < max_exact, dist, large).astype(np.int32)


def _cparams(sem, vmem=VMEM_LIMIT):
    return pltpu.CompilerParams(dimension_semantics=sem, vmem_limit_bytes=vmem)


def _proj_prompt_kernel(x_ref, g_ref, w_ref, cs_ref, aperm_ref, qb_ref, kvb_ref, akv_ref, bkv_ref,
                        h_scr, p_scr):
    n = pl.program_id(1)

    @pl.when(n == 0)
    def _():
        x = x_ref[...]
        ms = jnp.mean(x * x, axis=-1, keepdims=True)
        h_scr[...] = (x * lax.rsqrt(ms + EPS) * g_ref[...]).astype(BF16)

    p = jnp.dot(h_scr[...], w_ref[...], preferred_element_type=F32) * cs_ref[...]

    @pl.when(n < 6)
    def _():
        aperm_ref[0] = p.astype(BF16)
        p_scr[0, 0] = p[:, :LANES]
        p_scr[0, 1] = p[:, LANES:]
        quarter = SPAN // 4
        for r in range(4):
            lo = p_scr[0, 0, pl.ds(r, quarter, stride=4), :]
            hi = p_scr[0, 1, pl.ds(r, quarter, stride=4), :]
            p_scr[1, 0, r * quarter:(r + 1) * quarter, :] = lo
            p_scr[1, 1, r * quarter:(r + 1) * quarter, :] = hi
            aperm_ref[1, r * quarter:(r + 1) * quarter, :] = jnp.concatenate([lo, hi], axis=1).astype(BF16)
        for r16 in range(16):
            start = (r16 % 4) * quarter + r16 // 4
            t = jnp.concatenate([p_scr[1, 0, pl.ds(start, QB, stride=4), :],
                                 p_scr[1, 1, pl.ds(start, QB, stride=4), :]], axis=1)
            aperm_ref[2, r16 * QB:(r16 + 1) * QB, :] = t.astype(BF16)

    @pl.when(n < 4)
    def _():
        akv_ref[...] = p

    @pl.when(jnp.logical_or(n == 6, n == 7))
    def _():
        qb_ref[...] = p.astype(BF16)

    @pl.when(n == 8)
    def _():
        kvb_ref[...] = p.astype(BF16)
        bkv_ref[...] = p


def _proj_prompt(x, gamma, w, cscale):
    s = x.shape[0]
    nspan = s // SPAN
    return pl.pallas_call(
        _proj_prompt_kernel,
        grid=(nspan, NCHUNK),
        in_specs=[
            pl.BlockSpec((SPAN, D_MODEL), lambda b, n: (b, 0)),
            pl.BlockSpec((1, D_MODEL), lambda b, n: (0, 0)),
            pl.BlockSpec((D_MODEL, 256), lambda b, n: (0, n)),
            pl.BlockSpec((1, 256), lambda b, n: (0, n)),
        ],
        out_specs=[
            pl.BlockSpec((3, SPAN, 256), lambda b, n: (0, b, jnp.minimum(n, 5))),
            pl.BlockSpec((SPAN, 256), lambda b, n: (b, jnp.clip(n - 6, 0, 1))),
            pl.BlockSpec((SPAN, 256), lambda b, n: (b, 0)),
            pl.BlockSpec((SPAN, 256), lambda b, n: (b, jnp.minimum(n, 3))),
            pl.BlockSpec((SPAN, 256), lambda b, n: (b, 0)),
        ],
        out_shape=[
            jax.ShapeDtypeStruct((3, s, 3 * A_WIDTH), BF16),
            jax.ShapeDtypeStruct((s, 512), BF16),
            jax.ShapeDtypeStruct((s, 256), BF16),
            jax.ShapeDtypeStruct((s, 1024), F32),
            jax.ShapeDtypeStruct((s, 256), F32),
        ],
        scratch_shapes=[pltpu.VMEM((SPAN, D_MODEL), BF16), pltpu.VMEM((2, 2, SPAN, LANES), F32)],
        compiler_params=_cparams(("arbitrary", "arbitrary")),
        name="proj_prompt",
    )(x, gamma, w, cscale)


def _proj_sample_kernel(x_ref, g_ref, w_ref, cs_ref, q_ref, kv_ref):
    x = x_ref[...]
    ms = jnp.mean(x * x, axis=-1, keepdims=True)
    h = (x * lax.rsqrt(ms + EPS) * g_ref[...]).astype(BF16)
    p = jnp.dot(h, w_ref[...], preferred_element_type=F32) * cs_ref[...]
    kv_ref[:, :1024] = p[:, :1024]
    kv_ref[:, 1024:] = p[:, 2048:]
    q_ref[...] = p[:, 1024:2048]


def _proj_sample(x, gamma, w, cscale):
    t = x.shape[0]
    tm = 512
    return pl.pallas_call(
        _proj_sample_kernel,
        grid=(t // tm,),
        in_specs=[
            pl.BlockSpec((tm, D_MODEL), lambda i: (i, 0)),
            pl.BlockSpec((1, D_MODEL), lambda i: (0, 0)),
            pl.BlockSpec((D_MODEL, 2304), lambda i: (0, 0)),
            pl.BlockSpec((1, 2304), lambda i: (0, 0)),
        ],
        out_specs=[
            pl.BlockSpec((tm, 1024), lambda i: (i, 0)),
            pl.BlockSpec((tm, 1280), lambda i: (i, 0)),
        ],
        out_shape=[
            jax.ShapeDtypeStruct((t, 1024), F32),
            jax.ShapeDtypeStruct((t, 1280), F32),
        ],
        compiler_params=_cparams(("arbitrary",)),
        name="proj_sample",
    )(x, gamma, w, cscale)


def _spread_heads(w, e3_ref):
    hi = w.astype(BF16)
    r1 = w - hi.astype(F32)
    mid = r1.astype(BF16)
    low = (r1 - mid.astype(F32)).astype(BF16)
    return jnp.dot(jnp.concatenate([hi, mid, low], axis=1), e3_ref[...], preferred_element_type=F32)


def _pair_tile(q2, kk, vv, bias_t, lo, sink=None):
    zero = jnp.zeros_like(q2)
    qq = jnp.concatenate([jnp.where(lo, q2, zero), jnp.where(lo, zero, q2)], axis=0)
    st = lax.dot_general(kk, qq, (((1,), (1,)), ((), ())), preferred_element_type=F32)
    st = st + bias_t
    m = jnp.max(st, axis=0, keepdims=True)
    if sink is not None:
        m = jnp.maximum(m, sink)
    p = jnp.exp2(st - m)
    den = jnp.sum(p, axis=0, keepdims=True)
    if sink is not None:
        den = den + jnp.exp2(sink - m)
    pn = (p * (1.0 / den)).astype(BF16)
    o = lax.dot_general(pn, vv, (((0,), (0,)), ((), ())), preferred_element_type=F32)
    return jnp.where(lo, o[:QB], o[QB:]), m + jnp.log2(den)


def _fill_band_tiles(h_ref, bias_scr):
    nk = 2 * QB
    prev = lax.broadcasted_iota(jnp.int32, (nk, nk), 0) < QB
    for pair in range(h_ref.shape[0] // 2):
        halves = []
        for hh in range(2):
            row = h_ref[2 * pair + hh:2 * pair + hh + 1, :]
            band = pltpu.roll(jnp.broadcast_to(row, (nk, nk)), 0, 1, stride=1, stride_axis=0)
            halves.append(band[:, :QB])
        tile = jnp.concatenate(halves, axis=1)
        bias_scr[0, pair] = tile
        bias_scr[1, pair] = jnp.where(prev, NEG, tile)


def _attn_a_kernel(q_ref, kvc_ref, kvp_ref, h_ref, e_ref, out_ref, o_scr, st_scr, bias_scr):
    b = pl.program_id(0)
    g = pl.program_id(1)
    nblk = jnp.where(g == 0, 16, jnp.where(g == 1, 4, 1))
    lane = lax.broadcasted_iota(jnp.int32, (QB, LANES), 1)
    lo = lane < HEAD_DIM

    @pl.when(b == 0)
    def _():
        _fill_band_tiles(h_ref, bias_scr.at[g])

    bias_ref = bias_scr.at[g]

    for cb in range(SPAN // QB):
        first = lax.rem(jnp.int32(cb), nblk) == 0
        rows = slice(cb * QB, (cb + 1) * QB)
        prow_c = max(cb - 1, 0) * QB
        prow_p = pl.multiple_of(jnp.where(first, cb + nblk - 1, 0) * QB, QB)
        variant = jnp.logical_and(first, b == 0).astype(jnp.int32)
        stats = []
        for hp in range(4):
            ks = slice(hp * LANES, (hp + 1) * LANES)
            vs = slice(A_WIDTH + hp * LANES, A_WIDTH + (hp + 1) * LANES)
            kp = jnp.where(first, kvp_ref[pl.ds(prow_p, QB), ks], kvc_ref[prow_c:prow_c + QB, ks])
            vp = jnp.where(first, kvp_ref[pl.ds(prow_p, QB), vs], kvc_ref[prow_c:prow_c + QB, vs])
            kk = jnp.concatenate([kp, kvc_ref[rows, ks]], axis=0)
            vv = jnp.concatenate([vp, kvc_ref[rows, vs]], axis=0)
            o, lse = _pair_tile(q_ref[rows, ks], kk, vv, bias_ref[variant, hp], lo)
            o_scr[g, hp, rows, :] = o
            stats += [lse[:, :QB], lse[:, QB:]]
        sm = jnp.concatenate(stats + [jnp.zeros((LANES - H_A, QB), F32)], axis=0)
        st_scr[g, rows, :] = sm.T

    @pl.when(g == 2)
    def _():
        def merge(c, carry):
            r2 = lax.rem(c, 4) * (SPAN // 4) + c // 4
            r3 = pl.multiple_of(c * QB, QB)
            l1 = st_scr[0, pl.ds(c, QB, stride=16), :]
            l2 = st_scr[1, pl.ds(r2, QB, stride=4), :]
            l3 = st_scr[2, pl.ds(r3, QB), :]
            mx = jnp.maximum(jnp.maximum(l1, l2), l3)
            w1 = jnp.exp2(l1 - mx)
            w2 = jnp.exp2(l2 - mx)
            w3 = jnp.exp2(l3 - mx)
            tot = w1 + w2 + w3
            a1 = _spread_heads(w1 / tot, e_ref)
            a2 = _spread_heads(w2 / tot, e_ref)
            a3 = _spread_heads(w3 / tot, e_ref)
            for hp in range(4):
                sl = slice(hp * LANES, (hp + 1) * LANES)
                o1 = o_scr[0, hp, pl.ds(c, QB, stride=16), :]
                o2 = o_scr[1, hp, pl.ds(r2, QB, stride=4), :]
                o3 = o_scr[2, hp, pl.ds(r3, QB), :]
                out_ref[hp, pl.ds(c, QB, stride=16), :] = a1[:, sl] * o1 + a2[:, sl] * o2 + a3[:, sl] * o3
            return carry

        lax.fori_loop(0, 16, merge, 0, unroll=4)


def _attn_a_prompt(aperm, bias_a, emat):
    s = aperm.shape[1]
    nspan = s // SPAN
    return pl.pallas_call(
        _attn_a_kernel,
        grid=(nspan, 3),
        in_specs=[
            pl.BlockSpec((None, SPAN, A_WIDTH), lambda b, g: (g, b, 2)),
            pl.BlockSpec((None, SPAN, 2 * A_WIDTH), lambda b, g: (g, b, 0)),
            pl.BlockSpec((None, SPAN, 2 * A_WIDTH), lambda b, g: (g, jnp.maximum(b - 1, 0), 0)),
            pl.BlockSpec((None, H_A, 2 * QB), lambda b, g: (g, 0, 0)),
            pl.BlockSpec((3 * LANES, A_WIDTH), lambda b, g: (0, 0)),
        ],
        out_specs=pl.BlockSpec((4, SPAN, LANES), lambda b, g: (0, b, 0)),
        out_shape=jax.ShapeDtypeStruct((4, s, LANES), F32),
        scratch_shapes=[pltpu.VMEM((3, 4, SPAN, LANES), F32), pltpu.VMEM((3, SPAN, LANES), F32),
                        pltpu.VMEM((3, 2, 4, 2 * QB, 2 * QB), F32)],
        compiler_params=_cparams(("arbitrary", "arbitrary")),
        name="attn_a_prompt",
    )(aperm, aperm, aperm, bias_a, emat)


def _attn_b_kernel(q_ref, kvc_ref, kvp_ref, h_ref, sink_ref, out_ref, bias_ref):
    i = pl.program_id(0)
    lane = lax.broadcasted_iota(jnp.int32, (QB, LANES), 1)
    lo = lane < HEAD_DIM

    @pl.when(i == 0)
    def _():
        _fill_band_tiles(h_ref, bias_ref)

    variant = (i == 0).astype(jnp.int32)
    for j in range(B_STEP // QB):
        rows = slice(j * QB, (j + 1) * QB)
        if j == 0:
            kp, vp = kvp_ref[:, :LANES], kvp_ref[:, LANES:]
        else:
            kp, vp = kvc_ref[(j - 1) * QB:j * QB, :LANES], kvc_ref[(j - 1) * QB:j * QB, LANES:]
        kk = jnp.concatenate([kp, kvc_ref[rows, :LANES]], axis=0)
        vv = jnp.concatenate([vp, kvc_ref[rows, LANES:]], axis=0)
        for g in range(G_B):
            bias_t = bias_ref[variant, g] if j == 0 else bias_ref[0, g]
            o, _ = _pair_tile(q_ref[rows, g * LANES:(g + 1) * LANES], kk, vv, bias_t, lo, sink=sink_ref[g])
            out_ref[rows, g * LANES:(g + 1) * LANES] = o.astype(BF16)


def _attn_b_prompt(qb, kvb, bias_b, sink_rows):
    s = qb.shape[0]
    per = B_STEP // QB
    return pl.pallas_call(
        _attn_b_kernel,
        grid=(s // B_STEP,),
        in_specs=[
            pl.BlockSpec((B_STEP, 512), lambda i: (i, 0)),
            pl.BlockSpec((B_STEP, 256), lambda i: (i, 0)),
            pl.BlockSpec((QB, 256), lambda i: (jnp.maximum(i * per - 1, 0), 0)),
            pl.BlockSpec((H_B, 2 * QB), lambda i: (0, 0)),
            pl.BlockSpec((G_B, 1, 2 * QB), lambda i: (0, 0, 0)),
        ],
        out_specs=pl.BlockSpec((B_STEP, 512), lambda i: (i, 0)),
        out_shape=jax.ShapeDtypeStruct((s, 512), BF16),
        scratch_shapes=[pltpu.VMEM((2, G_B, 2 * QB, 2 * QB), F32)],
        compiler_params=_cparams(("arbitrary",)),
        name="attn_b_prompt",
    )(qb, kvb, kvb, bias_b, sink_rows)


def _sample_attention(q, kvn, akt, avt, bkt, bvt, cba, cbb, sink):
    t = q.shape[0]
    kvn_p = jnp.concatenate([kvn, jnp.zeros((LANES - t, kvn.shape[1]), F32)], axis=0).astype(BF16)
    lane_a = lax.broadcasted_iota(jnp.int32, (t, A_WIDTH), 1) // HEAD_DIM

    qa = q[:, :A_WIDTH]
    qbd = jnp.concatenate([jnp.where(lane_a == h, qa, 0.0) for h in range(H_A)], axis=0).astype(BF16)
    s_c = jnp.dot(qbd, akt.astype(BF16), preferred_element_type=F32)
    s_n = lax.dot_general(qbd, kvn_p[:, :A_WIDTH], (((1,), (1,)), ((), ())), preferred_element_type=F32)
    s = jnp.concatenate([s_c, s_n], axis=1) + cba
    m = jnp.max(s, axis=-1, keepdims=True)
    p = jnp.exp2(s - m)
    l = jnp.sum(p, axis=-1, keepdims=True)
    pb = p.astype(BF16)
    o_n = jnp.dot(pb[:, WIN_A:], kvn_p[:, A_WIDTH:2 * A_WIDTH], preferred_element_type=F32)
    o_all = lax.dot_general(pb[:, :WIN_A], avt.astype(BF16), (((1,), (1,)), ((), ())),
                            preferred_element_type=F32) + o_n
    o_sel = jnp.zeros((t, A_WIDTH), F32)
    l_b = jnp.ones((t, A_WIDTH), F32)
    for h in range(H_A):
        sel = lane_a == h
        o_sel = jnp.where(sel, o_all[h * t:(h + 1) * t], o_sel)
        l_b = jnp.where(sel, l[h * t:(h + 1) * t], l_b)
    oa = o_sel / l_b

    lane_b = lax.broadcasted_iota(jnp.int32, (G_B * t, LANES), 1)
    lo = lane_b < HEAD_DIM
    qb2 = jnp.concatenate([q[:, A_WIDTH + g * LANES:A_WIDTH + (g + 1) * LANES] for g in range(G_B)], axis=0)
    qm = jnp.concatenate([jnp.where(lo, qb2, 0.0), jnp.where(lo, 0.0, qb2)], axis=0).astype(BF16)
    kb_n = kvn_p[:, 2 * A_WIDTH:2 * A_WIDTH + LANES]
    vb_n = kvn_p[:, 2 * A_WIDTH + LANES:]
    sb_c = jnp.dot(qm, bkt.astype(BF16), preferred_element_type=F32)
    sb_n = lax.dot_general(qm, kb_n, (((1,), (1,)), ((), ())), preferred_element_type=F32)
    sb = jnp.concatenate([sb_c, sb_n], axis=1) + cbb
    mb = jnp.maximum(jnp.max(sb, axis=-1, keepdims=True), sink)
    pbb = jnp.exp2(sb - mb)
    den = jnp.sum(pbb, axis=-1, keepdims=True) + jnp.exp2(sink - mb)
    pbb = pbb.astype(BF16)
    ob = lax.dot_general(pbb[:, :WIN_B], bvt.astype(BF16), (((1,), (1,)), ((), ())),
                         preferred_element_type=F32)
    ob = (ob + jnp.dot(pbb[:, WIN_B:], vb_n, preferred_element_type=F32)) / den
    half = G_B * t
    lo8 = lo[:t]
    ob = jnp.concatenate([jnp.where(lo8, ob[g * t:(g + 1) * t], ob[half + g * t:half + (g + 1) * t])
                          for g in range(G_B)], axis=1)
    return oa, ob


def _route(logits):
    lane = lax.broadcasted_iota(jnp.int32, logits.shape, 1).astype(F32)
    big = jnp.float32(1 << 20)
    ninf = jnp.float32(-jnp.inf)
    gmask = lane < N_GROUPS
    lg = jnp.where(gmask, logits, ninf)
    gmax = jnp.max(lg, axis=-1, keepdims=True)
    grp = jnp.min(jnp.where(lg == gmax, lane, big), axis=-1, keepdims=True)
    pg_top = 1.0 / jnp.sum(jnp.exp(lg - gmax), axis=-1, keepdims=True)
    e0 = N_GROUPS + grp * EXPERTS_PER_GROUP
    emask = jnp.logical_and(lane >= e0, lane < e0 + EXPERTS_PER_GROUP)
    le = jnp.where(emask, logits, ninf)
    emax = jnp.max(le, axis=-1, keepdims=True)
    esum = jnp.sum(jnp.exp(le - emax), axis=-1, keepdims=True)
    i1 = jnp.min(jnp.where(le == emax, lane, big), axis=-1, keepdims=True)
    le2 = jnp.where(lane == i1, ninf, le)
    e2max = jnp.max(le2, axis=-1, keepdims=True)
    i2 = jnp.min(jnp.where(le2 == e2max, lane, big), axis=-1, keepdims=True)
    p1 = 1.0 / esum
    p2 = jnp.exp(e2max - emax) / esum
    g1 = pg_top * p1 / (p1 + p2)
    g2 = pg_top * p2 / (p1 + p2)
    out = jnp.where(lane == 0, i1 - N_GROUPS, 0.0)
    out = jnp.where(lane == 1, i2 - N_GROUPS, out)
    out = jnp.where(lane == 2, g1, out)
    out = jnp.where(lane == 3, g2, out)
    return out


def _pack_bf16_pairs(x):
    half = x.shape[1] // 2

    def rne(v):
        bits = lax.bitcast_convert_type(v, jnp.int32)
        return bits + 0x7FFF + (lax.shift_right_logical(bits, 16) & 1)

    lo = lax.shift_right_logical(rne(x[:, :half]), 16)
    hi = rne(x[:, half:]) & jnp.int32(-65536)
    return lo | hi


def _unpack_bf16_pairs(w):
    lo = lax.bitcast_convert_type(lax.shift_left(w, 16), F32)
    hi = lax.bitcast_convert_type(w & jnp.int32(-65536), F32)
    return jnp.concatenate([lo, hi], axis=1)


def _out_router_kernel(xp_ref, ap_ref, bp_ref, xs_ref, q_ref, kvn_ref, akt_hbm, avt_hbm, bkt_ref, bvt_ref,
                       cba_ref, cbb_ref, sink_ref, wo_ref, g_ref, wr_ref, br_ref,
                       x1_ref, xn_ref, route_ref, cnt_ref,
                       xcat_scr, mix_scr, kbuf, vbuf, sem, *, prompt_tiles, decode_tiles, seqs_per_step):
    i = pl.program_id(0)
    seqs = prompt_tiles * seqs_per_step

    def cache_copies(n, slot):
        return (pltpu.make_async_copy(akt_hbm.at[n], kbuf.at[slot], sem.at[0, slot]),
                pltpu.make_async_copy(avt_hbm.at[n], vbuf.at[slot], sem.at[1, slot]))

    @pl.when(i == 0)
    def _():
        cnt_ref[...] = jnp.zeros_like(cnt_ref)
        xcat_scr[...] = jnp.zeros_like(xcat_scr)
        for n0 in range(2):
            for c in cache_copies(n0, n0):
                c.start()

    @pl.when(i == prompt_tiles)
    def _():
        for n1 in (seqs, seqs + 1):
            for c in cache_copies(seqs - 1, n1 % CACHE_BUFFERS):
                c.wait()

    pslot = lax.rem(i, 2)

    def route_previous():
        logits = jnp.dot(xcat_scr[1 - pslot], wr_ref[...], preferred_element_type=F32)
        route = _route(logits + br_ref[...])
        route_ref[...] = route
        lanef = lax.broadcasted_iota(jnp.int32, route.shape, 1).astype(F32)
        hits = (lanef == route[:, 0:1]).astype(F32) + (lanef == route[:, 1:2]).astype(F32)
        cnt_ref[...] += jnp.sum(hits, axis=0, keepdims=True) * (i > 0).astype(F32)

    def project(x_ref, mix):
        x1 = x_ref[...] + jnp.dot(mix, wo_ref[...], preferred_element_type=F32)
        x1_ref[...] = x1
        ms = jnp.mean(x1 * x1, axis=-1, keepdims=True)
        xn = x1 * lax.rsqrt(ms + EPS) * g_ref[...]
        xn_ref[...] = _pack_bf16_pairs(xn)
        xh = xn.astype(BF16)
        xl = (xn - xh.astype(F32)).astype(BF16)
        xcat_scr[pslot] = jnp.concatenate([xh, xl, xh], axis=1)

    @pl.when(i < prompt_tiles)
    def _():
        route_previous()
        mix = jnp.concatenate([ap_ref[0], ap_ref[1], ap_ref[2], ap_ref[3]], axis=1).astype(BF16)
        project(xp_ref, jnp.concatenate([mix, bp_ref[...]], axis=1))
        t = q_ref.shape[1]
        for s in range(seqs_per_step):
            n = i * seqs_per_step + s
            slot = lax.rem(n, CACHE_BUFFERS)
            for c in cache_copies(n, slot):
                c.wait()
            for c in cache_copies(jnp.minimum(n + 2, seqs - 1), lax.rem(n + 2, CACHE_BUFFERS)):
                c.start()
            oa, ob = _sample_attention(q_ref[s], kvn_ref[s], kbuf[slot], vbuf[slot], bkt_ref[s], bvt_ref[s],
                                       cba_ref[...], cbb_ref[...], sink_ref[...])
            row = pl.multiple_of(n * t, t)
            mix_scr[pl.ds(row, t), :A_WIDTH] = oa
            mix_scr[pl.ds(row, t), A_WIDTH:] = ob

    @pl.when(i >= prompt_tiles)
    def _():
        route_previous()
        tm = xs_ref.shape[0]
        row = pl.multiple_of(jnp.clip(i - prompt_tiles, 0, decode_tiles - 1) * tm, tm)
        project(xs_ref, mix_scr[pl.ds(row, tm), :].astype(BF16))


def _out_router(xp, a4p, bp, xs, q3, kvn3, akt, avt, bkt, bvt, cbias_a, cbias_b, sink_rows, wo, gamma, wr, br):
    tp, tsm = xp.shape[0], xs.shape[0]
    ns, ts = q3.shape[0], q3.shape[1]
    tm = 512
    npt, nst = tp // tm, tsm // tm
    nt = npt + nst
    t = tp + tsm
    sps = ns // npt
    assert sps * npt == ns and ns * ts == tsm and ns >= CACHE_BUFFERS
    pmap = lambda i: (jnp.minimum(i, npt - 1), 0)
    pmap3 = lambda i: (jnp.minimum(i, npt - 1), 0, 0)
    smap = lambda i: (jnp.clip(i - npt, 0, nst - 1), 0)
    cur = lambda i: (jnp.minimum(i, nt - 1), 0)
    const = lambda i: (0, 0)
    return pl.pallas_call(
        functools.partial(_out_router_kernel, prompt_tiles=npt, decode_tiles=nst, seqs_per_step=sps),
        grid=(nt + 1,),
        in_specs=[
            pl.BlockSpec((tm, D_MODEL), pmap),
            pl.BlockSpec((4, tm, LANES), lambda i: (0, jnp.minimum(i, npt - 1), 0)),
            pl.BlockSpec((tm, 512), pmap),
            pl.BlockSpec((tm, D_MODEL), smap),
            pl.BlockSpec((sps, ts, 1024), pmap3),
            pl.BlockSpec((sps, ts, 1280), pmap3),
            pl.BlockSpec(memory_space=pl.ANY),
            pl.BlockSpec(memory_space=pl.ANY),
            pl.BlockSpec((sps, LANES, WIN_B), pmap3),
            pl.BlockSpec((sps, LANES, WIN_B), pmap3),
            pl.BlockSpec((H_A * ts, WIN_A + LANES), const),
            pl.BlockSpec((H_B * ts, WIN_B + LANES), const),
            pl.BlockSpec((H_B * ts, 1), const),
            pl.BlockSpec((D_MODEL, D_MODEL), const),
            pl.BlockSpec((1, D_MODEL), const),
            pl.BlockSpec((3 * D_MODEL, LANES), const),
            pl.BlockSpec((1, LANES), const),
        ],
        out_specs=[
            pl.BlockSpec((tm, D_MODEL), cur),
            pl.BlockSpec((tm, D_MODEL // 2), cur),
            pl.BlockSpec((tm, LANES), lambda i: (jnp.maximum(i - 1, 0), 0)),
            pl.BlockSpec((1, LANES), const),
        ],
        scratch_shapes=[pltpu.VMEM((2, tm, 3 * D_MODEL), BF16), pltpu.VMEM((tsm, D_MODEL), F32),
                        pltpu.VMEM((CACHE_BUFFERS, A_WIDTH, WIN_A), F32),
                        pltpu.VMEM((CACHE_BUFFERS, A_WIDTH, WIN_A), F32),
                        pltpu.SemaphoreType.DMA((2, CACHE_BUFFERS))],
        out_shape=[
            jax.ShapeDtypeStruct((t, D_MODEL), F32),
            jax.ShapeDtypeStruct((t, D_MODEL // 2), jnp.int32),
            jax.ShapeDtypeStruct((t, LANES), F32),
            jax.ShapeDtypeStruct((1, LANES), F32),
        ],
        compiler_params=_cparams(("arbitrary",), vmem=OUT_ROUTER_VMEM),
        name="out_router",
    )(xp, a4p, bp, xs, q3, kvn3, akt, avt, bkt, bvt, cbias_a, cbias_b, sink_rows, wo, gamma, wr, br)


def _sc_gather_rows(table, idx):
    b = idx.shape[0]
    d = table.shape[1]
    w = SC_WINDOW
    per_worker = b // SC_WORKERS
    nwin = per_worker // w
    assert per_worker * SC_WORKERS == b and nwin * w == per_worker
    mesh = plsc.VectorSubcoreMesh(core_axis_name="c", subcore_axis_name="s")

    @functools.partial(
        pl.kernel, mesh=mesh,
        out_type=jax.ShapeDtypeStruct((b, d), table.dtype),
        scratch_types=[pltpu.VMEM((nwin, w), jnp.int32), pltpu.VMEM((2, w, d), table.dtype),
                       pltpu.SemaphoreType.DMA((2,)), pltpu.SemaphoreType.DMA((2,))],
        name="sc_gather_rows",
    )
    def gather(table_hbm, idx_hbm, out_hbm, idx_v, rows_v, sem_in, sem_out):
        wid = lax.axis_index("s") * SC_CORES + lax.axis_index("c")
        base = wid * per_worker
        pltpu.sync_copy(idx_hbm.at[wid], idx_v)

        def fetch(j):
            return pltpu.make_async_copy(table_hbm.at[idx_v.at[j]], rows_v.at[j % 2], sem_in.at[j % 2])

        def flush(j):
            return pltpu.make_async_copy(rows_v.at[j % 2], out_hbm.at[pl.ds(base + j * w, w)],
                                         sem_out.at[j % 2])

        fetch(0).start()
        for j in range(nwin):
            fetch(j).wait()
            if j + 1 < nwin:
                if j >= 1:
                    flush(j - 1).wait()
                fetch(j + 1).start()
            flush(j).start()
        for j in range(max(nwin - 2, 0), nwin):
            flush(j).wait()

    return gather(table, idx.reshape(SC_WORKERS, nwin, w))


def _sc_scatter_rows(x, dest2, nrows):
    t, d = x.shape
    w = SC_SCATTER_WINDOW
    per_worker = t // SC_WORKERS
    nwin = per_worker // w
    assert per_worker * SC_WORKERS == t and nwin * w == per_worker
    mesh = plsc.VectorSubcoreMesh(core_axis_name="c", subcore_axis_name="s")

    @functools.partial(
        pl.kernel, mesh=mesh,
        out_type=jax.ShapeDtypeStruct((nrows, d), x.dtype),
        scratch_types=[pltpu.VMEM((TOP_K, nwin, w), jnp.int32), pltpu.VMEM((2, w, d), x.dtype),
                       pltpu.SemaphoreType.DMA((2,)), pltpu.SemaphoreType.DMA((2,))],
        name="sc_scatter_rows",
    )
    def scatter(x_hbm, dest_hbm, out_hbm, idx_v, rows_v, sem_in, sem_out):
        wid = lax.axis_index("s") * SC_CORES + lax.axis_index("c")
        base = wid * per_worker
        for k in range(TOP_K):
            pltpu.sync_copy(dest_hbm.at[k, wid], idx_v.at[k])

        def fetch(j):
            return pltpu.make_async_copy(x_hbm.at[pl.ds(base + j * w, w)], rows_v.at[j % 2], sem_in.at[j % 2])

        def spread(j, k):
            return pltpu.make_async_copy(rows_v.at[j % 2], out_hbm.at[idx_v.at[k, j]], sem_out.at[j % 2])

        fetch(0).start()
        for j in range(nwin):
            fetch(j).wait()
            if j + 1 < nwin:
                if j >= 1:
                    for k in range(TOP_K):
                        spread(j - 1, k).wait()
                fetch(j + 1).start()
            for k in range(TOP_K):
                spread(j, k).start()
        for j in range(max(nwin - 2, 0), nwin):
            for k in range(TOP_K):
                spread(j, k).wait()

    return scatter(x, dest2.reshape(TOP_K, SC_WORKERS, nwin, w))


def _expert_kernel(be_ref, nu_ref, nv_ref, nx_ref, x_ref, wg_hbm, wu_hbm, wd_hbm, o_ref,
                   wg_s, wu_s, wd_s, wg_f, wu_f, wd_f, slot_s, sem):
    i = pl.program_id(0)
    used = i < nu_ref[0]
    changed = jnp.logical_or(i == 0, be_ref[i] != be_ref[jnp.maximum(i - 1, 0)])

    def weight_copies(e, slot):
        return (pltpu.make_async_copy(wg_hbm.at[e], wg_f.at[slot], sem.at[slot, 0]),
                pltpu.make_async_copy(wu_hbm.at[e], wu_f.at[slot], sem.at[slot, 1]),
                pltpu.make_async_copy(wd_hbm.at[e], wd_f.at[slot], sem.at[slot, 2]))

    @pl.when(i == 0)
    def _():
        slot_s[0] = 0
        for c in weight_copies(be_ref[0], 0):
            c.start()

    @pl.when(jnp.logical_and(used, changed))
    def _():
        slot = slot_s[0]
        for c in weight_copies(be_ref[i], slot):
            c.wait()
        wg_s[...] = wg_f[slot].astype(BF16)
        wu_s[...] = wu_f[slot].astype(BF16)
        wd_s[...] = wd_f[slot].astype(BF16)

        @pl.when(nx_ref[i] != be_ref[i])
        def _():
            for c in weight_copies(nx_ref[i], 1 - slot):
                c.start()

        slot_s[0] = 1 - slot

    @pl.when(used)
    def _():
        row = lax.broadcasted_iota(jnp.int32, x_ref.shape, 0)
        x = _unpack_bf16_pairs(jnp.where(row < nv_ref[i], x_ref[...], 0)).astype(BF16)
        gate = jnp.dot(x, wg_s[...], preferred_element_type=F32)
        up = jnp.dot(x, wu_s[...], preferred_element_type=F32)
        h = (gate * jax.nn.sigmoid(gate) * up).astype(BF16)
        o_ref[...] = _pack_bf16_pairs(jnp.dot(h, wd_s[...], preferred_element_type=F32))

    @pl.when(jnp.logical_not(used))
    def _():
        o_ref[...] = jnp.zeros_like(o_ref)


def _experts(blk_e, n_used, nvalid, next_e, xb, w_gate, w_up, w_down):
    rows = xb.shape[0]
    nblocks = rows // MOE_ROWS
    grid_spec = pltpu.PrefetchScalarGridSpec(
        num_scalar_prefetch=4,
        grid=(nblocks,),
        in_specs=[
            pl.BlockSpec((MOE_ROWS, D_MODEL // 2), lambda i, be, nu, nv, nx: (i, 0)),
            pl.BlockSpec(memory_space=pl.ANY),
            pl.BlockSpec(memory_space=pl.ANY),
            pl.BlockSpec(memory_space=pl.ANY),
        ],
        out_specs=pl.BlockSpec((MOE_ROWS, D_MODEL // 2), lambda i, be, nu, nv, nx: (i, 0)),
        scratch_shapes=[pltpu.VMEM((D_MODEL, D_EXPERT), BF16), pltpu.VMEM((D_MODEL, D_EXPERT), BF16),
                        pltpu.VMEM((D_EXPERT, D_MODEL), BF16),
                        pltpu.VMEM((2, D_MODEL, D_EXPERT), F32), pltpu.VMEM((2, D_MODEL, D_EXPERT), F32),
                        pltpu.VMEM((2, D_EXPERT, D_MODEL), F32),
                        pltpu.SMEM((1,), jnp.int32), pltpu.SemaphoreType.DMA((2, 3))],
    )
    return pl.pallas_call(
        _expert_kernel,
        grid_spec=grid_spec,
        out_shape=jax.ShapeDtypeStruct((rows, D_MODEL // 2), jnp.int32),
        compiler_params=_cparams(("arbitrary",)),
        name="experts",
    )(blk_e, n_used, nvalid, next_e, xb, w_gate, w_up, w_down)


def _combine_kernel(x1_ref, y1_ref, y2_ref, route_ref, g_ref, outp_ref, outs_ref, *, prompt_tiles):
    r = route_ref[...]
    x = (x1_ref[...] + r[:, 2:3] * _unpack_bf16_pairs(y1_ref[...])
         + r[:, 3:4] * _unpack_bf16_pairs(y2_ref[...]))
    ms = jnp.mean(x * x, axis=-1, keepdims=True)
    y = x * lax.rsqrt(ms + EPS) * g_ref[...]
    i = pl.program_id(0)

    @pl.when(i < prompt_tiles)
    def _():
        outp_ref[...] = y

    @pl.when(i >= prompt_tiles)
    def _():
        outs_ref[...] = y


def _combine_norm(x1, ygath, route, gamma, tp):
    t = x1.shape[0]
    tm = 512
    nt, npt = t // tm, tp // tm
    return pl.pallas_call(
        functools.partial(_combine_kernel, prompt_tiles=npt),
        grid=(nt,),
        in_specs=[
            pl.BlockSpec((tm, D_MODEL), lambda i: (i, 0)),
            pl.BlockSpec((tm, D_MODEL // 2), lambda i: (i, 0)),
            pl.BlockSpec((tm, D_MODEL // 2), lambda i: (i + nt, 0)),
            pl.BlockSpec((tm, LANES), lambda i: (i, 0)),
            pl.BlockSpec((1, D_MODEL), lambda i: (0, 0)),
        ],
        out_specs=[
            pl.BlockSpec((tm, D_MODEL), lambda i: (jnp.minimum(i, npt - 1), 0)),
            pl.BlockSpec((tm, D_MODEL), lambda i: (jnp.maximum(i - npt, 0), 0)),
        ],
        out_shape=[jax.ShapeDtypeStruct((tp, D_MODEL), F32), jax.ShapeDtypeStruct((t - tp, D_MODEL), F32)],
        compiler_params=_cparams(("arbitrary",)),
        name="combine_norm",
    )(x1, ygath, ygath, route, gamma)


def _band_index():
    c = (2 * QB - np.arange(2 * QB)) % (2 * QB)
    return c, c <= QB


def _bias_a_prompt(table_a):
    c, valid = _band_index()
    idx = np.stack([_t5_bucket_np(d * np.clip(QB - c, 0, QB)) for d in DILATIONS])
    return jnp.where(valid, jnp.transpose(table_a[idx], (0, 2, 1)) * LOG2E, NEG)


def _bias_b_prompt(table_b):
    c, valid = _band_index()
    valid = valid & (c >= 1)
    h = jnp.where(valid, table_b[_t5_bucket_np(np.clip(QB - c, 0, QB))].T * LOG2E, NEG)
    return jnp.transpose(h.reshape(KV_B, G_B, 2 * QB), (1, 0, 2)).reshape(H_B, 2 * QB)


def _sample_bias(table, span, t, log2_weight):
    cols = span + LANES
    period = cols + LANES
    x = np.arange(period)
    dist = np.where(x >= period - t, span - x + period, span - x)
    extra = log2_weight(dist)
    valid = np.isfinite(extra)
    u = jnp.where(valid, table[_t5_bucket_np(np.maximum(dist, 0))].T * LOG2E
                  + np.where(valid, extra, 0.0).astype(np.float32), NEG)
    rows = jnp.tile(u, (1, t))[:, :t * (period - 1)].reshape(u.shape[0], t, period - 1)[:, :, :cols]
    return rows.reshape(u.shape[0] * t, cols)


def _bias_a_sample(table_a, t):
    def log2_count(dist):
        count = np.zeros(dist.shape, np.int64)
        for w, d in zip(WINDOWS, DILATIONS):
            count += (dist >= 0) & (dist % d == 0) & (dist <= w)
        return np.where(count > 0, np.log2(np.maximum(count, 1)), -np.inf)

    return _sample_bias(table_a, WIN_A, t, log2_count)


def _bias_b_sample(table_b, t):
    return _sample_bias(table_b, WIN_B, t,
                        lambda dist: np.where((dist >= 0) & (dist < WIN_B), 0.0, -np.inf))


def _dest_kernel(route_ref, cnt_ref, tri_ref, dest_ref, meta_ref, run_scr, pst_scr):
    i = pl.program_id(0)
    tm = route_ref.shape[0]
    r = route_ref[...]
    lane = lax.broadcasted_iota(jnp.int32, (tm, LANES), 1)
    lanef = lane.astype(F32)
    oh0 = lanef == r[:, 0:1]
    oh1 = lanef == r[:, 1:2]
    ohf = jnp.concatenate([oh0, oh1], axis=0).astype(F32)

    @pl.when(i == 0)
    def _():
        cnt = jnp.broadcast_to(cnt_ref[...], (LANES, LANES))
        padded = jnp.floor((cnt + (MOE_ROWS - 1)) * (1.0 / MOE_ROWS)) * MOE_ROWS
        lane_e = lax.broadcasted_iota(jnp.int32, (LANES, LANES), 1)
        x = padded
        for sh in (1, 2, 4, 8, 16, 32, 64):
            x = x + jnp.where(lane_e >= sh, pltpu.roll(x, sh, 1), 0.0)
        pst_scr[...] = (x - padded)[0:1]
        run_scr[...] = jnp.zeros_like(run_scr)
        wide = lambda v: jnp.concatenate([v.T, v.T], axis=1)
        cnt_t, bend_t = wide(cnt), wide(x * (1.0 / MOE_ROWS))
        bstart_t = wide((x - padded) * (1.0 / MOE_ROWS))
        blk = lax.broadcasted_iota(jnp.int32, (LANES, 2 * LANES), 1).astype(F32)
        exp = lax.broadcasted_iota(jnp.int32, (LANES, 2 * LANES), 0)
        real = exp < N_EXPERTS
        blk_e = jnp.minimum(jnp.sum(jnp.where(real & (bend_t <= blk), 1.0, 0.0), axis=0, keepdims=True),
                            N_EXPERTS - 1.0)
        mine = exp.astype(F32) == blk_e
        within = blk[0:1] - jnp.sum(jnp.where(mine, bstart_t, 0.0), axis=0, keepdims=True)
        nvalid = jnp.clip(jnp.sum(jnp.where(mine, cnt_t, 0.0), axis=0, keepdims=True) - within * MOE_ROWS,
                          0.0, float(MOE_ROWS))
        n_used = jnp.max(jnp.where(real, bend_t, 0.0), axis=0, keepdims=True)
        later = real & (exp.astype(F32) > blk_e) & (cnt_t > 0.0)
        nxt = jnp.min(jnp.where(later, exp.astype(F32), float(LANES)), axis=0, keepdims=True)
        nxt = jnp.where(nxt >= N_EXPERTS, blk_e, nxt)
        meta_ref[...] = jnp.concatenate([blk_e, nvalid, n_used, nxt, jnp.zeros((4, 2 * LANES), F32)],
                                        axis=0).astype(jnp.int32)

    base = run_scr[...] + pst_scr[...] - 1.0
    vals = []
    for c in range(2 * tm // LANES):
        ohc = ohf[c * LANES:(c + 1) * LANES]
        vals.append(jnp.dot(tri_ref[...], ohc.astype(BF16), preferred_element_type=F32) + base)
        base = base + jnp.sum(ohc, axis=0, keepdims=True)
    val = jnp.concatenate(vals, axis=0)
    d0 = jnp.sum(jnp.where(oh0, val[:tm], 0.0), axis=-1, keepdims=True)
    d1 = jnp.sum(jnp.where(oh1, val[tm:], 0.0), axis=-1, keepdims=True)
    tile = jnp.where(lane == 0, d0, jnp.where(lane == 1, d1, 0.0))
    dest_ref[...] = tile.T[:8].astype(jnp.int32)
    run_scr[...] += jnp.sum(ohf, axis=0, keepdims=True)


def _dispatch(route, cnt):
    t = route.shape[0]
    tm = 512
    tri = (jnp.arange(LANES)[:, None] >= jnp.arange(LANES)[None, :]).astype(BF16)
    nblocks = -(-t * TOP_K // MOE_ROWS) + N_EXPERTS
    assert nblocks <= 2 * LANES
    dest, meta = pl.pallas_call(
        _dest_kernel,
        grid=(t // tm,),
        in_specs=[pl.BlockSpec((tm, LANES), lambda i: (i, 0)),
                  pl.BlockSpec((1, LANES), lambda i: (0, 0)),
                  pl.BlockSpec((LANES, LANES), lambda i: (0, 0))],
        out_specs=[pl.BlockSpec((8, tm), lambda i: (0, i)),
                   pl.BlockSpec((8, 2 * LANES), lambda i: (0, 0))],
        out_shape=[jax.ShapeDtypeStruct((8, t), jnp.int32), jax.ShapeDtypeStruct((8, 2 * LANES), jnp.int32)],
        scratch_shapes=[pltpu.VMEM((1, LANES), F32), pltpu.VMEM((1, LANES), F32)],
        compiler_params=_cparams(("arbitrary",)),
        name="moe_dest",
    )(route, cnt, tri)
    return dest[:TOP_K], meta[0, :nblocks], meta[2, :1], meta[1, :nblocks], meta[3, :nblocks]


def kernel(x_prompt, x_sample, cache_a_k, cache_a_v, cache_b_k, cache_b_v, rel_bias_table, attn_norm, w_in,
           w_out, attn_sinks, ffn_norm, w_router_group, b_router_group, w_router_expert, b_router_expert,
           w_gate, w_up, w_down, final_norm):
    s = x_prompt.shape[1]
    ns, ts = x_sample.shape[0], x_sample.shape[1]
    table_a = rel_bias_table[:, :H_A]
    table_b = rel_bias_table[:, H_A:]

    w = w_in[0]
    wqa, wka, wva, wqb, wkb, wvb = (w[:, 0:512], w[:, 512:1024], w[:, 1024:1536], w[:, 1536:2048],
                                    w[:, 2048:2176], w[:, 2176:2304])
    wqb = jnp.transpose(wqb.reshape(D_MODEL, KV_B, G_B, HEAD_DIM), (0, 2, 1, 3)).reshape(D_MODEL, 512)
    wp = jnp.concatenate([wka, wva, wqa, wqb, wkb, wvb], axis=1).astype(BF16)
    cscale = jnp.concatenate([jnp.ones((1, 1024), F32), jnp.full((1, 1024), SCALE * LOG2E, F32),
                              jnp.ones((1, 256), F32)], axis=1)
    wo = w_out[0]
    wo_b = jnp.transpose(wo[512:].reshape(KV_B, G_B, HEAD_DIM, D_MODEL), (1, 0, 2, 3)).reshape(512, D_MODEL)
    wo_p = jnp.concatenate([wo[:512], wo_b], axis=0).astype(BF16)
    wr = jnp.concatenate([w_router_group[0],
                          jnp.transpose(w_router_expert[0], (1, 0, 2)).reshape(D_MODEL, N_EXPERTS),
                          jnp.zeros((D_MODEL, LANES - N_GROUPS - N_EXPERTS), F32)], axis=1)
    wr_hi = wr.astype(BF16)
    wr = jnp.concatenate([wr_hi, wr_hi, (wr - wr_hi.astype(F32)).astype(BF16)], axis=0)
    br = jnp.concatenate([b_router_group[0], b_router_expert[0].reshape(N_EXPERTS),
                          jnp.zeros((LANES - N_GROUPS - N_EXPERTS,), F32)]).reshape(1, LANES)
    sinks2 = attn_sinks[0] * LOG2E
    sinks_gk = jnp.transpose(sinks2.reshape(KV_B, G_B), (1, 0)).reshape(H_B)
    sink_rows_p = jnp.repeat(sinks_gk, QB).reshape(G_B, 1, 2 * QB)
    sink_rows_s = jnp.repeat(sinks2, ts).reshape(H_B * ts, 1)
    emat = jnp.tile(jnp.arange(LANES)[:, None] == (jnp.arange(A_WIDTH)[None, :] // HEAD_DIM),
                    (3, 1)).astype(BF16)
    attn_g = attn_norm[0].reshape(1, D_MODEL)
    ffn_g = ffn_norm[0].reshape(1, D_MODEL)

    xp = x_prompt.reshape(s, D_MODEL)
    aperm, qb_p, kvb_p, akv32, bkv32 = _proj_prompt(xp, attn_g, wp, cscale)
    a4 = _attn_a_prompt(aperm, _bias_a_prompt(table_a), emat)
    ob_p = _attn_b_prompt(qb_p, kvb_p, _bias_b_prompt(table_b), sink_rows_p)

    xs = x_sample.reshape(ns * ts, D_MODEL)
    q_s, kv_s = _proj_sample(xs, attn_g, wp, cscale)
    akt = jnp.transpose(cache_a_k[0], (0, 2, 3, 1)).reshape(ns, A_WIDTH, WIN_A)
    avt = jnp.transpose(cache_a_v[0], (0, 2, 3, 1)).reshape(ns, A_WIDTH, WIN_A)
    bkt = jnp.transpose(cache_b_k[0], (0, 2, 3, 1)).reshape(ns, LANES, WIN_B)
    bvt = jnp.transpose(cache_b_v[0], (0, 2, 3, 1)).reshape(ns, LANES, WIN_B)

    x1, xn, route, cnt = _out_router(xp, a4, ob_p, xs, q_s.reshape(ns, ts, 1024), kv_s.reshape(ns, ts, 1280),
                                     akt, avt, bkt, bvt, _bias_a_sample(table_a, ts),
                                     _bias_b_sample(table_b, ts), sink_rows_s, wo_p, ffn_g, wr, br)
    dest2, blk_e, n_used, nvalid, next_e = _dispatch(route, cnt)
    xb = _sc_scatter_rows(xn, dest2, blk_e.shape[0] * MOE_ROWS)
    yb = _experts(blk_e, n_used, nvalid, next_e, xb, w_gate[0], w_up[0], w_down[0])
    y_p, y_s = _combine_norm(x1, _sc_gather_rows(yb, dest2.reshape(-1)), route, final_norm.reshape(1, D_MODEL), s)

    y_prompt = y_p.reshape(1, s, D_MODEL)
    y_sample = y_s.reshape(ns, ts, D_MODEL)
    keep_a, keep_b = min(WIN_A, s), min(WIN_B, s)
    pak = akv32[s - keep_a:, :512].reshape(1, 1, keep_a, H_A, HEAD_DIM)
    pav = akv32[s - keep_a:, 512:].reshape(1, 1, keep_a, H_A, HEAD_DIM)
    pbk = bkv32[s - keep_b:, :128].reshape(1, 1, keep_b, KV_B, HEAD_DIM)
    pbv = bkv32[s - keep_b:, 128:].reshape(1, 1, keep_b, KV_B, HEAD_DIM)
    sak = kv_s[:, 0:512].reshape(1, ns, ts, H_A, HEAD_DIM)
    sav = kv_s[:, 512:1024].reshape(1, ns, ts, H_A, HEAD_DIM)
    sbk = kv_s[:, 1024:1152].reshape(1, ns, ts, KV_B, HEAD_DIM)
    sbv = kv_s[:, 1152:1280].reshape(1, ns, ts, KV_B, HEAD_DIM)
    return (y_prompt, y_sample, pak, pav, pbk, pbv, sak, sav, sbk, sbv)
```

```python
import functools
import math

import jax
import jax.numpy as jnp
import numpy as np
from jax import lax
from jax.experimental import pallas as pl
from jax.experimental.pallas import tpu as pltpu
from jax.experimental.pallas import tpu_sc as plsc

D_MODEL = 1024
HEAD_DIM = 64
H_A = 8
H_B = 8
KV_B = 2
G_B = 4
DILATIONS = (1, 4, 16)
WINDOWS = (128, 512, 2048)
WIN_A = 2048
WIN_B = 128
NUM_BUCKETS = 32
MAX_DISTANCE = 2048
N_GROUPS = 4
EXPERTS_PER_GROUP = 8
N_EXPERTS = 32
TOP_K = 2
D_EXPERT = 512
EPS = 1e-5
SCALE = HEAD_DIM ** -0.5
PAST_LEN = 16384

LANES = 128
SPAN = 2048
QB = 128
NCHUNK = 9
A_WIDTH = H_A * HEAD_DIM
MOE_ROWS = 512
SC_CORES = 2
SC_SUBCORES = 16
SC_WORKERS = SC_CORES * SC_SUBCORES
SC_COMBINE_WINDOW = 32
SC_SCATTER_WINDOW = 32
NEG = -1e30
LOG2E = math.log2(math.e)
B_STEP = 512
VMEM_LIMIT = 56 * 1024 * 1024
CACHE_BUFFERS = 3
OUT_ROUTER_VMEM = 60 * 1024 * 1024

F32 = jnp.float32
BF16 = jnp.bfloat16


def _t5_bucket_np(dist):
    dist = np.asarray(dist, np.int64)
    max_exact = NUM_BUCKETS // 2
    d = np.maximum(dist, 1).astype(np.float32)
    ratio = np.log(d / np.float32(max_exact)) / np.float32(math.log(MAX_DISTANCE / max_exact))
    large = max_exact + (ratio * np.float32(NUM_BUCKETS - max_exact)).astype(np.int32)
    large = np.minimum(large, NUM_BUCKETS - 1)
    return np.where(dist < max_exact, dist, large).astype(np.int32)


def _cparams(sem, vmem=VMEM_LIMIT):
    return pltpu.CompilerParams(dimension_semantics=sem, vmem_limit_bytes=vmem)


def _proj_prompt_kernel(x_ref, g_ref, w_ref, cs_ref, aperm_ref, qb_ref, kvb_ref, akv_ref, bkv_ref,
                        h_scr, p_scr):
    n = pl.program_id(1)

    @pl.when(n == 0)
    def _():
        x = x_ref[...]
        ms = jnp.mean(x * x, axis=-1, keepdims=True)
        h_scr[...] = (x * lax.rsqrt(ms + EPS) * g_ref[...]).astype(BF16)

    p = jnp.dot(h_scr[...], w_ref[...], preferred_element_type=F32) * cs_ref[...]

    @pl.when(n < 6)
    def _():
        aperm_ref[0] = p.astype(BF16)
        p_scr[0, 0] = p[:, :LANES]
        p_scr[0, 1] = p[:, LANES:]
        quarter = SPAN // 4
        for r in range(4):
            lo = p_scr[0, 0, pl.ds(r, quarter, stride=4), :]
            hi = p_scr[0, 1, pl.ds(r, quarter, stride=4), :]
            p_scr[1, 0, r * quarter:(r + 1) * quarter, :] = lo
            p_scr[1, 1, r * quarter:(r + 1) * quarter, :] = hi
            aperm_ref[1, r * quarter:(r + 1) * quarter, :] = jnp.concatenate([lo, hi], axis=1).astype(BF16)
        for r16 in range(16):
            start = (r16 % 4) * quarter + r16 // 4
            t = jnp.concatenate([p_scr[1, 0, pl.ds(start, QB, stride=4), :],
                                 p_scr[1, 1, pl.ds(start, QB, stride=4), :]], axis=1)
            aperm_ref[2, r16 * QB:(r16 + 1) * QB, :] = t.astype(BF16)

    @pl.when(n < 4)
    def _():
        akv_ref[...] = p

    @pl.when(jnp.logical_or(n == 6, n == 7))
    def _():
        qb_ref[...] = p.astype(BF16)

    @pl.when(n == 8)
    def _():
        kvb_ref[...] = p.astype(BF16)
        bkv_ref[...] = p


def _proj_prompt(x, gamma, w, cscale):
    s = x.shape[0]
    nspan = s // SPAN
    return pl.pallas_call(
        _proj_prompt_kernel,
        grid=(nspan, NCHUNK),
        in_specs=[
            pl.BlockSpec((SPAN, D_MODEL), lambda b, n: (b, 0)),
            pl.BlockSpec((1, D_MODEL), lambda b, n: (0, 0)),
            pl.BlockSpec((D_MODEL, 256), lambda b, n: (0, n)),
            pl.BlockSpec((1, 256), lambda b, n: (0, n)),
        ],
        out_specs=[
            pl.BlockSpec((3, SPAN, 256), lambda b, n: (0, b, jnp.minimum(n, 5))),
            pl.BlockSpec((SPAN, 256), lambda b, n: (b, jnp.clip(n - 6, 0, 1))),
            pl.BlockSpec((SPAN, 256), lambda b, n: (b, 0)),
            pl.BlockSpec((SPAN, 256), lambda b, n: (b, jnp.minimum(n, 3))),
            pl.BlockSpec((SPAN, 256), lambda b, n: (b, 0)),
        ],
        out_shape=[
            jax.ShapeDtypeStruct((3, s, 3 * A_WIDTH), BF16),
            jax.ShapeDtypeStruct((s, 512), BF16),
            jax.ShapeDtypeStruct((s, 256), BF16),
            jax.ShapeDtypeStruct((s, 1024), F32),
            jax.ShapeDtypeStruct((s, 256), F32),
        ],
        scratch_shapes=[pltpu.VMEM((SPAN, D_MODEL), BF16), pltpu.VMEM((2, 2, SPAN, LANES), F32)],
        compiler_params=_cparams(("arbitrary", "arbitrary")),
        name="proj_prompt",
    )(x, gamma, w, cscale)


def _proj_sample_kernel(x_ref, g_ref, w_ref, cs_ref, q_ref, kv_ref):
    x = x_ref[...]
    ms = jnp.mean(x * x, axis=-1, keepdims=True)
    h = (x * lax.rsqrt(ms + EPS) * g_ref[...]).astype(BF16)
    p = jnp.dot(h, w_ref[...], preferred_element_type=F32) * cs_ref[...]
    kv_ref[:, :1024] = p[:, :1024]
    kv_ref[:, 1024:] = p[:, 2048:]
    q_ref[...] = p[:, 1024:2048]


def _proj_sample(x, gamma, w, cscale):
    t = x.shape[0]
    tm = 512
    return pl.pallas_call(
        _proj_sample_kernel,
        grid=(t // tm,),
        in_specs=[
            pl.BlockSpec((tm, D_MODEL), lambda i: (i, 0)),
            pl.BlockSpec((1, D_MODEL), lambda i: (0, 0)),
            pl.BlockSpec((D_MODEL, 2304), lambda i: (0, 0)),
            pl.BlockSpec((1, 2304), lambda i: (0, 0)),
        ],
        out_specs=[
            pl.BlockSpec((tm, 1024), lambda i: (i, 0)),
            pl.BlockSpec((tm, 1280), lambda i: (i, 0)),
        ],
        out_shape=[
            jax.ShapeDtypeStruct((t, 1024), F32),
            jax.ShapeDtypeStruct((t, 1280), F32),
        ],
        compiler_params=_cparams(("arbitrary",)),
        name="proj_sample",
    )(x, gamma, w, cscale)


def _spread_heads(w, e3_ref):
    hi = w.astype(BF16)
    r1 = w - hi.astype(F32)
    mid = r1.astype(BF16)
    low = (r1 - mid.astype(F32)).astype(BF16)
    return jnp.dot(jnp.concatenate([hi, mid, low], axis=1), e3_ref[...], preferred_element_type=F32)


def _pair_tile(q2, kk, vv, bias_t, lo, sink=None):
    zero = jnp.zeros_like(q2)
    qq = jnp.concatenate([jnp.where(lo, q2, zero), jnp.where(lo, zero, q2)], axis=0)
    st = lax.dot_general(kk, qq, (((1,), (1,)), ((), ())), preferred_element_type=F32)
    st = st + bias_t
    m = jnp.max(st, axis=0, keepdims=True)
    if sink is not None:
        m = jnp.maximum(m, sink)
    p = jnp.exp2(st - m)
    den = jnp.sum(p, axis=0, keepdims=True)
    if sink is not None:
        den = den + jnp.exp2(sink - m)
    pn = (p * (1.0 / den)).astype(BF16)
    o = lax.dot_general(pn, vv, (((0,), (0,)), ((), ())), preferred_element_type=F32)
    return jnp.where(lo, o[:QB], o[QB:]), m + jnp.log2(den)


def _fill_band_tiles(h_ref, bias_scr):
    nk = 2 * QB
    prev = lax.broadcasted_iota(jnp.int32, (nk, nk), 0) < QB
    for pair in range(h_ref.shape[0] // 2):
        halves = []
        for hh in range(2):
            row = h_ref[2 * pair + hh:2 * pair + hh + 1, :]
            band = pltpu.roll(jnp.broadcast_to(row, (nk, nk)), 0, 1, stride=1, stride_axis=0)
            halves.append(band[:, :QB])
        tile = jnp.concatenate(halves, axis=1)
        bias_scr[0, pair] = tile
        bias_scr[1, pair] = jnp.where(prev, NEG, tile)


def _attn_a_kernel(q_ref, kvc_ref, kvp_ref, h_ref, e_ref, out_ref, o_scr, st_scr, bias_scr):
    b = pl.program_id(0)
    g = pl.program_id(1)
    nblk = jnp.where(g == 0, 16, jnp.where(g == 1, 4, 1))
    lane = lax.broadcasted_iota(jnp.int32, (QB, LANES), 1)
    lo = lane < HEAD_DIM

    @pl.when(b == 0)
    def _():
        _fill_band_tiles(h_ref, bias_scr.at[g])

    bias_ref = bias_scr.at[g]

    for cb in range(SPAN // QB):
        first = lax.rem(jnp.int32(cb), nblk) == 0
        rows = slice(cb * QB, (cb + 1) * QB)
        prow_c = max(cb - 1, 0) * QB
        prow_p = pl.multiple_of(jnp.where(first, cb + nblk - 1, 0) * QB, QB)
        variant = jnp.logical_and(first, b == 0).astype(jnp.int32)
        stats = []
        for hp in range(4):
            ks = slice(hp * LANES, (hp + 1) * LANES)
            vs = slice(A_WIDTH + hp * LANES, A_WIDTH + (hp + 1) * LANES)
            kp = jnp.where(first, kvp_ref[pl.ds(prow_p, QB), ks], kvc_ref[prow_c:prow_c + QB, ks])
            vp = jnp.where(first, kvp_ref[pl.ds(prow_p, QB), vs], kvc_ref[prow_c:prow_c + QB, vs])
            kk = jnp.concatenate([kp, kvc_ref[rows, ks]], axis=0)
            vv = jnp.concatenate([vp, kvc_ref[rows, vs]], axis=0)
            o, lse = _pair_tile(q_ref[rows, ks], kk, vv, bias_ref[variant, hp], lo)
            o_scr[g, hp, rows, :] = o
            stats += [lse[:, :QB], lse[:, QB:]]
        sm = jnp.concatenate(stats + [jnp.zeros((LANES - H_A, QB), F32)], axis=0)
        st_scr[g, rows, :] = sm.T

    @pl.when(g == 2)
    def _():
        def merge(c, carry):
            r2 = lax.rem(c, 4) * (SPAN // 4) + c // 4
            r3 = pl.multiple_of(c * QB, QB)
            l1 = st_scr[0, pl.ds(c, QB, stride=16), :]
            l2 = st_scr[1, pl.ds(r2, QB, stride=4), :]
            l3 = st_scr[2, pl.ds(r3, QB), :]
            mx = jnp.maximum(jnp.maximum(l1, l2), l3)
            w1 = jnp.exp2(l1 - mx)
            w2 = jnp.exp2(l2 - mx)
            w3 = jnp.exp2(l3 - mx)
            tot = w1 + w2 + w3
            a1 = _spread_heads(w1 / tot, e_ref)
            a2 = _spread_heads(w2 / tot, e_ref)
            a3 = _spread_heads(w3 / tot, e_ref)
            for hp in range(4):
                sl = slice(hp * LANES, (hp + 1) * LANES)
                o1 = o_scr[0, hp, pl.ds(c, QB, stride=16), :]
                o2 = o_scr[1, hp, pl.ds(r2, QB, stride=4), :]
                o3 = o_scr[2, hp, pl.ds(r3, QB), :]
                out_ref[hp, pl.ds(c, QB, stride=16), :] = a1[:, sl] * o1 + a2[:, sl] * o2 + a3[:, sl] * o3
            return carry

        lax.fori_loop(0, 16, merge, 0, unroll=4)


def _attn_a_prompt(aperm, bias_a, emat):
    s = aperm.shape[1]
    nspan = s // SPAN
    return pl.pallas_call(
        _attn_a_kernel,
        grid=(nspan, 3),
        in_specs=[
            pl.BlockSpec((None, SPAN, A_WIDTH), lambda b, g: (g, b, 2)),
            pl.BlockSpec((None, SPAN, 2 * A_WIDTH), lambda b, g: (g, b, 0)),
            pl.BlockSpec((None, SPAN, 2 * A_WIDTH), lambda b, g: (g, jnp.maximum(b - 1, 0), 0)),
            pl.BlockSpec((None, H_A, 2 * QB), lambda b, g: (g, 0, 0)),
            pl.BlockSpec((3 * LANES, A_WIDTH), lambda b, g: (0, 0)),
        ],
        out_specs=pl.BlockSpec((4, SPAN, LANES), lambda b, g: (0, b, 0)),
        out_shape=jax.ShapeDtypeStruct((4, s, LANES), F32),
        scratch_shapes=[pltpu.VMEM((3, 4, SPAN, LANES), F32), pltpu.VMEM((3, SPAN, LANES), F32),
                        pltpu.VMEM((3, 2, 4, 2 * QB, 2 * QB), F32)],
        compiler_params=_cparams(("arbitrary", "arbitrary")),
        name="attn_a_prompt",
    )(aperm, aperm, aperm, bias_a, emat)


def _attn_b_kernel(q_ref, kvc_ref, kvp_ref, h_ref, sink_ref, out_ref, bias_ref):
    i = pl.program_id(0)
    lane = lax.broadcasted_iota(jnp.int32, (QB, LANES), 1)
    lo = lane < HEAD_DIM

    @pl.when(i == 0)
    def _():
        _fill_band_tiles(h_ref, bias_ref)

    variant = (i == 0).astype(jnp.int32)
    for j in range(B_STEP // QB):
        rows = slice(j * QB, (j + 1) * QB)
        if j == 0:
            kp, vp = kvp_ref[:, :LANES], kvp_ref[:, LANES:]
        else:
            kp, vp = kvc_ref[(j - 1) * QB:j * QB, :LANES], kvc_ref[(j - 1) * QB:j * QB, LANES:]
        kk = jnp.concatenate([kp, kvc_ref[rows, :LANES]], axis=0)
        vv = jnp.concatenate([vp, kvc_ref[rows, LANES:]], axis=0)
        for g in range(G_B):
            bias_t = bias_ref[variant, g] if j == 0 else bias_ref[0, g]
            o, _ = _pair_tile(q_ref[rows, g * LANES:(g + 1) * LANES], kk, vv, bias_t, lo, sink=sink_ref[g])
            out_ref[rows, g * LANES:(g + 1) * LANES] = o.astype(BF16)


def _attn_b_prompt(qb, kvb, bias_b, sink_rows):
    s = qb.shape[0]
    per = B_STEP // QB
    return pl.pallas_call(
        _attn_b_kernel,
        grid=(s // B_STEP,),
        in_specs=[
            pl.BlockSpec((B_STEP, 512), lambda i: (i, 0)),
            pl.BlockSpec((B_STEP, 256), lambda i: (i, 0)),
            pl.BlockSpec((QB, 256), lambda i: (jnp.maximum(i * per - 1, 0), 0)),
            pl.BlockSpec((H_B, 2 * QB), lambda i: (0, 0)),
            pl.BlockSpec((G_B, 1, 2 * QB), lambda i: (0, 0, 0)),
        ],
        out_specs=pl.BlockSpec((B_STEP, 512), lambda i: (i, 0)),
        out_shape=jax.ShapeDtypeStruct((s, 512), BF16),
        scratch_shapes=[pltpu.VMEM((2, G_B, 2 * QB, 2 * QB), F32)],
        compiler_params=_cparams(("arbitrary",)),
        name="attn_b_prompt",
    )(qb, kvb, kvb, bias_b, sink_rows)


def _sample_attention(q, kvn, akt, avt, bkt, bvt, cba, cbb, sink):
    t = q.shape[0]
    kvn_p = jnp.concatenate([kvn, jnp.zeros((LANES - t, kvn.shape[1]), F32)], axis=0).astype(BF16)
    lane_a = lax.broadcasted_iota(jnp.int32, (t, A_WIDTH), 1) // HEAD_DIM

    qa = q[:, :A_WIDTH]
    qbd = jnp.concatenate([jnp.where(lane_a == h, qa, 0.0) for h in range(H_A)], axis=0).astype(BF16)
    s_c = jnp.dot(qbd, akt.astype(BF16), preferred_element_type=F32)
    s_n = lax.dot_general(qbd, kvn_p[:, :A_WIDTH], (((1,), (1,)), ((), ())), preferred_element_type=F32)
    s = jnp.concatenate([s_c, s_n], axis=1) + cba
    m = jnp.max(s, axis=-1, keepdims=True)
    p = jnp.exp2(s - m)
    l = jnp.sum(p, axis=-1, keepdims=True)
    pb = p.astype(BF16)
    o_n = jnp.dot(pb[:, WIN_A:], kvn_p[:, A_WIDTH:2 * A_WIDTH], preferred_element_type=F32)
    o_all = lax.dot_general(pb[:, :WIN_A], avt.astype(BF16), (((1,), (1,)), ((), ())),
                            preferred_element_type=F32) + o_n
    o_sel = jnp.zeros((t, A_WIDTH), F32)
    l_b = jnp.ones((t, A_WIDTH), F32)
    for h in range(H_A):
        sel = lane_a == h
        o_sel = jnp.where(sel, o_all[h * t:(h + 1) * t], o_sel)
        l_b = jnp.where(sel, l[h * t:(h + 1) * t], l_b)
    oa = o_sel / l_b

    lane_b = lax.broadcasted_iota(jnp.int32, (G_B * t, LANES), 1)
    lo = lane_b < HEAD_DIM
    qb2 = jnp.concatenate([q[:, A_WIDTH + g * LANES:A_WIDTH + (g + 1) * LANES] for g in range(G_B)], axis=0)
    qm = jnp.concatenate([jnp.where(lo, qb2, 0.0), jnp.where(lo, 0.0, qb2)], axis=0).astype(BF16)
    kb_n = kvn_p[:, 2 * A_WIDTH:2 * A_WIDTH + LANES]
    vb_n = kvn_p[:, 2 * A_WIDTH + LANES:]
    sb_c = jnp.dot(qm, bkt.astype(BF16), preferred_element_type=F32)
    sb_n = lax.dot_general(qm, kb_n, (((1,), (1,)), ((), ())), preferred_element_type=F32)
    sb = jnp.concatenate([sb_c, sb_n], axis=1) + cbb
    mb = jnp.maximum(jnp.max(sb, axis=-1, keepdims=True), sink)
    pbb = jnp.exp2(sb - mb)
    den = jnp.sum(pbb, axis=-1, keepdims=True) + jnp.exp2(sink - mb)
    pbb = pbb.astype(BF16)
    ob = lax.dot_general(pbb[:, :WIN_B], bvt.astype(BF16), (((1,), (1,)), ((), ())),
                         preferred_element_type=F32)
    ob = (ob + jnp.dot(pbb[:, WIN_B:], vb_n, preferred_element_type=F32)) / den
    half = G_B * t
    lo8 = lo[:t]
    ob = jnp.concatenate([jnp.where(lo8, ob[g * t:(g + 1) * t], ob[half + g * t:half + (g + 1) * t])
                          for g in range(G_B)], axis=1)
    return oa, ob


def _route(logits):
    lane = lax.broadcasted_iota(jnp.int32, logits.shape, 1).astype(F32)
    big = jnp.float32(1 << 20)
    ninf = jnp.float32(-jnp.inf)
    gmask = lane < N_GROUPS
    lg = jnp.where(gmask, logits, ninf)
    gmax = jnp.max(lg, axis=-1, keepdims=True)
    grp = jnp.min(jnp.where(lg == gmax, lane, big), axis=-1, keepdims=True)
    pg_top = 1.0 / jnp.sum(jnp.exp(lg - gmax), axis=-1, keepdims=True)
    e0 = N_GROUPS + grp * EXPERTS_PER_GROUP
    emask = jnp.logical_and(lane >= e0, lane < e0 + EXPERTS_PER_GROUP)
    le = jnp.where(emask, logits, ninf)
    emax = jnp.max(le, axis=-1, keepdims=True)
    esum = jnp.sum(jnp.exp(le - emax), axis=-1, keepdims=True)
    i1 = jnp.min(jnp.where(le == emax, lane, big), axis=-1, keepdims=True)
    le2 = jnp.where(lane == i1, ninf, le)
    e2max = jnp.max(le2, axis=-1, keepdims=True)
    i2 = jnp.min(jnp.where(le2 == e2max, lane, big), axis=-1, keepdims=True)
    p1 = 1.0 / esum
    p2 = jnp.exp(e2max - emax) / esum
    g1 = pg_top * p1 / (p1 + p2)
    g2 = pg_top * p2 / (p1 + p2)
    out = jnp.where(lane == 0, i1 - N_GROUPS, 0.0)
    out = jnp.where(lane == 1, i2 - N_GROUPS, out)
    out = jnp.where(lane == 2, g1, out)
    out = jnp.where(lane == 3, g2, out)
    return out


def _pack_bf16_pairs(x):
    half = x.shape[1] // 2

    def rne(v):
        bits = lax.bitcast_convert_type(v, jnp.int32)
        return bits + 0x7FFF + (lax.shift_right_logical(bits, 16) & 1)

    lo = lax.shift_right_logical(rne(x[:, :half]), 16)
    hi = rne(x[:, half:]) & jnp.int32(-65536)
    return lo | hi


def _unpack_bf16_pairs(w):
    lo = lax.bitcast_convert_type(lax.shift_left(w, 16), F32)
    hi = lax.bitcast_convert_type(w & jnp.int32(-65536), F32)
    return jnp.concatenate([lo, hi], axis=1)


def _out_router_kernel(xp_ref, ap_ref, bp_ref, xs_ref, q_ref, kvn_ref, akt_hbm, avt_hbm, bkt_ref, bvt_ref,
                       cba_ref, cbb_ref, sink_ref, wo_ref, g_ref, wr_ref, br_ref,
                       x1_ref, xn_ref, route_ref, cnt_ref,
                       xcat_scr, mix_scr, kbuf, vbuf, sem, *, prompt_tiles, decode_tiles, seqs_per_step):
    i = pl.program_id(0)
    seqs = prompt_tiles * seqs_per_step

    def cache_copies(n, slot):
        return (pltpu.make_async_copy(akt_hbm.at[n], kbuf.at[slot], sem.at[0, slot]),
                pltpu.make_async_copy(avt_hbm.at[n], vbuf.at[slot], sem.at[1, slot]))

    @pl.when(i == 0)
    def _():
        cnt_ref[...] = jnp.zeros_like(cnt_ref)
        xcat_scr[...] = jnp.zeros_like(xcat_scr)
        for n0 in range(2):
            for c in cache_copies(n0, n0):
                c.start()

    @pl.when(i == prompt_tiles)
    def _():
        for n1 in (seqs, seqs + 1):
            for c in cache_copies(seqs - 1, n1 % CACHE_BUFFERS):
                c.wait()

    pslot = lax.rem(i, 2)

    def route_previous():
        logits = jnp.dot(xcat_scr[1 - pslot], wr_ref[...], preferred_element_type=F32)
        route = _route(logits + br_ref[...])
        route_ref[...] = route
        lanef = lax.broadcasted_iota(jnp.int32, route.shape, 1).astype(F32)
        hits = (lanef == route[:, 0:1]).astype(F32) + (lanef == route[:, 1:2]).astype(F32)
        cnt_ref[...] += jnp.sum(hits, axis=0, keepdims=True) * (i > 0).astype(F32)

    def project(x_ref, mix):
        x1 = x_ref[...] + jnp.dot(mix, wo_ref[...], preferred_element_type=F32)
        x1_ref[...] = x1
        ms = jnp.mean(x1 * x1, axis=-1, keepdims=True)
        xn = x1 * lax.rsqrt(ms + EPS) * g_ref[...]
        xn_ref[...] = _pack_bf16_pairs(xn)
        xh = xn.astype(BF16)
        xl = (xn - xh.astype(F32)).astype(BF16)
        xcat_scr[pslot] = jnp.concatenate([xh, xl, xh], axis=1)

    @pl.when(i < prompt_tiles)
    def _():
        route_previous()
        mix = jnp.concatenate([ap_ref[0], ap_ref[1], ap_ref[2], ap_ref[3]], axis=1).astype(BF16)
        project(xp_ref, jnp.concatenate([mix, bp_ref[...]], axis=1))
        t = q_ref.shape[1]
        for s in range(seqs_per_step):
            n = i * seqs_per_step + s
            slot = lax.rem(n, CACHE_BUFFERS)
            for c in cache_copies(n, slot):
                c.wait()
            for c in cache_copies(jnp.minimum(n + 2, seqs - 1), lax.rem(n + 2, CACHE_BUFFERS)):
                c.start()
            oa, ob = _sample_attention(q_ref[s], kvn_ref[s], kbuf[slot], vbuf[slot], bkt_ref[s], bvt_ref[s],
                                       cba_ref[...], cbb_ref[...], sink_ref[...])
            row = pl.multiple_of(n * t, t)
            mix_scr[pl.ds(row, t), :A_WIDTH] = oa
            mix_scr[pl.ds(row, t), A_WIDTH:] = ob

    @pl.when(i >= prompt_tiles)
    def _():
        route_previous()
        tm = xs_ref.shape[0]
        row = pl.multiple_of(jnp.clip(i - prompt_tiles, 0, decode_tiles - 1) * tm, tm)
        project(xs_ref, mix_scr[pl.ds(row, tm), :].astype(BF16))


def _out_router(xp, a4p, bp, xs, q3, kvn3, akt, avt, bkt, bvt, cbias_a, cbias_b, sink_rows, wo, gamma, wr, br):
    tp, tsm = xp.shape[0], xs.shape[0]
    ns, ts = q3.shape[0], q3.shape[1]
    tm = 512
    npt, nst = tp // tm, tsm // tm
    nt = npt + nst
    t = tp + tsm
    sps = ns // npt
    assert sps * npt == ns and ns * ts == tsm and ns >= CACHE_BUFFERS
    pmap = lambda i: (jnp.minimum(i, npt - 1), 0)
    pmap3 = lambda i: (jnp.minimum(i, npt - 1), 0, 0)
    smap = lambda i: (jnp.clip(i - npt, 0, nst - 1), 0)
    cur = lambda i: (jnp.minimum(i, nt - 1), 0)
    const = lambda i: (0, 0)
    return pl.pallas_call(
        functools.partial(_out_router_kernel, prompt_tiles=npt, decode_tiles=nst, seqs_per_step=sps),
        grid=(nt + 1,),
        in_specs=[
            pl.BlockSpec((tm, D_MODEL), pmap),
            pl.BlockSpec((4, tm, LANES), lambda i: (0, jnp.minimum(i, npt - 1), 0)),
            pl.BlockSpec((tm, 512), pmap),
            pl.BlockSpec((tm, D_MODEL), smap),
            pl.BlockSpec((sps, ts, 1024), pmap3),
            pl.BlockSpec((sps, ts, 1280), pmap3),
            pl.BlockSpec(memory_space=pl.ANY),
            pl.BlockSpec(memory_space=pl.ANY),
            pl.BlockSpec((sps, LANES, WIN_B), pmap3),
            pl.BlockSpec((sps, LANES, WIN_B), pmap3),
            pl.BlockSpec((H_A * ts, WIN_A + LANES), const),
            pl.BlockSpec((H_B * ts, WIN_B + LANES), const),
            pl.BlockSpec((H_B * ts, 1), const),
            pl.BlockSpec((D_MODEL, D_MODEL), const),
            pl.BlockSpec((1, D_MODEL), const),
            pl.BlockSpec((3 * D_MODEL, LANES), const),
            pl.BlockSpec((1, LANES), const),
        ],
        out_specs=[
            pl.BlockSpec((tm, D_MODEL), cur),
            pl.BlockSpec((tm, D_MODEL // 2), cur),
            pl.BlockSpec((tm, LANES), lambda i: (jnp.maximum(i - 1, 0), 0)),
            pl.BlockSpec((1, LANES), const),
        ],
        scratch_shapes=[pltpu.VMEM((2, tm, 3 * D_MODEL), BF16), pltpu.VMEM((tsm, D_MODEL), F32),
                        pltpu.VMEM((CACHE_BUFFERS, A_WIDTH, WIN_A), F32),
                        pltpu.VMEM((CACHE_BUFFERS, A_WIDTH, WIN_A), F32),
                        pltpu.SemaphoreType.DMA((2, CACHE_BUFFERS))],
        out_shape=[
            jax.ShapeDtypeStruct((t, D_MODEL), F32),
            jax.ShapeDtypeStruct((t, D_MODEL // 2), jnp.int32),
            jax.ShapeDtypeStruct((t, LANES), F32),
            jax.ShapeDtypeStruct((1, LANES), F32),
        ],
        compiler_params=_cparams(("arbitrary",), vmem=OUT_ROUTER_VMEM),
        name="out_router",
    )(xp, a4p, bp, xs, q3, kvn3, akt, avt, bkt, bvt, cbias_a, cbias_b, sink_rows, wo, gamma, wr, br)


def _sc_gather_rows(table, idx, w):
    b = idx.shape[0]
    d = table.shape[1]
    per_worker = b // SC_WORKERS
    nwin = per_worker // w
    assert per_worker * SC_WORKERS == b and nwin * w == per_worker
    mesh = plsc.VectorSubcoreMesh(core_axis_name="c", subcore_axis_name="s")

    @functools.partial(
        pl.kernel, mesh=mesh,
        out_type=jax.ShapeDtypeStruct((b, d), table.dtype),
        scratch_types=[pltpu.VMEM((nwin, w), jnp.int32), pltpu.VMEM((2, w, d), table.dtype),
                       pltpu.SemaphoreType.DMA((2,)), pltpu.SemaphoreType.DMA((2,))],
        name="sc_gather_rows",
    )
    def gather(table_hbm, idx_hbm, out_hbm, idx_v, rows_v, sem_in, sem_out):
        wid = lax.axis_index("s") * SC_CORES + lax.axis_index("c")
        base = wid * per_worker
        pltpu.sync_copy(idx_hbm.at[wid], idx_v)

        def fetch(j):
            return pltpu.make_async_copy(table_hbm.at[idx_v.at[j]], rows_v.at[j % 2], sem_in.at[j % 2])

        def flush(j):
            return pltpu.make_async_copy(rows_v.at[j % 2], out_hbm.at[pl.ds(base + j * w, w)],
                                         sem_out.at[j % 2])

        fetch(0).start()
        for j in range(nwin):
            fetch(j).wait()
            if j + 1 < nwin:
                if j >= 1:
                    flush(j - 1).wait()
                fetch(j + 1).start()
            flush(j).start()
        for j in range(max(nwin - 2, 0), nwin):
            flush(j).wait()

    return gather(table, idx.reshape(SC_WORKERS, nwin, w))


def _sc_scatter_rows(x, dest2, nrows):
    t, d = x.shape
    w = SC_SCATTER_WINDOW
    per_worker = t // SC_WORKERS
    nwin = per_worker // w
    assert per_worker * SC_WORKERS == t and nwin * w == per_worker
    mesh = plsc.VectorSubcoreMesh(core_axis_name="c", subcore_axis_name="s")

    @functools.partial(
        pl.kernel, mesh=mesh,
        out_type=jax.ShapeDtypeStruct((nrows, d), x.dtype),
        scratch_types=[pltpu.VMEM((TOP_K, nwin, w), jnp.int32), pltpu.VMEM((2, w, d), x.dtype),
                       pltpu.SemaphoreType.DMA((2,)), pltpu.SemaphoreType.DMA((2,))],
        name="sc_scatter_rows",
    )
    def scatter(x_hbm, dest_hbm, out_hbm, idx_v, rows_v, sem_in, sem_out):
        wid = lax.axis_index("s") * SC_CORES + lax.axis_index("c")
        base = wid * per_worker
        for k in range(TOP_K):
            pltpu.sync_copy(dest_hbm.at[k, wid], idx_v.at[k])

        def fetch(j):
            return pltpu.make_async_copy(x_hbm.at[pl.ds(base + j * w, w)], rows_v.at[j % 2], sem_in.at[j % 2])

        def spread(j, k):
            return pltpu.make_async_copy(rows_v.at[j % 2], out_hbm.at[idx_v.at[k, j]], sem_out.at[j % 2])

        fetch(0).start()
        for j in range(nwin):
            fetch(j).wait()
            if j + 1 < nwin:
                if j >= 1:
                    for k in range(TOP_K):
                        spread(j - 1, k).wait()
                fetch(j + 1).start()
            for k in range(TOP_K):
                spread(j, k).start()
        for j in range(max(nwin - 2, 0), nwin):
            for k in range(TOP_K):
                spread(j, k).wait()

    return scatter(x, dest2.reshape(TOP_K, SC_WORKERS, nwin, w))


def _expert_kernel(be_ref, nu_ref, nv_ref, nx_ref, x_ref, wg_hbm, wu_hbm, wd_hbm, o_ref,
                   wg_s, wu_s, wd_s, wg_f, wu_f, wd_f, slot_s, sem):
    i = pl.program_id(0)
    used = i < nu_ref[0]
    changed = jnp.logical_or(i == 0, be_ref[i] != be_ref[jnp.maximum(i - 1, 0)])

    def weight_copies(e, slot):
        return (pltpu.make_async_copy(wg_hbm.at[e], wg_f.at[slot], sem.at[slot, 0]),
                pltpu.make_async_copy(wu_hbm.at[e], wu_f.at[slot], sem.at[slot, 1]),
                pltpu.make_async_copy(wd_hbm.at[e], wd_f.at[slot], sem.at[slot, 2]))

    @pl.when(i == 0)
    def _():
        slot_s[0] = 0
        for c in weight_copies(be_ref[0], 0):
            c.start()

    @pl.when(jnp.logical_and(used, changed))
    def _():
        slot = slot_s[0]
        for c in weight_copies(be_ref[i], slot):
            c.wait()
        wg_s[...] = wg_f[slot].astype(BF16)
        wu_s[...] = wu_f[slot].astype(BF16)
        wd_s[...] = wd_f[slot].astype(BF16)

        @pl.when(nx_ref[i] != be_ref[i])
        def _():
            for c in weight_copies(nx_ref[i], 1 - slot):
                c.start()

        slot_s[0] = 1 - slot

    @pl.when(used)
    def _():
        row = lax.broadcasted_iota(jnp.int32, x_ref.shape, 0)
        x = _unpack_bf16_pairs(jnp.where(row < nv_ref[i], x_ref[...], 0)).astype(BF16)
        gate = jnp.dot(x, wg_s[...], preferred_element_type=F32)
        up = jnp.dot(x, wu_s[...], preferred_element_type=F32)
        h = (gate * jax.nn.sigmoid(gate) * up).astype(BF16)
        o_ref[...] = _pack_bf16_pairs(jnp.dot(h, wd_s[...], preferred_element_type=F32))

    @pl.when(jnp.logical_not(used))
    def _():
        o_ref[...] = jnp.zeros_like(o_ref)


def _experts(blk_e, n_used, nvalid, next_e, xb, w_gate, w_up, w_down):
    rows = xb.shape[0]
    nblocks = rows // MOE_ROWS
    grid_spec = pltpu.PrefetchScalarGridSpec(
        num_scalar_prefetch=4,
        grid=(nblocks,),
        in_specs=[
            pl.BlockSpec((MOE_ROWS, D_MODEL // 2), lambda i, be, nu, nv, nx: (i, 0)),
            pl.BlockSpec(memory_space=pl.ANY),
            pl.BlockSpec(memory_space=pl.ANY),
            pl.BlockSpec(memory_space=pl.ANY),
        ],
        out_specs=pl.BlockSpec((MOE_ROWS, D_MODEL // 2), lambda i, be, nu, nv, nx: (i, 0)),
        scratch_shapes=[pltpu.VMEM((D_MODEL, D_EXPERT), BF16), pltpu.VMEM((D_MODEL, D_EXPERT), BF16),
                        pltpu.VMEM((D_EXPERT, D_MODEL), BF16),
                        pltpu.VMEM((2, D_MODEL, D_EXPERT), F32), pltpu.VMEM((2, D_MODEL, D_EXPERT), F32),
                        pltpu.VMEM((2, D_EXPERT, D_MODEL), F32),
                        pltpu.SMEM((1,), jnp.int32), pltpu.SemaphoreType.DMA((2, 3))],
    )
    return pl.pallas_call(
        _expert_kernel,
        grid_spec=grid_spec,
        out_shape=jax.ShapeDtypeStruct((rows, D_MODEL // 2), jnp.int32),
        compiler_params=_cparams(("arbitrary",)),
        name="experts",
    )(blk_e, n_used, nvalid, next_e, xb, w_gate, w_up, w_down)


def _combined_rows(x1_ref, y1_ref, y2_ref, route_ref, g_ref):
    r = route_ref[...]
    x = (x1_ref[...] + r[:, 2:3] * _unpack_bf16_pairs(y1_ref[...])
         + r[:, 3:4] * _unpack_bf16_pairs(y2_ref[...]))
    ms = jnp.mean(x * x, axis=-1, keepdims=True)
    return x * lax.rsqrt(ms + EPS) * g_ref[...]


def _combine_head_kernel(x1_ref, y1_ref, y2_ref, route_ref, g_ref, outp_ref):
    outp_ref[...] = _combined_rows(x1_ref, y1_ref, y2_ref, route_ref, g_ref)


def _combine_tail_kernel(x1_ref, y1_ref, y2_ref, route_ref, g_ref, prev_ref, outp_ref, outs_ref, *, prompt_left):
    del prev_ref
    y = _combined_rows(x1_ref, y1_ref, y2_ref, route_ref, g_ref)
    i = pl.program_id(0)

    @pl.when(i < prompt_left)
    def _():
        outp_ref[...] = y

    @pl.when(i >= prompt_left)
    def _():
        outs_ref[...] = y


def _combine_norm(x1, yb, dest2, route, gamma, tp):
    t = x1.shape[0]
    tm = 512
    nt, npt = t // tm, tp // tm
    head = nt // 2
    assert nt % 2 == 0 and head < npt
    hrows = head * tm
    gathered = [_sc_gather_rows(yb, dest2[:, k * hrows:(k + 1) * hrows].reshape(-1), SC_COMBINE_WINDOW)
                for k in range(2)]

    def in_specs(tile0):
        return [
            pl.BlockSpec((tm, D_MODEL), lambda i: (tile0 + i, 0)),
            pl.BlockSpec((tm, D_MODEL // 2), lambda i: (i, 0)),
            pl.BlockSpec((tm, D_MODEL // 2), lambda i: (i + head, 0)),
            pl.BlockSpec((tm, LANES), lambda i: (tile0 + i, 0)),
            pl.BlockSpec((1, D_MODEL), lambda i: (0, 0)),
        ]

    y_p = pl.pallas_call(
        _combine_head_kernel,
        grid=(head,),
        in_specs=in_specs(0),
        out_specs=pl.BlockSpec((tm, D_MODEL), lambda i: (i, 0)),
        out_shape=jax.ShapeDtypeStruct((tp, D_MODEL), F32),
        compiler_params=_cparams(("arbitrary",)),
        name="combine_head",
    )(x1, gathered[0], gathered[0], route, gamma)
    left = npt - head
    return pl.pallas_call(
        functools.partial(_combine_tail_kernel, prompt_left=left),
        grid=(nt - head,),
        in_specs=in_specs(head) + [pl.BlockSpec(memory_space=pl.ANY)],
        out_specs=[
            pl.BlockSpec((tm, D_MODEL), lambda i: (head + jnp.minimum(i, left - 1), 0)),
            pl.BlockSpec((tm, D_MODEL), lambda i: (jnp.maximum(i - left, 0), 0)),
        ],
        out_shape=[jax.ShapeDtypeStruct((tp, D_MODEL), F32), jax.ShapeDtypeStruct((t - tp, D_MODEL), F32)],
        input_output_aliases={5: 0},
        compiler_params=_cparams(("arbitrary",)),
        name="combine_tail",
    )(x1, gathered[1], gathered[1], route, gamma, y_p)


def _band_index():
    c = (2 * QB - np.arange(2 * QB)) % (2 * QB)
    return c, c <= QB


def _bias_a_prompt(table_a):
    c, valid = _band_index()
    idx = np.stack([_t5_bucket_np(d * np.clip(QB - c, 0, QB)) for d in DILATIONS])
    return jnp.where(valid, jnp.transpose(table_a[idx], (0, 2, 1)) * LOG2E, NEG)


def _bias_b_prompt(table_b):
    c, valid = _band_index()
    valid = valid & (c >= 1)
    h = jnp.where(valid, table_b[_t5_bucket_np(np.clip(QB - c, 0, QB))].T * LOG2E, NEG)
    return jnp.transpose(h.reshape(KV_B, G_B, 2 * QB), (1, 0, 2)).reshape(H_B, 2 * QB)


def _sample_bias(table, span, t, log2_weight):
    cols = span + LANES
    period = cols + LANES
    x = np.arange(period)
    dist = np.where(x >= period - t, span - x + period, span - x)
    extra = log2_weight(dist)
    valid = np.isfinite(extra)
    u = jnp.where(valid, table[_t5_bucket_np(np.maximum(dist, 0))].T * LOG2E
                  + np.where(valid, extra, 0.0).astype(np.float32), NEG)
    rows = jnp.tile(u, (1, t))[:, :t * (period - 1)].reshape(u.shape[0], t, period - 1)[:, :, :cols]
    return rows.reshape(u.shape[0] * t, cols)


def _bias_a_sample(table_a, t):
    def log2_count(dist):
        count = np.zeros(dist.shape, np.int64)
        for w, d in zip(WINDOWS, DILATIONS):
            count += (dist >= 0) & (dist % d == 0) & (dist <= w)
        return np.where(count > 0, np.log2(np.maximum(count, 1)), -np.inf)

    return _sample_bias(table_a, WIN_A, t, log2_count)


def _bias_b_sample(table_b, t):
    return _sample_bias(table_b, WIN_B, t,
                        lambda dist: np.where((dist >= 0) & (dist < WIN_B), 0.0, -np.inf))


def _dest_kernel(route_ref, cnt_ref, tri_ref, dest_ref, meta_ref, run_scr, pst_scr):
    i = pl.program_id(0)
    tm = route_ref.shape[0]
    r = route_ref[...]
    lane = lax.broadcasted_iota(jnp.int32, (tm, LANES), 1)
    lanef = lane.astype(F32)
    oh0 = lanef == r[:, 0:1]
    oh1 = lanef == r[:, 1:2]
    ohf = jnp.concatenate([oh0, oh1], axis=0).astype(F32)

    @pl.when(i == 0)
    def _():
        cnt = jnp.broadcast_to(cnt_ref[...], (LANES, LANES))
        padded = jnp.floor((cnt + (MOE_ROWS - 1)) * (1.0 / MOE_ROWS)) * MOE_ROWS
        lane_e = lax.broadcasted_iota(jnp.int32, (LANES, LANES), 1)
        x = padded
        for sh in (1, 2, 4, 8, 16, 32, 64):
            x = x + jnp.where(lane_e >= sh, pltpu.roll(x, sh, 1), 0.0)
        pst_scr[...] = (x - padded)[0:1]
        run_scr[...] = jnp.zeros_like(run_scr)
        wide = lambda v: jnp.concatenate([v.T, v.T], axis=1)
        cnt_t, bend_t = wide(cnt), wide(x * (1.0 / MOE_ROWS))
        bstart_t = wide((x - padded) * (1.0 / MOE_ROWS))
        blk = lax.broadcasted_iota(jnp.int32, (LANES, 2 * LANES), 1).astype(F32)
        exp = lax.broadcasted_iota(jnp.int32, (LANES, 2 * LANES), 0)
        real = exp < N_EXPERTS
        blk_e = jnp.minimum(jnp.sum(jnp.where(real & (bend_t <= blk), 1.0, 0.0), axis=0, keepdims=True),
                            N_EXPERTS - 1.0)
        mine = exp.astype(F32) == blk_e
        within = blk[0:1] - jnp.sum(jnp.where(mine, bstart_t, 0.0), axis=0, keepdims=True)
        nvalid = jnp.clip(jnp.sum(jnp.where(mine, cnt_t, 0.0), axis=0, keepdims=True) - within * MOE_ROWS,
                          0.0, float(MOE_ROWS))
        n_used = jnp.max(jnp.where(real, bend_t, 0.0), axis=0, keepdims=True)
        later = real & (exp.astype(F32) > blk_e) & (cnt_t > 0.0)
        nxt = jnp.min(jnp.where(later, exp.astype(F32), float(LANES)), axis=0, keepdims=True)
        nxt = jnp.where(nxt >= N_EXPERTS, blk_e, nxt)
        meta_ref[...] = jnp.concatenate([blk_e, nvalid, n_used, nxt, jnp.zeros((4, 2 * LANES), F32)],
                                        axis=0).astype(jnp.int32)

    base = run_scr[...] + pst_scr[...] - 1.0
    vals = []
    for c in range(2 * tm // LANES):
        ohc = ohf[c * LANES:(c + 1) * LANES]
        vals.append(jnp.dot(tri_ref[...], ohc.astype(BF16), preferred_element_type=F32) + base)
        base = base + jnp.sum(ohc, axis=0, keepdims=True)
    val = jnp.concatenate(vals, axis=0)
    d0 = jnp.sum(jnp.where(oh0, val[:tm], 0.0), axis=-1, keepdims=True)
    d1 = jnp.sum(jnp.where(oh1, val[tm:], 0.0), axis=-1, keepdims=True)
    tile = jnp.where(lane == 0, d0, jnp.where(lane == 1, d1, 0.0))
    dest_ref[...] = tile.T[:8].astype(jnp.int32)
    run_scr[...] += jnp.sum(ohf, axis=0, keepdims=True)


def _dispatch(route, cnt):
    t = route.shape[0]
    tm = 512
    tri = (jnp.arange(LANES)[:, None] >= jnp.arange(LANES)[None, :]).astype(BF16)
    nblocks = -(-t * TOP_K // MOE_ROWS) + N_EXPERTS
    assert nblocks <= 2 * LANES
    dest, meta = pl.pallas_call(
        _dest_kernel,
        grid=(t // tm,),
        in_specs=[pl.BlockSpec((tm, LANES), lambda i: (i, 0)),
                  pl.BlockSpec((1, LANES), lambda i: (0, 0)),
                  pl.BlockSpec((LANES, LANES), lambda i: (0, 0))],
        out_specs=[pl.BlockSpec((8, tm), lambda i: (0, i)),
                   pl.BlockSpec((8, 2 * LANES), lambda i: (0, 0))],
        out_shape=[jax.ShapeDtypeStruct((8, t), jnp.int32), jax.ShapeDtypeStruct((8, 2 * LANES), jnp.int32)],
        scratch_shapes=[pltpu.VMEM((1, LANES), F32), pltpu.VMEM((1, LANES), F32)],
        compiler_params=_cparams(("arbitrary",)),
        name="moe_dest",
    )(route, cnt, tri)
    return dest[:TOP_K], meta[0, :nblocks], meta[2, :1], meta[1, :nblocks], meta[3, :nblocks]


def kernel(x_prompt, x_sample, cache_a_k, cache_a_v, cache_b_k, cache_b_v, rel_bias_table, attn_norm, w_in,
           w_out, attn_sinks, ffn_norm, w_router_group, b_router_group, w_router_expert, b_router_expert,
           w_gate, w_up, w_down, final_norm):
    s = x_prompt.shape[1]
    ns, ts = x_sample.shape[0], x_sample.shape[1]
    table_a = rel_bias_table[:, :H_A]
    table_b = rel_bias_table[:, H_A:]

    w = w_in[0]
    wqa, wka, wva, wqb, wkb, wvb = (w[:, 0:512], w[:, 512:1024], w[:, 1024:1536], w[:, 1536:2048],
                                    w[:, 2048:2176], w[:, 2176:2304])
    wqb = jnp.transpose(wqb.reshape(D_MODEL, KV_B, G_B, HEAD_DIM), (0, 2, 1, 3)).reshape(D_MODEL, 512)
    wp = jnp.concatenate([wka, wva, wqa, wqb, wkb, wvb], axis=1).astype(BF16)
    cscale = jnp.concatenate([jnp.ones((1, 1024), F32), jnp.full((1, 1024), SCALE * LOG2E, F32),
                              jnp.ones((1, 256), F32)], axis=1)
    wo = w_out[0]
    wo_b = jnp.transpose(wo[512:].reshape(KV_B, G_B, HEAD_DIM, D_MODEL), (1, 0, 2, 3)).reshape(512, D_MODEL)
    wo_p = jnp.concatenate([wo[:512], wo_b], axis=0).astype(BF16)
    wr = jnp.concatenate([w_router_group[0],
                          jnp.transpose(w_router_expert[0], (1, 0, 2)).reshape(D_MODEL, N_EXPERTS),
                          jnp.zeros((D_MODEL, LANES - N_GROUPS - N_EXPERTS), F32)], axis=1)
    wr_hi = wr.astype(BF16)
    wr = jnp.concatenate([wr_hi, wr_hi, (wr - wr_hi.astype(F32)).astype(BF16)], axis=0)
    br = jnp.concatenate([b_router_group[0], b_router_expert[0].reshape(N_EXPERTS),
                          jnp.zeros((LANES - N_GROUPS - N_EXPERTS,), F32)]).reshape(1, LANES)
    sinks2 = attn_sinks[0] * LOG2E
    sinks_gk = jnp.transpose(sinks2.reshape(KV_B, G_B), (1, 0)).reshape(H_B)
    sink_rows_p = jnp.repeat(sinks_gk, QB).reshape(G_B, 1, 2 * QB)
    sink_rows_s = jnp.repeat(sinks2, ts).reshape(H_B * ts, 1)
    emat = jnp.tile(jnp.arange(LANES)[:, None] == (jnp.arange(A_WIDTH)[None, :] // HEAD_DIM),
                    (3, 1)).astype(BF16)
    attn_g = attn_norm[0].reshape(1, D_MODEL)
    ffn_g = ffn_norm[0].reshape(1, D_MODEL)

    xp = x_prompt.reshape(s, D_MODEL)
    aperm, qb_p, kvb_p, akv32, bkv32 = _proj_prompt(xp, attn_g, wp, cscale)
    a4 = _attn_a_prompt(aperm, _bias_a_prompt(table_a), emat)
    ob_p = _attn_b_prompt(qb_p, kvb_p, _bias_b_prompt(table_b), sink_rows_p)

    xs = x_sample.reshape(ns * ts, D_MODEL)
    q_s, kv_s = _proj_sample(xs, attn_g, wp, cscale)
    akt = jnp.transpose(cache_a_k[0], (0, 2, 3, 1)).reshape(ns, A_WIDTH, WIN_A)
    avt = jnp.transpose(cache_a_v[0], (0, 2, 3, 1)).reshape(ns, A_WIDTH, WIN_A)
    bkt = jnp.transpose(cache_b_k[0], (0, 2, 3, 1)).reshape(ns, LANES, WIN_B)
    bvt = jnp.transpose(cache_b_v[0], (0, 2, 3, 1)).reshape(ns, LANES, WIN_B)

    x1, xn, route, cnt = _out_router(xp, a4, ob_p, xs, q_s.reshape(ns, ts, 1024), kv_s.reshape(ns, ts, 1280),
                                     akt, avt, bkt, bvt, _bias_a_sample(table_a, ts),
                                     _bias_b_sample(table_b, ts), sink_rows_s, wo_p, ffn_g, wr, br)
    dest2, blk_e, n_used, nvalid, next_e = _dispatch(route, cnt)
    xb = _sc_scatter_rows(xn, dest2, blk_e.shape[0] * MOE_ROWS)
    yb = _experts(blk_e, n_used, nvalid, next_e, xb, w_gate[0], w_up[0], w_down[0])
    y_p, y_s = _combine_norm(x1, yb, dest2, route, final_norm.reshape(1, D_MODEL), s)

    y_prompt = y_p.reshape(1, s, D_MODEL)
    y_sample = y_s.reshape(ns, ts, D_MODEL)
    keep_a, keep_b = min(WIN_A, s), min(WIN_B, s)
    pak = akv32[s - keep_a:, :512].reshape(1, 1, keep_a, H_A, HEAD_DIM)
    pav = akv32[s - keep_a:, 512:].reshape(1, 1, keep_a, H_A, HEAD_DIM)
    pbk = bkv32[s - keep_b:, :128].reshape(1, 1, keep_b, KV_B, HEAD_DIM)
    pbv = bkv32[s - keep_b:, 128:].reshape(1, 1, keep_b, KV_B, HEAD_DIM)
    sak = kv_s[:, 0:512].reshape(1, ns, ts, H_A, HEAD_DIM)
    sav = kv_s[:, 512:1024].reshape(1, ns, ts, H_A, HEAD_DIM)
    sbk = kv_s[:, 1024:1152].reshape(1, ns, ts, KV_B, HEAD_DIM)
    sbv = kv_s[:, 1152:1280].reshape(1, ns, ts, KV_B, HEAD_DIM)
    return (y_prompt, y_sample, pak, pav, pbk, pbv, sak, sav, sbk, sbv)
```

```python
import functools
import math

import jax
import jax.numpy as jnp
import numpy as np
from jax import lax
from jax.experimental import pallas as pl
from jax.experimental.pallas import tpu as pltpu
from jax.experimental.pallas import tpu_sc as plsc

D_MODEL = 1024
HEAD_DIM = 64
H_A = 8
H_B = 8
KV_B = 2
G_B = 4
DILATIONS = (1, 4, 16)
WINDOWS = (128, 512, 2048)
WIN_A = 2048
WIN_B = 128
NUM_BUCKETS = 32
MAX_DISTANCE = 2048
N_GROUPS = 4
EXPERTS_PER_GROUP = 8
N_EXPERTS = 32
TOP_K = 2
D_EXPERT = 512
EPS = 1e-5
SCALE = HEAD_DIM ** -0.5
PAST_LEN = 16384

LANES = 128
SPAN = 2048
QB = 128
NCHUNK = 9
A_WIDTH = H_A * HEAD_DIM
MOE_ROWS = 512
SC_CORES = 2
SC_SUBCORES = 16
SC_WORKERS = SC_CORES * SC_SUBCORES
SC_WINDOW = 64
SC_SCATTER_WINDOW = 32
NEG = -1e30
LOG2E = math.log2(math.e)
B_STEP = 512
VMEM_LIMIT = 56 * 1024 * 1024
CACHE_BUFFERS = 3
OUT_ROUTER_VMEM = 60 * 1024 * 1024

F32 = jnp.float32
BF16 = jnp.bfloat16


def _t5_bucket_np(dist):
    dist = np.asarray(dist, np.int64)
    max_exact = NUM_BUCKETS // 2
    d = np.maximum(dist, 1).astype(np.float32)
    ratio = np.log(d / np.float32(max_exact)) / np.float32(math.log(MAX_DISTANCE / max_exact))
    large = max_exact + (ratio * np.float32(NUM_BUCKETS - max_exact)).astype(np.int32)
    large = np.minimum(large, NUM_BUCKETS - 1)
    return np.where(dist < max_exact, dist, large).astype(np.int32)


def _cparams(sem, vmem=VMEM_LIMIT):
    return pltpu.CompilerParams(dimension_semantics=sem, vmem_limit_bytes=vmem)


def _proj_prompt_kernel(x_ref, g_ref, w_ref, cs_ref, aperm_ref, qb_ref, kvb_ref, akv_ref, bkv_ref,
                        h_scr, p_scr):
    n = pl.program_id(1)

    @pl.when(n == 0)
    def _():
        x = x_ref[...]
        ms = jnp.mean(x * x, axis=-1, keepdims=True)
        h_scr[...] = (x * lax.rsqrt(ms + EPS) * g_ref[...]).astype(BF16)

    p = jnp.dot(h_scr[...], w_ref[...], preferred_element_type=F32) * cs_ref[...]

    @pl.when(n < 6)
    def _():
        aperm_ref[0] = p.astype(BF16)
        p_scr[0, 0] = p[:, :LANES]
        p_scr[0, 1] = p[:, LANES:]
        quarter = SPAN // 4
        for r in range(4):
            lo = p_scr[0, 0, pl.ds(r, quarter, stride=4), :]
            hi = p_scr[0, 1, pl.ds(r, quarter, stride=4), :]
            p_scr[1, 0, r * quarter:(r + 1) * quarter, :] = lo
            p_scr[1, 1, r * quarter:(r + 1) * quarter, :] = hi
            aperm_ref[1, r * quarter:(r + 1) * quarter, :] = jnp.concatenate([lo, hi], axis=1).astype(BF16)
        for r16 in range(16):
            start = (r16 % 4) * quarter + r16 // 4
            t = jnp.concatenate([p_scr[1, 0, pl.ds(start, QB, stride=4), :],
                                 p_scr[1, 1, pl.ds(start, QB, stride=4), :]], axis=1)
            aperm_ref[2, r16 * QB:(r16 + 1) * QB, :] = t.astype(BF16)

    @pl.when(n < 4)
    def _():
        akv_ref[...] = p

    @pl.when(jnp.logical_or(n == 6, n == 7))
    def _():
        qb_ref[...] = p.astype(BF16)

    @pl.when(n == 8)
    def _():
        kvb_ref[...] = p.astype(BF16)
        bkv_ref[...] = p


def _proj_prompt(x, gamma, w, cscale):
    s = x.shape[0]
    nspan = s // SPAN
    return pl.pallas_call(
        _proj_prompt_kernel,
        grid=(nspan, NCHUNK),
        in_specs=[
            pl.BlockSpec((SPAN, D_MODEL), lambda b, n: (b, 0)),
            pl.BlockSpec((1, D_MODEL), lambda b, n: (0, 0)),
            pl.BlockSpec((D_MODEL, 256), lambda b, n: (0, n)),
            pl.BlockSpec((1, 256), lambda b, n: (0, n)),
        ],
        out_specs=[
            pl.BlockSpec((3, SPAN, 256), lambda b, n: (0, b, jnp.minimum(n, 5))),
            pl.BlockSpec((SPAN, 256), lambda b, n: (b, jnp.clip(n - 6, 0, 1))),
            pl.BlockSpec((SPAN, 256), lambda b, n: (b, 0)),
            pl.BlockSpec((SPAN, 256), lambda b, n: (0, jnp.where(b == nspan - 1, jnp.minimum(n, 3), 0))),
            pl.BlockSpec((SPAN, 256), lambda b, n: (0, 0)),
        ],
        out_shape=[
            jax.ShapeDtypeStruct((3, s, 3 * A_WIDTH), BF16),
            jax.ShapeDtypeStruct((s, 512), BF16),
            jax.ShapeDtypeStruct((s, 256), BF16),
            jax.ShapeDtypeStruct((SPAN, 1024), F32),
            jax.ShapeDtypeStruct((SPAN, 256), F32),
        ],
        scratch_shapes=[pltpu.VMEM((SPAN, D_MODEL), BF16), pltpu.VMEM((2, 2, SPAN, LANES), F32)],
        compiler_params=_cparams(("arbitrary", "arbitrary")),
        name="proj_prompt",
    )(x, gamma, w, cscale)


def _proj_sample_kernel(x_ref, g_ref, w_ref, cs_ref, q_ref, kv_ref):
    x = x_ref[...]
    ms = jnp.mean(x * x, axis=-1, keepdims=True)
    h = (x * lax.rsqrt(ms + EPS) * g_ref[...]).astype(BF16)
    p = jnp.dot(h, w_ref[...], preferred_element_type=F32) * cs_ref[...]
    kv_ref[:, :1024] = p[:, :1024]
    kv_ref[:, 1024:] = p[:, 2048:]
    q_ref[...] = p[:, 1024:2048]


def _proj_sample(x, gamma, w, cscale):
    t = x.shape[0]
    tm = 512
    return pl.pallas_call(
        _proj_sample_kernel,
        grid=(t // tm,),
        in_specs=[
            pl.BlockSpec((tm, D_MODEL), lambda i: (i, 0)),
            pl.BlockSpec((1, D_MODEL), lambda i: (0, 0)),
            pl.BlockSpec((D_MODEL, 2304), lambda i: (0, 0)),
            pl.BlockSpec((1, 2304), lambda i: (0, 0)),
        ],
        out_specs=[
            pl.BlockSpec((tm, 1024), lambda i: (i, 0)),
            pl.BlockSpec((tm, 1280), lambda i: (i, 0)),
        ],
        out_shape=[
            jax.ShapeDtypeStruct((t, 1024), F32),
            jax.ShapeDtypeStruct((t, 1280), F32),
        ],
        compiler_params=_cparams(("arbitrary",)),
        name="proj_sample",
    )(x, gamma, w, cscale)


def _spread_heads(w, e3_ref):
    hi = w.astype(BF16)
    r1 = w - hi.astype(F32)
    mid = r1.astype(BF16)
    low = (r1 - mid.astype(F32)).astype(BF16)
    return jnp.dot(jnp.concatenate([hi, mid, low], axis=1), e3_ref[...], preferred_element_type=F32)


def _pair_tile(q2, kk, vv, bias_t, lo, sink=None):
    zero = jnp.zeros_like(q2)
    qq = jnp.concatenate([jnp.where(lo, q2, zero), jnp.where(lo, zero, q2)], axis=0)
    st = lax.dot_general(kk, qq, (((1,), (1,)), ((), ())), preferred_element_type=F32)
    st = st + bias_t
    m = jnp.max(st, axis=0, keepdims=True)
    if sink is not None:
        m = jnp.maximum(m, sink)
    p = jnp.exp2(st - m)
    den = jnp.sum(p, axis=0, keepdims=True)
    if sink is not None:
        den = den + jnp.exp2(sink - m)
    pn = (p * (1.0 / den)).astype(BF16)
    o = lax.dot_general(pn, vv, (((0,), (0,)), ((), ())), preferred_element_type=F32)
    return jnp.where(lo, o[:QB], o[QB:]), m + jnp.log2(den)


def _fill_band_tiles(h_ref, bias_scr):
    nk = 2 * QB
    prev = lax.broadcasted_iota(jnp.int32, (nk, nk), 0) < QB
    for pair in range(h_ref.shape[0] // 2):
        halves = []
        for hh in range(2):
            row = h_ref[2 * pair + hh:2 * pair + hh + 1, :]
            band = pltpu.roll(jnp.broadcast_to(row, (nk, nk)), 0, 1, stride=1, stride_axis=0)
            halves.append(band[:, :QB])
        tile = jnp.concatenate(halves, axis=1)
        bias_scr[0, pair] = tile
        bias_scr[1, pair] = jnp.where(prev, NEG, tile)


def _attn_a_kernel(q_ref, kvc_ref, kvp_ref, h_ref, e_ref, out_ref, o_scr, st_scr, bias_scr):
    b = pl.program_id(0)
    g = pl.program_id(1)
    nblk = jnp.where(g == 0, 16, jnp.where(g == 1, 4, 1))
    lane = lax.broadcasted_iota(jnp.int32, (QB, LANES), 1)
    lo = lane < HEAD_DIM

    @pl.when(b == 0)
    def _():
        _fill_band_tiles(h_ref, bias_scr.at[g])

    bias_ref = bias_scr.at[g]

    for cb in range(SPAN // QB):
        first = lax.rem(jnp.int32(cb), nblk) == 0
        rows = slice(cb * QB, (cb + 1) * QB)
        prow_c = max(cb - 1, 0) * QB
        prow_p = pl.multiple_of(jnp.where(first, cb + nblk - 1, 0) * QB, QB)
        variant = jnp.logical_and(first, b == 0).astype(jnp.int32)
        stats = []
        for hp in range(4):
            ks = slice(hp * LANES, (hp + 1) * LANES)
            vs = slice(A_WIDTH + hp * LANES, A_WIDTH + (hp + 1) * LANES)
            kp = jnp.where(first, kvp_ref[pl.ds(prow_p, QB), ks], kvc_ref[prow_c:prow_c + QB, ks])
            vp = jnp.where(first, kvp_ref[pl.ds(prow_p, QB), vs], kvc_ref[prow_c:prow_c + QB, vs])
            kk = jnp.concatenate([kp, kvc_ref[rows, ks]], axis=0)
            vv = jnp.concatenate([vp, kvc_ref[rows, vs]], axis=0)
            o, lse = _pair_tile(q_ref[rows, ks], kk, vv, bias_ref[variant, hp], lo)
            o_scr[g, hp, rows, :] = o
            stats += [lse[:, :QB], lse[:, QB:]]
        sm = jnp.concatenate(stats + [jnp.zeros((LANES - H_A, QB), F32)], axis=0)
        st_scr[g, rows, :] = sm.T

    @pl.when(g == 2)
    def _():
        def merge(c, carry):
            r2 = lax.rem(c, 4) * (SPAN // 4) + c // 4
            r3 = pl.multiple_of(c * QB, QB)
            l1 = st_scr[0, pl.ds(c, QB, stride=16), :]
            l2 = st_scr[1, pl.ds(r2, QB, stride=4), :]
            l3 = st_scr[2, pl.ds(r3, QB), :]
            mx = jnp.maximum(jnp.maximum(l1, l2), l3)
            w1 = jnp.exp2(l1 - mx)
            w2 = jnp.exp2(l2 - mx)
            w3 = jnp.exp2(l3 - mx)
            tot = w1 + w2 + w3
            a1 = _spread_heads(w1 / tot, e_ref)
            a2 = _spread_heads(w2 / tot, e_ref)
            a3 = _spread_heads(w3 / tot, e_ref)
            for hp in range(4):
                sl = slice(hp * LANES, (hp + 1) * LANES)
                o1 = o_scr[0, hp, pl.ds(c, QB, stride=16), :]
                o2 = o_scr[1, hp, pl.ds(r2, QB, stride=4), :]
                o3 = o_scr[2, hp, pl.ds(r3, QB), :]
                out_ref[hp, pl.ds(c, QB, stride=16), :] = a1[:, sl] * o1 + a2[:, sl] * o2 + a3[:, sl] * o3
            return carry

        lax.fori_loop(0, 16, merge, 0, unroll=4)


def _attn_a_prompt(aperm, bias_a, emat):
    s = aperm.shape[1]
    nspan = s // SPAN
    return pl.pallas_call(
        _attn_a_kernel,
        grid=(nspan, 3),
        in_specs=[
            pl.BlockSpec((None, SPAN, A_WIDTH), lambda b, g: (g, b, 2)),
            pl.BlockSpec((None, SPAN, 2 * A_WIDTH), lambda b, g: (g, b, 0)),
            pl.BlockSpec((None, SPAN, 2 * A_WIDTH), lambda b, g: (g, jnp.maximum(b - 1, 0), 0)),
            pl.BlockSpec((None, H_A, 2 * QB), lambda b, g: (g, 0, 0)),
            pl.BlockSpec((3 * LANES, A_WIDTH), lambda b, g: (0, 0)),
        ],
        out_specs=pl.BlockSpec((4, SPAN, LANES), lambda b, g: (0, b, 0)),
        out_shape=jax.ShapeDtypeStruct((4, s, LANES), F32),
        scratch_shapes=[pltpu.VMEM((3, 4, SPAN, LANES), F32), pltpu.VMEM((3, SPAN, LANES), F32),
                        pltpu.VMEM((3, 2, 4, 2 * QB, 2 * QB), F32)],
        compiler_params=_cparams(("arbitrary", "arbitrary")),
        name="attn_a_prompt",
    )(aperm, aperm, aperm, bias_a, emat)


def _attn_b_kernel(q_ref, kvc_ref, kvp_ref, h_ref, sink_ref, out_ref, bias_ref):
    i = pl.program_id(0)
    lane = lax.broadcasted_iota(jnp.int32, (QB, LANES), 1)
    lo = lane < HEAD_DIM

    @pl.when(i == 0)
    def _():
        _fill_band_tiles(h_ref, bias_ref)

    variant = (i == 0).astype(jnp.int32)
    for j in range(B_STEP // QB):
        rows = slice(j * QB, (j + 1) * QB)
        if j == 0:
            kp, vp = kvp_ref[:, :LANES], kvp_ref[:, LANES:]
        else:
            kp, vp = kvc_ref[(j - 1) * QB:j * QB, :LANES], kvc_ref[(j - 1) * QB:j * QB, LANES:]
        kk = jnp.concatenate([kp, kvc_ref[rows, :LANES]], axis=0)
        vv = jnp.concatenate([vp, kvc_ref[rows, LANES:]], axis=0)
        for g in range(G_B):
            bias_t = bias_ref[variant, g] if j == 0 else bias_ref[0, g]
            o, _ = _pair_tile(q_ref[rows, g * LANES:(g + 1) * LANES], kk, vv, bias_t, lo, sink=sink_ref[g])
            out_ref[rows, g * LANES:(g + 1) * LANES] = o.astype(BF16)


def _attn_b_prompt(qb, kvb, bias_b, sink_rows):
    s = qb.shape[0]
    per = B_STEP // QB
    return pl.pallas_call(
        _attn_b_kernel,
        grid=(s // B_STEP,),
        in_specs=[
            pl.BlockSpec((B_STEP, 512), lambda i: (i, 0)),
            pl.BlockSpec((B_STEP, 256), lambda i: (i, 0)),
            pl.BlockSpec((QB, 256), lambda i: (jnp.maximum(i * per - 1, 0), 0)),
            pl.BlockSpec((H_B, 2 * QB), lambda i: (0, 0)),
            pl.BlockSpec((G_B, 1, 2 * QB), lambda i: (0, 0, 0)),
        ],
        out_specs=pl.BlockSpec((B_STEP, 512), lambda i: (i, 0)),
        out_shape=jax.ShapeDtypeStruct((s, 512), BF16),
        scratch_shapes=[pltpu.VMEM((2, G_B, 2 * QB, 2 * QB), F32)],
        compiler_params=_cparams(("arbitrary",)),
        name="attn_b_prompt",
    )(qb, kvb, kvb, bias_b, sink_rows)


def _sample_attention(q, kvn, akt, avt, bkt, bvt, cba, cbb, sink):
    t = q.shape[0]
    kvn_p = jnp.concatenate([kvn, jnp.zeros((LANES - t, kvn.shape[1]), F32)], axis=0).astype(BF16)
    lane_a = lax.broadcasted_iota(jnp.int32, (t, A_WIDTH), 1) // HEAD_DIM

    qa = q[:, :A_WIDTH]
    qbd = jnp.concatenate([jnp.where(lane_a == h, qa, 0.0) for h in range(H_A)], axis=0).astype(BF16)
    s_c = jnp.dot(qbd, akt.astype(BF16), preferred_element_type=F32)
    s_n = lax.dot_general(qbd, kvn_p[:, :A_WIDTH], (((1,), (1,)), ((), ())), preferred_element_type=F32)
    s = jnp.concatenate([s_c, s_n], axis=1) + cba
    m = jnp.max(s, axis=-1, keepdims=True)
    p = jnp.exp2(s - m)
    l = jnp.sum(p, axis=-1, keepdims=True)
    pb = p.astype(BF16)
    o_n = jnp.dot(pb[:, WIN_A:], kvn_p[:, A_WIDTH:2 * A_WIDTH], preferred_element_type=F32)
    o_all = lax.dot_general(pb[:, :WIN_A], avt.astype(BF16), (((1,), (1,)), ((), ())),
                            preferred_element_type=F32) + o_n
    o_sel = jnp.zeros((t, A_WIDTH), F32)
    l_b = jnp.ones((t, A_WIDTH), F32)
    for h in range(H_A):
        sel = lane_a == h
        o_sel = jnp.where(sel, o_all[h * t:(h + 1) * t], o_sel)
        l_b = jnp.where(sel, l[h * t:(h + 1) * t], l_b)
    oa = o_sel / l_b

    lane_b = lax.broadcasted_iota(jnp.int32, (G_B * t, LANES), 1)
    lo = lane_b < HEAD_DIM
    qb2 = jnp.concatenate([q[:, A_WIDTH + g * LANES:A_WIDTH + (g + 1) * LANES] for g in range(G_B)], axis=0)
    qm = jnp.concatenate([jnp.where(lo, qb2, 0.0), jnp.where(lo, 0.0, qb2)], axis=0).astype(BF16)
    kb_n = kvn_p[:, 2 * A_WIDTH:2 * A_WIDTH + LANES]
    vb_n = kvn_p[:, 2 * A_WIDTH + LANES:]
    sb_c = jnp.dot(qm, bkt.astype(BF16), preferred_element_type=F32)
    sb_n = lax.dot_general(qm, kb_n, (((1,), (1,)), ((), ())), preferred_element_type=F32)
    sb = jnp.concatenate([sb_c, sb_n], axis=1) + cbb
    mb = jnp.maximum(jnp.max(sb, axis=-1, keepdims=True), sink)
    pbb = jnp.exp2(sb - mb)
    den = jnp.sum(pbb, axis=-1, keepdims=True) + jnp.exp2(sink - mb)
    pbb = pbb.astype(BF16)
    ob = lax.dot_general(pbb[:, :WIN_B], bvt.astype(BF16), (((1,), (1,)), ((), ())),
                         preferred_element_type=F32)
    ob = (ob + jnp.dot(pbb[:, WIN_B:], vb_n, preferred_element_type=F32)) / den
    half = G_B * t
    lo8 = lo[:t]
    ob = jnp.concatenate([jnp.where(lo8, ob[g * t:(g + 1) * t], ob[half + g * t:half + (g + 1) * t])
                          for g in range(G_B)], axis=1)
    return oa, ob


def _route(logits):
    lane = lax.broadcasted_iota(jnp.int32, logits.shape, 1).astype(F32)
    big = jnp.float32(1 << 20)
    ninf = jnp.float32(-jnp.inf)
    gmask = lane < N_GROUPS
    lg = jnp.where(gmask, logits, ninf)
    gmax = jnp.max(lg, axis=-1, keepdims=True)
    grp = jnp.min(jnp.where(lg == gmax, lane, big), axis=-1, keepdims=True)
    pg_top = 1.0 / jnp.sum(jnp.exp(lg - gmax), axis=-1, keepdims=True)
    e0 = N_GROUPS + grp * EXPERTS_PER_GROUP
    emask = jnp.logical_and(lane >= e0, lane < e0 + EXPERTS_PER_GROUP)
    le = jnp.where(emask, logits, ninf)
    emax = jnp.max(le, axis=-1, keepdims=True)
    esum = jnp.sum(jnp.exp(le - emax), axis=-1, keepdims=True)
    i1 = jnp.min(jnp.where(le == emax, lane, big), axis=-1, keepdims=True)
    le2 = jnp.where(lane == i1, ninf, le)
    e2max = jnp.max(le2, axis=-1, keepdims=True)
    i2 = jnp.min(jnp.where(le2 == e2max, lane, big), axis=-1, keepdims=True)
    p1 = 1.0 / esum
    p2 = jnp.exp(e2max - emax) / esum
    g1 = pg_top * p1 / (p1 + p2)
    g2 = pg_top * p2 / (p1 + p2)
    out = jnp.where(lane == 0, i1 - N_GROUPS, 0.0)
    out = jnp.where(lane == 1, i2 - N_GROUPS, out)
    out = jnp.where(lane == 2, g1, out)
    out = jnp.where(lane == 3, g2, out)
    return out


def _pack_bf16_pairs(x):
    half = x.shape[1] // 2

    def rne(v):
        bits = lax.bitcast_convert_type(v, jnp.int32)
        return bits + 0x7FFF + (lax.shift_right_logical(bits, 16) & 1)

    lo = lax.shift_right_logical(rne(x[:, :half]), 16)
    hi = rne(x[:, half:]) & jnp.int32(-65536)
    return lo | hi


def _unpack_bf16_pairs(w):
    lo = lax.bitcast_convert_type(lax.shift_left(w, 16), F32)
    hi = lax.bitcast_convert_type(w & jnp.int32(-65536), F32)
    return jnp.concatenate([lo, hi], axis=1)


def _out_router_kernel(xp_ref, ap_ref, bp_ref, xs_ref, q_ref, kvn_ref, akt_hbm, avt_hbm, bkt_ref, bvt_ref,
                       cba_ref, cbb_ref, sink_ref, wo_ref, g_ref, wr_ref, br_ref,
                       x1_ref, xn_ref, route_ref, cnt_ref,
                       xcat_scr, mix_scr, kbuf, vbuf, sem, *, prompt_tiles, decode_tiles, seqs_per_step):
    i = pl.program_id(0)
    seqs = prompt_tiles * seqs_per_step

    def cache_copies(n, slot):
        return (pltpu.make_async_copy(akt_hbm.at[n], kbuf.at[slot], sem.at[0, slot]),
                pltpu.make_async_copy(avt_hbm.at[n], vbuf.at[slot], sem.at[1, slot]))

    @pl.when(i == 0)
    def _():
        cnt_ref[...] = jnp.zeros_like(cnt_ref)
        xcat_scr[...] = jnp.zeros_like(xcat_scr)
        for n0 in range(2):
            for c in cache_copies(n0, n0):
                c.start()

    @pl.when(i == prompt_tiles)
    def _():
        for n1 in (seqs, seqs + 1):
            for c in cache_copies(seqs - 1, n1 % CACHE_BUFFERS):
                c.wait()

    pslot = lax.rem(i, 2)

    def route_previous():
        logits = jnp.dot(xcat_scr[1 - pslot], wr_ref[...], preferred_element_type=F32)
        route = _route(logits + br_ref[...])
        route_ref[...] = route
        lanef = lax.broadcasted_iota(jnp.int32, route.shape, 1).astype(F32)
        hits = (lanef == route[:, 0:1]).astype(F32) + (lanef == route[:, 1:2]).astype(F32)
        cnt_ref[...] += jnp.sum(hits, axis=0, keepdims=True) * (i > 0).astype(F32)

    def project(x_ref, mix):
        x1 = x_ref[...] + jnp.dot(mix, wo_ref[...], preferred_element_type=F32)
        x1_ref[...] = x1
        ms = jnp.mean(x1 * x1, axis=-1, keepdims=True)
        xn = x1 * lax.rsqrt(ms + EPS) * g_ref[...]
        xn_ref[...] = _pack_bf16_pairs(xn)
        xh = xn.astype(BF16)
        xl = (xn - xh.astype(F32)).astype(BF16)
        xcat_scr[pslot] = jnp.concatenate([xh, xl, xh], axis=1)

    @pl.when(i < prompt_tiles)
    def _():
        route_previous()
        mix = jnp.concatenate([ap_ref[0], ap_ref[1], ap_ref[2], ap_ref[3]], axis=1).astype(BF16)
        project(xp_ref, jnp.concatenate([mix, bp_ref[...]], axis=1))
        t = q_ref.shape[1]
        for s in range(seqs_per_step):
            n = i * seqs_per_step + s
            slot = lax.rem(n, CACHE_BUFFERS)
            for c in cache_copies(n, slot):
                c.wait()
            for c in cache_copies(jnp.minimum(n + 2, seqs - 1), lax.rem(n + 2, CACHE_BUFFERS)):
                c.start()
            oa, ob = _sample_attention(q_ref[s], kvn_ref[s], kbuf[slot], vbuf[slot], bkt_ref[s], bvt_ref[s],
                                       cba_ref[...], cbb_ref[...], sink_ref[...])
            row = pl.multiple_of(n * t, t)
            mix_scr[pl.ds(row, t), :A_WIDTH] = oa
            mix_scr[pl.ds(row, t), A_WIDTH:] = ob

    @pl.when(i >= prompt_tiles)
    def _():
        route_previous()
        tm = xs_ref.shape[0]
        row = pl.multiple_of(jnp.clip(i - prompt_tiles, 0, decode_tiles - 1) * tm, tm)
        project(xs_ref, mix_scr[pl.ds(row, tm), :].astype(BF16))


def _out_router(xp, a4p, bp, xs, q3, kvn3, akt, avt, bkt, bvt, cbias_a, cbias_b, sink_rows, wo, gamma, wr, br):
    tp, tsm = xp.shape[0], xs.shape[0]
    ns, ts = q3.shape[0], q3.shape[1]
    tm = 512
    npt, nst = tp // tm, tsm // tm
    nt = npt + nst
    t = tp + tsm
    sps = ns // npt
    assert sps * npt == ns and ns * ts == tsm and ns >= CACHE_BUFFERS
    pmap = lambda i: (jnp.minimum(i, npt - 1), 0)
    pmap3 = lambda i: (jnp.minimum(i, npt - 1), 0, 0)
    smap = lambda i: (jnp.clip(i - npt, 0, nst - 1), 0)
    cur = lambda i: (jnp.minimum(i, nt - 1), 0)
    const = lambda i: (0, 0)
    return pl.pallas_call(
        functools.partial(_out_router_kernel, prompt_tiles=npt, decode_tiles=nst, seqs_per_step=sps),
        grid=(nt + 1,),
        in_specs=[
            pl.BlockSpec((tm, D_MODEL), pmap),
            pl.BlockSpec((4, tm, LANES), lambda i: (0, jnp.minimum(i, npt - 1), 0)),
            pl.BlockSpec((tm, 512), pmap),
            pl.BlockSpec((tm, D_MODEL), smap),
            pl.BlockSpec((sps, ts, 1024), pmap3),
            pl.BlockSpec((sps, ts, 1280), pmap3),
            pl.BlockSpec(memory_space=pl.ANY),
            pl.BlockSpec(memory_space=pl.ANY),
            pl.BlockSpec((sps, LANES, WIN_B), pmap3),
            pl.BlockSpec((sps, LANES, WIN_B), pmap3),
            pl.BlockSpec((H_A * ts, WIN_A + LANES), const),
            pl.BlockSpec((H_B * ts, WIN_B + LANES), const),
            pl.BlockSpec((H_B * ts, 1), const),
            pl.BlockSpec((D_MODEL, D_MODEL), const),
            pl.BlockSpec((1, D_MODEL), const),
            pl.BlockSpec((3 * D_MODEL, LANES), const),
            pl.BlockSpec((1, LANES), const),
        ],
        out_specs=[
            pl.BlockSpec((tm, D_MODEL), cur),
            pl.BlockSpec((tm, D_MODEL // 2), cur),
            pl.BlockSpec((tm, LANES), lambda i: (jnp.maximum(i - 1, 0), 0)),
            pl.BlockSpec((1, LANES), const),
        ],
        scratch_shapes=[pltpu.VMEM((2, tm, 3 * D_MODEL), BF16), pltpu.VMEM((tsm, D_MODEL), F32),
                        pltpu.VMEM((CACHE_BUFFERS, A_WIDTH, WIN_A), F32),
                        pltpu.VMEM((CACHE_BUFFERS, A_WIDTH, WIN_A), F32),
                        pltpu.SemaphoreType.DMA((2, CACHE_BUFFERS))],
        out_shape=[
            jax.ShapeDtypeStruct((t, D_MODEL), F32),
            jax.ShapeDtypeStruct((t, D_MODEL // 2), jnp.int32),
            jax.ShapeDtypeStruct((t, LANES), F32),
            jax.ShapeDtypeStruct((1, LANES), F32),
        ],
        compiler_params=_cparams(("arbitrary",), vmem=OUT_ROUTER_VMEM),
        name="out_router",
    )(xp, a4p, bp, xs, q3, kvn3, akt, avt, bkt, bvt, cbias_a, cbias_b, sink_rows, wo, gamma, wr, br)


def _sc_gather_rows(table, idx):
    b = idx.shape[0]
    d = table.shape[1]
    w = SC_WINDOW
    per_worker = b // SC_WORKERS
    nwin = per_worker // w
    assert per_worker * SC_WORKERS == b and nwin * w == per_worker
    mesh = plsc.VectorSubcoreMesh(core_axis_name="c", subcore_axis_name="s")

    @functools.partial(
        pl.kernel, mesh=mesh,
        out_type=jax.ShapeDtypeStruct((b, d), table.dtype),
        scratch_types=[pltpu.VMEM((nwin, w), jnp.int32), pltpu.VMEM((2, w, d), table.dtype),
                       pltpu.SemaphoreType.DMA((2,)), pltpu.SemaphoreType.DMA((2,))],
        name="sc_gather_rows",
    )
    def gather(table_hbm, idx_hbm, out_hbm, idx_v, rows_v, sem_in, sem_out):
        wid = lax.axis_index("s") * SC_CORES + lax.axis_index("c")
        base = wid * per_worker
        pltpu.sync_copy(idx_hbm.at[wid], idx_v)

        def fetch(j):
            return pltpu.make_async_copy(table_hbm.at[idx_v.at[j]], rows_v.at[j % 2], sem_in.at[j % 2])

        def flush(j):
            return pltpu.make_async_copy(rows_v.at[j % 2], out_hbm.at[pl.ds(base + j * w, w)],
                                         sem_out.at[j % 2])

        fetch(0).start()
        for j in range(nwin):
            fetch(j).wait()
            if j + 1 < nwin:
                if j >= 1:
                    flush(j - 1).wait()
                fetch(j + 1).start()
            flush(j).start()
        for j in range(max(nwin - 2, 0), nwin):
            flush(j).wait()

    return gather(table, idx.reshape(SC_WORKERS, nwin, w))


def _sc_scatter_rows(x, dest2, nrows):
    t, d = x.shape
    w = SC_SCATTER_WINDOW
    per_worker = t // SC_WORKERS
    nwin = per_worker // w
    assert per_worker * SC_WORKERS == t and nwin * w == per_worker
    mesh = plsc.VectorSubcoreMesh(core_axis_name="c", subcore_axis_name="s")

    @functools.partial(
        pl.kernel, mesh=mesh,
        out_type=jax.ShapeDtypeStruct((nrows, d), x.dtype),
        scratch_types=[pltpu.VMEM((TOP_K, nwin, w), jnp.int32), pltpu.VMEM((2, w, d), x.dtype),
                       pltpu.SemaphoreType.DMA((2,)), pltpu.SemaphoreType.DMA((2,))],
        name="sc_scatter_rows",
    )
    def scatter(x_hbm, dest_hbm, out_hbm, idx_v, rows_v, sem_in, sem_out):
        wid = lax.axis_index("s") * SC_CORES + lax.axis_index("c")
        base = wid * per_worker
        for k in range(TOP_K):
            pltpu.sync_copy(dest_hbm.at[k, wid], idx_v.at[k])

        def fetch(j):
            return pltpu.make_async_copy(x_hbm.at[pl.ds(base + j * w, w)], rows_v.at[j % 2], sem_in.at[j % 2])

        def spread(j, k):
            return pltpu.make_async_copy(rows_v.at[j % 2], out_hbm.at[idx_v.at[k, j]], sem_out.at[j % 2])

        fetch(0).start()
        for j in range(nwin):
            fetch(j).wait()
            if j + 1 < nwin:
                if j >= 1:
                    for k in range(TOP_K):
                        spread(j - 1, k).wait()
                fetch(j + 1).start()
            for k in range(TOP_K):
                spread(j, k).start()
        for j in range(max(nwin - 2, 0), nwin):
            for k in range(TOP_K):
                spread(j, k).wait()

    return scatter(x, dest2.reshape(TOP_K, SC_WORKERS, nwin, w))


def _expert_kernel(be_ref, nu_ref, nv_ref, nx_ref, x_ref, wg_hbm, wu_hbm, wd_hbm, o_ref,
                   wg_s, wu_s, wd_s, wg_f, wu_f, wd_f, slot_s, sem):
    i = pl.program_id(0)
    used = i < nu_ref[0]
    changed = jnp.logical_or(i == 0, be_ref[i] != be_ref[jnp.maximum(i - 1, 0)])

    def weight_copies(e, slot):
        return (pltpu.make_async_copy(wg_hbm.at[e], wg_f.at[slot], sem.at[slot, 0]),
                pltpu.make_async_copy(wu_hbm.at[e], wu_f.at[slot], sem.at[slot, 1]),
                pltpu.make_async_copy(wd_hbm.at[e], wd_f.at[slot], sem.at[slot, 2]))

    @pl.when(i == 0)
    def _():
        slot_s[0] = 0
        for c in weight_copies(be_ref[0], 0):
            c.start()

    @pl.when(jnp.logical_and(used, changed))
    def _():
        slot = slot_s[0]
        for c in weight_copies(be_ref[i], slot):
            c.wait()
        wg_s[...] = wg_f[slot].astype(BF16)
        wu_s[...] = wu_f[slot].astype(BF16)
        wd_s[...] = wd_f[slot].astype(BF16)

        @pl.when(nx_ref[i] != be_ref[i])
        def _():
            for c in weight_copies(nx_ref[i], 1 - slot):
                c.start()

        slot_s[0] = 1 - slot

    @pl.when(used)
    def _():
        row = lax.broadcasted_iota(jnp.int32, x_ref.shape, 0)
        x = _unpack_bf16_pairs(jnp.where(row < nv_ref[i], x_ref[...], 0)).astype(BF16)
        gate = jnp.dot(x, wg_s[...], preferred_element_type=F32)
        up = jnp.dot(x, wu_s[...], preferred_element_type=F32)
        h = (gate * jax.nn.sigmoid(gate) * up).astype(BF16)
        o_ref[...] = _pack_bf16_pairs(jnp.dot(h, wd_s[...], preferred_element_type=F32))

    @pl.when(jnp.logical_not(used))
    def _():
        o_ref[...] = jnp.zeros_like(o_ref)


def _experts(blk_e, n_used, nvalid, next_e, xb, w_gate, w_up, w_down):
    rows = xb.shape[0]
    nblocks = rows // MOE_ROWS
    grid_spec = pltpu.PrefetchScalarGridSpec(
        num_scalar_prefetch=4,
        grid=(nblocks,),
        in_specs=[
            pl.BlockSpec((MOE_ROWS, D_MODEL // 2), lambda i, be, nu, nv, nx: (i, 0)),
            pl.BlockSpec(memory_space=pl.ANY),
            pl.BlockSpec(memory_space=pl.ANY),
            pl.BlockSpec(memory_space=pl.ANY),
        ],
        out_specs=pl.BlockSpec((MOE_ROWS, D_MODEL // 2), lambda i, be, nu, nv, nx: (i, 0)),
        scratch_shapes=[pltpu.VMEM((D_MODEL, D_EXPERT), BF16), pltpu.VMEM((D_MODEL, D_EXPERT), BF16),
                        pltpu.VMEM((D_EXPERT, D_MODEL), BF16),
                        pltpu.VMEM((2, D_MODEL, D_EXPERT), F32), pltpu.VMEM((2, D_MODEL, D_EXPERT), F32),
                        pltpu.VMEM((2, D_EXPERT, D_MODEL), F32),
                        pltpu.SMEM((1,), jnp.int32), pltpu.SemaphoreType.DMA((2, 3))],
    )
    return pl.pallas_call(
        _expert_kernel,
        grid_spec=grid_spec,
        out_shape=jax.ShapeDtypeStruct((rows, D_MODEL // 2), jnp.int32),
        compiler_params=_cparams(("arbitrary",)),
        name="experts",
    )(blk_e, n_used, nvalid, next_e, xb, w_gate, w_up, w_down)


def _combine_kernel(x1_ref, y1_ref, y2_ref, route_ref, g_ref, outp_ref, outs_ref, *, prompt_tiles):
    r = route_ref[...]
    x = (x1_ref[...] + r[:, 2:3] * _unpack_bf16_pairs(y1_ref[...])
         + r[:, 3:4] * _unpack_bf16_pairs(y2_ref[...]))
    ms = jnp.mean(x * x, axis=-1, keepdims=True)
    y = x * lax.rsqrt(ms + EPS) * g_ref[...]
    i = pl.program_id(0)

    @pl.when(i < prompt_tiles)
    def _():
        outp_ref[...] = y

    @pl.when(i >= prompt_tiles)
    def _():
        outs_ref[...] = y


def _combine_norm(x1, ygath, route, gamma, tp):
    t = x1.shape[0]
    tm = 512
    nt, npt = t // tm, tp // tm
    return pl.pallas_call(
        functools.partial(_combine_kernel, prompt_tiles=npt),
        grid=(nt,),
        in_specs=[
            pl.BlockSpec((tm, D_MODEL), lambda i: (i, 0)),
            pl.BlockSpec((tm, D_MODEL // 2), lambda i: (i, 0)),
            pl.BlockSpec((tm, D_MODEL // 2), lambda i: (i + nt, 0)),
            pl.BlockSpec((tm, LANES), lambda i: (i, 0)),
            pl.BlockSpec((1, D_MODEL), lambda i: (0, 0)),
        ],
        out_specs=[
            pl.BlockSpec((tm, D_MODEL), lambda i: (jnp.minimum(i, npt - 1), 0)),
            pl.BlockSpec((tm, D_MODEL), lambda i: (jnp.maximum(i - npt, 0), 0)),
        ],
        out_shape=[jax.ShapeDtypeStruct((tp, D_MODEL), F32), jax.ShapeDtypeStruct((t - tp, D_MODEL), F32)],
        compiler_params=_cparams(("arbitrary",)),
        name="combine_norm",
    )(x1, ygath, ygath, route, gamma)


def _band_index():
    c = (2 * QB - np.arange(2 * QB)) % (2 * QB)
    return c, c <= QB


def _bias_a_prompt(table_a):
    c, valid = _band_index()
    idx = np.stack([_t5_bucket_np(d * np.clip(QB - c, 0, QB)) for d in DILATIONS])
    return jnp.where(valid, jnp.transpose(table_a[idx], (0, 2, 1)) * LOG2E, NEG)


def _bias_b_prompt(table_b):
    c, valid = _band_index()
    valid = valid & (c >= 1)
    h = jnp.where(valid, table_b[_t5_bucket_np(np.clip(QB - c, 0, QB))].T * LOG2E, NEG)
    return jnp.transpose(h.reshape(KV_B, G_B, 2 * QB), (1, 0, 2)).reshape(H_B, 2 * QB)


def _sample_bias(table, span, t, log2_weight):
    cols = span + LANES
    period = cols + LANES
    x = np.arange(period)
    dist = np.where(x >= period - t, span - x + period, span - x)
    extra = log2_weight(dist)
    valid = np.isfinite(extra)
    u = jnp.where(valid, table[_t5_bucket_np(np.maximum(dist, 0))].T * LOG2E
                  + np.where(valid, extra, 0.0).astype(np.float32), NEG)
    rows = jnp.tile(u, (1, t))[:, :t * (period - 1)].reshape(u.shape[0], t, period - 1)[:, :, :cols]
    return rows.reshape(u.shape[0] * t, cols)


def _bias_a_sample(table_a, t):
    def log2_count(dist):
        count = np.zeros(dist.shape, np.int64)
        for w, d in zip(WINDOWS, DILATIONS):
            count += (dist >= 0) & (dist % d == 0) & (dist <= w)
        return np.where(count > 0, np.log2(np.maximum(count, 1)), -np.inf)

    return _sample_bias(table_a, WIN_A, t, log2_count)


def _bias_b_sample(table_b, t):
    return _sample_bias(table_b, WIN_B, t,
                        lambda dist: np.where((dist >= 0) & (dist < WIN_B), 0.0, -np.inf))


def _dest_kernel(route_ref, cnt_ref, tri_ref, dest_ref, meta_ref, run_scr, pst_scr):
    i = pl.program_id(0)
    tm = route_ref.shape[0]
    r = route_ref[...]
    lane = lax.broadcasted_iota(jnp.int32, (tm, LANES), 1)
    lanef = lane.astype(F32)
    oh0 = lanef == r[:, 0:1]
    oh1 = lanef == r[:, 1:2]
    ohf = jnp.concatenate([oh0, oh1], axis=0).astype(F32)

    @pl.when(i == 0)
    def _():
        cnt = jnp.broadcast_to(cnt_ref[...], (LANES, LANES))
        padded = jnp.floor((cnt + (MOE_ROWS - 1)) * (1.0 / MOE_ROWS)) * MOE_ROWS
        lane_e = lax.broadcasted_iota(jnp.int32, (LANES, LANES), 1)
        x = padded
        for sh in (1, 2, 4, 8, 16, 32, 64):
            x = x + jnp.where(lane_e >= sh, pltpu.roll(x, sh, 1), 0.0)
        pst_scr[...] = (x - padded)[0:1]
        run_scr[...] = jnp.zeros_like(run_scr)
        wide = lambda v: jnp.concatenate([v.T, v.T], axis=1)
        cnt_t, bend_t = wide(cnt), wide(x * (1.0 / MOE_ROWS))
        bstart_t = wide((x - padded) * (1.0 / MOE_ROWS))
        blk = lax.broadcasted_iota(jnp.int32, (LANES, 2 * LANES), 1).astype(F32)
        exp = lax.broadcasted_iota(jnp.int32, (LANES, 2 * LANES), 0)
        real = exp < N_EXPERTS
        blk_e = jnp.minimum(jnp.sum(jnp.where(real & (bend_t <= blk), 1.0, 0.0), axis=0, keepdims=True),
                            N_EXPERTS - 1.0)
        mine = exp.astype(F32) == blk_e
        within = blk[0:1] - jnp.sum(jnp.where(mine, bstart_t, 0.0), axis=0, keepdims=True)
        nvalid = jnp.clip(jnp.sum(jnp.where(mine, cnt_t, 0.0), axis=0, keepdims=True) - within * MOE_ROWS,
                          0.0, float(MOE_ROWS))
        n_used = jnp.max(jnp.where(real, bend_t, 0.0), axis=0, keepdims=True)
        later = real & (exp.astype(F32) > blk_e) & (cnt_t > 0.0)
        nxt = jnp.min(jnp.where(later, exp.astype(F32), float(LANES)), axis=0, keepdims=True)
        nxt = jnp.where(nxt >= N_EXPERTS, blk_e, nxt)
        meta_ref[...] = jnp.concatenate([blk_e, nvalid, n_used, nxt, jnp.zeros((4, 2 * LANES), F32)],
                                        axis=0).astype(jnp.int32)

    base = run_scr[...] + pst_scr[...] - 1.0
    vals = []
    for c in range(2 * tm // LANES):
        ohc = ohf[c * LANES:(c + 1) * LANES]
        vals.append(jnp.dot(tri_ref[...], ohc.astype(BF16), preferred_element_type=F32) + base)
        base = base + jnp.sum(ohc, axis=0, keepdims=True)
    val = jnp.concatenate(vals, axis=0)
    d0 = jnp.sum(jnp.where(oh0, val[:tm], 0.0), axis=-1, keepdims=True)
    d1 = jnp.sum(jnp.where(oh1, val[tm:], 0.0), axis=-1, keepdims=True)
    tile = jnp.where(lane == 0, d0, jnp.where(lane == 1, d1, 0.0))
    dest_ref[...] = tile.T[:8].astype(jnp.int32)
    run_scr[...] += jnp.sum(ohf, axis=0, keepdims=True)


def _dispatch(route, cnt):
    t = route.shape[0]
    tm = 512
    tri = (jnp.arange(LANES)[:, None] >= jnp.arange(LANES)[None, :]).astype(BF16)
    nblocks = -(-t * TOP_K // MOE_ROWS) + N_EXPERTS
    assert nblocks <= 2 * LANES
    dest, meta = pl.pallas_call(
        _dest_kernel,
        grid=(t // tm,),
        in_specs=[pl.BlockSpec((tm, LANES), lambda i: (i, 0)),
                  pl.BlockSpec((1, LANES), lambda i: (0, 0)),
                  pl.BlockSpec((LANES, LANES), lambda i: (0, 0))],
        out_specs=[pl.BlockSpec((8, tm), lambda i: (0, i)),
                   pl.BlockSpec((8, 2 * LANES), lambda i: (0, 0))],
        out_shape=[jax.ShapeDtypeStruct((8, t), jnp.int32), jax.ShapeDtypeStruct((8, 2 * LANES), jnp.int32)],
        scratch_shapes=[pltpu.VMEM((1, LANES), F32), pltpu.VMEM((1, LANES), F32)],
        compiler_params=_cparams(("arbitrary",)),
        name="moe_dest",
    )(route, cnt, tri)
    return dest[:TOP_K], meta[0, :nblocks], meta[2, :1], meta[1, :nblocks], meta[3, :nblocks]


def kernel(x_prompt, x_sample, cache_a_k, cache_a_v, cache_b_k, cache_b_v, rel_bias_table, attn_norm, w_in,
           w_out, attn_sinks, ffn_norm, w_router_group, b_router_group, w_router_expert, b_router_expert,
           w_gate, w_up, w_down, final_norm):
    s = x_prompt.shape[1]
    ns, ts = x_sample.shape[0], x_sample.shape[1]
    table_a = rel_bias_table[:, :H_A]
    table_b = rel_bias_table[:, H_A:]

    w = w_in[0]
    wqa, wka, wva, wqb, wkb, wvb = (w[:, 0:512], w[:, 512:1024], w[:, 1024:1536], w[:, 1536:2048],
                                    w[:, 2048:2176], w[:, 2176:2304])
    wqb = jnp.transpose(wqb.reshape(D_MODEL, KV_B, G_B, HEAD_DIM), (0, 2, 1, 3)).reshape(D_MODEL, 512)
    wp = jnp.concatenate([wka, wva, wqa, wqb, wkb, wvb], axis=1).astype(BF16)
    cscale = jnp.concatenate([jnp.ones((1, 1024), F32), jnp.full((1, 1024), SCALE * LOG2E, F32),
                              jnp.ones((1, 256), F32)], axis=1)
    wo = w_out[0]
    wo_b = jnp.transpose(wo[512:].reshape(KV_B, G_B, HEAD_DIM, D_MODEL), (1, 0, 2, 3)).reshape(512, D_MODEL)
    wo_p = jnp.concatenate([wo[:512], wo_b], axis=0).astype(BF16)
    wr = jnp.concatenate([w_router_group[0],
                          jnp.transpose(w_router_expert[0], (1, 0, 2)).reshape(D_MODEL, N_EXPERTS),
                          jnp.zeros((D_MODEL, LANES - N_GROUPS - N_EXPERTS), F32)], axis=1)
    wr_hi = wr.astype(BF16)
    wr = jnp.concatenate([wr_hi, wr_hi, (wr - wr_hi.astype(F32)).astype(BF16)], axis=0)
    br = jnp.concatenate([b_router_group[0], b_router_expert[0].reshape(N_EXPERTS),
                          jnp.zeros((LANES - N_GROUPS - N_EXPERTS,), F32)]).reshape(1, LANES)
    sinks2 = attn_sinks[0] * LOG2E
    sinks_gk = jnp.transpose(sinks2.reshape(KV_B, G_B), (1, 0)).reshape(H_B)
    sink_rows_p = jnp.repeat(sinks_gk, QB).reshape(G_B, 1, 2 * QB)
    sink_rows_s = jnp.repeat(sinks2, ts).reshape(H_B * ts, 1)
    emat = jnp.tile(jnp.arange(LANES)[:, None] == (jnp.arange(A_WIDTH)[None, :] // HEAD_DIM),
                    (3, 1)).astype(BF16)
    attn_g = attn_norm[0].reshape(1, D_MODEL)
    ffn_g = ffn_norm[0].reshape(1, D_MODEL)

    xp = x_prompt.reshape(s, D_MODEL)
    aperm, qb_p, kvb_p, akv32, bkv32 = _proj_prompt(xp, attn_g, wp, cscale)
    a4 = _attn_a_prompt(aperm, _bias_a_prompt(table_a), emat)
    ob_p = _attn_b_prompt(qb_p, kvb_p, _bias_b_prompt(table_b), sink_rows_p)

    xs = x_sample.reshape(ns * ts, D_MODEL)
    q_s, kv_s = _proj_sample(xs, attn_g, wp, cscale)
    akt = jnp.transpose(cache_a_k[0], (0, 2, 3, 1)).reshape(ns, A_WIDTH, WIN_A)
    avt = jnp.transpose(cache_a_v[0], (0, 2, 3, 1)).reshape(ns, A_WIDTH, WIN_A)
    bkt = jnp.transpose(cache_b_k[0], (0, 2, 3, 1)).reshape(ns, LANES, WIN_B)
    bvt = jnp.transpose(cache_b_v[0], (0, 2, 3, 1)).reshape(ns, LANES, WIN_B)

    x1, xn, route, cnt = _out_router(xp, a4, ob_p, xs, q_s.reshape(ns, ts, 1024), kv_s.reshape(ns, ts, 1280),
                                     akt, avt, bkt, bvt, _bias_a_sample(table_a, ts),
                                     _bias_b_sample(table_b, ts), sink_rows_s, wo_p, ffn_g, wr, br)
    dest2, blk_e, n_used, nvalid, next_e = _dispatch(route, cnt)
    xb = _sc_scatter_rows(xn, dest2, blk_e.shape[0] * MOE_ROWS)
    yb = _experts(blk_e, n_used, nvalid, next_e, xb, w_gate[0], w_up[0], w_down[0])
    y_p, y_s = _combine_norm(x1, _sc_gather_rows(yb, dest2.reshape(-1)), route, final_norm.reshape(1, D_MODEL), s)

    y_prompt = y_p.reshape(1, s, D_MODEL)
    y_sample = y_s.reshape(ns, ts, D_MODEL)
    keep_a, keep_b = min(WIN_A, s), min(WIN_B, s)
    pak = akv32[SPAN - keep_a:, :512].reshape(1, 1, keep_a, H_A, HEAD_DIM)
    pav = akv32[SPAN - keep_a:, 512:].reshape(1, 1, keep_a, H_A, HEAD_DIM)
    pbk = bkv32[SPAN - keep_b:, :128].reshape(1, 1, keep_b, KV_B, HEAD_DIM)
    pbv = bkv32[SPAN - keep_b:, 128:].reshape(1, 1, keep_b, KV_B, HEAD_DIM)
    sak = kv_s[:, 0:512].reshape(1, ns, ts, H_A, HEAD_DIM)
    sav = kv_s[:, 512:1024].reshape(1, ns, ts, H_A, HEAD_DIM)
    sbk = kv_s[:, 1024:1152].reshape(1, ns, ts, KV_B, HEAD_DIM)
    sbv = kv_s[:, 1152:1280].reshape(1, ns, ts, KV_B, HEAD_DIM)
    return (y_prompt, y_sample, pak, pav, pbk, pbv, sak, sav, sbk, sbv)
```

```python
import functools
import math

import jax
import jax.numpy as jnp
import numpy as np
from jax import lax
from jax.experimental import pallas as pl
from jax.experimental.pallas import tpu as pltpu
from jax.experimental.pallas import tpu_sc as plsc

D_MODEL = 1024
HEAD_DIM = 64
H_A = 8
H_B = 8
KV_B = 2
G_B = 4
DILATIONS = (1, 4, 16)
WINDOWS = (128, 512, 2048)
WIN_A = 2048
WIN_B = 128
NUM_BUCKETS = 32
MAX_DISTANCE = 2048
N_GROUPS = 4
EXPERTS_PER_GROUP = 8
N_EXPERTS = 32
TOP_K = 2
D_EXPERT = 512
EPS = 1e-5
SCALE = HEAD_DIM ** -0.5

LANES = 128
SPAN = 2048
QB = 128
NCHUNK = 9
A_WIDTH = H_A * HEAD_DIM
MOE_ROWS = 512
SC_CORES = 2
SC_SUBCORES = 16
SC_WORKERS = SC_CORES * SC_SUBCORES
SC_WINDOW = 64
SC_SCATTER_WINDOW = 32
NEG = -1e30
LOG2E = math.log2(math.e)
B_STEP = 512
V7X_VMEM_BYTES = 64 * 1024 * 1024
VMEM_LIMIT = V7X_VMEM_BYTES - 8 * 1024 * 1024
CACHE_BUFFERS = 3
OUT_ROUTER_VMEM = V7X_VMEM_BYTES - 4 * 1024 * 1024

F32 = jnp.float32
BF16 = jnp.bfloat16


def _t5_bucket_np(dist):
    dist = np.asarray(dist, np.int64)
    max_exact = NUM_BUCKETS // 2
    d = np.maximum(dist, 1).astype(np.float32)
    ratio = np.log(d / np.float32(max_exact)) / np.float32(math.log(MAX_DISTANCE / max_exact))
    large = max_exact + (ratio * np.float32(NUM_BUCKETS - max_exact)).astype(np.int32)
    large = np.minimum(large, NUM_BUCKETS - 1)
    return np.where(dist < max_exact, dist, large).astype(np.int32)


def _cparams(sem, vmem=VMEM_LIMIT):
    return pltpu.CompilerParams(dimension_semantics=sem, vmem_limit_bytes=vmem)


def _proj_prompt_kernel(x_ref, g_ref, w_ref, cs_ref, aperm_ref, qb_ref, kvb_ref, akv_ref, bkv_ref,
                        h_scr, p_scr):
    n = pl.program_id(1)

    @pl.when(n == 0)
    def _():
        x = x_ref[...]
        ms = jnp.mean(x * x, axis=-1, keepdims=True)
        h_scr[...] = (x * lax.rsqrt(ms + EPS) * g_ref[...]).astype(BF16)

    p = jnp.dot(h_scr[...], w_ref[...], preferred_element_type=F32) * cs_ref[...]

    @pl.when(n < 6)
    def _():
        aperm_ref[0] = p.astype(BF16)
        p_scr[0, 0] = p[:, :LANES]
        p_scr[0, 1] = p[:, LANES:]
        quarter = SPAN // 4
        for r in range(4):
            lo = p_scr[0, 0, pl.ds(r, quarter, stride=4), :]
            hi = p_scr[0, 1, pl.ds(r, quarter, stride=4), :]
            p_scr[1, 0, r * quarter:(r + 1) * quarter, :] = lo
            p_scr[1, 1, r * quarter:(r + 1) * quarter, :] = hi
            aperm_ref[1, r * quarter:(r + 1) * quarter, :] = jnp.concatenate([lo, hi], axis=1).astype(BF16)
        for r16 in range(16):
            start = (r16 % 4) * quarter + r16 // 4
            t = jnp.concatenate([p_scr[1, 0, pl.ds(start, QB, stride=4), :],
                                 p_scr[1, 1, pl.ds(start, QB, stride=4), :]], axis=1)
            aperm_ref[2, r16 * QB:(r16 + 1) * QB, :] = t.astype(BF16)

    @pl.when(n < 4)
    def _():
        akv_ref[...] = p

    @pl.when(jnp.logical_or(n == 6, n == 7))
    def _():
        qb_ref[...] = p.astype(BF16)

    @pl.when(n == 8)
    def _():
        kvb_ref[...] = p.astype(BF16)
        bkv_ref[...] = p


def _proj_prompt(x, gamma, w, cscale):
    s = x.shape[0]
    nspan = s // SPAN
    return pl.pallas_call(
        _proj_prompt_kernel,
        grid=(nspan, NCHUNK),
        in_specs=[
            pl.BlockSpec((SPAN, D_MODEL), lambda b, n: (b, 0)),
            pl.BlockSpec((1, D_MODEL), lambda b, n: (0, 0)),
            pl.BlockSpec((D_MODEL, 256), lambda b, n: (0, n)),
            pl.BlockSpec((1, 256), lambda b, n: (0, n)),
        ],
        out_specs=[
            pl.BlockSpec((3, SPAN, 256), lambda b, n: (0, b, jnp.minimum(n, 5))),
            pl.BlockSpec((SPAN, 256), lambda b, n: (b, jnp.clip(n - 6, 0, 1))),
            pl.BlockSpec((SPAN, 256), lambda b, n: (b, 0)),
            pl.BlockSpec((SPAN, 256), lambda b, n: (0, jnp.where(b == nspan - 1, jnp.minimum(n, 3), 0))),
            pl.BlockSpec((SPAN, 256), lambda b, n: (0, 0)),
        ],
        out_shape=[
            jax.ShapeDtypeStruct((3, s, 3 * A_WIDTH), BF16),
            jax.ShapeDtypeStruct((s, 512), BF16),
            jax.ShapeDtypeStruct((s, 256), BF16),
            jax.ShapeDtypeStruct((SPAN, 1024), F32),
            jax.ShapeDtypeStruct((SPAN, 256), F32),
        ],
        scratch_shapes=[pltpu.VMEM((SPAN, D_MODEL), BF16), pltpu.VMEM((2, 2, SPAN, LANES), F32)],
        compiler_params=_cparams(("arbitrary", "arbitrary")),
        name="proj_prompt",
    )(x, gamma, w, cscale)


def _proj_sample_kernel(x_ref, g_ref, w_ref, cs_ref, q_ref, kv_ref):
    x = x_ref[...]
    ms = jnp.mean(x * x, axis=-1, keepdims=True)
    h = (x * lax.rsqrt(ms + EPS) * g_ref[...]).astype(BF16)
    p = jnp.dot(h, w_ref[...], preferred_element_type=F32) * cs_ref[...]
    kv_ref[:, :1024] = p[:, :1024]
    kv_ref[:, 1024:] = p[:, 2048:]
    q_ref[...] = p[:, 1024:2048]


def _proj_sample(x, gamma, w, cscale):
    t = x.shape[0]
    tm = 512
    return pl.pallas_call(
        _proj_sample_kernel,
        grid=(t // tm,),
        in_specs=[
            pl.BlockSpec((tm, D_MODEL), lambda i: (i, 0)),
            pl.BlockSpec((1, D_MODEL), lambda i: (0, 0)),
            pl.BlockSpec((D_MODEL, 2304), lambda i: (0, 0)),
            pl.BlockSpec((1, 2304), lambda i: (0, 0)),
        ],
        out_specs=[
            pl.BlockSpec((tm, 1024), lambda i: (i, 0)),
            pl.BlockSpec((tm, 1280), lambda i: (i, 0)),
        ],
        out_shape=[
            jax.ShapeDtypeStruct((t, 1024), F32),
            jax.ShapeDtypeStruct((t, 1280), F32),
        ],
        compiler_params=_cparams(("arbitrary",)),
        name="proj_sample",
    )(x, gamma, w, cscale)


def _spread_heads(w, e3_ref):
    hi = w.astype(BF16)
    r1 = w - hi.astype(F32)
    mid = r1.astype(BF16)
    low = (r1 - mid.astype(F32)).astype(BF16)
    return jnp.dot(jnp.concatenate([hi, mid, low], axis=1), e3_ref[...], preferred_element_type=F32)


def _pair_tile(q2, kk, vv, bias_t, lo, sink=None):
    zero = jnp.zeros_like(q2)
    qq = jnp.concatenate([jnp.where(lo, q2, zero), jnp.where(lo, zero, q2)], axis=0)
    st = lax.dot_general(kk, qq, (((1,), (1,)), ((), ())), preferred_element_type=F32)
    st = st + bias_t
    m = jnp.max(st, axis=0, keepdims=True)
    if sink is not None:
        m = jnp.maximum(m, sink)
    p = jnp.exp2(st - m)
    den = jnp.sum(p, axis=0, keepdims=True)
    if sink is not None:
        den = den + jnp.exp2(sink - m)
    pn = (p * (1.0 / den)).astype(BF16)
    o = lax.dot_general(pn, vv, (((0,), (0,)), ((), ())), preferred_element_type=F32)
    return jnp.where(lo, o[:QB], o[QB:]), m + jnp.log2(den)


def _fill_band_tiles(h_ref, bias_scr):
    nk = 2 * QB
    prev = lax.broadcasted_iota(jnp.int32, (nk, nk), 0) < QB
    for pair in range(h_ref.shape[0] // 2):
        halves = []
        for hh in range(2):
            row = h_ref[2 * pair + hh:2 * pair + hh + 1, :]
            band = pltpu.roll(jnp.broadcast_to(row, (nk, nk)), 0, 1, stride=1, stride_axis=0)
            halves.append(band[:, :QB])
        tile = jnp.concatenate(halves, axis=1)
        bias_scr[0, pair] = tile
        bias_scr[1, pair] = jnp.where(prev, NEG, tile)


def _attn_a_kernel(q_ref, kvc_ref, kvp_ref, h_ref, e_ref, out_ref, o_scr, st_scr, bias_scr):
    b = pl.program_id(0)
    g = pl.program_id(1)
    nblk = jnp.where(g == 0, 16, jnp.where(g == 1, 4, 1))
    lane = lax.broadcasted_iota(jnp.int32, (QB, LANES), 1)
    lo = lane < HEAD_DIM

    @pl.when(b == 0)
    def _():
        _fill_band_tiles(h_ref, bias_scr.at[g])

    bias_ref = bias_scr.at[g]

    for cb in range(SPAN // QB):
        first = lax.rem(jnp.int32(cb), nblk) == 0
        rows = slice(cb * QB, (cb + 1) * QB)
        prow_c = max(cb - 1, 0) * QB
        prow_p = pl.multiple_of(jnp.where(first, cb + nblk - 1, 0) * QB, QB)
        variant = jnp.logical_and(first, b == 0).astype(jnp.int32)
        stats = []
        for hp in range(4):
            ks = slice(hp * LANES, (hp + 1) * LANES)
            vs = slice(A_WIDTH + hp * LANES, A_WIDTH + (hp + 1) * LANES)
            kp = jnp.where(first, kvp_ref[pl.ds(prow_p, QB), ks], kvc_ref[prow_c:prow_c + QB, ks])
            vp = jnp.where(first, kvp_ref[pl.ds(prow_p, QB), vs], kvc_ref[prow_c:prow_c + QB, vs])
            kk = jnp.concatenate([kp, kvc_ref[rows, ks]], axis=0)
            vv = jnp.concatenate([vp, kvc_ref[rows, vs]], axis=0)
            o, lse = _pair_tile(q_ref[rows, ks], kk, vv, bias_ref[variant, hp], lo)
            o_scr[g, hp, rows, :] = o
            stats += [lse[:, :QB], lse[:, QB:]]
        sm = jnp.concatenate(stats + [jnp.zeros((LANES - H_A, QB), F32)], axis=0)
        st_scr[g, rows, :] = sm.T

    @pl.when(g == 2)
    def _():
        def merge(c, carry):
            r2 = lax.rem(c, 4) * (SPAN // 4) + c // 4
            r3 = pl.multiple_of(c * QB, QB)
            l1 = st_scr[0, pl.ds(c, QB, stride=16), :]
            l2 = st_scr[1, pl.ds(r2, QB, stride=4), :]
            l3 = st_scr[2, pl.ds(r3, QB), :]
            mx = jnp.maximum(jnp.maximum(l1, l2), l3)
            w1 = jnp.exp2(l1 - mx)
            w2 = jnp.exp2(l2 - mx)
            w3 = jnp.exp2(l3 - mx)
            tot = w1 + w2 + w3
            a1 = _spread_heads(w1 / tot, e_ref)
            a2 = _spread_heads(w2 / tot, e_ref)
            a3 = _spread_heads(w3 / tot, e_ref)
            for hp in range(4):
                sl = slice(hp * LANES, (hp + 1) * LANES)
                o1 = o_scr[0, hp, pl.ds(c, QB, stride=16), :]
                o2 = o_scr[1, hp, pl.ds(r2, QB, stride=4), :]
                o3 = o_scr[2, hp, pl.ds(r3, QB), :]
                out_ref[hp, pl.ds(c, QB, stride=16), :] = a1[:, sl] * o1 + a2[:, sl] * o2 + a3[:, sl] * o3
            return carry

        lax.fori_loop(0, 16, merge, 0, unroll=4)


def _attn_a_prompt(aperm, bias_a, emat):
    s = aperm.shape[1]
    nspan = s // SPAN
    return pl.pallas_call(
        _attn_a_kernel,
        grid=(nspan, 3),
        in_specs=[
            pl.BlockSpec((None, SPAN, A_WIDTH), lambda b, g: (g, b, 2)),
            pl.BlockSpec((None, SPAN, 2 * A_WIDTH), lambda b, g: (g, b, 0)),
            pl.BlockSpec((None, SPAN, 2 * A_WIDTH), lambda b, g: (g, jnp.maximum(b - 1, 0), 0)),
            pl.BlockSpec((None, H_A, 2 * QB), lambda b, g: (g, 0, 0)),
            pl.BlockSpec((3 * LANES, A_WIDTH), lambda b, g: (0, 0)),
        ],
        out_specs=pl.BlockSpec((4, SPAN, LANES), lambda b, g: (0, b, 0)),
        out_shape=jax.ShapeDtypeStruct((4, s, LANES), F32),
        scratch_shapes=[pltpu.VMEM((3, 4, SPAN, LANES), F32), pltpu.VMEM((3, SPAN, LANES), F32),
                        pltpu.VMEM((3, 2, 4, 2 * QB, 2 * QB), F32)],
        compiler_params=_cparams(("arbitrary", "arbitrary")),
        name="attn_a_prompt",
    )(aperm, aperm, aperm, bias_a, emat)


def _attn_b_kernel(q_ref, kvc_ref, kvp_ref, h_ref, sink_ref, out_ref, bias_ref):
    i = pl.program_id(0)
    lane = lax.broadcasted_iota(jnp.int32, (QB, LANES), 1)
    lo = lane < HEAD_DIM

    @pl.when(i == 0)
    def _():
        _fill_band_tiles(h_ref, bias_ref)

    variant = (i == 0).astype(jnp.int32)
    for j in range(B_STEP // QB):
        rows = slice(j * QB, (j + 1) * QB)
        if j == 0:
            kp, vp = kvp_ref[:, :LANES], kvp_ref[:, LANES:]
        else:
            kp, vp = kvc_ref[(j - 1) * QB:j * QB, :LANES], kvc_ref[(j - 1) * QB:j * QB, LANES:]
        kk = jnp.concatenate([kp, kvc_ref[rows, :LANES]], axis=0)
        vv = jnp.concatenate([vp, kvc_ref[rows, LANES:]], axis=0)
        for g in range(G_B):
            bias_t = bias_ref[variant, g] if j == 0 else bias_ref[0, g]
            o, _ = _pair_tile(q_ref[rows, g * LANES:(g + 1) * LANES], kk, vv, bias_t, lo, sink=sink_ref[g])
            out_ref[rows, g * LANES:(g + 1) * LANES] = o.astype(BF16)


def _attn_b_prompt(qb, kvb, bias_b, sink_rows):
    s = qb.shape[0]
    per = B_STEP // QB
    return pl.pallas_call(
        _attn_b_kernel,
        grid=(s // B_STEP,),
        in_specs=[
            pl.BlockSpec((B_STEP, 512), lambda i: (i, 0)),
            pl.BlockSpec((B_STEP, 256), lambda i: (i, 0)),
            pl.BlockSpec((QB, 256), lambda i: (jnp.maximum(i * per - 1, 0), 0)),
            pl.BlockSpec((H_B, 2 * QB), lambda i: (0, 0)),
            pl.BlockSpec((G_B, 1, 2 * QB), lambda i: (0, 0, 0)),
        ],
        out_specs=pl.BlockSpec((B_STEP, 512), lambda i: (i, 0)),
        out_shape=jax.ShapeDtypeStruct((s, 512), BF16),
        scratch_shapes=[pltpu.VMEM((2, G_B, 2 * QB, 2 * QB), F32)],
        compiler_params=_cparams(("arbitrary",)),
        name="attn_b_prompt",
    )(qb, kvb, kvb, bias_b, sink_rows)


def _sample_attention(q, kvn, akt, avt, bkt, bvt, cba, cbb, sink):
    t = q.shape[0]
    kvn_p = jnp.concatenate([kvn, jnp.zeros((LANES - t, kvn.shape[1]), F32)], axis=0).astype(BF16)
    lane_a = lax.broadcasted_iota(jnp.int32, (t, A_WIDTH), 1) // HEAD_DIM

    qa = q[:, :A_WIDTH]
    qbd = jnp.concatenate([jnp.where(lane_a == h, qa, 0.0) for h in range(H_A)], axis=0).astype(BF16)
    s_c = jnp.dot(qbd, akt.astype(BF16), preferred_element_type=F32)
    s_n = lax.dot_general(qbd, kvn_p[:, :A_WIDTH], (((1,), (1,)), ((), ())), preferred_element_type=F32)
    s = jnp.concatenate([s_c, s_n], axis=1) + cba
    m = jnp.max(s, axis=-1, keepdims=True)
    p = jnp.exp2(s - m)
    l = jnp.sum(p, axis=-1, keepdims=True)
    pb = p.astype(BF16)
    o_n = jnp.dot(pb[:, WIN_A:], kvn_p[:, A_WIDTH:2 * A_WIDTH], preferred_element_type=F32)
    o_all = lax.dot_general(pb[:, :WIN_A], avt.astype(BF16), (((1,), (1,)), ((), ())),
                            preferred_element_type=F32) + o_n
    o_sel = jnp.zeros((t, A_WIDTH), F32)
    l_b = jnp.ones((t, A_WIDTH), F32)
    for h in range(H_A):
        sel = lane_a == h
        o_sel = jnp.where(sel, o_all[h * t:(h + 1) * t], o_sel)
        l_b = jnp.where(sel, l[h * t:(h + 1) * t], l_b)
    oa = o_sel / l_b

    lane_b = lax.broadcasted_iota(jnp.int32, (G_B * t, LANES), 1)
    lo = lane_b < HEAD_DIM
    qb2 = jnp.concatenate([q[:, A_WIDTH + g * LANES:A_WIDTH + (g + 1) * LANES] for g in range(G_B)], axis=0)
    qm = jnp.concatenate([jnp.where(lo, qb2, 0.0), jnp.where(lo, 0.0, qb2)], axis=0).astype(BF16)
    kb_n = kvn_p[:, 2 * A_WIDTH:2 * A_WIDTH + LANES]
    vb_n = kvn_p[:, 2 * A_WIDTH + LANES:]
    sb_c = jnp.dot(qm, bkt.astype(BF16), preferred_element_type=F32)
    sb_n = lax.dot_general(qm, kb_n, (((1,), (1,)), ((), ())), preferred_element_type=F32)
    sb = jnp.concatenate([sb_c, sb_n], axis=1) + cbb
    mb = jnp.maximum(jnp.max(sb, axis=-1, keepdims=True), sink)
    pbb = jnp.exp2(sb - mb)
    den = jnp.sum(pbb, axis=-1, keepdims=True) + jnp.exp2(sink - mb)
    pbb = pbb.astype(BF16)
    ob = lax.dot_general(pbb[:, :WIN_B], bvt.astype(BF16), (((1,), (1,)), ((), ())),
                         preferred_element_type=F32)
    ob = (ob + jnp.dot(pbb[:, WIN_B:], vb_n, preferred_element_type=F32)) / den
    half = G_B * t
    lo8 = lo[:t]
    ob = jnp.concatenate([jnp.where(lo8, ob[g * t:(g + 1) * t], ob[half + g * t:half + (g + 1) * t])
                          for g in range(G_B)], axis=1)
    return oa, ob


def _route(logits):
    lane = lax.broadcasted_iota(jnp.int32, logits.shape, 1).astype(F32)
    big = jnp.float32(1 << 20)
    ninf = jnp.float32(-jnp.inf)
    gmask = lane < N_GROUPS
    lg = jnp.where(gmask, logits, ninf)
    gmax = jnp.max(lg, axis=-1, keepdims=True)
    grp = jnp.min(jnp.where(lg == gmax, lane, big), axis=-1, keepdims=True)
    pg_top = 1.0 / jnp.sum(jnp.exp(lg - gmax), axis=-1, keepdims=True)
    e0 = N_GROUPS + grp * EXPERTS_PER_GROUP
    emask = jnp.logical_and(lane >= e0, lane < e0 + EXPERTS_PER_GROUP)
    le = jnp.where(emask, logits, ninf)
    emax = jnp.max(le, axis=-1, keepdims=True)
    esum = jnp.sum(jnp.exp(le - emax), axis=-1, keepdims=True)
    i1 = jnp.min(jnp.where(le == emax, lane, big), axis=-1, keepdims=True)
    le2 = jnp.where(lane == i1, ninf, le)
    e2max = jnp.max(le2, axis=-1, keepdims=True)
    i2 = jnp.min(jnp.where(le2 == e2max, lane, big), axis=-1, keepdims=True)
    p1 = 1.0 / esum
    p2 = jnp.exp(e2max - emax) / esum
    g1 = pg_top * p1 / (p1 + p2)
    g2 = pg_top * p2 / (p1 + p2)
    out = jnp.where(lane == 0, i1 - N_GROUPS, 0.0)
    out = jnp.where(lane == 1, i2 - N_GROUPS, out)
    out = jnp.where(lane == 2, g1, out)
    out = jnp.where(lane == 3, g2, out)
    return out


def _pack_bf16_pairs(x):
    half = x.shape[1] // 2

    def rne(v):
        bits = lax.bitcast_convert_type(v, jnp.int32)
        return bits + 0x7FFF + (lax.shift_right_logical(bits, 16) & 1)

    lo = lax.shift_right_logical(rne(x[:, :half]), 16)
    hi = rne(x[:, half:]) & jnp.int32(-65536)
    return lo | hi


def _unpack_bf16_pairs(w):
    lo = lax.bitcast_convert_type(lax.shift_left(w, 16), F32)
    hi = lax.bitcast_convert_type(w & jnp.int32(-65536), F32)
    return jnp.concatenate([lo, hi], axis=1)


def _out_router_kernel(xp_ref, ap_ref, bp_ref, xs_ref, q_ref, kvn_ref, akt_hbm, avt_hbm, bkt_ref, bvt_ref,
                       cba_ref, cbb_ref, sink_ref, wo_ref, g_ref, wr_ref, br_ref,
                       x1_ref, xn_ref, route_ref, cnt_ref,
                       xcat_scr, mix_scr, kbuf, vbuf, sem, *, prompt_tiles, decode_tiles, seqs_per_step):
    i = pl.program_id(0)
    seqs = prompt_tiles * seqs_per_step

    def cache_copies(n, slot):
        return (pltpu.make_async_copy(akt_hbm.at[n], kbuf.at[slot], sem.at[0, slot]),
                pltpu.make_async_copy(avt_hbm.at[n], vbuf.at[slot], sem.at[1, slot]))

    @pl.when(i == 0)
    def _():
        cnt_ref[...] = jnp.zeros_like(cnt_ref)
        xcat_scr[...] = jnp.zeros_like(xcat_scr)
        for n0 in range(2):
            for c in cache_copies(n0, n0):
                c.start()

    @pl.when(i == prompt_tiles)
    def _():
        for n1 in (seqs, seqs + 1):
            for c in cache_copies(seqs - 1, n1 % CACHE_BUFFERS):
                c.wait()

    pslot = lax.rem(i, 2)

    def route_previous():
        logits = jnp.dot(xcat_scr[1 - pslot], wr_ref[...], preferred_element_type=F32)
        route = _route(logits + br_ref[...])
        route_ref[...] = route
        lanef = lax.broadcasted_iota(jnp.int32, route.shape, 1).astype(F32)
        hits = (lanef == route[:, 0:1]).astype(F32) + (lanef == route[:, 1:2]).astype(F32)
        cnt_ref[...] += jnp.sum(hits, axis=0, keepdims=True) * (i > 0).astype(F32)

    def project(x_ref, mix):
        x1 = x_ref[...] + jnp.dot(mix, wo_ref[...], preferred_element_type=F32)
        x1_ref[...] = x1
        ms = jnp.mean(x1 * x1, axis=-1, keepdims=True)
        xn = x1 * lax.rsqrt(ms + EPS) * g_ref[...]
        xn_ref[...] = _pack_bf16_pairs(xn)
        xh = xn.astype(BF16)
        xl = (xn - xh.astype(F32)).astype(BF16)
        xcat_scr[pslot] = jnp.concatenate([xh, xl, xh], axis=1)

    @pl.when(i < prompt_tiles)
    def _():
        route_previous()
        mix = jnp.concatenate([ap_ref[0], ap_ref[1], ap_ref[2], ap_ref[3]], axis=1).astype(BF16)
        project(xp_ref, jnp.concatenate([mix, bp_ref[...]], axis=1))
        t = q_ref.shape[1]
        for s in range(seqs_per_step):
            n = i * seqs_per_step + s
            slot = lax.rem(n, CACHE_BUFFERS)
            for c in cache_copies(n, slot):
                c.wait()
            for c in cache_copies(jnp.minimum(n + 2, seqs - 1), lax.rem(n + 2, CACHE_BUFFERS)):
                c.start()
            oa, ob = _sample_attention(q_ref[s], kvn_ref[s], kbuf[slot], vbuf[slot], bkt_ref[s], bvt_ref[s],
                                       cba_ref[...], cbb_ref[...], sink_ref[...])
            row = pl.multiple_of(n * t, t)
            mix_scr[pl.ds(row, t), :A_WIDTH] = oa
            mix_scr[pl.ds(row, t), A_WIDTH:] = ob

    @pl.when(i >= prompt_tiles)
    def _():
        route_previous()
        tm = xs_ref.shape[0]
        row = pl.multiple_of(jnp.clip(i - prompt_tiles, 0, decode_tiles - 1) * tm, tm)
        project(xs_ref, mix_scr[pl.ds(row, tm), :].astype(BF16))


def _out_router(xp, a4p, bp, xs, q3, kvn3, akt, avt, bkt, bvt, cbias_a, cbias_b, sink_rows, wo, gamma, wr, br):
    tp, tsm = xp.shape[0], xs.shape[0]
    ns, ts = q3.shape[0], q3.shape[1]
    tm = 512
    npt, nst = tp // tm, tsm // tm
    nt = npt + nst
    t = tp + tsm
    sps = ns // npt
    assert sps * npt == ns and ns * ts == tsm and ns >= CACHE_BUFFERS
    pmap = lambda i: (jnp.minimum(i, npt - 1), 0)
    pmap3 = lambda i: (jnp.minimum(i, npt - 1), 0, 0)
    smap = lambda i: (jnp.clip(i - npt, 0, nst - 1), 0)
    cur = lambda i: (jnp.minimum(i, nt - 1), 0)
    const = lambda i: (0, 0)
    return pl.pallas_call(
        functools.partial(_out_router_kernel, prompt_tiles=npt, decode_tiles=nst, seqs_per_step=sps),
        grid=(nt + 1,),
        in_specs=[
            pl.BlockSpec((tm, D_MODEL), pmap),
            pl.BlockSpec((4, tm, LANES), lambda i: (0, jnp.minimum(i, npt - 1), 0)),
            pl.BlockSpec((tm, 512), pmap),
            pl.BlockSpec((tm, D_MODEL), smap),
            pl.BlockSpec((sps, ts, 1024), pmap3),
            pl.BlockSpec((sps, ts, 1280), pmap3),
            pl.BlockSpec(memory_space=pl.ANY),
            pl.BlockSpec(memory_space=pl.ANY),
            pl.BlockSpec((sps, LANES, WIN_B), pmap3),
            pl.BlockSpec((sps, LANES, WIN_B), pmap3),
            pl.BlockSpec((H_A * ts, WIN_A + LANES), const),
            pl.BlockSpec((H_B * ts, WIN_B + LANES), const),
            pl.BlockSpec((H_B * ts, 1), const),
            pl.BlockSpec((D_MODEL, D_MODEL), const),
            pl.BlockSpec((1, D_MODEL), const),
            pl.BlockSpec((3 * D_MODEL, LANES), const),
            pl.BlockSpec((1, LANES), const),
        ],
        out_specs=[
            pl.BlockSpec((tm, D_MODEL), cur),
            pl.BlockSpec((tm, D_MODEL // 2), cur),
            pl.BlockSpec((tm, LANES), lambda i: (jnp.maximum(i - 1, 0), 0)),
            pl.BlockSpec((1, LANES), const),
        ],
        scratch_shapes=[pltpu.VMEM((2, tm, 3 * D_MODEL), BF16), pltpu.VMEM((tsm, D_MODEL), F32),
                        pltpu.VMEM((CACHE_BUFFERS, A_WIDTH, WIN_A), F32),
                        pltpu.VMEM((CACHE_BUFFERS, A_WIDTH, WIN_A), F32),
                        pltpu.SemaphoreType.DMA((2, CACHE_BUFFERS))],
        out_shape=[
            jax.ShapeDtypeStruct((t, D_MODEL), F32),
            jax.ShapeDtypeStruct((t, D_MODEL // 2), jnp.int32),
            jax.ShapeDtypeStruct((t, LANES), F32),
            jax.ShapeDtypeStruct((1, LANES), F32),
        ],
        compiler_params=_cparams(("arbitrary",), vmem=OUT_ROUTER_VMEM),
        name="out_router",
    )(xp, a4p, bp, xs, q3, kvn3, akt, avt, bkt, bvt, cbias_a, cbias_b, sink_rows, wo, gamma, wr, br)


def _sc_gather_rows(table, idx):
    b = idx.shape[0]
    d = table.shape[1]
    w = SC_WINDOW
    per_worker = b // SC_WORKERS
    nwin = per_worker // w
    assert per_worker * SC_WORKERS == b and nwin * w == per_worker
    mesh = plsc.VectorSubcoreMesh(core_axis_name="c", subcore_axis_name="s")

    @functools.partial(
        pl.kernel, mesh=mesh,
        out_type=jax.ShapeDtypeStruct((b, d), table.dtype),
        scratch_types=[pltpu.VMEM((nwin, w), jnp.int32), pltpu.VMEM((2, w, d), table.dtype),
                       pltpu.SemaphoreType.DMA((2,)), pltpu.SemaphoreType.DMA((2,))],
        name="sc_gather_rows",
    )
    def gather(table_hbm, idx_hbm, out_hbm, idx_v, rows_v, sem_in, sem_out):
        wid = lax.axis_index("s") * SC_CORES + lax.axis_index("c")
        base = wid * per_worker
        pltpu.sync_copy(idx_hbm.at[wid], idx_v)

        def fetch(j):
            return pltpu.make_async_copy(table_hbm.at[idx_v.at[j]], rows_v.at[j % 2], sem_in.at[j % 2])

        def flush(j):
            return pltpu.make_async_copy(rows_v.at[j % 2], out_hbm.at[pl.ds(base + j * w, w)],
                                         sem_out.at[j % 2])

        fetch(0).start()
        for j in range(nwin):
            fetch(j).wait()
            if j + 1 < nwin:
                if j >= 1:
                    flush(j - 1).wait()
                fetch(j + 1).start()
            flush(j).start()
        for j in range(max(nwin - 2, 0), nwin):
            flush(j).wait()

    return gather(table, idx.reshape(SC_WORKERS, nwin, w))


def _sc_scatter_rows(x, dest2, nrows):
    t, d = x.shape
    w = SC_SCATTER_WINDOW
    per_worker = t // SC_WORKERS
    nwin = per_worker // w
    assert per_worker * SC_WORKERS == t and nwin * w == per_worker
    mesh = plsc.VectorSubcoreMesh(core_axis_name="c", subcore_axis_name="s")

    @functools.partial(
        pl.kernel, mesh=mesh,
        out_type=jax.ShapeDtypeStruct((nrows, d), x.dtype),
        scratch_types=[pltpu.VMEM((TOP_K, nwin, w), jnp.int32), pltpu.VMEM((2, w, d), x.dtype),
                       pltpu.SemaphoreType.DMA((2,)), pltpu.SemaphoreType.DMA((2,))],
        name="sc_scatter_rows",
    )
    def scatter(x_hbm, dest_hbm, out_hbm, idx_v, rows_v, sem_in, sem_out):
        wid = lax.axis_index("s") * SC_CORES + lax.axis_index("c")
        base = wid * per_worker
        for k in range(TOP_K):
            pltpu.sync_copy(dest_hbm.at[k, wid], idx_v.at[k])

        def fetch(j):
            return pltpu.make_async_copy(x_hbm.at[pl.ds(base + j * w, w)], rows_v.at[j % 2], sem_in.at[j % 2])

        def spread(j, k):
            return pltpu.make_async_copy(rows_v.at[j % 2], out_hbm.at[idx_v.at[k, j]], sem_out.at[j % 2])

        fetch(0).start()
        for j in range(nwin):
            fetch(j).wait()
            if j + 1 < nwin:
                if j >= 1:
                    for k in range(TOP_K):
                        spread(j - 1, k).wait()
                fetch(j + 1).start()
            for k in range(TOP_K):
                spread(j, k).start()
        for j in range(max(nwin - 2, 0), nwin):
            for k in range(TOP_K):
                spread(j, k).wait()

    return scatter(x, dest2.reshape(TOP_K, SC_WORKERS, nwin, w))


def _expert_kernel(be_ref, nu_ref, nv_ref, nx_ref, x_ref, wg_hbm, wu_hbm, wd_hbm, o_ref,
                   wg_s, wu_s, wd_s, wg_f, wu_f, wd_f, slot_s, sem):
    i = pl.program_id(0)
    used = i < nu_ref[0]
    changed = jnp.logical_or(i == 0, be_ref[i] != be_ref[jnp.maximum(i - 1, 0)])

    def weight_copies(e, slot):
        return (pltpu.make_async_copy(wg_hbm.at[e], wg_f.at[slot], sem.at[slot, 0]),
                pltpu.make_async_copy(wu_hbm.at[e], wu_f.at[slot], sem.at[slot, 1]),
                pltpu.make_async_copy(wd_hbm.at[e], wd_f.at[slot], sem.at[slot, 2]))

    @pl.when(i == 0)
    def _():
        slot_s[0] = 0
        for c in weight_copies(be_ref[0], 0):
            c.start()

    @pl.when(jnp.logical_and(used, changed))
    def _():
        slot = slot_s[0]
        for c in weight_copies(be_ref[i], slot):
            c.wait()
        wg_s[...] = wg_f[slot].astype(BF16)
        wu_s[...] = wu_f[slot].astype(BF16)
        wd_s[...] = wd_f[slot].astype(BF16)

        @pl.when(nx_ref[i] != be_ref[i])
        def _():
            for c in weight_copies(nx_ref[i], 1 - slot):
                c.start()

        slot_s[0] = 1 - slot

    @pl.when(used)
    def _():
        row = lax.broadcasted_iota(jnp.int32, x_ref.shape, 0)
        x = _unpack_bf16_pairs(jnp.where(row < nv_ref[i], x_ref[...], 0)).astype(BF16)
        gate = jnp.dot(x, wg_s[...], preferred_element_type=F32)
        up = jnp.dot(x, wu_s[...], preferred_element_type=F32)
        h = (gate * jax.nn.sigmoid(gate) * up).astype(BF16)
        o_ref[...] = _pack_bf16_pairs(jnp.dot(h, wd_s[...], preferred_element_type=F32))

    @pl.when(jnp.logical_not(used))
    def _():
        o_ref[...] = jnp.zeros_like(o_ref)


def _experts(blk_e, n_used, nvalid, next_e, xb, w_gate, w_up, w_down):
    rows = xb.shape[0]
    nblocks = rows // MOE_ROWS
    grid_spec = pltpu.PrefetchScalarGridSpec(
        num_scalar_prefetch=4,
        grid=(nblocks,),
        in_specs=[
            pl.BlockSpec((MOE_ROWS, D_MODEL // 2), lambda i, be, nu, nv, nx: (i, 0)),
            pl.BlockSpec(memory_space=pl.ANY),
            pl.BlockSpec(memory_space=pl.ANY),
            pl.BlockSpec(memory_space=pl.ANY),
        ],
        out_specs=pl.BlockSpec((MOE_ROWS, D_MODEL // 2), lambda i, be, nu, nv, nx: (i, 0)),
        scratch_shapes=[pltpu.VMEM((D_MODEL, D_EXPERT), BF16), pltpu.VMEM((D_MODEL, D_EXPERT), BF16),
                        pltpu.VMEM((D_EXPERT, D_MODEL), BF16),
                        pltpu.VMEM((2, D_MODEL, D_EXPERT), F32), pltpu.VMEM((2, D_MODEL, D_EXPERT), F32),
                        pltpu.VMEM((2, D_EXPERT, D_MODEL), F32),
                        pltpu.SMEM((1,), jnp.int32), pltpu.SemaphoreType.DMA((2, 3))],
    )
    return pl.pallas_call(
        _expert_kernel,
        grid_spec=grid_spec,
        out_shape=jax.ShapeDtypeStruct((rows, D_MODEL // 2), jnp.int32),
        compiler_params=_cparams(("arbitrary",)),
        name="experts",
    )(blk_e, n_used, nvalid, next_e, xb, w_gate, w_up, w_down)


def _combine_kernel(x1_ref, y1_ref, y2_ref, route_ref, g_ref, outp_ref, outs_ref, *, prompt_tiles):
    r = route_ref[...]
    x = (x1_ref[...] + r[:, 2:3] * _unpack_bf16_pairs(y1_ref[...])
         + r[:, 3:4] * _unpack_bf16_pairs(y2_ref[...]))
    ms = jnp.mean(x * x, axis=-1, keepdims=True)
    y = x * lax.rsqrt(ms + EPS) * g_ref[...]
    i = pl.program_id(0)

    @pl.when(i < prompt_tiles)
    def _():
        outp_ref[...] = y

    @pl.when(i >= prompt_tiles)
    def _():
        outs_ref[...] = y


def _combine_norm(x1, ygath, route, gamma, tp):
    t = x1.shape[0]
    tm = 512
    nt, npt = t // tm, tp // tm
    return pl.pallas_call(
        functools.partial(_combine_kernel, prompt_tiles=npt),
        grid=(nt,),
        in_specs=[
            pl.BlockSpec((tm, D_MODEL), lambda i: (i, 0)),
            pl.BlockSpec((tm, D_MODEL // 2), lambda i: (i, 0)),
            pl.BlockSpec((tm, D_MODEL // 2), lambda i: (i + nt, 0)),
            pl.BlockSpec((tm, LANES), lambda i: (i, 0)),
            pl.BlockSpec((1, D_MODEL), lambda i: (0, 0)),
        ],
        out_specs=[
            pl.BlockSpec((tm, D_MODEL), lambda i: (jnp.minimum(i, npt - 1), 0)),
            pl.BlockSpec((tm, D_MODEL), lambda i: (jnp.maximum(i - npt, 0), 0)),
        ],
        out_shape=[jax.ShapeDtypeStruct((tp, D_MODEL), F32), jax.ShapeDtypeStruct((t - tp, D_MODEL), F32)],
        compiler_params=_cparams(("arbitrary",)),
        name="combine_norm",
    )(x1, ygath, ygath, route, gamma)


def _band_index():
    c = (2 * QB - np.arange(2 * QB)) % (2 * QB)
    return c, c <= QB


def _bias_a_prompt(table_a):
    c, valid = _band_index()
    idx = np.stack([_t5_bucket_np(d * np.clip(QB - c, 0, QB)) for d in DILATIONS])
    return jnp.where(valid, jnp.transpose(table_a[idx], (0, 2, 1)) * LOG2E, NEG)


def _bias_b_prompt(table_b):
    c, valid = _band_index()
    valid = valid & (c >= 1)
    h = jnp.where(valid, table_b[_t5_bucket_np(np.clip(QB - c, 0, QB))].T * LOG2E, NEG)
    return jnp.transpose(h.reshape(KV_B, G_B, 2 * QB), (1, 0, 2)).reshape(H_B, 2 * QB)


def _sample_bias(table, span, t, log2_weight):
    cols = span + LANES
    period = cols + LANES
    x = np.arange(period)
    dist = np.where(x >= period - t, span - x + period, span - x)
    extra = log2_weight(dist)
    valid = np.isfinite(extra)
    u = jnp.where(valid, table[_t5_bucket_np(np.maximum(dist, 0))].T * LOG2E
                  + np.where(valid, extra, 0.0).astype(np.float32), NEG)
    rows = jnp.tile(u, (1, t))[:, :t * (period - 1)].reshape(u.shape[0], t, period - 1)[:, :, :cols]
    return rows.reshape(u.shape[0] * t, cols)


def _bias_a_sample(table_a, t):
    def log2_count(dist):
        count = np.zeros(dist.shape, np.int64)
        for w, d in zip(WINDOWS, DILATIONS):
            count += (dist >= 0) & (dist % d == 0) & (dist <= w)
        return np.where(count > 0, np.log2(np.maximum(count, 1)), -np.inf)

    return _sample_bias(table_a, WIN_A, t, log2_count)


def _bias_b_sample(table_b, t):
    return _sample_bias(table_b, WIN_B, t,
                        lambda dist: np.where((dist >= 0) & (dist < WIN_B), 0.0, -np.inf))


def _dest_kernel(route_ref, cnt_ref, tri_ref, dest_ref, meta_ref, run_scr, pst_scr):
    i = pl.program_id(0)
    tm = route_ref.shape[0]
    r = route_ref[...]
    lane = lax.broadcasted_iota(jnp.int32, (tm, LANES), 1)
    lanef = lane.astype(F32)
    oh0 = lanef == r[:, 0:1]
    oh1 = lanef == r[:, 1:2]
    ohf = jnp.concatenate([oh0, oh1], axis=0).astype(F32)

    @pl.when(i == 0)
    def _():
        cnt = jnp.broadcast_to(cnt_ref[...], (LANES, LANES))
        padded = jnp.floor((cnt + (MOE_ROWS - 1)) * (1.0 / MOE_ROWS)) * MOE_ROWS
        lane_e = lax.broadcasted_iota(jnp.int32, (LANES, LANES), 1)
        x = padded
        for sh in (1, 2, 4, 8, 16, 32, 64):
            x = x + jnp.where(lane_e >= sh, pltpu.roll(x, sh, 1), 0.0)
        pst_scr[...] = (x - padded)[0:1]
        run_scr[...] = jnp.zeros_like(run_scr)
        wide = lambda v: jnp.concatenate([v.T, v.T], axis=1)
        cnt_t, bend_t = wide(cnt), wide(x * (1.0 / MOE_ROWS))
        bstart_t = wide((x - padded) * (1.0 / MOE_ROWS))
        blk = lax.broadcasted_iota(jnp.int32, (LANES, 2 * LANES), 1).astype(F32)
        exp = lax.broadcasted_iota(jnp.int32, (LANES, 2 * LANES), 0)
        real = exp < N_EXPERTS
        blk_e = jnp.minimum(jnp.sum(jnp.where(real & (bend_t <= blk), 1.0, 0.0), axis=0, keepdims=True),
                            N_EXPERTS - 1.0)
        mine = exp.astype(F32) == blk_e
        within = blk[0:1] - jnp.sum(jnp.where(mine, bstart_t, 0.0), axis=0, keepdims=True)
        nvalid = jnp.clip(jnp.sum(jnp.where(mine, cnt_t, 0.0), axis=0, keepdims=True) - within * MOE_ROWS,
                          0.0, float(MOE_ROWS))
        n_used = jnp.max(jnp.where(real, bend_t, 0.0), axis=0, keepdims=True)
        later = real & (exp.astype(F32) > blk_e) & (cnt_t > 0.0)
        nxt = jnp.min(jnp.where(later, exp.astype(F32), float(LANES)), axis=0, keepdims=True)
        nxt = jnp.where(nxt >= N_EXPERTS, blk_e, nxt)
        meta_ref[...] = jnp.concatenate([blk_e, nvalid, n_used, nxt, jnp.zeros((4, 2 * LANES), F32)],
                                        axis=0).astype(jnp.int32)

    base = run_scr[...] + pst_scr[...] - 1.0
    vals = []
    for c in range(2 * tm // LANES):
        ohc = ohf[c * LANES:(c + 1) * LANES]
        vals.append(jnp.dot(tri_ref[...], ohc.astype(BF16), preferred_element_type=F32) + base)
        base = base + jnp.sum(ohc, axis=0, keepdims=True)
    val = jnp.concatenate(vals, axis=0)
    d0 = jnp.sum(jnp.where(oh0, val[:tm], 0.0), axis=-1, keepdims=True)
    d1 = jnp.sum(jnp.where(oh1, val[tm:], 0.0), axis=-1, keepdims=True)
    tile = jnp.where(lane == 0, d0, jnp.where(lane == 1, d1, 0.0))
    dest_ref[...] = tile.T[:8].astype(jnp.int32)
    run_scr[...] += jnp.sum(ohf, axis=0, keepdims=True)


def _dispatch(route, cnt):
    t = route.shape[0]
    tm = 512
    tri = (jnp.arange(LANES)[:, None] >= jnp.arange(LANES)[None, :]).astype(BF16)
    nblocks = -(-t * TOP_K // MOE_ROWS) + N_EXPERTS
    assert nblocks <= 2 * LANES
    dest, meta = pl.pallas_call(
        _dest_kernel,
        grid=(t // tm,),
        in_specs=[pl.BlockSpec((tm, LANES), lambda i: (i, 0)),
                  pl.BlockSpec((1, LANES), lambda i: (0, 0)),
                  pl.BlockSpec((LANES, LANES), lambda i: (0, 0))],
        out_specs=[pl.BlockSpec((8, tm), lambda i: (0, i)),
                   pl.BlockSpec((8, 2 * LANES), lambda i: (0, 0))],
        out_shape=[jax.ShapeDtypeStruct((8, t), jnp.int32), jax.ShapeDtypeStruct((8, 2 * LANES), jnp.int32)],
        scratch_shapes=[pltpu.VMEM((1, LANES), F32), pltpu.VMEM((1, LANES), F32)],
        compiler_params=_cparams(("arbitrary",)),
        name="moe_dest",
    )(route, cnt, tri)
    return dest[:TOP_K], meta[0, :nblocks], meta[2, :1], meta[1, :nblocks], meta[3, :nblocks]


def kernel(x_prompt, x_sample, cache_a_k, cache_a_v, cache_b_k, cache_b_v, rel_bias_table, attn_norm, w_in,
           w_out, attn_sinks, ffn_norm, w_router_group, b_router_group, w_router_expert, b_router_expert,
           w_gate, w_up, w_down, final_norm):
    s = x_prompt.shape[1]
    ns, ts = x_sample.shape[0], x_sample.shape[1]
    table_a = rel_bias_table[:, :H_A]
    table_b = rel_bias_table[:, H_A:]

    w = w_in[0]
    wqa, wka, wva, wqb, wkb, wvb = (w[:, 0:512], w[:, 512:1024], w[:, 1024:1536], w[:, 1536:2048],
                                    w[:, 2048:2176], w[:, 2176:2304])
    wqb = jnp.transpose(wqb.reshape(D_MODEL, KV_B, G_B, HEAD_DIM), (0, 2, 1, 3)).reshape(D_MODEL, 512)
    wp = jnp.concatenate([wka, wva, wqa, wqb, wkb, wvb], axis=1).astype(BF16)
    cscale = jnp.concatenate([jnp.ones((1, 1024), F32), jnp.full((1, 1024), SCALE * LOG2E, F32),
                              jnp.ones((1, 256), F32)], axis=1)
    wo = w_out[0]
    wo_b = jnp.transpose(wo[512:].reshape(KV_B, G_B, HEAD_DIM, D_MODEL), (1, 0, 2, 3)).reshape(512, D_MODEL)
    wo_p = jnp.concatenate([wo[:512], wo_b], axis=0).astype(BF16)
    wr = jnp.concatenate([w_router_group[0],
                          jnp.transpose(w_router_expert[0], (1, 0, 2)).reshape(D_MODEL, N_EXPERTS),
                          jnp.zeros((D_MODEL, LANES - N_GROUPS - N_EXPERTS), F32)], axis=1)
    wr_hi = wr.astype(BF16)
    wr = jnp.concatenate([wr_hi, wr_hi, (wr - wr_hi.astype(F32)).astype(BF16)], axis=0)
    br = jnp.concatenate([b_router_group[0], b_router_expert[0].reshape(N_EXPERTS),
                          jnp.zeros((LANES - N_GROUPS - N_EXPERTS,), F32)]).reshape(1, LANES)
    sinks2 = attn_sinks[0] * LOG2E
    sinks_gk = jnp.transpose(sinks2.reshape(KV_B, G_B), (1, 0)).reshape(H_B)
    sink_rows_p = jnp.repeat(sinks_gk, QB).reshape(G_B, 1, 2 * QB)
    sink_rows_s = jnp.repeat(sinks2, ts).reshape(H_B * ts, 1)
    emat = jnp.tile(jnp.arange(LANES)[:, None] == (jnp.arange(A_WIDTH)[None, :] // HEAD_DIM),
                    (3, 1)).astype(BF16)
    attn_g = attn_norm[0].reshape(1, D_MODEL)
    ffn_g = ffn_norm[0].reshape(1, D_MODEL)

    xp = x_prompt.reshape(s, D_MODEL)
    aperm, qb_p, kvb_p, akv32, bkv32 = _proj_prompt(xp, attn_g, wp, cscale)
    a4 = _attn_a_prompt(aperm, _bias_a_prompt(table_a), emat)
    ob_p = _attn_b_prompt(qb_p, kvb_p, _bias_b_prompt(table_b), sink_rows_p)

    xs = x_sample.reshape(ns * ts, D_MODEL)
    q_s, kv_s = _proj_sample(xs, attn_g, wp, cscale)
    akt = jnp.transpose(cache_a_k[0], (0, 2, 3, 1)).reshape(ns, A_WIDTH, WIN_A)
    avt = jnp.transpose(cache_a_v[0], (0, 2, 3, 1)).reshape(ns, A_WIDTH, WIN_A)
    bkt = jnp.transpose(cache_b_k[0], (0, 2, 3, 1)).reshape(ns, LANES, WIN_B)
    bvt = jnp.transpose(cache_b_v[0], (0, 2, 3, 1)).reshape(ns, LANES, WIN_B)

    x1, xn, route, cnt = _out_router(xp, a4, ob_p, xs, q_s.reshape(ns, ts, 1024), kv_s.reshape(ns, ts, 1280),
                                     akt, avt, bkt, bvt, _bias_a_sample(table_a, ts),
                                     _bias_b_sample(table_b, ts), sink_rows_s, wo_p, ffn_g, wr, br)
    dest2, blk_e, n_used, nvalid, next_e = _dispatch(route, cnt)
    xb = _sc_scatter_rows(xn, dest2, blk_e.shape[0] * MOE_ROWS)
    yb = _experts(blk_e, n_used, nvalid, next_e, xb, w_gate[0], w_up[0], w_down[0])
    y_p, y_s = _combine_norm(x1, _sc_gather_rows(yb, dest2.reshape(-1)), route, final_norm.reshape(1, D_MODEL), s)

    y_prompt = y_p.reshape(1, s, D_MODEL)
    y_sample = y_s.reshape(ns, ts, D_MODEL)
    keep_a, keep_b = min(WIN_A, s), min(WIN_B, s)
    pak = akv32[SPAN - keep_a:, :512].reshape(1, 1, keep_a, H_A, HEAD_DIM)
    pav = akv32[SPAN - keep_a:, 512:].reshape(1, 1, keep_a, H_A, HEAD_DIM)
    pbk = bkv32[SPAN - keep_b:, :128].reshape(1, 1, keep_b, KV_B, HEAD_DIM)
    pbv = bkv32[SPAN - keep_b:, 128:].reshape(1, 1, keep_b, KV_B, HEAD_DIM)
    sak = kv_s[:, 0:512].reshape(1, ns, ts, H_A, HEAD_DIM)
    sav = kv_s[:, 512:1024].reshape(1, ns, ts, H_A, HEAD_DIM)
    sbk = kv_s[:, 1024:1152].reshape(1, ns, ts, KV_B, HEAD_DIM)
    sbv = kv_s[:, 1152:1280].reshape(1, ns, ts, KV_B, HEAD_DIM)
    return (y_prompt, y_sample, pak, pav, pbk, pbv, sak, sav, sbk, sbv)
```

```python
import functools
import math

import jax
import jax.numpy as jnp
import numpy as np
from jax import lax
from jax.experimental import pallas as pl
from jax.experimental.pallas import tpu as pltpu
from jax.experimental.pallas import tpu_sc as plsc

D_MODEL = 1024
HEAD_DIM = 64
H_A = 8
H_B = 8
KV_B = 2
G_B = 4
DILATIONS = (1, 4, 16)
WINDOWS = (128, 512, 2048)
WIN_A = 2048
WIN_B = 128
NUM_BUCKETS = 32
MAX_DISTANCE = 2048
N_GROUPS = 4
EXPERTS_PER_GROUP = 8
N_EXPERTS = 32
TOP_K = 2
D_EXPERT = 512
EPS = 1e-5
SCALE = HEAD_DIM ** -0.5

LANES = 128
SPAN = 2048
QB = 128
NCHUNK = 9
A_WIDTH = H_A * HEAD_DIM
MOE_ROWS = 512
SC_CORES = 2
SC_SUBCORES = 16
SC_WORKERS = SC_CORES * SC_SUBCORES
SC_WINDOW = 64
SC_SCATTER_WINDOW = 32
NEG = -1e30
LOG2E = math.log2(math.e)
B_STEP = 1024
V7X_VMEM_BYTES = 64 * 1024 * 1024
VMEM_LIMIT = V7X_VMEM_BYTES - 8 * 1024 * 1024
CACHE_BUFFERS = 3
OUT_ROUTER_VMEM = V7X_VMEM_BYTES - 4 * 1024 * 1024

F32 = jnp.float32
BF16 = jnp.bfloat16


def _t5_bucket_np(dist):
    dist = np.asarray(dist, np.int64)
    max_exact = NUM_BUCKETS // 2
    d = np.maximum(dist, 1).astype(np.float32)
    ratio = np.log(d / np.float32(max_exact)) / np.float32(math.log(MAX_DISTANCE / max_exact))
    large = max_exact + (ratio * np.float32(NUM_BUCKETS - max_exact)).astype(np.int32)
    large = np.minimum(large, NUM_BUCKETS - 1)
    return np.where(dist < max_exact, dist, large).astype(np.int32)


def _cparams(sem, vmem=VMEM_LIMIT):
    return pltpu.CompilerParams(dimension_semantics=sem, vmem_limit_bytes=vmem)


def _proj_prompt_kernel(x_ref, g_ref, w_ref, cs_ref, aperm_ref, qb_ref, kvb_ref, akv_ref, bkv_ref,
                        h_scr, p_scr):
    n = pl.program_id(1)

    @pl.when(n == 0)
    def _():
        x = x_ref[...]
        ms = jnp.mean(x * x, axis=-1, keepdims=True)
        h_scr[...] = (x * lax.rsqrt(ms + EPS) * g_ref[...]).astype(BF16)

    p = jnp.dot(h_scr[...], w_ref[...], preferred_element_type=F32) * cs_ref[...]

    @pl.when(n < 6)
    def _():
        aperm_ref[0] = p.astype(BF16)
        p_scr[0, 0] = p[:, :LANES]
        p_scr[0, 1] = p[:, LANES:]
        quarter = SPAN // 4
        for r in range(4):
            lo = p_scr[0, 0, pl.ds(r, quarter, stride=4), :]
            hi = p_scr[0, 1, pl.ds(r, quarter, stride=4), :]
            p_scr[1, 0, r * quarter:(r + 1) * quarter, :] = lo
            p_scr[1, 1, r * quarter:(r + 1) * quarter, :] = hi
            aperm_ref[1, r * quarter:(r + 1) * quarter, :] = jnp.concatenate([lo, hi], axis=1).astype(BF16)
        for r16 in range(16):
            start = (r16 % 4) * quarter + r16 // 4
            t = jnp.concatenate([p_scr[1, 0, pl.ds(start, QB, stride=4), :],
                                 p_scr[1, 1, pl.ds(start, QB, stride=4), :]], axis=1)
            aperm_ref[2, r16 * QB:(r16 + 1) * QB, :] = t.astype(BF16)

    @pl.when(n < 4)
    def _():
        akv_ref[...] = p

    @pl.when(jnp.logical_or(n == 6, n == 7))
    def _():
        qb_ref[...] = p.astype(BF16)

    @pl.when(n == 8)
    def _():
        kvb_ref[...] = p.astype(BF16)
        bkv_ref[...] = p


def _proj_prompt(x, gamma, w, cscale):
    s = x.shape[0]
    nspan = s // SPAN
    return pl.pallas_call(
        _proj_prompt_kernel,
        grid=(nspan, NCHUNK),
        in_specs=[
            pl.BlockSpec((SPAN, D_MODEL), lambda b, n: (b, 0)),
            pl.BlockSpec((1, D_MODEL), lambda b, n: (0, 0)),
            pl.BlockSpec((D_MODEL, 256), lambda b, n: (0, n)),
            pl.BlockSpec((1, 256), lambda b, n: (0, n)),
        ],
        out_specs=[
            pl.BlockSpec((3, SPAN, 256), lambda b, n: (0, b, jnp.minimum(n, 5))),
            pl.BlockSpec((SPAN, 256), lambda b, n: (b, jnp.clip(n - 6, 0, 1))),
            pl.BlockSpec((SPAN, 256), lambda b, n: (b, 0)),
            pl.BlockSpec((SPAN, 256), lambda b, n: (0, jnp.where(b == nspan - 1, jnp.minimum(n, 3), 0))),
            pl.BlockSpec((SPAN, 256), lambda b, n: (0, 0)),
        ],
        out_shape=[
            jax.ShapeDtypeStruct((3, s, 3 * A_WIDTH), BF16),
            jax.ShapeDtypeStruct((s, 512), BF16),
            jax.ShapeDtypeStruct((s, 256), BF16),
            jax.ShapeDtypeStruct((SPAN, 1024), F32),
            jax.ShapeDtypeStruct((SPAN, 256), F32),
        ],
        scratch_shapes=[pltpu.VMEM((SPAN, D_MODEL), BF16), pltpu.VMEM((2, 2, SPAN, LANES), F32)],
        compiler_params=_cparams(("arbitrary", "arbitrary")),
        name="proj_prompt",
    )(x, gamma, w, cscale)


def _proj_sample_kernel(x_ref, g_ref, w_ref, cs_ref, q_ref, kv_ref):
    x = x_ref[...]
    ms = jnp.mean(x * x, axis=-1, keepdims=True)
    h = (x * lax.rsqrt(ms + EPS) * g_ref[...]).astype(BF16)
    p = jnp.dot(h, w_ref[...], preferred_element_type=F32) * cs_ref[...]
    kv_ref[:, :1024] = p[:, :1024]
    kv_ref[:, 1024:] = p[:, 2048:]
    q_ref[...] = p[:, 1024:2048]


def _proj_sample(x, gamma, w, cscale):
    t = x.shape[0]
    tm = 512
    return pl.pallas_call(
        _proj_sample_kernel,
        grid=(t // tm,),
        in_specs=[
            pl.BlockSpec((tm, D_MODEL), lambda i: (i, 0)),
            pl.BlockSpec((1, D_MODEL), lambda i: (0, 0)),
            pl.BlockSpec((D_MODEL, 2304), lambda i: (0, 0)),
            pl.BlockSpec((1, 2304), lambda i: (0, 0)),
        ],
        out_specs=[
            pl.BlockSpec((tm, 1024), lambda i: (i, 0)),
            pl.BlockSpec((tm, 1280), lambda i: (i, 0)),
        ],
        out_shape=[
            jax.ShapeDtypeStruct((t, 1024), F32),
            jax.ShapeDtypeStruct((t, 1280), F32),
        ],
        compiler_params=_cparams(("arbitrary",)),
        name="proj_sample",
    )(x, gamma, w, cscale)


def _spread_heads(w, e3_ref):
    hi = w.astype(BF16)
    r1 = w - hi.astype(F32)
    mid = r1.astype(BF16)
    low = (r1 - mid.astype(F32)).astype(BF16)
    return jnp.dot(jnp.concatenate([hi, mid, low], axis=1), e3_ref[...], preferred_element_type=F32)


def _pair_tile(q2, kk, vv, bias_t, lo, sink=None):
    zero = jnp.zeros_like(q2)
    qq = jnp.concatenate([jnp.where(lo, q2, zero), jnp.where(lo, zero, q2)], axis=0)
    st = lax.dot_general(kk, qq, (((1,), (1,)), ((), ())), preferred_element_type=F32)
    st = st + bias_t
    m = jnp.max(st, axis=0, keepdims=True)
    if sink is not None:
        m = jnp.maximum(m, sink)
    p = jnp.exp2(st - m)
    den = jnp.sum(p, axis=0, keepdims=True)
    if sink is not None:
        den = den + jnp.exp2(sink - m)
    pn = (p * (1.0 / den)).astype(BF16)
    o = lax.dot_general(pn, vv, (((0,), (0,)), ((), ())), preferred_element_type=F32)
    return jnp.where(lo, o[:QB], o[QB:]), m + jnp.log2(den)


def _fill_band_tiles(h_ref, bias_scr):
    nk = 2 * QB
    prev = lax.broadcasted_iota(jnp.int32, (nk, nk), 0) < QB
    for pair in range(h_ref.shape[0] // 2):
        halves = []
        for hh in range(2):
            row = h_ref[2 * pair + hh:2 * pair + hh + 1, :]
            band = pltpu.roll(jnp.broadcast_to(row, (nk, nk)), 0, 1, stride=1, stride_axis=0)
            halves.append(band[:, :QB])
        tile = jnp.concatenate(halves, axis=1)
        bias_scr[0, pair] = tile
        bias_scr[1, pair] = jnp.where(prev, NEG, tile)


def _attn_a_kernel(q_ref, kvc_ref, kvp_ref, h_ref, e_ref, out_ref, o_scr, st_scr, bias_scr):
    b = pl.program_id(0)
    g = pl.program_id(1)
    nblk = jnp.where(g == 0, 16, jnp.where(g == 1, 4, 1))
    lane = lax.broadcasted_iota(jnp.int32, (QB, LANES), 1)
    lo = lane < HEAD_DIM

    @pl.when(b == 0)
    def _():
        _fill_band_tiles(h_ref, bias_scr.at[g])

    bias_ref = bias_scr.at[g]

    for cb in range(SPAN // QB):
        first = lax.rem(jnp.int32(cb), nblk) == 0
        rows = slice(cb * QB, (cb + 1) * QB)
        prow_c = max(cb - 1, 0) * QB
        prow_p = pl.multiple_of(jnp.where(first, cb + nblk - 1, 0) * QB, QB)
        variant = jnp.logical_and(first, b == 0).astype(jnp.int32)
        stats = []
        for hp in range(4):
            ks = slice(hp * LANES, (hp + 1) * LANES)
            vs = slice(A_WIDTH + hp * LANES, A_WIDTH + (hp + 1) * LANES)
            kp = jnp.where(first, kvp_ref[pl.ds(prow_p, QB), ks], kvc_ref[prow_c:prow_c + QB, ks])
            vp = jnp.where(first, kvp_ref[pl.ds(prow_p, QB), vs], kvc_ref[prow_c:prow_c + QB, vs])
            kk = jnp.concatenate([kp, kvc_ref[rows, ks]], axis=0)
            vv = jnp.concatenate([vp, kvc_ref[rows, vs]], axis=0)
            o, lse = _pair_tile(q_ref[rows, ks], kk, vv, bias_ref[variant, hp], lo)
            o_scr[g, hp, rows, :] = o
            stats += [lse[:, :QB], lse[:, QB:]]
        sm = jnp.concatenate(stats + [jnp.zeros((LANES - H_A, QB), F32)], axis=0)
        st_scr[g, rows, :] = sm.T

    @pl.when(g == 2)
    def _():
        def merge(c, carry):
            r2 = lax.rem(c, 4) * (SPAN // 4) + c // 4
            r3 = pl.multiple_of(c * QB, QB)
            l1 = st_scr[0, pl.ds(c, QB, stride=16), :]
            l2 = st_scr[1, pl.ds(r2, QB, stride=4), :]
            l3 = st_scr[2, pl.ds(r3, QB), :]
            mx = jnp.maximum(jnp.maximum(l1, l2), l3)
            w1 = jnp.exp2(l1 - mx)
            w2 = jnp.exp2(l2 - mx)
            w3 = jnp.exp2(l3 - mx)
            tot = w1 + w2 + w3
            a1 = _spread_heads(w1 / tot, e_ref)
            a2 = _spread_heads(w2 / tot, e_ref)
            a3 = _spread_heads(w3 / tot, e_ref)
            for hp in range(4):
                sl = slice(hp * LANES, (hp + 1) * LANES)
                o1 = o_scr[0, hp, pl.ds(c, QB, stride=16), :]
                o2 = o_scr[1, hp, pl.ds(r2, QB, stride=4), :]
                o3 = o_scr[2, hp, pl.ds(r3, QB), :]
                out_ref[hp, pl.ds(c, QB, stride=16), :] = a1[:, sl] * o1 + a2[:, sl] * o2 + a3[:, sl] * o3
            return carry

        lax.fori_loop(0, 16, merge, 0, unroll=4)


def _attn_a_prompt(aperm, bias_a, emat):
    s = aperm.shape[1]
    nspan = s // SPAN
    return pl.pallas_call(
        _attn_a_kernel,
        grid=(nspan, 3),
        in_specs=[
            pl.BlockSpec((None, SPAN, A_WIDTH), lambda b, g: (g, b, 2)),
            pl.BlockSpec((None, SPAN, 2 * A_WIDTH), lambda b, g: (g, b, 0)),
            pl.BlockSpec((None, SPAN, 2 * A_WIDTH), lambda b, g: (g, jnp.maximum(b - 1, 0), 0)),
            pl.BlockSpec((None, H_A, 2 * QB), lambda b, g: (g, 0, 0)),
            pl.BlockSpec((3 * LANES, A_WIDTH), lambda b, g: (0, 0)),
        ],
        out_specs=pl.BlockSpec((4, SPAN, LANES), lambda b, g: (0, b, 0)),
        out_shape=jax.ShapeDtypeStruct((4, s, LANES), F32),
        scratch_shapes=[pltpu.VMEM((3, 4, SPAN, LANES), F32), pltpu.VMEM((3, SPAN, LANES), F32),
                        pltpu.VMEM((3, 2, 4, 2 * QB, 2 * QB), F32)],
        compiler_params=_cparams(("arbitrary", "arbitrary")),
        name="attn_a_prompt",
    )(aperm, aperm, aperm, bias_a, emat)


def _attn_b_kernel(q_ref, kvc_ref, kvp_ref, h_ref, sink_ref, out_ref, bias_ref):
    i = pl.program_id(0)
    lane = lax.broadcasted_iota(jnp.int32, (QB, LANES), 1)
    lo = lane < HEAD_DIM

    @pl.when(i == 0)
    def _():
        _fill_band_tiles(h_ref, bias_ref)

    variant = (i == 0).astype(jnp.int32)
    for j in range(B_STEP // QB):
        rows = slice(j * QB, (j + 1) * QB)
        if j == 0:
            kp, vp = kvp_ref[:, :LANES], kvp_ref[:, LANES:]
        else:
            kp, vp = kvc_ref[(j - 1) * QB:j * QB, :LANES], kvc_ref[(j - 1) * QB:j * QB, LANES:]
        kk = jnp.concatenate([kp, kvc_ref[rows, :LANES]], axis=0)
        vv = jnp.concatenate([vp, kvc_ref[rows, LANES:]], axis=0)
        for g in range(G_B):
            bias_t = bias_ref[variant, g] if j == 0 else bias_ref[0, g]
            o, _ = _pair_tile(q_ref[rows, g * LANES:(g + 1) * LANES], kk, vv, bias_t, lo, sink=sink_ref[g])
            out_ref[rows, g * LANES:(g + 1) * LANES] = o.astype(BF16)


def _attn_b_prompt(qb, kvb, bias_b, sink_rows):
    s = qb.shape[0]
    per = B_STEP // QB
    return pl.pallas_call(
        _attn_b_kernel,
        grid=(s // B_STEP,),
        in_specs=[
            pl.BlockSpec((B_STEP, 512), lambda i: (i, 0)),
            pl.BlockSpec((B_STEP, 256), lambda i: (i, 0)),
            pl.BlockSpec((QB, 256), lambda i: (jnp.maximum(i * per - 1, 0), 0)),
            pl.BlockSpec((H_B, 2 * QB), lambda i: (0, 0)),
            pl.BlockSpec((G_B, 1, 2 * QB), lambda i: (0, 0, 0)),
        ],
        out_specs=pl.BlockSpec((B_STEP, 512), lambda i: (i, 0)),
        out_shape=jax.ShapeDtypeStruct((s, 512), BF16),
        scratch_shapes=[pltpu.VMEM((2, G_B, 2 * QB, 2 * QB), F32)],
        compiler_params=_cparams(("arbitrary",)),
        name="attn_b_prompt",
    )(qb, kvb, kvb, bias_b, sink_rows)


def _sample_attention(q, kvn, akt, avt, bkt, bvt, cba, cbb, sink):
    t = q.shape[0]
    kvn_p = jnp.concatenate([kvn, jnp.zeros((LANES - t, kvn.shape[1]), F32)], axis=0).astype(BF16)
    lane_a = lax.broadcasted_iota(jnp.int32, (t, A_WIDTH), 1) // HEAD_DIM

    qa = q[:, :A_WIDTH]
    qbd = jnp.concatenate([jnp.where(lane_a == h, qa, 0.0) for h in range(H_A)], axis=0).astype(BF16)
    s_c = jnp.dot(qbd, akt.astype(BF16), preferred_element_type=F32)
    s_n = lax.dot_general(qbd, kvn_p[:, :A_WIDTH], (((1,), (1,)), ((), ())), preferred_element_type=F32)
    s = jnp.concatenate([s_c, s_n], axis=1) + cba
    m = jnp.max(s, axis=-1, keepdims=True)
    p = jnp.exp2(s - m)
    l = jnp.sum(p, axis=-1, keepdims=True)
    pb = p.astype(BF16)
    o_n = jnp.dot(pb[:, WIN_A:], kvn_p[:, A_WIDTH:2 * A_WIDTH], preferred_element_type=F32)
    o_all = lax.dot_general(pb[:, :WIN_A], avt.astype(BF16), (((1,), (1,)), ((), ())),
                            preferred_element_type=F32) + o_n
    o_sel = jnp.zeros((t, A_WIDTH), F32)
    l_b = jnp.ones((t, A_WIDTH), F32)
    for h in range(H_A):
        sel = lane_a == h
        o_sel = jnp.where(sel, o_all[h * t:(h + 1) * t], o_sel)
        l_b = jnp.where(sel, l[h * t:(h + 1) * t], l_b)
    oa = o_sel / l_b

    lane_b = lax.broadcasted_iota(jnp.int32, (G_B * t, LANES), 1)
    lo = lane_b < HEAD_DIM
    qb2 = jnp.concatenate([q[:, A_WIDTH + g * LANES:A_WIDTH + (g + 1) * LANES] for g in range(G_B)], axis=0)
    qm = jnp.concatenate([jnp.where(lo, qb2, 0.0), jnp.where(lo, 0.0, qb2)], axis=0).astype(BF16)
    kb_n = kvn_p[:, 2 * A_WIDTH:2 * A_WIDTH + LANES]
    vb_n = kvn_p[:, 2 * A_WIDTH + LANES:]
    sb_c = jnp.dot(qm, bkt.astype(BF16), preferred_element_type=F32)
    sb_n = lax.dot_general(qm, kb_n, (((1,), (1,)), ((), ())), preferred_element_type=F32)
    sb = jnp.concatenate([sb_c, sb_n], axis=1) + cbb
    mb = jnp.maximum(jnp.max(sb, axis=-1, keepdims=True), sink)
    pbb = jnp.exp2(sb - mb)
    den = jnp.sum(pbb, axis=-1, keepdims=True) + jnp.exp2(sink - mb)
    pbb = pbb.astype(BF16)
    ob = lax.dot_general(pbb[:, :WIN_B], bvt.astype(BF16), (((1,), (1,)), ((), ())),
                         preferred_element_type=F32)
    ob = (ob + jnp.dot(pbb[:, WIN_B:], vb_n, preferred_element_type=F32)) / den
    half = G_B * t
    lo8 = lo[:t]
    ob = jnp.concatenate([jnp.where(lo8, ob[g * t:(g + 1) * t], ob[half + g * t:half + (g + 1) * t])
                          for g in range(G_B)], axis=1)
    return oa, ob


def _route(logits):
    lane = lax.broadcasted_iota(jnp.int32, logits.shape, 1).astype(F32)
    big = jnp.float32(1 << 20)
    ninf = jnp.float32(-jnp.inf)
    gmask = lane < N_GROUPS
    lg = jnp.where(gmask, logits, ninf)
    gmax = jnp.max(lg, axis=-1, keepdims=True)
    grp = jnp.min(jnp.where(lg == gmax, lane, big), axis=-1, keepdims=True)
    pg_top = 1.0 / jnp.sum(jnp.exp(lg - gmax), axis=-1, keepdims=True)
    e0 = N_GROUPS + grp * EXPERTS_PER_GROUP
    emask = jnp.logical_and(lane >= e0, lane < e0 + EXPERTS_PER_GROUP)
    le = jnp.where(emask, logits, ninf)
    emax = jnp.max(le, axis=-1, keepdims=True)
    esum = jnp.sum(jnp.exp(le - emax), axis=-1, keepdims=True)
    i1 = jnp.min(jnp.where(le == emax, lane, big), axis=-1, keepdims=True)
    le2 = jnp.where(lane == i1, ninf, le)
    e2max = jnp.max(le2, axis=-1, keepdims=True)
    i2 = jnp.min(jnp.where(le2 == e2max, lane, big), axis=-1, keepdims=True)
    p1 = 1.0 / esum
    p2 = jnp.exp(e2max - emax) / esum
    g1 = pg_top * p1 / (p1 + p2)
    g2 = pg_top * p2 / (p1 + p2)
    out = jnp.where(lane == 0, i1 - N_GROUPS, 0.0)
    out = jnp.where(lane == 1, i2 - N_GROUPS, out)
    out = jnp.where(lane == 2, g1, out)
    out = jnp.where(lane == 3, g2, out)
    return out


def _pack_bf16_pairs(x):
    half = x.shape[1] // 2

    def rne(v):
        bits = lax.bitcast_convert_type(v, jnp.int32)
        return bits + 0x7FFF + (lax.shift_right_logical(bits, 16) & 1)

    lo = lax.shift_right_logical(rne(x[:, :half]), 16)
    hi = rne(x[:, half:]) & jnp.int32(-65536)
    return lo | hi


def _unpack_bf16_pairs(w):
    lo = lax.bitcast_convert_type(lax.shift_left(w, 16), F32)
    hi = lax.bitcast_convert_type(w & jnp.int32(-65536), F32)
    return jnp.concatenate([lo, hi], axis=1)


def _out_router_kernel(xp_ref, ap_ref, bp_ref, xs_ref, q_ref, kvn_ref, akt_hbm, avt_hbm, bkt_ref, bvt_ref,
                       cba_ref, cbb_ref, sink_ref, wo_ref, g_ref, wr_ref, br_ref,
                       x1_ref, xn_ref, route_ref, cnt_ref,
                       xcat_scr, mix_scr, kbuf, vbuf, sem, *, prompt_tiles, decode_tiles, seqs_per_step):
    i = pl.program_id(0)
    seqs = prompt_tiles * seqs_per_step

    def cache_copies(n, slot):
        return (pltpu.make_async_copy(akt_hbm.at[n], kbuf.at[slot], sem.at[0, slot]),
                pltpu.make_async_copy(avt_hbm.at[n], vbuf.at[slot], sem.at[1, slot]))

    @pl.when(i == 0)
    def _():
        cnt_ref[...] = jnp.zeros_like(cnt_ref)
        xcat_scr[...] = jnp.zeros_like(xcat_scr)
        for n0 in range(2):
            for c in cache_copies(n0, n0):
                c.start()

    @pl.when(i == prompt_tiles)
    def _():
        for n1 in (seqs, seqs + 1):
            for c in cache_copies(seqs - 1, n1 % CACHE_BUFFERS):
                c.wait()

    pslot = lax.rem(i, 2)

    def route_previous():
        logits = jnp.dot(xcat_scr[1 - pslot], wr_ref[...], preferred_element_type=F32)
        route = _route(logits + br_ref[...])
        route_ref[...] = route
        lanef = lax.broadcasted_iota(jnp.int32, route.shape, 1).astype(F32)
        hits = (lanef == route[:, 0:1]).astype(F32) + (lanef == route[:, 1:2]).astype(F32)
        cnt_ref[...] += jnp.sum(hits, axis=0, keepdims=True) * (i > 0).astype(F32)

    def project(x_ref, mix):
        x1 = x_ref[...] + jnp.dot(mix, wo_ref[...], preferred_element_type=F32)
        x1_ref[...] = x1
        ms = jnp.mean(x1 * x1, axis=-1, keepdims=True)
        xn = x1 * lax.rsqrt(ms + EPS) * g_ref[...]
        xn_ref[...] = _pack_bf16_pairs(xn)
        xh = xn.astype(BF16)
        xl = (xn - xh.astype(F32)).astype(BF16)
        xcat_scr[pslot] = jnp.concatenate([xh, xl, xh], axis=1)

    @pl.when(i < prompt_tiles)
    def _():
        route_previous()
        mix = jnp.concatenate([ap_ref[0], ap_ref[1], ap_ref[2], ap_ref[3]], axis=1).astype(BF16)
        project(xp_ref, jnp.concatenate([mix, bp_ref[...]], axis=1))
        t = q_ref.shape[1]
        for s in range(seqs_per_step):
            n = i * seqs_per_step + s
            slot = lax.rem(n, CACHE_BUFFERS)
            for c in cache_copies(n, slot):
                c.wait()
            for c in cache_copies(jnp.minimum(n + 2, seqs - 1), lax.rem(n + 2, CACHE_BUFFERS)):
                c.start()
            oa, ob = _sample_attention(q_ref[s], kvn_ref[s], kbuf[slot], vbuf[slot], bkt_ref[s], bvt_ref[s],
                                       cba_ref[...], cbb_ref[...], sink_ref[...])
            row = pl.multiple_of(n * t, t)
            mix_scr[pl.ds(row, t), :A_WIDTH] = oa
            mix_scr[pl.ds(row, t), A_WIDTH:] = ob

    @pl.when(i >= prompt_tiles)
    def _():
        route_previous()
        tm = xs_ref.shape[0]
        row = pl.multiple_of(jnp.clip(i - prompt_tiles, 0, decode_tiles - 1) * tm, tm)
        project(xs_ref, mix_scr[pl.ds(row, tm), :].astype(BF16))


def _out_router(xp, a4p, bp, xs, q3, kvn3, akt, avt, bkt, bvt, cbias_a, cbias_b, sink_rows, wo, gamma, wr, br):
    tp, tsm = xp.shape[0], xs.shape[0]
    ns, ts = q3.shape[0], q3.shape[1]
    tm = 512
    npt, nst = tp // tm, tsm // tm
    nt = npt + nst
    t = tp + tsm
    sps = ns // npt
    assert sps * npt == ns and ns * ts == tsm and ns >= CACHE_BUFFERS
    pmap = lambda i: (jnp.minimum(i, npt - 1), 0)
    pmap3 = lambda i: (jnp.minimum(i, npt - 1), 0, 0)
    smap = lambda i: (jnp.clip(i - npt, 0, nst - 1), 0)
    cur = lambda i: (jnp.minimum(i, nt - 1), 0)
    const = lambda i: (0, 0)
    return pl.pallas_call(
        functools.partial(_out_router_kernel, prompt_tiles=npt, decode_tiles=nst, seqs_per_step=sps),
        grid=(nt + 1,),
        in_specs=[
            pl.BlockSpec((tm, D_MODEL), pmap),
            pl.BlockSpec((4, tm, LANES), lambda i: (0, jnp.minimum(i, npt - 1), 0)),
            pl.BlockSpec((tm, 512), pmap),
            pl.BlockSpec((tm, D_MODEL), smap),
            pl.BlockSpec((sps, ts, 1024), pmap3),
            pl.BlockSpec((sps, ts, 1280), pmap3),
            pl.BlockSpec(memory_space=pl.ANY),
            pl.BlockSpec(memory_space=pl.ANY),
            pl.BlockSpec((sps, LANES, WIN_B), pmap3),
            pl.BlockSpec((sps, LANES, WIN_B), pmap3),
            pl.BlockSpec((H_A * ts, WIN_A + LANES), const),
            pl.BlockSpec((H_B * ts, WIN_B + LANES), const),
            pl.BlockSpec((H_B * ts, 1), const),
            pl.BlockSpec((D_MODEL, D_MODEL), const),
            pl.BlockSpec((1, D_MODEL), const),
            pl.BlockSpec((3 * D_MODEL, LANES), const),
            pl.BlockSpec((1, LANES), const),
        ],
        out_specs=[
            pl.BlockSpec((tm, D_MODEL), cur),
            pl.BlockSpec((tm, D_MODEL // 2), cur),
            pl.BlockSpec((tm, LANES), lambda i: (jnp.maximum(i - 1, 0), 0)),
            pl.BlockSpec((1, LANES), const),
        ],
        scratch_shapes=[pltpu.VMEM((2, tm, 3 * D_MODEL), BF16), pltpu.VMEM((tsm, D_MODEL), F32),
                        pltpu.VMEM((CACHE_BUFFERS, A_WIDTH, WIN_A), F32),
                        pltpu.VMEM((CACHE_BUFFERS, A_WIDTH, WIN_A), F32),
                        pltpu.SemaphoreType.DMA((2, CACHE_BUFFERS))],
        out_shape=[
            jax.ShapeDtypeStruct((t, D_MODEL), F32),
            jax.ShapeDtypeStruct((t, D_MODEL // 2), jnp.int32),
            jax.ShapeDtypeStruct((t, LANES), F32),
            jax.ShapeDtypeStruct((1, LANES), F32),
        ],
        compiler_params=_cparams(("arbitrary",), vmem=OUT_ROUTER_VMEM),
        name="out_router",
    )(xp, a4p, bp, xs, q3, kvn3, akt, avt, bkt, bvt, cbias_a, cbias_b, sink_rows, wo, gamma, wr, br)


def _sc_gather_rows(table, idx):
    b = idx.shape[0]
    d = table.shape[1]
    w = SC_WINDOW
    per_worker = b // SC_WORKERS
    nwin = per_worker // w
    assert per_worker * SC_WORKERS == b and nwin * w == per_worker
    mesh = plsc.VectorSubcoreMesh(core_axis_name="c", subcore_axis_name="s")

    @functools.partial(
        pl.kernel, mesh=mesh,
        out_type=jax.ShapeDtypeStruct((b, d), table.dtype),
        scratch_types=[pltpu.VMEM((nwin, w), jnp.int32), pltpu.VMEM((2, w, d), table.dtype),
                       pltpu.SemaphoreType.DMA((2,)), pltpu.SemaphoreType.DMA((2,))],
        name="sc_gather_rows",
    )
    def gather(table_hbm, idx_hbm, out_hbm, idx_v, rows_v, sem_in, sem_out):
        wid = lax.axis_index("s") * SC_CORES + lax.axis_index("c")
        base = wid * per_worker
        pltpu.sync_copy(idx_hbm.at[wid], idx_v)

        def fetch(j):
            return pltpu.make_async_copy(table_hbm.at[idx_v.at[j]], rows_v.at[j % 2], sem_in.at[j % 2])

        def flush(j):
            return pltpu.make_async_copy(rows_v.at[j % 2], out_hbm.at[pl.ds(base + j * w, w)],
                                         sem_out.at[j % 2])

        fetch(0).start()
        for j in range(nwin):
            fetch(j).wait()
            if j + 1 < nwin:
                if j >= 1:
                    flush(j - 1).wait()
                fetch(j + 1).start()
            flush(j).start()
        for j in range(max(nwin - 2, 0), nwin):
            flush(j).wait()

    return gather(table, idx.reshape(SC_WORKERS, nwin, w))


def _sc_scatter_rows(x, dest2, nrows):
    t, d = x.shape
    w = SC_SCATTER_WINDOW
    per_worker = t // SC_WORKERS
    nwin = per_worker // w
    assert per_worker * SC_WORKERS == t and nwin * w == per_worker
    mesh = plsc.VectorSubcoreMesh(core_axis_name="c", subcore_axis_name="s")

    @functools.partial(
        pl.kernel, mesh=mesh,
        out_type=jax.ShapeDtypeStruct((nrows, d), x.dtype),
        scratch_types=[pltpu.VMEM((TOP_K, nwin, w), jnp.int32), pltpu.VMEM((2, w, d), x.dtype),
                       pltpu.SemaphoreType.DMA((2,)), pltpu.SemaphoreType.DMA((2,))],
        name="sc_scatter_rows",
    )
    def scatter(x_hbm, dest_hbm, out_hbm, idx_v, rows_v, sem_in, sem_out):
        wid = lax.axis_index("s") * SC_CORES + lax.axis_index("c")
        base = wid * per_worker
        for k in range(TOP_K):
            pltpu.sync_copy(dest_hbm.at[k, wid], idx_v.at[k])

        def fetch(j):
            return pltpu.make_async_copy(x_hbm.at[pl.ds(base + j * w, w)], rows_v.at[j % 2], sem_in.at[j % 2])

        def spread(j, k):
            return pltpu.make_async_copy(rows_v.at[j % 2], out_hbm.at[idx_v.at[k, j]], sem_out.at[j % 2])

        fetch(0).start()
        for j in range(nwin):
            fetch(j).wait()
            if j + 1 < nwin:
                if j >= 1:
                    for k in range(TOP_K):
                        spread(j - 1, k).wait()
                fetch(j + 1).start()
            for k in range(TOP_K):
                spread(j, k).start()
        for j in range(max(nwin - 2, 0), nwin):
            for k in range(TOP_K):
                spread(j, k).wait()

    return scatter(x, dest2.reshape(TOP_K, SC_WORKERS, nwin, w))


def _expert_kernel(be_ref, nu_ref, nv_ref, nx_ref, x_ref, wg_hbm, wu_hbm, wd_hbm, o_ref,
                   wg_s, wu_s, wd_s, wg_f, wu_f, wd_f, slot_s, sem):
    i = pl.program_id(0)
    used = i < nu_ref[0]
    changed = jnp.logical_or(i == 0, be_ref[i] != be_ref[jnp.maximum(i - 1, 0)])

    def weight_copies(e, slot):
        return (pltpu.make_async_copy(wg_hbm.at[e], wg_f.at[slot], sem.at[slot, 0]),
                pltpu.make_async_copy(wu_hbm.at[e], wu_f.at[slot], sem.at[slot, 1]),
                pltpu.make_async_copy(wd_hbm.at[e], wd_f.at[slot], sem.at[slot, 2]))

    @pl.when(i == 0)
    def _():
        slot_s[0] = 0
        for c in weight_copies(be_ref[0], 0):
            c.start()

    @pl.when(jnp.logical_and(used, changed))
    def _():
        slot = slot_s[0]
        for c in weight_copies(be_ref[i], slot):
            c.wait()
        wg_s[...] = wg_f[slot].astype(BF16)
        wu_s[...] = wu_f[slot].astype(BF16)
        wd_s[...] = wd_f[slot].astype(BF16)

        @pl.when(nx_ref[i] != be_ref[i])
        def _():
            for c in weight_copies(nx_ref[i], 1 - slot):
                c.start()

        slot_s[0] = 1 - slot

    @pl.when(used)
    def _():
        row = lax.broadcasted_iota(jnp.int32, x_ref.shape, 0)
        x = _unpack_bf16_pairs(jnp.where(row < nv_ref[i], x_ref[...], 0)).astype(BF16)
        gate = jnp.dot(x, wg_s[...], preferred_element_type=F32)
        up = jnp.dot(x, wu_s[...], preferred_element_type=F32)
        h = (gate * jax.nn.sigmoid(gate) * up).astype(BF16)
        o_ref[...] = _pack_bf16_pairs(jnp.dot(h, wd_s[...], preferred_element_type=F32))

    @pl.when(jnp.logical_not(used))
    def _():
        o_ref[...] = jnp.zeros_like(o_ref)


def _experts(blk_e, n_used, nvalid, next_e, xb, w_gate, w_up, w_down):
    rows = xb.shape[0]
    nblocks = rows // MOE_ROWS
    grid_spec = pltpu.PrefetchScalarGridSpec(
        num_scalar_prefetch=4,
        grid=(nblocks,),
        in_specs=[
            pl.BlockSpec((MOE_ROWS, D_MODEL // 2), lambda i, be, nu, nv, nx: (i, 0)),
            pl.BlockSpec(memory_space=pl.ANY),
            pl.BlockSpec(memory_space=pl.ANY),
            pl.BlockSpec(memory_space=pl.ANY),
        ],
        out_specs=pl.BlockSpec((MOE_ROWS, D_MODEL // 2), lambda i, be, nu, nv, nx: (i, 0)),
        scratch_shapes=[pltpu.VMEM((D_MODEL, D_EXPERT), BF16), pltpu.VMEM((D_MODEL, D_EXPERT), BF16),
                        pltpu.VMEM((D_EXPERT, D_MODEL), BF16),
                        pltpu.VMEM((2, D_MODEL, D_EXPERT), F32), pltpu.VMEM((2, D_MODEL, D_EXPERT), F32),
                        pltpu.VMEM((2, D_EXPERT, D_MODEL), F32),
                        pltpu.SMEM((1,), jnp.int32), pltpu.SemaphoreType.DMA((2, 3))],
    )
    return pl.pallas_call(
        _expert_kernel,
        grid_spec=grid_spec,
        out_shape=jax.ShapeDtypeStruct((rows, D_MODEL // 2), jnp.int32),
        compiler_params=_cparams(("arbitrary",)),
        name="experts",
    )(blk_e, n_used, nvalid, next_e, xb, w_gate, w_up, w_down)


def _combine_kernel(x1_ref, y1_ref, y2_ref, route_ref, g_ref, outp_ref, outs_ref, *, prompt_tiles):
    r = route_ref[...]
    x = (x1_ref[...] + r[:, 2:3] * _unpack_bf16_pairs(y1_ref[...])
         + r[:, 3:4] * _unpack_bf16_pairs(y2_ref[...]))
    ms = jnp.mean(x * x, axis=-1, keepdims=True)
    y = x * lax.rsqrt(ms + EPS) * g_ref[...]
    i = pl.program_id(0)

    @pl.when(i < prompt_tiles)
    def _():
        outp_ref[...] = y

    @pl.when(i >= prompt_tiles)
    def _():
        outs_ref[...] = y


def _combine_norm(x1, ygath, route, gamma, tp):
    t = x1.shape[0]
    tm = 512
    nt, npt = t // tm, tp // tm
    return pl.pallas_call(
        functools.partial(_combine_kernel, prompt_tiles=npt),
        grid=(nt,),
        in_specs=[
            pl.BlockSpec((tm, D_MODEL), lambda i: (i, 0)),
            pl.BlockSpec((tm, D_MODEL // 2), lambda i: (i, 0)),
            pl.BlockSpec((tm, D_MODEL // 2), lambda i: (i + nt, 0)),
            pl.BlockSpec((tm, LANES), lambda i: (i, 0)),
            pl.BlockSpec((1, D_MODEL), lambda i: (0, 0)),
        ],
        out_specs=[
            pl.BlockSpec((tm, D_MODEL), lambda i: (jnp.minimum(i, npt - 1), 0)),
            pl.BlockSpec((tm, D_MODEL), lambda i: (jnp.maximum(i - npt, 0), 0)),
        ],
        out_shape=[jax.ShapeDtypeStruct((tp, D_MODEL), F32), jax.ShapeDtypeStruct((t - tp, D_MODEL), F32)],
        compiler_params=_cparams(("arbitrary",)),
        name="combine_norm",
    )(x1, ygath, ygath, route, gamma)


def _band_index():
    c = (2 * QB - np.arange(2 * QB)) % (2 * QB)
    return c, c <= QB


def _bias_a_prompt(table_a):
    c, valid = _band_index()
    idx = np.stack([_t5_bucket_np(d * np.clip(QB - c, 0, QB)) for d in DILATIONS])
    return jnp.where(valid, jnp.transpose(table_a[idx], (0, 2, 1)) * LOG2E, NEG)


def _bias_b_prompt(table_b):
    c, valid = _band_index()
    valid = valid & (c >= 1)
    h = jnp.where(valid, table_b[_t5_bucket_np(np.clip(QB - c, 0, QB))].T * LOG2E, NEG)
    return jnp.transpose(h.reshape(KV_B, G_B, 2 * QB), (1, 0, 2)).reshape(H_B, 2 * QB)


def _sample_bias(table, span, t, log2_weight):
    cols = span + LANES
    period = cols + LANES
    x = np.arange(period)
    dist = np.where(x >= period - t, span - x + period, span - x)
    extra = log2_weight(dist)
    valid = np.isfinite(extra)
    u = jnp.where(valid, table[_t5_bucket_np(np.maximum(dist, 0))].T * LOG2E
                  + np.where(valid, extra, 0.0).astype(np.float32), NEG)
    rows = jnp.tile(u, (1, t))[:, :t * (period - 1)].reshape(u.shape[0], t, period - 1)[:, :, :cols]
    return rows.reshape(u.shape[0] * t, cols)


def _bias_a_sample(table_a, t):
    def log2_count(dist):
        count = np.zeros(dist.shape, np.int64)
        for w, d in zip(WINDOWS, DILATIONS):
            count += (dist >= 0) & (dist % d == 0) & (dist <= w)
        return np.where(count > 0, np.log2(np.maximum(count, 1)), -np.inf)

    return _sample_bias(table_a, WIN_A, t, log2_count)


def _bias_b_sample(table_b, t):
    return _sample_bias(table_b, WIN_B, t,
                        lambda dist: np.where((dist >= 0) & (dist < WIN_B), 0.0, -np.inf))


def _dest_kernel(route_ref, cnt_ref, tri_ref, dest_ref, meta_ref, run_scr, pst_scr):
    i = pl.program_id(0)
    tm = route_ref.shape[0]
    r = route_ref[...]
    lane = lax.broadcasted_iota(jnp.int32, (tm, LANES), 1)
    lanef = lane.astype(F32)
    oh0 = lanef == r[:, 0:1]
    oh1 = lanef == r[:, 1:2]
    ohf = jnp.concatenate([oh0, oh1], axis=0).astype(F32)

    @pl.when(i == 0)
    def _():
        cnt = jnp.broadcast_to(cnt_ref[...], (LANES, LANES))
        padded = jnp.floor((cnt + (MOE_ROWS - 1)) * (1.0 / MOE_ROWS)) * MOE_ROWS
        lane_e = lax.broadcasted_iota(jnp.int32, (LANES, LANES), 1)
        x = padded
        for sh in (1, 2, 4, 8, 16, 32, 64):
            x = x + jnp.where(lane_e >= sh, pltpu.roll(x, sh, 1), 0.0)
        pst_scr[...] = (x - padded)[0:1]
        run_scr[...] = jnp.zeros_like(run_scr)
        wide = lambda v: jnp.concatenate([v.T, v.T], axis=1)
        cnt_t, bend_t = wide(cnt), wide(x * (1.0 / MOE_ROWS))
        bstart_t = wide((x - padded) * (1.0 / MOE_ROWS))
        blk = lax.broadcasted_iota(jnp.int32, (LANES, 2 * LANES), 1).astype(F32)
        exp = lax.broadcasted_iota(jnp.int32, (LANES, 2 * LANES), 0)
        real = exp < N_EXPERTS
        blk_e = jnp.minimum(jnp.sum(jnp.where(real & (bend_t <= blk), 1.0, 0.0), axis=0, keepdims=True),
                            N_EXPERTS - 1.0)
        mine = exp.astype(F32) == blk_e
        within = blk[0:1] - jnp.sum(jnp.where(mine, bstart_t, 0.0), axis=0, keepdims=True)
        nvalid = jnp.clip(jnp.sum(jnp.where(mine, cnt_t, 0.0), axis=0, keepdims=True) - within * MOE_ROWS,
                          0.0, float(MOE_ROWS))
        n_used = jnp.max(jnp.where(real, bend_t, 0.0), axis=0, keepdims=True)
        later = real & (exp.astype(F32) > blk_e) & (cnt_t > 0.0)
        nxt = jnp.min(jnp.where(later, exp.astype(F32), float(LANES)), axis=0, keepdims=True)
        nxt = jnp.where(nxt >= N_EXPERTS, blk_e, nxt)
        meta_ref[...] = jnp.concatenate([blk_e, nvalid, n_used, nxt, jnp.zeros((4, 2 * LANES), F32)],
                                        axis=0).astype(jnp.int32)

    base = run_scr[...] + pst_scr[...] - 1.0
    vals = []
    for c in range(2 * tm // LANES):
        ohc = ohf[c * LANES:(c + 1) * LANES]
        vals.append(jnp.dot(tri_ref[...], ohc.astype(BF16), preferred_element_type=F32) + base)
        base = base + jnp.sum(ohc, axis=0, keepdims=True)
    val = jnp.concatenate(vals, axis=0)
    d0 = jnp.sum(jnp.where(oh0, val[:tm], 0.0), axis=-1, keepdims=True)
    d1 = jnp.sum(jnp.where(oh1, val[tm:], 0.0), axis=-1, keepdims=True)
    tile = jnp.where(lane == 0, d0, jnp.where(lane == 1, d1, 0.0))
    dest_ref[...] = tile.T[:8].astype(jnp.int32)
    run_scr[...] += jnp.sum(ohf, axis=0, keepdims=True)


def _dispatch(route, cnt):
    t = route.shape[0]
    tm = 512
    tri = (jnp.arange(LANES)[:, None] >= jnp.arange(LANES)[None, :]).astype(BF16)
    nblocks = -(-t * TOP_K // MOE_ROWS) + N_EXPERTS
    assert nblocks <= 2 * LANES
    dest, meta = pl.pallas_call(
        _dest_kernel,
        grid=(t // tm,),
        in_specs=[pl.BlockSpec((tm, LANES), lambda i: (i, 0)),
                  pl.BlockSpec((1, LANES), lambda i: (0, 0)),
                  pl.BlockSpec((LANES, LANES), lambda i: (0, 0))],
        out_specs=[pl.BlockSpec((8, tm), lambda i: (0, i)),
                   pl.BlockSpec((8, 2 * LANES), lambda i: (0, 0))],
        out_shape=[jax.ShapeDtypeStruct((8, t), jnp.int32), jax.ShapeDtypeStruct((8, 2 * LANES), jnp.int32)],
        scratch_shapes=[pltpu.VMEM((1, LANES), F32), pltpu.VMEM((1, LANES), F32)],
        compiler_params=_cparams(("arbitrary",)),
        name="moe_dest",
    )(route, cnt, tri)
    return dest[:TOP_K], meta[0, :nblocks], meta[2, :1], meta[1, :nblocks], meta[3, :nblocks]


def kernel(x_prompt, x_sample, cache_a_k, cache_a_v, cache_b_k, cache_b_v, rel_bias_table, attn_norm, w_in,
           w_out, attn_sinks, ffn_norm, w_router_group, b_router_group, w_router_expert, b_router_expert,
           w_gate, w_up, w_down, final_norm):
    s = x_prompt.shape[1]
    ns, ts = x_sample.shape[0], x_sample.shape[1]
    table_a = rel_bias_table[:, :H_A]
    table_b = rel_bias_table[:, H_A:]

    w = w_in[0]
    wqa, wka, wva, wqb, wkb, wvb = (w[:, 0:512], w[:, 512:1024], w[:, 1024:1536], w[:, 1536:2048],
                                    w[:, 2048:2176], w[:, 2176:2304])
    wqb = jnp.transpose(wqb.reshape(D_MODEL, KV_B, G_B, HEAD_DIM), (0, 2, 1, 3)).reshape(D_MODEL, 512)
    wp = jnp.concatenate([wka, wva, wqa, wqb, wkb, wvb], axis=1).astype(BF16)
    cscale = jnp.concatenate([jnp.ones((1, 1024), F32), jnp.full((1, 1024), SCALE * LOG2E, F32),
                              jnp.ones((1, 256), F32)], axis=1)
    wo = w_out[0]
    wo_b = jnp.transpose(wo[512:].reshape(KV_B, G_B, HEAD_DIM, D_MODEL), (1, 0, 2, 3)).reshape(512, D_MODEL)
    wo_p = jnp.concatenate([wo[:512], wo_b], axis=0).astype(BF16)
    wr = jnp.concatenate([w_router_group[0],
                          jnp.transpose(w_router_expert[0], (1, 0, 2)).reshape(D_MODEL, N_EXPERTS),
                          jnp.zeros((D_MODEL, LANES - N_GROUPS - N_EXPERTS), F32)], axis=1)
    wr_hi = wr.astype(BF16)
    wr = jnp.concatenate([wr_hi, wr_hi, (wr - wr_hi.astype(F32)).astype(BF16)], axis=0)
    br = jnp.concatenate([b_router_group[0], b_router_expert[0].reshape(N_EXPERTS),
                          jnp.zeros((LANES - N_GROUPS - N_EXPERTS,), F32)]).reshape(1, LANES)
    sinks2 = attn_sinks[0] * LOG2E
    sinks_gk = jnp.transpose(sinks2.reshape(KV_B, G_B), (1, 0)).reshape(H_B)
    sink_rows_p = jnp.repeat(sinks_gk, QB).reshape(G_B, 1, 2 * QB)
    sink_rows_s = jnp.repeat(sinks2, ts).reshape(H_B * ts, 1)
    emat = jnp.tile(jnp.arange(LANES)[:, None] == (jnp.arange(A_WIDTH)[None, :] // HEAD_DIM),
                    (3, 1)).astype(BF16)
    attn_g = attn_norm[0].reshape(1, D_MODEL)
    ffn_g = ffn_norm[0].reshape(1, D_MODEL)

    xp = x_prompt.reshape(s, D_MODEL)
    aperm, qb_p, kvb_p, akv32, bkv32 = _proj_prompt(xp, attn_g, wp, cscale)
    a4 = _attn_a_prompt(aperm, _bias_a_prompt(table_a), emat)
    ob_p = _attn_b_prompt(qb_p, kvb_p, _bias_b_prompt(table_b), sink_rows_p)

    xs = x_sample.reshape(ns * ts, D_MODEL)
    q_s, kv_s = _proj_sample(xs, attn_g, wp, cscale)
    akt = jnp.transpose(cache_a_k[0], (0, 2, 3, 1)).reshape(ns, A_WIDTH, WIN_A)
    avt = jnp.transpose(cache_a_v[0], (0, 2, 3, 1)).reshape(ns, A_WIDTH, WIN_A)
    bkt = jnp.transpose(cache_b_k[0], (0, 2, 3, 1)).reshape(ns, LANES, WIN_B)
    bvt = jnp.transpose(cache_b_v[0], (0, 2, 3, 1)).reshape(ns, LANES, WIN_B)

    x1, xn, route, cnt = _out_router(xp, a4, ob_p, xs, q_s.reshape(ns, ts, 1024), kv_s.reshape(ns, ts, 1280),
                                     akt, avt, bkt, bvt, _bias_a_sample(table_a, ts),
                                     _bias_b_sample(table_b, ts), sink_rows_s, wo_p, ffn_g, wr, br)
    dest2, blk_e, n_used, nvalid, next_e = _dispatch(route, cnt)
    xb = _sc_scatter_rows(xn, dest2, blk_e.shape[0] * MOE_ROWS)
    yb = _experts(blk_e, n_used, nvalid, next_e, xb, w_gate[0], w_up[0], w_down[0])
    y_p, y_s = _combine_norm(x1, _sc_gather_rows(yb, dest2.reshape(-1)), route, final_norm.reshape(1, D_MODEL), s)

    y_prompt = y_p.reshape(1, s, D_MODEL)
    y_sample = y_s.reshape(ns, ts, D_MODEL)
    keep_a, keep_b = min(WIN_A, s), min(WIN_B, s)
    pak = akv32[SPAN - keep_a:, :512].reshape(1, 1, keep_a, H_A, HEAD_DIM)
    pav = akv32[SPAN - keep_a:, 512:].reshape(1, 1, keep_a, H_A, HEAD_DIM)
    pbk = bkv32[SPAN - keep_b:, :128].reshape(1, 1, keep_b, KV_B, HEAD_DIM)
    pbv = bkv32[SPAN - keep_b:, 128:].reshape(1, 1, keep_b, KV_B, HEAD_DIM)
    sak = kv_s[:, 0:512].reshape(1, ns, ts, H_A, HEAD_DIM)
    sav = kv_s[:, 512:1024].reshape(1, ns, ts, H_A, HEAD_DIM)
    sbk = kv_s[:, 1024:1152].reshape(1, ns, ts, KV_B, HEAD_DIM)
    sbv = kv_s[:, 1152:1280].reshape(1, ns, ts, KV_B, HEAD_DIM)
    return (y_prompt, y_sample, pak, pav, pbk, pbv, sak, sav, sbk, sbv)
```

```python
import functools
import math

import jax
import jax.numpy as jnp
import numpy as np
from jax import lax
from jax.experimental import pallas as pl
from jax.experimental.pallas import tpu as pltpu
from jax.experimental.pallas import tpu_sc as plsc

D_MODEL = 1024
HEAD_DIM = 64
H_A = 8
H_B = 8
KV_B = 2
G_B = 4
DILATIONS = (1, 4, 16)
WINDOWS = (128, 512, 2048)
WIN_A = 2048
WIN_B = 128
NUM_BUCKETS = 32
MAX_DISTANCE = 2048
N_GROUPS = 4
EXPERTS_PER_GROUP = 8
N_EXPERTS = 32
TOP_K = 2
D_EXPERT = 512
EPS = 1e-5
SCALE = HEAD_DIM ** -0.5

LANES = 128
SPAN = 2048
QB = 128
NCHUNK = 9
A_WIDTH = H_A * HEAD_DIM
MOE_ROWS = 512
SC_CORES = 2
SC_SUBCORES = 16
SC_WORKERS = SC_CORES * SC_SUBCORES
SC_WINDOW = 64
SC_SCATTER_WINDOW = 32
NEG = -1e30
LOG2E = math.log2(math.e)
B_STEP = 512
V7X_VMEM_BYTES = 64 * 1024 * 1024
VMEM_LIMIT = V7X_VMEM_BYTES - 8 * 1024 * 1024
CACHE_BUFFERS = 3
OUT_ROUTER_VMEM = V7X_VMEM_BYTES - 4 * 1024 * 1024

F32 = jnp.float32
BF16 = jnp.bfloat16


def _t5_bucket_np(dist):
    dist = np.asarray(dist, np.int64)
    max_exact = NUM_BUCKETS // 2
    d = np.maximum(dist, 1).astype(np.float32)
    ratio = np.log(d / np.float32(max_exact)) / np.float32(math.log(MAX_DISTANCE / max_exact))
    large = max_exact + (ratio * np.float32(NUM_BUCKETS - max_exact)).astype(np.int32)
    large = np.minimum(large, NUM_BUCKETS - 1)
    return np.where(dist < max_exact, dist, large).astype(np.int32)


def _cparams(sem, vmem=VMEM_LIMIT):
    return pltpu.CompilerParams(dimension_semantics=sem, vmem_limit_bytes=vmem)


def _proj_prompt_kernel(x_ref, g_ref, w_ref, cs_ref, aperm_ref, qb_ref, kvb_ref, akv_ref, bkv_ref,
                        h_scr, p_scr):
    n = pl.program_id(1)

    @pl.when(n == 0)
    def _():
        x = x_ref[...]
        ms = jnp.mean(x * x, axis=-1, keepdims=True)
        h_scr[...] = (x * lax.rsqrt(ms + EPS) * g_ref[...]).astype(BF16)

    p = jnp.dot(h_scr[...], w_ref[...], preferred_element_type=F32) * cs_ref[...]

    @pl.when(n < 6)
    def _():
        aperm_ref[0] = p.astype(BF16)
        p_scr[0, 0] = p[:, :LANES]
        p_scr[0, 1] = p[:, LANES:]
        quarter = SPAN // 4
        for r in range(4):
            lo = p_scr[0, 0, pl.ds(r, quarter, stride=4), :]
            hi = p_scr[0, 1, pl.ds(r, quarter, stride=4), :]
            p_scr[1, 0, r * quarter:(r + 1) * quarter, :] = lo
            p_scr[1, 1, r * quarter:(r + 1) * quarter, :] = hi
            aperm_ref[1, r * quarter:(r + 1) * quarter, :] = jnp.concatenate([lo, hi], axis=1).astype(BF16)
        for r16 in range(16):
            start = (r16 % 4) * quarter + r16 // 4
            t = jnp.concatenate([p_scr[1, 0, pl.ds(start, QB, stride=4), :],
                                 p_scr[1, 1, pl.ds(start, QB, stride=4), :]], axis=1)
            aperm_ref[2, r16 * QB:(r16 + 1) * QB, :] = t.astype(BF16)

    @pl.when(n < 4)
    def _():
        akv_ref[...] = p

    @pl.when(jnp.logical_or(n == 6, n == 7))
    def _():
        qb_ref[...] = p.astype(BF16)

    @pl.when(n == 8)
    def _():
        kvb_ref[...] = p.astype(BF16)
        bkv_ref[...] = p


def _proj_prompt(x, gamma, w, cscale):
    s = x.shape[0]
    nspan = s // SPAN
    return pl.pallas_call(
        _proj_prompt_kernel,
        grid=(nspan, NCHUNK),
        in_specs=[
            pl.BlockSpec((SPAN, D_MODEL), lambda b, n: (b, 0)),
            pl.BlockSpec((1, D_MODEL), lambda b, n: (0, 0)),
            pl.BlockSpec((D_MODEL, 256), lambda b, n: (0, n)),
            pl.BlockSpec((1, 256), lambda b, n: (0, n)),
        ],
        out_specs=[
            pl.BlockSpec((3, SPAN, 256), lambda b, n: (0, b, jnp.minimum(n, 5))),
            pl.BlockSpec((SPAN, 256), lambda b, n: (b, jnp.clip(n - 6, 0, 1))),
            pl.BlockSpec((SPAN, 256), lambda b, n: (b, 0)),
            pl.BlockSpec((SPAN, 256), lambda b, n: (0, jnp.where(b == nspan - 1, jnp.minimum(n, 3), 0))),
            pl.BlockSpec((SPAN, 256), lambda b, n: (0, 0)),
        ],
        out_shape=[
            jax.ShapeDtypeStruct((3, s, 3 * A_WIDTH), BF16),
            jax.ShapeDtypeStruct((s, 512), BF16),
            jax.ShapeDtypeStruct((s, 256), BF16),
            jax.ShapeDtypeStruct((SPAN, 1024), F32),
            jax.ShapeDtypeStruct((SPAN, 256), F32),
        ],
        scratch_shapes=[pltpu.VMEM((SPAN, D_MODEL), BF16), pltpu.VMEM((2, 2, SPAN, LANES), F32)],
        compiler_params=_cparams(("arbitrary", "arbitrary")),
        name="proj_prompt",
    )(x, gamma, w, cscale)


def _proj_sample_kernel(x_ref, g_ref, w_ref, cs_ref, q_ref, kv_ref):
    x = x_ref[...]
    ms = jnp.mean(x * x, axis=-1, keepdims=True)
    h = (x * lax.rsqrt(ms + EPS) * g_ref[...]).astype(BF16)
    p = jnp.dot(h, w_ref[...], preferred_element_type=F32) * cs_ref[...]
    kv_ref[:, :1024] = p[:, :1024]
    kv_ref[:, 1024:] = p[:, 2048:]
    q_ref[...] = p[:, 1024:2048]


def _proj_sample(x, gamma, w, cscale):
    t = x.shape[0]
    tm = 512
    return pl.pallas_call(
        _proj_sample_kernel,
        grid=(t // tm,),
        in_specs=[
            pl.BlockSpec((tm, D_MODEL), lambda i: (i, 0)),
            pl.BlockSpec((1, D_MODEL), lambda i: (0, 0)),
            pl.BlockSpec((D_MODEL, 2304), lambda i: (0, 0)),
            pl.BlockSpec((1, 2304), lambda i: (0, 0)),
        ],
        out_specs=[
            pl.BlockSpec((tm, 1024), lambda i: (i, 0)),
            pl.BlockSpec((tm, 1280), lambda i: (i, 0)),
        ],
        out_shape=[
            jax.ShapeDtypeStruct((t, 1024), F32),
            jax.ShapeDtypeStruct((t, 1280), F32),
        ],
        compiler_params=_cparams(("arbitrary",)),
        name="proj_sample",
    )(x, gamma, w, cscale)


def _spread_heads(w, e3_ref):
    hi = w.astype(BF16)
    r1 = w - hi.astype(F32)
    mid = r1.astype(BF16)
    low = (r1 - mid.astype(F32)).astype(BF16)
    return jnp.dot(jnp.concatenate([hi, mid, low], axis=1), e3_ref[...], preferred_element_type=F32)


def _pair_tile(q2, kk, vv, bias_t, lo, sink=None, normalize=True):
    zero = jnp.zeros_like(q2)
    qq = jnp.concatenate([jnp.where(lo, q2, zero), jnp.where(lo, zero, q2)], axis=0)
    st = lax.dot_general(kk, qq, (((1,), (1,)), ((), ())), preferred_element_type=F32)
    st = st + bias_t
    m = jnp.max(st, axis=0, keepdims=True)
    if sink is not None:
        m = jnp.maximum(m, sink)
    p = jnp.exp2(st - m)
    den = jnp.sum(p, axis=0, keepdims=True)
    if sink is not None:
        den = den + jnp.exp2(sink - m)
    pn = (p * (1.0 / den) if normalize else p).astype(BF16)
    o = lax.dot_general(pn, vv, (((0,), (0,)), ((), ())), preferred_element_type=F32)
    return jnp.where(lo, o[:QB], o[QB:]), m, den


def _fill_band_tiles(h_ref, bias_scr):
    nk = 2 * QB
    prev = lax.broadcasted_iota(jnp.int32, (nk, nk), 0) < QB
    for pair in range(h_ref.shape[0] // 2):
        halves = []
        for hh in range(2):
            row = h_ref[2 * pair + hh:2 * pair + hh + 1, :]
            band = pltpu.roll(jnp.broadcast_to(row, (nk, nk)), 0, 1, stride=1, stride_axis=0)
            halves.append(band[:, :QB])
        tile = jnp.concatenate(halves, axis=1)
        bias_scr[0, pair] = tile
        bias_scr[1, pair] = jnp.where(prev, NEG, tile)


def _attn_a_kernel(q_ref, kvc_ref, kvp_ref, h_ref, e_ref, out_ref, o_scr, st_scr, bias_scr):
    b = pl.program_id(0)
    g = pl.program_id(1)
    nblk = jnp.where(g == 0, 16, jnp.where(g == 1, 4, 1))
    lane = lax.broadcasted_iota(jnp.int32, (QB, LANES), 1)
    lo = lane < HEAD_DIM

    @pl.when(b == 0)
    def _():
        _fill_band_tiles(h_ref, bias_scr.at[g])

    bias_ref = bias_scr.at[g]

    for cb in range(SPAN // QB):
        first = lax.rem(jnp.int32(cb), nblk) == 0
        rows = slice(cb * QB, (cb + 1) * QB)
        prow_c = max(cb - 1, 0) * QB
        prow_p = pl.multiple_of(jnp.where(first, cb + nblk - 1, 0) * QB, QB)
        variant = jnp.logical_and(first, b == 0).astype(jnp.int32)
        maxes, dens = [], []
        for hp in range(4):
            ks = slice(hp * LANES, (hp + 1) * LANES)
            vs = slice(A_WIDTH + hp * LANES, A_WIDTH + (hp + 1) * LANES)
            kp = jnp.where(first, kvp_ref[pl.ds(prow_p, QB), ks], kvc_ref[prow_c:prow_c + QB, ks])
            vp = jnp.where(first, kvp_ref[pl.ds(prow_p, QB), vs], kvc_ref[prow_c:prow_c + QB, vs])
            kk = jnp.concatenate([kp, kvc_ref[rows, ks]], axis=0)
            vv = jnp.concatenate([vp, kvc_ref[rows, vs]], axis=0)
            o, m, den = _pair_tile(q_ref[rows, ks], kk, vv, bias_ref[variant, hp], lo, normalize=False)
            o_scr[g, hp, rows, :] = o
            maxes += [m[:, :QB], m[:, QB:]]
            dens += [den[:, :QB], den[:, QB:]]
        sm = jnp.concatenate(maxes + dens + [jnp.zeros((LANES - 2 * H_A, QB), F32)], axis=0)
        st_scr[g, rows, :] = sm.T

    @pl.when(g == 2)
    def _():
        def merge(c, carry):
            r2 = lax.rem(c, 4) * (SPAN // 4) + c // 4
            r3 = pl.multiple_of(c * QB, QB)
            s1 = st_scr[0, pl.ds(c, QB, stride=16), :]
            s2 = st_scr[1, pl.ds(r2, QB, stride=4), :]
            s3 = st_scr[2, pl.ds(r3, QB), :]
            mx = jnp.maximum(jnp.maximum(s1, s2), s3)
            w1 = jnp.exp2(s1 - mx)
            w2 = jnp.exp2(s2 - mx)
            w3 = jnp.exp2(s3 - mx)
            shift = LANES - H_A
            tot = (w1 * pltpu.roll(s1, shift, 1) + w2 * pltpu.roll(s2, shift, 1)
                   + w3 * pltpu.roll(s3, shift, 1))
            head_lane = lax.broadcasted_iota(jnp.int32, (QB, LANES), 1) < H_A
            a1 = _spread_heads(jnp.where(head_lane, w1 / tot, 0.0), e_ref)
            a2 = _spread_heads(jnp.where(head_lane, w2 / tot, 0.0), e_ref)
            a3 = _spread_heads(jnp.where(head_lane, w3 / tot, 0.0), e_ref)
            for hp in range(4):
                sl = slice(hp * LANES, (hp + 1) * LANES)
                o1 = o_scr[0, hp, pl.ds(c, QB, stride=16), :]
                o2 = o_scr[1, hp, pl.ds(r2, QB, stride=4), :]
                o3 = o_scr[2, hp, pl.ds(r3, QB), :]
                out_ref[hp, pl.ds(c, QB, stride=16), :] = a1[:, sl] * o1 + a2[:, sl] * o2 + a3[:, sl] * o3
            return carry

        lax.fori_loop(0, 16, merge, 0, unroll=4)


def _attn_a_prompt(aperm, bias_a, emat):
    s = aperm.shape[1]
    nspan = s // SPAN
    return pl.pallas_call(
        _attn_a_kernel,
        grid=(nspan, 3),
        in_specs=[
            pl.BlockSpec((None, SPAN, A_WIDTH), lambda b, g: (g, b, 2)),
            pl.BlockSpec((None, SPAN, 2 * A_WIDTH), lambda b, g: (g, b, 0)),
            pl.BlockSpec((None, SPAN, 2 * A_WIDTH), lambda b, g: (g, jnp.maximum(b - 1, 0), 0)),
            pl.BlockSpec((None, H_A, 2 * QB), lambda b, g: (g, 0, 0)),
            pl.BlockSpec((3 * LANES, A_WIDTH), lambda b, g: (0, 0)),
        ],
        out_specs=pl.BlockSpec((4, SPAN, LANES), lambda b, g: (0, b, 0)),
        out_shape=jax.ShapeDtypeStruct((4, s, LANES), F32),
        scratch_shapes=[pltpu.VMEM((3, 4, SPAN, LANES), F32), pltpu.VMEM((3, SPAN, LANES), F32),
                        pltpu.VMEM((3, 2, 4, 2 * QB, 2 * QB), F32)],
        compiler_params=_cparams(("arbitrary", "arbitrary")),
        name="attn_a_prompt",
    )(aperm, aperm, aperm, bias_a, emat)


def _attn_b_kernel(q_ref, kvc_ref, kvp_ref, h_ref, sink_ref, out_ref, bias_ref):
    i = pl.program_id(0)
    lane = lax.broadcasted_iota(jnp.int32, (QB, LANES), 1)
    lo = lane < HEAD_DIM

    @pl.when(i == 0)
    def _():
        _fill_band_tiles(h_ref, bias_ref)

    variant = (i == 0).astype(jnp.int32)
    for j in range(B_STEP // QB):
        rows = slice(j * QB, (j + 1) * QB)
        if j == 0:
            kp, vp = kvp_ref[:, :LANES], kvp_ref[:, LANES:]
        else:
            kp, vp = kvc_ref[(j - 1) * QB:j * QB, :LANES], kvc_ref[(j - 1) * QB:j * QB, LANES:]
        kk = jnp.concatenate([kp, kvc_ref[rows, :LANES]], axis=0)
        vv = jnp.concatenate([vp, kvc_ref[rows, LANES:]], axis=0)
        for g in range(G_B):
            bias_t = bias_ref[variant, g] if j == 0 else bias_ref[0, g]
            o, _, _ = _pair_tile(q_ref[rows, g * LANES:(g + 1) * LANES], kk, vv, bias_t, lo, sink=sink_ref[g])
            out_ref[rows, g * LANES:(g + 1) * LANES] = o.astype(BF16)


def _attn_b_prompt(qb, kvb, bias_b, sink_rows):
    s = qb.shape[0]
    per = B_STEP // QB
    return pl.pallas_call(
        _attn_b_kernel,
        grid=(s // B_STEP,),
        in_specs=[
            pl.BlockSpec((B_STEP, 512), lambda i: (i, 0)),
            pl.BlockSpec((B_STEP, 256), lambda i: (i, 0)),
            pl.BlockSpec((QB, 256), lambda i: (jnp.maximum(i * per - 1, 0), 0)),
            pl.BlockSpec((H_B, 2 * QB), lambda i: (0, 0)),
            pl.BlockSpec((G_B, 1, 2 * QB), lambda i: (0, 0, 0)),
        ],
        out_specs=pl.BlockSpec((B_STEP, 512), lambda i: (i, 0)),
        out_shape=jax.ShapeDtypeStruct((s, 512), BF16),
        scratch_shapes=[pltpu.VMEM((2, G_B, 2 * QB, 2 * QB), F32)],
        compiler_params=_cparams(("arbitrary",)),
        name="attn_b_prompt",
    )(qb, kvb, kvb, bias_b, sink_rows)


def _sample_attention(q, kvn, akt, avt, bkt, bvt, cba, cbb, sink):
    t = q.shape[0]
    kvn_p = jnp.concatenate([kvn, jnp.zeros((LANES - t, kvn.shape[1]), F32)], axis=0).astype(BF16)
    lane_a = lax.broadcasted_iota(jnp.int32, (t, A_WIDTH), 1) // HEAD_DIM

    qa = q[:, :A_WIDTH]
    qbd = jnp.concatenate([jnp.where(lane_a == h, qa, 0.0) for h in range(H_A)], axis=0).astype(BF16)
    s_c = jnp.dot(qbd, akt.astype(BF16), preferred_element_type=F32)
    s_n = lax.dot_general(qbd, kvn_p[:, :A_WIDTH], (((1,), (1,)), ((), ())), preferred_element_type=F32)
    s = jnp.concatenate([s_c, s_n], axis=1) + cba
    m = jnp.max(s, axis=-1, keepdims=True)
    p = jnp.exp2(s - m)
    l = jnp.sum(p, axis=-1, keepdims=True)
    pb = p.astype(BF16)
    o_n = jnp.dot(pb[:, WIN_A:], kvn_p[:, A_WIDTH:2 * A_WIDTH], preferred_element_type=F32)
    o_all = lax.dot_general(pb[:, :WIN_A], avt.astype(BF16), (((1,), (1,)), ((), ())),
                            preferred_element_type=F32) + o_n
    o_sel = jnp.zeros((t, A_WIDTH), F32)
    l_b = jnp.ones((t, A_WIDTH), F32)
    for h in range(H_A):
        sel = lane_a == h
        o_sel = jnp.where(sel, o_all[h * t:(h + 1) * t], o_sel)
        l_b = jnp.where(sel, l[h * t:(h + 1) * t], l_b)
    oa = o_sel / l_b

    lane_b = lax.broadcasted_iota(jnp.int32, (G_B * t, LANES), 1)
    lo = lane_b < HEAD_DIM
    qb2 = jnp.concatenate([q[:, A_WIDTH + g * LANES:A_WIDTH + (g + 1) * LANES] for g in range(G_B)], axis=0)
    qm = jnp.concatenate([jnp.where(lo, qb2, 0.0), jnp.where(lo, 0.0, qb2)], axis=0).astype(BF16)
    kb_n = kvn_p[:, 2 * A_WIDTH:2 * A_WIDTH + LANES]
    vb_n = kvn_p[:, 2 * A_WIDTH + LANES:]
    sb_c = jnp.dot(qm, bkt.astype(BF16), preferred_element_type=F32)
    sb_n = lax.dot_general(qm, kb_n, (((1,), (1,)), ((), ())), preferred_element_type=F32)
    sb = jnp.concatenate([sb_c, sb_n], axis=1) + cbb
    mb = jnp.maximum(jnp.max(sb, axis=-1, keepdims=True), sink)
    pbb = jnp.exp2(sb - mb)
    den = jnp.sum(pbb, axis=-1, keepdims=True) + jnp.exp2(sink - mb)
    pbb = pbb.astype(BF16)
    ob = lax.dot_general(pbb[:, :WIN_B], bvt.astype(BF16), (((1,), (1,)), ((), ())),
                         preferred_element_type=F32)
    ob = (ob + jnp.dot(pbb[:, WIN_B:], vb_n, preferred_element_type=F32)) / den
    half = G_B * t
    lo8 = lo[:t]
    ob = jnp.concatenate([jnp.where(lo8, ob[g * t:(g + 1) * t], ob[half + g * t:half + (g + 1) * t])
                          for g in range(G_B)], axis=1)
    return oa, ob


def _route(logits):
    lane = lax.broadcasted_iota(jnp.int32, logits.shape, 1).astype(F32)
    big = jnp.float32(1 << 20)
    ninf = jnp.float32(-jnp.inf)
    gmask = lane < N_GROUPS
    lg = jnp.where(gmask, logits, ninf)
    gmax = jnp.max(lg, axis=-1, keepdims=True)
    grp = jnp.min(jnp.where(lg == gmax, lane, big), axis=-1, keepdims=True)
    pg_top = 1.0 / jnp.sum(jnp.exp(lg - gmax), axis=-1, keepdims=True)
    e0 = N_GROUPS + grp * EXPERTS_PER_GROUP
    emask = jnp.logical_and(lane >= e0, lane < e0 + EXPERTS_PER_GROUP)
    le = jnp.where(emask, logits, ninf)
    emax = jnp.max(le, axis=-1, keepdims=True)
    esum = jnp.sum(jnp.exp(le - emax), axis=-1, keepdims=True)
    i1 = jnp.min(jnp.where(le == emax, lane, big), axis=-1, keepdims=True)
    le2 = jnp.where(lane == i1, ninf, le)
    e2max = jnp.max(le2, axis=-1, keepdims=True)
    i2 = jnp.min(jnp.where(le2 == e2max, lane, big), axis=-1, keepdims=True)
    p1 = 1.0 / esum
    p2 = jnp.exp(e2max - emax) / esum
    g1 = pg_top * p1 / (p1 + p2)
    g2 = pg_top * p2 / (p1 + p2)
    out = jnp.where(lane == 0, i1 - N_GROUPS, 0.0)
    out = jnp.where(lane == 1, i2 - N_GROUPS, out)
    out = jnp.where(lane == 2, g1, out)
    out = jnp.where(lane == 3, g2, out)
    return out


def _pack_bf16_pairs(x):
    half = x.shape[1] // 2

    def rne(v):
        bits = lax.bitcast_convert_type(v, jnp.int32)
        return bits + 0x7FFF + (lax.shift_right_logical(bits, 16) & 1)

    lo = lax.shift_right_logical(rne(x[:, :half]), 16)
    hi = rne(x[:, half:]) & jnp.int32(-65536)
    return lo | hi


def _unpack_bf16_pairs(w):
    lo = lax.bitcast_convert_type(lax.shift_left(w, 16), F32)
    hi = lax.bitcast_convert_type(w & jnp.int32(-65536), F32)
    return jnp.concatenate([lo, hi], axis=1)


def _out_router_kernel(xp_ref, ap_ref, bp_ref, xs_ref, q_ref, kvn_ref, akt_hbm, avt_hbm, bkt_ref, bvt_ref,
                       cba_ref, cbb_ref, sink_ref, wo_ref, g_ref, wr_ref, br_ref,
                       x1_ref, xn_ref, route_ref, cnt_ref,
                       xcat_scr, mix_scr, kbuf, vbuf, sem, *, prompt_tiles, decode_tiles, seqs_per_step):
    i = pl.program_id(0)
    seqs = prompt_tiles * seqs_per_step

    def cache_copies(n, slot):
        return (pltpu.make_async_copy(akt_hbm.at[n], kbuf.at[slot], sem.at[0, slot]),
                pltpu.make_async_copy(avt_hbm.at[n], vbuf.at[slot], sem.at[1, slot]))

    @pl.when(i == 0)
    def _():
        cnt_ref[...] = jnp.zeros_like(cnt_ref)
        xcat_scr[...] = jnp.zeros_like(xcat_scr)
        for n0 in range(2):
            for c in cache_copies(n0, n0):
                c.start()

    @pl.when(i == prompt_tiles)
    def _():
        for n1 in (seqs, seqs + 1):
            for c in cache_copies(seqs - 1, n1 % CACHE_BUFFERS):
                c.wait()

    pslot = lax.rem(i, 2)

    def route_previous():
        logits = jnp.dot(xcat_scr[1 - pslot], wr_ref[...], preferred_element_type=F32)
        route = _route(logits + br_ref[...])
        route_ref[...] = route
        lanef = lax.broadcasted_iota(jnp.int32, route.shape, 1).astype(F32)
        hits = (lanef == route[:, 0:1]).astype(F32) + (lanef == route[:, 1:2]).astype(F32)
        cnt_ref[...] += jnp.sum(hits, axis=0, keepdims=True) * (i > 0).astype(F32)

    def project(x_ref, mix):
        x1 = x_ref[...] + jnp.dot(mix, wo_ref[...], preferred_element_type=F32)
        x1_ref[...] = x1
        ms = jnp.mean(x1 * x1, axis=-1, keepdims=True)
        xn = x1 * lax.rsqrt(ms + EPS) * g_ref[...]
        xn_ref[...] = _pack_bf16_pairs(xn)
        xh = xn.astype(BF16)
        xl = (xn - xh.astype(F32)).astype(BF16)
        xcat_scr[pslot] = jnp.concatenate([xh, xl, xh], axis=1)

    @pl.when(i < prompt_tiles)
    def _():
        route_previous()
        mix = jnp.concatenate([ap_ref[0], ap_ref[1], ap_ref[2], ap_ref[3]], axis=1).astype(BF16)
        project(xp_ref, jnp.concatenate([mix, bp_ref[...]], axis=1))
        t = q_ref.shape[1]
        for s in range(seqs_per_step):
            n = i * seqs_per_step + s
            slot = lax.rem(n, CACHE_BUFFERS)
            for c in cache_copies(n, slot):
                c.wait()
            for c in cache_copies(jnp.minimum(n + 2, seqs - 1), lax.rem(n + 2, CACHE_BUFFERS)):
                c.start()
            oa, ob = _sample_attention(q_ref[s], kvn_ref[s], kbuf[slot], vbuf[slot], bkt_ref[s], bvt_ref[s],
                                       cba_ref[...], cbb_ref[...], sink_ref[...])
            row = pl.multiple_of(n * t, t)
            mix_scr[pl.ds(row, t), :A_WIDTH] = oa
            mix_scr[pl.ds(row, t), A_WIDTH:] = ob

    @pl.when(i >= prompt_tiles)
    def _():
        route_previous()
        tm = xs_ref.shape[0]
        row = pl.multiple_of(jnp.clip(i - prompt_tiles, 0, decode_tiles - 1) * tm, tm)
        project(xs_ref, mix_scr[pl.ds(row, tm), :].astype(BF16))


def _out_router(xp, a4p, bp, xs, q3, kvn3, akt, avt, bkt, bvt, cbias_a, cbias_b, sink_rows, wo, gamma, wr, br):
    tp, tsm = xp.shape[0], xs.shape[0]
    ns, ts = q3.shape[0], q3.shape[1]
    tm = 512
    npt, nst = tp // tm, tsm // tm
    nt = npt + nst
    t = tp + tsm
    sps = ns // npt
    assert sps * npt == ns and ns * ts == tsm and ns >= CACHE_BUFFERS
    pmap = lambda i: (jnp.minimum(i, npt - 1), 0)
    pmap3 = lambda i: (jnp.minimum(i, npt - 1), 0, 0)
    smap = lambda i: (jnp.clip(i - npt, 0, nst - 1), 0)
    cur = lambda i: (jnp.minimum(i, nt - 1), 0)
    const = lambda i: (0, 0)
    return pl.pallas_call(
        functools.partial(_out_router_kernel, prompt_tiles=npt, decode_tiles=nst, seqs_per_step=sps),
        grid=(nt + 1,),
        in_specs=[
            pl.BlockSpec((tm, D_MODEL), pmap),
            pl.BlockSpec((4, tm, LANES), lambda i: (0, jnp.minimum(i, npt - 1), 0)),
            pl.BlockSpec((tm, 512), pmap),
            pl.BlockSpec((tm, D_MODEL), smap),
            pl.BlockSpec((sps, ts, 1024), pmap3),
            pl.BlockSpec((sps, ts, 1280), pmap3),
            pl.BlockSpec(memory_space=pl.ANY),
            pl.BlockSpec(memory_space=pl.ANY),
            pl.BlockSpec((sps, LANES, WIN_B), pmap3),
            pl.BlockSpec((sps, LANES, WIN_B), pmap3),
            pl.BlockSpec((H_A * ts, WIN_A + LANES), const),
            pl.BlockSpec((H_B * ts, WIN_B + LANES), const),
            pl.BlockSpec((H_B * ts, 1), const),
            pl.BlockSpec((D_MODEL, D_MODEL), const),
            pl.BlockSpec((1, D_MODEL), const),
            pl.BlockSpec((3 * D_MODEL, LANES), const),
            pl.BlockSpec((1, LANES), const),
        ],
        out_specs=[
            pl.BlockSpec((tm, D_MODEL), cur),
            pl.BlockSpec((tm, D_MODEL // 2), cur),
            pl.BlockSpec((tm, LANES), lambda i: (jnp.maximum(i - 1, 0), 0)),
            pl.BlockSpec((1, LANES), const),
        ],
        scratch_shapes=[pltpu.VMEM((2, tm, 3 * D_MODEL), BF16), pltpu.VMEM((tsm, D_MODEL), F32),
                        pltpu.VMEM((CACHE_BUFFERS, A_WIDTH, WIN_A), F32),
                        pltpu.VMEM((CACHE_BUFFERS, A_WIDTH, WIN_A), F32),
                        pltpu.SemaphoreType.DMA((2, CACHE_BUFFERS))],
        out_shape=[
            jax.ShapeDtypeStruct((t, D_MODEL), F32),
            jax.ShapeDtypeStruct((t, D_MODEL // 2), jnp.int32),
            jax.ShapeDtypeStruct((t, LANES), F32),
            jax.ShapeDtypeStruct((1, LANES), F32),
        ],
        compiler_params=_cparams(("arbitrary",), vmem=OUT_ROUTER_VMEM),
        name="out_router",
    )(xp, a4p, bp, xs, q3, kvn3, akt, avt, bkt, bvt, cbias_a, cbias_b, sink_rows, wo, gamma, wr, br)


def _sc_gather_rows(table, idx):
    b = idx.shape[0]
    d = table.shape[1]
    w = SC_WINDOW
    per_worker = b // SC_WORKERS
    nwin = per_worker // w
    assert per_worker * SC_WORKERS == b and nwin * w == per_worker
    mesh = plsc.VectorSubcoreMesh(core_axis_name="c", subcore_axis_name="s")

    @functools.partial(
        pl.kernel, mesh=mesh,
        out_type=jax.ShapeDtypeStruct((b, d), table.dtype),
        scratch_types=[pltpu.VMEM((nwin, w), jnp.int32), pltpu.VMEM((2, w, d), table.dtype),
                       pltpu.SemaphoreType.DMA((2,)), pltpu.SemaphoreType.DMA((2,))],
        name="sc_gather_rows",
    )
    def gather(table_hbm, idx_hbm, out_hbm, idx_v, rows_v, sem_in, sem_out):
        wid = lax.axis_index("s") * SC_CORES + lax.axis_index("c")
        base = wid * per_worker
        pltpu.sync_copy(idx_hbm.at[wid], idx_v)

        def fetch(j):
            return pltpu.make_async_copy(table_hbm.at[idx_v.at[j]], rows_v.at[j % 2], sem_in.at[j % 2])

        def flush(j):
            return pltpu.make_async_copy(rows_v.at[j % 2], out_hbm.at[pl.ds(base + j * w, w)],
                                         sem_out.at[j % 2])

        fetch(0).start()
        for j in range(nwin):
            fetch(j).wait()
            if j + 1 < nwin:
                if j >= 1:
                    flush(j - 1).wait()
                fetch(j + 1).start()
            flush(j).start()
        for j in range(max(nwin - 2, 0), nwin):
            flush(j).wait()

    return gather(table, idx.reshape(SC_WORKERS, nwin, w))


def _sc_scatter_rows(x, dest2, nrows):
    t, d = x.shape
    w = SC_SCATTER_WINDOW
    per_worker = t // SC_WORKERS
    nwin = per_worker // w
    assert per_worker * SC_WORKERS == t and nwin * w == per_worker
    mesh = plsc.VectorSubcoreMesh(core_axis_name="c", subcore_axis_name="s")

    @functools.partial(
        pl.kernel, mesh=mesh,
        out_type=jax.ShapeDtypeStruct((nrows, d), x.dtype),
        scratch_types=[pltpu.VMEM((TOP_K, nwin, w), jnp.int32), pltpu.VMEM((2, w, d), x.dtype),
                       pltpu.SemaphoreType.DMA((2,)), pltpu.SemaphoreType.DMA((2,))],
        name="sc_scatter_rows",
    )
    def scatter(x_hbm, dest_hbm, out_hbm, idx_v, rows_v, sem_in, sem_out):
        wid = lax.axis_index("s") * SC_CORES + lax.axis_index("c")
        base = wid * per_worker
        for k in range(TOP_K):
            pltpu.sync_copy(dest_hbm.at[k, wid], idx_v.at[k])

        def fetch(j):
            return pltpu.make_async_copy(x_hbm.at[pl.ds(base + j * w, w)], rows_v.at[j % 2], sem_in.at[j % 2])

        def spread(j, k):
            return pltpu.make_async_copy(rows_v.at[j % 2], out_hbm.at[idx_v.at[k, j]], sem_out.at[j % 2])

        fetch(0).start()
        for j in range(nwin):
            fetch(j).wait()
            if j + 1 < nwin:
                if j >= 1:
                    for k in range(TOP_K):
                        spread(j - 1, k).wait()
                fetch(j + 1).start()
            for k in range(TOP_K):
                spread(j, k).start()
        for j in range(max(nwin - 2, 0), nwin):
            for k in range(TOP_K):
                spread(j, k).wait()

    return scatter(x, dest2.reshape(TOP_K, SC_WORKERS, nwin, w))


def _expert_kernel(be_ref, nu_ref, nv_ref, nx_ref, x_ref, wg_hbm, wu_hbm, wd_hbm, o_ref,
                   wg_s, wu_s, wd_s, wg_f, wu_f, wd_f, slot_s, sem):
    i = pl.program_id(0)
    used = i < nu_ref[0]
    changed = jnp.logical_or(i == 0, be_ref[i] != be_ref[jnp.maximum(i - 1, 0)])

    def weight_copies(e, slot):
        return (pltpu.make_async_copy(wg_hbm.at[e], wg_f.at[slot], sem.at[slot, 0]),
                pltpu.make_async_copy(wu_hbm.at[e], wu_f.at[slot], sem.at[slot, 1]),
                pltpu.make_async_copy(wd_hbm.at[e], wd_f.at[slot], sem.at[slot, 2]))

    @pl.when(i == 0)
    def _():
        slot_s[0] = 0
        for c in weight_copies(be_ref[0], 0):
            c.start()

    @pl.when(jnp.logical_and(used, changed))
    def _():
        slot = slot_s[0]
        for c in weight_copies(be_ref[i], slot):
            c.wait()
        wg_s[...] = wg_f[slot].astype(BF16)
        wu_s[...] = wu_f[slot].astype(BF16)
        wd_s[...] = wd_f[slot].astype(BF16)

        @pl.when(nx_ref[i] != be_ref[i])
        def _():
            for c in weight_copies(nx_ref[i], 1 - slot):
                c.start()

        slot_s[0] = 1 - slot

    @pl.when(used)
    def _():
        row = lax.broadcasted_iota(jnp.int32, x_ref.shape, 0)
        x = _unpack_bf16_pairs(jnp.where(row < nv_ref[i], x_ref[...], 0)).astype(BF16)
        gate = jnp.dot(x, wg_s[...], preferred_element_type=F32)
        up = jnp.dot(x, wu_s[...], preferred_element_type=F32)
        h = (gate * jax.nn.sigmoid(gate) * up).astype(BF16)
        o_ref[...] = _pack_bf16_pairs(jnp.dot(h, wd_s[...], preferred_element_type=F32))

    @pl.when(jnp.logical_not(used))
    def _():
        o_ref[...] = jnp.zeros_like(o_ref)


def _experts(blk_e, n_used, nvalid, next_e, xb, w_gate, w_up, w_down):
    rows = xb.shape[0]
    nblocks = rows // MOE_ROWS
    grid_spec = pltpu.PrefetchScalarGridSpec(
        num_scalar_prefetch=4,
        grid=(nblocks,),
        in_specs=[
            pl.BlockSpec((MOE_ROWS, D_MODEL // 2), lambda i, be, nu, nv, nx: (i, 0)),
            pl.BlockSpec(memory_space=pl.ANY),
            pl.BlockSpec(memory_space=pl.ANY),
            pl.BlockSpec(memory_space=pl.ANY),
        ],
        out_specs=pl.BlockSpec((MOE_ROWS, D_MODEL // 2), lambda i, be, nu, nv, nx: (i, 0)),
        scratch_shapes=[pltpu.VMEM((D_MODEL, D_EXPERT), BF16), pltpu.VMEM((D_MODEL, D_EXPERT), BF16),
                        pltpu.VMEM((D_EXPERT, D_MODEL), BF16),
                        pltpu.VMEM((2, D_MODEL, D_EXPERT), F32), pltpu.VMEM((2, D_MODEL, D_EXPERT), F32),
                        pltpu.VMEM((2, D_EXPERT, D_MODEL), F32),
                        pltpu.SMEM((1,), jnp.int32), pltpu.SemaphoreType.DMA((2, 3))],
    )
    return pl.pallas_call(
        _expert_kernel,
        grid_spec=grid_spec,
        out_shape=jax.ShapeDtypeStruct((rows, D_MODEL // 2), jnp.int32),
        compiler_params=_cparams(("arbitrary",)),
        name="experts",
    )(blk_e, n_used, nvalid, next_e, xb, w_gate, w_up, w_down)


def _combine_kernel(x1_ref, y1_ref, y2_ref, route_ref, g_ref, outp_ref, outs_ref, *, prompt_tiles):
    r = route_ref[...]
    x = (x1_ref[...] + r[:, 2:3] * _unpack_bf16_pairs(y1_ref[...])
         + r[:, 3:4] * _unpack_bf16_pairs(y2_ref[...]))
    ms = jnp.mean(x * x, axis=-1, keepdims=True)
    y = x * lax.rsqrt(ms + EPS) * g_ref[...]
    i = pl.program_id(0)

    @pl.when(i < prompt_tiles)
    def _():
        outp_ref[...] = y

    @pl.when(i >= prompt_tiles)
    def _():
        outs_ref[...] = y


def _combine_norm(x1, ygath, route, gamma, tp):
    t = x1.shape[0]
    tm = 512
    nt, npt = t // tm, tp // tm
    return pl.pallas_call(
        functools.partial(_combine_kernel, prompt_tiles=npt),
        grid=(nt,),
        in_specs=[
            pl.BlockSpec((tm, D_MODEL), lambda i: (i, 0)),
            pl.BlockSpec((tm, D_MODEL // 2), lambda i: (i, 0)),
            pl.BlockSpec((tm, D_MODEL // 2), lambda i: (i + nt, 0)),
            pl.BlockSpec((tm, LANES), lambda i: (i, 0)),
            pl.BlockSpec((1, D_MODEL), lambda i: (0, 0)),
        ],
        out_specs=[
            pl.BlockSpec((tm, D_MODEL), lambda i: (jnp.minimum(i, npt - 1), 0)),
            pl.BlockSpec((tm, D_MODEL), lambda i: (jnp.maximum(i - npt, 0), 0)),
        ],
        out_shape=[jax.ShapeDtypeStruct((tp, D_MODEL), F32), jax.ShapeDtypeStruct((t - tp, D_MODEL), F32)],
        compiler_params=_cparams(("arbitrary",)),
        name="combine_norm",
    )(x1, ygath, ygath, route, gamma)


def _band_index():
    c = (2 * QB - np.arange(2 * QB)) % (2 * QB)
    return c, c <= QB


def _bias_a_prompt(table_a):
    c, valid = _band_index()
    idx = np.stack([_t5_bucket_np(d * np.clip(QB - c, 0, QB)) for d in DILATIONS])
    return jnp.where(valid, jnp.transpose(table_a[idx], (0, 2, 1)) * LOG2E, NEG)


def _bias_b_prompt(table_b):
    c, valid = _band_index()
    valid = valid & (c >= 1)
    h = jnp.where(valid, table_b[_t5_bucket_np(np.clip(QB - c, 0, QB))].T * LOG2E, NEG)
    return jnp.transpose(h.reshape(KV_B, G_B, 2 * QB), (1, 0, 2)).reshape(H_B, 2 * QB)


def _sample_bias(table, span, t, log2_weight):
    cols = span + LANES
    period = cols + LANES
    x = np.arange(period)
    dist = np.where(x >= period - t, span - x + period, span - x)
    extra = log2_weight(dist)
    valid = np.isfinite(extra)
    u = jnp.where(valid, table[_t5_bucket_np(np.maximum(dist, 0))].T * LOG2E
                  + np.where(valid, extra, 0.0).astype(np.float32), NEG)
    rows = jnp.tile(u, (1, t))[:, :t * (period - 1)].reshape(u.shape[0], t, period - 1)[:, :, :cols]
    return rows.reshape(u.shape[0] * t, cols)


def _bias_a_sample(table_a, t):
    def log2_count(dist):
        count = np.zeros(dist.shape, np.int64)
        for w, d in zip(WINDOWS, DILATIONS):
            count += (dist >= 0) & (dist % d == 0) & (dist <= w)
        return np.where(count > 0, np.log2(np.maximum(count, 1)), -np.inf)

    return _sample_bias(table_a, WIN_A, t, log2_count)


def _bias_b_sample(table_b, t):
    return _sample_bias(table_b, WIN_B, t,
                        lambda dist: np.where((dist >= 0) & (dist < WIN_B), 0.0, -np.inf))


def _dest_kernel(route_ref, cnt_ref, tri_ref, dest_ref, meta_ref, run_scr, pst_scr):
    i = pl.program_id(0)
    tm = route_ref.shape[0]
    r = route_ref[...]
    lane = lax.broadcasted_iota(jnp.int32, (tm, LANES), 1)
    lanef = lane.astype(F32)
    oh0 = lanef == r[:, 0:1]
    oh1 = lanef == r[:, 1:2]
    ohf = jnp.concatenate([oh0, oh1], axis=0).astype(F32)

    @pl.when(i == 0)
    def _():
        cnt = jnp.broadcast_to(cnt_ref[...], (LANES, LANES))
        padded = jnp.floor((cnt + (MOE_ROWS - 1)) * (1.0 / MOE_ROWS)) * MOE_ROWS
        lane_e = lax.broadcasted_iota(jnp.int32, (LANES, LANES), 1)
        x = padded
        for sh in (1, 2, 4, 8, 16, 32, 64):
            x = x + jnp.where(lane_e >= sh, pltpu.roll(x, sh, 1), 0.0)
        pst_scr[...] = (x - padded)[0:1]
        run_scr[...] = jnp.zeros_like(run_scr)
        wide = lambda v: jnp.concatenate([v.T, v.T], axis=1)
        cnt_t, bend_t = wide(cnt), wide(x * (1.0 / MOE_ROWS))
        bstart_t = wide((x - padded) * (1.0 / MOE_ROWS))
        blk = lax.broadcasted_iota(jnp.int32, (LANES, 2 * LANES), 1).astype(F32)
        exp = lax.broadcasted_iota(jnp.int32, (LANES, 2 * LANES), 0)
        real = exp < N_EXPERTS
        blk_e = jnp.minimum(jnp.sum(jnp.where(real & (bend_t <= blk), 1.0, 0.0), axis=0, keepdims=True),
                            N_EXPERTS - 1.0)
        mine = exp.astype(F32) == blk_e
        within = blk[0:1] - jnp.sum(jnp.where(mine, bstart_t, 0.0), axis=0, keepdims=True)
        nvalid = jnp.clip(jnp.sum(jnp.where(mine, cnt_t, 0.0), axis=0, keepdims=True) - within * MOE_ROWS,
                          0.0, float(MOE_ROWS))
        n_used = jnp.max(jnp.where(real, bend_t, 0.0), axis=0, keepdims=True)
        later = real & (exp.astype(F32) > blk_e) & (cnt_t > 0.0)
        nxt = jnp.min(jnp.where(later, exp.astype(F32), float(LANES)), axis=0, keepdims=True)
        nxt = jnp.where(nxt >= N_EXPERTS, blk_e, nxt)
        meta_ref[...] = jnp.concatenate([blk_e, nvalid, n_used, nxt, jnp.zeros((4, 2 * LANES), F32)],
                                        axis=0).astype(jnp.int32)

    base = run_scr[...] + pst_scr[...] - 1.0
    vals = []
    for c in range(2 * tm // LANES):
        ohc = ohf[c * LANES:(c + 1) * LANES]
        vals.append(jnp.dot(tri_ref[...], ohc.astype(BF16), preferred_element_type=F32) + base)
        base = base + jnp.sum(ohc, axis=0, keepdims=True)
    val = jnp.concatenate(vals, axis=0)
    d0 = jnp.sum(jnp.where(oh0, val[:tm], 0.0), axis=-1, keepdims=True)
    d1 = jnp.sum(jnp.where(oh1, val[tm:], 0.0), axis=-1, keepdims=True)
    tile = jnp.where(lane == 0, d0, jnp.where(lane == 1, d1, 0.0))
    dest_ref[...] = tile.T[:8].astype(jnp.int32)
    run_scr[...] += jnp.sum(ohf, axis=0, keepdims=True)


def _dispatch(route, cnt):
    t = route.shape[0]
    tm = 512
    tri = (jnp.arange(LANES)[:, None] >= jnp.arange(LANES)[None, :]).astype(BF16)
    nblocks = -(-t * TOP_K // MOE_ROWS) + N_EXPERTS
    assert nblocks <= 2 * LANES
    dest, meta = pl.pallas_call(
        _dest_kernel,
        grid=(t // tm,),
        in_specs=[pl.BlockSpec((tm, LANES), lambda i: (i, 0)),
                  pl.BlockSpec((1, LANES), lambda i: (0, 0)),
                  pl.BlockSpec((LANES, LANES), lambda i: (0, 0))],
        out_specs=[pl.BlockSpec((8, tm), lambda i: (0, i)),
                   pl.BlockSpec((8, 2 * LANES), lambda i: (0, 0))],
        out_shape=[jax.ShapeDtypeStruct((8, t), jnp.int32), jax.ShapeDtypeStruct((8, 2 * LANES), jnp.int32)],
        scratch_shapes=[pltpu.VMEM((1, LANES), F32), pltpu.VMEM((1, LANES), F32)],
        compiler_params=_cparams(("arbitrary",)),
        name="moe_dest",
    )(route, cnt, tri)
    return dest[:TOP_K], meta[0, :nblocks], meta[2, :1], meta[1, :nblocks], meta[3, :nblocks]


def kernel(x_prompt, x_sample, cache_a_k, cache_a_v, cache_b_k, cache_b_v, rel_bias_table, attn_norm, w_in,
           w_out, attn_sinks, ffn_norm, w_router_group, b_router_group, w_router_expert, b_router_expert,
           w_gate, w_up, w_down, final_norm):
    s = x_prompt.shape[1]
    ns, ts = x_sample.shape[0], x_sample.shape[1]
    table_a = rel_bias_table[:, :H_A]
    table_b = rel_bias_table[:, H_A:]

    w = w_in[0]
    wqa, wka, wva, wqb, wkb, wvb = (w[:, 0:512], w[:, 512:1024], w[:, 1024:1536], w[:, 1536:2048],
                                    w[:, 2048:2176], w[:, 2176:2304])
    wqb = jnp.transpose(wqb.reshape(D_MODEL, KV_B, G_B, HEAD_DIM), (0, 2, 1, 3)).reshape(D_MODEL, 512)
    wp = jnp.concatenate([wka, wva, wqa, wqb, wkb, wvb], axis=1).astype(BF16)
    cscale = jnp.concatenate([jnp.ones((1, 1024), F32), jnp.full((1, 1024), SCALE * LOG2E, F32),
                              jnp.ones((1, 256), F32)], axis=1)
    wo = w_out[0]
    wo_b = jnp.transpose(wo[512:].reshape(KV_B, G_B, HEAD_DIM, D_MODEL), (1, 0, 2, 3)).reshape(512, D_MODEL)
    wo_p = jnp.concatenate([wo[:512], wo_b], axis=0).astype(BF16)
    wr = jnp.concatenate([w_router_group[0],
                          jnp.transpose(w_router_expert[0], (1, 0, 2)).reshape(D_MODEL, N_EXPERTS),
                          jnp.zeros((D_MODEL, LANES - N_GROUPS - N_EXPERTS), F32)], axis=1)
    wr_hi = wr.astype(BF16)
    wr = jnp.concatenate([wr_hi, wr_hi, (wr - wr_hi.astype(F32)).astype(BF16)], axis=0)
    br = jnp.concatenate([b_router_group[0], b_router_expert[0].reshape(N_EXPERTS),
                          jnp.zeros((LANES - N_GROUPS - N_EXPERTS,), F32)]).reshape(1, LANES)
    sinks2 = attn_sinks[0] * LOG2E
    sinks_gk = jnp.transpose(sinks2.reshape(KV_B, G_B), (1, 0)).reshape(H_B)
    sink_rows_p = jnp.repeat(sinks_gk, QB).reshape(G_B, 1, 2 * QB)
    sink_rows_s = jnp.repeat(sinks2, ts).reshape(H_B * ts, 1)
    emat = jnp.tile(jnp.arange(LANES)[:, None] == (jnp.arange(A_WIDTH)[None, :] // HEAD_DIM),
                    (3, 1)).astype(BF16)
    attn_g = attn_norm[0].reshape(1, D_MODEL)
    ffn_g = ffn_norm[0].reshape(1, D_MODEL)

    xp = x_prompt.reshape(s, D_MODEL)
    aperm, qb_p, kvb_p, akv32, bkv32 = _proj_prompt(xp, attn_g, wp, cscale)
    a4 = _attn_a_prompt(aperm, _bias_a_prompt(table_a), emat)
    ob_p = _attn_b_prompt(qb_p, kvb_p, _bias_b_prompt(table_b), sink_rows_p)

    xs = x_sample.reshape(ns * ts, D_MODEL)
    q_s, kv_s = _proj_sample(xs, attn_g, wp, cscale)
    akt = jnp.transpose(cache_a_k[0], (0, 2, 3, 1)).reshape(ns, A_WIDTH, WIN_A)
    avt = jnp.transpose(cache_a_v[0], (0, 2, 3, 1)).reshape(ns, A_WIDTH, WIN_A)
    bkt = jnp.transpose(cache_b_k[0], (0, 2, 3, 1)).reshape(ns, LANES, WIN_B)
    bvt = jnp.transpose(cache_b_v[0], (0, 2, 3, 1)).reshape(ns, LANES, WIN_B)

    x1, xn, route, cnt = _out_router(xp, a4, ob_p, xs, q_s.reshape(ns, ts, 1024), kv_s.reshape(ns, ts, 1280),
                                     akt, avt, bkt, bvt, _bias_a_sample(table_a, ts),
                                     _bias_b_sample(table_b, ts), sink_rows_s, wo_p, ffn_g, wr, br)
    dest2, blk_e, n_used, nvalid, next_e = _dispatch(route, cnt)
    xb = _sc_scatter_rows(xn, dest2, blk_e.shape[0] * MOE_ROWS)
    yb = _experts(blk_e, n_used, nvalid, next_e, xb, w_gate[0], w_up[0], w_down[0])
    y_p, y_s = _combine_norm(x1, _sc_gather_rows(yb, dest2.reshape(-1)), route, final_norm.reshape(1, D_MODEL), s)

    y_prompt = y_p.reshape(1, s, D_MODEL)
    y_sample = y_s.reshape(ns, ts, D_MODEL)
    keep_a, keep_b = min(WIN_A, s), min(WIN_B, s)
    pak = akv32[SPAN - keep_a:, :512].reshape(1, 1, keep_a, H_A, HEAD_DIM)
    pav = akv32[SPAN - keep_a:, 512:].reshape(1, 1, keep_a, H_A, HEAD_DIM)
    pbk = bkv32[SPAN - keep_b:, :128].reshape(1, 1, keep_b, KV_B, HEAD_DIM)
    pbv = bkv32[SPAN - keep_b:, 128:].reshape(1, 1, keep_b, KV_B, HEAD_DIM)
    sak = kv_s[:, 0:512].reshape(1, ns, ts, H_A, HEAD_DIM)
    sav = kv_s[:, 512:1024].reshape(1, ns, ts, H_A, HEAD_DIM)
    sbk = kv_s[:, 1024:1152].reshape(1, ns, ts, KV_B, HEAD_DIM)
    sbv = kv_s[:, 1152:1280].reshape(1, ns, ts, KV_B, HEAD_DIM)
    return (y_prompt, y_sample, pak, pav, pbk, pbv, sak, sav, sbk, sbv)
```

```python
import functools
import math

import jax
import jax.numpy as jnp
import numpy as np
from jax import lax
from jax.experimental import pallas as pl
from jax.experimental.pallas import tpu as pltpu
from jax.experimental.pallas import tpu_sc as plsc

D_MODEL = 1024
HEAD_DIM = 64
H_A = 8
H_B = 8
KV_B = 2
G_B = 4
DILATIONS = (1, 4, 16)
WINDOWS = (128, 512, 2048)
WIN_A = 2048
WIN_B = 128
NUM_BUCKETS = 32
MAX_DISTANCE = 2048
N_GROUPS = 4
EXPERTS_PER_GROUP = 8
N_EXPERTS = 32
TOP_K = 2
D_EXPERT = 512
EPS = 1e-5
SCALE = HEAD_DIM ** -0.5

LANES = 128
SPAN = 2048
QB = 128
NCHUNK = 9
A_WIDTH = H_A * HEAD_DIM
MOE_ROWS = 512
SC_CORES = 2
SC_SUBCORES = 16
SC_WORKERS = SC_CORES * SC_SUBCORES
SC_WINDOW = 64
SC_SCATTER_WINDOW = 32
NEG = -1e30
LOG2E = math.log2(math.e)
B_STEP = 1024
V7X_VMEM_BYTES = 64 * 1024 * 1024
VMEM_LIMIT = V7X_VMEM_BYTES - 8 * 1024 * 1024
CACHE_BUFFERS = 3
OUT_ROUTER_VMEM = V7X_VMEM_BYTES - 4 * 1024 * 1024

F32 = jnp.float32
BF16 = jnp.bfloat16


def _t5_bucket_np(dist):
    dist = np.asarray(dist, np.int64)
    max_exact = NUM_BUCKETS // 2
    d = np.maximum(dist, 1).astype(np.float32)
    ratio = np.log(d / np.float32(max_exact)) / np.float32(math.log(MAX_DISTANCE / max_exact))
    large = max_exact + (ratio * np.float32(NUM_BUCKETS - max_exact)).astype(np.int32)
    large = np.minimum(large, NUM_BUCKETS - 1)
    return np.where(dist < max_exact, dist, large).astype(np.int32)


def _cparams(sem, vmem=VMEM_LIMIT):
    return pltpu.CompilerParams(dimension_semantics=sem, vmem_limit_bytes=vmem)


def _proj_prompt_kernel(x_ref, g_ref, w_ref, cs_ref, aperm_ref, qb_ref, kvb_ref, akv_ref, bkv_ref,
                        h_scr, p_scr):
    n = pl.program_id(1)

    @pl.when(n == 0)
    def _():
        x = x_ref[...]
        ms = jnp.mean(x * x, axis=-1, keepdims=True)
        h_scr[...] = (x * lax.rsqrt(ms + EPS) * g_ref[...]).astype(BF16)

    p = jnp.dot(h_scr[...], w_ref[...], preferred_element_type=F32) * cs_ref[...]

    @pl.when(n < 6)
    def _():
        aperm_ref[0] = p.astype(BF16)
        p_scr[0, 0] = p[:, :LANES]
        p_scr[0, 1] = p[:, LANES:]
        quarter = SPAN // 4
        for r in range(4):
            lo = p_scr[0, 0, pl.ds(r, quarter, stride=4), :]
            hi = p_scr[0, 1, pl.ds(r, quarter, stride=4), :]
            p_scr[1, 0, r * quarter:(r + 1) * quarter, :] = lo
            p_scr[1, 1, r * quarter:(r + 1) * quarter, :] = hi
            aperm_ref[1, r * quarter:(r + 1) * quarter, :] = jnp.concatenate([lo, hi], axis=1).astype(BF16)
        for r16 in range(16):
            start = (r16 % 4) * quarter + r16 // 4
            t = jnp.concatenate([p_scr[1, 0, pl.ds(start, QB, stride=4), :],
                                 p_scr[1, 1, pl.ds(start, QB, stride=4), :]], axis=1)
            aperm_ref[2, r16 * QB:(r16 + 1) * QB, :] = t.astype(BF16)

    @pl.when(n < 4)
    def _():
        akv_ref[...] = p

    @pl.when(jnp.logical_or(n == 6, n == 7))
    def _():
        qb_ref[...] = p.astype(BF16)

    @pl.when(n == 8)
    def _():
        kvb_ref[...] = p.astype(BF16)
        bkv_ref[...] = p


def _proj_prompt(x, gamma, w, cscale):
    s = x.shape[0]
    nspan = s // SPAN
    return pl.pallas_call(
        _proj_prompt_kernel,
        grid=(nspan, NCHUNK),
        in_specs=[
            pl.BlockSpec((SPAN, D_MODEL), lambda b, n: (b, 0)),
            pl.BlockSpec((1, D_MODEL), lambda b, n: (0, 0)),
            pl.BlockSpec((D_MODEL, 256), lambda b, n: (0, n)),
            pl.BlockSpec((1, 256), lambda b, n: (0, n)),
        ],
        out_specs=[
            pl.BlockSpec((3, SPAN, 256), lambda b, n: (0, b, jnp.minimum(n, 5))),
            pl.BlockSpec((SPAN, 256), lambda b, n: (b, jnp.clip(n - 6, 0, 1))),
            pl.BlockSpec((SPAN, 256), lambda b, n: (b, 0)),
            pl.BlockSpec((SPAN, 256), lambda b, n: (0, jnp.where(b == nspan - 1, jnp.minimum(n, 3), 0))),
            pl.BlockSpec((SPAN, 256), lambda b, n: (0, 0)),
        ],
        out_shape=[
            jax.ShapeDtypeStruct((3, s, 3 * A_WIDTH), BF16),
            jax.ShapeDtypeStruct((s, 512), BF16),
            jax.ShapeDtypeStruct((s, 256), BF16),
            jax.ShapeDtypeStruct((SPAN, 1024), F32),
            jax.ShapeDtypeStruct((SPAN, 256), F32),
        ],
        scratch_shapes=[pltpu.VMEM((SPAN, D_MODEL), BF16), pltpu.VMEM((2, 2, SPAN, LANES), F32)],
        compiler_params=_cparams(("arbitrary", "arbitrary")),
        name="proj_prompt",
    )(x, gamma, w, cscale)


def _proj_sample_kernel(x_ref, g_ref, w_ref, cs_ref, q_ref, kv_ref):
    x = x_ref[...]
    ms = jnp.mean(x * x, axis=-1, keepdims=True)
    h = (x * lax.rsqrt(ms + EPS) * g_ref[...]).astype(BF16)
    p = jnp.dot(h, w_ref[...], preferred_element_type=F32) * cs_ref[...]
    kv_ref[:, :1024] = p[:, :1024]
    kv_ref[:, 1024:] = p[:, 2048:]
    q_ref[...] = p[:, 1024:2048]


def _proj_sample(x, gamma, w, cscale):
    t = x.shape[0]
    tm = 512
    return pl.pallas_call(
        _proj_sample_kernel,
        grid=(t // tm,),
        in_specs=[
            pl.BlockSpec((tm, D_MODEL), lambda i: (i, 0)),
            pl.BlockSpec((1, D_MODEL), lambda i: (0, 0)),
            pl.BlockSpec((D_MODEL, 2304), lambda i: (0, 0)),
            pl.BlockSpec((1, 2304), lambda i: (0, 0)),
        ],
        out_specs=[
            pl.BlockSpec((tm, 1024), lambda i: (i, 0)),
            pl.BlockSpec((tm, 1280), lambda i: (i, 0)),
        ],
        out_shape=[
            jax.ShapeDtypeStruct((t, 1024), F32),
            jax.ShapeDtypeStruct((t, 1280), F32),
        ],
        compiler_params=_cparams(("arbitrary",)),
        name="proj_sample",
    )(x, gamma, w, cscale)


def _spread_heads(w, e3_ref):
    hi = w.astype(BF16)
    r1 = w - hi.astype(F32)
    mid = r1.astype(BF16)
    low = (r1 - mid.astype(F32)).astype(BF16)
    return jnp.dot(jnp.concatenate([hi, mid, low], axis=1), e3_ref[...], preferred_element_type=F32)


def _pair_tile(q2, kk, vv, bias_t, lo, sink=None, normalize=True):
    zero = jnp.zeros_like(q2)
    qq = jnp.concatenate([jnp.where(lo, q2, zero), jnp.where(lo, zero, q2)], axis=0)
    st = lax.dot_general(kk, qq, (((1,), (1,)), ((), ())), preferred_element_type=F32)
    st = st + bias_t
    m = jnp.max(st, axis=0, keepdims=True)
    if sink is not None:
        m = jnp.maximum(m, sink)
    p = jnp.exp2(st - m)
    den = jnp.sum(p, axis=0, keepdims=True)
    if sink is not None:
        den = den + jnp.exp2(sink - m)
    pn = (p * (1.0 / den) if normalize else p).astype(BF16)
    o = lax.dot_general(pn, vv, (((0,), (0,)), ((), ())), preferred_element_type=F32)
    return jnp.where(lo, o[:QB], o[QB:]), m, den


def _fill_band_tiles(h_ref, bias_scr):
    nk = 2 * QB
    prev = lax.broadcasted_iota(jnp.int32, (nk, nk), 0) < QB
    for pair in range(h_ref.shape[0] // 2):
        halves = []
        for hh in range(2):
            row = h_ref[2 * pair + hh:2 * pair + hh + 1, :]
            band = pltpu.roll(jnp.broadcast_to(row, (nk, nk)), 0, 1, stride=1, stride_axis=0)
            halves.append(band[:, :QB])
        tile = jnp.concatenate(halves, axis=1)
        bias_scr[0, pair] = tile
        bias_scr[1, pair] = jnp.where(prev, NEG, tile)


def _attn_a_kernel(q_ref, kvc_ref, kvp_ref, h_ref, e_ref, out_ref, o_scr, st_scr, bias_scr):
    b = pl.program_id(0)
    g = pl.program_id(1)
    nblk = jnp.where(g == 0, 16, jnp.where(g == 1, 4, 1))
    lane = lax.broadcasted_iota(jnp.int32, (QB, LANES), 1)
    lo = lane < HEAD_DIM

    @pl.when(b == 0)
    def _():
        _fill_band_tiles(h_ref, bias_scr.at[g])

    bias_ref = bias_scr.at[g]

    for cb in range(SPAN // QB):
        first = lax.rem(jnp.int32(cb), nblk) == 0
        rows = slice(cb * QB, (cb + 1) * QB)
        prow_c = max(cb - 1, 0) * QB
        prow_p = pl.multiple_of(jnp.where(first, cb + nblk - 1, 0) * QB, QB)
        variant = jnp.logical_and(first, b == 0).astype(jnp.int32)
        maxes, dens = [], []
        for hp in range(4):
            ks = slice(hp * LANES, (hp + 1) * LANES)
            vs = slice(A_WIDTH + hp * LANES, A_WIDTH + (hp + 1) * LANES)
            kp = jnp.where(first, kvp_ref[pl.ds(prow_p, QB), ks], kvc_ref[prow_c:prow_c + QB, ks])
            vp = jnp.where(first, kvp_ref[pl.ds(prow_p, QB), vs], kvc_ref[prow_c:prow_c + QB, vs])
            kk = jnp.concatenate([kp, kvc_ref[rows, ks]], axis=0)
            vv = jnp.concatenate([vp, kvc_ref[rows, vs]], axis=0)
            o, m, den = _pair_tile(q_ref[rows, ks], kk, vv, bias_ref[variant, hp], lo, normalize=False)
            o_scr[g, hp, rows, :] = o
            maxes += [m[:, :QB], m[:, QB:]]
            dens += [den[:, :QB], den[:, QB:]]
        sm = jnp.concatenate(maxes + dens + [jnp.zeros((LANES - 2 * H_A, QB), F32)], axis=0)
        st_scr[g, rows, :] = sm.T

    @pl.when(g == 2)
    def _():
        def merge(c, carry):
            r2 = lax.rem(c, 4) * (SPAN // 4) + c // 4
            r3 = pl.multiple_of(c * QB, QB)
            s1 = st_scr[0, pl.ds(c, QB, stride=16), :]
            s2 = st_scr[1, pl.ds(r2, QB, stride=4), :]
            s3 = st_scr[2, pl.ds(r3, QB), :]
            mx = jnp.maximum(jnp.maximum(s1, s2), s3)
            w1 = jnp.exp2(s1 - mx)
            w2 = jnp.exp2(s2 - mx)
            w3 = jnp.exp2(s3 - mx)
            shift = LANES - H_A
            tot = (w1 * pltpu.roll(s1, shift, 1) + w2 * pltpu.roll(s2, shift, 1)
                   + w3 * pltpu.roll(s3, shift, 1))
            head_lane = lax.broadcasted_iota(jnp.int32, (QB, LANES), 1) < H_A
            a1 = _spread_heads(jnp.where(head_lane, w1 / tot, 0.0), e_ref)
            a2 = _spread_heads(jnp.where(head_lane, w2 / tot, 0.0), e_ref)
            a3 = _spread_heads(jnp.where(head_lane, w3 / tot, 0.0), e_ref)
            for hp in range(4):
                sl = slice(hp * LANES, (hp + 1) * LANES)
                o1 = o_scr[0, hp, pl.ds(c, QB, stride=16), :]
                o2 = o_scr[1, hp, pl.ds(r2, QB, stride=4), :]
                o3 = o_scr[2, hp, pl.ds(r3, QB), :]
                out_ref[hp, pl.ds(c, QB, stride=16), :] = a1[:, sl] * o1 + a2[:, sl] * o2 + a3[:, sl] * o3
            return carry

        lax.fori_loop(0, 16, merge, 0, unroll=4)


def _attn_a_prompt(aperm, bias_a, emat):
    s = aperm.shape[1]
    nspan = s // SPAN
    return pl.pallas_call(
        _attn_a_kernel,
        grid=(nspan, 3),
        in_specs=[
            pl.BlockSpec((None, SPAN, A_WIDTH), lambda b, g: (g, b, 2)),
            pl.BlockSpec((None, SPAN, 2 * A_WIDTH), lambda b, g: (g, b, 0)),
            pl.BlockSpec((None, SPAN, 2 * A_WIDTH), lambda b, g: (g, jnp.maximum(b - 1, 0), 0)),
            pl.BlockSpec((None, H_A, 2 * QB), lambda b, g: (g, 0, 0)),
            pl.BlockSpec((3 * LANES, A_WIDTH), lambda b, g: (0, 0)),
        ],
        out_specs=pl.BlockSpec((4, SPAN, LANES), lambda b, g: (0, b, 0)),
        out_shape=jax.ShapeDtypeStruct((4, s, LANES), F32),
        scratch_shapes=[pltpu.VMEM((3, 4, SPAN, LANES), F32), pltpu.VMEM((3, SPAN, LANES), F32),
                        pltpu.VMEM((3, 2, 4, 2 * QB, 2 * QB), F32)],
        compiler_params=_cparams(("arbitrary", "arbitrary")),
        name="attn_a_prompt",
    )(aperm, aperm, aperm, bias_a, emat)


def _attn_b_kernel(q_ref, kvc_ref, kvp_ref, h_ref, sink_ref, out_ref, bias_ref):
    i = pl.program_id(0)
    lane = lax.broadcasted_iota(jnp.int32, (QB, LANES), 1)
    lo = lane < HEAD_DIM

    @pl.when(i == 0)
    def _():
        _fill_band_tiles(h_ref, bias_ref)

    variant = (i == 0).astype(jnp.int32)
    for j in range(B_STEP // QB):
        rows = slice(j * QB, (j + 1) * QB)
        if j == 0:
            kp, vp = kvp_ref[:, :LANES], kvp_ref[:, LANES:]
        else:
            kp, vp = kvc_ref[(j - 1) * QB:j * QB, :LANES], kvc_ref[(j - 1) * QB:j * QB, LANES:]
        kk = jnp.concatenate([kp, kvc_ref[rows, :LANES]], axis=0)
        vv = jnp.concatenate([vp, kvc_ref[rows, LANES:]], axis=0)
        for g in range(G_B):
            bias_t = bias_ref[variant, g] if j == 0 else bias_ref[0, g]
            o, _, _ = _pair_tile(q_ref[rows, g * LANES:(g + 1) * LANES], kk, vv, bias_t, lo, sink=sink_ref[g])
            out_ref[rows, g * LANES:(g + 1) * LANES] = o.astype(BF16)


def _attn_b_prompt(qb, kvb, bias_b, sink_rows):
    s = qb.shape[0]
    per = B_STEP // QB
    return pl.pallas_call(
        _attn_b_kernel,
        grid=(s // B_STEP,),
        in_specs=[
            pl.BlockSpec((B_STEP, 512), lambda i: (i, 0)),
            pl.BlockSpec((B_STEP, 256), lambda i: (i, 0)),
            pl.BlockSpec((QB, 256), lambda i: (jnp.maximum(i * per - 1, 0), 0)),
            pl.BlockSpec((H_B, 2 * QB), lambda i: (0, 0)),
            pl.BlockSpec((G_B, 1, 2 * QB), lambda i: (0, 0, 0)),
        ],
        out_specs=pl.BlockSpec((B_STEP, 512), lambda i: (i, 0)),
        out_shape=jax.ShapeDtypeStruct((s, 512), BF16),
        scratch_shapes=[pltpu.VMEM((2, G_B, 2 * QB, 2 * QB), F32)],
        compiler_params=_cparams(("arbitrary",)),
        name="attn_b_prompt",
    )(qb, kvb, kvb, bias_b, sink_rows)


def _sample_attention(q, kvn, akt, avt, bkt, bvt, cba, cbb, sink):
    t = q.shape[0]
    kvn_p = jnp.concatenate([kvn, jnp.zeros((LANES - t, kvn.shape[1]), F32)], axis=0).astype(BF16)
    lane_a = lax.broadcasted_iota(jnp.int32, (t, A_WIDTH), 1) // HEAD_DIM

    qa = q[:, :A_WIDTH]
    qbd = jnp.concatenate([jnp.where(lane_a == h, qa, 0.0) for h in range(H_A)], axis=0).astype(BF16)
    s_c = jnp.dot(qbd, akt.astype(BF16), preferred_element_type=F32)
    s_n = lax.dot_general(qbd, kvn_p[:, :A_WIDTH], (((1,), (1,)), ((), ())), preferred_element_type=F32)
    s = jnp.concatenate([s_c, s_n], axis=1) + cba
    m = jnp.max(s, axis=-1, keepdims=True)
    p = jnp.exp2(s - m)
    l = jnp.sum(p, axis=-1, keepdims=True)
    pb = p.astype(BF16)
    o_n = jnp.dot(pb[:, WIN_A:], kvn_p[:, A_WIDTH:2 * A_WIDTH], preferred_element_type=F32)
    o_all = lax.dot_general(pb[:, :WIN_A], avt.astype(BF16), (((1,), (1,)), ((), ())),
                            preferred_element_type=F32) + o_n
    o_sel = jnp.zeros((t, A_WIDTH), F32)
    l_b = jnp.ones((t, A_WIDTH), F32)
    for h in range(H_A):
        sel = lane_a == h
        o_sel = jnp.where(sel, o_all[h * t:(h + 1) * t], o_sel)
        l_b = jnp.where(sel, l[h * t:(h + 1) * t], l_b)
    oa = o_sel / l_b

    lane_b = lax.broadcasted_iota(jnp.int32, (G_B * t, LANES), 1)
    lo = lane_b < HEAD_DIM
    qb2 = jnp.concatenate([q[:, A_WIDTH + g * LANES:A_WIDTH + (g + 1) * LANES] for g in range(G_B)], axis=0)
    qm = jnp.concatenate([jnp.where(lo, qb2, 0.0), jnp.where(lo, 0.0, qb2)], axis=0).astype(BF16)
    kb_n = kvn_p[:, 2 * A_WIDTH:2 * A_WIDTH + LANES]
    vb_n = kvn_p[:, 2 * A_WIDTH + LANES:]
    sb_c = jnp.dot(qm, bkt.astype(BF16), preferred_element_type=F32)
    sb_n = lax.dot_general(qm, kb_n, (((1,), (1,)), ((), ())), preferred_element_type=F32)
    sb = jnp.concatenate([sb_c, sb_n], axis=1) + cbb
    mb = jnp.maximum(jnp.max(sb, axis=-1, keepdims=True), sink)
    pbb = jnp.exp2(sb - mb)
    den = jnp.sum(pbb, axis=-1, keepdims=True) + jnp.exp2(sink - mb)
    pbb = pbb.astype(BF16)
    ob = lax.dot_general(pbb[:, :WIN_B], bvt.astype(BF16), (((1,), (1,)), ((), ())),
                         preferred_element_type=F32)
    ob = (ob + jnp.dot(pbb[:, WIN_B:], vb_n, preferred_element_type=F32)) / den
    half = G_B * t
    lo8 = lo[:t]
    ob = jnp.concatenate([jnp.where(lo8, ob[g * t:(g + 1) * t], ob[half + g * t:half + (g + 1) * t])
                          for g in range(G_B)], axis=1)
    return oa, ob


def _route(logits):
    lane = lax.broadcasted_iota(jnp.int32, logits.shape, 1).astype(F32)
    big = jnp.float32(1 << 20)
    ninf = jnp.float32(-jnp.inf)
    gmask = lane < N_GROUPS
    lg = jnp.where(gmask, logits, ninf)
    gmax = jnp.max(lg, axis=-1, keepdims=True)
    grp = jnp.min(jnp.where(lg == gmax, lane, big), axis=-1, keepdims=True)
    pg_top = 1.0 / jnp.sum(jnp.exp(lg - gmax), axis=-1, keepdims=True)
    e0 = N_GROUPS + grp * EXPERTS_PER_GROUP
    emask = jnp.logical_and(lane >= e0, lane < e0 + EXPERTS_PER_GROUP)
    le = jnp.where(emask, logits, ninf)
    emax = jnp.max(le, axis=-1, keepdims=True)
    esum = jnp.sum(jnp.exp(le - emax), axis=-1, keepdims=True)
    i1 = jnp.min(jnp.where(le == emax, lane, big), axis=-1, keepdims=True)
    le2 = jnp.where(lane == i1, ninf, le)
    e2max = jnp.max(le2, axis=-1, keepdims=True)
    i2 = jnp.min(jnp.where(le2 == e2max, lane, big), axis=-1, keepdims=True)
    p1 = 1.0 / esum
    p2 = jnp.exp(e2max - emax) / esum
    g1 = pg_top * p1 / (p1 + p2)
    g2 = pg_top * p2 / (p1 + p2)
    out = jnp.where(lane == 0, i1 - N_GROUPS, 0.0)
    out = jnp.where(lane == 1, i2 - N_GROUPS, out)
    out = jnp.where(lane == 2, g1, out)
    out = jnp.where(lane == 3, g2, out)
    return out


def _pack_bf16_pairs(x):
    half = x.shape[1] // 2

    def rne(v):
        bits = lax.bitcast_convert_type(v, jnp.int32)
        return bits + 0x7FFF + (lax.shift_right_logical(bits, 16) & 1)

    lo = lax.shift_right_logical(rne(x[:, :half]), 16)
    hi = rne(x[:, half:]) & jnp.int32(-65536)
    return lo | hi


def _unpack_bf16_pairs(w):
    lo = lax.bitcast_convert_type(lax.shift_left(w, 16), F32)
    hi = lax.bitcast_convert_type(w & jnp.int32(-65536), F32)
    return jnp.concatenate([lo, hi], axis=1)


def _out_router_kernel(xp_ref, ap_ref, bp_ref, xs_ref, q_ref, kvn_ref, akt_hbm, avt_hbm, bkt_ref, bvt_ref,
                       cba_ref, cbb_ref, sink_ref, wo_ref, g_ref, wr_ref, br_ref,
                       x1_ref, xn_ref, route_ref, cnt_ref,
                       xcat_scr, mix_scr, kbuf, vbuf, sem, *, prompt_tiles, decode_tiles, seqs_per_step):
    i = pl.program_id(0)
    seqs = prompt_tiles * seqs_per_step

    def cache_copies(n, slot):
        return (pltpu.make_async_copy(akt_hbm.at[n], kbuf.at[slot], sem.at[0, slot]),
                pltpu.make_async_copy(avt_hbm.at[n], vbuf.at[slot], sem.at[1, slot]))

    @pl.when(i == 0)
    def _():
        cnt_ref[...] = jnp.zeros_like(cnt_ref)
        xcat_scr[...] = jnp.zeros_like(xcat_scr)
        for n0 in range(2):
            for c in cache_copies(n0, n0):
                c.start()

    @pl.when(i == prompt_tiles)
    def _():
        for n1 in (seqs, seqs + 1):
            for c in cache_copies(seqs - 1, n1 % CACHE_BUFFERS):
                c.wait()

    pslot = lax.rem(i, 2)

    def route_previous():
        logits = jnp.dot(xcat_scr[1 - pslot], wr_ref[...], preferred_element_type=F32)
        route = _route(logits + br_ref[...])
        route_ref[...] = route
        lanef = lax.broadcasted_iota(jnp.int32, route.shape, 1).astype(F32)
        hits = (lanef == route[:, 0:1]).astype(F32) + (lanef == route[:, 1:2]).astype(F32)
        cnt_ref[...] += jnp.sum(hits, axis=0, keepdims=True) * (i > 0).astype(F32)

    def project(x_ref, mix):
        x1 = x_ref[...] + jnp.dot(mix, wo_ref[...], preferred_element_type=F32)
        x1_ref[...] = x1
        ms = jnp.mean(x1 * x1, axis=-1, keepdims=True)
        xn = x1 * lax.rsqrt(ms + EPS) * g_ref[...]
        xn_ref[...] = _pack_bf16_pairs(xn)
        xh = xn.astype(BF16)
        xl = (xn - xh.astype(F32)).astype(BF16)
        xcat_scr[pslot] = jnp.concatenate([xh, xl, xh], axis=1)

    @pl.when(i < prompt_tiles)
    def _():
        route_previous()
        mix = jnp.concatenate([ap_ref[0], ap_ref[1], ap_ref[2], ap_ref[3]], axis=1).astype(BF16)
        project(xp_ref, jnp.concatenate([mix, bp_ref[...]], axis=1))
        t = q_ref.shape[1]
        for s in range(seqs_per_step):
            n = i * seqs_per_step + s
            slot = lax.rem(n, CACHE_BUFFERS)
            for c in cache_copies(n, slot):
                c.wait()
            for c in cache_copies(jnp.minimum(n + 2, seqs - 1), lax.rem(n + 2, CACHE_BUFFERS)):
                c.start()
            oa, ob = _sample_attention(q_ref[s], kvn_ref[s], kbuf[slot], vbuf[slot], bkt_ref[s], bvt_ref[s],
                                       cba_ref[...], cbb_ref[...], sink_ref[...])
            row = pl.multiple_of(n * t, t)
            mix_scr[pl.ds(row, t), :A_WIDTH] = oa
            mix_scr[pl.ds(row, t), A_WIDTH:] = ob

    @pl.when(i >= prompt_tiles)
    def _():
        route_previous()
        tm = xs_ref.shape[0]
        row = pl.multiple_of(jnp.clip(i - prompt_tiles, 0, decode_tiles - 1) * tm, tm)
        project(xs_ref, mix_scr[pl.ds(row, tm), :].astype(BF16))


def _out_router(xp, a4p, bp, xs, q3, kvn3, akt, avt, bkt, bvt, cbias_a, cbias_b, sink_rows, wo, gamma, wr, br):
    tp, tsm = xp.shape[0], xs.shape[0]
    ns, ts = q3.shape[0], q3.shape[1]
    tm = 512
    npt, nst = tp // tm, tsm // tm
    nt = npt + nst
    t = tp + tsm
    sps = ns // npt
    assert sps * npt == ns and ns * ts == tsm and ns >= CACHE_BUFFERS
    pmap = lambda i: (jnp.minimum(i, npt - 1), 0)
    pmap3 = lambda i: (jnp.minimum(i, npt - 1), 0, 0)
    smap = lambda i: (jnp.clip(i - npt, 0, nst - 1), 0)
    cur = lambda i: (jnp.minimum(i, nt - 1), 0)
    const = lambda i: (0, 0)
    return pl.pallas_call(
        functools.partial(_out_router_kernel, prompt_tiles=npt, decode_tiles=nst, seqs_per_step=sps),
        grid=(nt + 1,),
        in_specs=[
            pl.BlockSpec((tm, D_MODEL), pmap),
            pl.BlockSpec((4, tm, LANES), lambda i: (0, jnp.minimum(i, npt - 1), 0)),
            pl.BlockSpec((tm, 512), pmap),
            pl.BlockSpec((tm, D_MODEL), smap),
            pl.BlockSpec((sps, ts, 1024), pmap3),
            pl.BlockSpec((sps, ts, 1280), pmap3),
            pl.BlockSpec(memory_space=pl.ANY),
            pl.BlockSpec(memory_space=pl.ANY),
            pl.BlockSpec((sps, LANES, WIN_B), pmap3),
            pl.BlockSpec((sps, LANES, WIN_B), pmap3),
            pl.BlockSpec((H_A * ts, WIN_A + LANES), const),
            pl.BlockSpec((H_B * ts, WIN_B + LANES), const),
            pl.BlockSpec((H_B * ts, 1), const),
            pl.BlockSpec((D_MODEL, D_MODEL), const),
            pl.BlockSpec((1, D_MODEL), const),
            pl.BlockSpec((3 * D_MODEL, LANES), const),
            pl.BlockSpec((1, LANES), const),
        ],
        out_specs=[
            pl.BlockSpec((tm, D_MODEL), cur),
            pl.BlockSpec((tm, D_MODEL // 2), cur),
            pl.BlockSpec((tm, LANES), lambda i: (jnp.maximum(i - 1, 0), 0)),
            pl.BlockSpec((1, LANES), const),
        ],
        scratch_shapes=[pltpu.VMEM((2, tm, 3 * D_MODEL), BF16), pltpu.VMEM((tsm, D_MODEL), F32),
                        pltpu.VMEM((CACHE_BUFFERS, A_WIDTH, WIN_A), F32),
                        pltpu.VMEM((CACHE_BUFFERS, A_WIDTH, WIN_A), F32),
                        pltpu.SemaphoreType.DMA((2, CACHE_BUFFERS))],
        out_shape=[
            jax.ShapeDtypeStruct((t, D_MODEL), F32),
            jax.ShapeDtypeStruct((t, D_MODEL // 2), jnp.int32),
            jax.ShapeDtypeStruct((t, LANES), F32),
            jax.ShapeDtypeStruct((1, LANES), F32),
        ],
        compiler_params=_cparams(("arbitrary",), vmem=OUT_ROUTER_VMEM),
        name="out_router",
    )(xp, a4p, bp, xs, q3, kvn3, akt, avt, bkt, bvt, cbias_a, cbias_b, sink_rows, wo, gamma, wr, br)


def _sc_gather_rows(table, idx):
    b = idx.shape[0]
    d = table.shape[1]
    w = SC_WINDOW
    per_worker = b // SC_WORKERS
    nwin = per_worker // w
    assert per_worker * SC_WORKERS == b and nwin * w == per_worker
    mesh = plsc.VectorSubcoreMesh(core_axis_name="c", subcore_axis_name="s")

    @functools.partial(
        pl.kernel, mesh=mesh,
        out_type=jax.ShapeDtypeStruct((b, d), table.dtype),
        scratch_types=[pltpu.VMEM((nwin, w), jnp.int32), pltpu.VMEM((2, w, d), table.dtype),
                       pltpu.SemaphoreType.DMA((2,)), pltpu.SemaphoreType.DMA((2,))],
        name="sc_gather_rows",
    )
    def gather(table_hbm, idx_hbm, out_hbm, idx_v, rows_v, sem_in, sem_out):
        wid = lax.axis_index("s") * SC_CORES + lax.axis_index("c")
        base = wid * per_worker
        pltpu.sync_copy(idx_hbm.at[wid], idx_v)

        def fetch(j):
            return pltpu.make_async_copy(table_hbm.at[idx_v.at[j]], rows_v.at[j % 2], sem_in.at[j % 2])

        def flush(j):
            return pltpu.make_async_copy(rows_v.at[j % 2], out_hbm.at[pl.ds(base + j * w, w)],
                                         sem_out.at[j % 2])

        fetch(0).start()
        for j in range(nwin):
            fetch(j).wait()
            if j + 1 < nwin:
                if j >= 1:
                    flush(j - 1).wait()
                fetch(j + 1).start()
            flush(j).start()
        for j in range(max(nwin - 2, 0), nwin):
            flush(j).wait()

    return gather(table, idx.reshape(SC_WORKERS, nwin, w))


def _sc_scatter_rows(x, dest2, nrows):
    t, d = x.shape
    w = SC_SCATTER_WINDOW
    per_worker = t // SC_WORKERS
    nwin = per_worker // w
    assert per_worker * SC_WORKERS == t and nwin * w == per_worker
    mesh = plsc.VectorSubcoreMesh(core_axis_name="c", subcore_axis_name="s")

    @functools.partial(
        pl.kernel, mesh=mesh,
        out_type=jax.ShapeDtypeStruct((nrows, d), x.dtype),
        scratch_types=[pltpu.VMEM((TOP_K, nwin, w), jnp.int32), pltpu.VMEM((2, w, d), x.dtype),
                       pltpu.SemaphoreType.DMA((2,)), pltpu.SemaphoreType.DMA((2,))],
        name="sc_scatter_rows",
    )
    def scatter(x_hbm, dest_hbm, out_hbm, idx_v, rows_v, sem_in, sem_out):
        wid = lax.axis_index("s") * SC_CORES + lax.axis_index("c")
        base = wid * per_worker
        for k in range(TOP_K):
            pltpu.sync_copy(dest_hbm.at[k, wid], idx_v.at[k])

        def fetch(j):
            return pltpu.make_async_copy(x_hbm.at[pl.ds(base + j * w, w)], rows_v.at[j % 2], sem_in.at[j % 2])

        def spread(j, k):
            return pltpu.make_async_copy(rows_v.at[j % 2], out_hbm.at[idx_v.at[k, j]], sem_out.at[j % 2])

        fetch(0).start()
        for j in range(nwin):
            fetch(j).wait()
            if j + 1 < nwin:
                if j >= 1:
                    for k in range(TOP_K):
                        spread(j - 1, k).wait()
                fetch(j + 1).start()
            for k in range(TOP_K):
                spread(j, k).start()
        for j in range(max(nwin - 2, 0), nwin):
            for k in range(TOP_K):
                spread(j, k).wait()

    return scatter(x, dest2.reshape(TOP_K, SC_WORKERS, nwin, w))


def _expert_kernel(be_ref, nu_ref, nv_ref, nx_ref, x_ref, wg_hbm, wu_hbm, wd_hbm, o_ref,
                   wg_s, wu_s, wd_s, wg_f, wu_f, wd_f, slot_s, sem):
    i = pl.program_id(0)
    used = i < nu_ref[0]
    changed = jnp.logical_or(i == 0, be_ref[i] != be_ref[jnp.maximum(i - 1, 0)])

    def weight_copies(e, slot):
        return (pltpu.make_async_copy(wg_hbm.at[e], wg_f.at[slot], sem.at[slot, 0]),
                pltpu.make_async_copy(wu_hbm.at[e], wu_f.at[slot], sem.at[slot, 1]),
                pltpu.make_async_copy(wd_hbm.at[e], wd_f.at[slot], sem.at[slot, 2]))

    @pl.when(i == 0)
    def _():
        slot_s[0] = 0
        for c in weight_copies(be_ref[0], 0):
            c.start()

    @pl.when(jnp.logical_and(used, changed))
    def _():
        slot = slot_s[0]
        for c in weight_copies(be_ref[i], slot):
            c.wait()
        wg_s[...] = wg_f[slot].astype(BF16)
        wu_s[...] = wu_f[slot].astype(BF16)
        wd_s[...] = wd_f[slot].astype(BF16)

        @pl.when(nx_ref[i] != be_ref[i])
        def _():
            for c in weight_copies(nx_ref[i], 1 - slot):
                c.start()

        slot_s[0] = 1 - slot

    @pl.when(used)
    def _():
        row = lax.broadcasted_iota(jnp.int32, x_ref.shape, 0)
        x = _unpack_bf16_pairs(jnp.where(row < nv_ref[i], x_ref[...], 0)).astype(BF16)
        gate = jnp.dot(x, wg_s[...], preferred_element_type=F32)
        up = jnp.dot(x, wu_s[...], preferred_element_type=F32)
        h = (gate * jax.nn.sigmoid(gate) * up).astype(BF16)
        o_ref[...] = _pack_bf16_pairs(jnp.dot(h, wd_s[...], preferred_element_type=F32))

    @pl.when(jnp.logical_not(used))
    def _():
        o_ref[...] = jnp.zeros_like(o_ref)


def _experts(blk_e, n_used, nvalid, next_e, xb, w_gate, w_up, w_down):
    rows = xb.shape[0]
    nblocks = rows // MOE_ROWS
    grid_spec = pltpu.PrefetchScalarGridSpec(
        num_scalar_prefetch=4,
        grid=(nblocks,),
        in_specs=[
            pl.BlockSpec((MOE_ROWS, D_MODEL // 2), lambda i, be, nu, nv, nx: (i, 0)),
            pl.BlockSpec(memory_space=pl.ANY),
            pl.BlockSpec(memory_space=pl.ANY),
            pl.BlockSpec(memory_space=pl.ANY),
        ],
        out_specs=pl.BlockSpec((MOE_ROWS, D_MODEL // 2), lambda i, be, nu, nv, nx: (i, 0)),
        scratch_shapes=[pltpu.VMEM((D_MODEL, D_EXPERT), BF16), pltpu.VMEM((D_MODEL, D_EXPERT), BF16),
                        pltpu.VMEM((D_EXPERT, D_MODEL), BF16),
                        pltpu.VMEM((2, D_MODEL, D_EXPERT), F32), pltpu.VMEM((2, D_MODEL, D_EXPERT), F32),
                        pltpu.VMEM((2, D_EXPERT, D_MODEL), F32),
                        pltpu.SMEM((1,), jnp.int32), pltpu.SemaphoreType.DMA((2, 3))],
    )
    return pl.pallas_call(
        _expert_kernel,
        grid_spec=grid_spec,
        out_shape=jax.ShapeDtypeStruct((rows, D_MODEL // 2), jnp.int32),
        compiler_params=_cparams(("arbitrary",)),
        name="experts",
    )(blk_e, n_used, nvalid, next_e, xb, w_gate, w_up, w_down)


def _combine_kernel(x1_ref, y1_ref, y2_ref, route_ref, g_ref, outp_ref, outs_ref, *, prompt_tiles):
    r = route_ref[...]
    x = (x1_ref[...] + r[:, 2:3] * _unpack_bf16_pairs(y1_ref[...])
         + r[:, 3:4] * _unpack_bf16_pairs(y2_ref[...]))
    ms = jnp.mean(x * x, axis=-1, keepdims=True)
    y = x * lax.rsqrt(ms + EPS) * g_ref[...]
    i = pl.program_id(0)

    @pl.when(i < prompt_tiles)
    def _():
        outp_ref[...] = y

    @pl.when(i >= prompt_tiles)
    def _():
        outs_ref[...] = y


def _combine_norm(x1, ygath, route, gamma, tp):
    t = x1.shape[0]
    tm = 512
    nt, npt = t // tm, tp // tm
    return pl.pallas_call(
        functools.partial(_combine_kernel, prompt_tiles=npt),
        grid=(nt,),
        in_specs=[
            pl.BlockSpec((tm, D_MODEL), lambda i: (i, 0)),
            pl.BlockSpec((tm, D_MODEL // 2), lambda i: (i, 0)),
            pl.BlockSpec((tm, D_MODEL // 2), lambda i: (i + nt, 0)),
            pl.BlockSpec((tm, LANES), lambda i: (i, 0)),
            pl.BlockSpec((1, D_MODEL), lambda i: (0, 0)),
        ],
        out_specs=[
            pl.BlockSpec((tm, D_MODEL), lambda i: (jnp.minimum(i, npt - 1), 0)),
            pl.BlockSpec((tm, D_MODEL), lambda i: (jnp.maximum(i - npt, 0), 0)),
        ],
        out_shape=[jax.ShapeDtypeStruct((tp, D_MODEL), F32), jax.ShapeDtypeStruct((t - tp, D_MODEL), F32)],
        compiler_params=_cparams(("arbitrary",)),
        name="combine_norm",
    )(x1, ygath, ygath, route, gamma)


def _band_index():
    c = (2 * QB - np.arange(2 * QB)) % (2 * QB)
    return c, c <= QB


def _bias_a_prompt(table_a):
    c, valid = _band_index()
    idx = np.stack([_t5_bucket_np(d * np.clip(QB - c, 0, QB)) for d in DILATIONS])
    return jnp.where(valid, jnp.transpose(table_a[idx], (0, 2, 1)) * LOG2E, NEG)


def _bias_b_prompt(table_b):
    c, valid = _band_index()
    valid = valid & (c >= 1)
    h = jnp.where(valid, table_b[_t5_bucket_np(np.clip(QB - c, 0, QB))].T * LOG2E, NEG)
    return jnp.transpose(h.reshape(KV_B, G_B, 2 * QB), (1, 0, 2)).reshape(H_B, 2 * QB)


def _sample_bias(table, span, t, log2_weight):
    cols = span + LANES
    period = cols + LANES
    x = np.arange(period)
    dist = np.where(x >= period - t, span - x + period, span - x)
    extra = log2_weight(dist)
    valid = np.isfinite(extra)
    u = jnp.where(valid, table[_t5_bucket_np(np.maximum(dist, 0))].T * LOG2E
                  + np.where(valid, extra, 0.0).astype(np.float32), NEG)
    rows = jnp.tile(u, (1, t))[:, :t * (period - 1)].reshape(u.shape[0], t, period - 1)[:, :, :cols]
    return rows.reshape(u.shape[0] * t, cols)


def _bias_a_sample(table_a, t):
    def log2_count(dist):
        count = np.zeros(dist.shape, np.int64)
        for w, d in zip(WINDOWS, DILATIONS):
            count += (dist >= 0) & (dist % d == 0) & (dist <= w)
        return np.where(count > 0, np.log2(np.maximum(count, 1)), -np.inf)

    return _sample_bias(table_a, WIN_A, t, log2_count)


def _bias_b_sample(table_b, t):
    return _sample_bias(table_b, WIN_B, t,
                        lambda dist: np.where((dist >= 0) & (dist < WIN_B), 0.0, -np.inf))


def _dest_kernel(route_ref, cnt_ref, tri_ref, dest_ref, meta_ref, run_scr, pst_scr):
    i = pl.program_id(0)
    tm = route_ref.shape[0]
    r = route_ref[...]
    lane = lax.broadcasted_iota(jnp.int32, (tm, LANES), 1)
    lanef = lane.astype(F32)
    oh0 = lanef == r[:, 0:1]
    oh1 = lanef == r[:, 1:2]
    ohf = jnp.concatenate([oh0, oh1], axis=0).astype(F32)

    @pl.when(i == 0)
    def _():
        cnt = jnp.broadcast_to(cnt_ref[...], (LANES, LANES))
        padded = jnp.floor((cnt + (MOE_ROWS - 1)) * (1.0 / MOE_ROWS)) * MOE_ROWS
        lane_e = lax.broadcasted_iota(jnp.int32, (LANES, LANES), 1)
        x = padded
        for sh in (1, 2, 4, 8, 16, 32, 64):
            x = x + jnp.where(lane_e >= sh, pltpu.roll(x, sh, 1), 0.0)
        pst_scr[...] = (x - padded)[0:1]
        run_scr[...] = jnp.zeros_like(run_scr)
        wide = lambda v: jnp.concatenate([v.T, v.T], axis=1)
        cnt_t, bend_t = wide(cnt), wide(x * (1.0 / MOE_ROWS))
        bstart_t = wide((x - padded) * (1.0 / MOE_ROWS))
        blk = lax.broadcasted_iota(jnp.int32, (LANES, 2 * LANES), 1).astype(F32)
        exp = lax.broadcasted_iota(jnp.int32, (LANES, 2 * LANES), 0)
        real = exp < N_EXPERTS
        blk_e = jnp.minimum(jnp.sum(jnp.where(real & (bend_t <= blk), 1.0, 0.0), axis=0, keepdims=True),
                            N_EXPERTS - 1.0)
        mine = exp.astype(F32) == blk_e
        within = blk[0:1] - jnp.sum(jnp.where(mine, bstart_t, 0.0), axis=0, keepdims=True)
        nvalid = jnp.clip(jnp.sum(jnp.where(mine, cnt_t, 0.0), axis=0, keepdims=True) - within * MOE_ROWS,
                          0.0, float(MOE_ROWS))
        n_used = jnp.max(jnp.where(real, bend_t, 0.0), axis=0, keepdims=True)
        later = real & (exp.astype(F32) > blk_e) & (cnt_t > 0.0)
        nxt = jnp.min(jnp.where(later, exp.astype(F32), float(LANES)), axis=0, keepdims=True)
        nxt = jnp.where(nxt >= N_EXPERTS, blk_e, nxt)
        meta_ref[...] = jnp.concatenate([blk_e, nvalid, n_used, nxt, jnp.zeros((4, 2 * LANES), F32)],
                                        axis=0).astype(jnp.int32)

    base = run_scr[...] + pst_scr[...] - 1.0
    vals = []
    for c in range(2 * tm // LANES):
        ohc = ohf[c * LANES:(c + 1) * LANES]
        vals.append(jnp.dot(tri_ref[...], ohc.astype(BF16), preferred_element_type=F32) + base)
        base = base + jnp.sum(ohc, axis=0, keepdims=True)
    val = jnp.concatenate(vals, axis=0)
    d0 = jnp.sum(jnp.where(oh0, val[:tm], 0.0), axis=-1, keepdims=True)
    d1 = jnp.sum(jnp.where(oh1, val[tm:], 0.0), axis=-1, keepdims=True)
    tile = jnp.where(lane == 0, d0, jnp.where(lane == 1, d1, 0.0))
    dest_ref[...] = tile.T[:8].astype(jnp.int32)
    run_scr[...] += jnp.sum(ohf, axis=0, keepdims=True)


def _dispatch(route, cnt):
    t = route.shape[0]
    tm = 512
    tri = (jnp.arange(LANES)[:, None] >= jnp.arange(LANES)[None, :]).astype(BF16)
    nblocks = -(-t * TOP_K // MOE_ROWS) + N_EXPERTS
    assert nblocks <= 2 * LANES
    dest, meta = pl.pallas_call(
        _dest_kernel,
        grid=(t // tm,),
        in_specs=[pl.BlockSpec((tm, LANES), lambda i: (i, 0)),
                  pl.BlockSpec((1, LANES), lambda i: (0, 0)),
                  pl.BlockSpec((LANES, LANES), lambda i: (0, 0))],
        out_specs=[pl.BlockSpec((8, tm), lambda i: (0, i)),
                   pl.BlockSpec((8, 2 * LANES), lambda i: (0, 0))],
        out_shape=[jax.ShapeDtypeStruct((8, t), jnp.int32), jax.ShapeDtypeStruct((8, 2 * LANES), jnp.int32)],
        scratch_shapes=[pltpu.VMEM((1, LANES), F32), pltpu.VMEM((1, LANES), F32)],
        compiler_params=_cparams(("arbitrary",)),
        name="moe_dest",
    )(route, cnt, tri)
    return dest[:TOP_K], meta[0, :nblocks], meta[2, :1], meta[1, :nblocks], meta[3, :nblocks]


def kernel(x_prompt, x_sample, cache_a_k, cache_a_v, cache_b_k, cache_b_v, rel_bias_table, attn_norm, w_in,
           w_out, attn_sinks, ffn_norm, w_router_group, b_router_group, w_router_expert, b_router_expert,
           w_gate, w_up, w_down, final_norm):
    s = x_prompt.shape[1]
    ns, ts = x_sample.shape[0], x_sample.shape[1]
    table_a = rel_bias_table[:, :H_A]
    table_b = rel_bias_table[:, H_A:]

    w = w_in[0]
    wqa, wka, wva, wqb, wkb, wvb = (w[:, 0:512], w[:, 512:1024], w[:, 1024:1536], w[:, 1536:2048],
                                    w[:, 2048:2176], w[:, 2176:2304])
    wqb = jnp.transpose(wqb.reshape(D_MODEL, KV_B, G_B, HEAD_DIM), (0, 2, 1, 3)).reshape(D_MODEL, 512)
    wp = jnp.concatenate([wka, wva, wqa, wqb, wkb, wvb], axis=1).astype(BF16)
    cscale = jnp.concatenate([jnp.ones((1, 1024), F32), jnp.full((1, 1024), SCALE * LOG2E, F32),
                              jnp.ones((1, 256), F32)], axis=1)
    wo = w_out[0]
    wo_b = jnp.transpose(wo[512:].reshape(KV_B, G_B, HEAD_DIM, D_MODEL), (1, 0, 2, 3)).reshape(512, D_MODEL)
    wo_p = jnp.concatenate([wo[:512], wo_b], axis=0).astype(BF16)
    wr = jnp.concatenate([w_router_group[0],
                          jnp.transpose(w_router_expert[0], (1, 0, 2)).reshape(D_MODEL, N_EXPERTS),
                          jnp.zeros((D_MODEL, LANES - N_GROUPS - N_EXPERTS), F32)], axis=1)
    wr_hi = wr.astype(BF16)
    wr = jnp.concatenate([wr_hi, wr_hi, (wr - wr_hi.astype(F32)).astype(BF16)], axis=0)
    br = jnp.concatenate([b_router_group[0], b_router_expert[0].reshape(N_EXPERTS),
                          jnp.zeros((LANES - N_GROUPS - N_EXPERTS,), F32)]).reshape(1, LANES)
    sinks2 = attn_sinks[0] * LOG2E
    sinks_gk = jnp.transpose(sinks2.reshape(KV_B, G_B), (1, 0)).reshape(H_B)
    sink_rows_p = jnp.repeat(sinks_gk, QB).reshape(G_B, 1, 2 * QB)
    sink_rows_s = jnp.repeat(sinks2, ts).reshape(H_B * ts, 1)
    emat = jnp.tile(jnp.arange(LANES)[:, None] == (jnp.arange(A_WIDTH)[None, :] // HEAD_DIM),
                    (3, 1)).astype(BF16)
    attn_g = attn_norm[0].reshape(1, D_MODEL)
    ffn_g = ffn_norm[0].reshape(1, D_MODEL)

    xp = x_prompt.reshape(s, D_MODEL)
    aperm, qb_p, kvb_p, akv32, bkv32 = _proj_prompt(xp, attn_g, wp, cscale)
    a4 = _attn_a_prompt(aperm, _bias_a_prompt(table_a), emat)
    ob_p = _attn_b_prompt(qb_p, kvb_p, _bias_b_prompt(table_b), sink_rows_p)

    xs = x_sample.reshape(ns * ts, D_MODEL)
    q_s, kv_s = _proj_sample(xs, attn_g, wp, cscale)
    akt = jnp.transpose(cache_a_k[0], (0, 2, 3, 1)).reshape(ns, A_WIDTH, WIN_A)
    avt = jnp.transpose(cache_a_v[0], (0, 2, 3, 1)).reshape(ns, A_WIDTH, WIN_A)
    bkt = jnp.transpose(cache_b_k[0], (0, 2, 3, 1)).reshape(ns, LANES, WIN_B)
    bvt = jnp.transpose(cache_b_v[0], (0, 2, 3, 1)).reshape(ns, LANES, WIN_B)

    x1, xn, route, cnt = _out_router(xp, a4, ob_p, xs, q_s.reshape(ns, ts, 1024), kv_s.reshape(ns, ts, 1280),
                                     akt, avt, bkt, bvt, _bias_a_sample(table_a, ts),
                                     _bias_b_sample(table_b, ts), sink_rows_s, wo_p, ffn_g, wr, br)
    dest2, blk_e, n_used, nvalid, next_e = _dispatch(route, cnt)
    xb = _sc_scatter_rows(xn, dest2, blk_e.shape[0] * MOE_ROWS)
    yb = _experts(blk_e, n_used, nvalid, next_e, xb, w_gate[0], w_up[0], w_down[0])
    y_p, y_s = _combine_norm(x1, _sc_gather_rows(yb, dest2.reshape(-1)), route, final_norm.reshape(1, D_MODEL), s)

    y_prompt = y_p.reshape(1, s, D_MODEL)
    y_sample = y_s.reshape(ns, ts, D_MODEL)
    keep_a, keep_b = min(WIN_A, s), min(WIN_B, s)
    pak = akv32[SPAN - keep_a:, :512].reshape(1, 1, keep_a, H_A, HEAD_DIM)
    pav = akv32[SPAN - keep_a:, 512:].reshape(1, 1, keep_a, H_A, HEAD_DIM)
    pbk = bkv32[SPAN - keep_b:, :128].reshape(1, 1, keep_b, KV_B, HEAD_DIM)
    pbv = bkv32[SPAN - keep_b:, 128:].reshape(1, 1, keep_b, KV_B, HEAD_DIM)
    sak = kv_s[:, 0:512].reshape(1, ns, ts, H_A, HEAD_DIM)
    sav = kv_s[:, 512:1024].reshape(1, ns, ts, H_A, HEAD_DIM)
    sbk = kv_s[:, 1024:1152].reshape(1, ns, ts, KV_B, HEAD_DIM)
    sbv = kv_s[:, 1152:1280].reshape(1, ns, ts, KV_B, HEAD_DIM)
    return (y_prompt, y_sample, pak, pav, pbk, pbv, sak, sav, sbk, sbv)
```

```python
import functools
import math

import jax
import jax.numpy as jnp
import numpy as np
from jax import lax
from jax.experimental import pallas as pl
from jax.experimental.pallas import tpu as pltpu
from jax.experimental.pallas import tpu_sc as plsc

D_MODEL = 1024
HEAD_DIM = 64
H_A = 8
H_B = 8
KV_B = 2
G_B = 4
DILATIONS = (1, 4, 16)
WINDOWS = (128, 512, 2048)
WIN_A = 2048
WIN_B = 128
NUM_BUCKETS = 32
MAX_DISTANCE = 2048
N_GROUPS = 4
EXPERTS_PER_GROUP = 8
N_EXPERTS = 32
TOP_K = 2
D_EXPERT = 512
EPS = 1e-5
SCALE = HEAD_DIM ** -0.5

LANES = 128
SPAN = 2048
QB = 128
NCHUNK = 9
A_WIDTH = H_A * HEAD_DIM
MOE_ROWS = 512
SC_CORES = 2
SC_SUBCORES = 16
SC_WORKERS = SC_CORES * SC_SUBCORES
SC_WINDOW = 64
SC_SCATTER_WINDOW = 32
NEG = -1e30
LOG2E = math.log2(math.e)
B_STEP = 1024
V7X_VMEM_BYTES = 64 * 1024 * 1024
VMEM_LIMIT = V7X_VMEM_BYTES - 8 * 1024 * 1024
CACHE_BUFFERS = 3
OUT_ROUTER_VMEM = V7X_VMEM_BYTES - 4 * 1024 * 1024

F32 = jnp.float32
BF16 = jnp.bfloat16


def _t5_bucket_np(dist):
    dist = np.asarray(dist, np.int64)
    max_exact = NUM_BUCKETS // 2
    d = np.maximum(dist, 1).astype(np.float32)
    ratio = np.log(d / np.float32(max_exact)) / np.float32(math.log(MAX_DISTANCE / max_exact))
    large = max_exact + (ratio * np.float32(NUM_BUCKETS - max_exact)).astype(np.int32)
    large = np.minimum(large, NUM_BUCKETS - 1)
    return np.where(dist < max_exact, dist, large).astype(np.int32)


def _cparams(sem, vmem=VMEM_LIMIT):
    return pltpu.CompilerParams(dimension_semantics=sem, vmem_limit_bytes=vmem)


def _proj_prompt_kernel(x_ref, g_ref, w_ref, cs_ref, aperm_ref, qb_ref, kvb_ref, akv_ref, bkv_ref,
                        h_scr, p_scr):
    n = pl.program_id(1)

    @pl.when(n == 0)
    def _():
        x = x_ref[...]
        ms = jnp.mean(x * x, axis=-1, keepdims=True)
        h_scr[...] = (x * lax.rsqrt(ms + EPS) * g_ref[...]).astype(BF16)

    p = jnp.dot(h_scr[...], w_ref[...], preferred_element_type=F32) * cs_ref[...]

    @pl.when(n < 6)
    def _():
        aperm_ref[0] = p.astype(BF16)
        p_scr[0, 0] = p[:, :LANES]
        p_scr[0, 1] = p[:, LANES:]
        quarter = SPAN // 4
        for r in range(4):
            lo = p_scr[0, 0, pl.ds(r, quarter, stride=4), :]
            hi = p_scr[0, 1, pl.ds(r, quarter, stride=4), :]
            p_scr[1, 0, r * quarter:(r + 1) * quarter, :] = lo
            p_scr[1, 1, r * quarter:(r + 1) * quarter, :] = hi
            aperm_ref[1, r * quarter:(r + 1) * quarter, :] = jnp.concatenate([lo, hi], axis=1).astype(BF16)
        for r16 in range(16):
            start = (r16 % 4) * quarter + r16 // 4
            t = jnp.concatenate([p_scr[1, 0, pl.ds(start, QB, stride=4), :],
                                 p_scr[1, 1, pl.ds(start, QB, stride=4), :]], axis=1)
            aperm_ref[2, r16 * QB:(r16 + 1) * QB, :] = t.astype(BF16)

    @pl.when(n < 4)
    def _():
        akv_ref[...] = p

    @pl.when(jnp.logical_or(n == 6, n == 7))
    def _():
        qb_ref[...] = p.astype(BF16)

    @pl.when(n == 8)
    def _():
        kvb_ref[...] = p.astype(BF16)
        bkv_ref[...] = p


def _proj_prompt(x, gamma, w, cscale):
    s = x.shape[0]
    nspan = s // SPAN
    return pl.pallas_call(
        _proj_prompt_kernel,
        grid=(nspan, NCHUNK),
        in_specs=[
            pl.BlockSpec((SPAN, D_MODEL), lambda b, n: (b, 0)),
            pl.BlockSpec((1, D_MODEL), lambda b, n: (0, 0)),
            pl.BlockSpec((D_MODEL, 256), lambda b, n: (0, n)),
            pl.BlockSpec((1, 256), lambda b, n: (0, n)),
        ],
        out_specs=[
            pl.BlockSpec((3, SPAN, 256), lambda b, n: (0, b, jnp.minimum(n, 5))),
            pl.BlockSpec((SPAN, 256), lambda b, n: (b, jnp.clip(n - 6, 0, 1))),
            pl.BlockSpec((SPAN, 256), lambda b, n: (b, 0)),
            pl.BlockSpec((SPAN, 256), lambda b, n: (0, jnp.where(b == nspan - 1, jnp.minimum(n, 3), 0))),
            pl.BlockSpec((SPAN, 256), lambda b, n: (0, 0)),
        ],
        out_shape=[
            jax.ShapeDtypeStruct((3, s, 3 * A_WIDTH), BF16),
            jax.ShapeDtypeStruct((s, 512), BF16),
            jax.ShapeDtypeStruct((s, 256), BF16),
            jax.ShapeDtypeStruct((SPAN, 1024), F32),
            jax.ShapeDtypeStruct((SPAN, 256), F32),
        ],
        scratch_shapes=[pltpu.VMEM((SPAN, D_MODEL), BF16), pltpu.VMEM((2, 2, SPAN, LANES), F32)],
        compiler_params=_cparams(("arbitrary", "arbitrary")),
        name="proj_prompt",
    )(x, gamma, w, cscale)


def _proj_sample_kernel(x_ref, g_ref, w_ref, cs_ref, q_ref, kv_ref):
    x = x_ref[...]
    ms = jnp.mean(x * x, axis=-1, keepdims=True)
    h = (x * lax.rsqrt(ms + EPS) * g_ref[...]).astype(BF16)
    p = jnp.dot(h, w_ref[...], preferred_element_type=F32) * cs_ref[...]
    kv_ref[:, :1024] = p[:, :1024]
    kv_ref[:, 1024:] = p[:, 2048:]
    q_ref[...] = p[:, 1024:2048]


def _proj_sample(x, gamma, w, cscale):
    t = x.shape[0]
    tm = 512
    return pl.pallas_call(
        _proj_sample_kernel,
        grid=(t // tm,),
        in_specs=[
            pl.BlockSpec((tm, D_MODEL), lambda i: (i, 0)),
            pl.BlockSpec((1, D_MODEL), lambda i: (0, 0)),
            pl.BlockSpec((D_MODEL, 2304), lambda i: (0, 0)),
            pl.BlockSpec((1, 2304), lambda i: (0, 0)),
        ],
        out_specs=[
            pl.BlockSpec((tm, 1024), lambda i: (i, 0)),
            pl.BlockSpec((tm, 1280), lambda i: (i, 0)),
        ],
        out_shape=[
            jax.ShapeDtypeStruct((t, 1024), F32),
            jax.ShapeDtypeStruct((t, 1280), F32),
        ],
        compiler_params=_cparams(("arbitrary",)),
        name="proj_sample",
    )(x, gamma, w, cscale)


def _spread_heads(w, e3_ref):
    hi = w.astype(BF16)
    r1 = w - hi.astype(F32)
    mid = r1.astype(BF16)
    low = (r1 - mid.astype(F32)).astype(BF16)
    return jnp.dot(jnp.concatenate([hi, mid, low], axis=1), e3_ref[...], preferred_element_type=F32)


def _pair_tile(q2, kk, vv, bias_t, lo, sink=None, normalize=True):
    zero = jnp.zeros_like(q2)
    qq = jnp.concatenate([jnp.where(lo, q2, zero), jnp.where(lo, zero, q2)], axis=0)
    st = lax.dot_general(kk, qq, (((1,), (1,)), ((), ())), preferred_element_type=F32)
    st = st + bias_t
    m = jnp.max(st, axis=0, keepdims=True)
    if sink is not None:
        m = jnp.maximum(m, sink)
    p = jnp.exp2(st - m)
    den = jnp.sum(p, axis=0, keepdims=True)
    if sink is not None:
        den = den + jnp.exp2(sink - m)
    pn = (p * (1.0 / den) if normalize else p).astype(BF16)
    o = lax.dot_general(pn, vv, (((0,), (0,)), ((), ())), preferred_element_type=F32)
    return jnp.where(lo, o[:QB], o[QB:]), m, den


def _fill_band_tiles(h_ref, bias_scr):
    nk = 2 * QB
    prev = lax.broadcasted_iota(jnp.int32, (nk, nk), 0) < QB
    for pair in range(h_ref.shape[0] // 2):
        halves = []
        for hh in range(2):
            row = h_ref[2 * pair + hh:2 * pair + hh + 1, :]
            band = pltpu.roll(jnp.broadcast_to(row, (nk, nk)), 0, 1, stride=1, stride_axis=0)
            halves.append(band[:, :QB])
        tile = jnp.concatenate(halves, axis=1)
        bias_scr[0, pair] = tile
        bias_scr[1, pair] = jnp.where(prev, NEG, tile)


def _attn_a_kernel(q_ref, kvc_ref, kvp_ref, h_ref, e_ref, out_ref, o_scr, st_scr, bias_scr):
    b = pl.program_id(0)
    g = pl.program_id(1)
    nblk = jnp.where(g == 0, 16, jnp.where(g == 1, 4, 1))
    lane = lax.broadcasted_iota(jnp.int32, (QB, LANES), 1)
    lo = lane < HEAD_DIM

    @pl.when(b == 0)
    def _():
        _fill_band_tiles(h_ref, bias_scr.at[g])

    bias_ref = bias_scr.at[g]

    for cb in range(SPAN // QB):
        first = lax.rem(jnp.int32(cb), nblk) == 0
        rows = slice(cb * QB, (cb + 1) * QB)
        prow_c = max(cb - 1, 0) * QB
        prow_p = pl.multiple_of(jnp.where(first, cb + nblk - 1, 0) * QB, QB)
        variant = jnp.logical_and(first, b == 0).astype(jnp.int32)
        maxes, dens = [], []
        for hp in range(4):
            ks = slice(hp * LANES, (hp + 1) * LANES)
            vs = slice(A_WIDTH + hp * LANES, A_WIDTH + (hp + 1) * LANES)
            kp = jnp.where(first, kvp_ref[pl.ds(prow_p, QB), ks], kvc_ref[prow_c:prow_c + QB, ks])
            vp = jnp.where(first, kvp_ref[pl.ds(prow_p, QB), vs], kvc_ref[prow_c:prow_c + QB, vs])
            kk = jnp.concatenate([kp, kvc_ref[rows, ks]], axis=0)
            vv = jnp.concatenate([vp, kvc_ref[rows, vs]], axis=0)
            o, m, den = _pair_tile(q_ref[rows, ks], kk, vv, bias_ref[variant, hp], lo, normalize=False)
            o_scr[g, hp, rows, :] = o
            maxes += [m[:, :QB], m[:, QB:]]
            dens += [den[:, :QB], den[:, QB:]]
        sm = jnp.concatenate(maxes + dens + [jnp.zeros((LANES - 2 * H_A, QB), F32)], axis=0)
        st_scr[g, rows, :] = sm.T

    @pl.when(g == 2)
    def _():
        def merge(c, carry):
            r2 = lax.rem(c, 4) * (SPAN // 4) + c // 4
            r3 = pl.multiple_of(c * QB, QB)
            s1 = st_scr[0, pl.ds(c, QB, stride=16), :]
            s2 = st_scr[1, pl.ds(r2, QB, stride=4), :]
            s3 = st_scr[2, pl.ds(r3, QB), :]
            mx = jnp.maximum(jnp.maximum(s1, s2), s3)
            w1 = jnp.exp2(s1 - mx)
            w2 = jnp.exp2(s2 - mx)
            w3 = jnp.exp2(s3 - mx)
            shift = LANES - H_A
            tot = (w1 * pltpu.roll(s1, shift, 1) + w2 * pltpu.roll(s2, shift, 1)
                   + w3 * pltpu.roll(s3, shift, 1))
            head_lane = lax.broadcasted_iota(jnp.int32, (QB, LANES), 1) < H_A
            a1 = _spread_heads(jnp.where(head_lane, w1 / tot, 0.0), e_ref)
            a2 = _spread_heads(jnp.where(head_lane, w2 / tot, 0.0), e_ref)
            a3 = _spread_heads(jnp.where(head_lane, w3 / tot, 0.0), e_ref)
            for hp in range(4):
                sl = slice(hp * LANES, (hp + 1) * LANES)
                o1 = o_scr[0, hp, pl.ds(c, QB, stride=16), :]
                o2 = o_scr[1, hp, pl.ds(r2, QB, stride=4), :]
                o3 = o_scr[2, hp, pl.ds(r3, QB), :]
                out_ref[hp, pl.ds(c, QB, stride=16), :] = a1[:, sl] * o1 + a2[:, sl] * o2 + a3[:, sl] * o3
            return carry

        lax.fori_loop(0, 16, merge, 0, unroll=4)


def _attn_a_prompt(aperm, bias_a, emat):
    s = aperm.shape[1]
    nspan = s // SPAN
    return pl.pallas_call(
        _attn_a_kernel,
        grid=(nspan, 3),
        in_specs=[
            pl.BlockSpec((None, SPAN, A_WIDTH), lambda b, g: (g, b, 2)),
            pl.BlockSpec((None, SPAN, 2 * A_WIDTH), lambda b, g: (g, b, 0)),
            pl.BlockSpec((None, SPAN, 2 * A_WIDTH), lambda b, g: (g, jnp.maximum(b - 1, 0), 0)),
            pl.BlockSpec((None, H_A, 2 * QB), lambda b, g: (g, 0, 0)),
            pl.BlockSpec((3 * LANES, A_WIDTH), lambda b, g: (0, 0)),
        ],
        out_specs=pl.BlockSpec((4, SPAN, LANES), lambda b, g: (0, b, 0)),
        out_shape=jax.ShapeDtypeStruct((4, s, LANES), F32),
        scratch_shapes=[pltpu.VMEM((3, 4, SPAN, LANES), F32), pltpu.VMEM((3, SPAN, LANES), F32),
                        pltpu.VMEM((3, 2, 4, 2 * QB, 2 * QB), F32)],
        compiler_params=_cparams(("arbitrary", "arbitrary")),
        name="attn_a_prompt",
    )(aperm, aperm, aperm, bias_a, emat)


def _attn_b_kernel(q_ref, kvc_ref, kvp_ref, h_ref, sink_ref, out_ref, bias_ref):
    i = pl.program_id(0)
    lane = lax.broadcasted_iota(jnp.int32, (QB, LANES), 1)
    lo = lane < HEAD_DIM

    @pl.when(i == 0)
    def _():
        _fill_band_tiles(h_ref, bias_ref)

    variant = (i == 0).astype(jnp.int32)
    for j in range(B_STEP // QB):
        rows = slice(j * QB, (j + 1) * QB)
        if j == 0:
            kp, vp = kvp_ref[:, :LANES], kvp_ref[:, LANES:]
        else:
            kp, vp = kvc_ref[(j - 1) * QB:j * QB, :LANES], kvc_ref[(j - 1) * QB:j * QB, LANES:]
        kk = jnp.concatenate([kp, kvc_ref[rows, :LANES]], axis=0)
        vv = jnp.concatenate([vp, kvc_ref[rows, LANES:]], axis=0)
        for g in range(G_B):
            bias_t = bias_ref[variant, g] if j == 0 else bias_ref[0, g]
            o, _, _ = _pair_tile(q_ref[rows, g * LANES:(g + 1) * LANES], kk, vv, bias_t, lo, sink=sink_ref[g])
            out_ref[rows, g * LANES:(g + 1) * LANES] = o.astype(BF16)


def _attn_b_prompt(qb, kvb, bias_b, sink_rows):
    s = qb.shape[0]
    per = B_STEP // QB
    return pl.pallas_call(
        _attn_b_kernel,
        grid=(s // B_STEP,),
        in_specs=[
            pl.BlockSpec((B_STEP, 512), lambda i: (i, 0)),
            pl.BlockSpec((B_STEP, 256), lambda i: (i, 0)),
            pl.BlockSpec((QB, 256), lambda i: (jnp.maximum(i * per - 1, 0), 0)),
            pl.BlockSpec((H_B, 2 * QB), lambda i: (0, 0)),
            pl.BlockSpec((G_B, 1, 2 * QB), lambda i: (0, 0, 0)),
        ],
        out_specs=pl.BlockSpec((B_STEP, 512), lambda i: (i, 0)),
        out_shape=jax.ShapeDtypeStruct((s, 512), BF16),
        scratch_shapes=[pltpu.VMEM((2, G_B, 2 * QB, 2 * QB), F32)],
        compiler_params=_cparams(("arbitrary",)),
        name="attn_b_prompt",
    )(qb, kvb, kvb, bias_b, sink_rows)


def _sample_attention(q, kvn, akt, avt, bkt, bvt, cba, cbb, sink):
    t = q.shape[0]
    kvn_p = jnp.concatenate([kvn, jnp.zeros((LANES - t, kvn.shape[1]), F32)], axis=0).astype(BF16)
    lane_a = lax.broadcasted_iota(jnp.int32, (t, A_WIDTH), 1) // HEAD_DIM

    qa = q[:, :A_WIDTH]
    qbd = jnp.concatenate([jnp.where(lane_a == h, qa, 0.0) for h in range(H_A)], axis=0).astype(BF16)
    s_c = jnp.dot(qbd, akt.astype(BF16), preferred_element_type=F32)
    s_n = lax.dot_general(qbd, kvn_p[:, :A_WIDTH], (((1,), (1,)), ((), ())), preferred_element_type=F32)
    s = jnp.concatenate([s_c, s_n], axis=1) + cba
    m = jnp.max(s, axis=-1, keepdims=True)
    p = jnp.exp2(s - m)
    l = jnp.sum(p, axis=-1, keepdims=True)
    pb = p.astype(BF16)
    o_n = jnp.dot(pb[:, WIN_A:], kvn_p[:, A_WIDTH:2 * A_WIDTH], preferred_element_type=F32)
    o_all = lax.dot_general(pb[:, :WIN_A], avt.astype(BF16), (((1,), (1,)), ((), ())),
                            preferred_element_type=F32) + o_n
    o_sel = jnp.zeros((t, A_WIDTH), F32)
    l_b = jnp.ones((t, A_WIDTH), F32)
    for h in range(H_A):
        sel = lane_a == h
        o_sel = jnp.where(sel, o_all[h * t:(h + 1) * t], o_sel)
        l_b = jnp.where(sel, l[h * t:(h + 1) * t], l_b)
    oa = o_sel / l_b

    lane_b = lax.broadcasted_iota(jnp.int32, (G_B * t, LANES), 1)
    lo = lane_b < HEAD_DIM
    qb2 = jnp.concatenate([q[:, A_WIDTH + g * LANES:A_WIDTH + (g + 1) * LANES] for g in range(G_B)], axis=0)
    qm = jnp.concatenate([jnp.where(lo, qb2, 0.0), jnp.where(lo, 0.0, qb2)], axis=0).astype(BF16)
    kb_n = kvn_p[:, 2 * A_WIDTH:2 * A_WIDTH + LANES]
    vb_n = kvn_p[:, 2 * A_WIDTH + LANES:]
    sb_c = jnp.dot(qm, bkt.astype(BF16), preferred_element_type=F32)
    sb_n = lax.dot_general(qm, kb_n, (((1,), (1,)), ((), ())), preferred_element_type=F32)
    sb = jnp.concatenate([sb_c, sb_n], axis=1) + cbb
    mb = jnp.maximum(jnp.max(sb, axis=-1, keepdims=True), sink)
    pbb = jnp.exp2(sb - mb)
    den = jnp.sum(pbb, axis=-1, keepdims=True) + jnp.exp2(sink - mb)
    pbb = pbb.astype(BF16)
    ob = lax.dot_general(pbb[:, :WIN_B], bvt.astype(BF16), (((1,), (1,)), ((), ())),
                         preferred_element_type=F32)
    ob = (ob + jnp.dot(pbb[:, WIN_B:], vb_n, preferred_element_type=F32)) / den
    half = G_B * t
    lo8 = lo[:t]
    ob = jnp.concatenate([jnp.where(lo8, ob[g * t:(g + 1) * t], ob[half + g * t:half + (g + 1) * t])
                          for g in range(G_B)], axis=1)
    return oa, ob


def _route(logits):
    lane = lax.broadcasted_iota(jnp.int32, logits.shape, 1).astype(F32)
    big = jnp.float32(1 << 20)
    ninf = jnp.float32(-jnp.inf)
    gmask = lane < N_GROUPS
    lg = jnp.where(gmask, logits, ninf)
    gmax = jnp.max(lg, axis=-1, keepdims=True)
    grp = jnp.min(jnp.where(lg == gmax, lane, big), axis=-1, keepdims=True)
    pg_top = 1.0 / jnp.sum(jnp.exp(lg - gmax), axis=-1, keepdims=True)
    e0 = N_GROUPS + grp * EXPERTS_PER_GROUP
    emask = jnp.logical_and(lane >= e0, lane < e0 + EXPERTS_PER_GROUP)
    le = jnp.where(emask, logits, ninf)
    emax = jnp.max(le, axis=-1, keepdims=True)
    esum = jnp.sum(jnp.exp(le - emax), axis=-1, keepdims=True)
    i1 = jnp.min(jnp.where(le == emax, lane, big), axis=-1, keepdims=True)
    le2 = jnp.where(lane == i1, ninf, le)
    e2max = jnp.max(le2, axis=-1, keepdims=True)
    i2 = jnp.min(jnp.where(le2 == e2max, lane, big), axis=-1, keepdims=True)
    p1 = 1.0 / esum
    p2 = jnp.exp(e2max - emax) / esum
    g1 = pg_top * p1 / (p1 + p2)
    g2 = pg_top * p2 / (p1 + p2)
    out = jnp.where(lane == 0, i1 - N_GROUPS, 0.0)
    out = jnp.where(lane == 1, i2 - N_GROUPS, out)
    out = jnp.where(lane == 2, g1, out)
    out = jnp.where(lane == 3, g2, out)
    return out


def _pack_bf16_pairs(x):
    half = x.shape[1] // 2

    def rne(v):
        bits = lax.bitcast_convert_type(v, jnp.int32)
        return bits + 0x7FFF + (lax.shift_right_logical(bits, 16) & 1)

    lo = lax.shift_right_logical(rne(x[:, :half]), 16)
    hi = rne(x[:, half:]) & jnp.int32(-65536)
    return lo | hi


def _unpack_bf16_pairs(w):
    lo = lax.bitcast_convert_type(lax.shift_left(w, 16), F32)
    hi = lax.bitcast_convert_type(w & jnp.int32(-65536), F32)
    return jnp.concatenate([lo, hi], axis=1)


def _out_router_kernel(xp_ref, ap_ref, bp_ref, xs_ref, q_ref, kvn_ref, akt_hbm, avt_hbm, bkt_ref, bvt_ref,
                       cba_ref, cbb_ref, sink_ref, wo_ref, g_ref, wr_ref, br_ref,
                       x1_ref, xn_ref, route_ref, cnt_ref,
                       xcat_scr, mix_scr, kbuf, vbuf, sem, *, prompt_tiles, decode_tiles, seqs_per_step):
    i = pl.program_id(0)
    seqs = prompt_tiles * seqs_per_step

    def cache_copies(n, slot):
        return (pltpu.make_async_copy(akt_hbm.at[n], kbuf.at[slot], sem.at[0, slot]),
                pltpu.make_async_copy(avt_hbm.at[n], vbuf.at[slot], sem.at[1, slot]))

    @pl.when(i == 0)
    def _():
        cnt_ref[...] = jnp.zeros_like(cnt_ref)
        xcat_scr[...] = jnp.zeros_like(xcat_scr)
        for n0 in range(2):
            for c in cache_copies(n0, n0):
                c.start()

    @pl.when(i == prompt_tiles)
    def _():
        for n1 in (seqs, seqs + 1):
            for c in cache_copies(seqs - 1, n1 % CACHE_BUFFERS):
                c.wait()

    pslot = lax.rem(i, 2)

    def route_previous():
        logits = jnp.dot(xcat_scr[1 - pslot], wr_ref[...], preferred_element_type=F32)
        route = _route(logits + br_ref[...])
        route_ref[...] = route
        lanef = lax.broadcasted_iota(jnp.int32, route.shape, 1).astype(F32)
        hits = (lanef == route[:, 0:1]).astype(F32) + (lanef == route[:, 1:2]).astype(F32)
        cnt_ref[...] += jnp.sum(hits, axis=0, keepdims=True) * (i > 0).astype(F32)

    def project(x_ref, mix):
        x1 = x_ref[...] + jnp.dot(mix, wo_ref[...], preferred_element_type=F32)
        x1_ref[...] = x1
        ms = jnp.mean(x1 * x1, axis=-1, keepdims=True)
        xn = x1 * lax.rsqrt(ms + EPS) * g_ref[...]
        xn_ref[...] = _pack_bf16_pairs(xn)
        xh = xn.astype(BF16)
        xl = (xn - xh.astype(F32)).astype(BF16)
        xcat_scr[pslot] = jnp.concatenate([xh, xl, xh], axis=1)

    @pl.when(i < prompt_tiles)
    def _():
        route_previous()
        mix = jnp.concatenate([ap_ref[0], ap_ref[1], ap_ref[2], ap_ref[3]], axis=1).astype(BF16)
        project(xp_ref, jnp.concatenate([mix, bp_ref[...]], axis=1))
        t = q_ref.shape[1]
        for s in range(seqs_per_step):
            n = i * seqs_per_step + s
            slot = lax.rem(n, CACHE_BUFFERS)
            for c in cache_copies(n, slot):
                c.wait()
            for c in cache_copies(jnp.minimum(n + 2, seqs - 1), lax.rem(n + 2, CACHE_BUFFERS)):
                c.start()
            oa, ob = _sample_attention(q_ref[s], kvn_ref[s], kbuf[slot], vbuf[slot], bkt_ref[s], bvt_ref[s],
                                       cba_ref[...], cbb_ref[...], sink_ref[...])
            row = pl.multiple_of(n * t, t)
            mix_scr[pl.ds(row, t), :A_WIDTH] = oa
            mix_scr[pl.ds(row, t), A_WIDTH:] = ob

    @pl.when(i >= prompt_tiles)
    def _():
        route_previous()
        tm = xs_ref.shape[0]
        row = pl.multiple_of(jnp.clip(i - prompt_tiles, 0, decode_tiles - 1) * tm, tm)
        project(xs_ref, mix_scr[pl.ds(row, tm), :].astype(BF16))


def _out_router(xp, a4p, bp, xs, q3, kvn3, akt, avt, bkt, bvt, cbias_a, cbias_b, sink_rows, wo, gamma, wr, br):
    tp, tsm = xp.shape[0], xs.shape[0]
    ns, ts = q3.shape[0], q3.shape[1]
    tm = 512
    npt, nst = tp // tm, tsm // tm
    nt = npt + nst
    t = tp + tsm
    sps = ns // npt
    assert sps * npt == ns and ns * ts == tsm and ns >= CACHE_BUFFERS
    pmap = lambda i: (jnp.minimum(i, npt - 1), 0)
    pmap3 = lambda i: (jnp.minimum(i, npt - 1), 0, 0)
    smap = lambda i: (jnp.clip(i - npt, 0, nst - 1), 0)
    cur = lambda i: (jnp.minimum(i, nt - 1), 0)
    const = lambda i: (0, 0)
    return pl.pallas_call(
        functools.partial(_out_router_kernel, prompt_tiles=npt, decode_tiles=nst, seqs_per_step=sps),
        grid=(nt + 1,),
        in_specs=[
            pl.BlockSpec((tm, D_MODEL), pmap),
            pl.BlockSpec((4, tm, LANES), lambda i: (0, jnp.minimum(i, npt - 1), 0)),
            pl.BlockSpec((tm, 512), pmap),
            pl.BlockSpec((tm, D_MODEL), smap),
            pl.BlockSpec((sps, ts, 1024), pmap3),
            pl.BlockSpec((sps, ts, 1280), pmap3),
            pl.BlockSpec(memory_space=pl.ANY),
            pl.BlockSpec(memory_space=pl.ANY),
            pl.BlockSpec((sps, LANES, WIN_B), pmap3),
            pl.BlockSpec((sps, LANES, WIN_B), pmap3),
            pl.BlockSpec((H_A * ts, WIN_A + LANES), const),
            pl.BlockSpec((H_B * ts, WIN_B + LANES), const),
            pl.BlockSpec((H_B * ts, 1), const),
            pl.BlockSpec((D_MODEL, D_MODEL), const),
            pl.BlockSpec((1, D_MODEL), const),
            pl.BlockSpec((3 * D_MODEL, LANES), const),
            pl.BlockSpec((1, LANES), const),
        ],
        out_specs=[
            pl.BlockSpec((tm, D_MODEL), cur),
            pl.BlockSpec((tm, D_MODEL // 2), cur),
            pl.BlockSpec((tm, LANES), lambda i: (jnp.maximum(i - 1, 0), 0)),
            pl.BlockSpec((1, LANES), const),
        ],
        scratch_shapes=[pltpu.VMEM((2, tm, 3 * D_MODEL), BF16), pltpu.VMEM((tsm, D_MODEL), F32),
                        pltpu.VMEM((CACHE_BUFFERS, A_WIDTH, WIN_A), F32),
                        pltpu.VMEM((CACHE_BUFFERS, A_WIDTH, WIN_A), F32),
                        pltpu.SemaphoreType.DMA((2, CACHE_BUFFERS))],
        out_shape=[
            jax.ShapeDtypeStruct((t, D_MODEL), F32),
            jax.ShapeDtypeStruct((t, D_MODEL // 2), jnp.int32),
            jax.ShapeDtypeStruct((t, LANES), F32),
            jax.ShapeDtypeStruct((1, LANES), F32),
        ],
        compiler_params=_cparams(("arbitrary",), vmem=OUT_ROUTER_VMEM),
        name="out_router",
    )(xp, a4p, bp, xs, q3, kvn3, akt, avt, bkt, bvt, cbias_a, cbias_b, sink_rows, wo, gamma, wr, br)


def _sc_gather_rows(table, idx):
    b = idx.shape[0]
    d = table.shape[1]
    w = SC_WINDOW
    per_worker = b // SC_WORKERS
    nwin = per_worker // w
    assert per_worker * SC_WORKERS == b and nwin * w == per_worker
    mesh = plsc.VectorSubcoreMesh(core_axis_name="c", subcore_axis_name="s")

    @functools.partial(
        pl.kernel, mesh=mesh,
        out_type=jax.ShapeDtypeStruct((b, d), table.dtype),
        scratch_types=[pltpu.VMEM((nwin, w), jnp.int32), pltpu.VMEM((2, w, d), table.dtype),
                       pltpu.SemaphoreType.DMA((2,)), pltpu.SemaphoreType.DMA((2,))],
        name="sc_gather_rows",
    )
    def gather(table_hbm, idx_hbm, out_hbm, idx_v, rows_v, sem_in, sem_out):
        wid = lax.axis_index("s") * SC_CORES + lax.axis_index("c")
        base = wid * per_worker
        pltpu.sync_copy(idx_hbm.at[wid], idx_v)

        def fetch(j):
            return pltpu.make_async_copy(table_hbm.at[idx_v.at[j]], rows_v.at[j % 2], sem_in.at[j % 2])

        def flush(j):
            return pltpu.make_async_copy(rows_v.at[j % 2], out_hbm.at[pl.ds(base + j * w, w)],
                                         sem_out.at[j % 2])

        fetch(0).start()
        for j in range(nwin):
            fetch(j).wait()
            if j + 1 < nwin:
                if j >= 1:
                    flush(j - 1).wait()
                fetch(j + 1).start()
            flush(j).start()
        for j in range(max(nwin - 2, 0), nwin):
            flush(j).wait()

    return gather(table, idx.reshape(SC_WORKERS, nwin, w))


def _sc_scatter_rows(x, dest2, nrows):
    t, d = x.shape
    w = SC_SCATTER_WINDOW
    per_worker = t // SC_WORKERS
    nwin = per_worker // w
    assert per_worker * SC_WORKERS == t and nwin * w == per_worker
    mesh = plsc.VectorSubcoreMesh(core_axis_name="c", subcore_axis_name="s")

    @functools.partial(
        pl.kernel, mesh=mesh,
        out_type=jax.ShapeDtypeStruct((nrows, d), x.dtype),
        scratch_types=[pltpu.VMEM((TOP_K, nwin, w), jnp.int32), pltpu.VMEM((2, w, d), x.dtype),
                       pltpu.SemaphoreType.DMA((2,)), pltpu.SemaphoreType.DMA((2,))],
        name="sc_scatter_rows",
    )
    def scatter(x_hbm, dest_hbm, out_hbm, idx_v, rows_v, sem_in, sem_out):
        wid = lax.axis_index("s") * SC_CORES + lax.axis_index("c")
        base = wid * per_worker
        for k in range(TOP_K):
            pltpu.sync_copy(dest_hbm.at[k, wid], idx_v.at[k])

        def fetch(j):
            return pltpu.make_async_copy(x_hbm.at[pl.ds(base + j * w, w)], rows_v.at[j % 2], sem_in.at[j % 2])

        def spread(j, k):
            return pltpu.make_async_copy(rows_v.at[j % 2], out_hbm.at[idx_v.at[k, j]], sem_out.at[j % 2])

        fetch(0).start()
        for j in range(nwin):
            fetch(j).wait()
            if j + 1 < nwin:
                if j >= 1:
                    for k in range(TOP_K):
                        spread(j - 1, k).wait()
                fetch(j + 1).start()
            for k in range(TOP_K):
                spread(j, k).start()
        for j in range(max(nwin - 2, 0), nwin):
            for k in range(TOP_K):
                spread(j, k).wait()

    return scatter(x, dest2.reshape(TOP_K, SC_WORKERS, nwin, w))


def _expert_kernel(be_ref, nu_ref, nv_ref, nx_ref, x_ref, wg_hbm, wu_hbm, wd_hbm, o_ref,
                   wg_s, wu_s, wd_s, wg_f, wu_f, wd_f, slot_s, sem):
    i = pl.program_id(0)
    used = i < nu_ref[0]
    changed = jnp.logical_or(i == 0, be_ref[i] != be_ref[jnp.maximum(i - 1, 0)])

    def weight_copies(e, slot):
        return (pltpu.make_async_copy(wg_hbm.at[e], wg_f.at[slot], sem.at[slot, 0]),
                pltpu.make_async_copy(wu_hbm.at[e], wu_f.at[slot], sem.at[slot, 1]),
                pltpu.make_async_copy(wd_hbm.at[e], wd_f.at[slot], sem.at[slot, 2]))

    @pl.when(i == 0)
    def _():
        slot_s[0] = 0
        for c in weight_copies(be_ref[0], 0):
            c.start()

    @pl.when(jnp.logical_and(used, changed))
    def _():
        slot = slot_s[0]
        for c in weight_copies(be_ref[i], slot):
            c.wait()
        wg_s[...] = wg_f[slot].astype(BF16)
        wu_s[...] = wu_f[slot].astype(BF16)
        wd_s[...] = wd_f[slot].astype(BF16)

        @pl.when(nx_ref[i] != be_ref[i])
        def _():
            for c in weight_copies(nx_ref[i], 1 - slot):
                c.start()

        slot_s[0] = 1 - slot

    @pl.when(used)
    def _():
        row = lax.broadcasted_iota(jnp.int32, x_ref.shape, 0)
        x = _unpack_bf16_pairs(jnp.where(row < nv_ref[i], x_ref[...], 0)).astype(BF16)
        gate = jnp.dot(x, wg_s[...], preferred_element_type=F32)
        up = jnp.dot(x, wu_s[...], preferred_element_type=F32)
        h = (gate * jax.nn.sigmoid(gate) * up).astype(BF16)
        o_ref[...] = _pack_bf16_pairs(jnp.dot(h, wd_s[...], preferred_element_type=F32))

    @pl.when(jnp.logical_not(used))
    def _():
        o_ref[...] = jnp.zeros_like(o_ref)


def _experts(blk_e, n_used, nvalid, next_e, xb, w_gate, w_up, w_down):
    rows = xb.shape[0]
    nblocks = rows // MOE_ROWS
    grid_spec = pltpu.PrefetchScalarGridSpec(
        num_scalar_prefetch=4,
        grid=(nblocks,),
        in_specs=[
            pl.BlockSpec((MOE_ROWS, D_MODEL // 2), lambda i, be, nu, nv, nx: (i, 0)),
            pl.BlockSpec(memory_space=pl.ANY),
            pl.BlockSpec(memory_space=pl.ANY),
            pl.BlockSpec(memory_space=pl.ANY),
        ],
        out_specs=pl.BlockSpec((MOE_ROWS, D_MODEL // 2), lambda i, be, nu, nv, nx: (i, 0)),
        scratch_shapes=[pltpu.VMEM((D_MODEL, D_EXPERT), BF16), pltpu.VMEM((D_MODEL, D_EXPERT), BF16),
                        pltpu.VMEM((D_EXPERT, D_MODEL), BF16),
                        pltpu.VMEM((2, D_MODEL, D_EXPERT), F32), pltpu.VMEM((2, D_MODEL, D_EXPERT), F32),
                        pltpu.VMEM((2, D_EXPERT, D_MODEL), F32),
                        pltpu.SMEM((1,), jnp.int32), pltpu.SemaphoreType.DMA((2, 3))],
    )
    return pl.pallas_call(
        _expert_kernel,
        grid_spec=grid_spec,
        out_shape=jax.ShapeDtypeStruct((rows, D_MODEL // 2), jnp.int32),
        compiler_params=_cparams(("arbitrary",)),
        name="experts",
    )(blk_e, n_used, nvalid, next_e, xb, w_gate, w_up, w_down)


def _combine_kernel(x1_ref, y1_ref, y2_ref, route_ref, g_ref, outp_ref, outs_ref, *, prompt_tiles):
    r = route_ref[...]
    x = (x1_ref[...] + r[:, 2:3] * _unpack_bf16_pairs(y1_ref[...])
         + r[:, 3:4] * _unpack_bf16_pairs(y2_ref[...]))
    ms = jnp.mean(x * x, axis=-1, keepdims=True)
    y = x * lax.rsqrt(ms + EPS) * g_ref[...]
    i = pl.program_id(0)

    @pl.when(i < prompt_tiles)
    def _():
        outp_ref[...] = y

    @pl.when(i >= prompt_tiles)
    def _():
        outs_ref[...] = y


def _combine_norm(x1, ygath, route, gamma, tp):
    t = x1.shape[0]
    tm = 512
    nt, npt = t // tm, tp // tm
    return pl.pallas_call(
        functools.partial(_combine_kernel, prompt_tiles=npt),
        grid=(nt,),
        in_specs=[
            pl.BlockSpec((tm, D_MODEL), lambda i: (i, 0)),
            pl.BlockSpec((tm, D_MODEL // 2), lambda i: (i, 0)),
            pl.BlockSpec((tm, D_MODEL // 2), lambda i: (i + nt, 0)),
            pl.BlockSpec((tm, LANES), lambda i: (i, 0)),
            pl.BlockSpec((1, D_MODEL), lambda i: (0, 0)),
        ],
        out_specs=[
            pl.BlockSpec((tm, D_MODEL), lambda i: (jnp.minimum(i, npt - 1), 0)),
            pl.BlockSpec((tm, D_MODEL), lambda i: (jnp.maximum(i - npt, 0), 0)),
        ],
        out_shape=[jax.ShapeDtypeStruct((tp, D_MODEL), F32), jax.ShapeDtypeStruct((t - tp, D_MODEL), F32)],
        compiler_params=_cparams(("arbitrary",)),
        name="combine_norm",
    )(x1, ygath, ygath, route, gamma)


def _band_index():
    c = (2 * QB - np.arange(2 * QB)) % (2 * QB)
    return c, c <= QB


def _bias_a_prompt(table_a):
    c, valid = _band_index()
    idx = np.stack([_t5_bucket_np(d * np.clip(QB - c, 0, QB)) for d in DILATIONS])
    return jnp.where(valid, jnp.transpose(table_a[idx], (0, 2, 1)) * LOG2E, NEG)


def _bias_b_prompt(table_b):
    c, valid = _band_index()
    valid = valid & (c >= 1)
    h = jnp.where(valid, table_b[_t5_bucket_np(np.clip(QB - c, 0, QB))].T * LOG2E, NEG)
    return jnp.transpose(h.reshape(KV_B, G_B, 2 * QB), (1, 0, 2)).reshape(H_B, 2 * QB)


def _sample_bias(table, span, t, log2_weight):
    cols = span + LANES
    period = cols + LANES
    x = np.arange(period)
    dist = np.where(x >= period - t, span - x + period, span - x)
    extra = log2_weight(dist)
    valid = np.isfinite(extra)
    u = jnp.where(valid, table[_t5_bucket_np(np.maximum(dist, 0))].T * LOG2E
                  + np.where(valid, extra, 0.0).astype(np.float32), NEG)
    rows = jnp.tile(u, (1, t))[:, :t * (period - 1)].reshape(u.shape[0], t, period - 1)[:, :, :cols]
    return rows.reshape(u.shape[0] * t, cols)


def _bias_a_sample(table_a, t):
    def log2_count(dist):
        count = np.zeros(dist.shape, np.int64)
        for w, d in zip(WINDOWS, DILATIONS):
            count += (dist >= 0) & (dist % d == 0) & (dist <= w)
        return np.where(count > 0, np.log2(np.maximum(count, 1)), -np.inf)

    return _sample_bias(table_a, WIN_A, t, log2_count)


def _bias_b_sample(table_b, t):
    return _sample_bias(table_b, WIN_B, t,
                        lambda dist: np.where((dist >= 0) & (dist < WIN_B), 0.0, -np.inf))


def _dest_kernel(route_ref, cnt_ref, tri_ref, dest_ref, meta_ref, run_scr, pst_scr):
    i = pl.program_id(0)
    tm = route_ref.shape[0]
    r = route_ref[...]
    lane = lax.broadcasted_iota(jnp.int32, (tm, LANES), 1)
    lanef = lane.astype(F32)
    oh0 = lanef == r[:, 0:1]
    oh1 = lanef == r[:, 1:2]
    ohf = jnp.concatenate([oh0, oh1], axis=0).astype(F32)

    @pl.when(i == 0)
    def _():
        cnt = jnp.broadcast_to(cnt_ref[...], (LANES, LANES))
        padded = jnp.floor((cnt + (MOE_ROWS - 1)) * (1.0 / MOE_ROWS)) * MOE_ROWS
        lane_e = lax.broadcasted_iota(jnp.int32, (LANES, LANES), 1)
        x = padded
        for sh in (1, 2, 4, 8, 16, 32, 64):
            x = x + jnp.where(lane_e >= sh, pltpu.roll(x, sh, 1), 0.0)
        pst_scr[...] = (x - padded)[0:1]
        run_scr[...] = jnp.zeros_like(run_scr)
        wide = lambda v: jnp.concatenate([v.T, v.T], axis=1)
        cnt_t, bend_t = wide(cnt), wide(x * (1.0 / MOE_ROWS))
        bstart_t = wide((x - padded) * (1.0 / MOE_ROWS))
        blk = lax.broadcasted_iota(jnp.int32, (LANES, 2 * LANES), 1).astype(F32)
        exp = lax.broadcasted_iota(jnp.int32, (LANES, 2 * LANES), 0)
        real = exp < N_EXPERTS
        blk_e = jnp.minimum(jnp.sum(jnp.where(real & (bend_t <= blk), 1.0, 0.0), axis=0, keepdims=True),
                            N_EXPERTS - 1.0)
        mine = exp.astype(F32) == blk_e
        within = blk[0:1] - jnp.sum(jnp.where(mine, bstart_t, 0.0), axis=0, keepdims=True)
        nvalid = jnp.clip(jnp.sum(jnp.where(mine, cnt_t, 0.0), axis=0, keepdims=True) - within * MOE_ROWS,
                          0.0, float(MOE_ROWS))
        n_used = jnp.max(jnp.where(real, bend_t, 0.0), axis=0, keepdims=True)
        later = real & (exp.astype(F32) > blk_e) & (cnt_t > 0.0)
        nxt = jnp.min(jnp.where(later, exp.astype(F32), float(LANES)), axis=0, keepdims=True)
        nxt = jnp.where(nxt >= N_EXPERTS, blk_e, nxt)
        meta_ref[...] = jnp.concatenate([blk_e, nvalid, n_used, nxt, jnp.zeros((4, 2 * LANES), F32)],
                                        axis=0).astype(jnp.int32)

    base = run_scr[...] + pst_scr[...] - 1.0
    vals = []
    for c in range(2 * tm // LANES):
        ohc = ohf[c * LANES:(c + 1) * LANES]
        vals.append(jnp.dot(tri_ref[...], ohc.astype(BF16), preferred_element_type=F32) + base)
        base = base + jnp.sum(ohc, axis=0, keepdims=True)
    val = jnp.concatenate(vals, axis=0)
    d0 = jnp.sum(jnp.where(oh0, val[:tm], 0.0), axis=-1, keepdims=True)
    d1 = jnp.sum(jnp.where(oh1, val[tm:], 0.0), axis=-1, keepdims=True)
    tile = jnp.where(lane == 0, d0, jnp.where(lane == 1, d1, 0.0))
    dest_ref[...] = tile.T[:8].astype(jnp.int32)
    run_scr[...] += jnp.sum(ohf, axis=0, keepdims=True)


def _dispatch(route, cnt):
    t = route.shape[0]
    tm = 1024
    tri = (jnp.arange(LANES)[:, None] >= jnp.arange(LANES)[None, :]).astype(BF16)
    nblocks = -(-t * TOP_K // MOE_ROWS) + N_EXPERTS
    assert nblocks <= 2 * LANES
    dest, meta = pl.pallas_call(
        _dest_kernel,
        grid=(t // tm,),
        in_specs=[pl.BlockSpec((tm, LANES), lambda i: (i, 0)),
                  pl.BlockSpec((1, LANES), lambda i: (0, 0)),
                  pl.BlockSpec((LANES, LANES), lambda i: (0, 0))],
        out_specs=[pl.BlockSpec((8, tm), lambda i: (0, i)),
                   pl.BlockSpec((8, 2 * LANES), lambda i: (0, 0))],
        out_shape=[jax.ShapeDtypeStruct((8, t), jnp.int32), jax.ShapeDtypeStruct((8, 2 * LANES), jnp.int32)],
        scratch_shapes=[pltpu.VMEM((1, LANES), F32), pltpu.VMEM((1, LANES), F32)],
        compiler_params=_cparams(("arbitrary",)),
        name="moe_dest",
    )(route, cnt, tri)
    return dest[:TOP_K], meta[0, :nblocks], meta[2, :1], meta[1, :nblocks], meta[3, :nblocks]


def kernel(x_prompt, x_sample, cache_a_k, cache_a_v, cache_b_k, cache_b_v, rel_bias_table, attn_norm, w_in,
           w_out, attn_sinks, ffn_norm, w_router_group, b_router_group, w_router_expert, b_router_expert,
           w_gate, w_up, w_down, final_norm):
    s = x_prompt.shape[1]
    ns, ts = x_sample.shape[0], x_sample.shape[1]
    table_a = rel_bias_table[:, :H_A]
    table_b = rel_bias_table[:, H_A:]

    w = w_in[0]
    wqa, wka, wva, wqb, wkb, wvb = (w[:, 0:512], w[:, 512:1024], w[:, 1024:1536], w[:, 1536:2048],
                                    w[:, 2048:2176], w[:, 2176:2304])
    wqb = jnp.transpose(wqb.reshape(D_MODEL, KV_B, G_B, HEAD_DIM), (0, 2, 1, 3)).reshape(D_MODEL, 512)
    wp = jnp.concatenate([wka, wva, wqa, wqb, wkb, wvb], axis=1).astype(BF16)
    cscale = jnp.concatenate([jnp.ones((1, 1024), F32), jnp.full((1, 1024), SCALE * LOG2E, F32),
                              jnp.ones((1, 256), F32)], axis=1)
    wo = w_out[0]
    wo_b = jnp.transpose(wo[512:].reshape(KV_B, G_B, HEAD_DIM, D_MODEL), (1, 0, 2, 3)).reshape(512, D_MODEL)
    wo_p = jnp.concatenate([wo[:512], wo_b], axis=0).astype(BF16)
    wr = jnp.concatenate([w_router_group[0],
                          jnp.transpose(w_router_expert[0], (1, 0, 2)).reshape(D_MODEL, N_EXPERTS),
                          jnp.zeros((D_MODEL, LANES - N_GROUPS - N_EXPERTS), F32)], axis=1)
    wr_hi = wr.astype(BF16)
    wr = jnp.concatenate([wr_hi, wr_hi, (wr - wr_hi.astype(F32)).astype(BF16)], axis=0)
    br = jnp.concatenate([b_router_group[0], b_router_expert[0].reshape(N_EXPERTS),
                          jnp.zeros((LANES - N_GROUPS - N_EXPERTS,), F32)]).reshape(1, LANES)
    sinks2 = attn_sinks[0] * LOG2E
    sinks_gk = jnp.transpose(sinks2.reshape(KV_B, G_B), (1, 0)).reshape(H_B)
    sink_rows_p = jnp.repeat(sinks_gk, QB).reshape(G_B, 1, 2 * QB)
    sink_rows_s = jnp.repeat(sinks2, ts).reshape(H_B * ts, 1)
    emat = jnp.tile(jnp.arange(LANES)[:, None] == (jnp.arange(A_WIDTH)[None, :] // HEAD_DIM),
                    (3, 1)).astype(BF16)
    attn_g = attn_norm[0].reshape(1, D_MODEL)
    ffn_g = ffn_norm[0].reshape(1, D_MODEL)

    xp = x_prompt.reshape(s, D_MODEL)
    aperm, qb_p, kvb_p, akv32, bkv32 = _proj_prompt(xp, attn_g, wp, cscale)
    a4 = _attn_a_prompt(aperm, _bias_a_prompt(table_a), emat)
    ob_p = _attn_b_prompt(qb_p, kvb_p, _bias_b_prompt(table_b), sink_rows_p)

    xs = x_sample.reshape(ns * ts, D_MODEL)
    q_s, kv_s = _proj_sample(xs, attn_g, wp, cscale)
    akt = jnp.transpose(cache_a_k[0], (0, 2, 3, 1)).reshape(ns, A_WIDTH, WIN_A)
    avt = jnp.transpose(cache_a_v[0], (0, 2, 3, 1)).reshape(ns, A_WIDTH, WIN_A)
    bkt = jnp.transpose(cache_b_k[0], (0, 2, 3, 1)).reshape(ns, LANES, WIN_B)
    bvt = jnp.transpose(cache_b_v[0], (0, 2, 3, 1)).reshape(ns, LANES, WIN_B)

    x1, xn, route, cnt = _out_router(xp, a4, ob_p, xs, q_s.reshape(ns, ts, 1024), kv_s.reshape(ns, ts, 1280),
                                     akt, avt, bkt, bvt, _bias_a_sample(table_a, ts),
                                     _bias_b_sample(table_b, ts), sink_rows_s, wo_p, ffn_g, wr, br)
    dest2, blk_e, n_used, nvalid, next_e = _dispatch(route, cnt)
    xb = _sc_scatter_rows(xn, dest2, blk_e.shape[0] * MOE_ROWS)
    yb = _experts(blk_e, n_used, nvalid, next_e, xb, w_gate[0], w_up[0], w_down[0])
    y_p, y_s = _combine_norm(x1, _sc_gather_rows(yb, dest2.reshape(-1)), route, final_norm.reshape(1, D_MODEL), s)

    y_prompt = y_p.reshape(1, s, D_MODEL)
    y_sample = y_s.reshape(ns, ts, D_MODEL)
    keep_a, keep_b = min(WIN_A, s), min(WIN_B, s)
    pak = akv32[SPAN - keep_a:, :512].reshape(1, 1, keep_a, H_A, HEAD_DIM)
    pav = akv32[SPAN - keep_a:, 512:].reshape(1, 1, keep_a, H_A, HEAD_DIM)
    pbk = bkv32[SPAN - keep_b:, :128].reshape(1, 1, keep_b, KV_B, HEAD_DIM)
    pbv = bkv32[SPAN - keep_b:, 128:].reshape(1, 1, keep_b, KV_B, HEAD_DIM)
    sak = kv_s[:, 0:512].reshape(1, ns, ts, H_A, HEAD_DIM)
    sav = kv_s[:, 512:1024].reshape(1, ns, ts, H_A, HEAD_DIM)
    sbk = kv_s[:, 1024:1152].reshape(1, ns, ts, KV_B, HEAD_DIM)
    sbv = kv_s[:, 1152:1280].reshape(1, ns, ts, KV_B, HEAD_DIM)
    return (y_prompt, y_sample, pak, pav, pbk, pbv, sak, sav, sbk, sbv)
```

```python
import functools
import math

import jax
import jax.numpy as jnp
import numpy as np
from jax import lax
from jax.experimental import pallas as pl
from jax.experimental.pallas import tpu as pltpu
from jax.experimental.pallas import tpu_sc as plsc

D_MODEL = 1024
HEAD_DIM = 64
H_A = 8
H_B = 8
KV_B = 2
G_B = 4
DILATIONS = (1, 4, 16)
WINDOWS = (128, 512, 2048)
WIN_A = 2048
WIN_B = 128
NUM_BUCKETS = 32
MAX_DISTANCE = 2048
N_GROUPS = 4
EXPERTS_PER_GROUP = 8
N_EXPERTS = 32
TOP_K = 2
D_EXPERT = 512
EPS = 1e-5
SCALE = HEAD_DIM ** -0.5

LANES = 128
SPAN = 2048
QB = 128
NCHUNK = 9
A_WIDTH = H_A * HEAD_DIM
MOE_ROWS = 512
SC_CORES = 2
SC_SUBCORES = 16
SC_WORKERS = SC_CORES * SC_SUBCORES
SC_WINDOW = 64
SC_SCATTER_WINDOW = 32
NEG = -1e30
LOG2E = math.log2(math.e)
B_STEP = 1024
V7X_VMEM_BYTES = 64 * 1024 * 1024
VMEM_LIMIT = V7X_VMEM_BYTES - 8 * 1024 * 1024
CACHE_BUFFERS = 3
OUT_ROUTER_VMEM = V7X_VMEM_BYTES - 4 * 1024 * 1024

F32 = jnp.float32
BF16 = jnp.bfloat16


def _t5_bucket_np(dist):
    dist = np.asarray(dist, np.int64)
    max_exact = NUM_BUCKETS // 2
    d = np.maximum(dist, 1).astype(np.float32)
    ratio = np.log(d / np.float32(max_exact)) / np.float32(math.log(MAX_DISTANCE / max_exact))
    large = max_exact + (ratio * np.float32(NUM_BUCKETS - max_exact)).astype(np.int32)
    large = np.minimum(large, NUM_BUCKETS - 1)
    return np.where(dist < max_exact, dist, large).astype(np.int32)


def _cparams(sem, vmem=VMEM_LIMIT):
    return pltpu.CompilerParams(dimension_semantics=sem, vmem_limit_bytes=vmem)


def _proj_prompt_kernel(x_ref, g_ref, w_ref, cs_ref, aperm_ref, qb_ref, kvb_ref, akv_ref, bkv_ref,
                        h_scr, p_scr):
    n = pl.program_id(1)

    @pl.when(n == 0)
    def _():
        x = x_ref[...]
        ms = jnp.mean(x * x, axis=-1, keepdims=True)
        h_scr[...] = (x * lax.rsqrt(ms + EPS) * g_ref[...]).astype(BF16)

    p = jnp.dot(h_scr[...], w_ref[...], preferred_element_type=F32) * cs_ref[...]

    @pl.when(n < 6)
    def _():
        aperm_ref[0] = p.astype(BF16)
        p_scr[0, 0] = p[:, :LANES]
        p_scr[0, 1] = p[:, LANES:]
        quarter = SPAN // 4
        for r in range(4):
            lo = p_scr[0, 0, pl.ds(r, quarter, stride=4), :]
            hi = p_scr[0, 1, pl.ds(r, quarter, stride=4), :]
            p_scr[1, 0, r * quarter:(r + 1) * quarter, :] = lo
            p_scr[1, 1, r * quarter:(r + 1) * quarter, :] = hi
            aperm_ref[1, r * quarter:(r + 1) * quarter, :] = jnp.concatenate([lo, hi], axis=1).astype(BF16)
        for r16 in range(16):
            start = (r16 % 4) * quarter + r16 // 4
            t = jnp.concatenate([p_scr[1, 0, pl.ds(start, QB, stride=4), :],
                                 p_scr[1, 1, pl.ds(start, QB, stride=4), :]], axis=1)
            aperm_ref[2, r16 * QB:(r16 + 1) * QB, :] = t.astype(BF16)

    @pl.when(n < 4)
    def _():
        akv_ref[...] = p

    @pl.when(jnp.logical_or(n == 6, n == 7))
    def _():
        qb_ref[...] = p.astype(BF16)

    @pl.when(n == 8)
    def _():
        kvb_ref[...] = p.astype(BF16)
        bkv_ref[...] = p


def _proj_prompt(x, gamma, w, cscale):
    s = x.shape[0]
    nspan = s // SPAN
    return pl.pallas_call(
        _proj_prompt_kernel,
        grid=(nspan, NCHUNK),
        in_specs=[
            pl.BlockSpec((SPAN, D_MODEL), lambda b, n: (b, 0)),
            pl.BlockSpec((1, D_MODEL), lambda b, n: (0, 0)),
            pl.BlockSpec((D_MODEL, 256), lambda b, n: (0, n)),
            pl.BlockSpec((1, 256), lambda b, n: (0, n)),
        ],
        out_specs=[
            pl.BlockSpec((3, SPAN, 256), lambda b, n: (0, b, jnp.minimum(n, 5))),
            pl.BlockSpec((SPAN, 256), lambda b, n: (b, jnp.clip(n - 6, 0, 1))),
            pl.BlockSpec((SPAN, 256), lambda b, n: (b, 0)),
            pl.BlockSpec((SPAN, 256), lambda b, n: (0, jnp.where(b == nspan - 1, jnp.minimum(n, 3), 0))),
            pl.BlockSpec((SPAN, 256), lambda b, n: (0, 0)),
        ],
        out_shape=[
            jax.ShapeDtypeStruct((3, s, 3 * A_WIDTH), BF16),
            jax.ShapeDtypeStruct((s, 512), BF16),
            jax.ShapeDtypeStruct((s, 256), BF16),
            jax.ShapeDtypeStruct((SPAN, 1024), F32),
            jax.ShapeDtypeStruct((SPAN, 256), F32),
        ],
        scratch_shapes=[pltpu.VMEM((SPAN, D_MODEL), BF16), pltpu.VMEM((2, 2, SPAN, LANES), F32)],
        compiler_params=_cparams(("arbitrary", "arbitrary")),
        name="proj_prompt",
    )(x, gamma, w, cscale)


def _proj_sample_kernel(x_ref, g_ref, w_ref, cs_ref, q_ref, kv_ref):
    x = x_ref[...]
    ms = jnp.mean(x * x, axis=-1, keepdims=True)
    h = (x * lax.rsqrt(ms + EPS) * g_ref[...]).astype(BF16)
    p = jnp.dot(h, w_ref[...], preferred_element_type=F32) * cs_ref[...]
    kv_ref[:, :1024] = p[:, :1024]
    kv_ref[:, 1024:] = p[:, 2048:]
    q_ref[...] = p[:, 1024:2048]


def _proj_sample(x, gamma, w, cscale):
    t = x.shape[0]
    tm = 512
    return pl.pallas_call(
        _proj_sample_kernel,
        grid=(t // tm,),
        in_specs=[
            pl.BlockSpec((tm, D_MODEL), lambda i: (i, 0)),
            pl.BlockSpec((1, D_MODEL), lambda i: (0, 0)),
            pl.BlockSpec((D_MODEL, 2304), lambda i: (0, 0)),
            pl.BlockSpec((1, 2304), lambda i: (0, 0)),
        ],
        out_specs=[
            pl.BlockSpec((tm, 1024), lambda i: (i, 0)),
            pl.BlockSpec((tm, 1280), lambda i: (i, 0)),
        ],
        out_shape=[
            jax.ShapeDtypeStruct((t, 1024), F32),
            jax.ShapeDtypeStruct((t, 1280), F32),
        ],
        compiler_params=_cparams(("arbitrary",)),
        name="proj_sample",
    )(x, gamma, w, cscale)


def _spread_heads(w, e3_ref):
    hi = w.astype(BF16)
    r1 = w - hi.astype(F32)
    mid = r1.astype(BF16)
    low = (r1 - mid.astype(F32)).astype(BF16)
    return jnp.dot(jnp.concatenate([hi, mid, low], axis=1), e3_ref[...], preferred_element_type=F32)


def _pair_tile(q2, kk, vv, bias_t, lo, sink=None, normalize=True):
    zero = jnp.zeros_like(q2)
    qq = jnp.concatenate([jnp.where(lo, q2, zero), jnp.where(lo, zero, q2)], axis=0)
    st = lax.dot_general(kk, qq, (((1,), (1,)), ((), ())), preferred_element_type=F32)
    st = st + bias_t
    m = jnp.max(st, axis=0, keepdims=True)
    if sink is not None:
        m = jnp.maximum(m, sink)
    p = jnp.exp2(st - m)
    den = jnp.sum(p, axis=0, keepdims=True)
    if sink is not None:
        den = den + jnp.exp2(sink - m)
    pn = (p * (1.0 / den) if normalize else p).astype(BF16)
    o = lax.dot_general(pn, vv, (((0,), (0,)), ((), ())), preferred_element_type=F32)
    return jnp.where(lo, o[:QB], o[QB:]), m, den


def _fill_band_tiles(h_ref, bias_scr):
    nk = 2 * QB
    prev = lax.broadcasted_iota(jnp.int32, (nk, nk), 0) < QB
    for pair in range(h_ref.shape[0] // 2):
        halves = []
        for hh in range(2):
            row = h_ref[2 * pair + hh:2 * pair + hh + 1, :]
            band = pltpu.roll(jnp.broadcast_to(row, (nk, nk)), 0, 1, stride=1, stride_axis=0)
            halves.append(band[:, :QB])
        tile = jnp.concatenate(halves, axis=1)
        bias_scr[0, pair] = tile
        bias_scr[1, pair] = jnp.where(prev, NEG, tile)


def _attn_a_kernel(q_ref, kvc_ref, kvp_ref, h_ref, e_ref, out_ref, o_scr, st_scr, bias_scr):
    b = pl.program_id(0)
    g = pl.program_id(1)
    nblk = jnp.where(g == 0, 16, jnp.where(g == 1, 4, 1))
    lane = lax.broadcasted_iota(jnp.int32, (QB, LANES), 1)
    lo = lane < HEAD_DIM

    @pl.when(b == 0)
    def _():
        _fill_band_tiles(h_ref, bias_scr.at[g])

    bias_ref = bias_scr.at[g]

    for cb in range(SPAN // QB):
        first = lax.rem(jnp.int32(cb), nblk) == 0
        rows = slice(cb * QB, (cb + 1) * QB)
        prow_c = max(cb - 1, 0) * QB
        prow_p = pl.multiple_of(jnp.where(first, cb + nblk - 1, 0) * QB, QB)
        variant = jnp.logical_and(first, b == 0).astype(jnp.int32)
        maxes, dens = [], []
        for hp in range(4):
            ks = slice(hp * LANES, (hp + 1) * LANES)
            vs = slice(A_WIDTH + hp * LANES, A_WIDTH + (hp + 1) * LANES)
            kp = jnp.where(first, kvp_ref[pl.ds(prow_p, QB), ks], kvc_ref[prow_c:prow_c + QB, ks])
            vp = jnp.where(first, kvp_ref[pl.ds(prow_p, QB), vs], kvc_ref[prow_c:prow_c + QB, vs])
            kk = jnp.concatenate([kp, kvc_ref[rows, ks]], axis=0)
            vv = jnp.concatenate([vp, kvc_ref[rows, vs]], axis=0)
            o, m, den = _pair_tile(q_ref[rows, ks], kk, vv, bias_ref[variant, hp], lo, normalize=False)
            o_scr[g, hp, rows, :] = o
            maxes += [m[:, :QB], m[:, QB:]]
            dens += [den[:, :QB], den[:, QB:]]
        sm = jnp.concatenate(maxes + dens + [jnp.zeros((LANES - 2 * H_A, QB), F32)], axis=0)
        st_scr[g, rows, :] = sm.T

    @pl.when(g == 2)
    def _():
        def merge(c, carry):
            r2 = lax.rem(c, 4) * (SPAN // 4) + c // 4
            r3 = pl.multiple_of(c * QB, QB)
            s1 = st_scr[0, pl.ds(c, QB, stride=16), :]
            s2 = st_scr[1, pl.ds(r2, QB, stride=4), :]
            s3 = st_scr[2, pl.ds(r3, QB), :]
            mx = jnp.maximum(jnp.maximum(s1, s2), s3)
            w1 = jnp.exp2(s1 - mx)
            w2 = jnp.exp2(s2 - mx)
            w3 = jnp.exp2(s3 - mx)
            shift = LANES - H_A
            tot = (w1 * pltpu.roll(s1, shift, 1) + w2 * pltpu.roll(s2, shift, 1)
                   + w3 * pltpu.roll(s3, shift, 1))
            head_lane = lax.broadcasted_iota(jnp.int32, (QB, LANES), 1) < H_A
            a1 = _spread_heads(jnp.where(head_lane, w1 / tot, 0.0), e_ref)
            a2 = _spread_heads(jnp.where(head_lane, w2 / tot, 0.0), e_ref)
            a3 = _spread_heads(jnp.where(head_lane, w3 / tot, 0.0), e_ref)
            for hp in range(4):
                sl = slice(hp * LANES, (hp + 1) * LANES)
                o1 = o_scr[0, hp, pl.ds(c, QB, stride=16), :]
                o2 = o_scr[1, hp, pl.ds(r2, QB, stride=4), :]
                o3 = o_scr[2, hp, pl.ds(r3, QB), :]
                out_ref[hp, pl.ds(c, QB, stride=16), :] = a1[:, sl] * o1 + a2[:, sl] * o2 + a3[:, sl] * o3
            return carry

        lax.fori_loop(0, 16, merge, 0, unroll=4)


def _attn_a_prompt(aperm, bias_a, emat):
    s = aperm.shape[1]
    nspan = s // SPAN
    return pl.pallas_call(
        _attn_a_kernel,
        grid=(nspan, 3),
        in_specs=[
            pl.BlockSpec((None, SPAN, A_WIDTH), lambda b, g: (g, b, 2)),
            pl.BlockSpec((None, SPAN, 2 * A_WIDTH), lambda b, g: (g, b, 0)),
            pl.BlockSpec((None, SPAN, 2 * A_WIDTH), lambda b, g: (g, jnp.maximum(b - 1, 0), 0)),
            pl.BlockSpec((None, H_A, 2 * QB), lambda b, g: (g, 0, 0)),
            pl.BlockSpec((3 * LANES, A_WIDTH), lambda b, g: (0, 0)),
        ],
        out_specs=pl.BlockSpec((4, SPAN, LANES), lambda b, g: (0, b, 0)),
        out_shape=jax.ShapeDtypeStruct((4, s, LANES), F32),
        scratch_shapes=[pltpu.VMEM((3, 4, SPAN, LANES), F32), pltpu.VMEM((3, SPAN, LANES), F32),
                        pltpu.VMEM((3, 2, 4, 2 * QB, 2 * QB), F32)],
        compiler_params=_cparams(("arbitrary", "arbitrary")),
        name="attn_a_prompt",
    )(aperm, aperm, aperm, bias_a, emat)


def _attn_b_kernel(q_ref, kvc_ref, kvp_ref, h_ref, sink_ref, out_ref, bias_ref):
    i = pl.program_id(0)
    lane = lax.broadcasted_iota(jnp.int32, (QB, LANES), 1)
    lo = lane < HEAD_DIM

    @pl.when(i == 0)
    def _():
        _fill_band_tiles(h_ref, bias_ref)

    variant = (i == 0).astype(jnp.int32)
    for j in range(B_STEP // QB):
        rows = slice(j * QB, (j + 1) * QB)
        if j == 0:
            kp, vp = kvp_ref[:, :LANES], kvp_ref[:, LANES:]
        else:
            kp, vp = kvc_ref[(j - 1) * QB:j * QB, :LANES], kvc_ref[(j - 1) * QB:j * QB, LANES:]
        kk = jnp.concatenate([kp, kvc_ref[rows, :LANES]], axis=0)
        vv = jnp.concatenate([vp, kvc_ref[rows, LANES:]], axis=0)
        for g in range(G_B):
            bias_t = bias_ref[variant, g] if j == 0 else bias_ref[0, g]
            o, _, _ = _pair_tile(q_ref[rows, g * LANES:(g + 1) * LANES], kk, vv, bias_t, lo, sink=sink_ref[g])
            out_ref[rows, g * LANES:(g + 1) * LANES] = o.astype(BF16)


def _attn_b_prompt(qb, kvb, bias_b, sink_rows):
    s = qb.shape[0]
    per = B_STEP // QB
    return pl.pallas_call(
        _attn_b_kernel,
        grid=(s // B_STEP,),
        in_specs=[
            pl.BlockSpec((B_STEP, 512), lambda i: (i, 0)),
            pl.BlockSpec((B_STEP, 256), lambda i: (i, 0)),
            pl.BlockSpec((QB, 256), lambda i: (jnp.maximum(i * per - 1, 0), 0)),
            pl.BlockSpec((H_B, 2 * QB), lambda i: (0, 0)),
            pl.BlockSpec((G_B, 1, 2 * QB), lambda i: (0, 0, 0)),
        ],
        out_specs=pl.BlockSpec((B_STEP, 512), lambda i: (i, 0)),
        out_shape=jax.ShapeDtypeStruct((s, 512), BF16),
        scratch_shapes=[pltpu.VMEM((2, G_B, 2 * QB, 2 * QB), F32)],
        compiler_params=_cparams(("arbitrary",)),
        name="attn_b_prompt",
    )(qb, kvb, kvb, bias_b, sink_rows)


def _sample_attention(q, kvn, akt, avt, bkt, bvt, cba, cbb, sink):
    t = q.shape[0]
    kvn_p = jnp.concatenate([kvn, jnp.zeros((LANES - t, kvn.shape[1]), F32)], axis=0).astype(BF16)
    lane_a = lax.broadcasted_iota(jnp.int32, (t, A_WIDTH), 1) // HEAD_DIM

    qa = q[:, :A_WIDTH]
    qbd = jnp.concatenate([jnp.where(lane_a == h, qa, 0.0) for h in range(H_A)], axis=0).astype(BF16)
    s_c = jnp.dot(qbd, akt.astype(BF16), preferred_element_type=F32)
    s_n = lax.dot_general(qbd, kvn_p[:, :A_WIDTH], (((1,), (1,)), ((), ())), preferred_element_type=F32)
    s = jnp.concatenate([s_c, s_n], axis=1) + cba
    m = jnp.max(s, axis=-1, keepdims=True)
    p = jnp.exp2(s - m)
    l = jnp.sum(p, axis=-1, keepdims=True)
    pb = p.astype(BF16)
    o_n = jnp.dot(pb[:, WIN_A:], kvn_p[:, A_WIDTH:2 * A_WIDTH], preferred_element_type=F32)
    o_all = lax.dot_general(pb[:, :WIN_A], avt.astype(BF16), (((1,), (1,)), ((), ())),
                            preferred_element_type=F32) + o_n
    o_sel = jnp.zeros((t, A_WIDTH), F32)
    l_b = jnp.ones((t, A_WIDTH), F32)
    for h in range(H_A):
        sel = lane_a == h
        o_sel = jnp.where(sel, o_all[h * t:(h + 1) * t], o_sel)
        l_b = jnp.where(sel, l[h * t:(h + 1) * t], l_b)
    oa = o_sel / l_b

    lane_b = lax.broadcasted_iota(jnp.int32, (G_B * t, LANES), 1)
    lo = lane_b < HEAD_DIM
    qb2 = jnp.concatenate([q[:, A_WIDTH + g * LANES:A_WIDTH + (g + 1) * LANES] for g in range(G_B)], axis=0)
    qm = jnp.concatenate([jnp.where(lo, qb2, 0.0), jnp.where(lo, 0.0, qb2)], axis=0).astype(BF16)
    kb_n = kvn_p[:, 2 * A_WIDTH:2 * A_WIDTH + LANES]
    vb_n = kvn_p[:, 2 * A_WIDTH + LANES:]
    sb_c = jnp.dot(qm, bkt.astype(BF16), preferred_element_type=F32)
    sb_n = lax.dot_general(qm, kb_n, (((1,), (1,)), ((), ())), preferred_element_type=F32)
    sb = jnp.concatenate([sb_c, sb_n], axis=1) + cbb
    mb = jnp.maximum(jnp.max(sb, axis=-1, keepdims=True), sink)
    pbb = jnp.exp2(sb - mb)
    den = jnp.sum(pbb, axis=-1, keepdims=True) + jnp.exp2(sink - mb)
    pbb = pbb.astype(BF16)
    ob = lax.dot_general(pbb[:, :WIN_B], bvt.astype(BF16), (((1,), (1,)), ((), ())),
                         preferred_element_type=F32)
    ob = (ob + jnp.dot(pbb[:, WIN_B:], vb_n, preferred_element_type=F32)) / den
    half = G_B * t
    lo8 = lo[:t]
    ob = jnp.concatenate([jnp.where(lo8, ob[g * t:(g + 1) * t], ob[half + g * t:half + (g + 1) * t])
                          for g in range(G_B)], axis=1)
    return oa, ob


def _route(logits):
    lane = lax.broadcasted_iota(jnp.int32, logits.shape, 1).astype(F32)
    big = jnp.float32(1 << 20)
    ninf = jnp.float32(-jnp.inf)
    gmask = lane < N_GROUPS
    lg = jnp.where(gmask, logits, ninf)
    gmax = jnp.max(lg, axis=-1, keepdims=True)
    grp = jnp.min(jnp.where(lg == gmax, lane, big), axis=-1, keepdims=True)
    pg_top = 1.0 / jnp.sum(jnp.exp(lg - gmax), axis=-1, keepdims=True)
    e0 = N_GROUPS + grp * EXPERTS_PER_GROUP
    emask = jnp.logical_and(lane >= e0, lane < e0 + EXPERTS_PER_GROUP)
    le = jnp.where(emask, logits, ninf)
    emax = jnp.max(le, axis=-1, keepdims=True)
    esum = jnp.sum(jnp.exp(le - emax), axis=-1, keepdims=True)
    i1 = jnp.min(jnp.where(le == emax, lane, big), axis=-1, keepdims=True)
    le2 = jnp.where(lane == i1, ninf, le)
    e2max = jnp.max(le2, axis=-1, keepdims=True)
    i2 = jnp.min(jnp.where(le2 == e2max, lane, big), axis=-1, keepdims=True)
    p1 = 1.0 / esum
    p2 = jnp.exp(e2max - emax) / esum
    g1 = pg_top * p1 / (p1 + p2)
    g2 = pg_top * p2 / (p1 + p2)
    out = jnp.where(lane == 0, i1 - N_GROUPS, 0.0)
    out = jnp.where(lane == 1, i2 - N_GROUPS, out)
    out = jnp.where(lane == 2, g1, out)
    out = jnp.where(lane == 3, g2, out)
    return out


def _pack_bf16_pairs(x):
    half = x.shape[1] // 2

    def rne(v):
        bits = lax.bitcast_convert_type(v, jnp.int32)
        return bits + 0x7FFF + (lax.shift_right_logical(bits, 16) & 1)

    lo = lax.shift_right_logical(rne(x[:, :half]), 16)
    hi = rne(x[:, half:]) & jnp.int32(-65536)
    return lo | hi


def _unpack_bf16_pairs(w):
    lo = lax.bitcast_convert_type(lax.shift_left(w, 16), F32)
    hi = lax.bitcast_convert_type(w & jnp.int32(-65536), F32)
    return jnp.concatenate([lo, hi], axis=1)


def _out_router_kernel(xp_ref, ap_ref, bp_ref, xs_ref, q_ref, kvn_ref, akt_hbm, avt_hbm, bkt_ref, bvt_ref,
                       cba_ref, cbb_ref, sink_ref, wo_ref, g_ref, wr_ref, br_ref,
                       x1_ref, xn_ref, route_ref, cnt_ref,
                       xcat_scr, mix_scr, kbuf, vbuf, sem, *, prompt_tiles, decode_tiles, seqs_per_step):
    i = pl.program_id(0)
    seqs = prompt_tiles * seqs_per_step

    def cache_copies(n, slot):
        return (pltpu.make_async_copy(akt_hbm.at[n], kbuf.at[slot], sem.at[0, slot]),
                pltpu.make_async_copy(avt_hbm.at[n], vbuf.at[slot], sem.at[1, slot]))

    @pl.when(i == 0)
    def _():
        cnt_ref[...] = jnp.zeros_like(cnt_ref)
        xcat_scr[...] = jnp.zeros_like(xcat_scr)
        for n0 in range(2):
            for c in cache_copies(n0, n0):
                c.start()

    @pl.when(i == prompt_tiles)
    def _():
        for n1 in (seqs, seqs + 1):
            for c in cache_copies(seqs - 1, n1 % CACHE_BUFFERS):
                c.wait()

    pslot = lax.rem(i, 2)

    def route_previous():
        logits = jnp.dot(xcat_scr[1 - pslot], wr_ref[...], preferred_element_type=F32)
        route = _route(logits + br_ref[...])
        route_ref[...] = route
        lanef = lax.broadcasted_iota(jnp.int32, route.shape, 1).astype(F32)
        hits = (lanef == route[:, 0:1]).astype(F32) + (lanef == route[:, 1:2]).astype(F32)
        cnt_ref[...] += jnp.sum(hits, axis=0, keepdims=True) * (i > 0).astype(F32)

    def project(x_ref, mix):
        x1 = x_ref[...] + jnp.dot(mix, wo_ref[...], preferred_element_type=F32)
        x1_ref[...] = x1
        ms = jnp.mean(x1 * x1, axis=-1, keepdims=True)
        xn = x1 * lax.rsqrt(ms + EPS) * g_ref[...]
        xn_ref[...] = _pack_bf16_pairs(xn)
        xh = xn.astype(BF16)
        xl = (xn - xh.astype(F32)).astype(BF16)
        xcat_scr[pslot] = jnp.concatenate([xh, xl, xh], axis=1)

    @pl.when(i < prompt_tiles)
    def _():
        route_previous()
        mix = jnp.concatenate([ap_ref[0], ap_ref[1], ap_ref[2], ap_ref[3]], axis=1).astype(BF16)
        project(xp_ref, jnp.concatenate([mix, bp_ref[...]], axis=1))
        t = q_ref.shape[1]
        for s in range(seqs_per_step):
            n = i * seqs_per_step + s
            slot = lax.rem(n, CACHE_BUFFERS)
            for c in cache_copies(n, slot):
                c.wait()
            for c in cache_copies(jnp.minimum(n + 2, seqs - 1), lax.rem(n + 2, CACHE_BUFFERS)):
                c.start()
            oa, ob = _sample_attention(q_ref[s], kvn_ref[s], kbuf[slot], vbuf[slot], bkt_ref[s], bvt_ref[s],
                                       cba_ref[...], cbb_ref[...], sink_ref[...])
            row = pl.multiple_of(n * t, t)
            mix_scr[pl.ds(row, t), :A_WIDTH] = oa
            mix_scr[pl.ds(row, t), A_WIDTH:] = ob

    @pl.when(i >= prompt_tiles)
    def _():
        route_previous()
        tm = xs_ref.shape[0]
        row = pl.multiple_of(jnp.clip(i - prompt_tiles, 0, decode_tiles - 1) * tm, tm)
        project(xs_ref, mix_scr[pl.ds(row, tm), :].astype(BF16))


def _out_router(xp, a4p, bp, xs, q3, kvn3, akt, avt, bkt, bvt, cbias_a, cbias_b, sink_rows, wo, gamma, wr, br):
    tp, tsm = xp.shape[0], xs.shape[0]
    ns, ts = q3.shape[0], q3.shape[1]
    tm = 512
    npt, nst = tp // tm, tsm // tm
    nt = npt + nst
    t = tp + tsm
    sps = ns // npt
    assert sps * npt == ns and ns * ts == tsm and ns >= CACHE_BUFFERS
    pmap = lambda i: (jnp.minimum(i, npt - 1), 0)
    pmap3 = lambda i: (jnp.minimum(i, npt - 1), 0, 0)
    smap = lambda i: (jnp.clip(i - npt, 0, nst - 1), 0)
    cur = lambda i: (jnp.minimum(i, nt - 1), 0)
    const = lambda i: (0, 0)
    return pl.pallas_call(
        functools.partial(_out_router_kernel, prompt_tiles=npt, decode_tiles=nst, seqs_per_step=sps),
        grid=(nt + 1,),
        in_specs=[
            pl.BlockSpec((tm, D_MODEL), pmap),
            pl.BlockSpec((4, tm, LANES), lambda i: (0, jnp.minimum(i, npt - 1), 0)),
            pl.BlockSpec((tm, 512), pmap),
            pl.BlockSpec((tm, D_MODEL), smap),
            pl.BlockSpec((sps, ts, 1024), pmap3),
            pl.BlockSpec((sps, ts, 1280), pmap3),
            pl.BlockSpec(memory_space=pl.ANY),
            pl.BlockSpec(memory_space=pl.ANY),
            pl.BlockSpec((sps, LANES, WIN_B), pmap3),
            pl.BlockSpec((sps, LANES, WIN_B), pmap3),
            pl.BlockSpec((H_A * ts, WIN_A + LANES), const),
            pl.BlockSpec((H_B * ts, WIN_B + LANES), const),
            pl.BlockSpec((H_B * ts, 1), const),
            pl.BlockSpec((D_MODEL, D_MODEL), const),
            pl.BlockSpec((1, D_MODEL), const),
            pl.BlockSpec((3 * D_MODEL, LANES), const),
            pl.BlockSpec((1, LANES), const),
        ],
        out_specs=[
            pl.BlockSpec((tm, D_MODEL), cur),
            pl.BlockSpec((tm, D_MODEL // 2), cur),
            pl.BlockSpec((tm, LANES), lambda i: (jnp.maximum(i - 1, 0), 0)),
            pl.BlockSpec((1, LANES), const),
        ],
        scratch_shapes=[pltpu.VMEM((2, tm, 3 * D_MODEL), BF16), pltpu.VMEM((tsm, D_MODEL), F32),
                        pltpu.VMEM((CACHE_BUFFERS, A_WIDTH, WIN_A), F32),
                        pltpu.VMEM((CACHE_BUFFERS, A_WIDTH, WIN_A), F32),
                        pltpu.SemaphoreType.DMA((2, CACHE_BUFFERS))],
        out_shape=[
            jax.ShapeDtypeStruct((t, D_MODEL), F32),
            jax.ShapeDtypeStruct((t, D_MODEL // 2), jnp.int32),
            jax.ShapeDtypeStruct((t, LANES), F32),
            jax.ShapeDtypeStruct((1, LANES), F32),
        ],
        compiler_params=_cparams(("arbitrary",), vmem=OUT_ROUTER_VMEM),
        name="out_router",
    )(xp, a4p, bp, xs, q3, kvn3, akt, avt, bkt, bvt, cbias_a, cbias_b, sink_rows, wo, gamma, wr, br)


def _sc_gather_rows(table, idx):
    b = idx.shape[0]
    d = table.shape[1]
    w = SC_WINDOW
    per_worker = b // SC_WORKERS
    nwin = per_worker // w
    assert per_worker * SC_WORKERS == b and nwin * w == per_worker
    mesh = plsc.VectorSubcoreMesh(core_axis_name="c", subcore_axis_name="s")

    @functools.partial(
        pl.kernel, mesh=mesh,
        out_type=jax.ShapeDtypeStruct((b, d), table.dtype),
        scratch_types=[pltpu.VMEM((nwin, w), jnp.int32), pltpu.VMEM((2, w, d), table.dtype),
                       pltpu.SemaphoreType.DMA((2,)), pltpu.SemaphoreType.DMA((2,))],
        name="sc_gather_rows",
    )
    def gather(table_hbm, idx_hbm, out_hbm, idx_v, rows_v, sem_in, sem_out):
        wid = lax.axis_index("s") * SC_CORES + lax.axis_index("c")
        base = wid * per_worker
        pltpu.sync_copy(idx_hbm.at[wid], idx_v)

        def fetch(j):
            return pltpu.make_async_copy(table_hbm.at[idx_v.at[j]], rows_v.at[j % 2], sem_in.at[j % 2])

        def flush(j):
            return pltpu.make_async_copy(rows_v.at[j % 2], out_hbm.at[pl.ds(base + j * w, w)],
                                         sem_out.at[j % 2])

        fetch(0).start()
        for j in range(nwin):
            fetch(j).wait()
            if j + 1 < nwin:
                if j >= 1:
                    flush(j - 1).wait()
                fetch(j + 1).start()
            flush(j).start()
        for j in range(max(nwin - 2, 0), nwin):
            flush(j).wait()

    return gather(table, idx.reshape(SC_WORKERS, nwin, w))


def _sc_scatter_rows(x, dest2, nrows):
    t, d = x.shape
    w = SC_SCATTER_WINDOW
    per_worker = t // SC_WORKERS
    nwin = per_worker // w
    assert per_worker * SC_WORKERS == t and nwin * w == per_worker
    mesh = plsc.VectorSubcoreMesh(core_axis_name="c", subcore_axis_name="s")

    @functools.partial(
        pl.kernel, mesh=mesh,
        out_type=jax.ShapeDtypeStruct((nrows, d), x.dtype),
        scratch_types=[pltpu.VMEM((TOP_K, nwin, w), jnp.int32), pltpu.VMEM((2, w, d), x.dtype),
                       pltpu.SemaphoreType.DMA((2,)), pltpu.SemaphoreType.DMA((2,))],
        name="sc_scatter_rows",
    )
    def scatter(x_hbm, dest_hbm, out_hbm, idx_v, rows_v, sem_in, sem_out):
        wid = lax.axis_index("s") * SC_CORES + lax.axis_index("c")
        base = wid * per_worker
        for k in range(TOP_K):
            pltpu.sync_copy(dest_hbm.at[k, wid], idx_v.at[k])

        def fetch(j):
            return pltpu.make_async_copy(x_hbm.at[pl.ds(base + j * w, w)], rows_v.at[j % 2], sem_in.at[j % 2])

        def spread(j, k):
            return pltpu.make_async_copy(rows_v.at[j % 2], out_hbm.at[idx_v.at[k, j]], sem_out.at[j % 2])

        fetch(0).start()
        for j in range(nwin):
            fetch(j).wait()
            if j + 1 < nwin:
                if j >= 1:
                    for k in range(TOP_K):
                        spread(j - 1, k).wait()
                fetch(j + 1).start()
            for k in range(TOP_K):
                spread(j, k).start()
        for j in range(max(nwin - 2, 0), nwin):
            for k in range(TOP_K):
                spread(j, k).wait()

    return scatter(x, dest2.reshape(TOP_K, SC_WORKERS, nwin, w))


def _expert_kernel(be_ref, nu_ref, nv_ref, nx_ref, x_ref, wg_hbm, wu_hbm, wd_hbm, o_ref,
                   wg_s, wu_s, wd_s, wg_f, wu_f, wd_f, slot_s, sem):
    i = pl.program_id(0)
    used = i < nu_ref[0]
    changed = jnp.logical_or(i == 0, be_ref[i] != be_ref[jnp.maximum(i - 1, 0)])

    def weight_copies(e, slot):
        return (pltpu.make_async_copy(wg_hbm.at[e], wg_f.at[slot], sem.at[slot, 0]),
                pltpu.make_async_copy(wu_hbm.at[e], wu_f.at[slot], sem.at[slot, 1]),
                pltpu.make_async_copy(wd_hbm.at[e], wd_f.at[slot], sem.at[slot, 2]))

    @pl.when(i == 0)
    def _():
        slot_s[0] = 0
        for c in weight_copies(be_ref[0], 0):
            c.start()

    @pl.when(jnp.logical_and(used, changed))
    def _():
        slot = slot_s[0]
        for c in weight_copies(be_ref[i], slot):
            c.wait()
        wg_s[...] = wg_f[slot].astype(BF16)
        wu_s[...] = wu_f[slot].astype(BF16)
        wd_s[...] = wd_f[slot].astype(BF16)

        @pl.when(nx_ref[i] != be_ref[i])
        def _():
            for c in weight_copies(nx_ref[i], 1 - slot):
                c.start()

        slot_s[0] = 1 - slot

    @pl.when(used)
    def _():
        row = lax.broadcasted_iota(jnp.int32, x_ref.shape, 0)
        x = _unpack_bf16_pairs(jnp.where(row < nv_ref[i], x_ref[...], 0)).astype(BF16)
        gate = jnp.dot(x, wg_s[...], preferred_element_type=F32)
        up = jnp.dot(x, wu_s[...], preferred_element_type=F32)
        h = (gate * jax.nn.sigmoid(gate) * up).astype(BF16)
        o_ref[...] = _pack_bf16_pairs(jnp.dot(h, wd_s[...], preferred_element_type=F32))

    @pl.when(jnp.logical_not(used))
    def _():
        o_ref[...] = jnp.zeros_like(o_ref)


def _experts(blk_e, n_used, nvalid, next_e, xb, w_gate, w_up, w_down):
    rows = xb.shape[0]
    nblocks = rows // MOE_ROWS
    grid_spec = pltpu.PrefetchScalarGridSpec(
        num_scalar_prefetch=4,
        grid=(nblocks,),
        in_specs=[
            pl.BlockSpec((MOE_ROWS, D_MODEL // 2), lambda i, be, nu, nv, nx: (i, 0)),
            pl.BlockSpec(memory_space=pl.ANY),
            pl.BlockSpec(memory_space=pl.ANY),
            pl.BlockSpec(memory_space=pl.ANY),
        ],
        out_specs=pl.BlockSpec((MOE_ROWS, D_MODEL // 2), lambda i, be, nu, nv, nx: (i, 0)),
        scratch_shapes=[pltpu.VMEM((D_MODEL, D_EXPERT), BF16), pltpu.VMEM((D_MODEL, D_EXPERT), BF16),
                        pltpu.VMEM((D_EXPERT, D_MODEL), BF16),
                        pltpu.VMEM((2, D_MODEL, D_EXPERT), F32), pltpu.VMEM((2, D_MODEL, D_EXPERT), F32),
                        pltpu.VMEM((2, D_EXPERT, D_MODEL), F32),
                        pltpu.SMEM((1,), jnp.int32), pltpu.SemaphoreType.DMA((2, 3))],
    )
    return pl.pallas_call(
        _expert_kernel,
        grid_spec=grid_spec,
        out_shape=jax.ShapeDtypeStruct((rows, D_MODEL // 2), jnp.int32),
        compiler_params=_cparams(("arbitrary",)),
        name="experts",
    )(blk_e, n_used, nvalid, next_e, xb, w_gate, w_up, w_down)


def _combine_kernel(x1_ref, y1_ref, y2_ref, route_ref, g_ref, outp_ref, outs_ref, *, prompt_tiles):
    r = route_ref[...]
    x = (x1_ref[...] + r[:, 2:3] * _unpack_bf16_pairs(y1_ref[...])
         + r[:, 3:4] * _unpack_bf16_pairs(y2_ref[...]))
    ms = jnp.mean(x * x, axis=-1, keepdims=True)
    y = x * lax.rsqrt(ms + EPS) * g_ref[...]
    i = pl.program_id(0)

    @pl.when(i < prompt_tiles)
    def _():
        outp_ref[...] = y

    @pl.when(i >= prompt_tiles)
    def _():
        outs_ref[...] = y


def _combine_norm(x1, ygath, route, gamma, tp):
    t = x1.shape[0]
    tm = 1024
    nt, npt = t // tm, tp // tm
    return pl.pallas_call(
        functools.partial(_combine_kernel, prompt_tiles=npt),
        grid=(nt,),
        in_specs=[
            pl.BlockSpec((tm, D_MODEL), lambda i: (i, 0)),
            pl.BlockSpec((tm, D_MODEL // 2), lambda i: (i, 0)),
            pl.BlockSpec((tm, D_MODEL // 2), lambda i: (i + nt, 0)),
            pl.BlockSpec((tm, LANES), lambda i: (i, 0)),
            pl.BlockSpec((1, D_MODEL), lambda i: (0, 0)),
        ],
        out_specs=[
            pl.BlockSpec((tm, D_MODEL), lambda i: (jnp.minimum(i, npt - 1), 0)),
            pl.BlockSpec((tm, D_MODEL), lambda i: (jnp.maximum(i - npt, 0), 0)),
        ],
        out_shape=[jax.ShapeDtypeStruct((tp, D_MODEL), F32), jax.ShapeDtypeStruct((t - tp, D_MODEL), F32)],
        compiler_params=_cparams(("arbitrary",)),
        name="combine_norm",
    )(x1, ygath, ygath, route, gamma)


def _band_index():
    c = (2 * QB - np.arange(2 * QB)) % (2 * QB)
    return c, c <= QB


def _bias_a_prompt(table_a):
    c, valid = _band_index()
    idx = np.stack([_t5_bucket_np(d * np.clip(QB - c, 0, QB)) for d in DILATIONS])
    return jnp.where(valid, jnp.transpose(table_a[idx], (0, 2, 1)) * LOG2E, NEG)


def _bias_b_prompt(table_b):
    c, valid = _band_index()
    valid = valid & (c >= 1)
    h = jnp.where(valid, table_b[_t5_bucket_np(np.clip(QB - c, 0, QB))].T * LOG2E, NEG)
    return jnp.transpose(h.reshape(KV_B, G_B, 2 * QB), (1, 0, 2)).reshape(H_B, 2 * QB)


def _sample_bias(table, span, t, log2_weight):
    cols = span + LANES
    period = cols + LANES
    x = np.arange(period)
    dist = np.where(x >= period - t, span - x + period, span - x)
    extra = log2_weight(dist)
    valid = np.isfinite(extra)
    u = jnp.where(valid, table[_t5_bucket_np(np.maximum(dist, 0))].T * LOG2E
                  + np.where(valid, extra, 0.0).astype(np.float32), NEG)
    rows = jnp.tile(u, (1, t))[:, :t * (period - 1)].reshape(u.shape[0], t, period - 1)[:, :, :cols]
    return rows.reshape(u.shape[0] * t, cols)


def _bias_a_sample(table_a, t):
    def log2_count(dist):
        count = np.zeros(dist.shape, np.int64)
        for w, d in zip(WINDOWS, DILATIONS):
            count += (dist >= 0) & (dist % d == 0) & (dist <= w)
        return np.where(count > 0, np.log2(np.maximum(count, 1)), -np.inf)

    return _sample_bias(table_a, WIN_A, t, log2_count)


def _bias_b_sample(table_b, t):
    return _sample_bias(table_b, WIN_B, t,
                        lambda dist: np.where((dist >= 0) & (dist < WIN_B), 0.0, -np.inf))


def _dest_kernel(route_ref, cnt_ref, tri_ref, dest_ref, meta_ref, run_scr, pst_scr):
    i = pl.program_id(0)
    tm = route_ref.shape[0]
    r = route_ref[...]
    lane = lax.broadcasted_iota(jnp.int32, (tm, LANES), 1)
    lanef = lane.astype(F32)
    oh0 = lanef == r[:, 0:1]
    oh1 = lanef == r[:, 1:2]
    ohf = jnp.concatenate([oh0, oh1], axis=0).astype(F32)

    @pl.when(i == 0)
    def _():
        cnt = jnp.broadcast_to(cnt_ref[...], (LANES, LANES))
        padded = jnp.floor((cnt + (MOE_ROWS - 1)) * (1.0 / MOE_ROWS)) * MOE_ROWS
        lane_e = lax.broadcasted_iota(jnp.int32, (LANES, LANES), 1)
        x = padded
        for sh in (1, 2, 4, 8, 16, 32, 64):
            x = x + jnp.where(lane_e >= sh, pltpu.roll(x, sh, 1), 0.0)
        pst_scr[...] = (x - padded)[0:1]
        run_scr[...] = jnp.zeros_like(run_scr)
        wide = lambda v: jnp.concatenate([v.T, v.T], axis=1)
        cnt_t, bend_t = wide(cnt), wide(x * (1.0 / MOE_ROWS))
        bstart_t = wide((x - padded) * (1.0 / MOE_ROWS))
        blk = lax.broadcasted_iota(jnp.int32, (LANES, 2 * LANES), 1).astype(F32)
        exp = lax.broadcasted_iota(jnp.int32, (LANES, 2 * LANES), 0)
        real = exp < N_EXPERTS
        blk_e = jnp.minimum(jnp.sum(jnp.where(real & (bend_t <= blk), 1.0, 0.0), axis=0, keepdims=True),
                            N_EXPERTS - 1.0)
        mine = exp.astype(F32) == blk_e
        within = blk[0:1] - jnp.sum(jnp.where(mine, bstart_t, 0.0), axis=0, keepdims=True)
        nvalid = jnp.clip(jnp.sum(jnp.where(mine, cnt_t, 0.0), axis=0, keepdims=True) - within * MOE_ROWS,
                          0.0, float(MOE_ROWS))
        n_used = jnp.max(jnp.where(real, bend_t, 0.0), axis=0, keepdims=True)
        later = real & (exp.astype(F32) > blk_e) & (cnt_t > 0.0)
        nxt = jnp.min(jnp.where(later, exp.astype(F32), float(LANES)), axis=0, keepdims=True)
        nxt = jnp.where(nxt >= N_EXPERTS, blk_e, nxt)
        meta_ref[...] = jnp.concatenate([blk_e, nvalid, n_used, nxt, jnp.zeros((4, 2 * LANES), F32)],
                                        axis=0).astype(jnp.int32)

    base = run_scr[...] + pst_scr[...] - 1.0
    vals = []
    for c in range(2 * tm // LANES):
        ohc = ohf[c * LANES:(c + 1) * LANES]
        vals.append(jnp.dot(tri_ref[...], ohc.astype(BF16), preferred_element_type=F32) + base)
        base = base + jnp.sum(ohc, axis=0, keepdims=True)
    val = jnp.concatenate(vals, axis=0)
    d0 = jnp.sum(jnp.where(oh0, val[:tm], 0.0), axis=-1, keepdims=True)
    d1 = jnp.sum(jnp.where(oh1, val[tm:], 0.0), axis=-1, keepdims=True)
    tile = jnp.where(lane == 0, d0, jnp.where(lane == 1, d1, 0.0))
    dest_ref[...] = tile.T[:8].astype(jnp.int32)
    run_scr[...] += jnp.sum(ohf, axis=0, keepdims=True)


def _dispatch(route, cnt):
    t = route.shape[0]
    tm = 1024
    tri = (jnp.arange(LANES)[:, None] >= jnp.arange(LANES)[None, :]).astype(BF16)
    nblocks = -(-t * TOP_K // MOE_ROWS) + N_EXPERTS
    assert nblocks <= 2 * LANES
    dest, meta = pl.pallas_call(
        _dest_kernel,
        grid=(t // tm,),
        in_specs=[pl.BlockSpec((tm, LANES), lambda i: (i, 0)),
                  pl.BlockSpec((1, LANES), lambda i: (0, 0)),
                  pl.BlockSpec((LANES, LANES), lambda i: (0, 0))],
        out_specs=[pl.BlockSpec((8, tm), lambda i: (0, i)),
                   pl.BlockSpec((8, 2 * LANES), lambda i: (0, 0))],
        out_shape=[jax.ShapeDtypeStruct((8, t), jnp.int32), jax.ShapeDtypeStruct((8, 2 * LANES), jnp.int32)],
        scratch_shapes=[pltpu.VMEM((1, LANES), F32), pltpu.VMEM((1, LANES), F32)],
        compiler_params=_cparams(("arbitrary",)),
        name="moe_dest",
    )(route, cnt, tri)
    return dest[:TOP_K], meta[0, :nblocks], meta[2, :1], meta[1, :nblocks], meta[3, :nblocks]


def kernel(x_prompt, x_sample, cache_a_k, cache_a_v, cache_b_k, cache_b_v, rel_bias_table, attn_norm, w_in,
           w_out, attn_sinks, ffn_norm, w_router_group, b_router_group, w_router_expert, b_router_expert,
           w_gate, w_up, w_down, final_norm):
    s = x_prompt.shape[1]
    ns, ts = x_sample.shape[0], x_sample.shape[1]
    table_a = rel_bias_table[:, :H_A]
    table_b = rel_bias_table[:, H_A:]

    w = w_in[0]
    wqa, wka, wva, wqb, wkb, wvb = (w[:, 0:512], w[:, 512:1024], w[:, 1024:1536], w[:, 1536:2048],
                                    w[:, 2048:2176], w[:, 2176:2304])
    wqb = jnp.transpose(wqb.reshape(D_MODEL, KV_B, G_B, HEAD_DIM), (0, 2, 1, 3)).reshape(D_MODEL, 512)
    wp = jnp.concatenate([wka, wva, wqa, wqb, wkb, wvb], axis=1).astype(BF16)
    cscale = jnp.concatenate([jnp.ones((1, 1024), F32), jnp.full((1, 1024), SCALE * LOG2E, F32),
                              jnp.ones((1, 256), F32)], axis=1)
    wo = w_out[0]
    wo_b = jnp.transpose(wo[512:].reshape(KV_B, G_B, HEAD_DIM, D_MODEL), (1, 0, 2, 3)).reshape(512, D_MODEL)
    wo_p = jnp.concatenate([wo[:512], wo_b], axis=0).astype(BF16)
    wr = jnp.concatenate([w_router_group[0],
                          jnp.transpose(w_router_expert[0], (1, 0, 2)).reshape(D_MODEL, N_EXPERTS),
                          jnp.zeros((D_MODEL, LANES - N_GROUPS - N_EXPERTS), F32)], axis=1)
    wr_hi = wr.astype(BF16)
    wr = jnp.concatenate([wr_hi, wr_hi, (wr - wr_hi.astype(F32)).astype(BF16)], axis=0)
    br = jnp.concatenate([b_router_group[0], b_router_expert[0].reshape(N_EXPERTS),
                          jnp.zeros((LANES - N_GROUPS - N_EXPERTS,), F32)]).reshape(1, LANES)
    sinks2 = attn_sinks[0] * LOG2E
    sinks_gk = jnp.transpose(sinks2.reshape(KV_B, G_B), (1, 0)).reshape(H_B)
    sink_rows_p = jnp.repeat(sinks_gk, QB).reshape(G_B, 1, 2 * QB)
    sink_rows_s = jnp.repeat(sinks2, ts).reshape(H_B * ts, 1)
    emat = jnp.tile(jnp.arange(LANES)[:, None] == (jnp.arange(A_WIDTH)[None, :] // HEAD_DIM),
                    (3, 1)).astype(BF16)
    attn_g = attn_norm[0].reshape(1, D_MODEL)
    ffn_g = ffn_norm[0].reshape(1, D_MODEL)

    xp = x_prompt.reshape(s, D_MODEL)
    aperm, qb_p, kvb_p, akv32, bkv32 = _proj_prompt(xp, attn_g, wp, cscale)
    a4 = _attn_a_prompt(aperm, _bias_a_prompt(table_a), emat)
    ob_p = _attn_b_prompt(qb_p, kvb_p, _bias_b_prompt(table_b), sink_rows_p)

    xs = x_sample.reshape(ns * ts, D_MODEL)
    q_s, kv_s = _proj_sample(xs, attn_g, wp, cscale)
    akt = jnp.transpose(cache_a_k[0], (0, 2, 3, 1)).reshape(ns, A_WIDTH, WIN_A)
    avt = jnp.transpose(cache_a_v[0], (0, 2, 3, 1)).reshape(ns, A_WIDTH, WIN_A)
    bkt = jnp.transpose(cache_b_k[0], (0, 2, 3, 1)).reshape(ns, LANES, WIN_B)
    bvt = jnp.transpose(cache_b_v[0], (0, 2, 3, 1)).reshape(ns, LANES, WIN_B)

    x1, xn, route, cnt = _out_router(xp, a4, ob_p, xs, q_s.reshape(ns, ts, 1024), kv_s.reshape(ns, ts, 1280),
                                     akt, avt, bkt, bvt, _bias_a_sample(table_a, ts),
                                     _bias_b_sample(table_b, ts), sink_rows_s, wo_p, ffn_g, wr, br)
    dest2, blk_e, n_used, nvalid, next_e = _dispatch(route, cnt)
    xb = _sc_scatter_rows(xn, dest2, blk_e.shape[0] * MOE_ROWS)
    yb = _experts(blk_e, n_used, nvalid, next_e, xb, w_gate[0], w_up[0], w_down[0])
    y_p, y_s = _combine_norm(x1, _sc_gather_rows(yb, dest2.reshape(-1)), route, final_norm.reshape(1, D_MODEL), s)

    y_prompt = y_p.reshape(1, s, D_MODEL)
    y_sample = y_s.reshape(ns, ts, D_MODEL)
    keep_a, keep_b = min(WIN_A, s), min(WIN_B, s)
    pak = akv32[SPAN - keep_a:, :512].reshape(1, 1, keep_a, H_A, HEAD_DIM)
    pav = akv32[SPAN - keep_a:, 512:].reshape(1, 1, keep_a, H_A, HEAD_DIM)
    pbk = bkv32[SPAN - keep_b:, :128].reshape(1, 1, keep_b, KV_B, HEAD_DIM)
    pbv = bkv32[SPAN - keep_b:, 128:].reshape(1, 1, keep_b, KV_B, HEAD_DIM)
    sak = kv_s[:, 0:512].reshape(1, ns, ts, H_A, HEAD_DIM)
    sav = kv_s[:, 512:1024].reshape(1, ns, ts, H_A, HEAD_DIM)
    sbk = kv_s[:, 1024:1152].reshape(1, ns, ts, KV_B, HEAD_DIM)
    sbv = kv_s[:, 1152:1280].reshape(1, ns, ts, KV_B, HEAD_DIM)
    return (y_prompt, y_sample, pak, pav, pbk, pbv, sak, sav, sbk, sbv)
```
